```python
import jax, jax.numpy as jnp
from jax import lax
import numpy as np

D_MODEL = 4096
BATCH = 8
SEQ = 4096
DEPTH = 1

EXPAND = 2
D_MIX = EXPAND * D_MODEL
D_ATTN = D_MIX // 2
D_RNN = D_MIX - D_ATTN
ATTN_HEAD_DIM = 64
N_Q_HEADS = D_ATTN // ATTN_HEAD_DIM
N_KV_HEADS = N_Q_HEADS // 8
GQA_GROUP = N_Q_HEADS // N_KV_HEADS
D_KV = N_KV_HEADS * ATTN_HEAD_DIM
WINDOW = 128
RNN_HEAD_DIM = 128
N_RNN_HEADS = D_RNN // RNN_HEAD_DIM
CHUNK = 64
NORM_EPS = 1e-6

COL_SIZES = [D_ATTN, D_KV, D_KV, D_ATTN, D_RNN, D_RNN, D_RNN, D_RNN]
D_IN = int(sum(COL_SIZES))
SPLIT_POINTS = [int(c) for c in np.cumsum(COL_SIZES)[:-1]]

kernel_name = "hymba_swa_sink_hgrn2_sandwich"


def rms_norm(x, gain):
    xf = x.astype(jnp.float32)
    y = xf * lax.rsqrt(jnp.mean(xf * xf, axis=-1, keepdims=True) + NORM_EPS)
    return (y * gain.astype(jnp.float32)).astype(x.dtype)


def sliding_window_attention(q, k, v, sinks):
    B, S = q.shape[0], q.shape[1]
    nb = S // WINDOW
    q = q.reshape(B, nb, WINDOW, N_KV_HEADS, GQA_GROUP, ATTN_HEAD_DIM)
    k = k.reshape(B, nb, WINDOW, N_KV_HEADS, ATTN_HEAD_DIM)
    v = v.reshape(B, nb, WINDOW, N_KV_HEADS, ATTN_HEAD_DIM)
    pad = ((0, 0), (1, 0), (0, 0), (0, 0), (0, 0))
    kk = jnp.concatenate([jnp.pad(k, pad)[:, :-1], k], axis=2)
    vv = jnp.concatenate([jnp.pad(v, pad)[:, :-1], v], axis=2)
    scale = ATTN_HEAD_DIM ** -0.5
    scores = jnp.einsum('bnqhgd,bnkhd->bnhgqk', q, kk).astype(jnp.float32) * scale
    qi = jnp.arange(WINDOW)[:, None]
    kj = jnp.arange(2 * WINDOW)[None, :]
    band = (kj > qi) & (kj <= qi + WINDOW)
    blk = jnp.arange(nb)[:, None, None]
    valid = band[None] & ((blk > 0) | (kj[None] >= WINDOW))
    scores = jnp.where(valid[None, :, None, None], scores, -jnp.inf)
    sink = sinks.astype(jnp.float32).reshape(N_KV_HEADS, GQA_GROUP)[None, None, :, :, None, None]
    m = jnp.maximum(jnp.max(scores, axis=-1, keepdims=True), sink)
    p = jnp.exp(scores - m)
    denom = jnp.sum(p, axis=-1, keepdims=True) + jnp.exp(sink - m)
    probs = (p / denom).astype(v.dtype)
    out = jnp.einsum('bnhgqk,bnkhd->bnqhgd', probs, vv)
    return out.reshape(B, S, D_ATTN)


def hgrn2_recurrence(q, k, v, g):
    B, S, H, dk = q.shape
    dv = v.shape[-1]
    nc = S // CHUNK

    def to_chunks(t):
        return t.astype(jnp.float32).reshape(B, nc, CHUNK, H, t.shape[-1]).transpose(1, 0, 3, 2, 4)

    qc, kc, vc, gc = to_chunks(q), to_chunks(k), to_chunks(v), to_chunks(g)
    causal = jnp.tril(jnp.ones((CHUNK, CHUNK), dtype=bool))

    def step(state, inp):
        qb, kb, vb, gb = inp
        G = jnp.cumsum(gb, axis=2)
        inter = jnp.einsum('bhtd,bhde->bhte', qb * jnp.exp(G), state)
        diff = G[:, :, :, None, :] - G[:, :, None, :, :]
        decay = jnp.exp(jnp.where(causal[:, :, None], diff, -jnp.inf))
        attn = jnp.einsum('bhtd,bhsd,bhtsd->bhts', qb, kb, decay)
        intra = jnp.einsum('bhts,bhse->bhte', attn, vb)
        G_last = G[:, :, -1:, :]
        new_state = (jnp.exp(G_last[:, :, 0, :])[..., None] * state
                     + jnp.einsum('bhsd,bhse->bhde', kb * jnp.exp(G_last - G), vb))
        return new_state, inter + intra

    init = jnp.zeros((B, H, dk, dv), jnp.float32)
    _, out = lax.scan(step, init, (qc, kc, vc, gc))
    return out.transpose(1, 0, 3, 2, 4).reshape(B, S, H, dv)


def _fwd_setup_inputs(seed: int = 0) -> dict:
    key = jax.random.key(seed)
    ks = jax.random.split(key, 8)
    x = jax.random.normal(ks[0], (BATCH, SEQ, D_MODEL), jnp.float32)
    w_in = jax.random.normal(ks[1], (DEPTH, D_MODEL, D_IN), jnp.float32) * D_MODEL ** -0.5
    attn_sinks = jax.random.normal(ks[2], (DEPTH, N_Q_HEADS), jnp.float32)
    lb_logits = 0.5 * jax.random.normal(ks[3], (DEPTH + 1, D_RNN), jnp.float32)
    rnn_norm = 1.0 + 0.1 * jax.random.normal(ks[4], (DEPTH, D_RNN), jnp.float32)
    w_out = jax.random.normal(ks[5], (DEPTH, D_MIX, D_MODEL), jnp.float32) * D_MIX ** -0.5
    pre_norm = 1.0 + 0.1 * jax.random.normal(ks[6], (DEPTH, D_MODEL), jnp.float32)
    post_norm = 1.0 + 0.1 * jax.random.normal(ks[7], (DEPTH, D_MODEL), jnp.float32)
    return {"x": x, "w_in": w_in, "attn_sinks": attn_sinks, "lb_logits": lb_logits,
            "rnn_norm": rnn_norm, "w_out": w_out, "pre_norm": pre_norm, "post_norm": post_norm}


def _fwd_reference(x, w_in, attn_sinks, lb_logits, rnn_norm, w_out, pre_norm, post_norm):
    B, S, _ = x.shape
    lb_table = jnp.cumsum(jax.nn.softmax(lb_logits.astype(jnp.float32), axis=0), axis=0)
    for layer in range(DEPTH):
        h = rms_norm(x, pre_norm[layer])
        proj = jnp.einsum('bsd,de->bse', h, w_in[layer])
        aq, ak, av, ag, rq, rf, ri, rg = jnp.split(proj, SPLIT_POINTS, axis=-1)

        attn = sliding_window_attention(
            aq.reshape(B, S, N_Q_HEADS, ATTN_HEAD_DIM),
            ak.reshape(B, S, N_KV_HEADS, ATTN_HEAD_DIM),
            av.reshape(B, S, N_KV_HEADS, ATTN_HEAD_DIM),
            attn_sinks[layer])
        attn = attn * jax.nn.silu(ag)

        lb = lb_table[layer]
        f = lb + (1.0 - lb) * jax.nn.sigmoid(rf.astype(jnp.float32))
        g = jnp.log(f)
        k = 1.0 - f
        q = jax.nn.silu(rq)
        shp = (B, S, N_RNN_HEADS, RNN_HEAD_DIM)
        o = hgrn2_recurrence(q.reshape(shp), k.reshape(shp), ri.reshape(shp), g.reshape(shp))
        o = rms_norm(o, rnn_norm[layer].reshape(N_RNN_HEADS, RNN_HEAD_DIM))
        o = o.reshape(B, S, D_RNN).astype(x.dtype) * jax.nn.silu(rg)

        mixed = jnp.concatenate([attn, o], axis=-1)
        y = jnp.einsum('bse,ed->bsd', mixed, w_out[layer])
        x = x + rms_norm(y, post_norm[layer])
    return x


import jax as _jax
import jax.numpy as _jnp

TWIN_FORMAT = 'train_step'
FWD_PARAMS = ['x', 'w_in', 'attn_sinks', 'lb_logits', 'rnn_norm', 'w_out', 'pre_norm', 'post_norm']
TWIN_WEIGHTS = ['w_in', 'attn_sinks', 'lb_logits', 'rnn_norm', 'w_out', 'pre_norm', 'post_norm']
TWIN_DIFF_INPUT = 'x'
TWIN_INPUTS = ['x', 'w_in', 'attn_sinks', 'lb_logits', 'rnn_norm', 'w_out', 'pre_norm', 'post_norm', 'loss_target', 'm_w_in', 'm_attn_sinks', 'm_lb_logits', 'm_rnn_norm', 'm_w_out', 'm_pre_norm', 'm_post_norm', 'v_w_in', 'v_attn_sinks', 'v_lb_logits', 'v_rnn_norm', 'v_w_out', 'v_pre_norm', 'v_post_norm']
TWIN_OUTPUTS = ['loss', 'grad_x', 'grad_w_in', 'grad_attn_sinks', 'grad_lb_logits', 'grad_rnn_norm', 'grad_w_out', 'grad_pre_norm', 'grad_post_norm', 'delta_w_in', 'delta_attn_sinks', 'delta_lb_logits', 'delta_rnn_norm', 'delta_w_out', 'delta_pre_norm', 'delta_post_norm', 'new_m_w_in', 'new_m_attn_sinks', 'new_m_lb_logits', 'new_m_rnn_norm', 'new_m_w_out', 'new_m_pre_norm', 'new_m_post_norm', 'new_v_w_in', 'new_v_attn_sinks', 'new_v_lb_logits', 'new_v_rnn_norm', 'new_v_w_out', 'new_v_pre_norm', 'new_v_post_norm']
TWIN_LEAF_KINDS = {'loss': 'loss', 'grad_x': 'grad_x', 'grad_w_in': 'grad_w', 'grad_attn_sinks': 'grad_w', 'grad_lb_logits': 'grad_w', 'grad_rnn_norm': 'grad_w', 'grad_w_out': 'grad_w', 'grad_pre_norm': 'grad_w', 'grad_post_norm': 'grad_w', 'delta_w_in': 'delta_w', 'delta_attn_sinks': 'delta_w', 'delta_lb_logits': 'delta_w', 'delta_rnn_norm': 'delta_w', 'delta_w_out': 'delta_w', 'delta_pre_norm': 'delta_w', 'delta_post_norm': 'delta_w', 'new_m_w_in': 'new_m', 'new_m_attn_sinks': 'new_m', 'new_m_lb_logits': 'new_m', 'new_m_rnn_norm': 'new_m', 'new_m_w_out': 'new_m', 'new_m_pre_norm': 'new_m', 'new_m_post_norm': 'new_m', 'new_v_w_in': 'new_v', 'new_v_attn_sinks': 'new_v', 'new_v_lb_logits': 'new_v', 'new_v_rnn_norm': 'new_v', 'new_v_w_out': 'new_v', 'new_v_pre_norm': 'new_v', 'new_v_post_norm': 'new_v'}


def _forward(args):
    return _fwd_reference(*[args[k] for k in FWD_PARAMS])


def _output_shape():
    out = _jax.eval_shape(lambda: _forward(_fwd_setup_inputs(0)))
    return out.shape, out.dtype

N_MICROBATCH = 1
ADAM_LR = 0.001
ADAM_B1 = 0.9
ADAM_B2 = 0.999
ADAM_EPS = 1e-08
ADAM_WD = 0.01
ADAM_STEP = 10
PER_EXAMPLE_BATCH_AXIS = {'x': 0, 'loss_target': 0}
SHARED_INPUTS = []
_WEIGHT_DTYPES = {'w_in': _jnp.float32, 'attn_sinks': _jnp.float32, 'lb_logits': _jnp.float32, 'rnn_norm': _jnp.float32, 'w_out': _jnp.float32, 'pre_norm': _jnp.float32, 'post_norm': _jnp.float32}
MOMENT_SCALE = {'w_in': 3.755696e-02, 'attn_sinks': 9.833246e-03, 'lb_logits': 5.402456e-03, 'rnn_norm': 6.800135e-02, 'w_out': 6.649222e-02, 'pre_norm': 9.405106e-02, 'post_norm': 8.039095e+00}


def _to_microbatches(a, axis):
    t = _jnp.moveaxis(a, axis, 0)
    t = t.reshape((N_MICROBATCH, t.shape[0] // N_MICROBATCH) + t.shape[1:])
    return _jnp.moveaxis(t, 1, axis + 1)


def setup_inputs(seed: int = 0) -> dict:
    inp = _fwd_setup_inputs(seed)
    key = _jax.random.fold_in(_jax.random.key(seed), 7919)
    shape, _ = _output_shape()
    out = dict(inp)
    out["loss_target"] = _jax.random.normal(_jax.random.fold_in(key, 0), shape, _jnp.float32)
    for i, name in enumerate(TWIN_WEIGHTS):
        w = inp[name].astype(_jnp.float32)
        if MOMENT_SCALE is None:
            s = _jnp.sqrt(_jnp.mean(_jnp.square(w)) + 1e-30)
        else:
            s = MOMENT_SCALE[name]
        km, kv = _jax.random.split(_jax.random.fold_in(key, i + 1))
        out[name] = w
        out["m_" + name] = s * _jax.random.normal(km, w.shape, _jnp.float32)
        out["v_" + name] = (s * s) * _jax.random.uniform(kv, w.shape, _jnp.float32, 0.5, 1.5)
    if N_MICROBATCH > 1:
        for name, axis in PER_EXAMPLE_BATCH_AXIS.items():
            out[name] = _to_microbatches(out[name], axis)
    return {'x': out['x'], 'w_in': out['w_in'], 'attn_sinks': out['attn_sinks'], 'lb_logits': out['lb_logits'], 'rnn_norm': out['rnn_norm'], 'w_out': out['w_out'], 'pre_norm': out['pre_norm'], 'post_norm': out['post_norm'], 'loss_target': out['loss_target'], 'm_w_in': out['m_w_in'], 'm_attn_sinks': out['m_attn_sinks'], 'm_lb_logits': out['m_lb_logits'], 'm_rnn_norm': out['m_rnn_norm'], 'm_w_out': out['m_w_out'], 'm_pre_norm': out['m_pre_norm'], 'm_post_norm': out['m_post_norm'], 'v_w_in': out['v_w_in'], 'v_attn_sinks': out['v_attn_sinks'], 'v_lb_logits': out['v_lb_logits'], 'v_rnn_norm': out['v_rnn_norm'], 'v_w_out': out['v_w_out'], 'v_pre_norm': out['v_pre_norm'], 'v_post_norm': out['v_post_norm']}


def _loss(weights, diff, rest, loss_target):
    with _jax.named_scope("forward"):
        args = {**rest, TWIN_DIFF_INPUT: diff, **{k: w.astype(_WEIGHT_DTYPES[k]) for k, w in weights.items()}}
        y = _forward(args)
    with _jax.named_scope("loss_head"):
        err = _jnp.square(y.astype(_jnp.float32) - loss_target)
        return 0.5 * _jnp.sum(_jnp.mean(err, axis=-1)) if err.ndim else 0.5 * err


def _adamw(w, g, m, v):
    m = ADAM_B1 * m + (1.0 - ADAM_B1) * g
    v = ADAM_B2 * v + (1.0 - ADAM_B2) * _jnp.square(g)
    m_hat = m / (1.0 - ADAM_B1 ** ADAM_STEP)
    v_hat = v / (1.0 - ADAM_B2 ** ADAM_STEP)
    delta = -ADAM_LR * (m_hat / (_jnp.sqrt(v_hat) + ADAM_EPS) + ADAM_WD * w)
    return delta, m, v


def reference(x, w_in, attn_sinks, lb_logits, rnn_norm, w_out, pre_norm, post_norm, loss_target, m_w_in, m_attn_sinks, m_lb_logits, m_rnn_norm, m_w_out, m_pre_norm, m_post_norm, v_w_in, v_attn_sinks, v_lb_logits, v_rnn_norm, v_w_out, v_pre_norm, v_post_norm):
    given = dict(x=x, w_in=w_in, attn_sinks=attn_sinks, lb_logits=lb_logits, rnn_norm=rnn_norm, w_out=w_out, pre_norm=pre_norm, post_norm=post_norm, loss_target=loss_target, m_w_in=m_w_in, m_attn_sinks=m_attn_sinks, m_lb_logits=m_lb_logits, m_rnn_norm=m_rnn_norm, m_w_out=m_w_out, m_pre_norm=m_pre_norm, m_post_norm=m_post_norm, v_w_in=v_w_in, v_attn_sinks=v_attn_sinks, v_lb_logits=v_lb_logits, v_rnn_norm=v_rnn_norm, v_w_out=v_w_out, v_pre_norm=v_pre_norm, v_post_norm=v_post_norm)
    weights = {n: given[n] for n in TWIN_WEIGHTS}
    shared = {n: given[n] for n in SHARED_INPUTS}
    per_example = {n: given[n] for n in ['x']}
    grad_fn = _jax.value_and_grad(_loss, argnums=(0, 1))

    def one_microbatch(ex, loss_target):
        ex = dict(ex)
        diff = ex.pop(TWIN_DIFF_INPUT)
        return grad_fn(weights, diff, {**shared, **ex}, loss_target)

    if N_MICROBATCH == 1:
        loss, (grad_w, grad_x) = one_microbatch(per_example, given["loss_target"])
    else:
        def body(carry, xs):
            loss_sum, grad_sum = carry
            l_k, (gw_k, gx_k) = one_microbatch(xs[0], xs[1])
            with _jax.named_scope("update"):
                return (loss_sum + l_k, _jax.tree.map(_jnp.add, grad_sum, gw_k)), gx_k

        init = (_jnp.zeros((), _jnp.float32), _jax.tree.map(_jnp.zeros_like, weights))
        (loss, grad_w), grad_x = _jax.lax.scan(body, init, (per_example, given["loss_target"]))
    with _jax.named_scope("update"):
        delta_w, new_m, new_v = {}, {}, {}
        for n in TWIN_WEIGHTS:
            delta_w[n], new_m[n], new_v[n] = _adamw(weights[n], grad_w[n], given["m_" + n], given["v_" + n])
    return (loss, grad_x, *[grad_w[n] for n in TWIN_WEIGHTS], *[delta_w[n] for n in TWIN_WEIGHTS],
            *[new_m[n] for n in TWIN_WEIGHTS], *[new_v[n] for n in TWIN_WEIGHTS])
```

```python
import numpy as np
import jax
import jax.numpy as jnp
from jax import lax
from jax.experimental import pallas as pl
from jax.experimental.pallas import tpu as pltpu

F32 = jnp.float32
BF16 = jnp.bfloat16
MESH = pl.DeviceIdType.MESH

NORM_EPS = 1e-6
ATTN_HEAD_DIM = 64
GQA_GROUP = 8
WINDOW = 128
RNN_HEAD_DIM = 128
CHUNK = 64
HALF_CHUNK = CHUNK // 2
ATTN_SCALE = ATTN_HEAD_DIM ** -0.5

ADAM_LR = 0.001
ADAM_B1 = 0.9
ADAM_B2 = 0.999
ADAM_EPS = 1e-08
ADAM_WD = 0.01
ADAM_STEP = 10

LANES = 128
COL_TILE = 512
RNN_GROUP_HEADS = 4
ATTN_GROUP_LANES = 1024
N_CHIPS = 4
VMEM_LIMIT_BYTES = 56 * 1024 * 1024
NEG_BIG = -1e30


def _params(**kw):
    return pltpu.CompilerParams(vmem_limit_bytes=VMEM_LIMIT_BYTES, **kw)


def _sigmoid(x):
    return 1.0 / (1.0 + jnp.exp(-x))


def _dot(a, b):
    return jnp.dot(a, b, preferred_element_type=F32)


def _dot_nt(a, b):
    return lax.dot_general(a, b, (((1,), (1,)), ((), ())), preferred_element_type=F32)


def _dot_tn(a, b):
    return lax.dot_general(a, b, (((0,), (0,)), ((), ())), preferred_element_type=F32)


def _split3(x):
    hi = x.astype(BF16)
    r1 = x - hi.astype(F32)
    mid = r1.astype(BF16)
    lo = (r1 - mid.astype(F32)).astype(BF16)
    return hi, mid, lo


def _tri_dot(tri_bf16, x):
    hi, mid, lo = _split3(x)
    return _dot(tri_bf16, hi) + _dot(tri_bf16, mid) + _dot(tri_bf16, lo)


def _layout(d_model):
    d = d_model
    dkv = d // GQA_GROUP
    orig = dict(aq=0, ak=d, av=d + dkv, ag=d + 2 * dkv)
    base = d + 2 * dkv + d
    orig.update(rq=base, rf=base + d, ri=base + 2 * d, rg=base + 3 * d)
    group_w = RNN_GROUP_HEADS * RNN_HEAD_DIM
    cols = []
    for hg in range(d // group_w):
        for seg in ("rq", "rf", "ri", "rg"):
            cols.append((orig[seg] + hg * group_w, group_w))
    for m in range(d // ATTN_GROUP_LANES):
        for seg in ("aq", "ag"):
            cols.append((orig[seg] + m * ATTN_GROUP_LANES, ATTN_GROUP_LANES))
    cols.append((orig["ak"], dkv))
    cols.append((orig["av"], dkv))
    units = []
    for start, width in cols:
        assert start % LANES == 0 and width % LANES == 0
        units += [start + u for u in range(0, width, LANES)]
    per = COL_TILE // LANES
    assert len(units) % per == 0
    tiles = []
    for t in range(len(units) // per):
        run = units[t * per:(t + 1) * per]
        assert run[0] % COL_TILE == 0 and all(run[i] == run[0] + i * LANES for i in range(per))
        tiles.append(run[0] // COL_TILE)
    return dict(a_off=4 * d, k_off=6 * d, v_off=6 * d + dkv, total=6 * d + 2 * dkv,
                perm=np.asarray(tiles, np.int32))


def _cast_bf16(a, name):
    rows, cols = a.shape
    tr = min(rows, 512)

    def body(a_ref, o_ref):
        o_ref[...] = a_ref[...].astype(BF16)

    return pl.pallas_call(
        body, name=name, grid=(rows // tr,),
        in_specs=[pl.BlockSpec((tr, cols), lambda i: (i, 0))],
        out_specs=pl.BlockSpec((tr, cols), lambda i: (i, 0)),
        out_shape=jax.ShapeDtypeStruct((rows, cols), BF16),
        compiler_params=_params(dimension_semantics=("parallel",)),
    )(a)


def _transpose_bf16(a, name):
    rows, cols = a.shape
    tr = min(rows, 256)
    tc = min(cols, 2048)

    def body(a_ref, o_ref):
        o_ref[...] = a_ref[...].astype(F32).T.astype(BF16)

    return pl.pallas_call(
        body, name=name, grid=(rows // tr, cols // tc),
        in_specs=[pl.BlockSpec((tr, tc), lambda i, j: (i, j))],
        out_specs=pl.BlockSpec((tc, tr), lambda i, j: (j, i)),
        out_shape=jax.ShapeDtypeStruct((cols, rows), BF16),
        compiler_params=_params(dimension_semantics=("parallel", "parallel")),
    )(a)


def _prenorm_fwd(x, gain):
    t, d = x.shape
    tm = min(t, 256)

    def body(x_ref, g_ref, h_ref, ht_ref):
        xv = x_ref[...]
        r = lax.rsqrt(jnp.mean(xv * xv, axis=-1, keepdims=True) + NORM_EPS)
        h = (xv * r) * g_ref[...]
        h_ref[...] = h.astype(BF16)
        ht_ref[...] = h.T.astype(BF16)

    return pl.pallas_call(
        body, name="prenorm_fwd", grid=(t // tm,),
        in_specs=[pl.BlockSpec((tm, d), lambda i: (i, 0)), pl.BlockSpec((1, d), lambda i: (0, 0))],
        out_specs=[pl.BlockSpec((tm, d), lambda i: (i, 0)), pl.BlockSpec((d, tm), lambda i: (0, i))],
        out_shape=[jax.ShapeDtypeStruct((t, d), BF16), jax.ShapeDtypeStruct((d, t), BF16)],
        compiler_params=_params(dimension_semantics=("parallel",)),
    )(x, gain)


def _post_loss(x, y, target, gain):
    t, d = x.shape
    tm = min(t, 256)
    inv_d = 1.0 / d

    def body(x_ref, y_ref, t_ref, g_ref, dy_ref, dz_ref, gp_ref, loss_ref):
        i = pl.program_id(0)
        yv = y_ref[...]
        gain_v = g_ref[...]
        r = lax.rsqrt(jnp.mean(yv * yv, axis=-1, keepdims=True) + NORM_EPS)
        n = yv * r
        e = (x_ref[...] + n * gain_v) - t_ref[...]
        dz = e * inv_d
        dn = dz * gain_v
        dy = r * (dn - n * jnp.mean(dn * n, axis=-1, keepdims=True))
        dy_ref[...] = dy.astype(BF16)
        dz_ref[...] = dz

        @pl.when(i == 0)
        def _():
            gp_ref[...] = jnp.zeros_like(gp_ref)
            loss_ref[...] = jnp.zeros_like(loss_ref)

        gp_ref[...] += jnp.sum(dz * n, axis=0, keepdims=True)
        row = jnp.sum(e * e, axis=-1, keepdims=True)
        loss_ref[...] += jnp.full(loss_ref.shape, 0.5 * inv_d * jnp.sum(row), F32)

    row_spec = pl.BlockSpec((tm, d), lambda i: (i, 0))
    vec_spec = pl.BlockSpec((1, d), lambda i: (0, 0))
    return pl.pallas_call(
        body, name="post_loss", grid=(t // tm,),
        in_specs=[row_spec, row_spec, row_spec, vec_spec],
        out_specs=[row_spec, row_spec, vec_spec, pl.BlockSpec((1, LANES), lambda i: (0, 0))],
        out_shape=[jax.ShapeDtypeStruct((t, d), BF16), jax.ShapeDtypeStruct((t, d), F32),
                   jax.ShapeDtypeStruct((1, d), F32), jax.ShapeDtypeStruct((1, LANES), F32)],
        compiler_params=_params(dimension_semantics=("arbitrary",)),
    )(x, y, target, gain)


def _prenorm_bwd(x, dh, dz, gain):
    t, d = x.shape
    tm = min(t, 256)

    def body(x_ref, dh_ref, dz_ref, g_ref, gx_ref, gp_ref):
        i = pl.program_id(0)
        xv = x_ref[...]
        r = lax.rsqrt(jnp.mean(xv * xv, axis=-1, keepdims=True) + NORM_EPS)
        n = xv * r
        dhv = dh_ref[...]
        dn = dhv * g_ref[...]
        gx_ref[...] = dz_ref[...] + r * (dn - n * jnp.mean(dn * n, axis=-1, keepdims=True))

        @pl.when(i == 0)
        def _():
            gp_ref[...] = jnp.zeros_like(gp_ref)

        gp_ref[...] += jnp.sum(dhv * n, axis=0, keepdims=True)

    row_spec = pl.BlockSpec((tm, d), lambda i: (i, 0))
    vec_spec = pl.BlockSpec((1, d), lambda i: (0, 0))
    return pl.pallas_call(
        body, name="prenorm_bwd", grid=(t // tm,),
        in_specs=[row_spec, row_spec, row_spec, vec_spec],
        out_specs=[row_spec, vec_spec],
        out_shape=[jax.ShapeDtypeStruct((t, d), F32), jax.ShapeDtypeStruct((1, d), F32)],
        compiler_params=_params(dimension_semantics=("arbitrary",)),
    )(x, dh, dz, gain)


def _matmul(name, a, b, *, out_shape, grid, a_spec, b_spec, o_spec, nt=False, perm=None):
    nk = grid[2]
    tm, tn = [s for s in o_spec.block_shape if s is not None][-2:]
    acc_in_out = out_shape.dtype == F32

    def body(*refs):
        if perm is not None:
            refs = refs[1:]
        a_ref, b_ref, o_ref = refs[:3]
        part = _dot_nt(a_ref[...], b_ref[...]) if nt else _dot(a_ref[...], b_ref[...])
        if nk == 1:
            o_ref[...] = part.astype(o_ref.dtype)
            return
        acc_ref = o_ref if acc_in_out else refs[3]
        k = pl.program_id(2)

        @pl.when(k == 0)
        def _():
            acc_ref[...] = part

        @pl.when(k > 0)
        def _():
            acc_ref[...] += part

        if not acc_in_out:
            @pl.when(k == nk - 1)
            def _():
                o_ref[...] = acc_ref[...].astype(o_ref.dtype)

    scratch = [] if (nk == 1 or acc_in_out) else [pltpu.VMEM((tm, tn), F32)]
    cp = _params(dimension_semantics=("parallel", "parallel", "arbitrary"))
    if perm is None:
        return pl.pallas_call(body, name=name, grid=grid, in_specs=[a_spec, b_spec], out_specs=o_spec,
                              out_shape=out_shape, scratch_shapes=scratch, compiler_params=cp)(a, b)
    gs = pltpu.PrefetchScalarGridSpec(num_scalar_prefetch=1, grid=grid, in_specs=[a_spec, b_spec],
                                      out_specs=o_spec, scratch_shapes=scratch)
    return pl.pallas_call(body, name=name, grid_spec=gs, out_shape=out_shape,
                          compiler_params=cp)(jnp.asarray(perm), a, b)


def _proj_mm(h, w_full, perm):
    t, d = h.shape
    n_tiles = len(perm)
    tm = min(t, 1024)
    return _matmul(
        "proj_mm", h, w_full, perm=perm, grid=(t // tm, n_tiles, 1),
        out_shape=jax.ShapeDtypeStruct((t, n_tiles * COL_TILE), F32),
        a_spec=pl.BlockSpec((tm, d), lambda i, j, k, p: (i, 0)),
        b_spec=pl.BlockSpec((d, COL_TILE), lambda i, j, k, p: (0, p[j])),
        o_spec=pl.BlockSpec((tm, COL_TILE), lambda i, j, k, p: (i, j)))


def _gw_in_mm(ht, dproj, perm):
    d, t = ht.shape
    n_tiles = len(perm)
    hd = d // 2
    tm = min(hd, 1024)
    per_half = hd // tm
    return _matmul(
        "gw_in_mm", ht, dproj, perm=perm, grid=(d // tm, n_tiles, 1),
        out_shape=jax.ShapeDtypeStruct((2, hd, n_tiles * COL_TILE), BF16),
        a_spec=pl.BlockSpec((tm, t), lambda i, j, k, p: (i, 0)),
        b_spec=pl.BlockSpec((t, COL_TILE), lambda i, j, k, p: (0, j)),
        o_spec=pl.BlockSpec((None, tm, COL_TILE), lambda i, j, k, p: (i // per_half, i % per_half, p[j])))


def _dh_mm(dproj, w_full, perm):
    t = dproj.shape[0]
    d = w_full.shape[0]
    n_tiles = len(perm)
    tm = min(t, 2048)
    tn = min(d, 2048)
    return _matmul(
        "dh_mm", dproj, w_full, perm=perm, nt=True, grid=(t // tm, d // tn, n_tiles),
        out_shape=jax.ShapeDtypeStruct((t, d), F32),
        a_spec=pl.BlockSpec((tm, COL_TILE), lambda i, j, k, p: (i, k)),
        b_spec=pl.BlockSpec((tn, COL_TILE), lambda i, j, k, p: (j, p[k])),
        o_spec=pl.BlockSpec((tm, tn), lambda i, j, k, p: (i, j)))


def _out_mm(mixed, w_out_full):
    t, dm = mixed.shape
    d = w_out_full.shape[1]
    tm = min(t, 1024)
    tn = min(d, 512)
    tk = min(dm, 4096)
    return _matmul(
        "out_mm", mixed, w_out_full, grid=(t // tm, d // tn, dm // tk),
        out_shape=jax.ShapeDtypeStruct((t, d), F32),
        a_spec=pl.BlockSpec((tm, tk), lambda i, j, k: (i, k)),
        b_spec=pl.BlockSpec((tk, tn), lambda i, j, k: (k, j)),
        o_spec=pl.BlockSpec((tm, tn), lambda i, j, k: (i, j)))


def _dmixed_mm(dy, w_out_full):
    t, d = dy.shape
    dm = w_out_full.shape[0]
    tm = min(t, 1024)
    tn = min(dm, 1024)
    return _matmul(
        "dmixed_mm", dy, w_out_full, nt=True, grid=(t // tm, dm // tn, 1),
        out_shape=jax.ShapeDtypeStruct((t, dm), F32),
        a_spec=pl.BlockSpec((tm, d), lambda i, j, k: (i, 0)),
        b_spec=pl.BlockSpec((tn, d), lambda i, j, k: (j, 0)),
        o_spec=pl.BlockSpec((tm, tn), lambda i, j, k: (i, j)))


def _gw_out_mm(mixed_t, dy):
    dm, t = mixed_t.shape
    d = dy.shape[1]
    hr = dm // (2 * N_CHIPS)
    tn = min(d, 1024)
    return _matmul(
        "gw_out_mm", mixed_t, dy, grid=(dm // hr, d // tn, 1),
        out_shape=jax.ShapeDtypeStruct((2, N_CHIPS, hr, d), BF16),
        a_spec=pl.BlockSpec((hr, t), lambda i, j, k: (i, 0)),
        b_spec=pl.BlockSpec((t, tn), lambda i, j, k: (0, j)),
        o_spec=pl.BlockSpec((None, None, hr, tn), lambda i, j, k: (i % 2, i // 2, 0, j)))


def _lane_half():
    return lax.broadcasted_iota(jnp.int32, (WINDOW, LANES), 1) // ATTN_HEAD_DIM


def _dup_kv(tile, kh):
    return jnp.where(_lane_half() == kh, tile, pltpu.roll(tile, ATTN_HEAD_DIM, 1))


def _stack_heads(tiles, kh):
    half = _lane_half()
    pieces = []
    for g in range(GQA_GROUP):
        pieces.append(jnp.where(half == g % 2, tiles[4 * kh + g // 2], 0.0))
    return jnp.concatenate(pieces, axis=0)


def _unstack_heads(stacked):
    half = _lane_half()
    out = []
    for j in range(GQA_GROUP // 2):
        a = stacked[(2 * j) * WINDOW:(2 * j + 1) * WINDOW]
        b = stacked[(2 * j + 1) * WINDOW:(2 * j + 2) * WINDOW]
        out.append(jnp.where(half == 0, a, b))
    return out


def _attn_probs(qs, kcat, sink_col, n):
    rows = GQA_GROUP * WINDOW
    s = _dot_nt(qs, kcat)
    qi = lax.broadcasted_iota(jnp.int32, (rows, 2 * WINDOW), 0) % WINDOW
    kj = lax.broadcasted_iota(jnp.int32, (rows, 2 * WINDOW), 1)
    first_key = WINDOW * (1 - jnp.minimum(n, 1))
    valid = (kj > qi) & (kj <= qi + WINDOW) & (kj >= first_key)
    s = jnp.where(valid, s, NEG_BIG)
    mx = jnp.maximum(jnp.max(s, axis=-1, keepdims=True), sink_col)
    p = jnp.where(valid, jnp.exp(s - mx), 0.0)
    p_sink = jnp.exp(sink_col - mx)
    inv = 1.0 / (jnp.sum(p, axis=-1, keepdims=True) + p_sink)
    return p * inv, p_sink * inv


def _attn_operands(sink_ref, q_tiles, kp_ref, kc_ref, vp_ref, vc_ref, m, kh):
    qs = _stack_heads([qt * ATTN_SCALE for qt in q_tiles], kh).astype(BF16)
    kcat = jnp.concatenate([_dup_kv(kp_ref[...], kh), _dup_kv(kc_ref[...], kh)], axis=0).astype(BF16)
    vcat = jnp.concatenate([_dup_kv(vp_ref[...], kh), _dup_kv(vc_ref[...], kh)], axis=0).astype(BF16)
    heads_per_group = ATTN_GROUP_LANES // ATTN_HEAD_DIM
    sink_col = jnp.concatenate(
        [jnp.full((WINDOW, 1), sink_ref[0, m * heads_per_group + kh * GQA_GROUP + g], F32)
         for g in range(GQA_GROUP)], axis=0)
    return qs, kcat, vcat, sink_col


def _attn_specs(lay, d):
    a_blk = lay["a_off"] // (2 * ATTN_GROUP_LANES)
    k_blk = lay["k_off"] // LANES
    v_blk = lay["v_off"] // LANES
    qg = pl.BlockSpec((WINDOW, 2 * ATTN_GROUP_LANES), lambda m, n: (n, a_blk + m))
    kp = pl.BlockSpec((WINDOW, LANES), lambda m, n: (jnp.maximum(n - 1, 0), k_blk + m))
    kc = pl.BlockSpec((WINDOW, LANES), lambda m, n: (n, k_blk + m))
    vp = pl.BlockSpec((WINDOW, LANES), lambda m, n: (jnp.maximum(n - 1, 0), v_blk + m))
    vc = pl.BlockSpec((WINDOW, LANES), lambda m, n: (n, v_blk + m))
    return qg, kp, kc, vp, vc


def _attn_fwd(proj, sinks, lay, d):
    t = proj.shape[0]
    n_groups = d // ATTN_GROUP_LANES
    pairs = ATTN_GROUP_LANES // LANES

    def body(sink_ref, qg_ref, kp_ref, kc_ref, vp_ref, vc_ref, mix_ref, o_ref):
        m = pl.program_id(0)
        n = pl.program_id(1)
        q_tiles = [qg_ref[:, p * LANES:(p + 1) * LANES] for p in range(pairs)]
        for kh in range(2):
            qs, kcat, vcat, sink_col = _attn_operands(sink_ref, q_tiles, kp_ref, kc_ref, vp_ref, vc_ref, m, kh)
            probs, _ = _attn_probs(qs, kcat, sink_col, n)
            out = _dot(probs.astype(BF16), vcat)
            for j, tile in enumerate(_unstack_heads(out)):
                p = 4 * kh + j
                lanes = slice(p * LANES, (p + 1) * LANES)
                gate = qg_ref[:, ATTN_GROUP_LANES + p * LANES:ATTN_GROUP_LANES + (p + 1) * LANES]
                o_ref[:, lanes] = tile
                mix_ref[:, lanes] = (tile * (gate * _sigmoid(gate))).astype(BF16)

    qg, kp, kc, vp, vc = _attn_specs(lay, d)
    out_blk = pl.BlockSpec((WINDOW, ATTN_GROUP_LANES), lambda m, n: (n, m))
    return pl.pallas_call(
        body, name="attn_fwd", grid=(n_groups, t // WINDOW),
        in_specs=[pl.BlockSpec(memory_space=pltpu.SMEM), qg, kp, kc, vp, vc],
        out_specs=[out_blk, out_blk],
        out_shape=[jax.ShapeDtypeStruct((t, 2 * d), BF16), jax.ShapeDtypeStruct((t, d), F32)],
        compiler_params=_params(dimension_semantics=("parallel", "parallel")),
    )(sinks, proj, proj, proj, proj, proj)


def _attn_bwd(proj, sinks, attn_o, dmixed, dproj, lay, d):
    t = proj.shape[0]
    n_groups = d // ATTN_GROUP_LANES
    pairs = ATTN_GROUP_LANES // LANES
    kv_w = n_groups * LANES

    def body(sink_ref, qg_ref, kp_ref, kc_ref, vp_ref, vc_ref, o_ref, dm_ref, dproj_hbm,
             dqg_ref, dkc_ref, dkp_ref, dvc_ref, dvp_ref, dsink_ref):
        del dproj_hbm
        m = pl.program_id(0)
        n = pl.program_id(1)
        half = _lane_half()
        q_tiles = [qg_ref[:, p * LANES:(p + 1) * LANES] for p in range(pairs)]
        do_tiles, o_tiles = [], []
        for p in range(pairs):
            lanes = slice(p * LANES, (p + 1) * LANES)
            gate = qg_ref[:, ATTN_GROUP_LANES + p * LANES:ATTN_GROUP_LANES + (p + 1) * LANES]
            sg = _sigmoid(gate)
            dmix = dm_ref[:, lanes]
            ov = o_ref[:, lanes]
            dqg_ref[:, ATTN_GROUP_LANES + p * LANES:ATTN_GROUP_LANES + (p + 1) * LANES] = (
                dmix * ov * (sg * (1.0 + gate * (1.0 - sg)))).astype(BF16)
            do_tiles.append(dmix * (gate * sg))
            o_tiles.append(ov)

        sub = lax.broadcasted_iota(jnp.int32, (8, LANES), 0)
        lane = lax.broadcasted_iota(jnp.int32, (8, LANES), 1)
        dsink = jnp.zeros((8, LANES), F32)
        dk_cur = dk_prev = dv_cur = dv_prev = jnp.zeros((WINDOW, LANES), F32)
        for kh in range(2):
            qs, kcat, vcat, sink_col = _attn_operands(sink_ref, q_tiles, kp_ref, kc_ref, vp_ref, vc_ref, m, kh)
            probs, p_sink = _attn_probs(qs, kcat, sink_col, n)
            dos = _stack_heads(do_tiles, kh)
            delta = jnp.sum(dos * _stack_heads(o_tiles, kh), axis=-1, keepdims=True)
            dos = dos.astype(BF16)
            dp = _dot_nt(dos, vcat)
            ds = (probs * (dp - delta)).astype(BF16)
            dv = _dot_tn(probs.astype(BF16), dos)
            dv = dv + pltpu.roll(dv, ATTN_HEAD_DIM, 1)
            dk = _dot_tn(ds, qs)
            dk = dk + pltpu.roll(dk, ATTN_HEAD_DIM, 1)
            dq = _dot(ds, kcat)
            for j, tile in enumerate(_unstack_heads(dq)):
                p = 4 * kh + j
                dqg_ref[:, p * LANES:(p + 1) * LANES] = (tile * ATTN_SCALE).astype(BF16)
            dk_prev = jnp.where(half == kh, dk[:WINDOW], dk_prev)
            dk_cur = jnp.where(half == kh, dk[WINDOW:], dk_cur)
            dv_prev = jnp.where(half == kh, dv[:WINDOW], dv_prev)
            dv_cur = jnp.where(half == kh, dv[WINDOW:], dv_cur)
            sink_terms = p_sink * delta
            for g in range(GQA_GROUP):
                val = -jnp.sum(sink_terms[g * WINDOW:(g + 1) * WINDOW])
                dsink = dsink + jnp.where((sub == 0) & (lane == kh * GQA_GROUP + g), val, 0.0)
        dkc_ref[...] = dk_cur
        dkp_ref[...] = dk_prev
        dvc_ref[...] = dv_cur
        dvp_ref[...] = dv_prev

        @pl.when(n == 0)
        def _():
            dsink_ref[...] = jnp.zeros_like(dsink_ref)

        dsink_ref[...] += dsink

    qg, kp, kc, vp, vc = _attn_specs(lay, d)
    a_blk = lay["a_off"] // (2 * ATTN_GROUP_LANES)
    grp = pl.BlockSpec((WINDOW, ATTN_GROUP_LANES), lambda m, n: (n, m))
    kv_blk = pl.BlockSpec((WINDOW, LANES), lambda m, n: (n, m))
    kv_shape = jax.ShapeDtypeStruct((t, kv_w), F32)
    outs = pl.pallas_call(
        body, name="attn_bwd", grid=(n_groups, t // WINDOW),
        in_specs=[pl.BlockSpec(memory_space=pltpu.SMEM), qg, kp, kc, vp, vc, grp, grp,
                  pl.BlockSpec(memory_space=pl.ANY)],
        out_specs=[pl.BlockSpec((WINDOW, 2 * ATTN_GROUP_LANES), lambda m, n: (n, a_blk + m)),
                   kv_blk, kv_blk, kv_blk, kv_blk, pl.BlockSpec((8, LANES), lambda m, n: (m, 0))],
        out_shape=[jax.ShapeDtypeStruct(dproj.shape, BF16), kv_shape, kv_shape, kv_shape, kv_shape,
                   jax.ShapeDtypeStruct((n_groups * 8, LANES), F32)],
        input_output_aliases={8: 0},
        compiler_params=_params(dimension_semantics=("parallel", "arbitrary")),
    )(sinks, proj, proj, proj, proj, proj, attn_o, dmixed, dproj)
    return outs


def _kv_combine(dkc, dkp, dvc, dvp, dproj, lay):
    t, kv_w = dkc.shape
    nb = t // WINDOW
    kv_blk_idx = lay["k_off"] // (2 * kv_w)

    def body(dkc_ref, dkp_ref, dvc_ref, dvp_ref, dproj_hbm, o_ref):
        del dproj_hbm
        keep = (pl.program_id(0) < nb - 1).astype(F32)
        o_ref[:, :kv_w] = (dkc_ref[...] + keep * dkp_ref[...]).astype(BF16)
        o_ref[:, kv_w:] = (dvc_ref[...] + keep * dvp_ref[...]).astype(BF16)

    cur = pl.BlockSpec((WINDOW, kv_w), lambda n: (n, 0))
    nxt = pl.BlockSpec((WINDOW, kv_w), lambda n: (jnp.minimum(n + 1, nb - 1), 0))
    return pl.pallas_call(
        body, name="kv_combine", grid=(nb,),
        in_specs=[cur, nxt, cur, nxt, pl.BlockSpec(memory_space=pl.ANY)],
        out_specs=pl.BlockSpec((WINDOW, 2 * kv_w), lambda n: (n, kv_blk_idx)),
        out_shape=jax.ShapeDtypeStruct(dproj.shape, BF16),
        input_output_aliases={4: 0},
        compiler_params=_params(dimension_semantics=("parallel",)),
    )(dkc, dkp, dvc, dvp, dproj)


def _lower_bound(lbl_ref):
    l0 = lbl_ref[0:1, :]
    l1 = lbl_ref[1:2, :]
    mx = jnp.maximum(l0, l1)
    e0 = jnp.exp(l0 - mx)
    e1 = jnp.exp(l1 - mx)
    return e0 / (e0 + e1)


def _chunk_masks():
    ti = lax.broadcasted_iota(jnp.int32, (CHUNK, CHUNK), 0)
    si = lax.broadcasted_iota(jnp.int32, (CHUNK, CHUNK), 1)
    diag = ((ti // HALF_CHUNK) == (si // HALF_CHUNK)) & (si <= ti)
    off = (ti >= HALF_CHUNK) & (si < HALF_CHUNK)
    lower = (si <= ti).astype(BF16)
    upper = (si >= ti).astype(BF16)
    return diag, off, lower, upper


def _rnn_gates(rq, rf, lb):
    sf = _sigmoid(rf)
    f = lb + (1.0 - lb) * sf
    sq = _sigmoid(rq)
    return sf, f, jnp.log(f), 1.0 - f, sq, rq * sq


def _rnn_decays(g_cum):
    row = lax.broadcasted_iota(jnp.int32, g_cum.shape, 0)
    ref_d = jnp.where(row < HALF_CHUNK, g_cum[HALF_CHUNK // 2 - 1:HALF_CHUNK // 2],
                      g_cum[HALF_CHUNK + HALF_CHUNK // 2 - 1:HALF_CHUNK + HALF_CHUNK // 2])
    ref_o = g_cum[HALF_CHUNK - 1:HALF_CHUNK]
    last = g_cum[CHUNK - 1:CHUNK]
    return dict(eq_d=jnp.exp(g_cum - ref_d), ek_d=jnp.exp(ref_d - g_cum),
                eq_o=jnp.exp(jnp.minimum(g_cum - ref_o, 0.0)), ek_o=jnp.exp(jnp.minimum(ref_o - g_cum, 0.0)),
                eg=jnp.exp(g_cum), ekl=jnp.exp(last - g_cum), e_last=jnp.exp(last))


def _head(a, j):
    return a[:, j * RNN_HEAD_DIM:(j + 1) * RNN_HEAD_DIM]


def _rnn_specs(t, tb, d):
    gw = RNN_GROUP_HEADS * RNN_HEAD_DIM
    return gw, t // tb, tb // CHUNK


def _rnn_fwd(proj, lb_logits, rnn_gain, mixed, d):
    t = proj.shape[0]
    tb = min(t, 256)
    gw, ntb, nch = _rnn_specs(t, tb, d)
    n_groups = d // gw
    n_heads = d // RNN_HEAD_DIM

    def body(blk_ref, lbl_ref, gain_ref, mixed_hbm, mix_ref, o_ref, st_out_ref, st_ref):
        del mixed_hbm

        @pl.when(pl.program_id(1) == 0)
        def _():
            st_ref[...] = jnp.zeros_like(st_ref)

        lb = _lower_bound(lbl_ref)
        gain = gain_ref[...]
        diag, off, lower, _ = _chunk_masks()

        def chunk(c, carry):
            rows = pl.ds(pl.multiple_of(c * CHUNK, CHUNK), CHUNK)
            rq = blk_ref[rows, 0:gw]
            rf = blk_ref[rows, gw:2 * gw]
            v = blk_ref[rows, 2 * gw:3 * gw]
            rg = blk_ref[rows, 3 * gw:4 * gw]
            _, _, g, k, _, q = _rnn_gates(rq, rf, lb)
            dec = _rnn_decays(_tri_dot(lower, g))
            qd = (q * dec["eq_d"]).astype(BF16)
            kd = (k * dec["ek_d"]).astype(BF16)
            qo = (q * dec["eq_o"]).astype(BF16)
            ko = (k * dec["ek_o"]).astype(BF16)
            qe = (q * dec["eg"]).astype(BF16)
            kl = (k * dec["ekl"]).astype(BF16)
            vb = v.astype(BF16)
            outs = []
            for j in range(RNN_GROUP_HEADS):
                st = st_ref[j]
                st_out_ref[j, c] = st
                attn = jnp.where(diag, _dot_nt(_head(qd, j), _head(kd, j)),
                                 jnp.where(off, _dot_nt(_head(qo, j), _head(ko, j)), 0.0))
                o = _dot(attn.astype(BF16), _head(vb, j)) + _dot_nt(_head(qe, j), st.astype(BF16))
                st_ref[j] = st * _head(dec["e_last"], j) + _dot_tn(_head(vb, j), _head(kl, j))
                rr = lax.rsqrt(jnp.mean(o * o, axis=-1, keepdims=True) + NORM_EPS)
                o_ref[rows, j * RNN_HEAD_DIM:(j + 1) * RNN_HEAD_DIM] = o
                outs.append(o * rr)
            on = jnp.concatenate(outs, axis=1) * gain
            mix_ref[rows, :] = (on * (rg * _sigmoid(rg))).astype(BF16)
            return carry

        lax.fori_loop(0, nch, chunk, 0)

    return pl.pallas_call(
        body, name="rnn_fwd", grid=(n_groups, ntb),
        in_specs=[pl.BlockSpec((tb, 4 * gw), lambda h, i: (i, h)),
                  pl.BlockSpec((2, gw), lambda h, i: (0, h)),
                  pl.BlockSpec((1, gw), lambda h, i: (0, h)),
                  pl.BlockSpec(memory_space=pl.ANY)],
        out_specs=[pl.BlockSpec((tb, gw), lambda h, i: (i, d // gw + h)),
                   pl.BlockSpec((tb, gw), lambda h, i: (i, h)),
                   pl.BlockSpec((RNN_GROUP_HEADS, nch, RNN_HEAD_DIM, RNN_HEAD_DIM), lambda h, i: (h, i, 0, 0))],
        out_shape=[jax.ShapeDtypeStruct(mixed.shape, BF16), jax.ShapeDtypeStruct((t, d), F32),
                   jax.ShapeDtypeStruct((n_heads, t // CHUNK, RNN_HEAD_DIM, RNN_HEAD_DIM), F32)],
        scratch_shapes=[pltpu.VMEM((RNN_GROUP_HEADS, RNN_HEAD_DIM, RNN_HEAD_DIM), F32)],
        input_output_aliases={3: 0},
        compiler_params=_params(dimension_semantics=("parallel", "arbitrary")),
    )(proj, lb_logits, rnn_gain, mixed)


def _rnn_bwd(proj, lb_logits, rnn_gain, o_pre, states, dmixed, d_total, d):
    t = proj.shape[0]
    tb = min(t, 256)
    gw, ntb, nch = _rnn_specs(t, tb, d)
    n_groups = d // gw

    def body(blk_ref, lbl_ref, gain_ref, o_ref, st_in_ref, dm_ref, dproj_ref, dgain_ref, dlb_ref, dst_ref):
        @pl.when(pl.program_id(1) == 0)
        def _():
            dst_ref[...] = jnp.zeros_like(dst_ref)
            dgain_ref[...] = jnp.zeros_like(dgain_ref)
            dlb_ref[...] = jnp.zeros_like(dlb_ref)

        lb = _lower_bound(lbl_ref)
        gain = gain_ref[...]
        diag, off, lower, upper = _chunk_masks()
        last_row = lax.broadcasted_iota(jnp.int32, (CHUNK, RNN_HEAD_DIM), 0) == CHUNK - 1

        def chunk(step, carry):
            c = nch - 1 - step
            rows = pl.ds(pl.multiple_of(c * CHUNK, CHUNK), CHUNK)
            rq = blk_ref[rows, 0:gw]
            rf = blk_ref[rows, gw:2 * gw]
            v = blk_ref[rows, 2 * gw:3 * gw]
            rg = blk_ref[rows, 3 * gw:4 * gw]
            sf, f, g, k, sq, q = _rnn_gates(rq, rf, lb)
            dec = _rnn_decays(_tri_dot(lower, g))
            qd = (q * dec["eq_d"]).astype(BF16)
            kd = (k * dec["ek_d"]).astype(BF16)
            qo = (q * dec["eq_o"]).astype(BF16)
            ko = (k * dec["ek_o"]).astype(BF16)
            qe = (q * dec["eg"]).astype(BF16)
            kl = (k * dec["ekl"]).astype(BF16)
            vb = v.astype(BF16)

            o = o_ref[rows, :]
            dmix = dm_ref[rows, :]
            sg = _sigmoid(rg)
            n_parts = []
            for j in range(RNN_GROUP_HEADS):
                oj = _head(o, j)
                n_parts.append(oj * lax.rsqrt(jnp.mean(oj * oj, axis=-1, keepdims=True) + NORM_EPS))
            nrm = jnp.concatenate(n_parts, axis=1)
            d_on = dmix * (rg * sg)
            d_rg = dmix * (nrm * gain) * (sg * (1.0 + rg * (1.0 - sg)))
            dgain_ref[...] += jnp.sum(d_on * nrm, axis=0, keepdims=True)
            dn = d_on * gain

            dq_parts, dk_parts, dv_parts, dg_parts = [], [], [], []
            for j in range(RNN_GROUP_HEADS):
                oj, nj, dnj = _head(o, j), _head(nrm, j), _head(dn, j)
                rr = lax.rsqrt(jnp.mean(oj * oj, axis=-1, keepdims=True) + NORM_EPS)
                do = (rr * (dnj - nj * jnp.mean(dnj * nj, axis=-1, keepdims=True))).astype(BF16)
                st = st_in_ref[j, c]
                dst = dst_ref[j]
                stb, dstb = st.astype(BF16), dst.astype(BF16)
                qdj, kdj, qoj, koj = _head(qd, j), _head(kd, j), _head(qo, j), _head(ko, j)
                attn = jnp.where(diag, _dot_nt(qdj, kdj), jnp.where(off, _dot_nt(qoj, koj), 0.0))
                dattn = _dot_nt(do, _head(vb, j))
                da_d = jnp.where(diag, dattn, 0.0).astype(BF16)
                da_o = jnp.where(off, dattn, 0.0).astype(BF16)
                dv = _dot_tn(attn.astype(BF16), do) + _dot_nt(_head(kl, j), dstb)
                dq_inter = _dot(do, stb) * _head(dec["eg"], j)
                dq_d, dq_o = _dot(da_d, kdj), _dot(da_o, koj)
                dq = dq_inter + dq_d * _head(dec["eq_d"], j) + dq_o * _head(dec["eq_o"], j)
                dk_inter = _dot(_head(vb, j), dstb) * _head(dec["ekl"], j)
                dk_d, dk_o = _dot_tn(da_d, qdj), _dot_tn(da_o, qoj)
                dk = dk_inter + dk_d * _head(dec["ek_d"], j) + dk_o * _head(dec["ek_o"], j)
                kj, qj = _head(k, j), _head(q, j)
                e_last = _head(dec["e_last"], j)
                extra = (jnp.sum(kj * dk_inter, axis=0, keepdims=True)
                         + e_last * jnp.sum(st * dst, axis=0, keepdims=True))
                dg_cum = (qj * dq_inter - kj * dk_inter
                          + (qdj.astype(F32) * dq_d + qoj.astype(F32) * dq_o)
                          - (kdj.astype(F32) * dk_d + koj.astype(F32) * dk_o))
                dg_parts.append(jnp.where(last_row, dg_cum + extra, dg_cum))
                dst_ref[j] = dst * e_last + _dot_tn(do, _head(qe, j))
                dq_parts.append(dq)
                dk_parts.append(dk)
                dv_parts.append(dv)

            dq = jnp.concatenate(dq_parts, axis=1)
            dk = jnp.concatenate(dk_parts, axis=1)
            dg = _tri_dot(upper, jnp.concatenate(dg_parts, axis=1))
            df = dg / f - dk
            dlb_ref[...] += jnp.sum(df * (1.0 - sf), axis=0, keepdims=True)
            d_rf = df * (1.0 - lb) * (sf * (1.0 - sf))
            d_rq = dq * (sq * (1.0 + rq * (1.0 - sq)))
            dproj_ref[rows, 0:gw] = d_rq.astype(BF16)
            dproj_ref[rows, gw:2 * gw] = d_rf.astype(BF16)
            dproj_ref[rows, 2 * gw:3 * gw] = jnp.concatenate(dv_parts, axis=1).astype(BF16)
            dproj_ref[rows, 3 * gw:4 * gw] = d_rg.astype(BF16)
            return carry

        lax.fori_loop(0, nch, chunk, 0)

    rev = lambda i: ntb - 1 - i
    vec = pl.BlockSpec((1, gw), lambda h, i: (0, h))
    return pl.pallas_call(
        body, name="rnn_bwd", grid=(n_groups, ntb),
        in_specs=[pl.BlockSpec((tb, 4 * gw), lambda h, i: (rev(i), h)),
                  pl.BlockSpec((2, gw), lambda h, i: (0, h)), vec,
                  pl.BlockSpec((tb, gw), lambda h, i: (rev(i), h)),
                  pl.BlockSpec((RNN_GROUP_HEADS, nch, RNN_HEAD_DIM, RNN_HEAD_DIM), lambda h, i: (h, rev(i), 0, 0)),
                  pl.BlockSpec((tb, gw), lambda h, i: (rev(i), d // gw + h))],
        out_specs=[pl.BlockSpec((tb, 4 * gw), lambda h, i: (rev(i), h)), vec, vec],
        out_shape=[jax.ShapeDtypeStruct((t, d_total), BF16), jax.ShapeDtypeStruct((1, d), F32),
                   jax.ShapeDtypeStruct((1, d), F32)],
        scratch_shapes=[pltpu.VMEM((RNN_GROUP_HEADS, RNN_HEAD_DIM, RNN_HEAD_DIM), F32)],
        compiler_params=_params(dimension_semantics=("parallel", "arbitrary")),
    )(proj, lb_logits, rnn_gain, o_pre, states, dmixed)


def _local_grads(x, target, w_in_full, w_out_full, sinks, lb_logits, rnn_gain, pre_gain, post_gain):
    t, d = x.shape
    lay = _layout(d)
    perm = lay["perm"]
    h, ht = _prenorm_fwd(x, pre_gain)
    proj = _proj_mm(h, w_in_full, perm)
    mixed, attn_o = _attn_fwd(proj, sinks, lay, d)
    mixed, o_pre, states = _rnn_fwd(proj, lb_logits, rnn_gain, mixed, d)
    y = _out_mm(mixed, w_out_full)
    dy, dz, g_post, loss = _post_loss(x, y, target, post_gain)
    dmixed = _dmixed_mm(dy, w_out_full)
    gw_out = _gw_out_mm(_transpose_bf16(mixed, "mixed_t"), dy)
    dproj, g_rnn, g_lb = _rnn_bwd(proj, lb_logits, rnn_gain, o_pre, states, dmixed, lay["total"], d)
    dproj, dkc, dkp, dvc, dvp, dsink = _attn_bwd(proj, sinks, attn_o, dmixed, dproj, lay, d)
    dproj = _kv_combine(dkc, dkp, dvc, dvp, dproj, lay)
    dh = _dh_mm(dproj, w_in_full, perm)
    gw_in = _gw_in_mm(ht, dproj, perm)
    grad_x, g_pre = _prenorm_bwd(x, dh, dz, pre_gain)
    heads_per_group = ATTN_GROUP_LANES // ATTN_HEAD_DIM
    g_sink = dsink.reshape(d // ATTN_GROUP_LANES, 8, LANES)[:, 0, :heads_per_group].reshape(1, -1)
    return loss, grad_x, gw_in, gw_out, dict(sink=g_sink, lb=g_lb, rnn=g_rnn, pre=g_pre, post=g_post)


def _mesh_pos():
    x, y, c = lax.axis_index("x"), lax.axis_index("y"), lax.axis_index("c")
    chips = [(1 - x, y), (x, 1 - y), (1 - x, 1 - y)]
    return x, y, c, chips


def _remote(src, dst, send_sem, recv_sem, device):
    return pltpu.make_async_remote_copy(src_ref=src, dst_ref=dst, send_sem=send_sem, recv_sem=recv_sem,
                                        device_id=device, device_id_type=MESH)


HBM_SPEC = pl.BlockSpec(memory_space=pl.ANY)


def _gather_weights(w_in_b, w_out_b):
    d, sc = w_in_b.shape
    sr = w_out_b.shape[0]
    hd, hr = d // 2, sr // 2

    def body(win_ref, wout_ref, wi_full, wo_full, send_sems, recv_sems, local_sems):
        x, y, c, chips = _mesh_pos()
        me = 2 * x + y
        sibling = (x, y, 1 - c)

        def in_piece(chip, half):
            return wi_full.at[pl.ds(half * hd, hd), pl.ds(pl.multiple_of(chip * sc, LANES), sc)]

        def out_piece(chip, half):
            return wo_full.at[pl.ds(pl.multiple_of(chip * sr + half * hr, 8), hr), :]

        mine_in = pltpu.make_async_copy(win_ref, wi_full.at[:, pl.ds(pl.multiple_of(me * sc, LANES), sc)],
                                        local_sems.at[0])
        mine_out = pltpu.make_async_copy(wout_ref, wo_full.at[pl.ds(pl.multiple_of(me * sr, 8), sr), :],
                                         local_sems.at[1])
        mine_in.start()
        mine_out.start()
        first = []
        for j, (px, py) in enumerate(chips):
            first.append(_remote(win_ref.at[pl.ds(c * hd, hd), :], in_piece(me, c),
                                 send_sems.at[j], recv_sems.at[j], (px, py, c)))
            first.append(_remote(wout_ref.at[pl.ds(c * hr, hr), :], out_piece(me, c),
                                 send_sems.at[3 + j], recv_sems.at[3 + j], (px, py, c)))
        for cp in first:
            cp.start()
        passed = []
        for j, (px, py) in enumerate(chips):
            chip = 2 * px + py
            for k, piece in ((j, in_piece), (3 + j, out_piece)):
                landed = piece(chip, c)
                _remote(landed, landed, send_sems.at[k], recv_sems.at[k], sibling).wait_recv()
                fwd = _remote(landed, landed, send_sems.at[6 + k], recv_sems.at[6 + k], sibling)
                fwd.start()
                passed.append(fwd)
        for j, (px, py) in enumerate(chips):
            chip = 2 * px + py
            for k, piece in ((j, in_piece), (3 + j, out_piece)):
                theirs = piece(chip, 1 - c)
                _remote(theirs, theirs, send_sems.at[6 + k], recv_sems.at[6 + k], sibling).wait_recv()
        for cp in first + passed:
            cp.wait_send()
        mine_in.wait()
        mine_out.wait()

    return pl.pallas_call(
        body, name="gather_weights",
        in_specs=[HBM_SPEC, HBM_SPEC], out_specs=[HBM_SPEC, HBM_SPEC],
        out_shape=[jax.ShapeDtypeStruct((d, N_CHIPS * sc), BF16), jax.ShapeDtypeStruct((N_CHIPS * sr, w_out_b.shape[1]), BF16)],
        scratch_shapes=[pltpu.SemaphoreType.DMA((12,)), pltpu.SemaphoreType.DMA((12,)), pltpu.SemaphoreType.DMA((2,))],
    )(w_in_b, w_out_b)


def _pair_exchange(gw_in, gw_out):
    _, hd, d_in = gw_in.shape
    _, _, hr, d = gw_out.shape

    def body(gi_ref, go_ref, ri_ref, ro_ref, send_sems, recv_sems):
        x, y, c, _ = _mesh_pos()
        sibling = (x, y, 1 - c)
        a = _remote(gi_ref.at[1 - c], ri_ref, send_sems.at[0], recv_sems.at[0], sibling)
        b = _remote(go_ref.at[1 - c], ro_ref, send_sems.at[1], recv_sems.at[1], sibling)
        a.start()
        b.start()
        a.wait()
        b.wait()

    return pl.pallas_call(
        body, name="pair_exchange",
        in_specs=[HBM_SPEC, HBM_SPEC], out_specs=[HBM_SPEC, HBM_SPEC],
        out_shape=[jax.ShapeDtypeStruct((hd, d_in), BF16), jax.ShapeDtypeStruct((N_CHIPS, hr, d), BF16)],
        scratch_shapes=[pltpu.SemaphoreType.DMA((2,)), pltpu.SemaphoreType.DMA((2,))],
    )(gw_in, gw_out)


def _pair_sum_in(gw_in, recv, sc):
    _, hd, d_in = gw_in.shape
    tr = min(hd, 256)
    c = lax.axis_index("c")

    def body(c_ref, a_ref, b_ref, o_ref):
        del c_ref
        o_ref[...] = (a_ref[...].astype(F32) + b_ref[...].astype(F32)).astype(BF16)

    blk = pl.BlockSpec((tr, sc), lambda i, j, cc: (i, j))
    gs = pltpu.PrefetchScalarGridSpec(
        num_scalar_prefetch=1, grid=(hd // tr, d_in // sc),
        in_specs=[pl.BlockSpec((None, tr, sc), lambda i, j, cc: (cc[0], i, j)), blk], out_specs=blk)
    return pl.pallas_call(
        body, name="pair_sum_in", grid_spec=gs, out_shape=jax.ShapeDtypeStruct((hd, d_in), BF16),
        compiler_params=_params(dimension_semantics=("parallel", "parallel")),
    )(jnp.reshape(c, (1,)).astype(jnp.int32), gw_in, recv)


def _pair_sum_out(gw_out, recv):
    _, n_chips, hr, d = gw_out.shape
    tr = min(hr, 256)
    c = lax.axis_index("c")

    def body(c_ref, a_ref, b_ref, o_ref):
        del c_ref
        o_ref[...] = (a_ref[...].astype(F32) + b_ref[...].astype(F32)).astype(BF16)

    blk = pl.BlockSpec((None, tr, d), lambda k, i, cc: (k, i, 0))
    gs = pltpu.PrefetchScalarGridSpec(
        num_scalar_prefetch=1, grid=(n_chips, hr // tr),
        in_specs=[pl.BlockSpec((None, None, tr, d), lambda k, i, cc: (cc[0], k, i, 0)), blk], out_specs=blk)
    return pl.pallas_call(
        body, name="pair_sum_out", grid_spec=gs, out_shape=jax.ShapeDtypeStruct((n_chips, hr, d), BF16),
        compiler_params=_params(dimension_semantics=("parallel", "parallel")),
    )(jnp.reshape(c, (1,)).astype(jnp.int32), gw_out, recv)


def _chip_exchange(p_in, p_out, sc):
    hd, d_in = p_in.shape
    _, hr, d = p_out.shape

    def body(pi_ref, po_ref, ri_ref, ro_ref, send_sems, recv_sems):
        x, y, c, chips = _mesh_pos()
        copies = []
        for j, (px, py) in enumerate(chips):
            chip = 2 * px + py
            copies.append(_remote(pi_ref.at[:, pl.ds(pl.multiple_of(chip * sc, LANES), sc)], ri_ref.at[j],
                                  send_sems.at[j], recv_sems.at[j], (px, py, c)))
            copies.append(_remote(po_ref.at[chip], ro_ref.at[j],
                                  send_sems.at[3 + j], recv_sems.at[3 + j], (px, py, c)))
        for cp in copies:
            cp.start()
        for cp in copies:
            cp.wait()

    return pl.pallas_call(
        body, name="chip_exchange",
        in_specs=[HBM_SPEC, HBM_SPEC], out_specs=[HBM_SPEC, HBM_SPEC],
        out_shape=[jax.ShapeDtypeStruct((3, hd, sc), BF16), jax.ShapeDtypeStruct((3, hr, d), BF16)],
        scratch_shapes=[pltpu.SemaphoreType.DMA((6,)), pltpu.SemaphoreType.DMA((6,))],
    )(p_in, p_out)


def _chip_sum_in(p_in, r_in, sc):
    hd = p_in.shape[0]
    tr = min(hd, 256)
    me = 2 * lax.axis_index("x") + lax.axis_index("y")

    def body(me_ref, p_ref, r_ref, o_ref):
        del me_ref
        acc = p_ref[...].astype(F32)
        for j in range(3):
            acc = acc + r_ref[j].astype(F32)
        o_ref[...] = acc

    gs = pltpu.PrefetchScalarGridSpec(
        num_scalar_prefetch=1, grid=(hd // tr,),
        in_specs=[pl.BlockSpec((tr, sc), lambda i, m: (i, m[0])), pl.BlockSpec((3, tr, sc), lambda i, m: (0, i, 0))],
        out_specs=pl.BlockSpec((tr, sc), lambda i, m: (i, 0)))
    return pl.pallas_call(
        body, name="chip_sum_in", grid_spec=gs, out_shape=jax.ShapeDtypeStruct((hd, sc), F32),
        compiler_params=_params(dimension_semantics=("parallel",)),
    )(jnp.reshape(me, (1,)).astype(jnp.int32), p_in, r_in)


def _chip_sum_out(p_out, r_out):
    _, hr, d = p_out.shape
    tr = min(hr, 256)
    me = 2 * lax.axis_index("x") + lax.axis_index("y")

    def body(me_ref, p_ref, r_ref, o_ref):
        del me_ref
        acc = p_ref[...].astype(F32)
        for j in range(3):
            acc = acc + r_ref[j].astype(F32)
        o_ref[...] = acc

    gs = pltpu.PrefetchScalarGridSpec(
        num_scalar_prefetch=1, grid=(hr // tr,),
        in_specs=[pl.BlockSpec((None, tr, d), lambda i, m: (m[0], i, 0)), pl.BlockSpec((3, tr, d), lambda i, m: (0, i, 0))],
        out_specs=pl.BlockSpec((tr, d), lambda i, m: (i, 0)))
    return pl.pallas_call(
        body, name="chip_sum_out", grid_spec=gs, out_shape=jax.ShapeDtypeStruct((hr, d), F32),
        compiler_params=_params(dimension_semantics=("parallel",)),
    )(jnp.reshape(me, (1,)).astype(jnp.int32), p_out, r_out)


def _share_halves(red_in, red_out):
    hd, sc = red_in.shape
    hr, d = red_out.shape

    def body(ri_ref, ro_ref, gi_ref, go_ref, send_sems, recv_sems, local_sems):
        x, y, c, _ = _mesh_pos()
        sibling = (x, y, 1 - c)
        mine_i = gi_ref.at[pl.ds(pl.multiple_of(c * hd, 8), hd), :]
        mine_o = go_ref.at[pl.ds(pl.multiple_of(c * hr, 8), hr), :]
        la = pltpu.make_async_copy(ri_ref, mine_i, local_sems.at[0])
        lb = pltpu.make_async_copy(ro_ref, mine_o, local_sems.at[1])
        la.start()
        lb.start()
        a = _remote(ri_ref, mine_i, send_sems.at[0], recv_sems.at[0], sibling)
        b = _remote(ro_ref, mine_o, send_sems.at[1], recv_sems.at[1], sibling)
        a.start()
        b.start()
        a.wait_send()
        b.wait_send()
        theirs_i = gi_ref.at[pl.ds(pl.multiple_of((1 - c) * hd, 8), hd), :]
        theirs_o = go_ref.at[pl.ds(pl.multiple_of((1 - c) * hr, 8), hr), :]
        _remote(theirs_i, theirs_i, send_sems.at[0], recv_sems.at[0], sibling).wait_recv()
        _remote(theirs_o, theirs_o, send_sems.at[1], recv_sems.at[1], sibling).wait_recv()
        la.wait()
        lb.wait()

    return pl.pallas_call(
        body, name="share_halves",
        in_specs=[HBM_SPEC, HBM_SPEC], out_specs=[HBM_SPEC, HBM_SPEC],
        out_shape=[jax.ShapeDtypeStruct((2 * hd, sc), F32), jax.ShapeDtypeStruct((2 * hr, d), F32)],
        scratch_shapes=[pltpu.SemaphoreType.DMA((2,)), pltpu.SemaphoreType.DMA((2,)), pltpu.SemaphoreType.DMA((2,))],
    )(red_in, red_out)


def _adamw_math(w, g, m, v):
    m_new = ADAM_B1 * m + (1.0 - ADAM_B1) * g
    v_new = ADAM_B2 * v + (1.0 - ADAM_B2) * (g * g)
    m_hat = m_new / (1.0 - ADAM_B1 ** ADAM_STEP)
    v_hat = v_new / (1.0 - ADAM_B2 ** ADAM_STEP)
    delta = -ADAM_LR * (m_hat / (jnp.sqrt(v_hat) + ADAM_EPS) + ADAM_WD * w)
    return delta, m_new, v_new


def _adamw(w, g, m, v, name):
    rows, cols = w.shape
    tr = min(rows, 128)

    def body(w_ref, g_ref, m_ref, v_ref, d_ref, mo_ref, vo_ref):
        delta, m_new, v_new = _adamw_math(w_ref[...], g_ref[...], m_ref[...], v_ref[...])
        d_ref[...] = delta
        mo_ref[...] = m_new
        vo_ref[...] = v_new

    spec = pl.BlockSpec((tr, cols), lambda i: (i, 0))
    shape = jax.ShapeDtypeStruct((rows, cols), F32)
    return pl.pallas_call(
        body, name=name, grid=(rows // tr,), in_specs=[spec] * 4, out_specs=[spec] * 3,
        out_shape=[shape] * 3, compiler_params=_params(dimension_semantics=("parallel",)),
    )(w, g, m, v)


SMALL_ROWS = 8


def _small_allreduce_adamw(part, w_pack, m_pack, v_pack):
    d = part.shape[1]

    def body(part_ref, w_ref, m_ref, v_ref, g_ref, d_ref, mo_ref, vo_ref, buf_ref, send_sems, recv_sems):
        x, y, c, _ = _mesh_pos()
        me = 4 * x + 2 * y + c
        buf_ref[0] = part_ref[...]
        copies = []
        for r in range(1, 8):
            rx, ry, rc = (r >> 2) & 1, (r >> 1) & 1, r & 1
            peer = (x ^ rx, y ^ ry, c ^ rc)
            copies.append(_remote(buf_ref.at[0], buf_ref.at[r], send_sems.at[r - 1], recv_sems.at[r - 1], peer))
        for cp in copies:
            cp.start()
        for cp in copies:
            cp.wait()
        total = buf_ref[me]
        for s in range(1, 8):
            total = total + buf_ref[s ^ me]
        w = w_ref[...]
        row = lax.broadcasted_iota(jnp.int32, (SMALL_ROWS, d), 0)
        l0, l1 = w[3:4], w[4:5]
        mx = jnp.maximum(l0, l1)
        e0, e1 = jnp.exp(l0 - mx), jnp.exp(l1 - mx)
        lb = e0 / (e0 + e1)
        g_l0 = total[3:4] * lb * (1.0 - lb)
        grads = jnp.where(row == 3, g_l0, jnp.where(row == 4, -g_l0, total))
        g_ref[...] = grads
        delta, m_new, v_new = _adamw_math(w, grads, m_ref[...], v_ref[...])
        d_ref[...] = delta
        mo_ref[...] = m_new
        vo_ref[...] = v_new

    vm = pl.BlockSpec(memory_space=pltpu.VMEM)
    shape = jax.ShapeDtypeStruct((SMALL_ROWS, d), F32)
    return pl.pallas_call(
        body, name="small_allreduce_adamw",
        in_specs=[vm] * 4, out_specs=[vm] * 4, out_shape=[shape] * 4,
        scratch_shapes=[pltpu.VMEM((8, SMALL_ROWS, d), F32), pltpu.SemaphoreType.DMA((7,)), pltpu.SemaphoreType.DMA((7,))],
    )(part, w_pack, m_pack, v_pack)


def _pack_small(d, pre, post, rnn, lb, sink, extra=None):
    rows = [pre, post, rnn, lb[0:1], lb[1:2],
            jnp.pad(sink, ((0, 0), (0, d - sink.shape[1]))),
            jnp.zeros((1, d), F32) if extra is None else extra,
            jnp.zeros((1, d), F32)]
    return jnp.concatenate(rows, axis=0)


def _unpack_small(p, n_sink):
    return dict(pre=p[0:1], post=p[1:2], rnn=p[2:3], lb=p[3:5], sink=p[5:6, :n_sink])


def kernel(x, w_in, attn_sinks, lb_logits, rnn_norm, w_out, pre_norm, post_norm, loss_target, m_w_in, m_attn_sinks, m_lb_logits, m_rnn_norm, m_w_out, m_pre_norm, m_post_norm, v_w_in, v_attn_sinks, v_lb_logits, v_rnn_norm, v_w_out, v_pre_norm, v_post_norm):
    t, d = x.shape[1], x.shape[2]
    sc = w_in.shape[2]
    n_sink = attn_sinks.shape[1]
    w_in2, w_out2 = w_in[0], w_out[0]

    w_in_full, w_out_full = _gather_weights(_cast_bf16(w_in2, "cast_w_in"), _cast_bf16(w_out2, "cast_w_out"))
    loss_part, grad_x, gw_in, gw_out, small = _local_grads(
        x[0], loss_target[0], w_in_full, w_out_full, attn_sinks, lb_logits, rnn_norm, pre_norm, post_norm)

    recv_in, recv_out = _pair_exchange(gw_in, gw_out)
    p_in = _pair_sum_in(gw_in, recv_in, sc)
    p_out = _pair_sum_out(gw_out, recv_out)
    r_in, r_out = _chip_exchange(p_in, p_out, sc)
    g_w_in, g_w_out = _share_halves(_chip_sum_in(p_in, r_in, sc), _chip_sum_out(p_out, r_out))

    d_w_in, nm_w_in, nv_w_in = _adamw(w_in2, g_w_in, m_w_in[0], v_w_in[0], "adamw_w_in")
    d_w_out, nm_w_out, nv_w_out = _adamw(w_out2, g_w_out, m_w_out[0], v_w_out[0], "adamw_w_out")

    lb_part = jnp.concatenate([small["lb"], jnp.zeros_like(small["lb"])], axis=0)
    loss_row = jnp.pad(loss_part[:, :1], ((0, 0), (0, d - 1)))
    part = _pack_small(d, small["pre"], small["post"], small["rnn"], lb_part, small["sink"], loss_row)
    w_pack = _pack_small(d, pre_norm, post_norm, rnn_norm, lb_logits, attn_sinks)
    m_pack = _pack_small(d, m_pre_norm, m_post_norm, m_rnn_norm, m_lb_logits, m_attn_sinks)
    v_pack = _pack_small(d, v_pre_norm, v_post_norm, v_rnn_norm, v_lb_logits, v_attn_sinks)
    g_pack, d_pack, nm_pack, nv_pack = _small_allreduce_adamw(part, w_pack, m_pack, v_pack)
    loss = g_pack[6, 0]
    g, dl, nm, nv = (_unpack_small(p, n_sink) for p in (g_pack, d_pack, nm_pack, nv_pack))

    def ordered(w_in_leaf, w_out_leaf, s):
        return (w_in_leaf[None], s["sink"], s["lb"], s["rnn"], w_out_leaf[None], s["pre"], s["post"])

    return (loss, grad_x[None],
            *ordered(g_w_in, g_w_out, g), *ordered(d_w_in, d_w_out, dl),
            *ordered(nm_w_in, nm_w_out, nm), *ordered(nv_w_in, nv_w_out, nv))
```

```python
import numpy as np
import jax
import jax.numpy as jnp
from jax import lax
from jax.experimental import pallas as pl
from jax.experimental.pallas import tpu as pltpu

F32 = jnp.float32
BF16 = jnp.bfloat16
MESH = pl.DeviceIdType.MESH

NORM_EPS = 1e-6
ATTN_HEAD_DIM = 64
GQA_GROUP = 8
WINDOW = 128
RNN_HEAD_DIM = 128
CHUNK = 64
HALF_CHUNK = CHUNK // 2
ATTN_SCALE = ATTN_HEAD_DIM ** -0.5

ADAM_LR = 0.001
ADAM_B1 = 0.9
ADAM_B2 = 0.999
ADAM_EPS = 1e-08
ADAM_WD = 0.01
ADAM_STEP = 10

LANES = 128
COL_TILE = 512
RNN_GROUP_HEADS = 4
ATTN_GROUP_LANES = 1024
N_CHIPS = 4
VMEM_LIMIT_BYTES = 56 * 1024 * 1024
NEG_BIG = -1e30


def _params(**kw):
    return pltpu.CompilerParams(vmem_limit_bytes=VMEM_LIMIT_BYTES, **kw)


def _sigmoid(x):
    return 1.0 / (1.0 + jnp.exp(-x))


def _dot(a, b):
    return jnp.dot(a, b, preferred_element_type=F32)


def _dot_nt(a, b):
    return lax.dot_general(a, b, (((1,), (1,)), ((), ())), preferred_element_type=F32)


def _dot_tn(a, b):
    return lax.dot_general(a, b, (((0,), (0,)), ((), ())), preferred_element_type=F32)


def _split3(x):
    hi = x.astype(BF16)
    r1 = x - hi.astype(F32)
    mid = r1.astype(BF16)
    lo = (r1 - mid.astype(F32)).astype(BF16)
    return hi, mid, lo


def _tri_dot(tri_bf16, x):
    hi, mid, lo = _split3(x)
    return _dot(tri_bf16, hi) + _dot(tri_bf16, mid) + _dot(tri_bf16, lo)


def _layout(d_model):
    d = d_model
    dkv = d // GQA_GROUP
    orig = dict(aq=0, ak=d, av=d + dkv, ag=d + 2 * dkv)
    base = d + 2 * dkv + d
    orig.update(rq=base, rf=base + d, ri=base + 2 * d, rg=base + 3 * d)
    group_w = RNN_GROUP_HEADS * RNN_HEAD_DIM
    cols = []
    for hg in range(d // group_w):
        for seg in ("rq", "rf", "ri", "rg"):
            cols.append((orig[seg] + hg * group_w, group_w))
    for m in range(d // ATTN_GROUP_LANES):
        for seg in ("aq", "ag"):
            cols.append((orig[seg] + m * ATTN_GROUP_LANES, ATTN_GROUP_LANES))
    cols.append((orig["ak"], dkv))
    cols.append((orig["av"], dkv))
    units = []
    for start, width in cols:
        assert start % LANES == 0 and width % LANES == 0
        units += [start + u for u in range(0, width, LANES)]
    per = COL_TILE // LANES
    assert len(units) % per == 0
    tiles = []
    for t in range(len(units) // per):
        run = units[t * per:(t + 1) * per]
        assert run[0] % COL_TILE == 0 and all(run[i] == run[0] + i * LANES for i in range(per))
        tiles.append(run[0] // COL_TILE)
    return dict(a_off=4 * d, k_off=6 * d, v_off=6 * d + dkv, total=6 * d + 2 * dkv,
                perm=np.asarray(tiles, np.int32))


def _chip_index():
    return jnp.reshape(2 * lax.axis_index("x") + lax.axis_index("y"), (1,)).astype(jnp.int32)


def _cast_into_gathered(a, name, axis):
    rows, cols = a.shape
    tr = min(rows, 512)
    nblk = rows // tr

    def body(me_ref, a_ref, o_ref):
        del me_ref
        o_ref[...] = a_ref[...].astype(BF16)

    if axis == 1:
        out_spec = pl.BlockSpec((tr, cols), lambda i, me: (i, me[0]))
        shape = (rows, N_CHIPS * cols)
    else:
        out_spec = pl.BlockSpec((tr, cols), lambda i, me: (me[0] * nblk + i, 0))
        shape = (N_CHIPS * rows, cols)
    gs = pltpu.PrefetchScalarGridSpec(num_scalar_prefetch=1, grid=(nblk,),
                                      in_specs=[pl.BlockSpec((tr, cols), lambda i, me: (i, 0))], out_specs=out_spec)
    return pl.pallas_call(
        body, name=name, grid_spec=gs, out_shape=jax.ShapeDtypeStruct(shape, BF16),
        compiler_params=_params(dimension_semantics=("parallel",)),
    )(_chip_index(), a)


def _transpose_bf16(a, name):
    rows, cols = a.shape
    tr = min(rows, 256)
    tc = min(cols, 2048)

    def body(a_ref, o_ref):
        o_ref[...] = a_ref[...].astype(F32).T.astype(BF16)

    return pl.pallas_call(
        body, name=name, grid=(rows // tr, cols // tc),
        in_specs=[pl.BlockSpec((tr, tc), lambda i, j: (i, j))],
        out_specs=pl.BlockSpec((tc, tr), lambda i, j: (j, i)),
        out_shape=jax.ShapeDtypeStruct((cols, rows), BF16),
        compiler_params=_params(dimension_semantics=("parallel", "parallel")),
    )(a)


def _prenorm_fwd(x, gain):
    t, d = x.shape
    tm = min(t, 256)

    def body(x_ref, g_ref, h_ref, ht_ref):
        xv = x_ref[...]
        r = lax.rsqrt(jnp.mean(xv * xv, axis=-1, keepdims=True) + NORM_EPS)
        h = (xv * r) * g_ref[...]
        h_ref[...] = h.astype(BF16)
        ht_ref[...] = h.T.astype(BF16)

    return pl.pallas_call(
        body, name="prenorm_fwd", grid=(t // tm,),
        in_specs=[pl.BlockSpec((tm, d), lambda i: (i, 0)), pl.BlockSpec((1, d), lambda i: (0, 0))],
        out_specs=[pl.BlockSpec((tm, d), lambda i: (i, 0)), pl.BlockSpec((d, tm), lambda i: (0, i))],
        out_shape=[jax.ShapeDtypeStruct((t, d), BF16), jax.ShapeDtypeStruct((d, t), BF16)],
        compiler_params=_params(dimension_semantics=("parallel",)),
    )(x, gain)


def _post_loss(x, y, target, gain):
    t, d = x.shape
    tm = min(t, 256)
    inv_d = 1.0 / d

    def body(x_ref, y_ref, t_ref, g_ref, dy_ref, dz_ref, gp_ref, loss_ref):
        i = pl.program_id(0)
        yv = y_ref[...]
        gain_v = g_ref[...]
        r = lax.rsqrt(jnp.mean(yv * yv, axis=-1, keepdims=True) + NORM_EPS)
        n = yv * r
        e = (x_ref[...] + n * gain_v) - t_ref[...]
        dz = e * inv_d
        dn = dz * gain_v
        dy = r * (dn - n * jnp.mean(dn * n, axis=-1, keepdims=True))
        dy_ref[...] = dy.astype(BF16)
        dz_ref[...] = dz

        @pl.when(i == 0)
        def _():
            gp_ref[...] = jnp.zeros_like(gp_ref)
            loss_ref[...] = jnp.zeros_like(loss_ref)

        gp_ref[...] += jnp.sum(dz * n, axis=0, keepdims=True)
        row = jnp.sum(e * e, axis=-1, keepdims=True)
        loss_ref[...] += jnp.full(loss_ref.shape, 0.5 * inv_d * jnp.sum(row), F32)

    row_spec = pl.BlockSpec((tm, d), lambda i: (i, 0))
    vec_spec = pl.BlockSpec((1, d), lambda i: (0, 0))
    return pl.pallas_call(
        body, name="post_loss", grid=(t // tm,),
        in_specs=[row_spec, row_spec, row_spec, vec_spec],
        out_specs=[row_spec, row_spec, vec_spec, pl.BlockSpec((1, LANES), lambda i: (0, 0))],
        out_shape=[jax.ShapeDtypeStruct((t, d), BF16), jax.ShapeDtypeStruct((t, d), F32),
                   jax.ShapeDtypeStruct((1, d), F32), jax.ShapeDtypeStruct((1, LANES), F32)],
        compiler_params=_params(dimension_semantics=("arbitrary",)),
    )(x, y, target, gain)


def _prenorm_bwd(x, dh, dz, gain):
    t, d = x.shape
    tm = min(t, 256)

    def body(x_ref, dh_ref, dz_ref, g_ref, gx_ref, gp_ref):
        i = pl.program_id(0)
        xv = x_ref[...]
        r = lax.rsqrt(jnp.mean(xv * xv, axis=-1, keepdims=True) + NORM_EPS)
        n = xv * r
        dhv = dh_ref[...]
        dn = dhv * g_ref[...]
        gx_ref[...] = dz_ref[...] + r * (dn - n * jnp.mean(dn * n, axis=-1, keepdims=True))

        @pl.when(i == 0)
        def _():
            gp_ref[...] = jnp.zeros_like(gp_ref)

        gp_ref[...] += jnp.sum(dhv * n, axis=0, keepdims=True)

    row_spec = pl.BlockSpec((tm, d), lambda i: (i, 0))
    vec_spec = pl.BlockSpec((1, d), lambda i: (0, 0))
    return pl.pallas_call(
        body, name="prenorm_bwd", grid=(t // tm,),
        in_specs=[row_spec, row_spec, row_spec, vec_spec],
        out_specs=[row_spec, vec_spec],
        out_shape=[jax.ShapeDtypeStruct((t, d), F32), jax.ShapeDtypeStruct((1, d), F32)],
        compiler_params=_params(dimension_semantics=("arbitrary",)),
    )(x, dh, dz, gain)


def _matmul(name, a, b, *, out_shape, grid, a_spec, b_spec, o_spec, nt=False, perm=None):
    nk = grid[2]
    tm, tn = [s for s in o_spec.block_shape if s is not None][-2:]
    acc_in_out = out_shape.dtype == F32

    def body(*refs):
        if perm is not None:
            refs = refs[1:]
        a_ref, b_ref, o_ref = refs[:3]
        part = _dot_nt(a_ref[...], b_ref[...]) if nt else _dot(a_ref[...], b_ref[...])
        if nk == 1:
            o_ref[...] = part.astype(o_ref.dtype)
            return
        acc_ref = o_ref if acc_in_out else refs[3]
        k = pl.program_id(2)

        @pl.when(k == 0)
        def _():
            acc_ref[...] = part

        @pl.when(k > 0)
        def _():
            acc_ref[...] += part

        if not acc_in_out:
            @pl.when(k == nk - 1)
            def _():
                o_ref[...] = acc_ref[...].astype(o_ref.dtype)

    scratch = [] if (nk == 1 or acc_in_out) else [pltpu.VMEM((tm, tn), F32)]
    cp = _params(dimension_semantics=("parallel", "parallel", "arbitrary"))
    if perm is None:
        return pl.pallas_call(body, name=name, grid=grid, in_specs=[a_spec, b_spec], out_specs=o_spec,
                              out_shape=out_shape, scratch_shapes=scratch, compiler_params=cp)(a, b)
    gs = pltpu.PrefetchScalarGridSpec(num_scalar_prefetch=1, grid=grid, in_specs=[a_spec, b_spec],
                                      out_specs=o_spec, scratch_shapes=scratch)
    return pl.pallas_call(body, name=name, grid_spec=gs, out_shape=out_shape,
                          compiler_params=cp)(jnp.asarray(perm), a, b)


def _proj_mm(h, w_full, perm):
    t, d = h.shape
    n_tiles = len(perm)
    tm = min(t, 1024)
    return _matmul(
        "proj_mm", h, w_full, perm=perm, grid=(t // tm, n_tiles, 1),
        out_shape=jax.ShapeDtypeStruct((t, n_tiles * COL_TILE), F32),
        a_spec=pl.BlockSpec((tm, d), lambda i, j, k, p: (i, 0)),
        b_spec=pl.BlockSpec((d, COL_TILE), lambda i, j, k, p: (0, p[j])),
        o_spec=pl.BlockSpec((tm, COL_TILE), lambda i, j, k, p: (i, j)))


def _gw_in_mm(ht, dproj, perm):
    d, t = ht.shape
    n_tiles = len(perm)
    hd = d // 2
    tm = min(hd, 1024)
    per_half = hd // tm
    return _matmul(
        "gw_in_mm", ht, dproj, perm=perm, grid=(d // tm, n_tiles, 1),
        out_shape=jax.ShapeDtypeStruct((2, hd, n_tiles * COL_TILE), BF16),
        a_spec=pl.BlockSpec((tm, t), lambda i, j, k, p: (i, 0)),
        b_spec=pl.BlockSpec((t, COL_TILE), lambda i, j, k, p: (0, j)),
        o_spec=pl.BlockSpec((None, tm, COL_TILE), lambda i, j, k, p: (i // per_half, i % per_half, p[j])))


def _dh_mm(dproj, w_full, perm):
    t = dproj.shape[0]
    d = w_full.shape[0]
    n_tiles = len(perm)
    tm = min(t, 2048)
    tn = min(d, 2048)
    return _matmul(
        "dh_mm", dproj, w_full, perm=perm, nt=True, grid=(t // tm, d // tn, n_tiles),
        out_shape=jax.ShapeDtypeStruct((t, d), F32),
        a_spec=pl.BlockSpec((tm, COL_TILE), lambda i, j, k, p: (i, k)),
        b_spec=pl.BlockSpec((tn, COL_TILE), lambda i, j, k, p: (j, p[k])),
        o_spec=pl.BlockSpec((tm, tn), lambda i, j, k, p: (i, j)))


def _out_mm(mixed, w_out_full):
    t, dm = mixed.shape
    d = w_out_full.shape[1]
    tm = min(t, 1024)
    tn = min(d, 512)
    tk = min(dm, 4096)
    return _matmul(
        "out_mm", mixed, w_out_full, grid=(t // tm, d // tn, dm // tk),
        out_shape=jax.ShapeDtypeStruct((t, d), F32),
        a_spec=pl.BlockSpec((tm, tk), lambda i, j, k: (i, k)),
        b_spec=pl.BlockSpec((tk, tn), lambda i, j, k: (k, j)),
        o_spec=pl.BlockSpec((tm, tn), lambda i, j, k: (i, j)))


def _dmixed_mm(dy, w_out_full):
    t, d = dy.shape
    dm = w_out_full.shape[0]
    tm = min(t, 1024)
    tn = min(dm, 1024)
    return _matmul(
        "dmixed_mm", dy, w_out_full, nt=True, grid=(t // tm, dm // tn, 1),
        out_shape=jax.ShapeDtypeStruct((t, dm), F32),
        a_spec=pl.BlockSpec((tm, d), lambda i, j, k: (i, 0)),
        b_spec=pl.BlockSpec((tn, d), lambda i, j, k: (j, 0)),
        o_spec=pl.BlockSpec((tm, tn), lambda i, j, k: (i, j)))


def _gw_out_mm(mixed_t, dy):
    dm, t = mixed_t.shape
    d = dy.shape[1]
    hr = dm // (2 * N_CHIPS)
    tn = min(d, 1024)
    return _matmul(
        "gw_out_mm", mixed_t, dy, grid=(dm // hr, d // tn, 1),
        out_shape=jax.ShapeDtypeStruct((2, N_CHIPS, hr, d), BF16),
        a_spec=pl.BlockSpec((hr, t), lambda i, j, k: (i, 0)),
        b_spec=pl.BlockSpec((t, tn), lambda i, j, k: (0, j)),
        o_spec=pl.BlockSpec((None, None, hr, tn), lambda i, j, k: (i % 2, i // 2, 0, j)))


def _lane_half():
    return lax.broadcasted_iota(jnp.int32, (WINDOW, LANES), 1) // ATTN_HEAD_DIM


def _dup_kv(tile, kh):
    return jnp.where(_lane_half() == kh, tile, pltpu.roll(tile, ATTN_HEAD_DIM, 1))


def _stack_heads(tiles, kh):
    half = _lane_half()
    pieces = []
    for g in range(GQA_GROUP):
        pieces.append(jnp.where(half == g % 2, tiles[4 * kh + g // 2], 0.0))
    return jnp.concatenate(pieces, axis=0)


def _unstack_heads(stacked):
    half = _lane_half()
    out = []
    for j in range(GQA_GROUP // 2):
        a = stacked[(2 * j) * WINDOW:(2 * j + 1) * WINDOW]
        b = stacked[(2 * j + 1) * WINDOW:(2 * j + 2) * WINDOW]
        out.append(jnp.where(half == 0, a, b))
    return out


def _attn_probs(qs, kcat, sink_col, n):
    rows = GQA_GROUP * WINDOW
    s = _dot_nt(qs, kcat)
    qi = lax.broadcasted_iota(jnp.int32, (rows, 2 * WINDOW), 0) % WINDOW
    kj = lax.broadcasted_iota(jnp.int32, (rows, 2 * WINDOW), 1)
    first_key = WINDOW * (1 - jnp.minimum(n, 1))
    valid = (kj > qi) & (kj <= qi + WINDOW) & (kj >= first_key)
    s = jnp.where(valid, s, NEG_BIG)
    mx = jnp.maximum(jnp.max(s, axis=-1, keepdims=True), sink_col)
    p = jnp.where(valid, jnp.exp(s - mx), 0.0)
    p_sink = jnp.exp(sink_col - mx)
    inv = 1.0 / (jnp.sum(p, axis=-1, keepdims=True) + p_sink)
    return p * inv, p_sink * inv


def _attn_operands(sink_ref, q_tiles, kp_ref, kc_ref, vp_ref, vc_ref, m, kh):
    qs = _stack_heads([qt * ATTN_SCALE for qt in q_tiles], kh).astype(BF16)
    kcat = jnp.concatenate([_dup_kv(kp_ref[...], kh), _dup_kv(kc_ref[...], kh)], axis=0).astype(BF16)
    vcat = jnp.concatenate([_dup_kv(vp_ref[...], kh), _dup_kv(vc_ref[...], kh)], axis=0).astype(BF16)
    heads_per_group = ATTN_GROUP_LANES // ATTN_HEAD_DIM
    sink_col = jnp.concatenate(
        [jnp.full((WINDOW, 1), sink_ref[0, m * heads_per_group + kh * GQA_GROUP + g], F32)
         for g in range(GQA_GROUP)], axis=0)
    return qs, kcat, vcat, sink_col


def _attn_specs(lay, d):
    a_blk = lay["a_off"] // (2 * ATTN_GROUP_LANES)
    k_blk = lay["k_off"] // LANES
    v_blk = lay["v_off"] // LANES
    qg = pl.BlockSpec((WINDOW, 2 * ATTN_GROUP_LANES), lambda m, n: (n, a_blk + m))
    kp = pl.BlockSpec((WINDOW, LANES), lambda m, n: (jnp.maximum(n - 1, 0), k_blk + m))
    kc = pl.BlockSpec((WINDOW, LANES), lambda m, n: (n, k_blk + m))
    vp = pl.BlockSpec((WINDOW, LANES), lambda m, n: (jnp.maximum(n - 1, 0), v_blk + m))
    vc = pl.BlockSpec((WINDOW, LANES), lambda m, n: (n, v_blk + m))
    return qg, kp, kc, vp, vc


def _attn_fwd(proj, sinks, lay, d):
    t = proj.shape[0]
    n_groups = d // ATTN_GROUP_LANES
    pairs = ATTN_GROUP_LANES // LANES

    def body(sink_ref, qg_ref, kp_ref, kc_ref, vp_ref, vc_ref, mix_ref, o_ref):
        m = pl.program_id(0)
        n = pl.program_id(1)
        q_tiles = [qg_ref[:, p * LANES:(p + 1) * LANES] for p in range(pairs)]
        for kh in range(2):
            qs, kcat, vcat, sink_col = _attn_operands(sink_ref, q_tiles, kp_ref, kc_ref, vp_ref, vc_ref, m, kh)
            probs, _ = _attn_probs(qs, kcat, sink_col, n)
            out = _dot(probs.astype(BF16), vcat)
            for j, tile in enumerate(_unstack_heads(out)):
                p = 4 * kh + j
                lanes = slice(p * LANES, (p + 1) * LANES)
                gate = qg_ref[:, ATTN_GROUP_LANES + p * LANES:ATTN_GROUP_LANES + (p + 1) * LANES]
                o_ref[:, lanes] = tile
                mix_ref[:, lanes] = (tile * (gate * _sigmoid(gate))).astype(BF16)

    qg, kp, kc, vp, vc = _attn_specs(lay, d)
    out_blk = pl.BlockSpec((WINDOW, ATTN_GROUP_LANES), lambda m, n: (n, m))
    return pl.pallas_call(
        body, name="attn_fwd", grid=(n_groups, t // WINDOW),
        in_specs=[pl.BlockSpec(memory_space=pltpu.SMEM), qg, kp, kc, vp, vc],
        out_specs=[out_blk, out_blk],
        out_shape=[jax.ShapeDtypeStruct((t, 2 * d), BF16), jax.ShapeDtypeStruct((t, d), F32)],
        compiler_params=_params(dimension_semantics=("parallel", "parallel")),
    )(sinks, proj, proj, proj, proj, proj)


def _attn_bwd(proj, sinks, attn_o, dmixed, dproj, lay, d):
    t = proj.shape[0]
    n_groups = d // ATTN_GROUP_LANES
    pairs = ATTN_GROUP_LANES // LANES
    kv_w = n_groups * LANES

    def body(sink_ref, qg_ref, kp_ref, kc_ref, vp_ref, vc_ref, o_ref, dm_ref, dproj_hbm,
             dqg_ref, dkc_ref, dkp_ref, dvc_ref, dvp_ref, dsink_ref):
        del dproj_hbm
        m = pl.program_id(0)
        n = pl.program_id(1)
        half = _lane_half()
        q_tiles = [qg_ref[:, p * LANES:(p + 1) * LANES] for p in range(pairs)]
        do_tiles, o_tiles = [], []
        for p in range(pairs):
            lanes = slice(p * LANES, (p + 1) * LANES)
            gate = qg_ref[:, ATTN_GROUP_LANES + p * LANES:ATTN_GROUP_LANES + (p + 1) * LANES]
            sg = _sigmoid(gate)
            dmix = dm_ref[:, lanes]
            ov = o_ref[:, lanes]
            dqg_ref[:, ATTN_GROUP_LANES + p * LANES:ATTN_GROUP_LANES + (p + 1) * LANES] = (
                dmix * ov * (sg * (1.0 + gate * (1.0 - sg)))).astype(BF16)
            do_tiles.append(dmix * (gate * sg))
            o_tiles.append(ov)

        sub = lax.broadcasted_iota(jnp.int32, (8, LANES), 0)
        lane = lax.broadcasted_iota(jnp.int32, (8, LANES), 1)
        dsink = jnp.zeros((8, LANES), F32)
        dk_cur = dk_prev = dv_cur = dv_prev = jnp.zeros((WINDOW, LANES), F32)
        for kh in range(2):
            qs, kcat, vcat, sink_col = _attn_operands(sink_ref, q_tiles, kp_ref, kc_ref, vp_ref, vc_ref, m, kh)
            probs, p_sink = _attn_probs(qs, kcat, sink_col, n)
            dos = _stack_heads(do_tiles, kh)
            delta = jnp.sum(dos * _stack_heads(o_tiles, kh), axis=-1, keepdims=True)
            dos = dos.astype(BF16)
            dp = _dot_nt(dos, vcat)
            ds = (probs * (dp - delta)).astype(BF16)
            dv = _dot_tn(probs.astype(BF16), dos)
            dv = dv + pltpu.roll(dv, ATTN_HEAD_DIM, 1)
            dk = _dot_tn(ds, qs)
            dk = dk + pltpu.roll(dk, ATTN_HEAD_DIM, 1)
            dq = _dot(ds, kcat)
            for j, tile in enumerate(_unstack_heads(dq)):
                p = 4 * kh + j
                dqg_ref[:, p * LANES:(p + 1) * LANES] = (tile * ATTN_SCALE).astype(BF16)
            dk_prev = jnp.where(half == kh, dk[:WINDOW], dk_prev)
            dk_cur = jnp.where(half == kh, dk[WINDOW:], dk_cur)
            dv_prev = jnp.where(half == kh, dv[:WINDOW], dv_prev)
            dv_cur = jnp.where(half == kh, dv[WINDOW:], dv_cur)
            sink_terms = p_sink * delta
            for g in range(GQA_GROUP):
                val = -jnp.sum(sink_terms[g * WINDOW:(g + 1) * WINDOW])
                dsink = dsink + jnp.where((sub == 0) & (lane == kh * GQA_GROUP + g), val, 0.0)
        dkc_ref[...] = dk_cur
        dkp_ref[...] = dk_prev
        dvc_ref[...] = dv_cur
        dvp_ref[...] = dv_prev

        @pl.when(n == 0)
        def _():
            dsink_ref[...] = jnp.zeros_like(dsink_ref)

        dsink_ref[...] += dsink

    qg, kp, kc, vp, vc = _attn_specs(lay, d)
    a_blk = lay["a_off"] // (2 * ATTN_GROUP_LANES)
    grp = pl.BlockSpec((WINDOW, ATTN_GROUP_LANES), lambda m, n: (n, m))
    kv_blk = pl.BlockSpec((WINDOW, LANES), lambda m, n: (n, m))
    kv_shape = jax.ShapeDtypeStruct((t, kv_w), F32)
    outs = pl.pallas_call(
        body, name="attn_bwd", grid=(n_groups, t // WINDOW),
        in_specs=[pl.BlockSpec(memory_space=pltpu.SMEM), qg, kp, kc, vp, vc, grp, grp,
                  pl.BlockSpec(memory_space=pl.ANY)],
        out_specs=[pl.BlockSpec((WINDOW, 2 * ATTN_GROUP_LANES), lambda m, n: (n, a_blk + m)),
                   kv_blk, kv_blk, kv_blk, kv_blk, pl.BlockSpec((8, LANES), lambda m, n: (m, 0))],
        out_shape=[jax.ShapeDtypeStruct(dproj.shape, BF16), kv_shape, kv_shape, kv_shape, kv_shape,
                   jax.ShapeDtypeStruct((n_groups * 8, LANES), F32)],
        input_output_aliases={8: 0},
        compiler_params=_params(dimension_semantics=("parallel", "arbitrary")),
    )(sinks, proj, proj, proj, proj, proj, attn_o, dmixed, dproj)
    return outs


def _kv_combine(dkc, dkp, dvc, dvp, dproj, lay):
    t, kv_w = dkc.shape
    nb = t // WINDOW
    kv_blk_idx = lay["k_off"] // (2 * kv_w)

    def body(dkc_ref, dkp_ref, dvc_ref, dvp_ref, dproj_hbm, o_ref):
        del dproj_hbm
        keep = (pl.program_id(0) < nb - 1).astype(F32)
        o_ref[:, :kv_w] = (dkc_ref[...] + keep * dkp_ref[...]).astype(BF16)
        o_ref[:, kv_w:] = (dvc_ref[...] + keep * dvp_ref[...]).astype(BF16)

    cur = pl.BlockSpec((WINDOW, kv_w), lambda n: (n, 0))
    nxt = pl.BlockSpec((WINDOW, kv_w), lambda n: (jnp.minimum(n + 1, nb - 1), 0))
    return pl.pallas_call(
        body, name="kv_combine", grid=(nb,),
        in_specs=[cur, nxt, cur, nxt, pl.BlockSpec(memory_space=pl.ANY)],
        out_specs=pl.BlockSpec((WINDOW, 2 * kv_w), lambda n: (n, kv_blk_idx)),
        out_shape=jax.ShapeDtypeStruct(dproj.shape, BF16),
        input_output_aliases={4: 0},
        compiler_params=_params(dimension_semantics=("parallel",)),
    )(dkc, dkp, dvc, dvp, dproj)


def _lower_bound(lbl_ref):
    l0 = lbl_ref[0:1, :]
    l1 = lbl_ref[1:2, :]
    mx = jnp.maximum(l0, l1)
    e0 = jnp.exp(l0 - mx)
    e1 = jnp.exp(l1 - mx)
    return e0 / (e0 + e1)


def _chunk_masks():
    ti = lax.broadcasted_iota(jnp.int32, (CHUNK, CHUNK), 0)
    si = lax.broadcasted_iota(jnp.int32, (CHUNK, CHUNK), 1)
    diag = ((ti // HALF_CHUNK) == (si // HALF_CHUNK)) & (si <= ti)
    off = (ti >= HALF_CHUNK) & (si < HALF_CHUNK)
    lower = (si <= ti).astype(BF16)
    upper = (si >= ti).astype(BF16)
    return diag, off, lower, upper


def _rnn_gates(rq, rf, lb):
    sf = _sigmoid(rf)
    f = lb + (1.0 - lb) * sf
    sq = _sigmoid(rq)
    return sf, f, jnp.log(f), 1.0 - f, sq, rq * sq


def _rnn_decays(g_cum):
    row = lax.broadcasted_iota(jnp.int32, g_cum.shape, 0)
    ref_d = jnp.where(row < HALF_CHUNK, g_cum[HALF_CHUNK // 2 - 1:HALF_CHUNK // 2],
                      g_cum[HALF_CHUNK + HALF_CHUNK // 2 - 1:HALF_CHUNK + HALF_CHUNK // 2])
    ref_o = g_cum[HALF_CHUNK - 1:HALF_CHUNK]
    last = g_cum[CHUNK - 1:CHUNK]
    return dict(eq_d=jnp.exp(g_cum - ref_d), ek_d=jnp.exp(ref_d - g_cum),
                eq_o=jnp.exp(jnp.minimum(g_cum - ref_o, 0.0)), ek_o=jnp.exp(jnp.minimum(ref_o - g_cum, 0.0)),
                eg=jnp.exp(g_cum), ekl=jnp.exp(last - g_cum), e_last=jnp.exp(last))


def _head(a, j):
    return a[:, j * RNN_HEAD_DIM:(j + 1) * RNN_HEAD_DIM]


def _rnn_specs(t, tb, d):
    gw = RNN_GROUP_HEADS * RNN_HEAD_DIM
    return gw, t // tb, tb // CHUNK


def _rnn_fwd(proj, lb_logits, rnn_gain, mixed, d):
    t = proj.shape[0]
    tb = min(t, 256)
    gw, ntb, nch = _rnn_specs(t, tb, d)
    n_groups = d // gw
    n_heads = d // RNN_HEAD_DIM

    def body(blk_ref, lbl_ref, gain_ref, mixed_hbm, mix_ref, o_ref, st_out_ref, st_ref):
        del mixed_hbm

        @pl.when(pl.program_id(1) == 0)
        def _():
            st_ref[...] = jnp.zeros_like(st_ref)

        lb = _lower_bound(lbl_ref)
        gain = gain_ref[...]
        diag, off, lower, _ = _chunk_masks()

        def chunk(c, carry):
            rows = pl.ds(pl.multiple_of(c * CHUNK, CHUNK), CHUNK)
            rq = blk_ref[rows, 0:gw]
            rf = blk_ref[rows, gw:2 * gw]
            v = blk_ref[rows, 2 * gw:3 * gw]
            rg = blk_ref[rows, 3 * gw:4 * gw]
            _, _, g, k, _, q = _rnn_gates(rq, rf, lb)
            dec = _rnn_decays(_tri_dot(lower, g))
            qd = (q * dec["eq_d"]).astype(BF16)
            kd = (k * dec["ek_d"]).astype(BF16)
            qo = (q * dec["eq_o"]).astype(BF16)
            ko = (k * dec["ek_o"]).astype(BF16)
            qe = (q * dec["eg"]).astype(BF16)
            kl = (k * dec["ekl"]).astype(BF16)
            vb = v.astype(BF16)
            outs = []
            for j in range(RNN_GROUP_HEADS):
                st = st_ref[j]
                st_out_ref[j, c] = st
                attn = jnp.where(diag, _dot_nt(_head(qd, j), _head(kd, j)),
                                 jnp.where(off, _dot_nt(_head(qo, j), _head(ko, j)), 0.0))
                o = _dot(attn.astype(BF16), _head(vb, j)) + _dot_nt(_head(qe, j), st.astype(BF16))
                st_ref[j] = st * _head(dec["e_last"], j) + _dot_tn(_head(vb, j), _head(kl, j))
                rr = lax.rsqrt(jnp.mean(o * o, axis=-1, keepdims=True) + NORM_EPS)
                o_ref[rows, j * RNN_HEAD_DIM:(j + 1) * RNN_HEAD_DIM] = o
                outs.append(o * rr)
            on = jnp.concatenate(outs, axis=1) * gain
            mix_ref[rows, :] = (on * (rg * _sigmoid(rg))).astype(BF16)
            return carry

        lax.fori_loop(0, nch, chunk, 0)

    return pl.pallas_call(
        body, name="rnn_fwd", grid=(n_groups, ntb),
        in_specs=[pl.BlockSpec((tb, 4 * gw), lambda h, i: (i, h)),
                  pl.BlockSpec((2, gw), lambda h, i: (0, h)),
                  pl.BlockSpec((1, gw), lambda h, i: (0, h)),
                  pl.BlockSpec(memory_space=pl.ANY)],
        out_specs=[pl.BlockSpec((tb, gw), lambda h, i: (i, d // gw + h)),
                   pl.BlockSpec((tb, gw), lambda h, i: (i, h)),
                   pl.BlockSpec((RNN_GROUP_HEADS, nch, RNN_HEAD_DIM, RNN_HEAD_DIM), lambda h, i: (h, i, 0, 0))],
        out_shape=[jax.ShapeDtypeStruct(mixed.shape, BF16), jax.ShapeDtypeStruct((t, d), F32),
                   jax.ShapeDtypeStruct((n_heads, t // CHUNK, RNN_HEAD_DIM, RNN_HEAD_DIM), F32)],
        scratch_shapes=[pltpu.VMEM((RNN_GROUP_HEADS, RNN_HEAD_DIM, RNN_HEAD_DIM), F32)],
        input_output_aliases={3: 0},
        compiler_params=_params(dimension_semantics=("parallel", "arbitrary")),
    )(proj, lb_logits, rnn_gain, mixed)


def _rnn_bwd(proj, lb_logits, rnn_gain, o_pre, states, dmixed, d_total, d):
    t = proj.shape[0]
    tb = min(t, 256)
    gw, ntb, nch = _rnn_specs(t, tb, d)
    n_groups = d // gw

    def body(blk_ref, lbl_ref, gain_ref, o_ref, st_in_ref, dm_ref, dproj_ref, dgain_ref, dlb_ref, dst_ref):
        @pl.when(pl.program_id(1) == 0)
        def _():
            dst_ref[...] = jnp.zeros_like(dst_ref)
            dgain_ref[...] = jnp.zeros_like(dgain_ref)
            dlb_ref[...] = jnp.zeros_like(dlb_ref)

        lb = _lower_bound(lbl_ref)
        gain = gain_ref[...]
        diag, off, lower, upper = _chunk_masks()
        last_row = lax.broadcasted_iota(jnp.int32, (CHUNK, RNN_HEAD_DIM), 0) == CHUNK - 1

        def chunk(step, carry):
            c = nch - 1 - step
            rows = pl.ds(pl.multiple_of(c * CHUNK, CHUNK), CHUNK)
            rq = blk_ref[rows, 0:gw]
            rf = blk_ref[rows, gw:2 * gw]
            v = blk_ref[rows, 2 * gw:3 * gw]
            rg = blk_ref[rows, 3 * gw:4 * gw]
            sf, f, g, k, sq, q = _rnn_gates(rq, rf, lb)
            dec = _rnn_decays(_tri_dot(lower, g))
            qd = (q * dec["eq_d"]).astype(BF16)
            kd = (k * dec["ek_d"]).astype(BF16)
            qo = (q * dec["eq_o"]).astype(BF16)
            ko = (k * dec["ek_o"]).astype(BF16)
            qe = (q * dec["eg"]).astype(BF16)
            kl = (k * dec["ekl"]).astype(BF16)
            vb = v.astype(BF16)

            o = o_ref[rows, :]
            dmix = dm_ref[rows, :]
            sg = _sigmoid(rg)
            n_parts = []
            for j in range(RNN_GROUP_HEADS):
                oj = _head(o, j)
                n_parts.append(oj * lax.rsqrt(jnp.mean(oj * oj, axis=-1, keepdims=True) + NORM_EPS))
            nrm = jnp.concatenate(n_parts, axis=1)
            d_on = dmix * (rg * sg)
            d_rg = dmix * (nrm * gain) * (sg * (1.0 + rg * (1.0 - sg)))
            dgain_ref[...] += jnp.sum(d_on * nrm, axis=0, keepdims=True)
            dn = d_on * gain

            dq_parts, dk_parts, dv_parts, dg_parts = [], [], [], []
            for j in range(RNN_GROUP_HEADS):
                oj, nj, dnj = _head(o, j), _head(nrm, j), _head(dn, j)
                rr = lax.rsqrt(jnp.mean(oj * oj, axis=-1, keepdims=True) + NORM_EPS)
                do = (rr * (dnj - nj * jnp.mean(dnj * nj, axis=-1, keepdims=True))).astype(BF16)
                st = st_in_ref[j, c]
                dst = dst_ref[j]
                stb, dstb = st.astype(BF16), dst.astype(BF16)
                qdj, kdj, qoj, koj = _head(qd, j), _head(kd, j), _head(qo, j), _head(ko, j)
                attn = jnp.where(diag, _dot_nt(qdj, kdj), jnp.where(off, _dot_nt(qoj, koj), 0.0))
                dattn = _dot_nt(do, _head(vb, j))
                da_d = jnp.where(diag, dattn, 0.0).astype(BF16)
                da_o = jnp.where(off, dattn, 0.0).astype(BF16)
                dv = _dot_tn(attn.astype(BF16), do) + _dot_nt(_head(kl, j), dstb)
                dq_inter = _dot(do, stb) * _head(dec["eg"], j)
                dq_d, dq_o = _dot(da_d, kdj), _dot(da_o, koj)
                dq = dq_inter + dq_d * _head(dec["eq_d"], j) + dq_o * _head(dec["eq_o"], j)
                dk_inter = _dot(_head(vb, j), dstb) * _head(dec["ekl"], j)
                dk_d, dk_o = _dot_tn(da_d, qdj), _dot_tn(da_o, qoj)
                dk = dk_inter + dk_d * _head(dec["ek_d"], j) + dk_o * _head(dec["ek_o"], j)
                kj, qj = _head(k, j), _head(q, j)
                e_last = _head(dec["e_last"], j)
                extra = (jnp.sum(kj * dk_inter, axis=0, keepdims=True)
                         + e_last * jnp.sum(st * dst, axis=0, keepdims=True))
                dg_cum = (qj * dq_inter - kj * dk_inter
                          + (qdj.astype(F32) * dq_d + qoj.astype(F32) * dq_o)
                          - (kdj.astype(F32) * dk_d + koj.astype(F32) * dk_o))
                dg_parts.append(jnp.where(last_row, dg_cum + extra, dg_cum))
                dst_ref[j] = dst * e_last + _dot_tn(do, _head(qe, j))
                dq_parts.append(dq)
                dk_parts.append(dk)
                dv_parts.append(dv)

            dq = jnp.concatenate(dq_parts, axis=1)
            dk = jnp.concatenate(dk_parts, axis=1)
            dg = _tri_dot(upper, jnp.concatenate(dg_parts, axis=1))
            df = dg / f - dk
            dlb_ref[...] += jnp.sum(df * (1.0 - sf), axis=0, keepdims=True)
            d_rf = df * (1.0 - lb) * (sf * (1.0 - sf))
            d_rq = dq * (sq * (1.0 + rq * (1.0 - sq)))
            dproj_ref[rows, 0:gw] = d_rq.astype(BF16)
            dproj_ref[rows, gw:2 * gw] = d_rf.astype(BF16)
            dproj_ref[rows, 2 * gw:3 * gw] = jnp.concatenate(dv_parts, axis=1).astype(BF16)
            dproj_ref[rows, 3 * gw:4 * gw] = d_rg.astype(BF16)
            return carry

        lax.fori_loop(0, nch, chunk, 0)

    rev = lambda i: ntb - 1 - i
    vec = pl.BlockSpec((1, gw), lambda h, i: (0, h))
    return pl.pallas_call(
        body, name="rnn_bwd", grid=(n_groups, ntb),
        in_specs=[pl.BlockSpec((tb, 4 * gw), lambda h, i: (rev(i), h)),
                  pl.BlockSpec((2, gw), lambda h, i: (0, h)), vec,
                  pl.BlockSpec((tb, gw), lambda h, i: (rev(i), h)),
                  pl.BlockSpec((RNN_GROUP_HEADS, nch, RNN_HEAD_DIM, RNN_HEAD_DIM), lambda h, i: (h, rev(i), 0, 0)),
                  pl.BlockSpec((tb, gw), lambda h, i: (rev(i), d // gw + h))],
        out_specs=[pl.BlockSpec((tb, 4 * gw), lambda h, i: (rev(i), h)), vec, vec],
        out_shape=[jax.ShapeDtypeStruct((t, d_total), BF16), jax.ShapeDtypeStruct((1, d), F32),
                   jax.ShapeDtypeStruct((1, d), F32)],
        scratch_shapes=[pltpu.VMEM((RNN_GROUP_HEADS, RNN_HEAD_DIM, RNN_HEAD_DIM), F32)],
        compiler_params=_params(dimension_semantics=("parallel", "arbitrary")),
    )(proj, lb_logits, rnn_gain, o_pre, states, dmixed)


def _local_grads(x, target, w_in_full, w_out_full, sinks, lb_logits, rnn_gain, pre_gain, post_gain):
    t, d = x.shape
    lay = _layout(d)
    perm = lay["perm"]
    h, ht = _prenorm_fwd(x, pre_gain)
    proj = _proj_mm(h, w_in_full, perm)
    mixed, attn_o = _attn_fwd(proj, sinks, lay, d)
    mixed, o_pre, states = _rnn_fwd(proj, lb_logits, rnn_gain, mixed, d)
    y = _out_mm(mixed, w_out_full)
    dy, dz, g_post, loss = _post_loss(x, y, target, post_gain)
    dmixed = _dmixed_mm(dy, w_out_full)
    gw_out = _gw_out_mm(_transpose_bf16(mixed, "mixed_t"), dy)
    dproj, g_rnn, g_lb = _rnn_bwd(proj, lb_logits, rnn_gain, o_pre, states, dmixed, lay["total"], d)
    dproj, dkc, dkp, dvc, dvp, dsink = _attn_bwd(proj, sinks, attn_o, dmixed, dproj, lay, d)
    dproj = _kv_combine(dkc, dkp, dvc, dvp, dproj, lay)
    dh = _dh_mm(dproj, w_in_full, perm)
    gw_in = _gw_in_mm(ht, dproj, perm)
    grad_x, g_pre = _prenorm_bwd(x, dh, dz, pre_gain)
    heads_per_group = ATTN_GROUP_LANES // ATTN_HEAD_DIM
    g_sink = dsink.reshape(d // ATTN_GROUP_LANES, 8, LANES)[:, 0, :heads_per_group].reshape(1, -1)
    return loss, grad_x, gw_in, gw_out, dict(sink=g_sink, lb=g_lb, rnn=g_rnn, pre=g_pre, post=g_post)


def _mesh_pos():
    x, y, c = lax.axis_index("x"), lax.axis_index("y"), lax.axis_index("c")
    chips = [(1 - x, y), (x, 1 - y), (1 - x, 1 - y)]
    return x, y, c, chips


def _remote(src, dst, send_sem, recv_sem, device):
    return pltpu.make_async_remote_copy(src_ref=src, dst_ref=dst, send_sem=send_sem, recv_sem=recv_sem,
                                        device_id=device, device_id_type=MESH)


HBM_SPEC = pl.BlockSpec(memory_space=pl.ANY)


def _gather_weights(wi_part, wo_part):
    d = wi_part.shape[0]
    sc = wi_part.shape[1] // N_CHIPS
    sr = wo_part.shape[0] // N_CHIPS
    hd, hr = d // 2, sr // 2

    def body(wi_in, wo_in, wi_full, wo_full, send_sems, recv_sems):
        del wi_in, wo_in
        x, y, c, chips = _mesh_pos()
        me = 2 * x + y
        sibling = (x, y, 1 - c)

        def in_piece(chip, half):
            return wi_full.at[pl.ds(half * hd, hd), pl.ds(pl.multiple_of(chip * sc, LANES), sc)]

        def out_piece(chip, half):
            return wo_full.at[pl.ds(pl.multiple_of(chip * sr + half * hr, 8), hr), :]

        first = []
        for j, (px, py) in enumerate(chips):
            first.append(_remote(in_piece(me, c), in_piece(me, c),
                                 send_sems.at[j], recv_sems.at[j], (px, py, c)))
            first.append(_remote(out_piece(me, c), out_piece(me, c),
                                 send_sems.at[3 + j], recv_sems.at[3 + j], (px, py, c)))
        for cp in first:
            cp.start()
        passed = []
        for j, (px, py) in enumerate(chips):
            chip = 2 * px + py
            for k, piece in ((j, in_piece), (3 + j, out_piece)):
                landed = piece(chip, c)
                _remote(landed, landed, send_sems.at[k], recv_sems.at[k], sibling).wait_recv()
                fwd = _remote(landed, landed, send_sems.at[6 + k], recv_sems.at[6 + k], sibling)
                fwd.start()
                passed.append(fwd)
        for j, (px, py) in enumerate(chips):
            chip = 2 * px + py
            for k, piece in ((j, in_piece), (3 + j, out_piece)):
                theirs = piece(chip, 1 - c)
                _remote(theirs, theirs, send_sems.at[6 + k], recv_sems.at[6 + k], sibling).wait_recv()
        for cp in first + passed:
            cp.wait_send()

    return pl.pallas_call(
        body, name="gather_weights",
        in_specs=[HBM_SPEC, HBM_SPEC], out_specs=[HBM_SPEC, HBM_SPEC],
        out_shape=[jax.ShapeDtypeStruct(wi_part.shape, BF16), jax.ShapeDtypeStruct(wo_part.shape, BF16)],
        input_output_aliases={0: 0, 1: 1},
        scratch_shapes=[pltpu.SemaphoreType.DMA((12,)), pltpu.SemaphoreType.DMA((12,))],
    )(wi_part, wo_part)


def _pair_exchange(gw_in, gw_out):
    _, hd, d_in = gw_in.shape
    _, _, hr, d = gw_out.shape

    def body(gi_ref, go_ref, ri_ref, ro_ref, send_sems, recv_sems):
        x, y, c, _ = _mesh_pos()
        sibling = (x, y, 1 - c)
        a = _remote(gi_ref.at[1 - c], ri_ref, send_sems.at[0], recv_sems.at[0], sibling)
        b = _remote(go_ref.at[1 - c], ro_ref, send_sems.at[1], recv_sems.at[1], sibling)
        a.start()
        b.start()
        a.wait()
        b.wait()

    return pl.pallas_call(
        body, name="pair_exchange",
        in_specs=[HBM_SPEC, HBM_SPEC], out_specs=[HBM_SPEC, HBM_SPEC],
        out_shape=[jax.ShapeDtypeStruct((hd, d_in), BF16), jax.ShapeDtypeStruct((N_CHIPS, hr, d), BF16)],
        scratch_shapes=[pltpu.SemaphoreType.DMA((2,)), pltpu.SemaphoreType.DMA((2,))],
    )(gw_in, gw_out)


def _pair_sum_in(gw_in, recv, sc):
    _, hd, d_in = gw_in.shape
    tr = min(hd, 256)
    c = lax.axis_index("c")

    def body(c_ref, a_ref, b_ref, o_ref):
        del c_ref
        o_ref[...] = (a_ref[...].astype(F32) + b_ref[...].astype(F32)).astype(BF16)

    blk = pl.BlockSpec((tr, sc), lambda i, j, cc: (i, j))
    gs = pltpu.PrefetchScalarGridSpec(
        num_scalar_prefetch=1, grid=(hd // tr, d_in // sc),
        in_specs=[pl.BlockSpec((None, tr, sc), lambda i, j, cc: (cc[0], i, j)), blk], out_specs=blk)
    return pl.pallas_call(
        body, name="pair_sum_in", grid_spec=gs, out_shape=jax.ShapeDtypeStruct((hd, d_in), BF16),
        compiler_params=_params(dimension_semantics=("parallel", "parallel")),
    )(jnp.reshape(c, (1,)).astype(jnp.int32), gw_in, recv)


def _pair_sum_out(gw_out, recv):
    _, n_chips, hr, d = gw_out.shape
    tr = min(hr, 256)
    c = lax.axis_index("c")

    def body(c_ref, a_ref, b_ref, o_ref):
        del c_ref
        o_ref[...] = (a_ref[...].astype(F32) + b_ref[...].astype(F32)).astype(BF16)

    blk = pl.BlockSpec((None, tr, d), lambda k, i, cc: (k, i, 0))
    gs = pltpu.PrefetchScalarGridSpec(
        num_scalar_prefetch=1, grid=(n_chips, hr // tr),
        in_specs=[pl.BlockSpec((None, None, tr, d), lambda k, i, cc: (cc[0], k, i, 0)), blk], out_specs=blk)
    return pl.pallas_call(
        body, name="pair_sum_out", grid_spec=gs, out_shape=jax.ShapeDtypeStruct((n_chips, hr, d), BF16),
        compiler_params=_params(dimension_semantics=("parallel", "parallel")),
    )(jnp.reshape(c, (1,)).astype(jnp.int32), gw_out, recv)


def _chip_exchange(p_in, p_out, sc):
    hd, d_in = p_in.shape
    _, hr, d = p_out.shape

    def body(pi_ref, po_ref, ri_ref, ro_ref, send_sems, recv_sems):
        x, y, c, chips = _mesh_pos()
        copies = []
        for j, (px, py) in enumerate(chips):
            chip = 2 * px + py
            copies.append(_remote(pi_ref.at[:, pl.ds(pl.multiple_of(chip * sc, LANES), sc)], ri_ref.at[j],
                                  send_sems.at[j], recv_sems.at[j], (px, py, c)))
            copies.append(_remote(po_ref.at[chip], ro_ref.at[j],
                                  send_sems.at[3 + j], recv_sems.at[3 + j], (px, py, c)))
        for cp in copies:
            cp.start()
        for cp in copies:
            cp.wait()

    return pl.pallas_call(
        body, name="chip_exchange",
        in_specs=[HBM_SPEC, HBM_SPEC], out_specs=[HBM_SPEC, HBM_SPEC],
        out_shape=[jax.ShapeDtypeStruct((3, hd, sc), BF16), jax.ShapeDtypeStruct((3, hr, d), BF16)],
        scratch_shapes=[pltpu.SemaphoreType.DMA((6,)), pltpu.SemaphoreType.DMA((6,))],
    )(p_in, p_out)


def _place():
    return jnp.stack([2 * lax.axis_index("x") + lax.axis_index("y"), lax.axis_index("c")]).astype(jnp.int32)


def _chip_sum_in(p_in, r_in, sc):
    hd = p_in.shape[0]
    tr = min(hd, 256)
    nblk = hd // tr

    def body(pos_ref, p_ref, r_ref, o_ref):
        del pos_ref
        acc = p_ref[...].astype(F32)
        for j in range(3):
            acc = acc + r_ref[j].astype(F32)
        o_ref[...] = acc

    gs = pltpu.PrefetchScalarGridSpec(
        num_scalar_prefetch=1, grid=(nblk,),
        in_specs=[pl.BlockSpec((tr, sc), lambda i, pos: (i, pos[0])), pl.BlockSpec((3, tr, sc), lambda i, pos: (0, i, 0))],
        out_specs=pl.BlockSpec((tr, sc), lambda i, pos: (pos[1] * nblk + i, 0)))
    return pl.pallas_call(
        body, name="chip_sum_in", grid_spec=gs, out_shape=jax.ShapeDtypeStruct((2 * hd, sc), F32),
        compiler_params=_params(dimension_semantics=("parallel",)),
    )(_place(), p_in, r_in)


def _chip_sum_out(p_out, r_out):
    _, hr, d = p_out.shape
    tr = min(hr, 256)
    nblk = hr // tr

    def body(pos_ref, p_ref, r_ref, o_ref):
        del pos_ref
        acc = p_ref[...].astype(F32)
        for j in range(3):
            acc = acc + r_ref[j].astype(F32)
        o_ref[...] = acc

    gs = pltpu.PrefetchScalarGridSpec(
        num_scalar_prefetch=1, grid=(nblk,),
        in_specs=[pl.BlockSpec((None, tr, d), lambda i, pos: (pos[0], i, 0)), pl.BlockSpec((3, tr, d), lambda i, pos: (0, i, 0))],
        out_specs=pl.BlockSpec((tr, d), lambda i, pos: (pos[1] * nblk + i, 0)))
    return pl.pallas_call(
        body, name="chip_sum_out", grid_spec=gs, out_shape=jax.ShapeDtypeStruct((2 * hr, d), F32),
        compiler_params=_params(dimension_semantics=("parallel",)),
    )(_place(), p_out, r_out)


def _share_halves(g_in, g_out):
    hd = g_in.shape[0] // 2
    hr = g_out.shape[0] // 2

    def body(gi_in, go_in, gi_ref, go_ref, send_sems, recv_sems):
        del gi_in, go_in
        x, y, c, _ = _mesh_pos()
        sibling = (x, y, 1 - c)
        mine_i = gi_ref.at[pl.ds(pl.multiple_of(c * hd, 8), hd), :]
        mine_o = go_ref.at[pl.ds(pl.multiple_of(c * hr, 8), hr), :]
        a = _remote(mine_i, mine_i, send_sems.at[0], recv_sems.at[0], sibling)
        b = _remote(mine_o, mine_o, send_sems.at[1], recv_sems.at[1], sibling)
        a.start()
        b.start()
        a.wait_send()
        b.wait_send()
        theirs_i = gi_ref.at[pl.ds(pl.multiple_of((1 - c) * hd, 8), hd), :]
        theirs_o = go_ref.at[pl.ds(pl.multiple_of((1 - c) * hr, 8), hr), :]
        _remote(theirs_i, theirs_i, send_sems.at[0], recv_sems.at[0], sibling).wait_recv()
        _remote(theirs_o, theirs_o, send_sems.at[1], recv_sems.at[1], sibling).wait_recv()

    return pl.pallas_call(
        body, name="share_halves",
        in_specs=[HBM_SPEC, HBM_SPEC], out_specs=[HBM_SPEC, HBM_SPEC],
        out_shape=[jax.ShapeDtypeStruct(g_in.shape, F32), jax.ShapeDtypeStruct(g_out.shape, F32)],
        input_output_aliases={0: 0, 1: 1},
        scratch_shapes=[pltpu.SemaphoreType.DMA((2,)), pltpu.SemaphoreType.DMA((2,))],
    )(g_in, g_out)


def _adamw_math(w, g, m, v):
    m_new = ADAM_B1 * m + (1.0 - ADAM_B1) * g
    v_new = ADAM_B2 * v + (1.0 - ADAM_B2) * (g * g)
    m_hat = m_new / (1.0 - ADAM_B1 ** ADAM_STEP)
    v_hat = v_new / (1.0 - ADAM_B2 ** ADAM_STEP)
    delta = -ADAM_LR * (m_hat / (jnp.sqrt(v_hat) + ADAM_EPS) + ADAM_WD * w)
    return delta, m_new, v_new


def _adamw(w, g, m, v, name):
    rows, cols = w.shape
    tr = min(rows, 128)

    def body(w_ref, g_ref, m_ref, v_ref, d_ref, mo_ref, vo_ref):
        delta, m_new, v_new = _adamw_math(w_ref[...], g_ref[...], m_ref[...], v_ref[...])
        d_ref[...] = delta
        mo_ref[...] = m_new
        vo_ref[...] = v_new

    spec = pl.BlockSpec((tr, cols), lambda i: (i, 0))
    shape = jax.ShapeDtypeStruct((rows, cols), F32)
    return pl.pallas_call(
        body, name=name, grid=(rows // tr,), in_specs=[spec] * 4, out_specs=[spec] * 3,
        out_shape=[shape] * 3, compiler_params=_params(dimension_semantics=("parallel",)),
    )(w, g, m, v)


SMALL_ROWS = 8


def _small_allreduce_adamw(part, w_pack, m_pack, v_pack):
    d = part.shape[1]

    def body(part_ref, w_ref, m_ref, v_ref, g_ref, d_ref, mo_ref, vo_ref, buf_ref, send_sems, recv_sems):
        x, y, c, _ = _mesh_pos()
        me = 4 * x + 2 * y + c
        buf_ref[0] = part_ref[...]
        copies = []
        for r in range(1, 8):
            rx, ry, rc = (r >> 2) & 1, (r >> 1) & 1, r & 1
            peer = (x ^ rx, y ^ ry, c ^ rc)
            copies.append(_remote(buf_ref.at[0], buf_ref.at[r], send_sems.at[r - 1], recv_sems.at[r - 1], peer))
        for cp in copies:
            cp.start()
        for cp in copies:
            cp.wait()
        total = buf_ref[me]
        for s in range(1, 8):
            total = total + buf_ref[s ^ me]
        w = w_ref[...]
        row = lax.broadcasted_iota(jnp.int32, (SMALL_ROWS, d), 0)
        l0, l1 = w[3:4], w[4:5]
        mx = jnp.maximum(l0, l1)
        e0, e1 = jnp.exp(l0 - mx), jnp.exp(l1 - mx)
        lb = e0 / (e0 + e1)
        g_l0 = total[3:4] * lb * (1.0 - lb)
        grads = jnp.where(row == 3, g_l0, jnp.where(row == 4, -g_l0, total))
        g_ref[...] = grads
        delta, m_new, v_new = _adamw_math(w, grads, m_ref[...], v_ref[...])
        d_ref[...] = delta
        mo_ref[...] = m_new
        vo_ref[...] = v_new

    vm = pl.BlockSpec(memory_space=pltpu.VMEM)
    shape = jax.ShapeDtypeStruct((SMALL_ROWS, d), F32)
    return pl.pallas_call(
        body, name="small_allreduce_adamw",
        in_specs=[vm] * 4, out_specs=[vm] * 4, out_shape=[shape] * 4,
        scratch_shapes=[pltpu.VMEM((8, SMALL_ROWS, d), F32), pltpu.SemaphoreType.DMA((7,)), pltpu.SemaphoreType.DMA((7,))],
    )(part, w_pack, m_pack, v_pack)


def _pack_small(d, pre, post, rnn, lb, sink, extra=None):
    rows = [pre, post, rnn, lb[0:1], lb[1:2],
            jnp.pad(sink, ((0, 0), (0, d - sink.shape[1]))),
            jnp.zeros((1, d), F32) if extra is None else extra,
            jnp.zeros((1, d), F32)]
    return jnp.concatenate(rows, axis=0)


def _unpack_small(p, n_sink):
    return dict(pre=p[0:1], post=p[1:2], rnn=p[2:3], lb=p[3:5], sink=p[5:6, :n_sink])


def kernel(x, w_in, attn_sinks, lb_logits, rnn_norm, w_out, pre_norm, post_norm, loss_target, m_w_in, m_attn_sinks, m_lb_logits, m_rnn_norm, m_w_out, m_pre_norm, m_post_norm, v_w_in, v_attn_sinks, v_lb_logits, v_rnn_norm, v_w_out, v_pre_norm, v_post_norm):
    t, d = x.shape[1], x.shape[2]
    sc = w_in.shape[2]
    n_sink = attn_sinks.shape[1]
    w_in2, w_out2 = w_in[0], w_out[0]

    w_in_full, w_out_full = _gather_weights(_cast_into_gathered(w_in2, "cast_w_in", 1),
                                            _cast_into_gathered(w_out2, "cast_w_out", 0))
    loss_part, grad_x, gw_in, gw_out, small = _local_grads(
        x[0], loss_target[0], w_in_full, w_out_full, attn_sinks, lb_logits, rnn_norm, pre_norm, post_norm)

    recv_in, recv_out = _pair_exchange(gw_in, gw_out)
    p_in = _pair_sum_in(gw_in, recv_in, sc)
    p_out = _pair_sum_out(gw_out, recv_out)
    r_in, r_out = _chip_exchange(p_in, p_out, sc)
    g_w_in, g_w_out = _share_halves(_chip_sum_in(p_in, r_in, sc), _chip_sum_out(p_out, r_out))

    d_w_in, nm_w_in, nv_w_in = _adamw(w_in2, g_w_in, m_w_in[0], v_w_in[0], "adamw_w_in")
    d_w_out, nm_w_out, nv_w_out = _adamw(w_out2, g_w_out, m_w_out[0], v_w_out[0], "adamw_w_out")

    lb_part = jnp.concatenate([small["lb"], jnp.zeros_like(small["lb"])], axis=0)
    loss_row = jnp.pad(loss_part[:, :1], ((0, 0), (0, d - 1)))
    part = _pack_small(d, small["pre"], small["post"], small["rnn"], lb_part, small["sink"], loss_row)
    w_pack = _pack_small(d, pre_norm, post_norm, rnn_norm, lb_logits, attn_sinks)
    m_pack = _pack_small(d, m_pre_norm, m_post_norm, m_rnn_norm, m_lb_logits, m_attn_sinks)
    v_pack = _pack_small(d, v_pre_norm, v_post_norm, v_rnn_norm, v_lb_logits, v_attn_sinks)
    g_pack, d_pack, nm_pack, nv_pack = _small_allreduce_adamw(part, w_pack, m_pack, v_pack)
    loss = g_pack[6, 0]
    g, dl, nm, nv = (_unpack_small(p, n_sink) for p in (g_pack, d_pack, nm_pack, nv_pack))

    def ordered(w_in_leaf, w_out_leaf, s):
        return (w_in_leaf[None], s["sink"], s["lb"], s["rnn"], w_out_leaf[None], s["pre"], s["post"])

    return (loss, grad_x[None],
            *ordered(g_w_in, g_w_out, g), *ordered(d_w_in, d_w_out, dl),
            *ordered(nm_w_in, nm_w_out, nm), *ordered(nv_w_in, nv_w_out, nv))
```

```python
import numpy as np
import jax
import jax.numpy as jnp
from jax import lax
from jax.experimental import pallas as pl
from jax.experimental.pallas import tpu as pltpu

F32 = jnp.float32
BF16 = jnp.bfloat16
MESH = pl.DeviceIdType.MESH

NORM_EPS = 1e-6
ATTN_HEAD_DIM = 64
GQA_GROUP = 8
WINDOW = 128
RNN_HEAD_DIM = 128
CHUNK = 64
HALF_CHUNK = CHUNK // 2
ATTN_SCALE = ATTN_HEAD_DIM ** -0.5

ADAM_LR = 0.001
ADAM_B1 = 0.9
ADAM_B2 = 0.999
ADAM_EPS = 1e-08
ADAM_WD = 0.01
ADAM_STEP = 10

LANES = 128
COL_TILE = 512
RNN_GROUP_HEADS = 4
ATTN_GROUP_LANES = 1024
N_CHIPS = 4
VMEM_LIMIT_BYTES = 56 * 1024 * 1024
NEG_BIG = -1e30


def _params(**kw):
    return pltpu.CompilerParams(vmem_limit_bytes=VMEM_LIMIT_BYTES, **kw)


def _sigmoid(x):
    return 1.0 / (1.0 + jnp.exp(-x))


def _dot(a, b):
    return jnp.dot(a, b, preferred_element_type=F32)


def _dot_nt(a, b):
    return lax.dot_general(a, b, (((1,), (1,)), ((), ())), preferred_element_type=F32)


def _dot_tn(a, b):
    return lax.dot_general(a, b, (((0,), (0,)), ((), ())), preferred_element_type=F32)


def _split3(x):
    hi = x.astype(BF16)
    r1 = x - hi.astype(F32)
    mid = r1.astype(BF16)
    lo = (r1 - mid.astype(F32)).astype(BF16)
    return hi, mid, lo


def _tri_dot(tri_bf16, x):
    hi, mid, lo = _split3(x)
    return _dot(tri_bf16, hi) + _dot(tri_bf16, mid) + _dot(tri_bf16, lo)


def _layout(d_model):
    d = d_model
    dkv = d // GQA_GROUP
    orig = dict(aq=0, ak=d, av=d + dkv, ag=d + 2 * dkv)
    base = d + 2 * dkv + d
    orig.update(rq=base, rf=base + d, ri=base + 2 * d, rg=base + 3 * d)
    group_w = RNN_GROUP_HEADS * RNN_HEAD_DIM
    cols = []
    for hg in range(d // group_w):
        for seg in ("rq", "rf", "ri", "rg"):
            cols.append((orig[seg] + hg * group_w, group_w))
    for m in range(d // ATTN_GROUP_LANES):
        for seg in ("aq", "ag"):
            cols.append((orig[seg] + m * ATTN_GROUP_LANES, ATTN_GROUP_LANES))
    cols.append((orig["ak"], dkv))
    cols.append((orig["av"], dkv))
    units = []
    for start, width in cols:
        assert start % LANES == 0 and width % LANES == 0
        units += [start + u for u in range(0, width, LANES)]
    per = COL_TILE // LANES
    assert len(units) % per == 0
    tiles = []
    for t in range(len(units) // per):
        run = units[t * per:(t + 1) * per]
        assert run[0] % COL_TILE == 0 and all(run[i] == run[0] + i * LANES for i in range(per))
        tiles.append(run[0] // COL_TILE)
    return dict(a_off=4 * d, k_off=6 * d, v_off=6 * d + dkv, total=6 * d + 2 * dkv,
                perm=np.asarray(tiles, np.int32))


def _chip_index():
    return jnp.reshape(2 * lax.axis_index("x") + lax.axis_index("y"), (1,)).astype(jnp.int32)


def _cast_into_gathered(a, name, axis):
    rows, cols = a.shape
    tr = min(rows, 512)
    nblk = rows // tr

    def body(me_ref, a_ref, o_ref):
        del me_ref
        o_ref[...] = a_ref[...].astype(BF16)

    if axis == 1:
        out_spec = pl.BlockSpec((tr, cols), lambda i, me: (i, me[0]))
        shape = (rows, N_CHIPS * cols)
    else:
        out_spec = pl.BlockSpec((tr, cols), lambda i, me: (me[0] * nblk + i, 0))
        shape = (N_CHIPS * rows, cols)
    gs = pltpu.PrefetchScalarGridSpec(num_scalar_prefetch=1, grid=(nblk,),
                                      in_specs=[pl.BlockSpec((tr, cols), lambda i, me: (i, 0))], out_specs=out_spec)
    return pl.pallas_call(
        body, name=name, grid_spec=gs, out_shape=jax.ShapeDtypeStruct(shape, BF16),
        compiler_params=_params(dimension_semantics=("parallel",)),
    )(_chip_index(), a)


def _transpose_bf16(a, name):
    rows, cols = a.shape
    tr = min(rows, 256)
    tc = min(cols, 2048)

    def body(a_ref, o_ref):
        o_ref[...] = a_ref[...].astype(F32).T.astype(BF16)

    return pl.pallas_call(
        body, name=name, grid=(rows // tr, cols // tc),
        in_specs=[pl.BlockSpec((tr, tc), lambda i, j: (i, j))],
        out_specs=pl.BlockSpec((tc, tr), lambda i, j: (j, i)),
        out_shape=jax.ShapeDtypeStruct((cols, rows), BF16),
        compiler_params=_params(dimension_semantics=("parallel", "parallel")),
    )(a)


def _prenorm_fwd(x, gain):
    t, d = x.shape
    tm = min(t, 256)

    def body(x_ref, g_ref, h_ref, ht_ref):
        xv = x_ref[...]
        r = lax.rsqrt(jnp.mean(xv * xv, axis=-1, keepdims=True) + NORM_EPS)
        h = (xv * r) * g_ref[...]
        h_ref[...] = h.astype(BF16)
        ht_ref[...] = h.T.astype(BF16)

    return pl.pallas_call(
        body, name="prenorm_fwd", grid=(t // tm,),
        in_specs=[pl.BlockSpec((tm, d), lambda i: (i, 0)), pl.BlockSpec((1, d), lambda i: (0, 0))],
        out_specs=[pl.BlockSpec((tm, d), lambda i: (i, 0)), pl.BlockSpec((d, tm), lambda i: (0, i))],
        out_shape=[jax.ShapeDtypeStruct((t, d), BF16), jax.ShapeDtypeStruct((d, t), BF16)],
        compiler_params=_params(dimension_semantics=("parallel",)),
    )(x, gain)


def _post_loss(x, y, target, gain):
    t, d = x.shape
    tm = min(t, 256)
    inv_d = 1.0 / d

    def body(x_ref, y_ref, t_ref, g_ref, dy_ref, dz_ref, gp_ref, loss_ref):
        i = pl.program_id(0)
        yv = y_ref[...]
        gain_v = g_ref[...]
        r = lax.rsqrt(jnp.mean(yv * yv, axis=-1, keepdims=True) + NORM_EPS)
        n = yv * r
        e = (x_ref[...] + n * gain_v) - t_ref[...]
        dz = e * inv_d
        dn = dz * gain_v
        dy = r * (dn - n * jnp.mean(dn * n, axis=-1, keepdims=True))
        dy_ref[...] = dy.astype(BF16)
        dz_ref[...] = dz

        @pl.when(i == 0)
        def _():
            gp_ref[...] = jnp.zeros_like(gp_ref)
            loss_ref[...] = jnp.zeros_like(loss_ref)

        gp_ref[...] += jnp.sum(dz * n, axis=0, keepdims=True)
        row = jnp.sum(e * e, axis=-1, keepdims=True)
        loss_ref[...] += jnp.full(loss_ref.shape, 0.5 * inv_d * jnp.sum(row), F32)

    row_spec = pl.BlockSpec((tm, d), lambda i: (i, 0))
    vec_spec = pl.BlockSpec((1, d), lambda i: (0, 0))
    return pl.pallas_call(
        body, name="post_loss", grid=(t // tm,),
        in_specs=[row_spec, row_spec, row_spec, vec_spec],
        out_specs=[row_spec, row_spec, vec_spec, pl.BlockSpec((1, LANES), lambda i: (0, 0))],
        out_shape=[jax.ShapeDtypeStruct((t, d), BF16), jax.ShapeDtypeStruct((t, d), F32),
                   jax.ShapeDtypeStruct((1, d), F32), jax.ShapeDtypeStruct((1, LANES), F32)],
        compiler_params=_params(dimension_semantics=("arbitrary",)),
    )(x, y, target, gain)


def _prenorm_bwd(x, dh, dz, gain):
    t, d = x.shape
    tm = min(t, 256)

    def body(x_ref, dh_ref, dz_ref, g_ref, gx_ref, gp_ref):
        i = pl.program_id(0)
        xv = x_ref[...]
        r = lax.rsqrt(jnp.mean(xv * xv, axis=-1, keepdims=True) + NORM_EPS)
        n = xv * r
        dhv = dh_ref[...]
        dn = dhv * g_ref[...]
        gx_ref[...] = dz_ref[...] + r * (dn - n * jnp.mean(dn * n, axis=-1, keepdims=True))

        @pl.when(i == 0)
        def _():
            gp_ref[...] = jnp.zeros_like(gp_ref)

        gp_ref[...] += jnp.sum(dhv * n, axis=0, keepdims=True)

    row_spec = pl.BlockSpec((tm, d), lambda i: (i, 0))
    vec_spec = pl.BlockSpec((1, d), lambda i: (0, 0))
    return pl.pallas_call(
        body, name="prenorm_bwd", grid=(t // tm,),
        in_specs=[row_spec, row_spec, row_spec, vec_spec],
        out_specs=[row_spec, vec_spec],
        out_shape=[jax.ShapeDtypeStruct((t, d), F32), jax.ShapeDtypeStruct((1, d), F32)],
        compiler_params=_params(dimension_semantics=("arbitrary",)),
    )(x, dh, dz, gain)


class _Rider:
    def __init__(self, operands, out_shapes, aliases, n_sems, stages):
        self.operands = tuple(operands)
        self.out_shapes = tuple(out_shapes)
        self.aliases = dict(aliases)
        self.n_sems = n_sems
        self.stages = stages

    def scratch(self):
        return [pltpu.SemaphoreType.DMA((self.n_sems,)), pltpu.SemaphoreType.DMA((self.n_sems,))]

    def emit(self, step, n_steps, in_refs, out_refs, send_sems, recv_sems):
        for frac, fn in self.stages(in_refs, out_refs, send_sems, recv_sems):
            at = min(n_steps - 1, int(frac * (n_steps - 1) + 0.5))
            pl.when(step == at)(fn)


def _run_rider_alone(name, rider):
    n_in, n_out = len(rider.operands), len(rider.out_shapes)

    def body(*refs):
        ins, outs = refs[:n_in], refs[n_in:n_in + n_out]
        send_sems, recv_sems = refs[n_in + n_out:]
        for _, fn in rider.stages(ins, outs, send_sems, recv_sems):
            fn()

    return pl.pallas_call(
        body, name=name, in_specs=[HBM_SPEC] * n_in, out_specs=[HBM_SPEC] * n_out,
        out_shape=list(rider.out_shapes), input_output_aliases=rider.aliases,
        scratch_shapes=rider.scratch())(*rider.operands)


def _matmul(name, a, b, *, out_shape, grid, a_spec, b_spec, o_spec, nt=False, perm=None, rider=None):
    nk = grid[2]
    n_steps = grid[0] * grid[1] * grid[2]
    tm, tn = [s for s in o_spec.block_shape if s is not None][-2:]
    acc_in_out = out_shape.dtype == F32
    n_pre = 0 if perm is None else 1
    n_rin = 0 if rider is None else len(rider.operands)
    n_rout = 0 if rider is None else len(rider.out_shapes)
    use_acc = not (nk == 1 or acc_in_out)

    def body(*refs):
        refs = refs[n_pre:]
        a_ref, b_ref = refs[:2]
        rin = refs[2:2 + n_rin]
        o_ref = refs[2 + n_rin]
        rout = refs[3 + n_rin:3 + n_rin + n_rout]
        scratch_refs = refs[3 + n_rin + n_rout:]
        if rider is not None:
            step = (pl.program_id(0) * grid[1] + pl.program_id(1)) * grid[2] + pl.program_id(2)
            rider.emit(step, n_steps, rin, rout, scratch_refs[-2], scratch_refs[-1])
        part = _dot_nt(a_ref[...], b_ref[...]) if nt else _dot(a_ref[...], b_ref[...])
        if nk == 1:
            o_ref[...] = part.astype(o_ref.dtype)
            return
        acc_ref = o_ref if acc_in_out else scratch_refs[0]
        k = pl.program_id(2)

        @pl.when(k == 0)
        def _():
            acc_ref[...] = part

        @pl.when(k > 0)
        def _():
            acc_ref[...] += part

        if not acc_in_out:
            @pl.when(k == nk - 1)
            def _():
                o_ref[...] = acc_ref[...].astype(o_ref.dtype)

    scratch = [pltpu.VMEM((tm, tn), F32)] if use_acc else []
    in_specs = [a_spec, b_spec] + [HBM_SPEC] * n_rin
    out_specs = [o_spec] + [HBM_SPEC] * n_rout
    out_shapes = [out_shape]
    operands = [a, b]
    aliases = {}
    sem = ("parallel", "parallel", "arbitrary")
    if rider is not None:
        scratch += rider.scratch()
        out_shapes += list(rider.out_shapes)
        operands += list(rider.operands)
        aliases = {n_pre + 2 + i: 1 + o for i, o in rider.aliases.items()}
        sem = ("arbitrary", "arbitrary", "arbitrary")
    cp = _params(dimension_semantics=sem)
    if perm is None:
        return pl.pallas_call(body, name=name, grid=grid, in_specs=in_specs, out_specs=out_specs,
                              out_shape=out_shapes, scratch_shapes=scratch, input_output_aliases=aliases,
                              compiler_params=cp)(*operands)
    gs = pltpu.PrefetchScalarGridSpec(num_scalar_prefetch=1, grid=grid, in_specs=in_specs,
                                      out_specs=out_specs, scratch_shapes=scratch)
    return pl.pallas_call(body, name=name, grid_spec=gs, out_shape=out_shapes, input_output_aliases=aliases,
                          compiler_params=cp)(jnp.asarray(perm), *operands)


def _proj_mm(h, w_full, perm, rider=None):
    t, d = h.shape
    n_tiles = len(perm)
    tm = min(t, 1024)
    return _matmul(
        "proj_mm", h, w_full, perm=perm, rider=rider, grid=(t // tm, n_tiles, 1),
        out_shape=jax.ShapeDtypeStruct((t, n_tiles * COL_TILE), F32),
        a_spec=pl.BlockSpec((tm, d), lambda i, j, k, p: (i, 0)),
        b_spec=pl.BlockSpec((d, COL_TILE), lambda i, j, k, p: (0, p[j])),
        o_spec=pl.BlockSpec((tm, COL_TILE), lambda i, j, k, p: (i, j)))


def _gw_in_mm(ht, dproj, perm):
    d, t = ht.shape
    n_tiles = len(perm)
    hd = d // 2
    tm = min(hd, 1024)
    per_half = hd // tm
    return _matmul(
        "gw_in_mm", ht, dproj, perm=perm, grid=(d // tm, n_tiles, 1),
        out_shape=jax.ShapeDtypeStruct((2, hd, n_tiles * COL_TILE), BF16),
        a_spec=pl.BlockSpec((tm, t), lambda i, j, k, p: (i, 0)),
        b_spec=pl.BlockSpec((t, COL_TILE), lambda i, j, k, p: (0, j)),
        o_spec=pl.BlockSpec((None, tm, COL_TILE), lambda i, j, k, p: (i // per_half, i % per_half, p[j])))[0]


def _dh_mm(dproj, w_full, perm, rider=None):
    t = dproj.shape[0]
    d = w_full.shape[0]
    n_tiles = len(perm)
    tm = min(t, 2048)
    tn = min(d, 2048)
    return _matmul(
        "dh_mm", dproj, w_full, perm=perm, rider=rider, nt=True, grid=(t // tm, d // tn, n_tiles),
        out_shape=jax.ShapeDtypeStruct((t, d), F32),
        a_spec=pl.BlockSpec((tm, COL_TILE), lambda i, j, k, p: (i, k)),
        b_spec=pl.BlockSpec((tn, COL_TILE), lambda i, j, k, p: (j, p[k])),
        o_spec=pl.BlockSpec((tm, tn), lambda i, j, k, p: (i, j)))


def _out_mm(mixed, w_out_full):
    t, dm = mixed.shape
    d = w_out_full.shape[1]
    tm = min(t, 1024)
    tn = min(d, 512)
    tk = min(dm, 4096)
    return _matmul(
        "out_mm", mixed, w_out_full, grid=(t // tm, d // tn, dm // tk),
        out_shape=jax.ShapeDtypeStruct((t, d), F32),
        a_spec=pl.BlockSpec((tm, tk), lambda i, j, k: (i, k)),
        b_spec=pl.BlockSpec((tk, tn), lambda i, j, k: (k, j)),
        o_spec=pl.BlockSpec((tm, tn), lambda i, j, k: (i, j)))[0]


def _dmixed_mm(dy, w_out_full):
    t, d = dy.shape
    dm = w_out_full.shape[0]
    tm = min(t, 1024)
    tn = min(dm, 1024)
    return _matmul(
        "dmixed_mm", dy, w_out_full, nt=True, grid=(t // tm, dm // tn, 1),
        out_shape=jax.ShapeDtypeStruct((t, dm), F32),
        a_spec=pl.BlockSpec((tm, d), lambda i, j, k: (i, 0)),
        b_spec=pl.BlockSpec((tn, d), lambda i, j, k: (j, 0)),
        o_spec=pl.BlockSpec((tm, tn), lambda i, j, k: (i, j)))[0]


def _gw_out_mm(mixed_t, dy):
    dm, t = mixed_t.shape
    d = dy.shape[1]
    hr = dm // (2 * N_CHIPS)
    tn = min(d, 1024)
    return _matmul(
        "gw_out_mm", mixed_t, dy, grid=(dm // hr, d // tn, 1),
        out_shape=jax.ShapeDtypeStruct((2, N_CHIPS, hr, d), BF16),
        a_spec=pl.BlockSpec((hr, t), lambda i, j, k: (i, 0)),
        b_spec=pl.BlockSpec((t, tn), lambda i, j, k: (0, j)),
        o_spec=pl.BlockSpec((None, None, hr, tn), lambda i, j, k: (i % 2, i // 2, 0, j)))[0]


def _lane_half():
    return lax.broadcasted_iota(jnp.int32, (WINDOW, LANES), 1) // ATTN_HEAD_DIM


def _dup_kv(tile, kh):
    return jnp.where(_lane_half() == kh, tile, pltpu.roll(tile, ATTN_HEAD_DIM, 1))


def _stack_heads(tiles, kh):
    half = _lane_half()
    pieces = []
    for g in range(GQA_GROUP):
        pieces.append(jnp.where(half == g % 2, tiles[4 * kh + g // 2], 0.0))
    return jnp.concatenate(pieces, axis=0)


def _unstack_heads(stacked):
    half = _lane_half()
    out = []
    for j in range(GQA_GROUP // 2):
        a = stacked[(2 * j) * WINDOW:(2 * j + 1) * WINDOW]
        b = stacked[(2 * j + 1) * WINDOW:(2 * j + 2) * WINDOW]
        out.append(jnp.where(half == 0, a, b))
    return out


def _attn_probs(qs, kcat, sink_col, n):
    rows = GQA_GROUP * WINDOW
    s = _dot_nt(qs, kcat)
    qi = lax.broadcasted_iota(jnp.int32, (rows, 2 * WINDOW), 0) % WINDOW
    kj = lax.broadcasted_iota(jnp.int32, (rows, 2 * WINDOW), 1)
    first_key = WINDOW * (1 - jnp.minimum(n, 1))
    valid = (kj > qi) & (kj <= qi + WINDOW) & (kj >= first_key)
    s = jnp.where(valid, s, NEG_BIG)
    mx = jnp.maximum(jnp.max(s, axis=-1, keepdims=True), sink_col)
    p = jnp.where(valid, jnp.exp(s - mx), 0.0)
    p_sink = jnp.exp(sink_col - mx)
    inv = 1.0 / (jnp.sum(p, axis=-1, keepdims=True) + p_sink)
    return p * inv, p_sink * inv


def _attn_operands(sink_ref, q_tiles, kp_ref, kc_ref, vp_ref, vc_ref, m, kh):
    qs = _stack_heads([qt * ATTN_SCALE for qt in q_tiles], kh).astype(BF16)
    kcat = jnp.concatenate([_dup_kv(kp_ref[...], kh), _dup_kv(kc_ref[...], kh)], axis=0).astype(BF16)
    vcat = jnp.concatenate([_dup_kv(vp_ref[...], kh), _dup_kv(vc_ref[...], kh)], axis=0).astype(BF16)
    heads_per_group = ATTN_GROUP_LANES // ATTN_HEAD_DIM
    sink_col = jnp.concatenate(
        [jnp.full((WINDOW, 1), sink_ref[0, m * heads_per_group + kh * GQA_GROUP + g], F32)
         for g in range(GQA_GROUP)], axis=0)
    return qs, kcat, vcat, sink_col


def _attn_specs(lay, d):
    a_blk = lay["a_off"] // (2 * ATTN_GROUP_LANES)
    k_blk = lay["k_off"] // LANES
    v_blk = lay["v_off"] // LANES
    qg = pl.BlockSpec((WINDOW, 2 * ATTN_GROUP_LANES), lambda m, n: (n, a_blk + m))
    kp = pl.BlockSpec((WINDOW, LANES), lambda m, n: (jnp.maximum(n - 1, 0), k_blk + m))
    kc = pl.BlockSpec((WINDOW, LANES), lambda m, n: (n, k_blk + m))
    vp = pl.BlockSpec((WINDOW, LANES), lambda m, n: (jnp.maximum(n - 1, 0), v_blk + m))
    vc = pl.BlockSpec((WINDOW, LANES), lambda m, n: (n, v_blk + m))
    return qg, kp, kc, vp, vc


def _attn_fwd(proj, sinks, lay, d):
    t = proj.shape[0]
    n_groups = d // ATTN_GROUP_LANES
    pairs = ATTN_GROUP_LANES // LANES

    def body(sink_ref, qg_ref, kp_ref, kc_ref, vp_ref, vc_ref, mix_ref, o_ref):
        m = pl.program_id(0)
        n = pl.program_id(1)
        q_tiles = [qg_ref[:, p * LANES:(p + 1) * LANES] for p in range(pairs)]
        for kh in range(2):
            qs, kcat, vcat, sink_col = _attn_operands(sink_ref, q_tiles, kp_ref, kc_ref, vp_ref, vc_ref, m, kh)
            probs, _ = _attn_probs(qs, kcat, sink_col, n)
            out = _dot(probs.astype(BF16), vcat)
            for j, tile in enumerate(_unstack_heads(out)):
                p = 4 * kh + j
                lanes = slice(p * LANES, (p + 1) * LANES)
                gate = qg_ref[:, ATTN_GROUP_LANES + p * LANES:ATTN_GROUP_LANES + (p + 1) * LANES]
                o_ref[:, lanes] = tile
                mix_ref[:, lanes] = (tile * (gate * _sigmoid(gate))).astype(BF16)

    qg, kp, kc, vp, vc = _attn_specs(lay, d)
    out_blk = pl.BlockSpec((WINDOW, ATTN_GROUP_LANES), lambda m, n: (n, m))
    return pl.pallas_call(
        body, name="attn_fwd", grid=(n_groups, t // WINDOW),
        in_specs=[pl.BlockSpec(memory_space=pltpu.SMEM), qg, kp, kc, vp, vc],
        out_specs=[out_blk, out_blk],
        out_shape=[jax.ShapeDtypeStruct((t, 2 * d), BF16), jax.ShapeDtypeStruct((t, d), F32)],
        compiler_params=_params(dimension_semantics=("parallel", "parallel")),
    )(sinks, proj, proj, proj, proj, proj)


def _attn_bwd(proj, sinks, attn_o, dmixed, dproj, lay, d):
    t = proj.shape[0]
    n_groups = d // ATTN_GROUP_LANES
    pairs = ATTN_GROUP_LANES // LANES
    kv_w = n_groups * LANES

    def body(sink_ref, qg_ref, kp_ref, kc_ref, vp_ref, vc_ref, o_ref, dm_ref, dproj_hbm,
             dqg_ref, dkc_ref, dkp_ref, dvc_ref, dvp_ref, dsink_ref):
        del dproj_hbm
        m = pl.program_id(0)
        n = pl.program_id(1)
        half = _lane_half()
        q_tiles = [qg_ref[:, p * LANES:(p + 1) * LANES] for p in range(pairs)]
        do_tiles, o_tiles = [], []
        for p in range(pairs):
            lanes = slice(p * LANES, (p + 1) * LANES)
            gate = qg_ref[:, ATTN_GROUP_LANES + p * LANES:ATTN_GROUP_LANES + (p + 1) * LANES]
            sg = _sigmoid(gate)
            dmix = dm_ref[:, lanes]
            ov = o_ref[:, lanes]
            dqg_ref[:, ATTN_GROUP_LANES + p * LANES:ATTN_GROUP_LANES + (p + 1) * LANES] = (
                dmix * ov * (sg * (1.0 + gate * (1.0 - sg)))).astype(BF16)
            do_tiles.append(dmix * (gate * sg))
            o_tiles.append(ov)

        sub = lax.broadcasted_iota(jnp.int32, (8, LANES), 0)
        lane = lax.broadcasted_iota(jnp.int32, (8, LANES), 1)
        dsink = jnp.zeros((8, LANES), F32)
        dk_cur = dk_prev = dv_cur = dv_prev = jnp.zeros((WINDOW, LANES), F32)
        for kh in range(2):
            qs, kcat, vcat, sink_col = _attn_operands(sink_ref, q_tiles, kp_ref, kc_ref, vp_ref, vc_ref, m, kh)
            probs, p_sink = _attn_probs(qs, kcat, sink_col, n)
            dos = _stack_heads(do_tiles, kh)
            delta = jnp.sum(dos * _stack_heads(o_tiles, kh), axis=-1, keepdims=True)
            dos = dos.astype(BF16)
            dp = _dot_nt(dos, vcat)
            ds = (probs * (dp - delta)).astype(BF16)
            dv = _dot_tn(probs.astype(BF16), dos)
            dv = dv + pltpu.roll(dv, ATTN_HEAD_DIM, 1)
            dk = _dot_tn(ds, qs)
            dk = dk + pltpu.roll(dk, ATTN_HEAD_DIM, 1)
            dq = _dot(ds, kcat)
            for j, tile in enumerate(_unstack_heads(dq)):
                p = 4 * kh + j
                dqg_ref[:, p * LANES:(p + 1) * LANES] = (tile * ATTN_SCALE).astype(BF16)
            dk_prev = jnp.where(half == kh, dk[:WINDOW], dk_prev)
            dk_cur = jnp.where(half == kh, dk[WINDOW:], dk_cur)
            dv_prev = jnp.where(half == kh, dv[:WINDOW], dv_prev)
            dv_cur = jnp.where(half == kh, dv[WINDOW:], dv_cur)
            sink_terms = p_sink * delta
            for g in range(GQA_GROUP):
                val = -jnp.sum(sink_terms[g * WINDOW:(g + 1) * WINDOW])
                dsink = dsink + jnp.where((sub == 0) & (lane == kh * GQA_GROUP + g), val, 0.0)
        dkc_ref[...] = dk_cur
        dkp_ref[...] = dk_prev
        dvc_ref[...] = dv_cur
        dvp_ref[...] = dv_prev

        @pl.when(n == 0)
        def _():
            dsink_ref[...] = jnp.zeros_like(dsink_ref)

        dsink_ref[...] += dsink

    qg, kp, kc, vp, vc = _attn_specs(lay, d)
    a_blk = lay["a_off"] // (2 * ATTN_GROUP_LANES)
    grp = pl.BlockSpec((WINDOW, ATTN_GROUP_LANES), lambda m, n: (n, m))
    kv_blk = pl.BlockSpec((WINDOW, LANES), lambda m, n: (n, m))
    kv_shape = jax.ShapeDtypeStruct((t, kv_w), F32)
    outs = pl.pallas_call(
        body, name="attn_bwd", grid=(n_groups, t // WINDOW),
        in_specs=[pl.BlockSpec(memory_space=pltpu.SMEM), qg, kp, kc, vp, vc, grp, grp,
                  pl.BlockSpec(memory_space=pl.ANY)],
        out_specs=[pl.BlockSpec((WINDOW, 2 * ATTN_GROUP_LANES), lambda m, n: (n, a_blk + m)),
                   kv_blk, kv_blk, kv_blk, kv_blk, pl.BlockSpec((8, LANES), lambda m, n: (m, 0))],
        out_shape=[jax.ShapeDtypeStruct(dproj.shape, BF16), kv_shape, kv_shape, kv_shape, kv_shape,
                   jax.ShapeDtypeStruct((n_groups * 8, LANES), F32)],
        input_output_aliases={8: 0},
        compiler_params=_params(dimension_semantics=("parallel", "arbitrary")),
    )(sinks, proj, proj, proj, proj, proj, attn_o, dmixed, dproj)
    return outs


def _kv_combine(dkc, dkp, dvc, dvp, dproj, lay):
    t, kv_w = dkc.shape
    nb = t // WINDOW
    kv_blk_idx = lay["k_off"] // (2 * kv_w)

    def body(dkc_ref, dkp_ref, dvc_ref, dvp_ref, dproj_hbm, o_ref):
        del dproj_hbm
        keep = (pl.program_id(0) < nb - 1).astype(F32)
        o_ref[:, :kv_w] = (dkc_ref[...] + keep * dkp_ref[...]).astype(BF16)
        o_ref[:, kv_w:] = (dvc_ref[...] + keep * dvp_ref[...]).astype(BF16)

    cur = pl.BlockSpec((WINDOW, kv_w), lambda n: (n, 0))
    nxt = pl.BlockSpec((WINDOW, kv_w), lambda n: (jnp.minimum(n + 1, nb - 1), 0))
    return pl.pallas_call(
        body, name="kv_combine", grid=(nb,),
        in_specs=[cur, nxt, cur, nxt, pl.BlockSpec(memory_space=pl.ANY)],
        out_specs=pl.BlockSpec((WINDOW, 2 * kv_w), lambda n: (n, kv_blk_idx)),
        out_shape=jax.ShapeDtypeStruct(dproj.shape, BF16),
        input_output_aliases={4: 0},
        compiler_params=_params(dimension_semantics=("parallel",)),
    )(dkc, dkp, dvc, dvp, dproj)


def _lower_bound(lbl_ref):
    l0 = lbl_ref[0:1, :]
    l1 = lbl_ref[1:2, :]
    mx = jnp.maximum(l0, l1)
    e0 = jnp.exp(l0 - mx)
    e1 = jnp.exp(l1 - mx)
    return e0 / (e0 + e1)


def _chunk_masks():
    ti = lax.broadcasted_iota(jnp.int32, (CHUNK, CHUNK), 0)
    si = lax.broadcasted_iota(jnp.int32, (CHUNK, CHUNK), 1)
    diag = ((ti // HALF_CHUNK) == (si // HALF_CHUNK)) & (si <= ti)
    off = (ti >= HALF_CHUNK) & (si < HALF_CHUNK)
    lower = (si <= ti).astype(BF16)
    upper = (si >= ti).astype(BF16)
    return diag, off, lower, upper


def _rnn_gates(rq, rf, lb):
    sf = _sigmoid(rf)
    f = lb + (1.0 - lb) * sf
    sq = _sigmoid(rq)
    return sf, f, jnp.log(f), 1.0 - f, sq, rq * sq


def _rnn_decays(g_cum):
    row = lax.broadcasted_iota(jnp.int32, g_cum.shape, 0)
    ref_d = jnp.where(row < HALF_CHUNK, g_cum[HALF_CHUNK // 2 - 1:HALF_CHUNK // 2],
                      g_cum[HALF_CHUNK + HALF_CHUNK // 2 - 1:HALF_CHUNK + HALF_CHUNK // 2])
    ref_o = g_cum[HALF_CHUNK - 1:HALF_CHUNK]
    last = g_cum[CHUNK - 1:CHUNK]
    return dict(eq_d=jnp.exp(g_cum - ref_d), ek_d=jnp.exp(ref_d - g_cum),
                eq_o=jnp.exp(jnp.minimum(g_cum - ref_o, 0.0)), ek_o=jnp.exp(jnp.minimum(ref_o - g_cum, 0.0)),
                eg=jnp.exp(g_cum), ekl=jnp.exp(last - g_cum), e_last=jnp.exp(last))


def _head(a, j):
    return a[:, j * RNN_HEAD_DIM:(j + 1) * RNN_HEAD_DIM]


def _rnn_specs(t, tb, d):
    gw = RNN_GROUP_HEADS * RNN_HEAD_DIM
    return gw, t // tb, tb // CHUNK


def _rnn_fwd(proj, lb_logits, rnn_gain, mixed, d):
    t = proj.shape[0]
    tb = min(t, 256)
    gw, ntb, nch = _rnn_specs(t, tb, d)
    n_groups = d // gw
    n_heads = d // RNN_HEAD_DIM

    def body(blk_ref, lbl_ref, gain_ref, mixed_hbm, mix_ref, o_ref, st_out_ref, st_ref):
        del mixed_hbm

        @pl.when(pl.program_id(1) == 0)
        def _():
            st_ref[...] = jnp.zeros_like(st_ref)

        lb = _lower_bound(lbl_ref)
        gain = gain_ref[...]
        diag, off, lower, _ = _chunk_masks()

        def chunk(c, carry):
            rows = pl.ds(pl.multiple_of(c * CHUNK, CHUNK), CHUNK)
            rq = blk_ref[rows, 0:gw]
            rf = blk_ref[rows, gw:2 * gw]
            v = blk_ref[rows, 2 * gw:3 * gw]
            rg = blk_ref[rows, 3 * gw:4 * gw]
            _, _, g, k, _, q = _rnn_gates(rq, rf, lb)
            dec = _rnn_decays(_tri_dot(lower, g))
            qd = (q * dec["eq_d"]).astype(BF16)
            kd = (k * dec["ek_d"]).astype(BF16)
            qo = (q * dec["eq_o"]).astype(BF16)
            ko = (k * dec["ek_o"]).astype(BF16)
            qe = (q * dec["eg"]).astype(BF16)
            kl = (k * dec["ekl"]).astype(BF16)
            vb = v.astype(BF16)
            outs = []
            for j in range(RNN_GROUP_HEADS):
                st = st_ref[j]
                st_out_ref[j, c] = st
                attn = jnp.where(diag, _dot_nt(_head(qd, j), _head(kd, j)),
                                 jnp.where(off, _dot_nt(_head(qo, j), _head(ko, j)), 0.0))
                o = _dot(attn.astype(BF16), _head(vb, j)) + _dot_nt(_head(qe, j), st.astype(BF16))
                st_ref[j] = st * _head(dec["e_last"], j) + _dot_tn(_head(vb, j), _head(kl, j))
                rr = lax.rsqrt(jnp.mean(o * o, axis=-1, keepdims=True) + NORM_EPS)
                o_ref[rows, j * RNN_HEAD_DIM:(j + 1) * RNN_HEAD_DIM] = o
                outs.append(o * rr)
            on = jnp.concatenate(outs, axis=1) * gain
            mix_ref[rows, :] = (on * (rg * _sigmoid(rg))).astype(BF16)
            return carry

        lax.fori_loop(0, nch, chunk, 0)

    return pl.pallas_call(
        body, name="rnn_fwd", grid=(n_groups, ntb),
        in_specs=[pl.BlockSpec((tb, 4 * gw), lambda h, i: (i, h)),
                  pl.BlockSpec((2, gw), lambda h, i: (0, h)),
                  pl.BlockSpec((1, gw), lambda h, i: (0, h)),
                  pl.BlockSpec(memory_space=pl.ANY)],
        out_specs=[pl.BlockSpec((tb, gw), lambda h, i: (i, d // gw + h)),
                   pl.BlockSpec((tb, gw), lambda h, i: (i, h)),
                   pl.BlockSpec((RNN_GROUP_HEADS, nch, RNN_HEAD_DIM, RNN_HEAD_DIM), lambda h, i: (h, i, 0, 0))],
        out_shape=[jax.ShapeDtypeStruct(mixed.shape, BF16), jax.ShapeDtypeStruct((t, d), F32),
                   jax.ShapeDtypeStruct((n_heads, t // CHUNK, RNN_HEAD_DIM, RNN_HEAD_DIM), F32)],
        scratch_shapes=[pltpu.VMEM((RNN_GROUP_HEADS, RNN_HEAD_DIM, RNN_HEAD_DIM), F32)],
        input_output_aliases={3: 0},
        compiler_params=_params(dimension_semantics=("parallel", "arbitrary")),
    )(proj, lb_logits, rnn_gain, mixed)


def _rnn_bwd(proj, lb_logits, rnn_gain, o_pre, states, dmixed, d_total, d, rider=None):
    t = proj.shape[0]
    tb = min(t, 256)
    gw, ntb, nch = _rnn_specs(t, tb, d)
    n_groups = d // gw
    n_rin = 0 if rider is None else len(rider.operands)
    n_rout = 0 if rider is None else len(rider.out_shapes)

    def body(*refs):
        blk_ref, lbl_ref, gain_ref, o_ref, st_in_ref, dm_ref = refs[:6]
        rin = refs[6:6 + n_rin]
        dproj_ref, dgain_ref, dlb_ref = refs[6 + n_rin:9 + n_rin]
        rout = refs[9 + n_rin:9 + n_rin + n_rout]
        dst_ref = refs[9 + n_rin + n_rout]
        if rider is not None:
            rider.emit(pl.program_id(0) * ntb + pl.program_id(1), n_groups * ntb, rin, rout, refs[-2], refs[-1])

        @pl.when(pl.program_id(1) == 0)
        def _():
            dst_ref[...] = jnp.zeros_like(dst_ref)
            dgain_ref[...] = jnp.zeros_like(dgain_ref)
            dlb_ref[...] = jnp.zeros_like(dlb_ref)

        lb = _lower_bound(lbl_ref)
        gain = gain_ref[...]
        diag, off, lower, upper = _chunk_masks()
        last_row = lax.broadcasted_iota(jnp.int32, (CHUNK, RNN_HEAD_DIM), 0) == CHUNK - 1

        def chunk(step, carry):
            c = nch - 1 - step
            rows = pl.ds(pl.multiple_of(c * CHUNK, CHUNK), CHUNK)
            rq = blk_ref[rows, 0:gw]
            rf = blk_ref[rows, gw:2 * gw]
            v = blk_ref[rows, 2 * gw:3 * gw]
            rg = blk_ref[rows, 3 * gw:4 * gw]
            sf, f, g, k, sq, q = _rnn_gates(rq, rf, lb)
            dec = _rnn_decays(_tri_dot(lower, g))
            qd = (q * dec["eq_d"]).astype(BF16)
            kd = (k * dec["ek_d"]).astype(BF16)
            qo = (q * dec["eq_o"]).astype(BF16)
            ko = (k * dec["ek_o"]).astype(BF16)
            qe = (q * dec["eg"]).astype(BF16)
            kl = (k * dec["ekl"]).astype(BF16)
            vb = v.astype(BF16)

            o = o_ref[rows, :]
            dmix = dm_ref[rows, :]
            sg = _sigmoid(rg)
            n_parts = []
            for j in range(RNN_GROUP_HEADS):
                oj = _head(o, j)
                n_parts.append(oj * lax.rsqrt(jnp.mean(oj * oj, axis=-1, keepdims=True) + NORM_EPS))
            nrm = jnp.concatenate(n_parts, axis=1)
            d_on = dmix * (rg * sg)
            d_rg = dmix * (nrm * gain) * (sg * (1.0 + rg * (1.0 - sg)))
            dgain_ref[...] += jnp.sum(d_on * nrm, axis=0, keepdims=True)
            dn = d_on * gain

            dq_parts, dk_parts, dv_parts, dg_parts = [], [], [], []
            for j in range(RNN_GROUP_HEADS):
                oj, nj, dnj = _head(o, j), _head(nrm, j), _head(dn, j)
                rr = lax.rsqrt(jnp.mean(oj * oj, axis=-1, keepdims=True) + NORM_EPS)
                do = (rr * (dnj - nj * jnp.mean(dnj * nj, axis=-1, keepdims=True))).astype(BF16)
                st = st_in_ref[j, c]
                dst = dst_ref[j]
                stb, dstb = st.astype(BF16), dst.astype(BF16)
                qdj, kdj, qoj, koj = _head(qd, j), _head(kd, j), _head(qo, j), _head(ko, j)
                attn = jnp.where(diag, _dot_nt(qdj, kdj), jnp.where(off, _dot_nt(qoj, koj), 0.0))
                dattn = _dot_nt(do, _head(vb, j))
                da_d = jnp.where(diag, dattn, 0.0).astype(BF16)
                da_o = jnp.where(off, dattn, 0.0).astype(BF16)
                dv = _dot_tn(attn.astype(BF16), do) + _dot_nt(_head(kl, j), dstb)
                dq_inter = _dot(do, stb) * _head(dec["eg"], j)
                dq_d, dq_o = _dot(da_d, kdj), _dot(da_o, koj)
                dq = dq_inter + dq_d * _head(dec["eq_d"], j) + dq_o * _head(dec["eq_o"], j)
                dk_inter = _dot(_head(vb, j), dstb) * _head(dec["ekl"], j)
                dk_d, dk_o = _dot_tn(da_d, qdj), _dot_tn(da_o, qoj)
                dk = dk_inter + dk_d * _head(dec["ek_d"], j) + dk_o * _head(dec["ek_o"], j)
                kj, qj = _head(k, j), _head(q, j)
                e_last = _head(dec["e_last"], j)
                extra = (jnp.sum(kj * dk_inter, axis=0, keepdims=True)
                         + e_last * jnp.sum(st * dst, axis=0, keepdims=True))
                dg_cum = (qj * dq_inter - kj * dk_inter
                          + (qdj.astype(F32) * dq_d + qoj.astype(F32) * dq_o)
                          - (kdj.astype(F32) * dk_d + koj.astype(F32) * dk_o))
                dg_parts.append(jnp.where(last_row, dg_cum + extra, dg_cum))
                dst_ref[j] = dst * e_last + _dot_tn(do, _head(qe, j))
                dq_parts.append(dq)
                dk_parts.append(dk)
                dv_parts.append(dv)

            dq = jnp.concatenate(dq_parts, axis=1)
            dk = jnp.concatenate(dk_parts, axis=1)
            dg = _tri_dot(upper, jnp.concatenate(dg_parts, axis=1))
            df = dg / f - dk
            dlb_ref[...] += jnp.sum(df * (1.0 - sf), axis=0, keepdims=True)
            d_rf = df * (1.0 - lb) * (sf * (1.0 - sf))
            d_rq = dq * (sq * (1.0 + rq * (1.0 - sq)))
            dproj_ref[rows, 0:gw] = d_rq.astype(BF16)
            dproj_ref[rows, gw:2 * gw] = d_rf.astype(BF16)
            dproj_ref[rows, 2 * gw:3 * gw] = jnp.concatenate(dv_parts, axis=1).astype(BF16)
            dproj_ref[rows, 3 * gw:4 * gw] = d_rg.astype(BF16)
            return carry

        lax.fori_loop(0, nch, chunk, 0)

    rev = lambda i: ntb - 1 - i
    vec = pl.BlockSpec((1, gw), lambda h, i: (0, h))
    scratch = [pltpu.VMEM((RNN_GROUP_HEADS, RNN_HEAD_DIM, RNN_HEAD_DIM), F32)]
    out_shapes = [jax.ShapeDtypeStruct((t, d_total), BF16), jax.ShapeDtypeStruct((1, d), F32),
                  jax.ShapeDtypeStruct((1, d), F32)]
    operands = [proj, lb_logits, rnn_gain, o_pre, states, dmixed]
    sem = ("parallel", "arbitrary")
    if rider is not None:
        scratch += rider.scratch()
        out_shapes += list(rider.out_shapes)
        operands += list(rider.operands)
        sem = ("arbitrary", "arbitrary")
    return pl.pallas_call(
        body, name="rnn_bwd", grid=(n_groups, ntb),
        in_specs=[pl.BlockSpec((tb, 4 * gw), lambda h, i: (rev(i), h)),
                  pl.BlockSpec((2, gw), lambda h, i: (0, h)), vec,
                  pl.BlockSpec((tb, gw), lambda h, i: (rev(i), h)),
                  pl.BlockSpec((RNN_GROUP_HEADS, nch, RNN_HEAD_DIM, RNN_HEAD_DIM), lambda h, i: (h, rev(i), 0, 0)),
                  pl.BlockSpec((tb, gw), lambda h, i: (rev(i), d // gw + h))] + [HBM_SPEC] * n_rin,
        out_specs=[pl.BlockSpec((tb, 4 * gw), lambda h, i: (rev(i), h)), vec, vec] + [HBM_SPEC] * n_rout,
        out_shape=out_shapes, scratch_shapes=scratch,
        compiler_params=_params(dimension_semantics=sem),
    )(*operands)


def _local_grads(x, target, w_in_full, w_out, sinks, lb_logits, rnn_gain, pre_gain, post_gain, sc=None):
    t, d = x.shape
    comm = sc is not None
    lay = _layout(d)
    perm = lay["perm"]
    h, ht = _prenorm_fwd(x, pre_gain)
    proj, *gathered = _proj_mm(h, w_in_full, perm, rider=_gather_rider(w_out, 0) if comm else None)
    w_out_full = gathered[0] if comm else w_out
    mixed, attn_o = _attn_fwd(proj, sinks, lay, d)
    mixed, o_pre, states = _rnn_fwd(proj, lb_logits, rnn_gain, mixed, d)
    y = _out_mm(mixed, w_out_full)
    dy, dz, g_post, loss = _post_loss(x, y, target, post_gain)
    dmixed = _dmixed_mm(dy, w_out_full)
    gw_out = _gw_out_mm(_transpose_bf16(mixed, "mixed_t"), dy)
    rider = None
    if comm:
        p_out = _pair_sum_out(gw_out, _pair_exchange(gw_out, "pair_exchange_out"))
        rider = _chip_exchange_rider(p_out, lambda ref, chip: ref.at[chip])
    dproj, g_rnn, g_lb, *r_out = _rnn_bwd(proj, lb_logits, rnn_gain, o_pre, states, dmixed, lay["total"], d, rider)
    dproj, dkc, dkp, dvc, dvp, dsink = _attn_bwd(proj, sinks, attn_o, dmixed, dproj, lay, d)
    dproj = _kv_combine(dkc, dkp, dvc, dvp, dproj, lay)
    gw_in = _gw_in_mm(ht, dproj, perm)
    if comm:
        p_in = _pair_sum_in(gw_in, _pair_exchange(gw_in, "pair_exchange_in"), sc)
        rider = _chip_exchange_rider(
            p_in, lambda ref, chip: ref.at[:, pl.ds(pl.multiple_of(chip * sc, LANES), sc)])
    dh, *r_in = _dh_mm(dproj, w_in_full, perm, rider)
    grad_x, g_pre = _prenorm_bwd(x, dh, dz, pre_gain)
    heads_per_group = ATTN_GROUP_LANES // ATTN_HEAD_DIM
    g_sink = dsink.reshape(d // ATTN_GROUP_LANES, 8, LANES)[:, 0, :heads_per_group].reshape(1, -1)
    small = dict(sink=g_sink, lb=g_lb, rnn=g_rnn, pre=g_pre, post=g_post)
    if comm:
        return loss, grad_x, (p_in, r_in[0]), (p_out, r_out[0]), small
    return loss, grad_x, gw_in, gw_out, small


def _mesh_pos():
    x, y, c = lax.axis_index("x"), lax.axis_index("y"), lax.axis_index("c")
    chips = [(1 - x, y), (x, 1 - y), (1 - x, 1 - y)]
    return x, y, c, chips


def _remote(src, dst, send_sem, recv_sem, device):
    return pltpu.make_async_remote_copy(src_ref=src, dst_ref=dst, send_sem=send_sem, recv_sem=recv_sem,
                                        device_id=device, device_id_type=MESH)


HBM_SPEC = pl.BlockSpec(memory_space=pl.ANY)


def _gather_rider(part, axis):
    rows = part.shape[0] if axis == 1 else part.shape[0] // N_CHIPS
    cols = part.shape[1] // N_CHIPS if axis == 1 else part.shape[1]
    half_rows = rows // 2

    def stages(ins, outs, send_sems, recv_sems):
        del ins
        full = outs[0]

        def piece(chip, half):
            if axis == 1:
                return full.at[pl.ds(half * half_rows, half_rows), pl.ds(pl.multiple_of(chip * cols, LANES), cols)]
            return full.at[pl.ds(pl.multiple_of(chip * rows + half * half_rows, 8), half_rows), :]

        def sends():
            x, y, c, chips = _mesh_pos()
            mine = piece(2 * x + y, c)
            return [_remote(mine, mine, send_sems.at[j], recv_sems.at[j], (px, py, c))
                    for j, (px, py) in enumerate(chips)]

        def forwards(half_of):
            x, y, c, chips = _mesh_pos()
            out = []
            for j, (px, py) in enumerate(chips):
                block = piece(2 * px + py, half_of(c))
                out.append(_remote(block, block, send_sems.at[3 + j], recv_sems.at[3 + j], (x, y, 1 - c)))
            return out

        def start():
            for cp in sends():
                cp.start()

        def forward():
            x, y, c, chips = _mesh_pos()
            for j, (px, py) in enumerate(chips):
                landed = piece(2 * px + py, c)
                _remote(landed, landed, send_sems.at[j], recv_sems.at[j], (x, y, 1 - c)).wait_recv()
            for cp in forwards(lambda c: c):
                cp.start()

        def finish():
            for cp in forwards(lambda c: 1 - c):
                cp.wait_recv()
            for cp in sends() + forwards(lambda c: c):
                cp.wait_send()

        return [(0.0, start), (0.6, forward), (1.0, finish)]

    return _Rider((part,), (jax.ShapeDtypeStruct(part.shape, BF16),), {0: 0}, 6, stages)


def _chip_exchange_rider(partial, piece):
    if partial.ndim == 3:
        recv_shape = (N_CHIPS - 1,) + partial.shape[1:]
    else:
        recv_shape = (N_CHIPS - 1, partial.shape[0], partial.shape[1] // N_CHIPS)

    def stages(ins, outs, send_sems, recv_sems):
        def copies():
            x, y, c, chips = _mesh_pos()
            return [_remote(piece(ins[0], 2 * px + py), outs[0].at[j], send_sems.at[j], recv_sems.at[j], (px, py, c))
                    for j, (px, py) in enumerate(chips)]

        def start():
            for cp in copies():
                cp.start()

        def finish():
            for cp in copies():
                cp.wait()

        return [(0.0, start), (1.0, finish)]

    return _Rider((partial,), (jax.ShapeDtypeStruct(recv_shape, BF16),), {}, N_CHIPS - 1, stages)


def _pair_exchange(g, name):
    def body(g_ref, r_ref, send_sems, recv_sems):
        x, y, c, _ = _mesh_pos()
        cp = _remote(g_ref.at[1 - c], r_ref, send_sems.at[0], recv_sems.at[0], (x, y, 1 - c))
        cp.start()
        cp.wait()

    return pl.pallas_call(
        body, name=name, in_specs=[HBM_SPEC], out_specs=HBM_SPEC,
        out_shape=jax.ShapeDtypeStruct(g.shape[1:], BF16),
        scratch_shapes=[pltpu.SemaphoreType.DMA((1,)), pltpu.SemaphoreType.DMA((1,))],
    )(g)


def _pair_sum_in(gw_in, recv, sc):
    _, hd, d_in = gw_in.shape
    tr = min(hd, 256)
    c = lax.axis_index("c")

    def body(c_ref, a_ref, b_ref, o_ref):
        del c_ref
        o_ref[...] = (a_ref[...].astype(F32) + b_ref[...].astype(F32)).astype(BF16)

    blk = pl.BlockSpec((tr, sc), lambda i, j, cc: (i, j))
    gs = pltpu.PrefetchScalarGridSpec(
        num_scalar_prefetch=1, grid=(hd // tr, d_in // sc),
        in_specs=[pl.BlockSpec((None, tr, sc), lambda i, j, cc: (cc[0], i, j)), blk], out_specs=blk)
    return pl.pallas_call(
        body, name="pair_sum_in", grid_spec=gs, out_shape=jax.ShapeDtypeStruct((hd, d_in), BF16),
        compiler_params=_params(dimension_semantics=("parallel", "parallel")),
    )(jnp.reshape(c, (1,)).astype(jnp.int32), gw_in, recv)


def _pair_sum_out(gw_out, recv):
    _, n_chips, hr, d = gw_out.shape
    tr = min(hr, 256)
    c = lax.axis_index("c")

    def body(c_ref, a_ref, b_ref, o_ref):
        del c_ref
        o_ref[...] = (a_ref[...].astype(F32) + b_ref[...].astype(F32)).astype(BF16)

    blk = pl.BlockSpec((None, tr, d), lambda k, i, cc: (k, i, 0))
    gs = pltpu.PrefetchScalarGridSpec(
        num_scalar_prefetch=1, grid=(n_chips, hr // tr),
        in_specs=[pl.BlockSpec((None, None, tr, d), lambda k, i, cc: (cc[0], k, i, 0)), blk], out_specs=blk)
    return pl.pallas_call(
        body, name="pair_sum_out", grid_spec=gs, out_shape=jax.ShapeDtypeStruct((n_chips, hr, d), BF16),
        compiler_params=_params(dimension_semantics=("parallel", "parallel")),
    )(jnp.reshape(c, (1,)).astype(jnp.int32), gw_out, recv)


def _place():
    return jnp.stack([2 * lax.axis_index("x") + lax.axis_index("y"), lax.axis_index("c")]).astype(jnp.int32)


def _chip_sum_in(p_in, r_in, sc):
    hd = p_in.shape[0]
    tr = min(hd, 256)
    nblk = hd // tr

    def body(pos_ref, p_ref, r_ref, o_ref):
        del pos_ref
        acc = p_ref[...].astype(F32)
        for j in range(3):
            acc = acc + r_ref[j].astype(F32)
        o_ref[...] = acc

    gs = pltpu.PrefetchScalarGridSpec(
        num_scalar_prefetch=1, grid=(nblk,),
        in_specs=[pl.BlockSpec((tr, sc), lambda i, pos: (i, pos[0])), pl.BlockSpec((3, tr, sc), lambda i, pos: (0, i, 0))],
        out_specs=pl.BlockSpec((tr, sc), lambda i, pos: (pos[1] * nblk + i, 0)))
    return pl.pallas_call(
        body, name="chip_sum_in", grid_spec=gs, out_shape=jax.ShapeDtypeStruct((2 * hd, sc), F32),
        compiler_params=_params(dimension_semantics=("parallel",)),
    )(_place(), p_in, r_in)


def _chip_sum_out(p_out, r_out):
    _, hr, d = p_out.shape
    tr = min(hr, 256)
    nblk = hr // tr

    def body(pos_ref, p_ref, r_ref, o_ref):
        del pos_ref
        acc = p_ref[...].astype(F32)
        for j in range(3):
            acc = acc + r_ref[j].astype(F32)
        o_ref[...] = acc

    gs = pltpu.PrefetchScalarGridSpec(
        num_scalar_prefetch=1, grid=(nblk,),
        in_specs=[pl.BlockSpec((None, tr, d), lambda i, pos: (pos[0], i, 0)), pl.BlockSpec((3, tr, d), lambda i, pos: (0, i, 0))],
        out_specs=pl.BlockSpec((tr, d), lambda i, pos: (pos[1] * nblk + i, 0)))
    return pl.pallas_call(
        body, name="chip_sum_out", grid_spec=gs, out_shape=jax.ShapeDtypeStruct((2 * hr, d), F32),
        compiler_params=_params(dimension_semantics=("parallel",)),
    )(_place(), p_out, r_out)


def _share_halves(g_in, g_out):
    hd = g_in.shape[0] // 2
    hr = g_out.shape[0] // 2

    def body(gi_in, go_in, gi_ref, go_ref, send_sems, recv_sems):
        del gi_in, go_in
        x, y, c, _ = _mesh_pos()
        sibling = (x, y, 1 - c)
        mine_i = gi_ref.at[pl.ds(pl.multiple_of(c * hd, 8), hd), :]
        mine_o = go_ref.at[pl.ds(pl.multiple_of(c * hr, 8), hr), :]
        a = _remote(mine_i, mine_i, send_sems.at[0], recv_sems.at[0], sibling)
        b = _remote(mine_o, mine_o, send_sems.at[1], recv_sems.at[1], sibling)
        a.start()
        b.start()
        a.wait_send()
        b.wait_send()
        theirs_i = gi_ref.at[pl.ds(pl.multiple_of((1 - c) * hd, 8), hd), :]
        theirs_o = go_ref.at[pl.ds(pl.multiple_of((1 - c) * hr, 8), hr), :]
        _remote(theirs_i, theirs_i, send_sems.at[0], recv_sems.at[0], sibling).wait_recv()
        _remote(theirs_o, theirs_o, send_sems.at[1], recv_sems.at[1], sibling).wait_recv()

    return pl.pallas_call(
        body, name="share_halves",
        in_specs=[HBM_SPEC, HBM_SPEC], out_specs=[HBM_SPEC, HBM_SPEC],
        out_shape=[jax.ShapeDtypeStruct(g_in.shape, F32), jax.ShapeDtypeStruct(g_out.shape, F32)],
        input_output_aliases={0: 0, 1: 1},
        scratch_shapes=[pltpu.SemaphoreType.DMA((2,)), pltpu.SemaphoreType.DMA((2,))],
    )(g_in, g_out)


def _adamw_math(w, g, m, v):
    m_new = ADAM_B1 * m + (1.0 - ADAM_B1) * g
    v_new = ADAM_B2 * v + (1.0 - ADAM_B2) * (g * g)
    m_hat = m_new / (1.0 - ADAM_B1 ** ADAM_STEP)
    v_hat = v_new / (1.0 - ADAM_B2 ** ADAM_STEP)
    delta = -ADAM_LR * (m_hat / (jnp.sqrt(v_hat) + ADAM_EPS) + ADAM_WD * w)
    return delta, m_new, v_new


def _adamw(w, g, m, v, name):
    rows, cols = w.shape
    tr = min(rows, 128)

    def body(w_ref, g_ref, m_ref, v_ref, d_ref, mo_ref, vo_ref):
        delta, m_new, v_new = _adamw_math(w_ref[...], g_ref[...], m_ref[...], v_ref[...])
        d_ref[...] = delta
        mo_ref[...] = m_new
        vo_ref[...] = v_new

    spec = pl.BlockSpec((tr, cols), lambda i: (i, 0))
    shape = jax.ShapeDtypeStruct((rows, cols), F32)
    return pl.pallas_call(
        body, name=name, grid=(rows // tr,), in_specs=[spec] * 4, out_specs=[spec] * 3,
        out_shape=[shape] * 3, compiler_params=_params(dimension_semantics=("parallel",)),
    )(w, g, m, v)


SMALL_ROWS = 8


def _small_allreduce_adamw(part, w_pack, m_pack, v_pack):
    d = part.shape[1]

    def body(part_ref, w_ref, m_ref, v_ref, g_ref, d_ref, mo_ref, vo_ref, buf_ref, send_sems, recv_sems):
        x, y, c, _ = _mesh_pos()
        me = 4 * x + 2 * y + c
        buf_ref[0] = part_ref[...]
        copies = []
        for r in range(1, 8):
            rx, ry, rc = (r >> 2) & 1, (r >> 1) & 1, r & 1
            peer = (x ^ rx, y ^ ry, c ^ rc)
            copies.append(_remote(buf_ref.at[0], buf_ref.at[r], send_sems.at[r - 1], recv_sems.at[r - 1], peer))
        for cp in copies:
            cp.start()
        for cp in copies:
            cp.wait()
        total = buf_ref[me]
        for s in range(1, 8):
            total = total + buf_ref[s ^ me]
        w = w_ref[...]
        row = lax.broadcasted_iota(jnp.int32, (SMALL_ROWS, d), 0)
        l0, l1 = w[3:4], w[4:5]
        mx = jnp.maximum(l0, l1)
        e0, e1 = jnp.exp(l0 - mx), jnp.exp(l1 - mx)
        lb = e0 / (e0 + e1)
        g_l0 = total[3:4] * lb * (1.0 - lb)
        grads = jnp.where(row == 3, g_l0, jnp.where(row == 4, -g_l0, total))
        g_ref[...] = grads
        delta, m_new, v_new = _adamw_math(w, grads, m_ref[...], v_ref[...])
        d_ref[...] = delta
        mo_ref[...] = m_new
        vo_ref[...] = v_new

    vm = pl.BlockSpec(memory_space=pltpu.VMEM)
    shape = jax.ShapeDtypeStruct((SMALL_ROWS, d), F32)
    return pl.pallas_call(
        body, name="small_allreduce_adamw",
        in_specs=[vm] * 4, out_specs=[vm] * 4, out_shape=[shape] * 4,
        scratch_shapes=[pltpu.VMEM((8, SMALL_ROWS, d), F32), pltpu.SemaphoreType.DMA((7,)), pltpu.SemaphoreType.DMA((7,))],
    )(part, w_pack, m_pack, v_pack)


def _pack_small(d, pre, post, rnn, lb, sink, extra=None):
    rows = [pre, post, rnn, lb[0:1], lb[1:2],
            jnp.pad(sink, ((0, 0), (0, d - sink.shape[1]))),
            jnp.zeros((1, d), F32) if extra is None else extra,
            jnp.zeros((1, d), F32)]
    return jnp.concatenate(rows, axis=0)


def _unpack_small(p, n_sink):
    return dict(pre=p[0:1], post=p[1:2], rnn=p[2:3], lb=p[3:5], sink=p[5:6, :n_sink])


def kernel(x, w_in, attn_sinks, lb_logits, rnn_norm, w_out, pre_norm, post_norm, loss_target, m_w_in, m_attn_sinks, m_lb_logits, m_rnn_norm, m_w_out, m_pre_norm, m_post_norm, v_w_in, v_attn_sinks, v_lb_logits, v_rnn_norm, v_w_out, v_pre_norm, v_post_norm):
    t, d = x.shape[1], x.shape[2]
    sc = w_in.shape[2]
    n_sink = attn_sinks.shape[1]
    w_in2, w_out2 = w_in[0], w_out[0]

    (w_in_full,) = _run_rider_alone("gather_w_in", _gather_rider(_cast_into_gathered(w_in2, "cast_w_in", 1), 1))
    w_out_part = _cast_into_gathered(w_out2, "cast_w_out", 0)
    loss_part, grad_x, (p_in, r_in), (p_out, r_out), small = _local_grads(
        x[0], loss_target[0], w_in_full, w_out_part, attn_sinks, lb_logits, rnn_norm, pre_norm, post_norm, sc)
    g_w_in, g_w_out = _share_halves(_chip_sum_in(p_in, r_in, sc), _chip_sum_out(p_out, r_out))

    d_w_in, nm_w_in, nv_w_in = _adamw(w_in2, g_w_in, m_w_in[0], v_w_in[0], "adamw_w_in")
    d_w_out, nm_w_out, nv_w_out = _adamw(w_out2, g_w_out, m_w_out[0], v_w_out[0], "adamw_w_out")

    lb_part = jnp.concatenate([small["lb"], jnp.zeros_like(small["lb"])], axis=0)
    loss_row = jnp.pad(loss_part[:, :1], ((0, 0), (0, d - 1)))
    part = _pack_small(d, small["pre"], small["post"], small["rnn"], lb_part, small["sink"], loss_row)
    w_pack = _pack_small(d, pre_norm, post_norm, rnn_norm, lb_logits, attn_sinks)
    m_pack = _pack_small(d, m_pre_norm, m_post_norm, m_rnn_norm, m_lb_logits, m_attn_sinks)
    v_pack = _pack_small(d, v_pre_norm, v_post_norm, v_rnn_norm, v_lb_logits, v_attn_sinks)
    g_pack, d_pack, nm_pack, nv_pack = _small_allreduce_adamw(part, w_pack, m_pack, v_pack)
    loss = g_pack[6, 0]
    g, dl, nm, nv = (_unpack_small(p, n_sink) for p in (g_pack, d_pack, nm_pack, nv_pack))

    def ordered(w_in_leaf, w_out_leaf, s):
        return (w_in_leaf[None], s["sink"], s["lb"], s["rnn"], w_out_leaf[None], s["pre"], s["post"])

    return (loss, grad_x[None],
            *ordered(g_w_in, g_w_out, g), *ordered(d_w_in, d_w_out, dl),
            *ordered(nm_w_in, nm_w_out, nm), *ordered(nv_w_in, nv_w_out, nv))
```

```python
import numpy as np
import jax
import jax.numpy as jnp
from jax import lax
from jax.experimental import pallas as pl
from jax.experimental.pallas import tpu as pltpu

F32 = jnp.float32
BF16 = jnp.bfloat16
MESH = pl.DeviceIdType.MESH

NORM_EPS = 1e-6
ATTN_HEAD_DIM = 64
GQA_GROUP = 8
WINDOW = 128
RNN_HEAD_DIM = 128
CHUNK = 64
HALF_CHUNK = CHUNK // 2
ATTN_SCALE = ATTN_HEAD_DIM ** -0.5

ADAM_LR = 0.001
ADAM_B1 = 0.9
ADAM_B2 = 0.999
ADAM_EPS = 1e-08
ADAM_WD = 0.01
ADAM_STEP = 10

LANES = 128
COL_TILE = 512
RNN_GROUP_HEADS = 4
ATTN_GROUP_LANES = 1024
N_CHIPS = 4
VMEM_LIMIT_BYTES = 56 * 1024 * 1024
NEG_BIG = -1e30


def _params(**kw):
    return pltpu.CompilerParams(vmem_limit_bytes=VMEM_LIMIT_BYTES, **kw)


def _sigmoid(x):
    return 1.0 / (1.0 + jnp.exp(-x))


def _dot(a, b):
    return jnp.dot(a, b, preferred_element_type=F32)


def _dot_nt(a, b):
    return lax.dot_general(a, b, (((1,), (1,)), ((), ())), preferred_element_type=F32)


def _dot_tn(a, b):
    return lax.dot_general(a, b, (((0,), (0,)), ((), ())), preferred_element_type=F32)


def _split3(x):
    hi = x.astype(BF16)
    r1 = x - hi.astype(F32)
    mid = r1.astype(BF16)
    lo = (r1 - mid.astype(F32)).astype(BF16)
    return hi, mid, lo


def _tri_dot(tri_bf16, x):
    hi, mid, lo = _split3(x)
    return _dot(tri_bf16, hi) + _dot(tri_bf16, mid) + _dot(tri_bf16, lo)


def _layout(d_model):
    d = d_model
    dkv = d // GQA_GROUP
    orig = dict(aq=0, ak=d, av=d + dkv, ag=d + 2 * dkv)
    base = d + 2 * dkv + d
    orig.update(rq=base, rf=base + d, ri=base + 2 * d, rg=base + 3 * d)
    group_w = RNN_GROUP_HEADS * RNN_HEAD_DIM
    cols = []
    for hg in range(d // group_w):
        for seg in ("rq", "rf", "ri", "rg"):
            cols.append((orig[seg] + hg * group_w, group_w))
    for m in range(d // ATTN_GROUP_LANES):
        for seg in ("aq", "ag"):
            cols.append((orig[seg] + m * ATTN_GROUP_LANES, ATTN_GROUP_LANES))
    cols.append((orig["ak"], dkv))
    cols.append((orig["av"], dkv))
    units = []
    for start, width in cols:
        assert start % LANES == 0 and width % LANES == 0
        units += [start + u for u in range(0, width, LANES)]
    per = COL_TILE // LANES
    assert len(units) % per == 0
    tiles = []
    for t in range(len(units) // per):
        run = units[t * per:(t + 1) * per]
        assert run[0] % COL_TILE == 0 and all(run[i] == run[0] + i * LANES for i in range(per))
        tiles.append(run[0] // COL_TILE)
    return dict(a_off=4 * d, k_off=6 * d, v_off=6 * d + dkv, total=6 * d + 2 * dkv,
                perm=np.asarray(tiles, np.int32))


def _chip_index():
    return jnp.reshape(2 * lax.axis_index("x") + lax.axis_index("y"), (1,)).astype(jnp.int32)


def _cast_into_gathered(a, name, axis):
    rows, cols = a.shape
    tr = min(rows, 512)
    nblk = rows // tr

    def body(me_ref, a_ref, o_ref):
        del me_ref
        o_ref[...] = a_ref[...].astype(BF16)

    if axis == 1:
        out_spec = pl.BlockSpec((tr, cols), lambda i, me: (i, me[0]))
        shape = (rows, N_CHIPS * cols)
    else:
        out_spec = pl.BlockSpec((tr, cols), lambda i, me: (me[0] * nblk + i, 0))
        shape = (N_CHIPS * rows, cols)
    gs = pltpu.PrefetchScalarGridSpec(num_scalar_prefetch=1, grid=(nblk,),
                                      in_specs=[pl.BlockSpec((tr, cols), lambda i, me: (i, 0))], out_specs=out_spec)
    return pl.pallas_call(
        body, name=name, grid_spec=gs, out_shape=jax.ShapeDtypeStruct(shape, BF16),
        compiler_params=_params(dimension_semantics=("parallel",)),
    )(_chip_index(), a)


def _transpose_bf16(a, name):
    rows, cols = a.shape
    tr = min(rows, 256)
    tc = min(cols, 2048)

    def body(a_ref, o_ref):
        o_ref[...] = a_ref[...].astype(F32).T.astype(BF16)

    return pl.pallas_call(
        body, name=name, grid=(rows // tr, cols // tc),
        in_specs=[pl.BlockSpec((tr, tc), lambda i, j: (i, j))],
        out_specs=pl.BlockSpec((tc, tr), lambda i, j: (j, i)),
        out_shape=jax.ShapeDtypeStruct((cols, rows), BF16),
        compiler_params=_params(dimension_semantics=("parallel", "parallel")),
    )(a)


def _prenorm_fwd(x, gain):
    t, d = x.shape
    tm = min(t, 256)

    def body(x_ref, g_ref, h_ref, ht_ref):
        xv = x_ref[...]
        r = lax.rsqrt(jnp.mean(xv * xv, axis=-1, keepdims=True) + NORM_EPS)
        h = (xv * r) * g_ref[...]
        h_ref[...] = h.astype(BF16)
        ht_ref[...] = h.T.astype(BF16)

    return pl.pallas_call(
        body, name="prenorm_fwd", grid=(t // tm,),
        in_specs=[pl.BlockSpec((tm, d), lambda i: (i, 0)), pl.BlockSpec((1, d), lambda i: (0, 0))],
        out_specs=[pl.BlockSpec((tm, d), lambda i: (i, 0)), pl.BlockSpec((d, tm), lambda i: (0, i))],
        out_shape=[jax.ShapeDtypeStruct((t, d), BF16), jax.ShapeDtypeStruct((d, t), BF16)],
        compiler_params=_params(dimension_semantics=("parallel",)),
    )(x, gain)


def _post_loss(x, y, target, gain):
    t, d = x.shape
    tm = min(t, 256)
    inv_d = 1.0 / d

    def body(x_ref, y_ref, t_ref, g_ref, dy_ref, dz_ref, gp_ref, loss_ref):
        i = pl.program_id(0)
        yv = y_ref[...]
        gain_v = g_ref[...]
        r = lax.rsqrt(jnp.mean(yv * yv, axis=-1, keepdims=True) + NORM_EPS)
        n = yv * r
        e = (x_ref[...] + n * gain_v) - t_ref[...]
        dz = e * inv_d
        dn = dz * gain_v
        dy = r * (dn - n * jnp.mean(dn * n, axis=-1, keepdims=True))
        dy_ref[...] = dy.astype(BF16)
        dz_ref[...] = dz

        @pl.when(i == 0)
        def _():
            gp_ref[...] = jnp.zeros_like(gp_ref)
            loss_ref[...] = jnp.zeros_like(loss_ref)

        gp_ref[...] += jnp.sum(dz * n, axis=0, keepdims=True)
        row = jnp.sum(e * e, axis=-1, keepdims=True)
        loss_ref[...] += jnp.full(loss_ref.shape, 0.5 * inv_d * jnp.sum(row), F32)

    row_spec = pl.BlockSpec((tm, d), lambda i: (i, 0))
    vec_spec = pl.BlockSpec((1, d), lambda i: (0, 0))
    return pl.pallas_call(
        body, name="post_loss", grid=(t // tm,),
        in_specs=[row_spec, row_spec, row_spec, vec_spec],
        out_specs=[row_spec, row_spec, vec_spec, pl.BlockSpec((1, LANES), lambda i: (0, 0))],
        out_shape=[jax.ShapeDtypeStruct((t, d), BF16), jax.ShapeDtypeStruct((t, d), F32),
                   jax.ShapeDtypeStruct((1, d), F32), jax.ShapeDtypeStruct((1, LANES), F32)],
        compiler_params=_params(dimension_semantics=("arbitrary",)),
    )(x, y, target, gain)


def _prenorm_bwd(x, dh, dz, gain):
    t, d = x.shape
    tm = min(t, 256)

    def body(x_ref, dh_ref, dz_ref, g_ref, gx_ref, gp_ref):
        i = pl.program_id(0)
        xv = x_ref[...]
        r = lax.rsqrt(jnp.mean(xv * xv, axis=-1, keepdims=True) + NORM_EPS)
        n = xv * r
        dhv = dh_ref[...]
        dn = dhv * g_ref[...]
        gx_ref[...] = dz_ref[...] + r * (dn - n * jnp.mean(dn * n, axis=-1, keepdims=True))

        @pl.when(i == 0)
        def _():
            gp_ref[...] = jnp.zeros_like(gp_ref)

        gp_ref[...] += jnp.sum(dhv * n, axis=0, keepdims=True)

    row_spec = pl.BlockSpec((tm, d), lambda i: (i, 0))
    vec_spec = pl.BlockSpec((1, d), lambda i: (0, 0))
    return pl.pallas_call(
        body, name="prenorm_bwd", grid=(t // tm,),
        in_specs=[row_spec, row_spec, row_spec, vec_spec],
        out_specs=[row_spec, vec_spec],
        out_shape=[jax.ShapeDtypeStruct((t, d), F32), jax.ShapeDtypeStruct((1, d), F32)],
        compiler_params=_params(dimension_semantics=("arbitrary",)),
    )(x, dh, dz, gain)


class _Rider:
    def __init__(self, operands, out_shapes, aliases, n_sems, stages):
        self.operands = tuple(operands)
        self.out_shapes = tuple(out_shapes)
        self.aliases = dict(aliases)
        self.n_sems = n_sems
        self.stages = stages

    def scratch(self):
        return [pltpu.SemaphoreType.DMA((self.n_sems,)), pltpu.SemaphoreType.DMA((self.n_sems,))]

    def emit(self, step, n_steps, in_refs, out_refs, send_sems, recv_sems):
        for frac, fn in self.stages(in_refs, out_refs, send_sems, recv_sems):
            at = min(n_steps - 1, int(frac * (n_steps - 1) + 0.5))
            pl.when(step == at)(fn)


def _matmul(name, a, b, *, out_shape, grid, a_spec, b_spec, o_spec, nt=False, perm=None, rider=None):
    nk = grid[2]
    n_steps = grid[0] * grid[1] * grid[2]
    tm, tn = [s for s in o_spec.block_shape if s is not None][-2:]
    acc_in_out = out_shape.dtype == F32
    n_pre = 0 if perm is None else 1
    n_rin = 0 if rider is None else len(rider.operands)
    n_rout = 0 if rider is None else len(rider.out_shapes)
    use_acc = not (nk == 1 or acc_in_out)

    def body(*refs):
        refs = refs[n_pre:]
        a_ref, b_ref = refs[:2]
        rin = refs[2:2 + n_rin]
        o_ref = refs[2 + n_rin]
        rout = refs[3 + n_rin:3 + n_rin + n_rout]
        scratch_refs = refs[3 + n_rin + n_rout:]
        if rider is not None:
            step = (pl.program_id(0) * grid[1] + pl.program_id(1)) * grid[2] + pl.program_id(2)
            rider.emit(step, n_steps, rin, rout, scratch_refs[-2], scratch_refs[-1])
        part = _dot_nt(a_ref[...], b_ref[...]) if nt else _dot(a_ref[...], b_ref[...])
        if nk == 1:
            o_ref[...] = part.astype(o_ref.dtype)
            return
        acc_ref = o_ref if acc_in_out else scratch_refs[0]
        k = pl.program_id(2)

        @pl.when(k == 0)
        def _():
            acc_ref[...] = part

        @pl.when(k > 0)
        def _():
            acc_ref[...] += part

        if not acc_in_out:
            @pl.when(k == nk - 1)
            def _():
                o_ref[...] = acc_ref[...].astype(o_ref.dtype)

    scratch = [pltpu.VMEM((tm, tn), F32)] if use_acc else []
    in_specs = [a_spec, b_spec] + [HBM_SPEC] * n_rin
    out_specs = [o_spec] + [HBM_SPEC] * n_rout
    out_shapes = [out_shape]
    operands = [a, b]
    aliases = {}
    sem = ("parallel", "parallel", "arbitrary")
    if rider is not None:
        scratch += rider.scratch()
        out_shapes += list(rider.out_shapes)
        operands += list(rider.operands)
        aliases = {n_pre + 2 + i: 1 + o for i, o in rider.aliases.items()}
        sem = ("arbitrary", "arbitrary", "arbitrary")
    cp = _params(dimension_semantics=sem)
    if perm is None:
        return pl.pallas_call(body, name=name, grid=grid, in_specs=in_specs, out_specs=out_specs,
                              out_shape=out_shapes, scratch_shapes=scratch, input_output_aliases=aliases,
                              compiler_params=cp)(*operands)
    gs = pltpu.PrefetchScalarGridSpec(num_scalar_prefetch=1, grid=grid, in_specs=in_specs,
                                      out_specs=out_specs, scratch_shapes=scratch)
    return pl.pallas_call(body, name=name, grid_spec=gs, out_shape=out_shapes, input_output_aliases=aliases,
                          compiler_params=cp)(jnp.asarray(perm), *operands)


def _proj_mm(h, w_full, perm, rider=None):
    t, d = h.shape
    n_tiles = len(perm)
    tm = min(t, 1024)
    return _matmul(
        "proj_mm", h, w_full, perm=perm, rider=rider, grid=(t // tm, n_tiles, 1),
        out_shape=jax.ShapeDtypeStruct((t, n_tiles * COL_TILE), F32),
        a_spec=pl.BlockSpec((tm, d), lambda i, j, k, p: (i, 0)),
        b_spec=pl.BlockSpec((d, COL_TILE), lambda i, j, k, p: (0, p[j])),
        o_spec=pl.BlockSpec((tm, COL_TILE), lambda i, j, k, p: (i, j)))


def _proj_gather_mm(h, wi_part, perm, sc):
    t, d = h.shape
    n_tiles = len(perm)
    tm = min(t, 1024)
    n_i = t // tm
    hd = d // 2
    nf = sc // COL_TILE
    rem = sc - nf * COL_TILE
    assert 2 * rem == COL_TILE and n_tiles == N_CHIPS * nf + 2
    n_chunks = next(q for q in (4, 3, 2, 1) if nf % q == 0)
    tpc = nf // n_chunks
    n_kinds = n_chunks + 1
    n_sems = 3 * n_kinds
    rem_at = nf + 3 * n_chunks * tpc

    def first_full(chip):
        return (chip * sc + (rem if chip % 2 else 0)) // COL_TILE

    inverse = np.argsort(perm)
    table = np.zeros((N_CHIPS, 2, n_tiles), np.int32)
    for chip in range(N_CHIPS):
        seq = list(range(first_full(chip), first_full(chip) + nf))
        for q in range(n_chunks):
            for src in (chip ^ 2, chip ^ 1, chip ^ 3):
                seq += list(range(first_full(src) + q * tpc, first_full(src) + (q + 1) * tpc))
        seq += [first_full(chip - chip % 2) + nf, first_full((chip ^ 2) - chip % 2) + nf]
        assert sorted(seq) == list(range(n_tiles)), seq
        table[chip, 0] = inverse[seq]
        table[chip, 1] = seq
    me_chip = 2 * lax.axis_index("x") + lax.axis_index("y")
    tab = lax.dynamic_index_in_dim(jnp.asarray(table), me_chip, 0, keepdims=False)

    def body(tab_ref, h_hbm, wi_in, proj_ref, full, hbuf, bbuf, local_sems, send_sems, recv_sems):
        del wi_in
        jj = pl.program_id(0)
        i = pl.program_id(1)
        x, y, c, chips = _mesh_pos()
        sibling = (x, y, 1 - c)

        def piece(chip, half, kind):
            odd = chip % 2
            if kind == n_chunks:
                start, width = chip * sc + (1 - odd) * (nf * COL_TILE), rem
            else:
                start, width = chip * sc + odd * rem + kind * (tpc * COL_TILE), tpc * COL_TILE
            return full.at[pl.ds(half * hd, hd), pl.ds(pl.multiple_of(start, LANES), width)]

        def ici(j, kind):
            mine = piece(2 * x + y, c, kind)
            k = j * n_kinds + kind
            return _remote(mine, mine, send_sems.at[k], recv_sems.at[k], (chips[j][0], chips[j][1], c))

        def landed(j, kind):
            blk = piece(2 * chips[j][0] + chips[j][1], c, kind)
            k = j * n_kinds + kind
            return _remote(blk, blk, send_sems.at[k], recv_sems.at[k], sibling)

        def passed(j, kind, half):
            blk = piece(2 * chips[j][0] + chips[j][1], half, kind)
            k = n_sems + j * n_kinds + kind
            return _remote(blk, blk, send_sems.at[k], recv_sems.at[k], sibling)

        def fetch(pos, slot):
            col = pl.multiple_of(tab_ref[1, pos] * COL_TILE, LANES)
            return pltpu.make_async_copy(full.at[:, pl.ds(col, COL_TILE)], bbuf.at[slot], local_sems.at[slot])

        def load_h():
            return pltpu.make_async_copy(h_hbm, hbuf, local_sems.at[2])

        def relay(j, kind):
            landed(j, kind).wait_recv()
            passed(j, kind, c).start()

        @pl.when(i == 0)
        def _():
            @pl.when(jj == 0)
            def _():
                load_h().start()
                for kind in range(n_kinds):
                    for j in range(3):
                        ici(j, kind).start()
                fetch(0, 0).start()
                load_h().wait()

            for n in range(3 * n_chunks):
                at = nf + n * tpc
                pl.when(jj == at - 2)(lambda n=n: relay(n % 3, n // 3))
                pl.when(jj == at - 1)(lambda n=n: passed(n % 3, n // 3, 1 - c).wait_recv())

            @pl.when(jj == rem_at - 2)
            def _():
                for j in range(3):
                    relay(j, n_chunks)

            @pl.when(jj == rem_at - 1)
            def _():
                for j in range(3):
                    passed(j, n_chunks, 1 - c).wait_recv()

            @pl.when(jj + 1 < n_tiles)
            def _():
                fetch(jj + 1, (jj + 1) % 2).start()

            fetch(jj, jj % 2).wait()

            @pl.when(jj == n_tiles - 1)
            def _():
                for kind in range(n_kinds):
                    for j in range(3):
                        ici(j, kind).wait_send()
                        passed(j, kind, c).wait_send()

        rows = pl.ds(pl.multiple_of(i * tm, tm), tm)
        proj_ref[...] = _dot(hbuf[rows, :], bbuf[jj % 2])

    gs = pltpu.PrefetchScalarGridSpec(
        num_scalar_prefetch=1, grid=(n_tiles, n_i),
        in_specs=[HBM_SPEC, HBM_SPEC],
        out_specs=[pl.BlockSpec((tm, COL_TILE), lambda jj, i, tb: (i, tb[0, jj])), HBM_SPEC],
        scratch_shapes=[pltpu.VMEM((t, d), BF16), pltpu.VMEM((2, d, COL_TILE), BF16), pltpu.SemaphoreType.DMA((3,)),
                        pltpu.SemaphoreType.DMA((2 * n_sems,)), pltpu.SemaphoreType.DMA((2 * n_sems,))])
    return pl.pallas_call(
        body, name="proj_gather_mm", grid_spec=gs,
        out_shape=[jax.ShapeDtypeStruct((t, n_tiles * COL_TILE), F32), jax.ShapeDtypeStruct(wi_part.shape, BF16)],
        input_output_aliases={2: 1},
        compiler_params=_params(dimension_semantics=("arbitrary", "arbitrary")),
    )(tab, h, wi_part)


def _gw_in_mm(ht, dproj, perm):
    d, t = ht.shape
    n_tiles = len(perm)
    hd = d // 2
    tm = min(hd, 1024)
    per_half = hd // tm
    return _matmul(
        "gw_in_mm", ht, dproj, perm=perm, grid=(d // tm, n_tiles, 1),
        out_shape=jax.ShapeDtypeStruct((2, hd, n_tiles * COL_TILE), BF16),
        a_spec=pl.BlockSpec((tm, t), lambda i, j, k, p: (i, 0)),
        b_spec=pl.BlockSpec((t, COL_TILE), lambda i, j, k, p: (0, j)),
        o_spec=pl.BlockSpec((None, tm, COL_TILE), lambda i, j, k, p: (i // per_half, i % per_half, p[j])))[0]


def _dh_mm(dproj, w_full, perm, rider=None):
    t = dproj.shape[0]
    d = w_full.shape[0]
    n_tiles = len(perm)
    tm = min(t, 2048)
    tn = min(d, 2048)
    return _matmul(
        "dh_mm", dproj, w_full, perm=perm, rider=rider, nt=True, grid=(t // tm, d // tn, n_tiles),
        out_shape=jax.ShapeDtypeStruct((t, d), F32),
        a_spec=pl.BlockSpec((tm, COL_TILE), lambda i, j, k, p: (i, k)),
        b_spec=pl.BlockSpec((tn, COL_TILE), lambda i, j, k, p: (j, p[k])),
        o_spec=pl.BlockSpec((tm, tn), lambda i, j, k, p: (i, j)))


def _out_mm(mixed, w_out_full):
    t, dm = mixed.shape
    d = w_out_full.shape[1]
    tm = min(t, 1024)
    tn = min(d, 512)
    tk = min(dm, 4096)
    return _matmul(
        "out_mm", mixed, w_out_full, grid=(t // tm, d // tn, dm // tk),
        out_shape=jax.ShapeDtypeStruct((t, d), F32),
        a_spec=pl.BlockSpec((tm, tk), lambda i, j, k: (i, k)),
        b_spec=pl.BlockSpec((tk, tn), lambda i, j, k: (k, j)),
        o_spec=pl.BlockSpec((tm, tn), lambda i, j, k: (i, j)))[0]


def _dmixed_mm(dy, w_out_full):
    t, d = dy.shape
    dm = w_out_full.shape[0]
    tm = min(t, 1024)
    tn = min(dm, 1024)
    return _matmul(
        "dmixed_mm", dy, w_out_full, nt=True, grid=(t // tm, dm // tn, 1),
        out_shape=jax.ShapeDtypeStruct((t, dm), F32),
        a_spec=pl.BlockSpec((tm, d), lambda i, j, k: (i, 0)),
        b_spec=pl.BlockSpec((tn, d), lambda i, j, k: (j, 0)),
        o_spec=pl.BlockSpec((tm, tn), lambda i, j, k: (i, j)))[0]


def _gw_out_mm(mixed_t, dy):
    dm, t = mixed_t.shape
    d = dy.shape[1]
    hr = dm // (2 * N_CHIPS)
    tn = min(d, 1024)
    return _matmul(
        "gw_out_mm", mixed_t, dy, grid=(dm // hr, d // tn, 1),
        out_shape=jax.ShapeDtypeStruct((2, N_CHIPS, hr, d), BF16),
        a_spec=pl.BlockSpec((hr, t), lambda i, j, k: (i, 0)),
        b_spec=pl.BlockSpec((t, tn), lambda i, j, k: (0, j)),
        o_spec=pl.BlockSpec((None, None, hr, tn), lambda i, j, k: (i % 2, i // 2, 0, j)))[0]


def _lane_half():
    return lax.broadcasted_iota(jnp.int32, (WINDOW, LANES), 1) // ATTN_HEAD_DIM


def _dup_kv(tile, kh):
    return jnp.where(_lane_half() == kh, tile, pltpu.roll(tile, ATTN_HEAD_DIM, 1))


def _stack_heads(tiles, kh):
    half = _lane_half()
    pieces = []
    for g in range(GQA_GROUP):
        pieces.append(jnp.where(half == g % 2, tiles[4 * kh + g // 2], 0.0))
    return jnp.concatenate(pieces, axis=0)


def _unstack_heads(stacked):
    half = _lane_half()
    out = []
    for j in range(GQA_GROUP // 2):
        a = stacked[(2 * j) * WINDOW:(2 * j + 1) * WINDOW]
        b = stacked[(2 * j + 1) * WINDOW:(2 * j + 2) * WINDOW]
        out.append(jnp.where(half == 0, a, b))
    return out


def _attn_probs(qs, kcat, sink_col, n):
    rows = GQA_GROUP * WINDOW
    s = _dot_nt(qs, kcat)
    qi = lax.broadcasted_iota(jnp.int32, (rows, 2 * WINDOW), 0) % WINDOW
    kj = lax.broadcasted_iota(jnp.int32, (rows, 2 * WINDOW), 1)
    first_key = WINDOW * (1 - jnp.minimum(n, 1))
    valid = (kj > qi) & (kj <= qi + WINDOW) & (kj >= first_key)
    s = jnp.where(valid, s, NEG_BIG)
    mx = jnp.maximum(jnp.max(s, axis=-1, keepdims=True), sink_col)
    p = jnp.where(valid, jnp.exp(s - mx), 0.0)
    p_sink = jnp.exp(sink_col - mx)
    inv = 1.0 / (jnp.sum(p, axis=-1, keepdims=True) + p_sink)
    return p * inv, p_sink * inv


def _attn_operands(sink_ref, q_tiles, kp_ref, kc_ref, vp_ref, vc_ref, m, kh):
    qs = _stack_heads([qt * ATTN_SCALE for qt in q_tiles], kh).astype(BF16)
    kcat = jnp.concatenate([_dup_kv(kp_ref[...], kh), _dup_kv(kc_ref[...], kh)], axis=0).astype(BF16)
    vcat = jnp.concatenate([_dup_kv(vp_ref[...], kh), _dup_kv(vc_ref[...], kh)], axis=0).astype(BF16)
    heads_per_group = ATTN_GROUP_LANES // ATTN_HEAD_DIM
    sink_col = jnp.concatenate(
        [jnp.full((WINDOW, 1), sink_ref[0, m * heads_per_group + kh * GQA_GROUP + g], F32)
         for g in range(GQA_GROUP)], axis=0)
    return qs, kcat, vcat, sink_col


def _attn_specs(lay, d):
    a_blk = lay["a_off"] // (2 * ATTN_GROUP_LANES)
    k_blk = lay["k_off"] // LANES
    v_blk = lay["v_off"] // LANES
    qg = pl.BlockSpec((WINDOW, 2 * ATTN_GROUP_LANES), lambda m, n: (n, a_blk + m))
    kp = pl.BlockSpec((WINDOW, LANES), lambda m, n: (jnp.maximum(n - 1, 0), k_blk + m))
    kc = pl.BlockSpec((WINDOW, LANES), lambda m, n: (n, k_blk + m))
    vp = pl.BlockSpec((WINDOW, LANES), lambda m, n: (jnp.maximum(n - 1, 0), v_blk + m))
    vc = pl.BlockSpec((WINDOW, LANES), lambda m, n: (n, v_blk + m))
    return qg, kp, kc, vp, vc


def _attn_fwd(proj, sinks, lay, d, rider=None):
    t = proj.shape[0]
    n_groups = d // ATTN_GROUP_LANES
    n_blocks = t // WINDOW
    pairs = ATTN_GROUP_LANES // LANES
    n_rin = 0 if rider is None else len(rider.operands)
    n_rout = 0 if rider is None else len(rider.out_shapes)

    def body(*refs):
        sink_ref, qg_ref, kp_ref, kc_ref, vp_ref, vc_ref = refs[:6]
        rin = refs[6:6 + n_rin]
        mix_ref, o_ref = refs[6 + n_rin:8 + n_rin]
        rout = refs[8 + n_rin:8 + n_rin + n_rout]
        m = pl.program_id(0)
        n = pl.program_id(1)
        if rider is not None:
            rider.emit(m * n_blocks + n, n_groups * n_blocks, rin, rout, refs[-2], refs[-1])
        q_tiles = [qg_ref[:, p * LANES:(p + 1) * LANES] for p in range(pairs)]
        for kh in range(2):
            qs, kcat, vcat, sink_col = _attn_operands(sink_ref, q_tiles, kp_ref, kc_ref, vp_ref, vc_ref, m, kh)
            probs, _ = _attn_probs(qs, kcat, sink_col, n)
            out = _dot(probs.astype(BF16), vcat)
            for j, tile in enumerate(_unstack_heads(out)):
                p = 4 * kh + j
                lanes = slice(p * LANES, (p + 1) * LANES)
                gate = qg_ref[:, ATTN_GROUP_LANES + p * LANES:ATTN_GROUP_LANES + (p + 1) * LANES]
                o_ref[:, lanes] = tile
                mix_ref[:, lanes] = (tile * (gate * _sigmoid(gate))).astype(BF16)

    qg, kp, kc, vp, vc = _attn_specs(lay, d)
    out_blk = pl.BlockSpec((WINDOW, ATTN_GROUP_LANES), lambda m, n: (n, m))
    out_shapes = [jax.ShapeDtypeStruct((t, 2 * d), BF16), jax.ShapeDtypeStruct((t, d), F32)]
    operands = [sinks, proj, proj, proj, proj, proj]
    scratch, aliases, sem = [], {}, ("parallel", "parallel")
    if rider is not None:
        scratch = rider.scratch()
        out_shapes += list(rider.out_shapes)
        operands += list(rider.operands)
        aliases = {6 + i: 2 + o for i, o in rider.aliases.items()}
        sem = ("arbitrary", "arbitrary")
    return pl.pallas_call(
        body, name="attn_fwd", grid=(n_groups, n_blocks),
        in_specs=[pl.BlockSpec(memory_space=pltpu.SMEM), qg, kp, kc, vp, vc] + [HBM_SPEC] * n_rin,
        out_specs=[out_blk, out_blk] + [HBM_SPEC] * n_rout,
        out_shape=out_shapes, scratch_shapes=scratch, input_output_aliases=aliases,
        compiler_params=_params(dimension_semantics=sem),
    )(*operands)


def _attn_bwd(proj, sinks, attn_o, dmixed, dproj, lay, d):
    t = proj.shape[0]
    n_groups = d // ATTN_GROUP_LANES
    pairs = ATTN_GROUP_LANES // LANES
    kv_w = n_groups * LANES

    def body(sink_ref, qg_ref, kp_ref, kc_ref, vp_ref, vc_ref, o_ref, dm_ref, dproj_hbm,
             dqg_ref, dkc_ref, dkp_ref, dvc_ref, dvp_ref, dsink_ref):
        del dproj_hbm
        m = pl.program_id(0)
        n = pl.program_id(1)
        half = _lane_half()
        q_tiles = [qg_ref[:, p * LANES:(p + 1) * LANES] for p in range(pairs)]
        do_tiles, o_tiles = [], []
        for p in range(pairs):
            lanes = slice(p * LANES, (p + 1) * LANES)
            gate = qg_ref[:, ATTN_GROUP_LANES + p * LANES:ATTN_GROUP_LANES + (p + 1) * LANES]
            sg = _sigmoid(gate)
            dmix = dm_ref[:, lanes]
            ov = o_ref[:, lanes]
            dqg_ref[:, ATTN_GROUP_LANES + p * LANES:ATTN_GROUP_LANES + (p + 1) * LANES] = (
                dmix * ov * (sg * (1.0 + gate * (1.0 - sg)))).astype(BF16)
            do_tiles.append(dmix * (gate * sg))
            o_tiles.append(ov)

        sub = lax.broadcasted_iota(jnp.int32, (8, LANES), 0)
        lane = lax.broadcasted_iota(jnp.int32, (8, LANES), 1)
        dsink = jnp.zeros((8, LANES), F32)
        dk_cur = dk_prev = dv_cur = dv_prev = jnp.zeros((WINDOW, LANES), F32)
        for kh in range(2):
            qs, kcat, vcat, sink_col = _attn_operands(sink_ref, q_tiles, kp_ref, kc_ref, vp_ref, vc_ref, m, kh)
            probs, p_sink = _attn_probs(qs, kcat, sink_col, n)
            dos = _stack_heads(do_tiles, kh)
            delta = jnp.sum(dos * _stack_heads(o_tiles, kh), axis=-1, keepdims=True)
            dos = dos.astype(BF16)
            dp = _dot_nt(dos, vcat)
            ds = (probs * (dp - delta)).astype(BF16)
            dv = _dot_tn(probs.astype(BF16), dos)
            dv = dv + pltpu.roll(dv, ATTN_HEAD_DIM, 1)
            dk = _dot_tn(ds, qs)
            dk = dk + pltpu.roll(dk, ATTN_HEAD_DIM, 1)
            dq = _dot(ds, kcat)
            for j, tile in enumerate(_unstack_heads(dq)):
                p = 4 * kh + j
                dqg_ref[:, p * LANES:(p + 1) * LANES] = (tile * ATTN_SCALE).astype(BF16)
            dk_prev = jnp.where(half == kh, dk[:WINDOW], dk_prev)
            dk_cur = jnp.where(half == kh, dk[WINDOW:], dk_cur)
            dv_prev = jnp.where(half == kh, dv[:WINDOW], dv_prev)
            dv_cur = jnp.where(half == kh, dv[WINDOW:], dv_cur)
            sink_terms = p_sink * delta
            for g in range(GQA_GROUP):
                val = -jnp.sum(sink_terms[g * WINDOW:(g + 1) * WINDOW])
                dsink = dsink + jnp.where((sub == 0) & (lane == kh * GQA_GROUP + g), val, 0.0)
        dkc_ref[...] = dk_cur
        dkp_ref[...] = dk_prev
        dvc_ref[...] = dv_cur
        dvp_ref[...] = dv_prev

        @pl.when(n == 0)
        def _():
            dsink_ref[...] = jnp.zeros_like(dsink_ref)

        dsink_ref[...] += dsink

    qg, kp, kc, vp, vc = _attn_specs(lay, d)
    a_blk = lay["a_off"] // (2 * ATTN_GROUP_LANES)
    grp = pl.BlockSpec((WINDOW, ATTN_GROUP_LANES), lambda m, n: (n, m))
    kv_blk = pl.BlockSpec((WINDOW, LANES), lambda m, n: (n, m))
    kv_shape = jax.ShapeDtypeStruct((t, kv_w), F32)
    outs = pl.pallas_call(
        body, name="attn_bwd", grid=(n_groups, t // WINDOW),
        in_specs=[pl.BlockSpec(memory_space=pltpu.SMEM), qg, kp, kc, vp, vc, grp, grp,
                  pl.BlockSpec(memory_space=pl.ANY)],
        out_specs=[pl.BlockSpec((WINDOW, 2 * ATTN_GROUP_LANES), lambda m, n: (n, a_blk + m)),
                   kv_blk, kv_blk, kv_blk, kv_blk, pl.BlockSpec((8, LANES), lambda m, n: (m, 0))],
        out_shape=[jax.ShapeDtypeStruct(dproj.shape, BF16), kv_shape, kv_shape, kv_shape, kv_shape,
                   jax.ShapeDtypeStruct((n_groups * 8, LANES), F32)],
        input_output_aliases={8: 0},
        compiler_params=_params(dimension_semantics=("parallel", "arbitrary")),
    )(sinks, proj, proj, proj, proj, proj, attn_o, dmixed, dproj)
    return outs


def _kv_combine(dkc, dkp, dvc, dvp, dproj, lay):
    t, kv_w = dkc.shape
    nb = t // WINDOW
    kv_blk_idx = lay["k_off"] // (2 * kv_w)

    def body(dkc_ref, dkp_ref, dvc_ref, dvp_ref, dproj_hbm, o_ref):
        del dproj_hbm
        keep = (pl.program_id(0) < nb - 1).astype(F32)
        o_ref[:, :kv_w] = (dkc_ref[...] + keep * dkp_ref[...]).astype(BF16)
        o_ref[:, kv_w:] = (dvc_ref[...] + keep * dvp_ref[...]).astype(BF16)

    cur = pl.BlockSpec((WINDOW, kv_w), lambda n: (n, 0))
    nxt = pl.BlockSpec((WINDOW, kv_w), lambda n: (jnp.minimum(n + 1, nb - 1), 0))
    return pl.pallas_call(
        body, name="kv_combine", grid=(nb,),
        in_specs=[cur, nxt, cur, nxt, pl.BlockSpec(memory_space=pl.ANY)],
        out_specs=pl.BlockSpec((WINDOW, 2 * kv_w), lambda n: (n, kv_blk_idx)),
        out_shape=jax.ShapeDtypeStruct(dproj.shape, BF16),
        input_output_aliases={4: 0},
        compiler_params=_params(dimension_semantics=("parallel",)),
    )(dkc, dkp, dvc, dvp, dproj)


def _lower_bound(lbl_ref):
    l0 = lbl_ref[0:1, :]
    l1 = lbl_ref[1:2, :]
    mx = jnp.maximum(l0, l1)
    e0 = jnp.exp(l0 - mx)
    e1 = jnp.exp(l1 - mx)
    return e0 / (e0 + e1)


def _chunk_masks():
    ti = lax.broadcasted_iota(jnp.int32, (CHUNK, CHUNK), 0)
    si = lax.broadcasted_iota(jnp.int32, (CHUNK, CHUNK), 1)
    diag = ((ti // HALF_CHUNK) == (si // HALF_CHUNK)) & (si <= ti)
    off = (ti >= HALF_CHUNK) & (si < HALF_CHUNK)
    lower = (si <= ti).astype(BF16)
    upper = (si >= ti).astype(BF16)
    return diag, off, lower, upper


def _rnn_gates(rq, rf, lb):
    sf = _sigmoid(rf)
    f = lb + (1.0 - lb) * sf
    sq = _sigmoid(rq)
    return sf, f, jnp.log(f), 1.0 - f, sq, rq * sq


def _rnn_decays(g_cum):
    row = lax.broadcasted_iota(jnp.int32, g_cum.shape, 0)
    ref_d = jnp.where(row < HALF_CHUNK, g_cum[HALF_CHUNK // 2 - 1:HALF_CHUNK // 2],
                      g_cum[HALF_CHUNK + HALF_CHUNK // 2 - 1:HALF_CHUNK + HALF_CHUNK // 2])
    ref_o = g_cum[HALF_CHUNK - 1:HALF_CHUNK]
    last = g_cum[CHUNK - 1:CHUNK]
    return dict(eq_d=jnp.exp(g_cum - ref_d), ek_d=jnp.exp(ref_d - g_cum),
                eq_o=jnp.exp(jnp.minimum(g_cum - ref_o, 0.0)), ek_o=jnp.exp(jnp.minimum(ref_o - g_cum, 0.0)),
                eg=jnp.exp(g_cum), ekl=jnp.exp(last - g_cum), e_last=jnp.exp(last))


def _head(a, j):
    return a[:, j * RNN_HEAD_DIM:(j + 1) * RNN_HEAD_DIM]


def _rnn_specs(t, tb, d):
    gw = RNN_GROUP_HEADS * RNN_HEAD_DIM
    return gw, t // tb, tb // CHUNK


def _rnn_fwd(proj, lb_logits, rnn_gain, mixed, d):
    t = proj.shape[0]
    tb = min(t, 256)
    gw, ntb, nch = _rnn_specs(t, tb, d)
    n_groups = d // gw
    n_heads = d // RNN_HEAD_DIM

    def body(blk_ref, lbl_ref, gain_ref, mixed_hbm, mix_ref, o_ref, st_out_ref, st_ref):
        del mixed_hbm

        @pl.when(pl.program_id(1) == 0)
        def _():
            st_ref[...] = jnp.zeros_like(st_ref)

        lb = _lower_bound(lbl_ref)
        gain = gain_ref[...]
        diag, off, lower, _ = _chunk_masks()

        def chunk(c, carry):
            rows = pl.ds(pl.multiple_of(c * CHUNK, CHUNK), CHUNK)
            rq = blk_ref[rows, 0:gw]
            rf = blk_ref[rows, gw:2 * gw]
            v = blk_ref[rows, 2 * gw:3 * gw]
            rg = blk_ref[rows, 3 * gw:4 * gw]
            _, _, g, k, _, q = _rnn_gates(rq, rf, lb)
            dec = _rnn_decays(_tri_dot(lower, g))
            qd = (q * dec["eq_d"]).astype(BF16)
            kd = (k * dec["ek_d"]).astype(BF16)
            qo = (q * dec["eq_o"]).astype(BF16)
            ko = (k * dec["ek_o"]).astype(BF16)
            qe = (q * dec["eg"]).astype(BF16)
            kl = (k * dec["ekl"]).astype(BF16)
            vb = v.astype(BF16)
            outs = []
            for j in range(RNN_GROUP_HEADS):
                st = st_ref[j]
                st_out_ref[j, c] = st
                attn = jnp.where(diag, _dot_nt(_head(qd, j), _head(kd, j)),
                                 jnp.where(off, _dot_nt(_head(qo, j), _head(ko, j)), 0.0))
                o = _dot(attn.astype(BF16), _head(vb, j)) + _dot_nt(_head(qe, j), st.astype(BF16))
                st_ref[j] = st * _head(dec["e_last"], j) + _dot_tn(_head(vb, j), _head(kl, j))
                rr = lax.rsqrt(jnp.mean(o * o, axis=-1, keepdims=True) + NORM_EPS)
                o_ref[rows, j * RNN_HEAD_DIM:(j + 1) * RNN_HEAD_DIM] = o
                outs.append(o * rr)
            on = jnp.concatenate(outs, axis=1) * gain
            mix_ref[rows, :] = (on * (rg * _sigmoid(rg))).astype(BF16)
            return carry

        lax.fori_loop(0, nch, chunk, 0)

    return pl.pallas_call(
        body, name="rnn_fwd", grid=(n_groups, ntb),
        in_specs=[pl.BlockSpec((tb, 4 * gw), lambda h, i: (i, h)),
                  pl.BlockSpec((2, gw), lambda h, i: (0, h)),
                  pl.BlockSpec((1, gw), lambda h, i: (0, h)),
                  pl.BlockSpec(memory_space=pl.ANY)],
        out_specs=[pl.BlockSpec((tb, gw), lambda h, i: (i, d // gw + h)),
                   pl.BlockSpec((tb, gw), lambda h, i: (i, h)),
                   pl.BlockSpec((RNN_GROUP_HEADS, nch, RNN_HEAD_DIM, RNN_HEAD_DIM), lambda h, i: (h, i, 0, 0))],
        out_shape=[jax.ShapeDtypeStruct(mixed.shape, BF16), jax.ShapeDtypeStruct((t, d), F32),
                   jax.ShapeDtypeStruct((n_heads, t // CHUNK, RNN_HEAD_DIM, RNN_HEAD_DIM), F32)],
        scratch_shapes=[pltpu.VMEM((RNN_GROUP_HEADS, RNN_HEAD_DIM, RNN_HEAD_DIM), F32)],
        input_output_aliases={3: 0},
        compiler_params=_params(dimension_semantics=("parallel", "arbitrary")),
    )(proj, lb_logits, rnn_gain, mixed)


def _rnn_bwd(proj, lb_logits, rnn_gain, o_pre, states, dmixed, d_total, d, rider=None):
    t = proj.shape[0]
    tb = min(t, 256)
    gw, ntb, nch = _rnn_specs(t, tb, d)
    n_groups = d // gw
    n_rin = 0 if rider is None else len(rider.operands)
    n_rout = 0 if rider is None else len(rider.out_shapes)

    def body(*refs):
        blk_ref, lbl_ref, gain_ref, o_ref, st_in_ref, dm_ref = refs[:6]
        rin = refs[6:6 + n_rin]
        dproj_ref, dgain_ref, dlb_ref = refs[6 + n_rin:9 + n_rin]
        rout = refs[9 + n_rin:9 + n_rin + n_rout]
        dst_ref = refs[9 + n_rin + n_rout]
        if rider is not None:
            rider.emit(pl.program_id(0) * ntb + pl.program_id(1), n_groups * ntb, rin, rout, refs[-2], refs[-1])

        @pl.when(pl.program_id(1) == 0)
        def _():
            dst_ref[...] = jnp.zeros_like(dst_ref)
            dgain_ref[...] = jnp.zeros_like(dgain_ref)
            dlb_ref[...] = jnp.zeros_like(dlb_ref)

        lb = _lower_bound(lbl_ref)
        gain = gain_ref[...]
        diag, off, lower, upper = _chunk_masks()
        last_row = lax.broadcasted_iota(jnp.int32, (CHUNK, RNN_HEAD_DIM), 0) == CHUNK - 1

        def chunk(step, carry):
            c = nch - 1 - step
            rows = pl.ds(pl.multiple_of(c * CHUNK, CHUNK), CHUNK)
            rq = blk_ref[rows, 0:gw]
            rf = blk_ref[rows, gw:2 * gw]
            v = blk_ref[rows, 2 * gw:3 * gw]
            rg = blk_ref[rows, 3 * gw:4 * gw]
            sf, f, g, k, sq, q = _rnn_gates(rq, rf, lb)
            dec = _rnn_decays(_tri_dot(lower, g))
            qd = (q * dec["eq_d"]).astype(BF16)
            kd = (k * dec["ek_d"]).astype(BF16)
            qo = (q * dec["eq_o"]).astype(BF16)
            ko = (k * dec["ek_o"]).astype(BF16)
            qe = (q * dec["eg"]).astype(BF16)
            kl = (k * dec["ekl"]).astype(BF16)
            vb = v.astype(BF16)

            o = o_ref[rows, :]
            dmix = dm_ref[rows, :]
            sg = _sigmoid(rg)
            n_parts = []
            for j in range(RNN_GROUP_HEADS):
                oj = _head(o, j)
                n_parts.append(oj * lax.rsqrt(jnp.mean(oj * oj, axis=-1, keepdims=True) + NORM_EPS))
            nrm = jnp.concatenate(n_parts, axis=1)
            d_on = dmix * (rg * sg)
            d_rg = dmix * (nrm * gain) * (sg * (1.0 + rg * (1.0 - sg)))
            dgain_ref[...] += jnp.sum(d_on * nrm, axis=0, keepdims=True)
            dn = d_on * gain

            dq_parts, dk_parts, dv_parts, dg_parts = [], [], [], []
            for j in range(RNN_GROUP_HEADS):
                oj, nj, dnj = _head(o, j), _head(nrm, j), _head(dn, j)
                rr = lax.rsqrt(jnp.mean(oj * oj, axis=-1, keepdims=True) + NORM_EPS)
                do = (rr * (dnj - nj * jnp.mean(dnj * nj, axis=-1, keepdims=True))).astype(BF16)
                st = st_in_ref[j, c]
                dst = dst_ref[j]
                stb, dstb = st.astype(BF16), dst.astype(BF16)
                qdj, kdj, qoj, koj = _head(qd, j), _head(kd, j), _head(qo, j), _head(ko, j)
                attn = jnp.where(diag, _dot_nt(qdj, kdj), jnp.where(off, _dot_nt(qoj, koj), 0.0))
                dattn = _dot_nt(do, _head(vb, j))
                da_d = jnp.where(diag, dattn, 0.0).astype(BF16)
                da_o = jnp.where(off, dattn, 0.0).astype(BF16)
                dv = _dot_tn(attn.astype(BF16), do) + _dot_nt(_head(kl, j), dstb)
                dq_inter = _dot(do, stb) * _head(dec["eg"], j)
                dq_d, dq_o = _dot(da_d, kdj), _dot(da_o, koj)
                dq = dq_inter + dq_d * _head(dec["eq_d"], j) + dq_o * _head(dec["eq_o"], j)
                dk_inter = _dot(_head(vb, j), dstb) * _head(dec["ekl"], j)
                dk_d, dk_o = _dot_tn(da_d, qdj), _dot_tn(da_o, qoj)
                dk = dk_inter + dk_d * _head(dec["ek_d"], j) + dk_o * _head(dec["ek_o"], j)
                kj, qj = _head(k, j), _head(q, j)
                e_last = _head(dec["e_last"], j)
                extra = (jnp.sum(kj * dk_inter, axis=0, keepdims=True)
                         + e_last * jnp.sum(st * dst, axis=0, keepdims=True))
                dg_cum = (qj * dq_inter - kj * dk_inter
                          + (qdj.astype(F32) * dq_d + qoj.astype(F32) * dq_o)
                          - (kdj.astype(F32) * dk_d + koj.astype(F32) * dk_o))
                dg_parts.append(jnp.where(last_row, dg_cum + extra, dg_cum))
                dst_ref[j] = dst * e_last + _dot_tn(do, _head(qe, j))
                dq_parts.append(dq)
                dk_parts.append(dk)
                dv_parts.append(dv)

            dq = jnp.concatenate(dq_parts, axis=1)
            dk = jnp.concatenate(dk_parts, axis=1)
            dg = _tri_dot(upper, jnp.concatenate(dg_parts, axis=1))
            df = dg / f - dk
            dlb_ref[...] += jnp.sum(df * (1.0 - sf), axis=0, keepdims=True)
            d_rf = df * (1.0 - lb) * (sf * (1.0 - sf))
            d_rq = dq * (sq * (1.0 + rq * (1.0 - sq)))
            dproj_ref[rows, 0:gw] = d_rq.astype(BF16)
            dproj_ref[rows, gw:2 * gw] = d_rf.astype(BF16)
            dproj_ref[rows, 2 * gw:3 * gw] = jnp.concatenate(dv_parts, axis=1).astype(BF16)
            dproj_ref[rows, 3 * gw:4 * gw] = d_rg.astype(BF16)
            return carry

        lax.fori_loop(0, nch, chunk, 0)

    rev = lambda i: ntb - 1 - i
    vec = pl.BlockSpec((1, gw), lambda h, i: (0, h))
    scratch = [pltpu.VMEM((RNN_GROUP_HEADS, RNN_HEAD_DIM, RNN_HEAD_DIM), F32)]
    out_shapes = [jax.ShapeDtypeStruct((t, d_total), BF16), jax.ShapeDtypeStruct((1, d), F32),
                  jax.ShapeDtypeStruct((1, d), F32)]
    operands = [proj, lb_logits, rnn_gain, o_pre, states, dmixed]
    sem = ("parallel", "arbitrary")
    if rider is not None:
        scratch += rider.scratch()
        out_shapes += list(rider.out_shapes)
        operands += list(rider.operands)
        sem = ("arbitrary", "arbitrary")
    return pl.pallas_call(
        body, name="rnn_bwd", grid=(n_groups, ntb),
        in_specs=[pl.BlockSpec((tb, 4 * gw), lambda h, i: (rev(i), h)),
                  pl.BlockSpec((2, gw), lambda h, i: (0, h)), vec,
                  pl.BlockSpec((tb, gw), lambda h, i: (rev(i), h)),
                  pl.BlockSpec((RNN_GROUP_HEADS, nch, RNN_HEAD_DIM, RNN_HEAD_DIM), lambda h, i: (h, rev(i), 0, 0)),
                  pl.BlockSpec((tb, gw), lambda h, i: (rev(i), d // gw + h))] + [HBM_SPEC] * n_rin,
        out_specs=[pl.BlockSpec((tb, 4 * gw), lambda h, i: (rev(i), h)), vec, vec] + [HBM_SPEC] * n_rout,
        out_shape=out_shapes, scratch_shapes=scratch,
        compiler_params=_params(dimension_semantics=sem),
    )(*operands)


def _local_grads(x, target, w_in_full, w_out, sinks, lb_logits, rnn_gain, pre_gain, post_gain, sc=None):
    t, d = x.shape
    comm = sc is not None
    lay = _layout(d)
    perm = lay["perm"]
    h, ht = _prenorm_fwd(x, pre_gain)
    if comm:
        proj, w_in_full = _proj_gather_mm(h, w_in_full, perm, sc)
        mixed, attn_o, w_out_full = _attn_fwd(proj, sinks, lay, d, _gather_rider(w_out, 0, 0.8))
    else:
        (proj,) = _proj_mm(h, w_in_full, perm)
        w_out_full = w_out
        mixed, attn_o = _attn_fwd(proj, sinks, lay, d)
    mixed, o_pre, states = _rnn_fwd(proj, lb_logits, rnn_gain, mixed, d)
    y = _out_mm(mixed, w_out_full)
    dy, dz, g_post, loss = _post_loss(x, y, target, post_gain)
    dmixed = _dmixed_mm(dy, w_out_full)
    gw_out = _gw_out_mm(_transpose_bf16(mixed, "mixed_t"), dy)
    rider = None
    if comm:
        p_out = _pair_sum_out(gw_out, _pair_exchange(gw_out, "pair_exchange_out"))
        rider = _chip_exchange_rider(p_out, lambda ref, chip: ref.at[chip])
    dproj, g_rnn, g_lb, *r_out = _rnn_bwd(proj, lb_logits, rnn_gain, o_pre, states, dmixed, lay["total"], d, rider)
    dproj, dkc, dkp, dvc, dvp, dsink = _attn_bwd(proj, sinks, attn_o, dmixed, dproj, lay, d)
    dproj = _kv_combine(dkc, dkp, dvc, dvp, dproj, lay)
    gw_in = _gw_in_mm(ht, dproj, perm)
    if comm:
        p_in = _pair_sum_in(gw_in, _pair_exchange(gw_in, "pair_exchange_in"), sc)
        rider = _chip_exchange_rider(
            p_in, lambda ref, chip: ref.at[:, pl.ds(pl.multiple_of(chip * sc, LANES), sc)])
    dh, *r_in = _dh_mm(dproj, w_in_full, perm, rider)
    grad_x, g_pre = _prenorm_bwd(x, dh, dz, pre_gain)
    heads_per_group = ATTN_GROUP_LANES // ATTN_HEAD_DIM
    g_sink = dsink.reshape(d // ATTN_GROUP_LANES, 8, LANES)[:, 0, :heads_per_group].reshape(1, -1)
    small = dict(sink=g_sink, lb=g_lb, rnn=g_rnn, pre=g_pre, post=g_post)
    if comm:
        return loss, grad_x, (p_in, r_in[0]), (p_out, r_out[0]), small
    return loss, grad_x, gw_in, gw_out, small


def _mesh_pos():
    x, y, c = lax.axis_index("x"), lax.axis_index("y"), lax.axis_index("c")
    chips = [(1 - x, y), (x, 1 - y), (1 - x, 1 - y)]
    return x, y, c, chips


def _remote(src, dst, send_sem, recv_sem, device):
    return pltpu.make_async_remote_copy(src_ref=src, dst_ref=dst, send_sem=send_sem, recv_sem=recv_sem,
                                        device_id=device, device_id_type=MESH)


HBM_SPEC = pl.BlockSpec(memory_space=pl.ANY)


def _gather_rider(part, axis, forward_at):
    rows = part.shape[0] if axis == 1 else part.shape[0] // N_CHIPS
    cols = part.shape[1] // N_CHIPS if axis == 1 else part.shape[1]
    half_rows = rows // 2

    def stages(ins, outs, send_sems, recv_sems):
        del ins
        full = outs[0]

        def piece(chip, half):
            if axis == 1:
                return full.at[pl.ds(half * half_rows, half_rows), pl.ds(pl.multiple_of(chip * cols, LANES), cols)]
            return full.at[pl.ds(pl.multiple_of(chip * rows + half * half_rows, 8), half_rows), :]

        def sends():
            x, y, c, chips = _mesh_pos()
            mine = piece(2 * x + y, c)
            return [_remote(mine, mine, send_sems.at[j], recv_sems.at[j], (px, py, c))
                    for j, (px, py) in enumerate(chips)]

        def forwards(half_of):
            x, y, c, chips = _mesh_pos()
            out = []
            for j, (px, py) in enumerate(chips):
                block = piece(2 * px + py, half_of(c))
                out.append(_remote(block, block, send_sems.at[3 + j], recv_sems.at[3 + j], (x, y, 1 - c)))
            return out

        def start():
            for cp in sends():
                cp.start()

        def forward():
            x, y, c, chips = _mesh_pos()
            for j, (px, py) in enumerate(chips):
                landed = piece(2 * px + py, c)
                _remote(landed, landed, send_sems.at[j], recv_sems.at[j], (x, y, 1 - c)).wait_recv()
            for cp in forwards(lambda c: c):
                cp.start()

        def finish():
            for cp in forwards(lambda c: 1 - c):
                cp.wait_recv()
            for cp in sends() + forwards(lambda c: c):
                cp.wait_send()

        return [(0.0, start), (forward_at, forward), (1.0, finish)]

    return _Rider((part,), (jax.ShapeDtypeStruct(part.shape, BF16),), {0: 0}, 6, stages)


def _chip_exchange_rider(partial, piece):
    if partial.ndim == 3:
        recv_shape = (N_CHIPS - 1,) + partial.shape[1:]
    else:
        recv_shape = (N_CHIPS - 1, partial.shape[0], partial.shape[1] // N_CHIPS)

    def stages(ins, outs, send_sems, recv_sems):
        def copies():
            x, y, c, chips = _mesh_pos()
            return [_remote(piece(ins[0], 2 * px + py), outs[0].at[j], send_sems.at[j], recv_sems.at[j], (px, py, c))
                    for j, (px, py) in enumerate(chips)]

        def start():
            for cp in copies():
                cp.start()

        def finish():
            for cp in copies():
                cp.wait()

        return [(0.0, start), (1.0, finish)]

    return _Rider((partial,), (jax.ShapeDtypeStruct(recv_shape, BF16),), {}, N_CHIPS - 1, stages)


def _pair_exchange(g, name):
    def body(g_ref, r_ref, send_sems, recv_sems):
        x, y, c, _ = _mesh_pos()
        cp = _remote(g_ref.at[1 - c], r_ref, send_sems.at[0], recv_sems.at[0], (x, y, 1 - c))
        cp.start()
        cp.wait()

    return pl.pallas_call(
        body, name=name, in_specs=[HBM_SPEC], out_specs=HBM_SPEC,
        out_shape=jax.ShapeDtypeStruct(g.shape[1:], BF16),
        scratch_shapes=[pltpu.SemaphoreType.DMA((1,)), pltpu.SemaphoreType.DMA((1,))],
    )(g)


def _pair_sum_in(gw_in, recv, sc):
    _, hd, d_in = gw_in.shape
    tr = min(hd, 256)
    c = lax.axis_index("c")

    def body(c_ref, a_ref, b_ref, o_ref):
        del c_ref
        o_ref[...] = (a_ref[...].astype(F32) + b_ref[...].astype(F32)).astype(BF16)

    blk = pl.BlockSpec((tr, sc), lambda i, j, cc: (i, j))
    gs = pltpu.PrefetchScalarGridSpec(
        num_scalar_prefetch=1, grid=(hd // tr, d_in // sc),
        in_specs=[pl.BlockSpec((None, tr, sc), lambda i, j, cc: (cc[0], i, j)), blk], out_specs=blk)
    return pl.pallas_call(
        body, name="pair_sum_in", grid_spec=gs, out_shape=jax.ShapeDtypeStruct((hd, d_in), BF16),
        compiler_params=_params(dimension_semantics=("parallel", "parallel")),
    )(jnp.reshape(c, (1,)).astype(jnp.int32), gw_in, recv)


def _pair_sum_out(gw_out, recv):
    _, n_chips, hr, d = gw_out.shape
    tr = min(hr, 256)
    c = lax.axis_index("c")

    def body(c_ref, a_ref, b_ref, o_ref):
        del c_ref
        o_ref[...] = (a_ref[...].astype(F32) + b_ref[...].astype(F32)).astype(BF16)

    blk = pl.BlockSpec((None, tr, d), lambda k, i, cc: (k, i, 0))
    gs = pltpu.PrefetchScalarGridSpec(
        num_scalar_prefetch=1, grid=(n_chips, hr // tr),
        in_specs=[pl.BlockSpec((None, None, tr, d), lambda k, i, cc: (cc[0], k, i, 0)), blk], out_specs=blk)
    return pl.pallas_call(
        body, name="pair_sum_out", grid_spec=gs, out_shape=jax.ShapeDtypeStruct((n_chips, hr, d), BF16),
        compiler_params=_params(dimension_semantics=("parallel", "parallel")),
    )(jnp.reshape(c, (1,)).astype(jnp.int32), gw_out, recv)


def _place():
    return jnp.stack([2 * lax.axis_index("x") + lax.axis_index("y"), lax.axis_index("c")]).astype(jnp.int32)


def _chip_sum_in(p_in, r_in, sc):
    hd = p_in.shape[0]
    tr = min(hd, 256)
    nblk = hd // tr

    def body(pos_ref, p_ref, r_ref, o_ref):
        del pos_ref
        acc = p_ref[...].astype(F32)
        for j in range(3):
            acc = acc + r_ref[j].astype(F32)
        o_ref[...] = acc

    gs = pltpu.PrefetchScalarGridSpec(
        num_scalar_prefetch=1, grid=(nblk,),
        in_specs=[pl.BlockSpec((tr, sc), lambda i, pos: (i, pos[0])), pl.BlockSpec((3, tr, sc), lambda i, pos: (0, i, 0))],
        out_specs=pl.BlockSpec((tr, sc), lambda i, pos: (pos[1] * nblk + i, 0)))
    return pl.pallas_call(
        body, name="chip_sum_in", grid_spec=gs, out_shape=jax.ShapeDtypeStruct((2 * hd, sc), F32),
        compiler_params=_params(dimension_semantics=("parallel",)),
    )(_place(), p_in, r_in)


def _chip_sum_out(p_out, r_out):
    _, hr, d = p_out.shape
    tr = min(hr, 256)
    nblk = hr // tr

    def body(pos_ref, p_ref, r_ref, o_ref):
        del pos_ref
        acc = p_ref[...].astype(F32)
        for j in range(3):
            acc = acc + r_ref[j].astype(F32)
        o_ref[...] = acc

    gs = pltpu.PrefetchScalarGridSpec(
        num_scalar_prefetch=1, grid=(nblk,),
        in_specs=[pl.BlockSpec((None, tr, d), lambda i, pos: (pos[0], i, 0)), pl.BlockSpec((3, tr, d), lambda i, pos: (0, i, 0))],
        out_specs=pl.BlockSpec((tr, d), lambda i, pos: (pos[1] * nblk + i, 0)))
    return pl.pallas_call(
        body, name="chip_sum_out", grid_spec=gs, out_shape=jax.ShapeDtypeStruct((2 * hr, d), F32),
        compiler_params=_params(dimension_semantics=("parallel",)),
    )(_place(), p_out, r_out)


def _share_halves(g_in, g_out):
    hd = g_in.shape[0] // 2
    hr = g_out.shape[0] // 2

    def body(gi_in, go_in, gi_ref, go_ref, send_sems, recv_sems):
        del gi_in, go_in
        x, y, c, _ = _mesh_pos()
        sibling = (x, y, 1 - c)
        mine_i = gi_ref.at[pl.ds(pl.multiple_of(c * hd, 8), hd), :]
        mine_o = go_ref.at[pl.ds(pl.multiple_of(c * hr, 8), hr), :]
        a = _remote(mine_i, mine_i, send_sems.at[0], recv_sems.at[0], sibling)
        b = _remote(mine_o, mine_o, send_sems.at[1], recv_sems.at[1], sibling)
        a.start()
        b.start()
        a.wait_send()
        b.wait_send()
        theirs_i = gi_ref.at[pl.ds(pl.multiple_of((1 - c) * hd, 8), hd), :]
        theirs_o = go_ref.at[pl.ds(pl.multiple_of((1 - c) * hr, 8), hr), :]
        _remote(theirs_i, theirs_i, send_sems.at[0], recv_sems.at[0], sibling).wait_recv()
        _remote(theirs_o, theirs_o, send_sems.at[1], recv_sems.at[1], sibling).wait_recv()

    return pl.pallas_call(
        body, name="share_halves",
        in_specs=[HBM_SPEC, HBM_SPEC], out_specs=[HBM_SPEC, HBM_SPEC],
        out_shape=[jax.ShapeDtypeStruct(g_in.shape, F32), jax.ShapeDtypeStruct(g_out.shape, F32)],
        input_output_aliases={0: 0, 1: 1},
        scratch_shapes=[pltpu.SemaphoreType.DMA((2,)), pltpu.SemaphoreType.DMA((2,))],
    )(g_in, g_out)


def _adamw_math(w, g, m, v):
    m_new = ADAM_B1 * m + (1.0 - ADAM_B1) * g
    v_new = ADAM_B2 * v + (1.0 - ADAM_B2) * (g * g)
    m_hat = m_new / (1.0 - ADAM_B1 ** ADAM_STEP)
    v_hat = v_new / (1.0 - ADAM_B2 ** ADAM_STEP)
    delta = -ADAM_LR * (m_hat / (jnp.sqrt(v_hat) + ADAM_EPS) + ADAM_WD * w)
    return delta, m_new, v_new


def _adamw(w, g, m, v, name):
    rows, cols = w.shape
    tr = min(rows, 128)

    def body(w_ref, g_ref, m_ref, v_ref, d_ref, mo_ref, vo_ref):
        delta, m_new, v_new = _adamw_math(w_ref[...], g_ref[...], m_ref[...], v_ref[...])
        d_ref[...] = delta
        mo_ref[...] = m_new
        vo_ref[...] = v_new

    spec = pl.BlockSpec((tr, cols), lambda i: (i, 0))
    shape = jax.ShapeDtypeStruct((rows, cols), F32)
    return pl.pallas_call(
        body, name=name, grid=(rows // tr,), in_specs=[spec] * 4, out_specs=[spec] * 3,
        out_shape=[shape] * 3, compiler_params=_params(dimension_semantics=("parallel",)),
    )(w, g, m, v)


SMALL_ROWS = 8


def _small_allreduce_adamw(part, w_pack, m_pack, v_pack):
    d = part.shape[1]

    def body(part_ref, w_ref, m_ref, v_ref, g_ref, d_ref, mo_ref, vo_ref, buf_ref, send_sems, recv_sems):
        x, y, c, _ = _mesh_pos()
        me = 4 * x + 2 * y + c
        buf_ref[0] = part_ref[...]
        copies = []
        for r in range(1, 8):
            rx, ry, rc = (r >> 2) & 1, (r >> 1) & 1, r & 1
            peer = (x ^ rx, y ^ ry, c ^ rc)
            copies.append(_remote(buf_ref.at[0], buf_ref.at[r], send_sems.at[r - 1], recv_sems.at[r - 1], peer))
        for cp in copies:
            cp.start()
        for cp in copies:
            cp.wait()
        total = buf_ref[me]
        for s in range(1, 8):
            total = total + buf_ref[s ^ me]
        w = w_ref[...]
        row = lax.broadcasted_iota(jnp.int32, (SMALL_ROWS, d), 0)
        l0, l1 = w[3:4], w[4:5]
        mx = jnp.maximum(l0, l1)
        e0, e1 = jnp.exp(l0 - mx), jnp.exp(l1 - mx)
        lb = e0 / (e0 + e1)
        g_l0 = total[3:4] * lb * (1.0 - lb)
        grads = jnp.where(row == 3, g_l0, jnp.where(row == 4, -g_l0, total))
        g_ref[...] = grads
        delta, m_new, v_new = _adamw_math(w, grads, m_ref[...], v_ref[...])
        d_ref[...] = delta
        mo_ref[...] = m_new
        vo_ref[...] = v_new

    vm = pl.BlockSpec(memory_space=pltpu.VMEM)
    shape = jax.ShapeDtypeStruct((SMALL_ROWS, d), F32)
    return pl.pallas_call(
        body, name="small_allreduce_adamw",
        in_specs=[vm] * 4, out_specs=[vm] * 4, out_shape=[shape] * 4,
        scratch_shapes=[pltpu.VMEM((8, SMALL_ROWS, d), F32), pltpu.SemaphoreType.DMA((7,)), pltpu.SemaphoreType.DMA((7,))],
    )(part, w_pack, m_pack, v_pack)


def _pack_small(d, pre, post, rnn, lb, sink, extra=None):
    rows = [pre, post, rnn, lb[0:1], lb[1:2],
            jnp.pad(sink, ((0, 0), (0, d - sink.shape[1]))),
            jnp.zeros((1, d), F32) if extra is None else extra,
            jnp.zeros((1, d), F32)]
    return jnp.concatenate(rows, axis=0)


def _unpack_small(p, n_sink):
    return dict(pre=p[0:1], post=p[1:2], rnn=p[2:3], lb=p[3:5], sink=p[5:6, :n_sink])


def kernel(x, w_in, attn_sinks, lb_logits, rnn_norm, w_out, pre_norm, post_norm, loss_target, m_w_in, m_attn_sinks, m_lb_logits, m_rnn_norm, m_w_out, m_pre_norm, m_post_norm, v_w_in, v_attn_sinks, v_lb_logits, v_rnn_norm, v_w_out, v_pre_norm, v_post_norm):
    t, d = x.shape[1], x.shape[2]
    sc = w_in.shape[2]
    n_sink = attn_sinks.shape[1]
    w_in2, w_out2 = w_in[0], w_out[0]

    w_in_part = _cast_into_gathered(w_in2, "cast_w_in", 1)
    w_out_part = _cast_into_gathered(w_out2, "cast_w_out", 0)
    loss_part, grad_x, (p_in, r_in), (p_out, r_out), small = _local_grads(
        x[0], loss_target[0], w_in_part, w_out_part, attn_sinks, lb_logits, rnn_norm, pre_norm, post_norm, sc)
    g_w_in, g_w_out = _share_halves(_chip_sum_in(p_in, r_in, sc), _chip_sum_out(p_out, r_out))

    d_w_in, nm_w_in, nv_w_in = _adamw(w_in2, g_w_in, m_w_in[0], v_w_in[0], "adamw_w_in")
    d_w_out, nm_w_out, nv_w_out = _adamw(w_out2, g_w_out, m_w_out[0], v_w_out[0], "adamw_w_out")

    lb_part = jnp.concatenate([small["lb"], jnp.zeros_like(small["lb"])], axis=0)
    loss_row = jnp.pad(loss_part[:, :1], ((0, 0), (0, d - 1)))
    part = _pack_small(d, small["pre"], small["post"], small["rnn"], lb_part, small["sink"], loss_row)
    w_pack = _pack_small(d, pre_norm, post_norm, rnn_norm, lb_logits, attn_sinks)
    m_pack = _pack_small(d, m_pre_norm, m_post_norm, m_rnn_norm, m_lb_logits, m_attn_sinks)
    v_pack = _pack_small(d, v_pre_norm, v_post_norm, v_rnn_norm, v_lb_logits, v_attn_sinks)
    g_pack, d_pack, nm_pack, nv_pack = _small_allreduce_adamw(part, w_pack, m_pack, v_pack)
    loss = g_pack[6, 0]
    g, dl, nm, nv = (_unpack_small(p, n_sink) for p in (g_pack, d_pack, nm_pack, nv_pack))

    def ordered(w_in_leaf, w_out_leaf, s):
        return (w_in_leaf[None], s["sink"], s["lb"], s["rnn"], w_out_leaf[None], s["pre"], s["post"])

    return (loss, grad_x[None],
            *ordered(g_w_in, g_w_out, g), *ordered(d_w_in, d_w_out, dl),
            *ordered(nm_w_in, nm_w_out, nm), *ordered(nv_w_in, nv_w_out, nv))
```

```python
import numpy as np
import jax
import jax.numpy as jnp
from jax import lax
from jax.experimental import pallas as pl
from jax.experimental.pallas import tpu as pltpu

F32 = jnp.float32
BF16 = jnp.bfloat16
MESH = pl.DeviceIdType.MESH

NORM_EPS = 1e-6
ATTN_HEAD_DIM = 64
GQA_GROUP = 8
WINDOW = 128
RNN_HEAD_DIM = 128
CHUNK = 64
HALF_CHUNK = CHUNK // 2
ATTN_SCALE = ATTN_HEAD_DIM ** -0.5

ADAM_LR = 0.001
ADAM_B1 = 0.9
ADAM_B2 = 0.999
ADAM_EPS = 1e-08
ADAM_WD = 0.01
ADAM_STEP = 10

LANES = 128
COL_TILE = 512
RNN_GROUP_HEADS = 4
ATTN_GROUP_LANES = 1024
N_CHIPS = 4
VMEM_LIMIT_BYTES = 56 * 1024 * 1024
NEG_BIG = -1e30


def _params(**kw):
    return pltpu.CompilerParams(vmem_limit_bytes=VMEM_LIMIT_BYTES, **kw)


def _sigmoid(x):
    return 1.0 / (1.0 + jnp.exp(-x))


def _dot(a, b):
    return jnp.dot(a, b, preferred_element_type=F32)


def _dot_nt(a, b):
    return lax.dot_general(a, b, (((1,), (1,)), ((), ())), preferred_element_type=F32)


def _dot_tn(a, b):
    return lax.dot_general(a, b, (((0,), (0,)), ((), ())), preferred_element_type=F32)


def _split3(x):
    hi = x.astype(BF16)
    r1 = x - hi.astype(F32)
    mid = r1.astype(BF16)
    lo = (r1 - mid.astype(F32)).astype(BF16)
    return hi, mid, lo


def _tri_dot(tri_bf16, x):
    hi, mid, lo = _split3(x)
    return _dot(tri_bf16, hi) + _dot(tri_bf16, mid) + _dot(tri_bf16, lo)


def _layout(d_model):
    d = d_model
    dkv = d // GQA_GROUP
    orig = dict(aq=0, ak=d, av=d + dkv, ag=d + 2 * dkv)
    base = d + 2 * dkv + d
    orig.update(rq=base, rf=base + d, ri=base + 2 * d, rg=base + 3 * d)
    group_w = RNN_GROUP_HEADS * RNN_HEAD_DIM
    cols = []
    for hg in range(d // group_w):
        for seg in ("rq", "rf", "ri", "rg"):
            cols.append((orig[seg] + hg * group_w, group_w))
    for m in range(d // ATTN_GROUP_LANES):
        for seg in ("aq", "ag"):
            cols.append((orig[seg] + m * ATTN_GROUP_LANES, ATTN_GROUP_LANES))
    cols.append((orig["ak"], dkv))
    cols.append((orig["av"], dkv))
    units = []
    for start, width in cols:
        assert start % LANES == 0 and width % LANES == 0
        units += [start + u for u in range(0, width, LANES)]
    per = COL_TILE // LANES
    assert len(units) % per == 0
    tiles = []
    for t in range(len(units) // per):
        run = units[t * per:(t + 1) * per]
        assert run[0] % COL_TILE == 0 and all(run[i] == run[0] + i * LANES for i in range(per))
        tiles.append(run[0] // COL_TILE)
    return dict(a_off=4 * d, k_off=6 * d, v_off=6 * d + dkv, total=6 * d + 2 * dkv,
                perm=np.asarray(tiles, np.int32))


def _chip_index():
    return jnp.reshape(2 * lax.axis_index("x") + lax.axis_index("y"), (1,)).astype(jnp.int32)


def _cast_into_gathered(a, name, axis):
    rows, cols = a.shape
    tr = min(rows, 512)
    nblk = rows // tr

    def body(me_ref, a_ref, o_ref):
        del me_ref
        o_ref[...] = a_ref[...].astype(BF16)

    if axis == 1:
        out_spec = pl.BlockSpec((tr, cols), lambda i, me: (i, me[0]))
        shape = (rows, N_CHIPS * cols)
    else:
        out_spec = pl.BlockSpec((tr, cols), lambda i, me: (me[0] * nblk + i, 0))
        shape = (N_CHIPS * rows, cols)
    gs = pltpu.PrefetchScalarGridSpec(num_scalar_prefetch=1, grid=(nblk,),
                                      in_specs=[pl.BlockSpec((tr, cols), lambda i, me: (i, 0))], out_specs=out_spec)
    return pl.pallas_call(
        body, name=name, grid_spec=gs, out_shape=jax.ShapeDtypeStruct(shape, BF16),
        compiler_params=_params(dimension_semantics=("parallel",)),
    )(_chip_index(), a)


def _transpose_bf16(a, name):
    rows, cols = a.shape
    tr = min(rows, 256)
    tc = min(cols, 2048)

    def body(a_ref, o_ref):
        o_ref[...] = a_ref[...].astype(F32).T.astype(BF16)

    return pl.pallas_call(
        body, name=name, grid=(rows // tr, cols // tc),
        in_specs=[pl.BlockSpec((tr, tc), lambda i, j: (i, j))],
        out_specs=pl.BlockSpec((tc, tr), lambda i, j: (j, i)),
        out_shape=jax.ShapeDtypeStruct((cols, rows), BF16),
        compiler_params=_params(dimension_semantics=("parallel", "parallel")),
    )(a)


def _prenorm_fwd(x, gain):
    t, d = x.shape
    tm = min(t, 256)

    def body(x_ref, g_ref, h_ref, ht_ref):
        xv = x_ref[...]
        r = lax.rsqrt(jnp.mean(xv * xv, axis=-1, keepdims=True) + NORM_EPS)
        h = (xv * r) * g_ref[...]
        h_ref[...] = h.astype(BF16)
        ht_ref[...] = h.T.astype(BF16)

    return pl.pallas_call(
        body, name="prenorm_fwd", grid=(t // tm,),
        in_specs=[pl.BlockSpec((tm, d), lambda i: (i, 0)), pl.BlockSpec((1, d), lambda i: (0, 0))],
        out_specs=[pl.BlockSpec((tm, d), lambda i: (i, 0)), pl.BlockSpec((d, tm), lambda i: (0, i))],
        out_shape=[jax.ShapeDtypeStruct((t, d), BF16), jax.ShapeDtypeStruct((d, t), BF16)],
        compiler_params=_params(dimension_semantics=("parallel",)),
    )(x, gain)


def _post_loss(x, y, target, gain):
    t, d = x.shape
    tm = min(t, 256)
    inv_d = 1.0 / d

    def body(x_ref, y_ref, t_ref, g_ref, dy_ref, dz_ref, gp_ref, loss_ref):
        i = pl.program_id(0)
        yv = y_ref[...]
        gain_v = g_ref[...]
        r = lax.rsqrt(jnp.mean(yv * yv, axis=-1, keepdims=True) + NORM_EPS)
        n = yv * r
        e = (x_ref[...] + n * gain_v) - t_ref[...]
        dz = e * inv_d
        dn = dz * gain_v
        dy = r * (dn - n * jnp.mean(dn * n, axis=-1, keepdims=True))
        dy_ref[...] = dy.astype(BF16)
        dz_ref[...] = dz

        @pl.when(i == 0)
        def _():
            gp_ref[...] = jnp.zeros_like(gp_ref)
            loss_ref[...] = jnp.zeros_like(loss_ref)

        gp_ref[...] += jnp.sum(dz * n, axis=0, keepdims=True)
        row = jnp.sum(e * e, axis=-1, keepdims=True)
        loss_ref[...] += jnp.full(loss_ref.shape, 0.5 * inv_d * jnp.sum(row), F32)

    row_spec = pl.BlockSpec((tm, d), lambda i: (i, 0))
    vec_spec = pl.BlockSpec((1, d), lambda i: (0, 0))
    return pl.pallas_call(
        body, name="post_loss", grid=(t // tm,),
        in_specs=[row_spec, row_spec, row_spec, vec_spec],
        out_specs=[row_spec, row_spec, vec_spec, pl.BlockSpec((1, LANES), lambda i: (0, 0))],
        out_shape=[jax.ShapeDtypeStruct((t, d), BF16), jax.ShapeDtypeStruct((t, d), F32),
                   jax.ShapeDtypeStruct((1, d), F32), jax.ShapeDtypeStruct((1, LANES), F32)],
        compiler_params=_params(dimension_semantics=("arbitrary",)),
    )(x, y, target, gain)


def _prenorm_bwd(x, dh, dz, gain):
    t, d = x.shape
    tm = min(t, 256)

    def body(x_ref, dh_ref, dz_ref, g_ref, gx_ref, gp_ref):
        i = pl.program_id(0)
        xv = x_ref[...]
        r = lax.rsqrt(jnp.mean(xv * xv, axis=-1, keepdims=True) + NORM_EPS)
        n = xv * r
        dhv = dh_ref[...]
        dn = dhv * g_ref[...]
        gx_ref[...] = dz_ref[...] + r * (dn - n * jnp.mean(dn * n, axis=-1, keepdims=True))

        @pl.when(i == 0)
        def _():
            gp_ref[...] = jnp.zeros_like(gp_ref)

        gp_ref[...] += jnp.sum(dhv * n, axis=0, keepdims=True)

    row_spec = pl.BlockSpec((tm, d), lambda i: (i, 0))
    vec_spec = pl.BlockSpec((1, d), lambda i: (0, 0))
    return pl.pallas_call(
        body, name="prenorm_bwd", grid=(t // tm,),
        in_specs=[row_spec, row_spec, row_spec, vec_spec],
        out_specs=[row_spec, vec_spec],
        out_shape=[jax.ShapeDtypeStruct((t, d), F32), jax.ShapeDtypeStruct((1, d), F32)],
        compiler_params=_params(dimension_semantics=("arbitrary",)),
    )(x, dh, dz, gain)


class _Rider:
    def __init__(self, operands, out_shapes, aliases, n_sems, stages):
        self.operands = tuple(operands)
        self.out_shapes = tuple(out_shapes)
        self.aliases = dict(aliases)
        self.n_sems = n_sems
        self.stages = stages

    def scratch(self):
        return [pltpu.SemaphoreType.DMA((self.n_sems,)), pltpu.SemaphoreType.DMA((self.n_sems,))]

    def emit(self, step, n_steps, in_refs, out_refs, send_sems, recv_sems):
        for frac, fn in self.stages(in_refs, out_refs, send_sems, recv_sems):
            at = min(n_steps - 1, int(frac * (n_steps - 1) + 0.5))
            pl.when(step == at)(fn)


def _matmul(name, a, b, *, out_shape, grid, a_spec, b_spec, o_spec, nt=False, perm=None, rider=None):
    nk = grid[2]
    n_steps = grid[0] * grid[1] * grid[2]
    tm, tn = [s for s in o_spec.block_shape if s is not None][-2:]
    acc_in_out = out_shape.dtype == F32
    n_pre = 0 if perm is None else 1
    n_rin = 0 if rider is None else len(rider.operands)
    n_rout = 0 if rider is None else len(rider.out_shapes)
    use_acc = not (nk == 1 or acc_in_out)

    def body(*refs):
        refs = refs[n_pre:]
        a_ref, b_ref = refs[:2]
        rin = refs[2:2 + n_rin]
        o_ref = refs[2 + n_rin]
        rout = refs[3 + n_rin:3 + n_rin + n_rout]
        scratch_refs = refs[3 + n_rin + n_rout:]
        if rider is not None:
            step = (pl.program_id(0) * grid[1] + pl.program_id(1)) * grid[2] + pl.program_id(2)
            rider.emit(step, n_steps, rin, rout, scratch_refs[-2], scratch_refs[-1])
        part = _dot_nt(a_ref[...], b_ref[...]) if nt else _dot(a_ref[...], b_ref[...])
        if nk == 1:
            o_ref[...] = part.astype(o_ref.dtype)
            return
        acc_ref = o_ref if acc_in_out else scratch_refs[0]
        k = pl.program_id(2)

        @pl.when(k == 0)
        def _():
            acc_ref[...] = part

        @pl.when(k > 0)
        def _():
            acc_ref[...] += part

        if not acc_in_out:
            @pl.when(k == nk - 1)
            def _():
                o_ref[...] = acc_ref[...].astype(o_ref.dtype)

    scratch = [pltpu.VMEM((tm, tn), F32)] if use_acc else []
    in_specs = [a_spec, b_spec] + [HBM_SPEC] * n_rin
    out_specs = [o_spec] + [HBM_SPEC] * n_rout
    out_shapes = [out_shape]
    operands = [a, b]
    aliases = {}
    sem = ("parallel", "parallel", "arbitrary")
    if rider is not None:
        scratch += rider.scratch()
        out_shapes += list(rider.out_shapes)
        operands += list(rider.operands)
        aliases = {n_pre + 2 + i: 1 + o for i, o in rider.aliases.items()}
        sem = ("arbitrary", "arbitrary", "arbitrary")
    cp = _params(dimension_semantics=sem)
    if perm is None:
        return pl.pallas_call(body, name=name, grid=grid, in_specs=in_specs, out_specs=out_specs,
                              out_shape=out_shapes, scratch_shapes=scratch, input_output_aliases=aliases,
                              compiler_params=cp)(*operands)
    gs = pltpu.PrefetchScalarGridSpec(num_scalar_prefetch=1, grid=grid, in_specs=in_specs,
                                      out_specs=out_specs, scratch_shapes=scratch)
    return pl.pallas_call(body, name=name, grid_spec=gs, out_shape=out_shapes, input_output_aliases=aliases,
                          compiler_params=cp)(jnp.asarray(perm), *operands)


def _proj_mm(h, w_full, perm, rider=None):
    t, d = h.shape
    n_tiles = len(perm)
    tm = min(t, 1024)
    return _matmul(
        "proj_mm", h, w_full, perm=perm, rider=rider, grid=(t // tm, n_tiles, 1),
        out_shape=jax.ShapeDtypeStruct((t, n_tiles * COL_TILE), F32),
        a_spec=pl.BlockSpec((tm, d), lambda i, j, k, p: (i, 0)),
        b_spec=pl.BlockSpec((d, COL_TILE), lambda i, j, k, p: (0, p[j])),
        o_spec=pl.BlockSpec((tm, COL_TILE), lambda i, j, k, p: (i, j)))


def _proj_gather_mm(h, wi_part, perm, sc):
    t, d = h.shape
    n_tiles = len(perm)
    tm = min(t, 1024)
    n_i = t // tm
    hd = d // 2
    nf = sc // COL_TILE
    rem = sc - nf * COL_TILE
    assert 2 * rem == COL_TILE and n_tiles == N_CHIPS * nf + 2
    n_chunks = next(q for q in (4, 3, 2, 1) if nf % q == 0)
    tpc = nf // n_chunks
    n_kinds = n_chunks + 1
    n_sems = 3 * n_kinds
    rem_at = nf + 3 * n_chunks * tpc

    def first_full(chip):
        return (chip * sc + (rem if chip % 2 else 0)) // COL_TILE

    inverse = np.argsort(perm)
    table = np.zeros((N_CHIPS, 2, n_tiles), np.int32)
    for chip in range(N_CHIPS):
        seq = list(range(first_full(chip), first_full(chip) + nf))
        for q in range(n_chunks):
            for src in (chip ^ 2, chip ^ 1, chip ^ 3):
                seq += list(range(first_full(src) + q * tpc, first_full(src) + (q + 1) * tpc))
        seq += [first_full(chip - chip % 2) + nf, first_full((chip ^ 2) - chip % 2) + nf]
        assert sorted(seq) == list(range(n_tiles)), seq
        table[chip, 0] = inverse[seq]
        table[chip, 1] = seq
    me_chip = 2 * lax.axis_index("x") + lax.axis_index("y")
    tab = lax.dynamic_index_in_dim(jnp.asarray(table), me_chip, 0, keepdims=False)

    def body(tab_ref, h_hbm, wi_in, proj_ref, full, hbuf, bbuf, local_sems, send_sems, recv_sems):
        del wi_in
        jj = pl.program_id(0)
        i = pl.program_id(1)
        x, y, c, chips = _mesh_pos()
        sibling = (x, y, 1 - c)

        def piece(chip, half, kind):
            odd = chip % 2
            if kind == n_chunks:
                start, width = chip * sc + (1 - odd) * (nf * COL_TILE), rem
            else:
                start, width = chip * sc + odd * rem + kind * (tpc * COL_TILE), tpc * COL_TILE
            return full.at[pl.ds(half * hd, hd), pl.ds(pl.multiple_of(start, LANES), width)]

        def ici(j, kind):
            mine = piece(2 * x + y, c, kind)
            k = j * n_kinds + kind
            return _remote(mine, mine, send_sems.at[k], recv_sems.at[k], (chips[j][0], chips[j][1], c))

        def landed(j, kind):
            blk = piece(2 * chips[j][0] + chips[j][1], c, kind)
            k = j * n_kinds + kind
            return _remote(blk, blk, send_sems.at[k], recv_sems.at[k], sibling)

        def passed(j, kind, half):
            blk = piece(2 * chips[j][0] + chips[j][1], half, kind)
            k = n_sems + j * n_kinds + kind
            return _remote(blk, blk, send_sems.at[k], recv_sems.at[k], sibling)

        def fetch(pos, slot):
            col = pl.multiple_of(tab_ref[1, pos] * COL_TILE, LANES)
            return pltpu.make_async_copy(full.at[:, pl.ds(col, COL_TILE)], bbuf.at[slot], local_sems.at[slot])

        def load_h():
            return pltpu.make_async_copy(h_hbm, hbuf, local_sems.at[2])

        def relay(j, kind):
            landed(j, kind).wait_recv()
            passed(j, kind, c).start()

        @pl.when(i == 0)
        def _():
            @pl.when(jj == 0)
            def _():
                load_h().start()
                for kind in range(n_kinds):
                    for j in range(3):
                        ici(j, kind).start()
                fetch(0, 0).start()
                load_h().wait()

            for n in range(3 * n_chunks):
                at = nf + n * tpc
                pl.when(jj == at - 2)(lambda n=n: relay(n % 3, n // 3))
                pl.when(jj == at - 1)(lambda n=n: passed(n % 3, n // 3, 1 - c).wait_recv())

            @pl.when(jj == rem_at - 2)
            def _():
                for j in range(3):
                    relay(j, n_chunks)

            @pl.when(jj == rem_at - 1)
            def _():
                for j in range(3):
                    passed(j, n_chunks, 1 - c).wait_recv()

            @pl.when(jj + 1 < n_tiles)
            def _():
                fetch(jj + 1, (jj + 1) % 2).start()

            fetch(jj, jj % 2).wait()

            @pl.when(jj == n_tiles - 1)
            def _():
                for kind in range(n_kinds):
                    for j in range(3):
                        ici(j, kind).wait_send()
                        passed(j, kind, c).wait_send()

        rows = pl.ds(pl.multiple_of(i * tm, tm), tm)
        proj_ref[...] = _dot(hbuf[rows, :], bbuf[jj % 2])

    gs = pltpu.PrefetchScalarGridSpec(
        num_scalar_prefetch=1, grid=(n_tiles, n_i),
        in_specs=[HBM_SPEC, HBM_SPEC],
        out_specs=[pl.BlockSpec((tm, COL_TILE), lambda jj, i, tb: (i, tb[0, jj])), HBM_SPEC],
        scratch_shapes=[pltpu.VMEM((t, d), BF16), pltpu.VMEM((2, d, COL_TILE), BF16), pltpu.SemaphoreType.DMA((3,)),
                        pltpu.SemaphoreType.DMA((2 * n_sems,)), pltpu.SemaphoreType.DMA((2 * n_sems,))])
    return pl.pallas_call(
        body, name="proj_gather_mm", grid_spec=gs,
        out_shape=[jax.ShapeDtypeStruct((t, n_tiles * COL_TILE), F32), jax.ShapeDtypeStruct(wi_part.shape, BF16)],
        input_output_aliases={2: 1},
        compiler_params=_params(dimension_semantics=("arbitrary", "arbitrary")),
    )(tab, h, wi_part)


def _gw_in_mm(name, ht, dproj, perm, half, rider=None):
    d, t = ht.shape
    n_tiles = len(perm)
    hd = d // 2
    tm = min(hd, 1024)
    per_half = hd // tm
    table = jnp.concatenate([jnp.asarray(perm), jnp.reshape(half, (1,)).astype(jnp.int32)])
    return _matmul(
        name, ht, dproj, perm=table, rider=rider, grid=(per_half, n_tiles, 1),
        out_shape=jax.ShapeDtypeStruct((hd, n_tiles * COL_TILE), BF16),
        a_spec=pl.BlockSpec((tm, t), lambda i, j, k, p: (p[n_tiles] * per_half + i, 0)),
        b_spec=pl.BlockSpec((t, COL_TILE), lambda i, j, k, p: (0, j)),
        o_spec=pl.BlockSpec((tm, COL_TILE), lambda i, j, k, p: (i, p[j])))


def _dh_mm(dproj, w_full, perm, rider=None):
    t = dproj.shape[0]
    d = w_full.shape[0]
    n_tiles = len(perm)
    tm = min(t, 2048)
    tn = min(d, 2048)
    return _matmul(
        "dh_mm", dproj, w_full, perm=perm, rider=rider, nt=True, grid=(t // tm, d // tn, n_tiles),
        out_shape=jax.ShapeDtypeStruct((t, d), F32),
        a_spec=pl.BlockSpec((tm, COL_TILE), lambda i, j, k, p: (i, k)),
        b_spec=pl.BlockSpec((tn, COL_TILE), lambda i, j, k, p: (j, p[k])),
        o_spec=pl.BlockSpec((tm, tn), lambda i, j, k, p: (i, j)))


def _out_mm(mixed, w_out_full):
    t, dm = mixed.shape
    d = w_out_full.shape[1]
    tm = min(t, 1024)
    tn = min(d, 512)
    tk = min(dm, 4096)
    return _matmul(
        "out_mm", mixed, w_out_full, grid=(t // tm, d // tn, dm // tk),
        out_shape=jax.ShapeDtypeStruct((t, d), F32),
        a_spec=pl.BlockSpec((tm, tk), lambda i, j, k: (i, k)),
        b_spec=pl.BlockSpec((tk, tn), lambda i, j, k: (k, j)),
        o_spec=pl.BlockSpec((tm, tn), lambda i, j, k: (i, j)))[0]


def _dmixed_mm(dy, w_out_full):
    t, d = dy.shape
    dm = w_out_full.shape[0]
    tm = min(t, 1024)
    tn = min(dm, 1024)
    return _matmul(
        "dmixed_mm", dy, w_out_full, nt=True, grid=(t // tm, dm // tn, 1),
        out_shape=jax.ShapeDtypeStruct((t, dm), F32),
        a_spec=pl.BlockSpec((tm, d), lambda i, j, k: (i, 0)),
        b_spec=pl.BlockSpec((tn, d), lambda i, j, k: (j, 0)),
        o_spec=pl.BlockSpec((tm, tn), lambda i, j, k: (i, j)))[0]


def _gw_out_mm(mixed_t, dy):
    dm, t = mixed_t.shape
    d = dy.shape[1]
    hr = dm // (2 * N_CHIPS)
    tn = min(d, 1024)
    return _matmul(
        "gw_out_mm", mixed_t, dy, grid=(dm // hr, d // tn, 1),
        out_shape=jax.ShapeDtypeStruct((2, N_CHIPS, hr, d), BF16),
        a_spec=pl.BlockSpec((hr, t), lambda i, j, k: (i, 0)),
        b_spec=pl.BlockSpec((t, tn), lambda i, j, k: (0, j)),
        o_spec=pl.BlockSpec((None, None, hr, tn), lambda i, j, k: (i % 2, i // 2, 0, j)))[0]


def _lane_half():
    return lax.broadcasted_iota(jnp.int32, (WINDOW, LANES), 1) // ATTN_HEAD_DIM


def _dup_kv(tile, kh):
    return jnp.where(_lane_half() == kh, tile, pltpu.roll(tile, ATTN_HEAD_DIM, 1))


def _stack_heads(tiles, kh):
    half = _lane_half()
    pieces = []
    for g in range(GQA_GROUP):
        pieces.append(jnp.where(half == g % 2, tiles[4 * kh + g // 2], 0.0))
    return jnp.concatenate(pieces, axis=0)


def _unstack_heads(stacked):
    half = _lane_half()
    out = []
    for j in range(GQA_GROUP // 2):
        a = stacked[(2 * j) * WINDOW:(2 * j + 1) * WINDOW]
        b = stacked[(2 * j + 1) * WINDOW:(2 * j + 2) * WINDOW]
        out.append(jnp.where(half == 0, a, b))
    return out


def _attn_probs(qs, kcat, sink_col, n):
    rows = GQA_GROUP * WINDOW
    s = _dot_nt(qs, kcat)
    qi = lax.broadcasted_iota(jnp.int32, (rows, 2 * WINDOW), 0) % WINDOW
    kj = lax.broadcasted_iota(jnp.int32, (rows, 2 * WINDOW), 1)
    first_key = WINDOW * (1 - jnp.minimum(n, 1))
    valid = (kj > qi) & (kj <= qi + WINDOW) & (kj >= first_key)
    s = jnp.where(valid, s, NEG_BIG)
    mx = jnp.maximum(jnp.max(s, axis=-1, keepdims=True), sink_col)
    p = jnp.exp(s - mx)
    p_sink = jnp.exp(sink_col - mx)
    inv = 1.0 / (jnp.sum(p, axis=-1, keepdims=True) + p_sink)
    return p * inv, p_sink * inv


def _attn_operands(sink_ref, q_tiles, kp_ref, kc_ref, vp_ref, vc_ref, m, kh):
    qs = _stack_heads([qt * ATTN_SCALE for qt in q_tiles], kh).astype(BF16)
    kcat = jnp.concatenate([_dup_kv(kp_ref[...], kh), _dup_kv(kc_ref[...], kh)], axis=0).astype(BF16)
    vcat = jnp.concatenate([_dup_kv(vp_ref[...], kh), _dup_kv(vc_ref[...], kh)], axis=0).astype(BF16)
    heads_per_group = ATTN_GROUP_LANES // ATTN_HEAD_DIM
    sink_col = jnp.concatenate(
        [jnp.full((WINDOW, 1), sink_ref[0, m * heads_per_group + kh * GQA_GROUP + g], F32)
         for g in range(GQA_GROUP)], axis=0)
    return qs, kcat, vcat, sink_col


def _attn_specs(lay, d):
    a_blk = lay["a_off"] // (2 * ATTN_GROUP_LANES)
    k_blk = lay["k_off"] // LANES
    v_blk = lay["v_off"] // LANES
    qg = pl.BlockSpec((WINDOW, 2 * ATTN_GROUP_LANES), lambda m, n: (n, a_blk + m))
    kp = pl.BlockSpec((WINDOW, LANES), lambda m, n: (jnp.maximum(n - 1, 0), k_blk + m))
    kc = pl.BlockSpec((WINDOW, LANES), lambda m, n: (n, k_blk + m))
    vp = pl.BlockSpec((WINDOW, LANES), lambda m, n: (jnp.maximum(n - 1, 0), v_blk + m))
    vc = pl.BlockSpec((WINDOW, LANES), lambda m, n: (n, v_blk + m))
    return qg, kp, kc, vp, vc


def _attn_fwd(proj, sinks, lay, d, rider=None):
    t = proj.shape[0]
    n_groups = d // ATTN_GROUP_LANES
    n_blocks = t // WINDOW
    pairs = ATTN_GROUP_LANES // LANES
    n_rin = 0 if rider is None else len(rider.operands)
    n_rout = 0 if rider is None else len(rider.out_shapes)

    def body(*refs):
        sink_ref, qg_ref, kp_ref, kc_ref, vp_ref, vc_ref = refs[:6]
        rin = refs[6:6 + n_rin]
        mix_ref, o_ref = refs[6 + n_rin:8 + n_rin]
        rout = refs[8 + n_rin:8 + n_rin + n_rout]
        m = pl.program_id(0)
        n = pl.program_id(1)
        if rider is not None:
            rider.emit(m * n_blocks + n, n_groups * n_blocks, rin, rout, refs[-2], refs[-1])
        q_tiles = [qg_ref[:, p * LANES:(p + 1) * LANES] for p in range(pairs)]
        for kh in range(2):
            qs, kcat, vcat, sink_col = _attn_operands(sink_ref, q_tiles, kp_ref, kc_ref, vp_ref, vc_ref, m, kh)
            probs, _ = _attn_probs(qs, kcat, sink_col, n)
            out = _dot(probs.astype(BF16), vcat)
            for j, tile in enumerate(_unstack_heads(out)):
                p = 4 * kh + j
                lanes = slice(p * LANES, (p + 1) * LANES)
                gate = qg_ref[:, ATTN_GROUP_LANES + p * LANES:ATTN_GROUP_LANES + (p + 1) * LANES]
                o_ref[:, lanes] = tile
                mix_ref[:, lanes] = (tile * (gate * _sigmoid(gate))).astype(BF16)

    qg, kp, kc, vp, vc = _attn_specs(lay, d)
    out_blk = pl.BlockSpec((WINDOW, ATTN_GROUP_LANES), lambda m, n: (n, m))
    out_shapes = [jax.ShapeDtypeStruct((t, 2 * d), BF16), jax.ShapeDtypeStruct((t, d), F32)]
    operands = [sinks, proj, proj, proj, proj, proj]
    scratch, aliases, sem = [], {}, ("parallel", "parallel")
    if rider is not None:
        scratch = rider.scratch()
        out_shapes += list(rider.out_shapes)
        operands += list(rider.operands)
        aliases = {6 + i: 2 + o for i, o in rider.aliases.items()}
        sem = ("arbitrary", "arbitrary")
    return pl.pallas_call(
        body, name="attn_fwd", grid=(n_groups, n_blocks),
        in_specs=[pl.BlockSpec(memory_space=pltpu.SMEM), qg, kp, kc, vp, vc] + [HBM_SPEC] * n_rin,
        out_specs=[out_blk, out_blk] + [HBM_SPEC] * n_rout,
        out_shape=out_shapes, scratch_shapes=scratch, input_output_aliases=aliases,
        compiler_params=_params(dimension_semantics=sem),
    )(*operands)


def _attn_bwd(proj, sinks, attn_o, dmixed, dproj, lay, d):
    t = proj.shape[0]
    n_groups = d // ATTN_GROUP_LANES
    pairs = ATTN_GROUP_LANES // LANES
    kv_w = n_groups * LANES

    def body(sink_ref, qg_ref, kp_ref, kc_ref, vp_ref, vc_ref, o_ref, dm_ref, dproj_hbm,
             dqg_ref, dkc_ref, dkp_ref, dvc_ref, dvp_ref, dsink_ref):
        del dproj_hbm
        m = pl.program_id(0)
        n = pl.program_id(1)
        half = _lane_half()
        q_tiles = [qg_ref[:, p * LANES:(p + 1) * LANES] for p in range(pairs)]
        do_tiles, o_tiles = [], []
        for p in range(pairs):
            lanes = slice(p * LANES, (p + 1) * LANES)
            gate = qg_ref[:, ATTN_GROUP_LANES + p * LANES:ATTN_GROUP_LANES + (p + 1) * LANES]
            sg = _sigmoid(gate)
            dmix = dm_ref[:, lanes]
            ov = o_ref[:, lanes]
            dqg_ref[:, ATTN_GROUP_LANES + p * LANES:ATTN_GROUP_LANES + (p + 1) * LANES] = (
                dmix * ov * (sg * (1.0 + gate * (1.0 - sg)))).astype(BF16)
            do_tiles.append(dmix * (gate * sg))
            o_tiles.append(ov)

        sub = lax.broadcasted_iota(jnp.int32, (8, LANES), 0)
        lane = lax.broadcasted_iota(jnp.int32, (8, LANES), 1)
        dsink = jnp.zeros((8, LANES), F32)
        dk_cur = dk_prev = dv_cur = dv_prev = jnp.zeros((WINDOW, LANES), F32)
        for kh in range(2):
            qs, kcat, vcat, sink_col = _attn_operands(sink_ref, q_tiles, kp_ref, kc_ref, vp_ref, vc_ref, m, kh)
            probs, p_sink = _attn_probs(qs, kcat, sink_col, n)
            dos = _stack_heads(do_tiles, kh)
            delta = jnp.sum(dos * _stack_heads(o_tiles, kh), axis=-1, keepdims=True)
            dos = dos.astype(BF16)
            dp = _dot_nt(dos, vcat)
            ds = (probs * (dp - delta)).astype(BF16)
            dv = _dot_tn(probs.astype(BF16), dos)
            dv = dv + pltpu.roll(dv, ATTN_HEAD_DIM, 1)
            dk = _dot_tn(ds, qs)
            dk = dk + pltpu.roll(dk, ATTN_HEAD_DIM, 1)
            dq = _dot(ds, kcat)
            for j, tile in enumerate(_unstack_heads(dq)):
                p = 4 * kh + j
                dqg_ref[:, p * LANES:(p + 1) * LANES] = (tile * ATTN_SCALE).astype(BF16)
            dk_prev = jnp.where(half == kh, dk[:WINDOW], dk_prev)
            dk_cur = jnp.where(half == kh, dk[WINDOW:], dk_cur)
            dv_prev = jnp.where(half == kh, dv[:WINDOW], dv_prev)
            dv_cur = jnp.where(half == kh, dv[WINDOW:], dv_cur)
            sink_terms = p_sink * delta
            for g in range(GQA_GROUP):
                val = -jnp.sum(sink_terms[g * WINDOW:(g + 1) * WINDOW])
                dsink = dsink + jnp.where((sub == 0) & (lane == kh * GQA_GROUP + g), val, 0.0)
        dkc_ref[...] = dk_cur
        dkp_ref[...] = dk_prev
        dvc_ref[...] = dv_cur
        dvp_ref[...] = dv_prev

        @pl.when(n == 0)
        def _():
            dsink_ref[...] = jnp.zeros_like(dsink_ref)

        dsink_ref[...] += dsink

    qg, kp, kc, vp, vc = _attn_specs(lay, d)
    a_blk = lay["a_off"] // (2 * ATTN_GROUP_LANES)
    grp = pl.BlockSpec((WINDOW, ATTN_GROUP_LANES), lambda m, n: (n, m))
    kv_blk = pl.BlockSpec((WINDOW, LANES), lambda m, n: (n, m))
    kv_shape = jax.ShapeDtypeStruct((t, kv_w), F32)
    outs = pl.pallas_call(
        body, name="attn_bwd", grid=(n_groups, t // WINDOW),
        in_specs=[pl.BlockSpec(memory_space=pltpu.SMEM), qg, kp, kc, vp, vc, grp, grp,
                  pl.BlockSpec(memory_space=pl.ANY)],
        out_specs=[pl.BlockSpec((WINDOW, 2 * ATTN_GROUP_LANES), lambda m, n: (n, a_blk + m)),
                   kv_blk, kv_blk, kv_blk, kv_blk, pl.BlockSpec((8, LANES), lambda m, n: (m, 0))],
        out_shape=[jax.ShapeDtypeStruct(dproj.shape, BF16), kv_shape, kv_shape, kv_shape, kv_shape,
                   jax.ShapeDtypeStruct((n_groups * 8, LANES), F32)],
        input_output_aliases={8: 0},
        compiler_params=_params(dimension_semantics=("parallel", "arbitrary")),
    )(sinks, proj, proj, proj, proj, proj, attn_o, dmixed, dproj)
    return outs


def _kv_combine(dkc, dkp, dvc, dvp, dproj, lay):
    t, kv_w = dkc.shape
    nb = t // WINDOW
    kv_blk_idx = lay["k_off"] // (2 * kv_w)

    def body(dkc_ref, dkp_ref, dvc_ref, dvp_ref, dproj_hbm, o_ref):
        del dproj_hbm
        keep = (pl.program_id(0) < nb - 1).astype(F32)
        o_ref[:, :kv_w] = (dkc_ref[...] + keep * dkp_ref[...]).astype(BF16)
        o_ref[:, kv_w:] = (dvc_ref[...] + keep * dvp_ref[...]).astype(BF16)

    cur = pl.BlockSpec((WINDOW, kv_w), lambda n: (n, 0))
    nxt = pl.BlockSpec((WINDOW, kv_w), lambda n: (jnp.minimum(n + 1, nb - 1), 0))
    return pl.pallas_call(
        body, name="kv_combine", grid=(nb,),
        in_specs=[cur, nxt, cur, nxt, pl.BlockSpec(memory_space=pl.ANY)],
        out_specs=pl.BlockSpec((WINDOW, 2 * kv_w), lambda n: (n, kv_blk_idx)),
        out_shape=jax.ShapeDtypeStruct(dproj.shape, BF16),
        input_output_aliases={4: 0},
        compiler_params=_params(dimension_semantics=("parallel",)),
    )(dkc, dkp, dvc, dvp, dproj)


def _lower_bound(lbl_ref):
    l0 = lbl_ref[0:1, :]
    l1 = lbl_ref[1:2, :]
    mx = jnp.maximum(l0, l1)
    e0 = jnp.exp(l0 - mx)
    e1 = jnp.exp(l1 - mx)
    return e0 / (e0 + e1)


def _chunk_masks():
    ti = lax.broadcasted_iota(jnp.int32, (CHUNK, CHUNK), 0)
    si = lax.broadcasted_iota(jnp.int32, (CHUNK, CHUNK), 1)
    diag = ((ti // HALF_CHUNK) == (si // HALF_CHUNK)) & (si <= ti)
    off = (ti >= HALF_CHUNK) & (si < HALF_CHUNK)
    lower = (si <= ti).astype(BF16)
    upper = (si >= ti).astype(BF16)
    return diag, off, lower, upper


def _rnn_gates(rq, rf, lb):
    sf = _sigmoid(rf)
    f = lb + (1.0 - lb) * sf
    sq = _sigmoid(rq)
    return sf, f, jnp.log(f), 1.0 - f, sq, rq * sq


def _rnn_decays(g_cum):
    row = lax.broadcasted_iota(jnp.int32, g_cum.shape, 0)
    ref_d = jnp.where(row < HALF_CHUNK, g_cum[HALF_CHUNK // 2 - 1:HALF_CHUNK // 2],
                      g_cum[HALF_CHUNK + HALF_CHUNK // 2 - 1:HALF_CHUNK + HALF_CHUNK // 2])
    ref_o = g_cum[HALF_CHUNK - 1:HALF_CHUNK]
    last = g_cum[CHUNK - 1:CHUNK]
    return dict(eq_d=jnp.exp(g_cum - ref_d), ek_d=jnp.exp(ref_d - g_cum),
                eq_o=jnp.exp(jnp.minimum(g_cum - ref_o, 0.0)), ek_o=jnp.exp(jnp.minimum(ref_o - g_cum, 0.0)),
                eg=jnp.exp(g_cum), ekl=jnp.exp(last - g_cum), e_last=jnp.exp(last))


def _head(a, j):
    return a[:, j * RNN_HEAD_DIM:(j + 1) * RNN_HEAD_DIM]


def _rnn_specs(t, tb, d):
    gw = RNN_GROUP_HEADS * RNN_HEAD_DIM
    return gw, t // tb, tb // CHUNK


def _rnn_fwd(proj, lb_logits, rnn_gain, mixed, d):
    t = proj.shape[0]
    tb = min(t, 256)
    gw, ntb, nch = _rnn_specs(t, tb, d)
    n_groups = d // gw
    n_heads = d // RNN_HEAD_DIM

    def body(blk_ref, lbl_ref, gain_ref, mixed_hbm, mix_ref, o_ref, st_out_ref, st_ref):
        del mixed_hbm

        @pl.when(pl.program_id(1) == 0)
        def _():
            st_ref[...] = jnp.zeros_like(st_ref)

        lb = _lower_bound(lbl_ref)
        gain = gain_ref[...]
        diag, off, lower, _ = _chunk_masks()

        def chunk(c, carry):
            rows = pl.ds(pl.multiple_of(c * CHUNK, CHUNK), CHUNK)
            rq = blk_ref[rows, 0:gw]
            rf = blk_ref[rows, gw:2 * gw]
            v = blk_ref[rows, 2 * gw:3 * gw]
            rg = blk_ref[rows, 3 * gw:4 * gw]
            _, _, g, k, _, q = _rnn_gates(rq, rf, lb)
            dec = _rnn_decays(_tri_dot(lower, g))
            qd = (q * dec["eq_d"]).astype(BF16)
            kd = (k * dec["ek_d"]).astype(BF16)
            qo = (q * dec["eq_o"]).astype(BF16)
            ko = (k * dec["ek_o"]).astype(BF16)
            qe = (q * dec["eg"]).astype(BF16)
            kl = (k * dec["ekl"]).astype(BF16)
            vb = v.astype(BF16)
            outs = []
            for j in range(RNN_GROUP_HEADS):
                st = st_ref[j]
                st_out_ref[j, c] = st
                attn = jnp.where(diag, _dot_nt(_head(qd, j), _head(kd, j)),
                                 jnp.where(off, _dot_nt(_head(qo, j), _head(ko, j)), 0.0))
                o = _dot(attn.astype(BF16), _head(vb, j)) + _dot_nt(_head(qe, j), st.astype(BF16))
                st_ref[j] = st * _head(dec["e_last"], j) + _dot_tn(_head(vb, j), _head(kl, j))
                rr = lax.rsqrt(jnp.mean(o * o, axis=-1, keepdims=True) + NORM_EPS)
                o_ref[rows, j * RNN_HEAD_DIM:(j + 1) * RNN_HEAD_DIM] = o
                outs.append(o * rr)
            on = jnp.concatenate(outs, axis=1) * gain
            mix_ref[rows, :] = (on * (rg * _sigmoid(rg))).astype(BF16)
            return carry

        lax.fori_loop(0, nch, chunk, 0, unroll=True)

    return pl.pallas_call(
        body, name="rnn_fwd", grid=(n_groups, ntb),
        in_specs=[pl.BlockSpec((tb, 4 * gw), lambda h, i: (i, h)),
                  pl.BlockSpec((2, gw), lambda h, i: (0, h)),
                  pl.BlockSpec((1, gw), lambda h, i: (0, h)),
                  pl.BlockSpec(memory_space=pl.ANY)],
        out_specs=[pl.BlockSpec((tb, gw), lambda h, i: (i, d // gw + h)),
                   pl.BlockSpec((tb, gw), lambda h, i: (i, h)),
                   pl.BlockSpec((RNN_GROUP_HEADS, nch, RNN_HEAD_DIM, RNN_HEAD_DIM), lambda h, i: (h, i, 0, 0))],
        out_shape=[jax.ShapeDtypeStruct(mixed.shape, BF16), jax.ShapeDtypeStruct((t, d), F32),
                   jax.ShapeDtypeStruct((n_heads, t // CHUNK, RNN_HEAD_DIM, RNN_HEAD_DIM), F32)],
        scratch_shapes=[pltpu.VMEM((RNN_GROUP_HEADS, RNN_HEAD_DIM, RNN_HEAD_DIM), F32)],
        input_output_aliases={3: 0},
        compiler_params=_params(dimension_semantics=("parallel", "arbitrary")),
    )(proj, lb_logits, rnn_gain, mixed)


def _rnn_bwd(proj, lb_logits, rnn_gain, o_pre, states, dmixed, d_total, d, rider=None):
    t = proj.shape[0]
    tb = min(t, 256)
    gw, ntb, nch = _rnn_specs(t, tb, d)
    n_groups = d // gw
    n_rin = 0 if rider is None else len(rider.operands)
    n_rout = 0 if rider is None else len(rider.out_shapes)

    def body(*refs):
        blk_ref, lbl_ref, gain_ref, o_ref, st_in_ref, dm_ref = refs[:6]
        rin = refs[6:6 + n_rin]
        dproj_ref, dgain_ref, dlb_ref = refs[6 + n_rin:9 + n_rin]
        rout = refs[9 + n_rin:9 + n_rin + n_rout]
        dst_ref = refs[9 + n_rin + n_rout]
        if rider is not None:
            rider.emit(pl.program_id(0) * ntb + pl.program_id(1), n_groups * ntb, rin, rout, refs[-2], refs[-1])

        @pl.when(pl.program_id(1) == 0)
        def _():
            dst_ref[...] = jnp.zeros_like(dst_ref)
            dgain_ref[...] = jnp.zeros_like(dgain_ref)
            dlb_ref[...] = jnp.zeros_like(dlb_ref)

        lb = _lower_bound(lbl_ref)
        gain = gain_ref[...]
        diag, off, lower, upper = _chunk_masks()
        last_row = lax.broadcasted_iota(jnp.int32, (CHUNK, RNN_HEAD_DIM), 0) == CHUNK - 1

        def chunk(step, carry):
            c = nch - 1 - step
            rows = pl.ds(pl.multiple_of(c * CHUNK, CHUNK), CHUNK)
            rq = blk_ref[rows, 0:gw]
            rf = blk_ref[rows, gw:2 * gw]
            v = blk_ref[rows, 2 * gw:3 * gw]
            rg = blk_ref[rows, 3 * gw:4 * gw]
            sf, f, g, k, sq, q = _rnn_gates(rq, rf, lb)
            dec = _rnn_decays(_tri_dot(lower, g))
            qd = (q * dec["eq_d"]).astype(BF16)
            kd = (k * dec["ek_d"]).astype(BF16)
            qo = (q * dec["eq_o"]).astype(BF16)
            ko = (k * dec["ek_o"]).astype(BF16)
            qe = (q * dec["eg"]).astype(BF16)
            kl = (k * dec["ekl"]).astype(BF16)
            vb = v.astype(BF16)

            o = o_ref[rows, :]
            dmix = dm_ref[rows, :]
            sg = _sigmoid(rg)
            n_parts = []
            for j in range(RNN_GROUP_HEADS):
                oj = _head(o, j)
                n_parts.append(oj * lax.rsqrt(jnp.mean(oj * oj, axis=-1, keepdims=True) + NORM_EPS))
            nrm = jnp.concatenate(n_parts, axis=1)
            d_on = dmix * (rg * sg)
            d_rg = dmix * (nrm * gain) * (sg * (1.0 + rg * (1.0 - sg)))
            dgain_ref[...] += jnp.sum(d_on * nrm, axis=0, keepdims=True)
            dn = d_on * gain

            dq_parts, dk_parts, dv_parts, dg_parts = [], [], [], []
            for j in range(RNN_GROUP_HEADS):
                oj, nj, dnj = _head(o, j), _head(nrm, j), _head(dn, j)
                rr = lax.rsqrt(jnp.mean(oj * oj, axis=-1, keepdims=True) + NORM_EPS)
                do = (rr * (dnj - nj * jnp.mean(dnj * nj, axis=-1, keepdims=True))).astype(BF16)
                st = st_in_ref[j, c]
                dst = dst_ref[j]
                stb, dstb = st.astype(BF16), dst.astype(BF16)
                qdj, kdj, qoj, koj = _head(qd, j), _head(kd, j), _head(qo, j), _head(ko, j)
                attn = jnp.where(diag, _dot_nt(qdj, kdj), jnp.where(off, _dot_nt(qoj, koj), 0.0))
                dattn = _dot_nt(do, _head(vb, j))
                da_d = jnp.where(diag, dattn, 0.0).astype(BF16)
                da_o = jnp.where(off, dattn, 0.0).astype(BF16)
                dv = _dot_tn(attn.astype(BF16), do) + _dot_nt(_head(kl, j), dstb)
                dq_inter = _dot(do, stb) * _head(dec["eg"], j)
                dq_d, dq_o = _dot(da_d, kdj), _dot(da_o, koj)
                dq = dq_inter + dq_d * _head(dec["eq_d"], j) + dq_o * _head(dec["eq_o"], j)
                dk_inter = _dot(_head(vb, j), dstb) * _head(dec["ekl"], j)
                dk_d, dk_o = _dot_tn(da_d, qdj), _dot_tn(da_o, qoj)
                dk = dk_inter + dk_d * _head(dec["ek_d"], j) + dk_o * _head(dec["ek_o"], j)
                kj, qj = _head(k, j), _head(q, j)
                e_last = _head(dec["e_last"], j)
                extra = (jnp.sum(kj * dk_inter, axis=0, keepdims=True)
                         + e_last * jnp.sum(st * dst, axis=0, keepdims=True))
                dg_cum = (qj * dq_inter - kj * dk_inter
                          + (qdj.astype(F32) * dq_d + qoj.astype(F32) * dq_o)
                          - (kdj.astype(F32) * dk_d + koj.astype(F32) * dk_o))
                dg_parts.append(jnp.where(last_row, dg_cum + extra, dg_cum))
                dst_ref[j] = dst * e_last + _dot_tn(do, _head(qe, j))
                dq_parts.append(dq)
                dk_parts.append(dk)
                dv_parts.append(dv)

            dq = jnp.concatenate(dq_parts, axis=1)
            dk = jnp.concatenate(dk_parts, axis=1)
            dg = _tri_dot(upper, jnp.concatenate(dg_parts, axis=1))
            df = dg / f - dk
            dlb_ref[...] += jnp.sum(df * (1.0 - sf), axis=0, keepdims=True)
            d_rf = df * (1.0 - lb) * (sf * (1.0 - sf))
            d_rq = dq * (sq * (1.0 + rq * (1.0 - sq)))
            dproj_ref[rows, 0:gw] = d_rq.astype(BF16)
            dproj_ref[rows, gw:2 * gw] = d_rf.astype(BF16)
            dproj_ref[rows, 2 * gw:3 * gw] = jnp.concatenate(dv_parts, axis=1).astype(BF16)
            dproj_ref[rows, 3 * gw:4 * gw] = d_rg.astype(BF16)
            return carry

        lax.fori_loop(0, nch, chunk, 0, unroll=True)

    rev = lambda i: ntb - 1 - i
    vec = pl.BlockSpec((1, gw), lambda h, i: (0, h))
    scratch = [pltpu.VMEM((RNN_GROUP_HEADS, RNN_HEAD_DIM, RNN_HEAD_DIM), F32)]
    out_shapes = [jax.ShapeDtypeStruct((t, d_total), BF16), jax.ShapeDtypeStruct((1, d), F32),
                  jax.ShapeDtypeStruct((1, d), F32)]
    operands = [proj, lb_logits, rnn_gain, o_pre, states, dmixed]
    sem = ("parallel", "arbitrary")
    if rider is not None:
        scratch += rider.scratch()
        out_shapes += list(rider.out_shapes)
        operands += list(rider.operands)
        sem = ("arbitrary", "arbitrary")
    return pl.pallas_call(
        body, name="rnn_bwd", grid=(n_groups, ntb),
        in_specs=[pl.BlockSpec((tb, 4 * gw), lambda h, i: (rev(i), h)),
                  pl.BlockSpec((2, gw), lambda h, i: (0, h)), vec,
                  pl.BlockSpec((tb, gw), lambda h, i: (rev(i), h)),
                  pl.BlockSpec((RNN_GROUP_HEADS, nch, RNN_HEAD_DIM, RNN_HEAD_DIM), lambda h, i: (h, rev(i), 0, 0)),
                  pl.BlockSpec((tb, gw), lambda h, i: (rev(i), d // gw + h))] + [HBM_SPEC] * n_rin,
        out_specs=[pl.BlockSpec((tb, 4 * gw), lambda h, i: (rev(i), h)), vec, vec] + [HBM_SPEC] * n_rout,
        out_shape=out_shapes, scratch_shapes=scratch,
        compiler_params=_params(dimension_semantics=sem),
    )(*operands)


def _local_grads(x, target, w_in_full, w_out, sinks, lb_logits, rnn_gain, pre_gain, post_gain, sc=None):
    t, d = x.shape
    comm = sc is not None
    lay = _layout(d)
    perm = lay["perm"]
    h, ht = _prenorm_fwd(x, pre_gain)
    if comm:
        proj, w_in_full = _proj_gather_mm(h, w_in_full, perm, sc)
        mixed, attn_o, w_out_full = _attn_fwd(proj, sinks, lay, d, _gather_rider(w_out, 0, 0.8))
    else:
        (proj,) = _proj_mm(h, w_in_full, perm)
        w_out_full = w_out
        mixed, attn_o = _attn_fwd(proj, sinks, lay, d)
    mixed, o_pre, states = _rnn_fwd(proj, lb_logits, rnn_gain, mixed, d)
    y = _out_mm(mixed, w_out_full)
    dy, dz, g_post, loss = _post_loss(x, y, target, post_gain)
    dmixed = _dmixed_mm(dy, w_out_full)
    gw_out = _gw_out_mm(_transpose_bf16(mixed, "mixed_t"), dy)
    rider = None
    if comm:
        p_out = _pair_sum_out(gw_out, _pair_exchange(gw_out, "pair_exchange_out"))
        rider = _chip_exchange_rider(p_out, lambda ref, chip: ref.at[chip])
    dproj, g_rnn, g_lb, *r_out = _rnn_bwd(proj, lb_logits, rnn_gain, o_pre, states, dmixed, lay["total"], d, rider)
    dproj, dkc, dkp, dvc, dvp, dsink = _attn_bwd(proj, sinks, attn_o, dmixed, dproj, lay, d)
    dproj = _kv_combine(dkc, dkp, dvc, dvp, dproj, lay)
    if comm:
        c = lax.axis_index("c")
        (g_other,) = _gw_in_mm("gw_in_mm_other", ht, dproj, perm, 1 - c)
        g_mine, recv_in = _gw_in_mm("gw_in_mm_mine", ht, dproj, perm, c, _pair_exchange_rider(g_other))
        p_in = _pair_sum_in(g_mine, recv_in, sc)
        rider = _chip_exchange_rider(
            p_in, lambda ref, chip: ref.at[:, pl.ds(pl.multiple_of(chip * sc, LANES), sc)])
    dh, *r_in = _dh_mm(dproj, w_in_full, perm, rider)
    grad_x, g_pre = _prenorm_bwd(x, dh, dz, pre_gain)
    heads_per_group = ATTN_GROUP_LANES // ATTN_HEAD_DIM
    g_sink = dsink.reshape(d // ATTN_GROUP_LANES, 8, LANES)[:, 0, :heads_per_group].reshape(1, -1)
    small = dict(sink=g_sink, lb=g_lb, rnn=g_rnn, pre=g_pre, post=g_post)
    if comm:
        return loss, grad_x, (p_in, r_in[0]), (p_out, r_out[0]), small
    gw_in = jnp.stack([_gw_in_mm("gw_in_mm_%d" % half, ht, dproj, perm, half)[0] for half in range(2)])
    return loss, grad_x, gw_in, gw_out, small


def _mesh_pos():
    x, y, c = lax.axis_index("x"), lax.axis_index("y"), lax.axis_index("c")
    chips = [(1 - x, y), (x, 1 - y), (1 - x, 1 - y)]
    return x, y, c, chips


def _remote(src, dst, send_sem, recv_sem, device):
    return pltpu.make_async_remote_copy(src_ref=src, dst_ref=dst, send_sem=send_sem, recv_sem=recv_sem,
                                        device_id=device, device_id_type=MESH)


HBM_SPEC = pl.BlockSpec(memory_space=pl.ANY)


def _gather_rider(part, axis, forward_at):
    rows = part.shape[0] if axis == 1 else part.shape[0] // N_CHIPS
    cols = part.shape[1] // N_CHIPS if axis == 1 else part.shape[1]
    half_rows = rows // 2

    def stages(ins, outs, send_sems, recv_sems):
        del ins
        full = outs[0]

        def piece(chip, half):
            if axis == 1:
                return full.at[pl.ds(half * half_rows, half_rows), pl.ds(pl.multiple_of(chip * cols, LANES), cols)]
            return full.at[pl.ds(pl.multiple_of(chip * rows + half * half_rows, 8), half_rows), :]

        def sends():
            x, y, c, chips = _mesh_pos()
            mine = piece(2 * x + y, c)
            return [_remote(mine, mine, send_sems.at[j], recv_sems.at[j], (px, py, c))
                    for j, (px, py) in enumerate(chips)]

        def forwards(half_of):
            x, y, c, chips = _mesh_pos()
            out = []
            for j, (px, py) in enumerate(chips):
                block = piece(2 * px + py, half_of(c))
                out.append(_remote(block, block, send_sems.at[3 + j], recv_sems.at[3 + j], (x, y, 1 - c)))
            return out

        def start():
            for cp in sends():
                cp.start()

        def forward():
            x, y, c, chips = _mesh_pos()
            for j, (px, py) in enumerate(chips):
                landed = piece(2 * px + py, c)
                _remote(landed, landed, send_sems.at[j], recv_sems.at[j], (x, y, 1 - c)).wait_recv()
            for cp in forwards(lambda c: c):
                cp.start()

        def finish():
            for cp in forwards(lambda c: 1 - c):
                cp.wait_recv()
            for cp in sends() + forwards(lambda c: c):
                cp.wait_send()

        return [(0.0, start), (forward_at, forward), (1.0, finish)]

    return _Rider((part,), (jax.ShapeDtypeStruct(part.shape, BF16),), {0: 0}, 6, stages)


def _chip_exchange_rider(partial, piece):
    if partial.ndim == 3:
        recv_shape = (N_CHIPS - 1,) + partial.shape[1:]
    else:
        recv_shape = (N_CHIPS - 1, partial.shape[0], partial.shape[1] // N_CHIPS)

    def stages(ins, outs, send_sems, recv_sems):
        def copies():
            x, y, c, chips = _mesh_pos()
            return [_remote(piece(ins[0], 2 * px + py), outs[0].at[j], send_sems.at[j], recv_sems.at[j], (px, py, c))
                    for j, (px, py) in enumerate(chips)]

        def start():
            for cp in copies():
                cp.start()

        def finish():
            for cp in copies():
                cp.wait()

        return [(0.0, start), (1.0, finish)]

    return _Rider((partial,), (jax.ShapeDtypeStruct(recv_shape, BF16),), {}, N_CHIPS - 1, stages)


def _pair_exchange(g, name):
    def body(g_ref, r_ref, send_sems, recv_sems):
        x, y, c, _ = _mesh_pos()
        cp = _remote(g_ref.at[1 - c], r_ref, send_sems.at[0], recv_sems.at[0], (x, y, 1 - c))
        cp.start()
        cp.wait()

    return pl.pallas_call(
        body, name=name, in_specs=[HBM_SPEC], out_specs=HBM_SPEC,
        out_shape=jax.ShapeDtypeStruct(g.shape[1:], BF16),
        scratch_shapes=[pltpu.SemaphoreType.DMA((1,)), pltpu.SemaphoreType.DMA((1,))],
    )(g)


def _pair_exchange_rider(g_other):
    def stages(ins, outs, send_sems, recv_sems):
        def copy():
            x, y, c, _ = _mesh_pos()
            return _remote(ins[0], outs[0], send_sems.at[0], recv_sems.at[0], (x, y, 1 - c))

        return [(0.0, lambda: copy().start()), (1.0, lambda: copy().wait())]

    return _Rider((g_other,), (jax.ShapeDtypeStruct(g_other.shape, BF16),), {}, 1, stages)


def _pair_sum_in(mine, recv, sc):
    hd, d_in = mine.shape
    tr = min(hd, 256)

    def body(a_ref, b_ref, o_ref):
        o_ref[...] = (a_ref[...].astype(F32) + b_ref[...].astype(F32)).astype(BF16)

    blk = pl.BlockSpec((tr, sc), lambda i, j: (i, j))
    return pl.pallas_call(
        body, name="pair_sum_in", grid=(hd // tr, d_in // sc), in_specs=[blk, blk], out_specs=blk,
        out_shape=jax.ShapeDtypeStruct((hd, d_in), BF16),
        compiler_params=_params(dimension_semantics=("parallel", "parallel")),
    )(mine, recv)


def _pair_sum_out(gw_out, recv):
    _, n_chips, hr, d = gw_out.shape
    tr = min(hr, 256)
    c = lax.axis_index("c")

    def body(c_ref, a_ref, b_ref, o_ref):
        del c_ref
        o_ref[...] = (a_ref[...].astype(F32) + b_ref[...].astype(F32)).astype(BF16)

    blk = pl.BlockSpec((None, tr, d), lambda k, i, cc: (k, i, 0))
    gs = pltpu.PrefetchScalarGridSpec(
        num_scalar_prefetch=1, grid=(n_chips, hr // tr),
        in_specs=[pl.BlockSpec((None, None, tr, d), lambda k, i, cc: (cc[0], k, i, 0)), blk], out_specs=blk)
    return pl.pallas_call(
        body, name="pair_sum_out", grid_spec=gs, out_shape=jax.ShapeDtypeStruct((n_chips, hr, d), BF16),
        compiler_params=_params(dimension_semantics=("parallel", "parallel")),
    )(jnp.reshape(c, (1,)).astype(jnp.int32), gw_out, recv)


def _place():
    return jnp.stack([2 * lax.axis_index("x") + lax.axis_index("y"), lax.axis_index("c")]).astype(jnp.int32)


def _chip_sum_in(p_in, r_in, sc):
    hd = p_in.shape[0]
    tr = min(hd, 256)
    nblk = hd // tr

    def body(pos_ref, p_ref, r_ref, o_ref):
        del pos_ref
        acc = p_ref[...].astype(F32)
        for j in range(3):
            acc = acc + r_ref[j].astype(F32)
        o_ref[...] = acc

    gs = pltpu.PrefetchScalarGridSpec(
        num_scalar_prefetch=1, grid=(nblk,),
        in_specs=[pl.BlockSpec((tr, sc), lambda i, pos: (i, pos[0])), pl.BlockSpec((3, tr, sc), lambda i, pos: (0, i, 0))],
        out_specs=pl.BlockSpec((tr, sc), lambda i, pos: (pos[1] * nblk + i, 0)))
    return pl.pallas_call(
        body, name="chip_sum_in", grid_spec=gs, out_shape=jax.ShapeDtypeStruct((2 * hd, sc), F32),
        compiler_params=_params(dimension_semantics=("parallel",)),
    )(_place(), p_in, r_in)


def _chip_sum_out(p_out, r_out):
    _, hr, d = p_out.shape
    tr = min(hr, 256)
    nblk = hr // tr

    def body(pos_ref, p_ref, r_ref, o_ref):
        del pos_ref
        acc = p_ref[...].astype(F32)
        for j in range(3):
            acc = acc + r_ref[j].astype(F32)
        o_ref[...] = acc

    gs = pltpu.PrefetchScalarGridSpec(
        num_scalar_prefetch=1, grid=(nblk,),
        in_specs=[pl.BlockSpec((None, tr, d), lambda i, pos: (pos[0], i, 0)), pl.BlockSpec((3, tr, d), lambda i, pos: (0, i, 0))],
        out_specs=pl.BlockSpec((tr, d), lambda i, pos: (pos[1] * nblk + i, 0)))
    return pl.pallas_call(
        body, name="chip_sum_out", grid_spec=gs, out_shape=jax.ShapeDtypeStruct((2 * hr, d), F32),
        compiler_params=_params(dimension_semantics=("parallel",)),
    )(_place(), p_out, r_out)


def _share_halves(g_in, g_out):
    hd = g_in.shape[0] // 2
    hr = g_out.shape[0] // 2

    def body(gi_in, go_in, gi_ref, go_ref, send_sems, recv_sems):
        del gi_in, go_in
        x, y, c, _ = _mesh_pos()
        sibling = (x, y, 1 - c)
        mine_i = gi_ref.at[pl.ds(pl.multiple_of(c * hd, 8), hd), :]
        mine_o = go_ref.at[pl.ds(pl.multiple_of(c * hr, 8), hr), :]
        a = _remote(mine_i, mine_i, send_sems.at[0], recv_sems.at[0], sibling)
        b = _remote(mine_o, mine_o, send_sems.at[1], recv_sems.at[1], sibling)
        a.start()
        b.start()
        a.wait_send()
        b.wait_send()
        theirs_i = gi_ref.at[pl.ds(pl.multiple_of((1 - c) * hd, 8), hd), :]
        theirs_o = go_ref.at[pl.ds(pl.multiple_of((1 - c) * hr, 8), hr), :]
        _remote(theirs_i, theirs_i, send_sems.at[0], recv_sems.at[0], sibling).wait_recv()
        _remote(theirs_o, theirs_o, send_sems.at[1], recv_sems.at[1], sibling).wait_recv()

    return pl.pallas_call(
        body, name="share_halves",
        in_specs=[HBM_SPEC, HBM_SPEC], out_specs=[HBM_SPEC, HBM_SPEC],
        out_shape=[jax.ShapeDtypeStruct(g_in.shape, F32), jax.ShapeDtypeStruct(g_out.shape, F32)],
        input_output_aliases={0: 0, 1: 1},
        scratch_shapes=[pltpu.SemaphoreType.DMA((2,)), pltpu.SemaphoreType.DMA((2,))],
    )(g_in, g_out)


def _adamw_math(w, g, m, v):
    m_new = ADAM_B1 * m + (1.0 - ADAM_B1) * g
    v_new = ADAM_B2 * v + (1.0 - ADAM_B2) * (g * g)
    m_hat = m_new / (1.0 - ADAM_B1 ** ADAM_STEP)
    v_hat = v_new / (1.0 - ADAM_B2 ** ADAM_STEP)
    delta = -ADAM_LR * (m_hat / (jnp.sqrt(v_hat) + ADAM_EPS) + ADAM_WD * w)
    return delta, m_new, v_new


def _adamw(w, g, m, v, name):
    rows, cols = w.shape
    tr = min(rows, 128)

    def body(w_ref, g_ref, m_ref, v_ref, d_ref, mo_ref, vo_ref):
        delta, m_new, v_new = _adamw_math(w_ref[...], g_ref[...], m_ref[...], v_ref[...])
        d_ref[...] = delta
        mo_ref[...] = m_new
        vo_ref[...] = v_new

    spec = pl.BlockSpec((tr, cols), lambda i: (i, 0))
    shape = jax.ShapeDtypeStruct((rows, cols), F32)
    return pl.pallas_call(
        body, name=name, grid=(rows // tr,), in_specs=[spec] * 4, out_specs=[spec] * 3,
        out_shape=[shape] * 3, compiler_params=_params(dimension_semantics=("parallel",)),
    )(w, g, m, v)


SMALL_ROWS = 8


def _small_allreduce_adamw(part, w_pack, m_pack, v_pack):
    d = part.shape[1]

    def body(part_ref, w_ref, m_ref, v_ref, g_ref, d_ref, mo_ref, vo_ref, buf_ref, send_sems, recv_sems):
        x, y, c, _ = _mesh_pos()
        me = 4 * x + 2 * y + c
        buf_ref[0] = part_ref[...]
        copies = []
        for r in range(1, 8):
            rx, ry, rc = (r >> 2) & 1, (r >> 1) & 1, r & 1
            peer = (x ^ rx, y ^ ry, c ^ rc)
            copies.append(_remote(buf_ref.at[0], buf_ref.at[r], send_sems.at[r - 1], recv_sems.at[r - 1], peer))
        for cp in copies:
            cp.start()
        for cp in copies:
            cp.wait()
        total = buf_ref[me]
        for s in range(1, 8):
            total = total + buf_ref[s ^ me]
        w = w_ref[...]
        row = lax.broadcasted_iota(jnp.int32, (SMALL_ROWS, d), 0)
        l0, l1 = w[3:4], w[4:5]
        mx = jnp.maximum(l0, l1)
        e0, e1 = jnp.exp(l0 - mx), jnp.exp(l1 - mx)
        lb = e0 / (e0 + e1)
        g_l0 = total[3:4] * lb * (1.0 - lb)
        grads = jnp.where(row == 3, g_l0, jnp.where(row == 4, -g_l0, total))
        g_ref[...] = grads
        delta, m_new, v_new = _adamw_math(w, grads, m_ref[...], v_ref[...])
        d_ref[...] = delta
        mo_ref[...] = m_new
        vo_ref[...] = v_new

    vm = pl.BlockSpec(memory_space=pltpu.VMEM)
    shape = jax.ShapeDtypeStruct((SMALL_ROWS, d), F32)
    return pl.pallas_call(
        body, name="small_allreduce_adamw",
        in_specs=[vm] * 4, out_specs=[vm] * 4, out_shape=[shape] * 4,
        scratch_shapes=[pltpu.VMEM((8, SMALL_ROWS, d), F32), pltpu.SemaphoreType.DMA((7,)), pltpu.SemaphoreType.DMA((7,))],
    )(part, w_pack, m_pack, v_pack)


def _pack_small(d, pre, post, rnn, lb, sink, extra=None):
    rows = [pre, post, rnn, lb[0:1], lb[1:2],
            jnp.pad(sink, ((0, 0), (0, d - sink.shape[1]))),
            jnp.zeros((1, d), F32) if extra is None else extra,
            jnp.zeros((1, d), F32)]
    return jnp.concatenate(rows, axis=0)


def _unpack_small(p, n_sink):
    return dict(pre=p[0:1], post=p[1:2], rnn=p[2:3], lb=p[3:5], sink=p[5:6, :n_sink])


def kernel(x, w_in, attn_sinks, lb_logits, rnn_norm, w_out, pre_norm, post_norm, loss_target, m_w_in, m_attn_sinks, m_lb_logits, m_rnn_norm, m_w_out, m_pre_norm, m_post_norm, v_w_in, v_attn_sinks, v_lb_logits, v_rnn_norm, v_w_out, v_pre_norm, v_post_norm):
    t, d = x.shape[1], x.shape[2]
    sc = w_in.shape[2]
    n_sink = attn_sinks.shape[1]
    w_in2, w_out2 = w_in[0], w_out[0]

    w_in_part = _cast_into_gathered(w_in2, "cast_w_in", 1)
    w_out_part = _cast_into_gathered(w_out2, "cast_w_out", 0)
    loss_part, grad_x, (p_in, r_in), (p_out, r_out), small = _local_grads(
        x[0], loss_target[0], w_in_part, w_out_part, attn_sinks, lb_logits, rnn_norm, pre_norm, post_norm, sc)
    g_w_in, g_w_out = _share_halves(_chip_sum_in(p_in, r_in, sc), _chip_sum_out(p_out, r_out))

    d_w_in, nm_w_in, nv_w_in = _adamw(w_in2, g_w_in, m_w_in[0], v_w_in[0], "adamw_w_in")
    d_w_out, nm_w_out, nv_w_out = _adamw(w_out2, g_w_out, m_w_out[0], v_w_out[0], "adamw_w_out")

    lb_part = jnp.concatenate([small["lb"], jnp.zeros_like(small["lb"])], axis=0)
    loss_row = jnp.pad(loss_part[:, :1], ((0, 0), (0, d - 1)))
    part = _pack_small(d, small["pre"], small["post"], small["rnn"], lb_part, small["sink"], loss_row)
    w_pack = _pack_small(d, pre_norm, post_norm, rnn_norm, lb_logits, attn_sinks)
    m_pack = _pack_small(d, m_pre_norm, m_post_norm, m_rnn_norm, m_lb_logits, m_attn_sinks)
    v_pack = _pack_small(d, v_pre_norm, v_post_norm, v_rnn_norm, v_lb_logits, v_attn_sinks)
    g_pack, d_pack, nm_pack, nv_pack = _small_allreduce_adamw(part, w_pack, m_pack, v_pack)
    loss = g_pack[6, 0]
    g, dl, nm, nv = (_unpack_small(p, n_sink) for p in (g_pack, d_pack, nm_pack, nv_pack))

    def ordered(w_in_leaf, w_out_leaf, s):
        return (w_in_leaf[None], s["sink"], s["lb"], s["rnn"], w_out_leaf[None], s["pre"], s["post"])

    return (loss, grad_x[None],
            *ordered(g_w_in, g_w_out, g), *ordered(d_w_in, d_w_out, dl),
            *ordered(nm_w_in, nm_w_out, nm), *ordered(nv_w_in, nv_w_out, nv))
```

```python
import numpy as np
import jax
import jax.numpy as jnp
from jax import lax
from jax.experimental import pallas as pl
from jax.experimental.pallas import tpu as pltpu

F32 = jnp.float32
BF16 = jnp.bfloat16
MESH = pl.DeviceIdType.MESH

NORM_EPS = 1e-6
ATTN_HEAD_DIM = 64
GQA_GROUP = 8
WINDOW = 128
RNN_HEAD_DIM = 128
CHUNK = 64
HALF_CHUNK = CHUNK // 2
ATTN_SCALE = ATTN_HEAD_DIM ** -0.5

ADAM_LR = 0.001
ADAM_B1 = 0.9
ADAM_B2 = 0.999
ADAM_EPS = 1e-08
ADAM_WD = 0.01
ADAM_STEP = 10

LANES = 128
COL_TILE = 512
RNN_GROUP_HEADS = 4
ATTN_GROUP_LANES = 1024
N_CHIPS = 4
VMEM_LIMIT_BYTES = 56 * 1024 * 1024
NEG_BIG = -1e30


def _params(**kw):
    return pltpu.CompilerParams(vmem_limit_bytes=VMEM_LIMIT_BYTES, **kw)


def _sigmoid(x):
    return 1.0 / (1.0 + jnp.exp(-x))


def _dot(a, b):
    return jnp.dot(a, b, preferred_element_type=F32)


def _dot_nt(a, b):
    return lax.dot_general(a, b, (((1,), (1,)), ((), ())), preferred_element_type=F32)


def _dot_tn(a, b):
    return lax.dot_general(a, b, (((0,), (0,)), ((), ())), preferred_element_type=F32)


def _split3(x):
    hi = x.astype(BF16)
    r1 = x - hi.astype(F32)
    mid = r1.astype(BF16)
    lo = (r1 - mid.astype(F32)).astype(BF16)
    return hi, mid, lo


def _tri_dot(tri_bf16, x):
    hi, mid, lo = _split3(x)
    return _dot(tri_bf16, hi) + _dot(tri_bf16, mid) + _dot(tri_bf16, lo)


def _layout(d_model):
    d = d_model
    dkv = d // GQA_GROUP
    orig = dict(aq=0, ak=d, av=d + dkv, ag=d + 2 * dkv)
    base = d + 2 * dkv + d
    orig.update(rq=base, rf=base + d, ri=base + 2 * d, rg=base + 3 * d)
    group_w = RNN_GROUP_HEADS * RNN_HEAD_DIM
    cols = []
    for hg in range(d // group_w):
        for seg in ("rq", "rf", "ri", "rg"):
            cols.append((orig[seg] + hg * group_w, group_w))
    for m in range(d // ATTN_GROUP_LANES):
        for seg in ("aq", "ag"):
            cols.append((orig[seg] + m * ATTN_GROUP_LANES, ATTN_GROUP_LANES))
    cols.append((orig["ak"], dkv))
    cols.append((orig["av"], dkv))
    units = []
    for start, width in cols:
        assert start % LANES == 0 and width % LANES == 0
        units += [start + u for u in range(0, width, LANES)]
    per = COL_TILE // LANES
    assert len(units) % per == 0
    tiles = []
    for t in range(len(units) // per):
        run = units[t * per:(t + 1) * per]
        assert run[0] % COL_TILE == 0 and all(run[i] == run[0] + i * LANES for i in range(per))
        tiles.append(run[0] // COL_TILE)
    return dict(a_off=4 * d, k_off=6 * d, v_off=6 * d + dkv, total=6 * d + 2 * dkv,
                perm=np.asarray(tiles, np.int32))


def _chip_index():
    return jnp.reshape(2 * lax.axis_index("x") + lax.axis_index("y"), (1,)).astype(jnp.int32)


def _cast_into_gathered(a, name, axis):
    rows, cols = a.shape
    tr = min(rows, 512)
    nblk = rows // tr

    def body(me_ref, a_ref, o_ref):
        del me_ref
        o_ref[...] = a_ref[...].astype(BF16)

    if axis == 1:
        out_spec = pl.BlockSpec((tr, cols), lambda i, me: (i, me[0]))
        shape = (rows, N_CHIPS * cols)
    else:
        out_spec = pl.BlockSpec((tr, cols), lambda i, me: (me[0] * nblk + i, 0))
        shape = (N_CHIPS * rows, cols)
    gs = pltpu.PrefetchScalarGridSpec(num_scalar_prefetch=1, grid=(nblk,),
                                      in_specs=[pl.BlockSpec((tr, cols), lambda i, me: (i, 0))], out_specs=out_spec)
    return pl.pallas_call(
        body, name=name, grid_spec=gs, out_shape=jax.ShapeDtypeStruct(shape, BF16),
        compiler_params=_params(dimension_semantics=("parallel",)),
    )(_chip_index(), a)


def _transpose_bf16(a, name):
    rows, cols = a.shape
    tr = min(rows, 256)
    tc = min(cols, 2048)

    def body(a_ref, o_ref):
        o_ref[...] = a_ref[...].astype(F32).T.astype(BF16)

    return pl.pallas_call(
        body, name=name, grid=(rows // tr, cols // tc),
        in_specs=[pl.BlockSpec((tr, tc), lambda i, j: (i, j))],
        out_specs=pl.BlockSpec((tc, tr), lambda i, j: (j, i)),
        out_shape=jax.ShapeDtypeStruct((cols, rows), BF16),
        compiler_params=_params(dimension_semantics=("parallel", "parallel")),
    )(a)


def _prenorm_fwd(x, gain):
    t, d = x.shape
    tm = min(t, 256)

    def body(x_ref, g_ref, h_ref, ht_ref):
        xv = x_ref[...]
        r = lax.rsqrt(jnp.mean(xv * xv, axis=-1, keepdims=True) + NORM_EPS)
        h = (xv * r) * g_ref[...]
        h_ref[...] = h.astype(BF16)
        ht_ref[...] = h.T.astype(BF16)

    return pl.pallas_call(
        body, name="prenorm_fwd", grid=(t // tm,),
        in_specs=[pl.BlockSpec((tm, d), lambda i: (i, 0)), pl.BlockSpec((1, d), lambda i: (0, 0))],
        out_specs=[pl.BlockSpec((tm, d), lambda i: (i, 0)), pl.BlockSpec((d, tm), lambda i: (0, i))],
        out_shape=[jax.ShapeDtypeStruct((t, d), BF16), jax.ShapeDtypeStruct((d, t), BF16)],
        compiler_params=_params(dimension_semantics=("parallel",)),
    )(x, gain)


def _post_loss(x, y, target, gain):
    t, d = x.shape
    tm = min(t, 256)
    inv_d = 1.0 / d

    def body(x_ref, y_ref, t_ref, g_ref, dy_ref, dz_ref, gp_ref, loss_ref):
        i = pl.program_id(0)
        yv = y_ref[...]
        gain_v = g_ref[...]
        r = lax.rsqrt(jnp.mean(yv * yv, axis=-1, keepdims=True) + NORM_EPS)
        n = yv * r
        e = (x_ref[...] + n * gain_v) - t_ref[...]
        dz = e * inv_d
        dn = dz * gain_v
        dy = r * (dn - n * jnp.mean(dn * n, axis=-1, keepdims=True))
        dy_ref[...] = dy.astype(BF16)
        dz_ref[...] = dz

        @pl.when(i == 0)
        def _():
            gp_ref[...] = jnp.zeros_like(gp_ref)
            loss_ref[...] = jnp.zeros_like(loss_ref)

        gp_ref[...] += jnp.sum(dz * n, axis=0, keepdims=True)
        row = jnp.sum(e * e, axis=-1, keepdims=True)
        loss_ref[...] += jnp.full(loss_ref.shape, 0.5 * inv_d * jnp.sum(row), F32)

    row_spec = pl.BlockSpec((tm, d), lambda i: (i, 0))
    vec_spec = pl.BlockSpec((1, d), lambda i: (0, 0))
    return pl.pallas_call(
        body, name="post_loss", grid=(t // tm,),
        in_specs=[row_spec, row_spec, row_spec, vec_spec],
        out_specs=[row_spec, row_spec, vec_spec, pl.BlockSpec((1, LANES), lambda i: (0, 0))],
        out_shape=[jax.ShapeDtypeStruct((t, d), BF16), jax.ShapeDtypeStruct((t, d), F32),
                   jax.ShapeDtypeStruct((1, d), F32), jax.ShapeDtypeStruct((1, LANES), F32)],
        compiler_params=_params(dimension_semantics=("arbitrary",)),
    )(x, y, target, gain)


def _prenorm_bwd(x, dh, dz, gain):
    t, d = x.shape
    tm = min(t, 256)

    def body(x_ref, dh_ref, dz_ref, g_ref, gx_ref, gp_ref):
        i = pl.program_id(0)
        xv = x_ref[...]
        r = lax.rsqrt(jnp.mean(xv * xv, axis=-1, keepdims=True) + NORM_EPS)
        n = xv * r
        dhv = dh_ref[...]
        dn = dhv * g_ref[...]
        gx_ref[...] = dz_ref[...] + r * (dn - n * jnp.mean(dn * n, axis=-1, keepdims=True))

        @pl.when(i == 0)
        def _():
            gp_ref[...] = jnp.zeros_like(gp_ref)

        gp_ref[...] += jnp.sum(dhv * n, axis=0, keepdims=True)

    row_spec = pl.BlockSpec((tm, d), lambda i: (i, 0))
    vec_spec = pl.BlockSpec((1, d), lambda i: (0, 0))
    return pl.pallas_call(
        body, name="prenorm_bwd", grid=(t // tm,),
        in_specs=[row_spec, row_spec, row_spec, vec_spec],
        out_specs=[row_spec, vec_spec],
        out_shape=[jax.ShapeDtypeStruct((t, d), F32), jax.ShapeDtypeStruct((1, d), F32)],
        compiler_params=_params(dimension_semantics=("arbitrary",)),
    )(x, dh, dz, gain)


class _Rider:
    def __init__(self, operands, out_shapes, aliases, n_sems, stages):
        self.operands = tuple(operands)
        self.out_shapes = tuple(out_shapes)
        self.aliases = dict(aliases)
        self.n_sems = n_sems
        self.stages = stages

    def scratch(self):
        return [pltpu.SemaphoreType.DMA((self.n_sems,)), pltpu.SemaphoreType.DMA((self.n_sems,))]

    def emit(self, step, n_steps, in_refs, out_refs, send_sems, recv_sems):
        for frac, fn in self.stages(in_refs, out_refs, send_sems, recv_sems):
            at = min(n_steps - 1, int(frac * (n_steps - 1) + 0.5))
            pl.when(step == at)(fn)


def _matmul(name, a, b, *, out_shape, grid, a_spec, b_spec, o_spec, nt=False, perm=None, rider=None):
    nk = grid[2]
    n_steps = grid[0] * grid[1] * grid[2]
    tm, tn = [s for s in o_spec.block_shape if s is not None][-2:]
    acc_in_out = out_shape.dtype == F32
    n_pre = 0 if perm is None else 1
    n_rin = 0 if rider is None else len(rider.operands)
    n_rout = 0 if rider is None else len(rider.out_shapes)
    use_acc = not (nk == 1 or acc_in_out)

    def body(*refs):
        refs = refs[n_pre:]
        a_ref, b_ref = refs[:2]
        rin = refs[2:2 + n_rin]
        o_ref = refs[2 + n_rin]
        rout = refs[3 + n_rin:3 + n_rin + n_rout]
        scratch_refs = refs[3 + n_rin + n_rout:]
        if rider is not None:
            step = (pl.program_id(0) * grid[1] + pl.program_id(1)) * grid[2] + pl.program_id(2)
            rider.emit(step, n_steps, rin, rout, scratch_refs[-2], scratch_refs[-1])
        def product():
            return _dot_nt(a_ref[...], b_ref[...]) if nt else _dot(a_ref[...], b_ref[...])

        if nk == 1:
            o_ref[...] = product().astype(o_ref.dtype)
            return
        acc_ref = o_ref if acc_in_out else scratch_refs[0]
        k = pl.program_id(2)

        @pl.when(k == 0)
        def _():
            acc_ref[...] = jnp.zeros_like(acc_ref)

        acc_ref[...] += product()

        if not acc_in_out:
            @pl.when(k == nk - 1)
            def _():
                o_ref[...] = acc_ref[...].astype(o_ref.dtype)

    scratch = [pltpu.VMEM((tm, tn), F32)] if use_acc else []
    in_specs = [a_spec, b_spec] + [HBM_SPEC] * n_rin
    out_specs = [o_spec] + [HBM_SPEC] * n_rout
    out_shapes = [out_shape]
    operands = [a, b]
    aliases = {}
    sem = ("parallel", "parallel", "arbitrary")
    if rider is not None:
        scratch += rider.scratch()
        out_shapes += list(rider.out_shapes)
        operands += list(rider.operands)
        aliases = {n_pre + 2 + i: 1 + o for i, o in rider.aliases.items()}
        sem = ("arbitrary", "arbitrary", "arbitrary")
    cp = _params(dimension_semantics=sem)
    if perm is None:
        return pl.pallas_call(body, name=name, grid=grid, in_specs=in_specs, out_specs=out_specs,
                              out_shape=out_shapes, scratch_shapes=scratch, input_output_aliases=aliases,
                              compiler_params=cp)(*operands)
    gs = pltpu.PrefetchScalarGridSpec(num_scalar_prefetch=1, grid=grid, in_specs=in_specs,
                                      out_specs=out_specs, scratch_shapes=scratch)
    return pl.pallas_call(body, name=name, grid_spec=gs, out_shape=out_shapes, input_output_aliases=aliases,
                          compiler_params=cp)(jnp.asarray(perm), *operands)


def _proj_mm(h, w_full, perm, rider=None):
    t, d = h.shape
    n_tiles = len(perm)
    tm = min(t, 1024)
    return _matmul(
        "proj_mm", h, w_full, perm=perm, rider=rider, grid=(t // tm, n_tiles, 1),
        out_shape=jax.ShapeDtypeStruct((t, n_tiles * COL_TILE), F32),
        a_spec=pl.BlockSpec((tm, d), lambda i, j, k, p: (i, 0)),
        b_spec=pl.BlockSpec((d, COL_TILE), lambda i, j, k, p: (0, p[j])),
        o_spec=pl.BlockSpec((tm, COL_TILE), lambda i, j, k, p: (i, j)))


def _proj_gather_mm(h, wi_part, perm, sc):
    t, d = h.shape
    n_tiles = len(perm)
    tm = min(t, 1024)
    n_i = t // tm
    hd = d // 2
    nf = sc // COL_TILE
    rem = sc - nf * COL_TILE
    assert 2 * rem == COL_TILE and n_tiles == N_CHIPS * nf + 2
    n_chunks = next(q for q in (4, 3, 2, 1) if nf % q == 0)
    tpc = nf // n_chunks
    n_kinds = n_chunks + 1
    n_sems = 3 * n_kinds
    rem_at = nf + 3 * n_chunks * tpc

    def first_full(chip):
        return (chip * sc + (rem if chip % 2 else 0)) // COL_TILE

    inverse = np.argsort(perm)
    table = np.zeros((N_CHIPS, 2, n_tiles), np.int32)
    for chip in range(N_CHIPS):
        seq = list(range(first_full(chip), first_full(chip) + nf))
        for q in range(n_chunks):
            for src in (chip ^ 2, chip ^ 1, chip ^ 3):
                seq += list(range(first_full(src) + q * tpc, first_full(src) + (q + 1) * tpc))
        seq += [first_full(chip - chip % 2) + nf, first_full((chip ^ 2) - chip % 2) + nf]
        assert sorted(seq) == list(range(n_tiles)), seq
        table[chip, 0] = inverse[seq]
        table[chip, 1] = seq
    me_chip = 2 * lax.axis_index("x") + lax.axis_index("y")
    tab = lax.dynamic_index_in_dim(jnp.asarray(table), me_chip, 0, keepdims=False)

    def body(tab_ref, h_hbm, wi_in, proj_ref, full, hbuf, bbuf, local_sems, send_sems, recv_sems):
        del wi_in
        jj = pl.program_id(0)
        i = pl.program_id(1)
        x, y, c, chips = _mesh_pos()
        sibling = (x, y, 1 - c)

        def piece(chip, half, kind):
            odd = chip % 2
            if kind == n_chunks:
                start, width = chip * sc + (1 - odd) * (nf * COL_TILE), rem
            else:
                start, width = chip * sc + odd * rem + kind * (tpc * COL_TILE), tpc * COL_TILE
            return full.at[pl.ds(half * hd, hd), pl.ds(pl.multiple_of(start, LANES), width)]

        def ici(j, kind):
            mine = piece(2 * x + y, c, kind)
            k = j * n_kinds + kind
            return _remote(mine, mine, send_sems.at[k], recv_sems.at[k], (chips[j][0], chips[j][1], c))

        def landed(j, kind):
            blk = piece(2 * chips[j][0] + chips[j][1], c, kind)
            k = j * n_kinds + kind
            return _remote(blk, blk, send_sems.at[k], recv_sems.at[k], sibling)

        def passed(j, kind, half):
            blk = piece(2 * chips[j][0] + chips[j][1], half, kind)
            k = n_sems + j * n_kinds + kind
            return _remote(blk, blk, send_sems.at[k], recv_sems.at[k], sibling)

        def fetch(pos, slot):
            col = pl.multiple_of(tab_ref[1, pos] * COL_TILE, LANES)
            return pltpu.make_async_copy(full.at[:, pl.ds(col, COL_TILE)], bbuf.at[slot], local_sems.at[slot])

        def load_h():
            return pltpu.make_async_copy(h_hbm, hbuf, local_sems.at[2])

        def relay(j, kind):
            landed(j, kind).wait_recv()
            passed(j, kind, c).start()

        @pl.when(i == 0)
        def _():
            @pl.when(jj == 0)
            def _():
                load_h().start()
                for kind in range(n_kinds):
                    for j in range(3):
                        ici(j, kind).start()
                fetch(0, 0).start()
                load_h().wait()

            for n in range(3 * n_chunks):
                at = nf + n * tpc
                pl.when(jj == at - 2)(lambda n=n: relay(n % 3, n // 3))
                pl.when(jj == at - 1)(lambda n=n: passed(n % 3, n // 3, 1 - c).wait_recv())

            @pl.when(jj == rem_at - 2)
            def _():
                for j in range(3):
                    relay(j, n_chunks)

            @pl.when(jj == rem_at - 1)
            def _():
                for j in range(3):
                    passed(j, n_chunks, 1 - c).wait_recv()

            @pl.when(jj + 1 < n_tiles)
            def _():
                fetch(jj + 1, (jj + 1) % 2).start()

            fetch(jj, jj % 2).wait()

            @pl.when(jj == n_tiles - 1)
            def _():
                for kind in range(n_kinds):
                    for j in range(3):
                        ici(j, kind).wait_send()
                        passed(j, kind, c).wait_send()

        rows = pl.ds(pl.multiple_of(i * tm, tm), tm)
        proj_ref[...] = _dot(hbuf[rows, :], bbuf[jj % 2])

    gs = pltpu.PrefetchScalarGridSpec(
        num_scalar_prefetch=1, grid=(n_tiles, n_i),
        in_specs=[HBM_SPEC, HBM_SPEC],
        out_specs=[pl.BlockSpec((tm, COL_TILE), lambda jj, i, tb: (i, tb[0, jj])), HBM_SPEC],
        scratch_shapes=[pltpu.VMEM((t, d), BF16), pltpu.VMEM((2, d, COL_TILE), BF16), pltpu.SemaphoreType.DMA((3,)),
                        pltpu.SemaphoreType.DMA((2 * n_sems,)), pltpu.SemaphoreType.DMA((2 * n_sems,))])
    return pl.pallas_call(
        body, name="proj_gather_mm", grid_spec=gs,
        out_shape=[jax.ShapeDtypeStruct((t, n_tiles * COL_TILE), F32), jax.ShapeDtypeStruct(wi_part.shape, BF16)],
        input_output_aliases={2: 1},
        compiler_params=_params(dimension_semantics=("arbitrary", "arbitrary")),
    )(tab, h, wi_part)


def _gw_in_mm(name, ht, dproj, perm, half, rider=None):
    d, t = ht.shape
    n_tiles = len(perm)
    hd = d // 2
    tm = min(hd, 1024)
    per_half = hd // tm
    table = jnp.concatenate([jnp.asarray(perm), jnp.reshape(half, (1,)).astype(jnp.int32)])
    return _matmul(
        name, ht, dproj, perm=table, rider=rider, grid=(per_half, n_tiles, 1),
        out_shape=jax.ShapeDtypeStruct((hd, n_tiles * COL_TILE), BF16),
        a_spec=pl.BlockSpec((tm, t), lambda i, j, k, p: (p[n_tiles] * per_half + i, 0)),
        b_spec=pl.BlockSpec((t, COL_TILE), lambda i, j, k, p: (0, j)),
        o_spec=pl.BlockSpec((tm, COL_TILE), lambda i, j, k, p: (i, p[j])))


def _dh_mm(dproj, w_full, perm, rider=None):
    t = dproj.shape[0]
    d = w_full.shape[0]
    n_tiles = len(perm)
    tm = min(t, 2048)
    tn = min(d, 2048)
    return _matmul(
        "dh_mm", dproj, w_full, perm=perm, rider=rider, nt=True, grid=(t // tm, d // tn, n_tiles),
        out_shape=jax.ShapeDtypeStruct((t, d), F32),
        a_spec=pl.BlockSpec((tm, COL_TILE), lambda i, j, k, p: (i, k)),
        b_spec=pl.BlockSpec((tn, COL_TILE), lambda i, j, k, p: (j, p[k])),
        o_spec=pl.BlockSpec((tm, tn), lambda i, j, k, p: (i, j)))


def _out_mm(mixed, w_out_full):
    t, dm = mixed.shape
    d = w_out_full.shape[1]
    tm = min(t, 1024)
    tn = min(d, 512)
    tk = min(dm, 4096)
    return _matmul(
        "out_mm", mixed, w_out_full, grid=(t // tm, d // tn, dm // tk),
        out_shape=jax.ShapeDtypeStruct((t, d), F32),
        a_spec=pl.BlockSpec((tm, tk), lambda i, j, k: (i, k)),
        b_spec=pl.BlockSpec((tk, tn), lambda i, j, k: (k, j)),
        o_spec=pl.BlockSpec((tm, tn), lambda i, j, k: (i, j)))[0]


def _dmixed_mm(dy, w_out_full, rider=None):
    t, d = dy.shape
    dm = w_out_full.shape[0]
    tm = min(t, 1024)
    tn = min(dm, 1024)
    return _matmul(
        "dmixed_mm", dy, w_out_full, nt=True, rider=rider, grid=(t // tm, dm // tn, 1),
        out_shape=jax.ShapeDtypeStruct((t, dm), F32),
        a_spec=pl.BlockSpec((tm, d), lambda i, j, k: (i, 0)),
        b_spec=pl.BlockSpec((tn, d), lambda i, j, k: (j, 0)),
        o_spec=pl.BlockSpec((tm, tn), lambda i, j, k: (i, j)))


def _gw_out_mm(mixed_t, dy):
    dm, t = mixed_t.shape
    d = dy.shape[1]
    hr = dm // (2 * N_CHIPS)
    tn = min(d, 1024)
    return _matmul(
        "gw_out_mm", mixed_t, dy, grid=(dm // hr, d // tn, 1),
        out_shape=jax.ShapeDtypeStruct((2, N_CHIPS, hr, d), BF16),
        a_spec=pl.BlockSpec((hr, t), lambda i, j, k: (i, 0)),
        b_spec=pl.BlockSpec((t, tn), lambda i, j, k: (0, j)),
        o_spec=pl.BlockSpec((None, None, hr, tn), lambda i, j, k: (i % 2, i // 2, 0, j)))[0]


def _lane_half():
    return lax.broadcasted_iota(jnp.int32, (WINDOW, LANES), 1) // ATTN_HEAD_DIM


def _dup_kv(tile, kh):
    return jnp.where(_lane_half() == kh, tile, pltpu.roll(tile, ATTN_HEAD_DIM, 1))


def _stack_heads(tiles, kh):
    half = _lane_half()
    pieces = []
    for g in range(GQA_GROUP):
        pieces.append(jnp.where(half == g % 2, tiles[4 * kh + g // 2], 0.0))
    return jnp.concatenate(pieces, axis=0)


def _unstack_heads(stacked):
    half = _lane_half()
    out = []
    for j in range(GQA_GROUP // 2):
        a = stacked[(2 * j) * WINDOW:(2 * j + 1) * WINDOW]
        b = stacked[(2 * j + 1) * WINDOW:(2 * j + 2) * WINDOW]
        out.append(jnp.where(half == 0, a, b))
    return out


def _attn_probs(qs, kcat, sink_col, n):
    rows = GQA_GROUP * WINDOW
    s = _dot_nt(qs, kcat)
    qi = lax.broadcasted_iota(jnp.int32, (rows, 2 * WINDOW), 0) % WINDOW
    kj = lax.broadcasted_iota(jnp.int32, (rows, 2 * WINDOW), 1)
    first_key = WINDOW * (1 - jnp.minimum(n, 1))
    valid = (kj > qi) & (kj <= qi + WINDOW) & (kj >= first_key)
    s = jnp.where(valid, s, NEG_BIG)
    mx = jnp.maximum(jnp.max(s, axis=-1, keepdims=True), sink_col)
    p = jnp.exp(s - mx)
    p_sink = jnp.exp(sink_col - mx)
    inv = 1.0 / (jnp.sum(p, axis=-1, keepdims=True) + p_sink)
    return p * inv, p_sink * inv


def _attn_operands(sink_ref, q_tiles, kp_ref, kc_ref, vp_ref, vc_ref, m, kh):
    qs = _stack_heads([qt * ATTN_SCALE for qt in q_tiles], kh).astype(BF16)
    kcat = jnp.concatenate([_dup_kv(kp_ref[...], kh), _dup_kv(kc_ref[...], kh)], axis=0).astype(BF16)
    vcat = jnp.concatenate([_dup_kv(vp_ref[...], kh), _dup_kv(vc_ref[...], kh)], axis=0).astype(BF16)
    heads_per_group = ATTN_GROUP_LANES // ATTN_HEAD_DIM
    sink_col = jnp.concatenate(
        [jnp.full((WINDOW, 1), sink_ref[0, m * heads_per_group + kh * GQA_GROUP + g], F32)
         for g in range(GQA_GROUP)], axis=0)
    return qs, kcat, vcat, sink_col


def _attn_specs(lay, d):
    a_blk = lay["a_off"] // (2 * ATTN_GROUP_LANES)
    k_blk = lay["k_off"] // LANES
    v_blk = lay["v_off"] // LANES
    qg = pl.BlockSpec((WINDOW, 2 * ATTN_GROUP_LANES), lambda m, n: (n, a_blk + m))
    kp = pl.BlockSpec((WINDOW, LANES), lambda m, n: (jnp.maximum(n - 1, 0), k_blk + m))
    kc = pl.BlockSpec((WINDOW, LANES), lambda m, n: (n, k_blk + m))
    vp = pl.BlockSpec((WINDOW, LANES), lambda m, n: (jnp.maximum(n - 1, 0), v_blk + m))
    vc = pl.BlockSpec((WINDOW, LANES), lambda m, n: (n, v_blk + m))
    return qg, kp, kc, vp, vc


def _attn_fwd(proj, sinks, lay, d, rider=None):
    t = proj.shape[0]
    n_groups = d // ATTN_GROUP_LANES
    n_blocks = t // WINDOW
    pairs = ATTN_GROUP_LANES // LANES
    n_rin = 0 if rider is None else len(rider.operands)
    n_rout = 0 if rider is None else len(rider.out_shapes)

    def body(*refs):
        sink_ref, qg_ref, kp_ref, kc_ref, vp_ref, vc_ref = refs[:6]
        rin = refs[6:6 + n_rin]
        mix_ref, o_ref = refs[6 + n_rin:8 + n_rin]
        rout = refs[8 + n_rin:8 + n_rin + n_rout]
        m = pl.program_id(0)
        n = pl.program_id(1)
        if rider is not None:
            rider.emit(m * n_blocks + n, n_groups * n_blocks, rin, rout, refs[-2], refs[-1])
        q_tiles = [qg_ref[:, p * LANES:(p + 1) * LANES] for p in range(pairs)]
        for kh in range(2):
            qs, kcat, vcat, sink_col = _attn_operands(sink_ref, q_tiles, kp_ref, kc_ref, vp_ref, vc_ref, m, kh)
            probs, _ = _attn_probs(qs, kcat, sink_col, n)
            out = _dot(probs.astype(BF16), vcat)
            for j, tile in enumerate(_unstack_heads(out)):
                p = 4 * kh + j
                lanes = slice(p * LANES, (p + 1) * LANES)
                gate = qg_ref[:, ATTN_GROUP_LANES + p * LANES:ATTN_GROUP_LANES + (p + 1) * LANES]
                o_ref[:, lanes] = tile
                mix_ref[:, lanes] = (tile * (gate * _sigmoid(gate))).astype(BF16)

    qg, kp, kc, vp, vc = _attn_specs(lay, d)
    out_blk = pl.BlockSpec((WINDOW, ATTN_GROUP_LANES), lambda m, n: (n, m))
    out_shapes = [jax.ShapeDtypeStruct((t, 2 * d), BF16), jax.ShapeDtypeStruct((t, d), F32)]
    operands = [sinks, proj, proj, proj, proj, proj]
    scratch, aliases, sem = [], {}, ("parallel", "parallel")
    if rider is not None:
        scratch = rider.scratch()
        out_shapes += list(rider.out_shapes)
        operands += list(rider.operands)
        aliases = {6 + i: 2 + o for i, o in rider.aliases.items()}
        sem = ("arbitrary", "arbitrary")
    return pl.pallas_call(
        body, name="attn_fwd", grid=(n_groups, n_blocks),
        in_specs=[pl.BlockSpec(memory_space=pltpu.SMEM), qg, kp, kc, vp, vc] + [HBM_SPEC] * n_rin,
        out_specs=[out_blk, out_blk] + [HBM_SPEC] * n_rout,
        out_shape=out_shapes, scratch_shapes=scratch, input_output_aliases=aliases,
        compiler_params=_params(dimension_semantics=sem),
    )(*operands)


def _attn_bwd(proj, sinks, attn_o, dmixed, dproj, lay, d):
    t = proj.shape[0]
    n_groups = d // ATTN_GROUP_LANES
    pairs = ATTN_GROUP_LANES // LANES
    kv_w = n_groups * LANES

    def body(sink_ref, qg_ref, kp_ref, kc_ref, vp_ref, vc_ref, o_ref, dm_ref, dproj_hbm,
             dqg_ref, dkc_ref, dkp_ref, dvc_ref, dvp_ref, dsink_ref):
        del dproj_hbm
        m = pl.program_id(0)
        n = pl.program_id(1)
        half = _lane_half()
        q_tiles = [qg_ref[:, p * LANES:(p + 1) * LANES] for p in range(pairs)]
        do_tiles, o_tiles = [], []
        for p in range(pairs):
            lanes = slice(p * LANES, (p + 1) * LANES)
            gate = qg_ref[:, ATTN_GROUP_LANES + p * LANES:ATTN_GROUP_LANES + (p + 1) * LANES]
            sg = _sigmoid(gate)
            dmix = dm_ref[:, lanes]
            ov = o_ref[:, lanes]
            dqg_ref[:, ATTN_GROUP_LANES + p * LANES:ATTN_GROUP_LANES + (p + 1) * LANES] = (
                dmix * ov * (sg * (1.0 + gate * (1.0 - sg)))).astype(BF16)
            do_tiles.append(dmix * (gate * sg))
            o_tiles.append(ov)

        sub = lax.broadcasted_iota(jnp.int32, (8, LANES), 0)
        lane = lax.broadcasted_iota(jnp.int32, (8, LANES), 1)
        dsink = jnp.zeros((8, LANES), F32)
        dk_cur = dk_prev = dv_cur = dv_prev = jnp.zeros((WINDOW, LANES), F32)
        for kh in range(2):
            qs, kcat, vcat, sink_col = _attn_operands(sink_ref, q_tiles, kp_ref, kc_ref, vp_ref, vc_ref, m, kh)
            probs, p_sink = _attn_probs(qs, kcat, sink_col, n)
            dos = _stack_heads(do_tiles, kh)
            delta = jnp.sum(dos * _stack_heads(o_tiles, kh), axis=-1, keepdims=True)
            dos = dos.astype(BF16)
            dp = _dot_nt(dos, vcat)
            ds = (probs * (dp - delta)).astype(BF16)
            dv = _dot_tn(probs.astype(BF16), dos)
            dv = dv + pltpu.roll(dv, ATTN_HEAD_DIM, 1)
            dk = _dot_tn(ds, qs)
            dk = dk + pltpu.roll(dk, ATTN_HEAD_DIM, 1)
            dq = _dot(ds, kcat)
            for j, tile in enumerate(_unstack_heads(dq)):
                p = 4 * kh + j
                dqg_ref[:, p * LANES:(p + 1) * LANES] = (tile * ATTN_SCALE).astype(BF16)
            dk_prev = jnp.where(half == kh, dk[:WINDOW], dk_prev)
            dk_cur = jnp.where(half == kh, dk[WINDOW:], dk_cur)
            dv_prev = jnp.where(half == kh, dv[:WINDOW], dv_prev)
            dv_cur = jnp.where(half == kh, dv[WINDOW:], dv_cur)
            sink_terms = p_sink * delta
            for g in range(GQA_GROUP):
                val = -jnp.sum(sink_terms[g * WINDOW:(g + 1) * WINDOW])
                dsink = dsink + jnp.where((sub == 0) & (lane == kh * GQA_GROUP + g), val, 0.0)
        dkc_ref[...] = dk_cur
        dkp_ref[...] = dk_prev
        dvc_ref[...] = dv_cur
        dvp_ref[...] = dv_prev

        @pl.when(n == 0)
        def _():
            dsink_ref[...] = jnp.zeros_like(dsink_ref)

        dsink_ref[...] += dsink

    qg, kp, kc, vp, vc = _attn_specs(lay, d)
    a_blk = lay["a_off"] // (2 * ATTN_GROUP_LANES)
    grp = pl.BlockSpec((WINDOW, ATTN_GROUP_LANES), lambda m, n: (n, m))
    kv_blk = pl.BlockSpec((WINDOW, LANES), lambda m, n: (n, m))
    kv_shape = jax.ShapeDtypeStruct((t, kv_w), F32)
    outs = pl.pallas_call(
        body, name="attn_bwd", grid=(n_groups, t // WINDOW),
        in_specs=[pl.BlockSpec(memory_space=pltpu.SMEM), qg, kp, kc, vp, vc, grp, grp,
                  pl.BlockSpec(memory_space=pl.ANY)],
        out_specs=[pl.BlockSpec((WINDOW, 2 * ATTN_GROUP_LANES), lambda m, n: (n, a_blk + m)),
                   kv_blk, kv_blk, kv_blk, kv_blk, pl.BlockSpec((8, LANES), lambda m, n: (m, 0))],
        out_shape=[jax.ShapeDtypeStruct(dproj.shape, BF16), kv_shape, kv_shape, kv_shape, kv_shape,
                   jax.ShapeDtypeStruct((n_groups * 8, LANES), F32)],
        input_output_aliases={8: 0},
        compiler_params=_params(dimension_semantics=("parallel", "arbitrary")),
    )(sinks, proj, proj, proj, proj, proj, attn_o, dmixed, dproj)
    return outs


def _kv_combine(dkc, dkp, dvc, dvp, dproj, lay):
    t, kv_w = dkc.shape
    nb = t // WINDOW
    kv_blk_idx = lay["k_off"] // (2 * kv_w)

    def body(dkc_ref, dkp_ref, dvc_ref, dvp_ref, dproj_hbm, o_ref):
        del dproj_hbm
        keep = (pl.program_id(0) < nb - 1).astype(F32)
        o_ref[:, :kv_w] = (dkc_ref[...] + keep * dkp_ref[...]).astype(BF16)
        o_ref[:, kv_w:] = (dvc_ref[...] + keep * dvp_ref[...]).astype(BF16)

    cur = pl.BlockSpec((WINDOW, kv_w), lambda n: (n, 0))
    nxt = pl.BlockSpec((WINDOW, kv_w), lambda n: (jnp.minimum(n + 1, nb - 1), 0))
    return pl.pallas_call(
        body, name="kv_combine", grid=(nb,),
        in_specs=[cur, nxt, cur, nxt, pl.BlockSpec(memory_space=pl.ANY)],
        out_specs=pl.BlockSpec((WINDOW, 2 * kv_w), lambda n: (n, kv_blk_idx)),
        out_shape=jax.ShapeDtypeStruct(dproj.shape, BF16),
        input_output_aliases={4: 0},
        compiler_params=_params(dimension_semantics=("parallel",)),
    )(dkc, dkp, dvc, dvp, dproj)


def _lower_bound(lbl_ref):
    l0 = lbl_ref[0:1, :]
    l1 = lbl_ref[1:2, :]
    mx = jnp.maximum(l0, l1)
    e0 = jnp.exp(l0 - mx)
    e1 = jnp.exp(l1 - mx)
    return e0 / (e0 + e1)


def _chunk_masks():
    ti = lax.broadcasted_iota(jnp.int32, (CHUNK, CHUNK), 0)
    si = lax.broadcasted_iota(jnp.int32, (CHUNK, CHUNK), 1)
    diag = ((ti // HALF_CHUNK) == (si // HALF_CHUNK)) & (si <= ti)
    off = (ti >= HALF_CHUNK) & (si < HALF_CHUNK)
    lower = (si <= ti).astype(BF16)
    upper = (si >= ti).astype(BF16)
    return diag, off, lower, upper


def _rnn_gates(rq, rf, lb):
    sf = _sigmoid(rf)
    f = lb + (1.0 - lb) * sf
    sq = _sigmoid(rq)
    return sf, f, jnp.log(f), 1.0 - f, sq, rq * sq


def _rnn_decays(g_cum):
    row = lax.broadcasted_iota(jnp.int32, g_cum.shape, 0)
    ref_d = jnp.where(row < HALF_CHUNK, g_cum[HALF_CHUNK // 2 - 1:HALF_CHUNK // 2],
                      g_cum[HALF_CHUNK + HALF_CHUNK // 2 - 1:HALF_CHUNK + HALF_CHUNK // 2])
    ref_o = g_cum[HALF_CHUNK - 1:HALF_CHUNK]
    last = g_cum[CHUNK - 1:CHUNK]
    return dict(eq_d=jnp.exp(g_cum - ref_d), ek_d=jnp.exp(ref_d - g_cum),
                eq_o=jnp.exp(jnp.minimum(g_cum - ref_o, 0.0)), ek_o=jnp.exp(jnp.minimum(ref_o - g_cum, 0.0)),
                eg=jnp.exp(g_cum), ekl=jnp.exp(last - g_cum), e_last=jnp.exp(last))


def _head(a, j):
    return a[:, j * RNN_HEAD_DIM:(j + 1) * RNN_HEAD_DIM]


def _rnn_specs(t, tb, d):
    gw = RNN_GROUP_HEADS * RNN_HEAD_DIM
    return gw, t // tb, tb // CHUNK


def _rnn_fwd(proj, lb_logits, rnn_gain, mixed, d):
    t = proj.shape[0]
    tb = min(t, 256)
    gw, ntb, nch = _rnn_specs(t, tb, d)
    n_groups = d // gw
    n_heads = d // RNN_HEAD_DIM

    def body(blk_ref, lbl_ref, gain_ref, mixed_hbm, mix_ref, o_ref, st_out_ref, st_ref):
        del mixed_hbm

        @pl.when(pl.program_id(1) == 0)
        def _():
            st_ref[...] = jnp.zeros_like(st_ref)

        lb = _lower_bound(lbl_ref)
        gain = gain_ref[...]
        diag, off, lower, _ = _chunk_masks()

        def chunk(c, carry):
            rows = pl.ds(pl.multiple_of(c * CHUNK, CHUNK), CHUNK)
            rq = blk_ref[rows, 0:gw]
            rf = blk_ref[rows, gw:2 * gw]
            v = blk_ref[rows, 2 * gw:3 * gw]
            rg = blk_ref[rows, 3 * gw:4 * gw]
            _, _, g, k, _, q = _rnn_gates(rq, rf, lb)
            dec = _rnn_decays(_tri_dot(lower, g))
            qd = (q * dec["eq_d"]).astype(BF16)
            kd = (k * dec["ek_d"]).astype(BF16)
            qo = (q * dec["eq_o"]).astype(BF16)
            ko = (k * dec["ek_o"]).astype(BF16)
            qe = (q * dec["eg"]).astype(BF16)
            kl = (k * dec["ekl"]).astype(BF16)
            vb = v.astype(BF16)
            outs = []
            for j in range(RNN_GROUP_HEADS):
                st = st_ref[j]
                st_out_ref[j, c] = st
                attn = jnp.where(diag, _dot_nt(_head(qd, j), _head(kd, j)),
                                 jnp.where(off, _dot_nt(_head(qo, j), _head(ko, j)), 0.0))
                o = _dot(attn.astype(BF16), _head(vb, j)) + _dot_nt(_head(qe, j), st.astype(BF16))
                st_ref[j] = st * _head(dec["e_last"], j) + _dot_tn(_head(vb, j), _head(kl, j))
                rr = lax.rsqrt(jnp.mean(o * o, axis=-1, keepdims=True) + NORM_EPS)
                o_ref[rows, j * RNN_HEAD_DIM:(j + 1) * RNN_HEAD_DIM] = o
                outs.append(o * rr)
            on = jnp.concatenate(outs, axis=1) * gain
            mix_ref[rows, :] = (on * (rg * _sigmoid(rg))).astype(BF16)
            return carry

        lax.fori_loop(0, nch, chunk, 0, unroll=True)

    return pl.pallas_call(
        body, name="rnn_fwd", grid=(n_groups, ntb),
        in_specs=[pl.BlockSpec((tb, 4 * gw), lambda h, i: (i, h)),
                  pl.BlockSpec((2, gw), lambda h, i: (0, h)),
                  pl.BlockSpec((1, gw), lambda h, i: (0, h)),
                  pl.BlockSpec(memory_space=pl.ANY)],
        out_specs=[pl.BlockSpec((tb, gw), lambda h, i: (i, d // gw + h)),
                   pl.BlockSpec((tb, gw), lambda h, i: (i, h)),
                   pl.BlockSpec((RNN_GROUP_HEADS, nch, RNN_HEAD_DIM, RNN_HEAD_DIM), lambda h, i: (h, i, 0, 0))],
        out_shape=[jax.ShapeDtypeStruct(mixed.shape, BF16), jax.ShapeDtypeStruct((t, d), F32),
                   jax.ShapeDtypeStruct((n_heads, t // CHUNK, RNN_HEAD_DIM, RNN_HEAD_DIM), F32)],
        scratch_shapes=[pltpu.VMEM((RNN_GROUP_HEADS, RNN_HEAD_DIM, RNN_HEAD_DIM), F32)],
        input_output_aliases={3: 0},
        compiler_params=_params(dimension_semantics=("parallel", "arbitrary")),
    )(proj, lb_logits, rnn_gain, mixed)


def _rnn_bwd(proj, lb_logits, rnn_gain, o_pre, states, dmixed, d_total, d, rider=None):
    t = proj.shape[0]
    tb = min(t, 256)
    gw, ntb, nch = _rnn_specs(t, tb, d)
    n_groups = d // gw
    n_rin = 0 if rider is None else len(rider.operands)
    n_rout = 0 if rider is None else len(rider.out_shapes)

    def body(*refs):
        blk_ref, lbl_ref, gain_ref, o_ref, st_in_ref, dm_ref = refs[:6]
        rin = refs[6:6 + n_rin]
        dproj_ref, dgain_ref, dlb_ref = refs[6 + n_rin:9 + n_rin]
        rout = refs[9 + n_rin:9 + n_rin + n_rout]
        dst_ref = refs[9 + n_rin + n_rout]
        if rider is not None:
            rider.emit(pl.program_id(0) * ntb + pl.program_id(1), n_groups * ntb, rin, rout, refs[-2], refs[-1])

        @pl.when(pl.program_id(1) == 0)
        def _():
            dst_ref[...] = jnp.zeros_like(dst_ref)
            dgain_ref[...] = jnp.zeros_like(dgain_ref)
            dlb_ref[...] = jnp.zeros_like(dlb_ref)

        lb = _lower_bound(lbl_ref)
        gain = gain_ref[...]
        diag, off, lower, upper = _chunk_masks()
        last_row = lax.broadcasted_iota(jnp.int32, (CHUNK, RNN_HEAD_DIM), 0) == CHUNK - 1

        def chunk(step, carry):
            c = nch - 1 - step
            rows = pl.ds(pl.multiple_of(c * CHUNK, CHUNK), CHUNK)
            rq = blk_ref[rows, 0:gw]
            rf = blk_ref[rows, gw:2 * gw]
            v = blk_ref[rows, 2 * gw:3 * gw]
            rg = blk_ref[rows, 3 * gw:4 * gw]
            sf, f, g, k, sq, q = _rnn_gates(rq, rf, lb)
            dec = _rnn_decays(_tri_dot(lower, g))
            qd = (q * dec["eq_d"]).astype(BF16)
            kd = (k * dec["ek_d"]).astype(BF16)
            qo = (q * dec["eq_o"]).astype(BF16)
            ko = (k * dec["ek_o"]).astype(BF16)
            qe = (q * dec["eg"]).astype(BF16)
            kl = (k * dec["ekl"]).astype(BF16)
            vb = v.astype(BF16)

            o = o_ref[rows, :]
            dmix = dm_ref[rows, :]
            sg = _sigmoid(rg)
            n_parts = []
            for j in range(RNN_GROUP_HEADS):
                oj = _head(o, j)
                n_parts.append(oj * lax.rsqrt(jnp.mean(oj * oj, axis=-1, keepdims=True) + NORM_EPS))
            nrm = jnp.concatenate(n_parts, axis=1)
            d_on = dmix * (rg * sg)
            d_rg = dmix * (nrm * gain) * (sg * (1.0 + rg * (1.0 - sg)))
            dgain_ref[...] += jnp.sum(d_on * nrm, axis=0, keepdims=True)
            dn = d_on * gain

            dq_parts, dk_parts, dv_parts, dg_parts = [], [], [], []
            for j in range(RNN_GROUP_HEADS):
                oj, nj, dnj = _head(o, j), _head(nrm, j), _head(dn, j)
                rr = lax.rsqrt(jnp.mean(oj * oj, axis=-1, keepdims=True) + NORM_EPS)
                do = (rr * (dnj - nj * jnp.mean(dnj * nj, axis=-1, keepdims=True))).astype(BF16)
                st = st_in_ref[j, c]
                dst = dst_ref[j]
                stb, dstb = st.astype(BF16), dst.astype(BF16)
                qdj, kdj, qoj, koj = _head(qd, j), _head(kd, j), _head(qo, j), _head(ko, j)
                attn = jnp.where(diag, _dot_nt(qdj, kdj), jnp.where(off, _dot_nt(qoj, koj), 0.0))
                dattn = _dot_nt(do, _head(vb, j))
                da_d = jnp.where(diag, dattn, 0.0).astype(BF16)
                da_o = jnp.where(off, dattn, 0.0).astype(BF16)
                dv = _dot_tn(attn.astype(BF16), do) + _dot_nt(_head(kl, j), dstb)
                dq_inter = _dot(do, stb) * _head(dec["eg"], j)
                dq_d, dq_o = _dot(da_d, kdj), _dot(da_o, koj)
                dq = dq_inter + dq_d * _head(dec["eq_d"], j) + dq_o * _head(dec["eq_o"], j)
                dk_inter = _dot(_head(vb, j), dstb) * _head(dec["ekl"], j)
                dk_d, dk_o = _dot_tn(da_d, qdj), _dot_tn(da_o, qoj)
                dk = dk_inter + dk_d * _head(dec["ek_d"], j) + dk_o * _head(dec["ek_o"], j)
                kj, qj = _head(k, j), _head(q, j)
                e_last = _head(dec["e_last"], j)
                extra = (jnp.sum(kj * dk_inter, axis=0, keepdims=True)
                         + e_last * jnp.sum(st * dst, axis=0, keepdims=True))
                dg_cum = (qj * dq_inter - kj * dk_inter
                          + (qdj.astype(F32) * dq_d + qoj.astype(F32) * dq_o)
                          - (kdj.astype(F32) * dk_d + koj.astype(F32) * dk_o))
                dg_parts.append(jnp.where(last_row, dg_cum + extra, dg_cum))
                dst_ref[j] = dst * e_last + _dot_tn(do, _head(qe, j))
                dq_parts.append(dq)
                dk_parts.append(dk)
                dv_parts.append(dv)

            dq = jnp.concatenate(dq_parts, axis=1)
            dk = jnp.concatenate(dk_parts, axis=1)
            dg = _tri_dot(upper, jnp.concatenate(dg_parts, axis=1))
            df = dg / f - dk
            dlb_ref[...] += jnp.sum(df * (1.0 - sf), axis=0, keepdims=True)
            d_rf = df * (1.0 - lb) * (sf * (1.0 - sf))
            d_rq = dq * (sq * (1.0 + rq * (1.0 - sq)))
            dproj_ref[rows, 0:gw] = d_rq.astype(BF16)
            dproj_ref[rows, gw:2 * gw] = d_rf.astype(BF16)
            dproj_ref[rows, 2 * gw:3 * gw] = jnp.concatenate(dv_parts, axis=1).astype(BF16)
            dproj_ref[rows, 3 * gw:4 * gw] = d_rg.astype(BF16)
            return carry

        lax.fori_loop(0, nch, chunk, 0, unroll=True)

    rev = lambda i: ntb - 1 - i
    vec = pl.BlockSpec((1, gw), lambda h, i: (0, h))
    scratch = [pltpu.VMEM((RNN_GROUP_HEADS, RNN_HEAD_DIM, RNN_HEAD_DIM), F32)]
    out_shapes = [jax.ShapeDtypeStruct((t, d_total), BF16), jax.ShapeDtypeStruct((1, d), F32),
                  jax.ShapeDtypeStruct((1, d), F32)]
    operands = [proj, lb_logits, rnn_gain, o_pre, states, dmixed]
    sem = ("parallel", "arbitrary")
    if rider is not None:
        scratch += rider.scratch()
        out_shapes += list(rider.out_shapes)
        operands += list(rider.operands)
        sem = ("arbitrary", "arbitrary")
    return pl.pallas_call(
        body, name="rnn_bwd", grid=(n_groups, ntb),
        in_specs=[pl.BlockSpec((tb, 4 * gw), lambda h, i: (rev(i), h)),
                  pl.BlockSpec((2, gw), lambda h, i: (0, h)), vec,
                  pl.BlockSpec((tb, gw), lambda h, i: (rev(i), h)),
                  pl.BlockSpec((RNN_GROUP_HEADS, nch, RNN_HEAD_DIM, RNN_HEAD_DIM), lambda h, i: (h, rev(i), 0, 0)),
                  pl.BlockSpec((tb, gw), lambda h, i: (rev(i), d // gw + h))] + [HBM_SPEC] * n_rin,
        out_specs=[pl.BlockSpec((tb, 4 * gw), lambda h, i: (rev(i), h)), vec, vec] + [HBM_SPEC] * n_rout,
        out_shape=out_shapes, scratch_shapes=scratch,
        compiler_params=_params(dimension_semantics=sem),
    )(*operands)


def _local_grads(x, target, w_in_full, w_out, sinks, lb_logits, rnn_gain, pre_gain, post_gain, sc=None):
    t, d = x.shape
    comm = sc is not None
    lay = _layout(d)
    perm = lay["perm"]
    h, ht = _prenorm_fwd(x, pre_gain)
    if comm:
        proj, w_in_full = _proj_gather_mm(h, w_in_full, perm, sc)
        mixed, attn_o, w_out_full = _attn_fwd(proj, sinks, lay, d, _gather_rider(w_out, 0, 0.8))
    else:
        (proj,) = _proj_mm(h, w_in_full, perm)
        w_out_full = w_out
        mixed, attn_o = _attn_fwd(proj, sinks, lay, d)
    mixed, o_pre, states = _rnn_fwd(proj, lb_logits, rnn_gain, mixed, d)
    y = _out_mm(mixed, w_out_full)
    dy, dz, g_post, loss = _post_loss(x, y, target, post_gain)
    gw_out = _gw_out_mm(_transpose_bf16(mixed, "mixed_t"), dy)
    rider = None
    if comm:
        dmixed, recv_out = _dmixed_mm(dy, w_out_full, _pair_exchange_rider(gw_out, stacked=True))
        p_out = _pair_sum_out(gw_out, recv_out)
        rider = _chip_exchange_rider(p_out, lambda ref, chip: ref.at[chip])
    else:
        (dmixed,) = _dmixed_mm(dy, w_out_full)
    dproj, g_rnn, g_lb, *r_out = _rnn_bwd(proj, lb_logits, rnn_gain, o_pre, states, dmixed, lay["total"], d, rider)
    dproj, dkc, dkp, dvc, dvp, dsink = _attn_bwd(proj, sinks, attn_o, dmixed, dproj, lay, d)
    dproj = _kv_combine(dkc, dkp, dvc, dvp, dproj, lay)
    if comm:
        c = lax.axis_index("c")
        (g_other,) = _gw_in_mm("gw_in_mm_other", ht, dproj, perm, 1 - c)
        g_mine, recv_in = _gw_in_mm("gw_in_mm_mine", ht, dproj, perm, c, _pair_exchange_rider(g_other))
        p_in = _pair_sum_in(g_mine, recv_in, sc)
        rider = _chip_exchange_rider(
            p_in, lambda ref, chip: ref.at[:, pl.ds(pl.multiple_of(chip * sc, LANES), sc)])
    dh, *r_in = _dh_mm(dproj, w_in_full, perm, rider)
    grad_x, g_pre = _prenorm_bwd(x, dh, dz, pre_gain)
    heads_per_group = ATTN_GROUP_LANES // ATTN_HEAD_DIM
    g_sink = dsink.reshape(d // ATTN_GROUP_LANES, 8, LANES)[:, 0, :heads_per_group].reshape(1, -1)
    small = dict(sink=g_sink, lb=g_lb, rnn=g_rnn, pre=g_pre, post=g_post)
    if comm:
        return loss, grad_x, (p_in, r_in[0]), (p_out, r_out[0]), small
    gw_in = jnp.stack([_gw_in_mm("gw_in_mm_%d" % half, ht, dproj, perm, half)[0] for half in range(2)])
    return loss, grad_x, gw_in, gw_out, small


def _mesh_pos():
    x, y, c = lax.axis_index("x"), lax.axis_index("y"), lax.axis_index("c")
    chips = [(1 - x, y), (x, 1 - y), (1 - x, 1 - y)]
    return x, y, c, chips


def _remote(src, dst, send_sem, recv_sem, device):
    return pltpu.make_async_remote_copy(src_ref=src, dst_ref=dst, send_sem=send_sem, recv_sem=recv_sem,
                                        device_id=device, device_id_type=MESH)


HBM_SPEC = pl.BlockSpec(memory_space=pl.ANY)


def _gather_rider(part, axis, forward_at):
    rows = part.shape[0] if axis == 1 else part.shape[0] // N_CHIPS
    cols = part.shape[1] // N_CHIPS if axis == 1 else part.shape[1]
    half_rows = rows // 2

    def stages(ins, outs, send_sems, recv_sems):
        del ins
        full = outs[0]

        def piece(chip, half):
            if axis == 1:
                return full.at[pl.ds(half * half_rows, half_rows), pl.ds(pl.multiple_of(chip * cols, LANES), cols)]
            return full.at[pl.ds(pl.multiple_of(chip * rows + half * half_rows, 8), half_rows), :]

        def sends():
            x, y, c, chips = _mesh_pos()
            mine = piece(2 * x + y, c)
            return [_remote(mine, mine, send_sems.at[j], recv_sems.at[j], (px, py, c))
                    for j, (px, py) in enumerate(chips)]

        def forwards(half_of):
            x, y, c, chips = _mesh_pos()
            out = []
            for j, (px, py) in enumerate(chips):
                block = piece(2 * px + py, half_of(c))
                out.append(_remote(block, block, send_sems.at[3 + j], recv_sems.at[3 + j], (x, y, 1 - c)))
            return out

        def start():
            for cp in sends():
                cp.start()

        def forward():
            x, y, c, chips = _mesh_pos()
            for j, (px, py) in enumerate(chips):
                landed = piece(2 * px + py, c)
                _remote(landed, landed, send_sems.at[j], recv_sems.at[j], (x, y, 1 - c)).wait_recv()
            for cp in forwards(lambda c: c):
                cp.start()

        def finish():
            for cp in forwards(lambda c: 1 - c):
                cp.wait_recv()
            for cp in sends() + forwards(lambda c: c):
                cp.wait_send()

        return [(0.0, start), (forward_at, forward), (1.0, finish)]

    return _Rider((part,), (jax.ShapeDtypeStruct(part.shape, BF16),), {0: 0}, 6, stages)


def _chip_exchange_rider(partial, piece):
    if partial.ndim == 3:
        recv_shape = (N_CHIPS - 1,) + partial.shape[1:]
    else:
        recv_shape = (N_CHIPS - 1, partial.shape[0], partial.shape[1] // N_CHIPS)

    def stages(ins, outs, send_sems, recv_sems):
        def copies():
            x, y, c, chips = _mesh_pos()
            return [_remote(piece(ins[0], 2 * px + py), outs[0].at[j], send_sems.at[j], recv_sems.at[j], (px, py, c))
                    for j, (px, py) in enumerate(chips)]

        def start():
            for cp in copies():
                cp.start()

        def finish():
            for cp in copies():
                cp.wait()

        return [(0.0, start), (1.0, finish)]

    return _Rider((partial,), (jax.ShapeDtypeStruct(recv_shape, BF16),), {}, N_CHIPS - 1, stages)


def _pair_exchange_rider(g, stacked=False):
    shape = g.shape[1:] if stacked else g.shape

    def stages(ins, outs, send_sems, recv_sems):
        def copy():
            x, y, c, _ = _mesh_pos()
            src = ins[0].at[1 - c] if stacked else ins[0]
            return _remote(src, outs[0], send_sems.at[0], recv_sems.at[0], (x, y, 1 - c))

        return [(0.0, lambda: copy().start()), (1.0, lambda: copy().wait())]

    return _Rider((g,), (jax.ShapeDtypeStruct(shape, g.dtype),), {}, 1, stages)


def _pair_sum_in(mine, recv, sc):
    hd, d_in = mine.shape
    tr = min(hd, 256)

    def body(a_ref, b_ref, o_ref):
        o_ref[...] = (a_ref[...].astype(F32) + b_ref[...].astype(F32)).astype(BF16)

    blk = pl.BlockSpec((tr, sc), lambda i, j: (i, j))
    return pl.pallas_call(
        body, name="pair_sum_in", grid=(hd // tr, d_in // sc), in_specs=[blk, blk], out_specs=blk,
        out_shape=jax.ShapeDtypeStruct((hd, d_in), BF16),
        compiler_params=_params(dimension_semantics=("parallel", "parallel")),
    )(mine, recv)


def _pair_sum_out(gw_out, recv):
    _, n_chips, hr, d = gw_out.shape
    tr = min(hr, 256)
    c = lax.axis_index("c")

    def body(c_ref, a_ref, b_ref, o_ref):
        del c_ref
        o_ref[...] = (a_ref[...].astype(F32) + b_ref[...].astype(F32)).astype(BF16)

    blk = pl.BlockSpec((None, tr, d), lambda k, i, cc: (k, i, 0))
    gs = pltpu.PrefetchScalarGridSpec(
        num_scalar_prefetch=1, grid=(n_chips, hr // tr),
        in_specs=[pl.BlockSpec((None, None, tr, d), lambda k, i, cc: (cc[0], k, i, 0)), blk], out_specs=blk)
    return pl.pallas_call(
        body, name="pair_sum_out", grid_spec=gs, out_shape=jax.ShapeDtypeStruct((n_chips, hr, d), BF16),
        compiler_params=_params(dimension_semantics=("parallel", "parallel")),
    )(jnp.reshape(c, (1,)).astype(jnp.int32), gw_out, recv)


def _chip_sum_in(p_in, r_in, sc):
    hd = p_in.shape[0]
    tr = min(hd, 256)

    def body(me_ref, p_ref, r_ref, o_ref):
        del me_ref
        acc = p_ref[...].astype(F32)
        for j in range(3):
            acc = acc + r_ref[j].astype(F32)
        o_ref[...] = acc

    gs = pltpu.PrefetchScalarGridSpec(
        num_scalar_prefetch=1, grid=(hd // tr,),
        in_specs=[pl.BlockSpec((tr, sc), lambda i, me: (i, me[0])), pl.BlockSpec((3, tr, sc), lambda i, me: (0, i, 0))],
        out_specs=pl.BlockSpec((tr, sc), lambda i, me: (i, 0)))
    return pl.pallas_call(
        body, name="chip_sum_in", grid_spec=gs, out_shape=jax.ShapeDtypeStruct((hd, sc), F32),
        compiler_params=_params(dimension_semantics=("parallel",)),
    )(_chip_index(), p_in, r_in)


def _chip_sum_out(p_out, r_out):
    _, hr, d = p_out.shape
    tr = min(hr, 256)

    def body(me_ref, p_ref, r_ref, o_ref):
        del me_ref
        acc = p_ref[...].astype(F32)
        for j in range(3):
            acc = acc + r_ref[j].astype(F32)
        o_ref[...] = acc

    gs = pltpu.PrefetchScalarGridSpec(
        num_scalar_prefetch=1, grid=(hr // tr,),
        in_specs=[pl.BlockSpec((None, tr, d), lambda i, me: (me[0], i, 0)), pl.BlockSpec((3, tr, d), lambda i, me: (0, i, 0))],
        out_specs=pl.BlockSpec((tr, d), lambda i, me: (i, 0)))
    return pl.pallas_call(
        body, name="chip_sum_out", grid_spec=gs, out_shape=jax.ShapeDtypeStruct((hr, d), F32),
        compiler_params=_params(dimension_semantics=("parallel",)),
    )(_chip_index(), p_out, r_out)


def _adamw_math(w, g, m, v):
    m_new = ADAM_B1 * m + (1.0 - ADAM_B1) * g
    v_new = ADAM_B2 * v + (1.0 - ADAM_B2) * (g * g)
    m_hat = m_new / (1.0 - ADAM_B1 ** ADAM_STEP)
    v_hat = v_new / (1.0 - ADAM_B2 ** ADAM_STEP)
    delta = -ADAM_LR * (m_hat / (jnp.sqrt(v_hat) + ADAM_EPS) + ADAM_WD * w)
    return delta, m_new, v_new


def _adamw_half(name, w, g_half, m, v, half, carried=None, rider=None):
    rows, cols = w.shape
    hr = rows // 2
    streams = 8
    fit = (VMEM_LIMIT_BYTES // 2) // (streams * 2 * cols * 4)
    tr = min(hr, 1 << (fit.bit_length() - 1))
    nblk = hr // tr
    n_rin = 0 if rider is None else len(rider.operands)
    n_rout = 0 if rider is None else len(rider.out_shapes)
    n_car = 0 if carried is None else 4

    def body(*refs):
        w_ref, g_ref, m_ref, v_ref = refs[1:5]
        rin = refs[5:5 + n_rin]
        outs = refs[5 + n_rin + n_car:]
        d_ref, mo_ref, vo_ref, go_ref = outs[:4]
        rout = outs[4:4 + n_rout]
        if rider is not None:
            rider.emit(pl.program_id(0), nblk, rin, rout, refs[-2], refs[-1])
        g = g_ref[...]
        delta, m_new, v_new = _adamw_math(w_ref[...], g, m_ref[...], v_ref[...])
        d_ref[...] = delta
        mo_ref[...] = m_new
        vo_ref[...] = v_new
        go_ref[...] = g

    row_spec = pl.BlockSpec((tr, cols), lambda i, h: (h[0] * nblk + i, 0))
    g_spec = pl.BlockSpec((tr, cols), lambda i, h: (i, 0))
    shape = jax.ShapeDtypeStruct((rows, cols), F32)
    operands = [w, g_half, m, v]
    out_shapes = [shape] * 4
    scratch, aliases, sem = [], {}, ("parallel",)
    if rider is not None:
        operands += list(rider.operands)
        out_shapes += list(rider.out_shapes)
        scratch = rider.scratch()
        sem = ("arbitrary",)
    if carried is not None:
        aliases = {1 + len(operands) + q: q for q in range(4)}
        operands += list(carried)
    gs = pltpu.PrefetchScalarGridSpec(
        num_scalar_prefetch=1, grid=(nblk,),
        in_specs=[row_spec, g_spec, row_spec, row_spec] + [HBM_SPEC] * (n_rin + n_car),
        out_specs=[row_spec] * 4 + [HBM_SPEC] * n_rout, scratch_shapes=scratch)
    return pl.pallas_call(
        body, name=name, grid_spec=gs, out_shape=out_shapes, input_output_aliases=aliases,
        compiler_params=_params(dimension_semantics=sem),
    )(jnp.reshape(half, (1,)).astype(jnp.int32), *operands)


def _adamw_shard(name, w, red, m, v):
    c = lax.axis_index("c")
    *mine, recv = _adamw_half(name + "_mine", w, red, m, v, c, rider=_pair_exchange_rider(red))
    return _adamw_half(name + "_other", w, recv, m, v, 1 - c, carried=mine)


SMALL_ROWS = 8


def _small_allreduce_adamw(part, w_pack, m_pack, v_pack):
    d = part.shape[1]

    def body(part_ref, w_ref, m_ref, v_ref, g_ref, d_ref, mo_ref, vo_ref, buf_ref, send_sems, recv_sems):
        x, y, c, _ = _mesh_pos()
        me = 4 * x + 2 * y + c
        buf_ref[0] = part_ref[...]
        copies = []
        for r in range(1, 8):
            rx, ry, rc = (r >> 2) & 1, (r >> 1) & 1, r & 1
            peer = (x ^ rx, y ^ ry, c ^ rc)
            copies.append(_remote(buf_ref.at[0], buf_ref.at[r], send_sems.at[r - 1], recv_sems.at[r - 1], peer))
        for cp in copies:
            cp.start()
        for cp in copies:
            cp.wait()
        total = buf_ref[me]
        for s in range(1, 8):
            total = total + buf_ref[s ^ me]
        w = w_ref[...]
        row = lax.broadcasted_iota(jnp.int32, (SMALL_ROWS, d), 0)
        l0, l1 = w[3:4], w[4:5]
        mx = jnp.maximum(l0, l1)
        e0, e1 = jnp.exp(l0 - mx), jnp.exp(l1 - mx)
        lb = e0 / (e0 + e1)
        g_l0 = total[3:4] * lb * (1.0 - lb)
        grads = jnp.where(row == 3, g_l0, jnp.where(row == 4, -g_l0, total))
        g_ref[...] = grads
        delta, m_new, v_new = _adamw_math(w, grads, m_ref[...], v_ref[...])
        d_ref[...] = delta
        mo_ref[...] = m_new
        vo_ref[...] = v_new

    vm = pl.BlockSpec(memory_space=pltpu.VMEM)
    shape = jax.ShapeDtypeStruct((SMALL_ROWS, d), F32)
    return pl.pallas_call(
        body, name="small_allreduce_adamw",
        in_specs=[vm] * 4, out_specs=[vm] * 4, out_shape=[shape] * 4,
        scratch_shapes=[pltpu.VMEM((8, SMALL_ROWS, d), F32), pltpu.SemaphoreType.DMA((7,)), pltpu.SemaphoreType.DMA((7,))],
    )(part, w_pack, m_pack, v_pack)


def _pack_small(d, pre, post, rnn, lb, sink, extra=None):
    rows = [pre, post, rnn, lb[0:1], lb[1:2],
            jnp.pad(sink, ((0, 0), (0, d - sink.shape[1]))),
            jnp.zeros((1, d), F32) if extra is None else extra,
            jnp.zeros((1, d), F32)]
    return jnp.concatenate(rows, axis=0)


def _unpack_small(p, n_sink):
    return dict(pre=p[0:1], post=p[1:2], rnn=p[2:3], lb=p[3:5], sink=p[5:6, :n_sink])


def kernel(x, w_in, attn_sinks, lb_logits, rnn_norm, w_out, pre_norm, post_norm, loss_target, m_w_in, m_attn_sinks, m_lb_logits, m_rnn_norm, m_w_out, m_pre_norm, m_post_norm, v_w_in, v_attn_sinks, v_lb_logits, v_rnn_norm, v_w_out, v_pre_norm, v_post_norm):
    t, d = x.shape[1], x.shape[2]
    sc = w_in.shape[2]
    n_sink = attn_sinks.shape[1]
    w_in2, w_out2 = w_in[0], w_out[0]

    w_in_part = _cast_into_gathered(w_in2, "cast_w_in", 1)
    w_out_part = _cast_into_gathered(w_out2, "cast_w_out", 0)
    loss_part, grad_x, (p_in, r_in), (p_out, r_out), small = _local_grads(
        x[0], loss_target[0], w_in_part, w_out_part, attn_sinks, lb_logits, rnn_norm, pre_norm, post_norm, sc)
    d_w_out, nm_w_out, nv_w_out, g_w_out = _adamw_shard(
        "adamw_w_out", w_out2, _chip_sum_out(p_out, r_out), m_w_out[0], v_w_out[0])
    d_w_in, nm_w_in, nv_w_in, g_w_in = _adamw_shard(
        "adamw_w_in", w_in2, _chip_sum_in(p_in, r_in, sc), m_w_in[0], v_w_in[0])

    lb_part = jnp.concatenate([small["lb"], jnp.zeros_like(small["lb"])], axis=0)
    loss_row = jnp.pad(loss_part[:, :1], ((0, 0), (0, d - 1)))
    part = _pack_small(d, small["pre"], small["post"], small["rnn"], lb_part, small["sink"], loss_row)
    w_pack = _pack_small(d, pre_norm, post_norm, rnn_norm, lb_logits, attn_sinks)
    m_pack = _pack_small(d, m_pre_norm, m_post_norm, m_rnn_norm, m_lb_logits, m_attn_sinks)
    v_pack = _pack_small(d, v_pre_norm, v_post_norm, v_rnn_norm, v_lb_logits, v_attn_sinks)
    g_pack, d_pack, nm_pack, nv_pack = _small_allreduce_adamw(part, w_pack, m_pack, v_pack)
    loss = g_pack[6, 0]
    g, dl, nm, nv = (_unpack_small(p, n_sink) for p in (g_pack, d_pack, nm_pack, nv_pack))

    def ordered(w_in_leaf, w_out_leaf, s):
        return (w_in_leaf[None], s["sink"], s["lb"], s["rnn"], w_out_leaf[None], s["pre"], s["post"])

    return (loss, grad_x[None],
            *ordered(g_w_in, g_w_out, g), *ordered(d_w_in, d_w_out, dl),
            *ordered(nm_w_in, nm_w_out, nm), *ordered(nv_w_in, nv_w_out, nv))
```

```python
import numpy as np
import jax
import jax.numpy as jnp
from jax import lax
from jax.experimental import pallas as pl
from jax.experimental.pallas import tpu as pltpu

F32 = jnp.float32
BF16 = jnp.bfloat16
MESH = pl.DeviceIdType.MESH

NORM_EPS = 1e-6
ATTN_HEAD_DIM = 64
GQA_GROUP = 8
WINDOW = 128
ATTN_STEP_BLOCKS = 4
ATTN_STEP_ROWS = ATTN_STEP_BLOCKS * WINDOW
RNN_HEAD_DIM = 128
CHUNK = 64
HALF_CHUNK = CHUNK // 2
ATTN_SCALE = ATTN_HEAD_DIM ** -0.5

ADAM_LR = 0.001
ADAM_B1 = 0.9
ADAM_B2 = 0.999
ADAM_EPS = 1e-08
ADAM_WD = 0.01
ADAM_STEP = 10

LANES = 128
COL_TILE = 512
RNN_GROUP_HEADS = 4
RNN_STEP_ROWS = 512
ATTN_GROUP_LANES = 1024
N_CHIPS = 4
VMEM_LIMIT_BYTES = 56 * 1024 * 1024
NEG_BIG = -1e30


def _params(**kw):
    return pltpu.CompilerParams(vmem_limit_bytes=VMEM_LIMIT_BYTES, **kw)


def _sigmoid(x):
    return 1.0 / (1.0 + jnp.exp(-x))


def _dot(a, b):
    return jnp.dot(a, b, preferred_element_type=F32)


def _dot_nt(a, b):
    return lax.dot_general(a, b, (((1,), (1,)), ((), ())), preferred_element_type=F32)


def _dot_tn(a, b):
    return lax.dot_general(a, b, (((0,), (0,)), ((), ())), preferred_element_type=F32)


def _split3(x):
    hi = x.astype(BF16)
    r1 = x - hi.astype(F32)
    mid = r1.astype(BF16)
    lo = (r1 - mid.astype(F32)).astype(BF16)
    return hi, mid, lo


def _tri_dot(tri_bf16, x):
    hi, mid, lo = _split3(x)
    return _dot(tri_bf16, hi) + _dot(tri_bf16, mid) + _dot(tri_bf16, lo)


def _layout(d_model):
    d = d_model
    dkv = d // GQA_GROUP
    orig = dict(aq=0, ak=d, av=d + dkv, ag=d + 2 * dkv)
    base = d + 2 * dkv + d
    orig.update(rq=base, rf=base + d, ri=base + 2 * d, rg=base + 3 * d)
    group_w = RNN_GROUP_HEADS * RNN_HEAD_DIM
    cols = []
    for hg in range(d // group_w):
        for seg in ("rq", "rf", "ri", "rg"):
            cols.append((orig[seg] + hg * group_w, group_w))
    for m in range(d // ATTN_GROUP_LANES):
        for seg in ("aq", "ag"):
            cols.append((orig[seg] + m * ATTN_GROUP_LANES, ATTN_GROUP_LANES))
    cols.append((orig["ak"], dkv))
    cols.append((orig["av"], dkv))
    units = []
    for start, width in cols:
        assert start % LANES == 0 and width % LANES == 0
        units += [start + u for u in range(0, width, LANES)]
    per = COL_TILE // LANES
    assert len(units) % per == 0
    tiles = []
    for t in range(len(units) // per):
        run = units[t * per:(t + 1) * per]
        assert run[0] % COL_TILE == 0 and all(run[i] == run[0] + i * LANES for i in range(per))
        tiles.append(run[0] // COL_TILE)
    return dict(a_off=4 * d, k_off=6 * d, v_off=6 * d + dkv, total=6 * d + 2 * dkv,
                perm=np.asarray(tiles, np.int32))


def _chip_index():
    return jnp.reshape(2 * lax.axis_index("x") + lax.axis_index("y"), (1,)).astype(jnp.int32)


def _cast_into_gathered(a, name, axis):
    rows, cols = a.shape
    tr = min(rows, 512)
    nblk = rows // tr

    def body(me_ref, a_ref, o_ref):
        del me_ref
        o_ref[...] = a_ref[...].astype(BF16)

    if axis == 1:
        out_spec = pl.BlockSpec((tr, cols), lambda i, me: (i, me[0]))
        shape = (rows, N_CHIPS * cols)
    else:
        out_spec = pl.BlockSpec((tr, cols), lambda i, me: (me[0] * nblk + i, 0))
        shape = (N_CHIPS * rows, cols)
    gs = pltpu.PrefetchScalarGridSpec(num_scalar_prefetch=1, grid=(nblk,),
                                      in_specs=[pl.BlockSpec((tr, cols), lambda i, me: (i, 0))], out_specs=out_spec)
    return pl.pallas_call(
        body, name=name, grid_spec=gs, out_shape=jax.ShapeDtypeStruct(shape, BF16),
        compiler_params=_params(dimension_semantics=("parallel",)),
    )(_chip_index(), a)


def _transpose_bf16(a, name):
    rows, cols = a.shape
    tr = min(rows, 256)
    tc = min(cols, 2048)

    def body(a_ref, o_ref):
        o_ref[...] = a_ref[...].astype(F32).T.astype(BF16)

    return pl.pallas_call(
        body, name=name, grid=(rows // tr, cols // tc),
        in_specs=[pl.BlockSpec((tr, tc), lambda i, j: (i, j))],
        out_specs=pl.BlockSpec((tc, tr), lambda i, j: (j, i)),
        out_shape=jax.ShapeDtypeStruct((cols, rows), BF16),
        compiler_params=_params(dimension_semantics=("parallel", "parallel")),
    )(a)


def _prenorm_fwd(x, gain):
    t, d = x.shape
    tm = min(t, 256)

    def body(x_ref, g_ref, h_ref, ht_ref):
        xv = x_ref[...]
        r = lax.rsqrt(jnp.mean(xv * xv, axis=-1, keepdims=True) + NORM_EPS)
        h = (xv * r) * g_ref[...]
        h_ref[...] = h.astype(BF16)
        ht_ref[...] = h.T.astype(BF16)

    return pl.pallas_call(
        body, name="prenorm_fwd", grid=(t // tm,),
        in_specs=[pl.BlockSpec((tm, d), lambda i: (i, 0)), pl.BlockSpec((1, d), lambda i: (0, 0))],
        out_specs=[pl.BlockSpec((tm, d), lambda i: (i, 0)), pl.BlockSpec((d, tm), lambda i: (0, i))],
        out_shape=[jax.ShapeDtypeStruct((t, d), BF16), jax.ShapeDtypeStruct((d, t), BF16)],
        compiler_params=_params(dimension_semantics=("parallel",)),
    )(x, gain)


def _post_loss(x, y, target, gain):
    t, d = x.shape
    tm = min(t, 256)
    inv_d = 1.0 / d

    def body(x_ref, y_ref, t_ref, g_ref, dy_ref, dz_ref, gp_ref, loss_ref):
        i = pl.program_id(0)
        yv = y_ref[...]
        gain_v = g_ref[...]
        r = lax.rsqrt(jnp.mean(yv * yv, axis=-1, keepdims=True) + NORM_EPS)
        n = yv * r
        e = (x_ref[...] + n * gain_v) - t_ref[...]
        dz = e * inv_d
        dn = dz * gain_v
        dy = r * (dn - n * jnp.mean(dn * n, axis=-1, keepdims=True))
        dy_ref[...] = dy.astype(BF16)
        dz_ref[...] = dz

        @pl.when(i == 0)
        def _():
            gp_ref[...] = jnp.zeros_like(gp_ref)
            loss_ref[...] = jnp.zeros_like(loss_ref)

        gp_ref[...] += jnp.sum(dz * n, axis=0, keepdims=True)
        row = jnp.sum(e * e, axis=-1, keepdims=True)
        loss_ref[...] += jnp.full(loss_ref.shape, 0.5 * inv_d * jnp.sum(row), F32)

    row_spec = pl.BlockSpec((tm, d), lambda i: (i, 0))
    vec_spec = pl.BlockSpec((1, d), lambda i: (0, 0))
    return pl.pallas_call(
        body, name="post_loss", grid=(t // tm,),
        in_specs=[row_spec, row_spec, row_spec, vec_spec],
        out_specs=[row_spec, row_spec, vec_spec, pl.BlockSpec((1, LANES), lambda i: (0, 0))],
        out_shape=[jax.ShapeDtypeStruct((t, d), BF16), jax.ShapeDtypeStruct((t, d), F32),
                   jax.ShapeDtypeStruct((1, d), F32), jax.ShapeDtypeStruct((1, LANES), F32)],
        compiler_params=_params(dimension_semantics=("arbitrary",)),
    )(x, y, target, gain)


def _prenorm_bwd(x, dh, dz, gain):
    t, d = x.shape
    tm = min(t, 256)

    def body(x_ref, dh_ref, dz_ref, g_ref, gx_ref, gp_ref):
        i = pl.program_id(0)
        xv = x_ref[...]
        r = lax.rsqrt(jnp.mean(xv * xv, axis=-1, keepdims=True) + NORM_EPS)
        n = xv * r
        dhv = dh_ref[...]
        dn = dhv * g_ref[...]
        gx_ref[...] = dz_ref[...] + r * (dn - n * jnp.mean(dn * n, axis=-1, keepdims=True))

        @pl.when(i == 0)
        def _():
            gp_ref[...] = jnp.zeros_like(gp_ref)

        gp_ref[...] += jnp.sum(dhv * n, axis=0, keepdims=True)

    row_spec = pl.BlockSpec((tm, d), lambda i: (i, 0))
    vec_spec = pl.BlockSpec((1, d), lambda i: (0, 0))
    return pl.pallas_call(
        body, name="prenorm_bwd", grid=(t // tm,),
        in_specs=[row_spec, row_spec, row_spec, vec_spec],
        out_specs=[row_spec, vec_spec],
        out_shape=[jax.ShapeDtypeStruct((t, d), F32), jax.ShapeDtypeStruct((1, d), F32)],
        compiler_params=_params(dimension_semantics=("arbitrary",)),
    )(x, dh, dz, gain)


class _Rider:
    def __init__(self, operands, out_shapes, aliases, n_sems, stages):
        self.operands = tuple(operands)
        self.out_shapes = tuple(out_shapes)
        self.aliases = dict(aliases)
        self.n_sems = n_sems
        self.stages = stages

    def scratch(self):
        return [pltpu.SemaphoreType.DMA((self.n_sems,)), pltpu.SemaphoreType.DMA((self.n_sems,))]

    def emit(self, step, n_steps, in_refs, out_refs, send_sems, recv_sems):
        for frac, fn in self.stages(in_refs, out_refs, send_sems, recv_sems):
            at = min(n_steps - 1, int(frac * (n_steps - 1) + 0.5))
            pl.when(step == at)(fn)


def _matmul(name, a, b, *, out_shape, grid, a_spec, b_spec, o_spec, nt=False, perm=None, rider=None):
    nk = grid[2]
    n_steps = grid[0] * grid[1] * grid[2]
    tm, tn = [s for s in o_spec.block_shape if s is not None][-2:]
    acc_in_out = out_shape.dtype == F32
    n_pre = 0 if perm is None else 1
    n_rin = 0 if rider is None else len(rider.operands)
    n_rout = 0 if rider is None else len(rider.out_shapes)
    use_acc = not (nk == 1 or acc_in_out)

    def body(*refs):
        refs = refs[n_pre:]
        a_ref, b_ref = refs[:2]
        rin = refs[2:2 + n_rin]
        o_ref = refs[2 + n_rin]
        rout = refs[3 + n_rin:3 + n_rin + n_rout]
        scratch_refs = refs[3 + n_rin + n_rout:]
        if rider is not None:
            step = (pl.program_id(0) * grid[1] + pl.program_id(1)) * grid[2] + pl.program_id(2)
            rider.emit(step, n_steps, rin, rout, scratch_refs[-2], scratch_refs[-1])
        def product():
            return _dot_nt(a_ref[...], b_ref[...]) if nt else _dot(a_ref[...], b_ref[...])

        if nk == 1:
            o_ref[...] = product().astype(o_ref.dtype)
            return
        acc_ref = o_ref if acc_in_out else scratch_refs[0]
        k = pl.program_id(2)

        @pl.when(k == 0)
        def _():
            acc_ref[...] = jnp.zeros_like(acc_ref)

        acc_ref[...] += product()

        if not acc_in_out:
            @pl.when(k == nk - 1)
            def _():
                o_ref[...] = acc_ref[...].astype(o_ref.dtype)

    scratch = [pltpu.VMEM((tm, tn), F32)] if use_acc else []
    in_specs = [a_spec, b_spec] + [HBM_SPEC] * n_rin
    out_specs = [o_spec] + [HBM_SPEC] * n_rout
    out_shapes = [out_shape]
    operands = [a, b]
    aliases = {}
    sem = ("parallel", "parallel", "arbitrary")
    if rider is not None:
        scratch += rider.scratch()
        out_shapes += list(rider.out_shapes)
        operands += list(rider.operands)
        aliases = {n_pre + 2 + i: 1 + o for i, o in rider.aliases.items()}
        sem = ("arbitrary", "arbitrary", "arbitrary")
    cp = _params(dimension_semantics=sem)
    if perm is None:
        return pl.pallas_call(body, name=name, grid=grid, in_specs=in_specs, out_specs=out_specs,
                              out_shape=out_shapes, scratch_shapes=scratch, input_output_aliases=aliases,
                              compiler_params=cp)(*operands)
    gs = pltpu.PrefetchScalarGridSpec(num_scalar_prefetch=1, grid=grid, in_specs=in_specs,
                                      out_specs=out_specs, scratch_shapes=scratch)
    return pl.pallas_call(body, name=name, grid_spec=gs, out_shape=out_shapes, input_output_aliases=aliases,
                          compiler_params=cp)(jnp.asarray(perm), *operands)


def _proj_mm(h, w_full, perm, rider=None):
    t, d = h.shape
    n_tiles = len(perm)
    tm = min(t, 1024)
    return _matmul(
        "proj_mm", h, w_full, perm=perm, rider=rider, grid=(t // tm, n_tiles, 1),
        out_shape=jax.ShapeDtypeStruct((t, n_tiles * COL_TILE), F32),
        a_spec=pl.BlockSpec((tm, d), lambda i, j, k, p: (i, 0)),
        b_spec=pl.BlockSpec((d, COL_TILE), lambda i, j, k, p: (0, p[j])),
        o_spec=pl.BlockSpec((tm, COL_TILE), lambda i, j, k, p: (i, j)))


def _proj_gather_mm(h, wi_part, perm, sc):
    t, d = h.shape
    n_tiles = len(perm)
    tm = min(t, 1024)
    n_i = t // tm
    hd = d // 2
    nf = sc // COL_TILE
    rem = sc - nf * COL_TILE
    assert 2 * rem == COL_TILE and n_tiles == N_CHIPS * nf + 2
    n_chunks = next(q for q in (4, 3, 2, 1) if nf % q == 0)
    tpc = nf // n_chunks
    n_kinds = n_chunks + 1
    n_sems = 3 * n_kinds
    rem_at = nf + 3 * n_chunks * tpc

    def first_full(chip):
        return (chip * sc + (rem if chip % 2 else 0)) // COL_TILE

    inverse = np.argsort(perm)
    table = np.zeros((N_CHIPS, 2, n_tiles), np.int32)
    for chip in range(N_CHIPS):
        seq = list(range(first_full(chip), first_full(chip) + nf))
        for q in range(n_chunks):
            for src in (chip ^ 2, chip ^ 1, chip ^ 3):
                seq += list(range(first_full(src) + q * tpc, first_full(src) + (q + 1) * tpc))
        seq += [first_full(chip - chip % 2) + nf, first_full((chip ^ 2) - chip % 2) + nf]
        assert sorted(seq) == list(range(n_tiles)), seq
        table[chip, 0] = inverse[seq]
        table[chip, 1] = seq
    me_chip = 2 * lax.axis_index("x") + lax.axis_index("y")
    tab = lax.dynamic_index_in_dim(jnp.asarray(table), me_chip, 0, keepdims=False)

    def body(tab_ref, h_hbm, wi_in, proj_ref, full, hbuf, bbuf, local_sems, send_sems, recv_sems):
        del wi_in
        jj = pl.program_id(0)
        i = pl.program_id(1)
        x, y, c, chips = _mesh_pos()
        sibling = (x, y, 1 - c)

        def piece(chip, half, kind):
            odd = chip % 2
            if kind == n_chunks:
                start, width = chip * sc + (1 - odd) * (nf * COL_TILE), rem
            else:
                start, width = chip * sc + odd * rem + kind * (tpc * COL_TILE), tpc * COL_TILE
            return full.at[pl.ds(half * hd, hd), pl.ds(pl.multiple_of(start, LANES), width)]

        def ici(j, kind):
            mine = piece(2 * x + y, c, kind)
            k = j * n_kinds + kind
            return _remote(mine, mine, send_sems.at[k], recv_sems.at[k], (chips[j][0], chips[j][1], c))

        def landed(j, kind):
            blk = piece(2 * chips[j][0] + chips[j][1], c, kind)
            k = j * n_kinds + kind
            return _remote(blk, blk, send_sems.at[k], recv_sems.at[k], sibling)

        def passed(j, kind, half):
            blk = piece(2 * chips[j][0] + chips[j][1], half, kind)
            k = n_sems + j * n_kinds + kind
            return _remote(blk, blk, send_sems.at[k], recv_sems.at[k], sibling)

        def fetch(pos, slot):
            col = pl.multiple_of(tab_ref[1, pos] * COL_TILE, LANES)
            return pltpu.make_async_copy(full.at[:, pl.ds(col, COL_TILE)], bbuf.at[slot], local_sems.at[slot])

        def load_h():
            return pltpu.make_async_copy(h_hbm, hbuf, local_sems.at[2])

        def relay(j, kind):
            landed(j, kind).wait_recv()
            passed(j, kind, c).start()

        @pl.when(i == 0)
        def _():
            @pl.when(jj == 0)
            def _():
                load_h().start()
                for kind in range(n_kinds):
                    for j in range(3):
                        ici(j, kind).start()
                fetch(0, 0).start()
                load_h().wait()

            for n in range(3 * n_chunks):
                at = nf + n * tpc
                pl.when(jj == at - 2)(lambda n=n: relay(n % 3, n // 3))
                pl.when(jj == at - 1)(lambda n=n: passed(n % 3, n // 3, 1 - c).wait_recv())

            @pl.when(jj == rem_at - 2)
            def _():
                for j in range(3):
                    relay(j, n_chunks)

            @pl.when(jj == rem_at - 1)
            def _():
                for j in range(3):
                    passed(j, n_chunks, 1 - c).wait_recv()

            @pl.when(jj + 1 < n_tiles)
            def _():
                fetch(jj + 1, (jj + 1) % 2).start()

            fetch(jj, jj % 2).wait()

            @pl.when(jj == n_tiles - 1)
            def _():
                for kind in range(n_kinds):
                    for j in range(3):
                        ici(j, kind).wait_send()
                        passed(j, kind, c).wait_send()

        rows = pl.ds(pl.multiple_of(i * tm, tm), tm)
        proj_ref[...] = _dot(hbuf[rows, :], bbuf[jj % 2])

    gs = pltpu.PrefetchScalarGridSpec(
        num_scalar_prefetch=1, grid=(n_tiles, n_i),
        in_specs=[HBM_SPEC, HBM_SPEC],
        out_specs=[pl.BlockSpec((tm, COL_TILE), lambda jj, i, tb: (i, tb[0, jj])), HBM_SPEC],
        scratch_shapes=[pltpu.VMEM((t, d), BF16), pltpu.VMEM((2, d, COL_TILE), BF16), pltpu.SemaphoreType.DMA((3,)),
                        pltpu.SemaphoreType.DMA((2 * n_sems,)), pltpu.SemaphoreType.DMA((2 * n_sems,))])
    return pl.pallas_call(
        body, name="proj_gather_mm", grid_spec=gs,
        out_shape=[jax.ShapeDtypeStruct((t, n_tiles * COL_TILE), F32), jax.ShapeDtypeStruct(wi_part.shape, BF16)],
        input_output_aliases={2: 1},
        compiler_params=_params(dimension_semantics=("arbitrary", "arbitrary")),
    )(tab, h, wi_part)


def _gw_in_mm(name, ht, dproj, perm, half, rider=None):
    d, t = ht.shape
    n_tiles = len(perm)
    hd = d // 2
    tm = min(hd, 1024)
    per_half = hd // tm
    table = jnp.concatenate([jnp.asarray(perm), jnp.reshape(half, (1,)).astype(jnp.int32)])
    return _matmul(
        name, ht, dproj, perm=table, rider=rider, grid=(per_half, n_tiles, 1),
        out_shape=jax.ShapeDtypeStruct((hd, n_tiles * COL_TILE), BF16),
        a_spec=pl.BlockSpec((tm, t), lambda i, j, k, p: (p[n_tiles] * per_half + i, 0)),
        b_spec=pl.BlockSpec((t, COL_TILE), lambda i, j, k, p: (0, j)),
        o_spec=pl.BlockSpec((tm, COL_TILE), lambda i, j, k, p: (i, p[j])))


def _dh_mm(dproj, w_full, perm, rider=None):
    t = dproj.shape[0]
    d = w_full.shape[0]
    n_tiles = len(perm)
    tm = min(t, 2048)
    tn = min(d, 2048)
    return _matmul(
        "dh_mm", dproj, w_full, perm=perm, rider=rider, nt=True, grid=(t // tm, d // tn, n_tiles),
        out_shape=jax.ShapeDtypeStruct((t, d), F32),
        a_spec=pl.BlockSpec((tm, COL_TILE), lambda i, j, k, p: (i, k)),
        b_spec=pl.BlockSpec((tn, COL_TILE), lambda i, j, k, p: (j, p[k])),
        o_spec=pl.BlockSpec((tm, tn), lambda i, j, k, p: (i, j)))


def _out_mm(mixed, w_out_full):
    t, dm = mixed.shape
    d = w_out_full.shape[1]
    tm = min(t, 1024)
    tn = min(d, 512)
    tk = min(dm, 4096)
    return _matmul(
        "out_mm", mixed, w_out_full, grid=(t // tm, d // tn, dm // tk),
        out_shape=jax.ShapeDtypeStruct((t, d), F32),
        a_spec=pl.BlockSpec((tm, tk), lambda i, j, k: (i, k)),
        b_spec=pl.BlockSpec((tk, tn), lambda i, j, k: (k, j)),
        o_spec=pl.BlockSpec((tm, tn), lambda i, j, k: (i, j)))[0]


def _dmixed_mm(dy, w_out_full, rider=None):
    t, d = dy.shape
    dm = w_out_full.shape[0]
    tm = min(t, 1024)
    tn = min(dm, 1024)
    return _matmul(
        "dmixed_mm", dy, w_out_full, nt=True, rider=rider, grid=(t // tm, dm // tn, 1),
        out_shape=jax.ShapeDtypeStruct((t, dm), F32),
        a_spec=pl.BlockSpec((tm, d), lambda i, j, k: (i, 0)),
        b_spec=pl.BlockSpec((tn, d), lambda i, j, k: (j, 0)),
        o_spec=pl.BlockSpec((tm, tn), lambda i, j, k: (i, j)))


def _gw_out_mm(mixed_t, dy):
    dm, t = mixed_t.shape
    d = dy.shape[1]
    hr = dm // (2 * N_CHIPS)
    tn = min(d, 1024)
    return _matmul(
        "gw_out_mm", mixed_t, dy, grid=(dm // hr, d // tn, 1),
        out_shape=jax.ShapeDtypeStruct((2, N_CHIPS, hr, d), BF16),
        a_spec=pl.BlockSpec((hr, t), lambda i, j, k: (i, 0)),
        b_spec=pl.BlockSpec((t, tn), lambda i, j, k: (0, j)),
        o_spec=pl.BlockSpec((None, None, hr, tn), lambda i, j, k: (i % 2, i // 2, 0, j)))[0]


def _lane_half():
    return lax.broadcasted_iota(jnp.int32, (WINDOW, LANES), 1) // ATTN_HEAD_DIM


def _dup_kv(tile, kh):
    return jnp.where(_lane_half() == kh, tile, pltpu.roll(tile, ATTN_HEAD_DIM, 1))


def _stack_heads(tiles, kh):
    half = _lane_half()
    pieces = []
    for g in range(GQA_GROUP):
        pieces.append(jnp.where(half == g % 2, tiles[4 * kh + g // 2], 0.0))
    return jnp.concatenate(pieces, axis=0)


def _unstack_heads(stacked):
    half = _lane_half()
    out = []
    for j in range(GQA_GROUP // 2):
        a = stacked[(2 * j) * WINDOW:(2 * j + 1) * WINDOW]
        b = stacked[(2 * j + 1) * WINDOW:(2 * j + 2) * WINDOW]
        out.append(jnp.where(half == 0, a, b))
    return out


def _attn_probs(qs, kcat, sink_col, n):
    rows = GQA_GROUP * WINDOW
    s = _dot_nt(qs, kcat)
    qi = lax.broadcasted_iota(jnp.int32, (rows, 2 * WINDOW), 0) % WINDOW
    kj = lax.broadcasted_iota(jnp.int32, (rows, 2 * WINDOW), 1)
    first_key = WINDOW * (1 - jnp.minimum(n, 1))
    valid = (kj > qi) & (kj <= qi + WINDOW) & (kj >= first_key)
    s = jnp.where(valid, s, NEG_BIG)
    mx = jnp.maximum(jnp.max(s, axis=-1, keepdims=True), sink_col)
    p = jnp.exp(s - mx)
    p_sink = jnp.exp(sink_col - mx)
    inv = 1.0 / (jnp.sum(p, axis=-1, keepdims=True) + p_sink)
    return p * inv, p_sink * inv


def _attn_operands(sink_ref, q_tiles, k_prev, k_cur, v_prev, v_cur, m, kh):
    qs = _stack_heads([qt * ATTN_SCALE for qt in q_tiles], kh).astype(BF16)
    kcat = jnp.concatenate([_dup_kv(k_prev, kh), _dup_kv(k_cur, kh)], axis=0).astype(BF16)
    vcat = jnp.concatenate([_dup_kv(v_prev, kh), _dup_kv(v_cur, kh)], axis=0).astype(BF16)
    heads_per_group = ATTN_GROUP_LANES // ATTN_HEAD_DIM
    sink_col = jnp.concatenate(
        [jnp.full((WINDOW, 1), sink_ref[0, m * heads_per_group + kh * GQA_GROUP + g], F32)
         for g in range(GQA_GROUP)], axis=0)
    return qs, kcat, vcat, sink_col


def _attn_specs(lay, d):
    a_blk = lay["a_off"] // (2 * ATTN_GROUP_LANES)
    k_blk = lay["k_off"] // LANES
    v_blk = lay["v_off"] // LANES
    before = lambda n: jnp.maximum(ATTN_STEP_BLOCKS * n - 1, 0)
    qg = pl.BlockSpec((ATTN_STEP_ROWS, 2 * ATTN_GROUP_LANES), lambda m, n: (n, a_blk + m))
    kp = pl.BlockSpec((WINDOW, LANES), lambda m, n: (before(n), k_blk + m))
    kc = pl.BlockSpec((ATTN_STEP_ROWS, LANES), lambda m, n: (n, k_blk + m))
    vp = pl.BlockSpec((WINDOW, LANES), lambda m, n: (before(n), v_blk + m))
    vc = pl.BlockSpec((ATTN_STEP_ROWS, LANES), lambda m, n: (n, v_blk + m))
    return qg, kp, kc, vp, vc


def _block_rows(b):
    return slice(b * WINDOW, (b + 1) * WINDOW)


def _kv_tiles(prev_ref, cur_ref, b):
    prev = prev_ref[...] if b == 0 else cur_ref[_block_rows(b - 1), :]
    return prev, cur_ref[_block_rows(b), :]


def _attn_fwd(proj, sinks, lay, d, rider=None):
    t = proj.shape[0]
    n_groups = d // ATTN_GROUP_LANES
    n_blocks = t // ATTN_STEP_ROWS
    pairs = ATTN_GROUP_LANES // LANES
    n_rin = 0 if rider is None else len(rider.operands)
    n_rout = 0 if rider is None else len(rider.out_shapes)

    def body(*refs):
        sink_ref, qg_ref, kp_ref, kc_ref, vp_ref, vc_ref = refs[:6]
        rin = refs[6:6 + n_rin]
        mix_ref, o_ref = refs[6 + n_rin:8 + n_rin]
        rout = refs[8 + n_rin:8 + n_rin + n_rout]
        m = pl.program_id(0)
        n = pl.program_id(1)
        if rider is not None:
            rider.emit(m * n_blocks + n, n_groups * n_blocks, rin, rout, refs[-2], refs[-1])
        for b in range(ATTN_STEP_BLOCKS):
            rows = _block_rows(b)
            k_prev, k_cur = _kv_tiles(kp_ref, kc_ref, b)
            v_prev, v_cur = _kv_tiles(vp_ref, vc_ref, b)
            q_tiles = [qg_ref[rows, p * LANES:(p + 1) * LANES] for p in range(pairs)]
            for kh in range(2):
                qs, kcat, vcat, sink_col = _attn_operands(sink_ref, q_tiles, k_prev, k_cur, v_prev, v_cur, m, kh)
                probs, _ = _attn_probs(qs, kcat, sink_col, ATTN_STEP_BLOCKS * n + b)
                out = _dot(probs.astype(BF16), vcat)
                for j, tile in enumerate(_unstack_heads(out)):
                    p = 4 * kh + j
                    lanes = slice(p * LANES, (p + 1) * LANES)
                    gate = qg_ref[rows, ATTN_GROUP_LANES + p * LANES:ATTN_GROUP_LANES + (p + 1) * LANES]
                    o_ref[rows, lanes] = tile
                    mix_ref[rows, lanes] = (tile * (gate * _sigmoid(gate))).astype(BF16)

    qg, kp, kc, vp, vc = _attn_specs(lay, d)
    out_blk = pl.BlockSpec((ATTN_STEP_ROWS, ATTN_GROUP_LANES), lambda m, n: (n, m))
    out_shapes = [jax.ShapeDtypeStruct((t, 2 * d), BF16), jax.ShapeDtypeStruct((t, d), F32)]
    operands = [sinks, proj, proj, proj, proj, proj]
    scratch, aliases, sem = [], {}, ("parallel", "parallel")
    if rider is not None:
        scratch = rider.scratch()
        out_shapes += list(rider.out_shapes)
        operands += list(rider.operands)
        aliases = {6 + i: 2 + o for i, o in rider.aliases.items()}
        sem = ("arbitrary", "arbitrary")
    return pl.pallas_call(
        body, name="attn_fwd", grid=(n_groups, n_blocks),
        in_specs=[pl.BlockSpec(memory_space=pltpu.SMEM), qg, kp, kc, vp, vc] + [HBM_SPEC] * n_rin,
        out_specs=[out_blk, out_blk] + [HBM_SPEC] * n_rout,
        out_shape=out_shapes, scratch_shapes=scratch, input_output_aliases=aliases,
        compiler_params=_params(dimension_semantics=sem),
    )(*operands)


def _attn_bwd(proj, sinks, attn_o, dmixed, dproj, lay, d):
    t = proj.shape[0]
    n_groups = d // ATTN_GROUP_LANES
    pairs = ATTN_GROUP_LANES // LANES
    kv_w = n_groups * LANES

    def body(sink_ref, qg_ref, kp_ref, kc_ref, vp_ref, vc_ref, o_ref, dm_ref, dproj_hbm,
             dqg_ref, dkc_ref, dkp_ref, dvc_ref, dvp_ref, dsink_ref):
        del dproj_hbm
        m = pl.program_id(0)
        n = pl.program_id(1)
        half = _lane_half()
        sub = lax.broadcasted_iota(jnp.int32, (8, LANES), 0)
        lane = lax.broadcasted_iota(jnp.int32, (8, LANES), 1)
        dsink = jnp.zeros((8, LANES), F32)
        for b in range(ATTN_STEP_BLOCKS):
            rows = _block_rows(b)
            k_prev, k_cur = _kv_tiles(kp_ref, kc_ref, b)
            v_prev, v_cur = _kv_tiles(vp_ref, vc_ref, b)
            q_tiles = [qg_ref[rows, p * LANES:(p + 1) * LANES] for p in range(pairs)]
            do_tiles, o_tiles = [], []
            for p in range(pairs):
                lanes = slice(p * LANES, (p + 1) * LANES)
                gate_lanes = slice(ATTN_GROUP_LANES + p * LANES, ATTN_GROUP_LANES + (p + 1) * LANES)
                gate = qg_ref[rows, gate_lanes]
                sg = _sigmoid(gate)
                dmix = dm_ref[rows, lanes]
                ov = o_ref[rows, lanes]
                dqg_ref[rows, gate_lanes] = (dmix * ov * (sg * (1.0 + gate * (1.0 - sg)))).astype(BF16)
                do_tiles.append(dmix * (gate * sg))
                o_tiles.append(ov)

            dk_cur = dk_prev = dv_cur = dv_prev = jnp.zeros((WINDOW, LANES), F32)
            for kh in range(2):
                qs, kcat, vcat, sink_col = _attn_operands(sink_ref, q_tiles, k_prev, k_cur, v_prev, v_cur, m, kh)
                probs, p_sink = _attn_probs(qs, kcat, sink_col, ATTN_STEP_BLOCKS * n + b)
                dos = _stack_heads(do_tiles, kh)
                delta = jnp.sum(dos * _stack_heads(o_tiles, kh), axis=-1, keepdims=True)
                dos = dos.astype(BF16)
                dp = _dot_nt(dos, vcat)
                ds = (probs * (dp - delta)).astype(BF16)
                dv = _dot_tn(probs.astype(BF16), dos)
                dv = dv + pltpu.roll(dv, ATTN_HEAD_DIM, 1)
                dk = _dot_tn(ds, qs)
                dk = dk + pltpu.roll(dk, ATTN_HEAD_DIM, 1)
                dq = _dot(ds, kcat)
                for j, tile in enumerate(_unstack_heads(dq)):
                    p = 4 * kh + j
                    dqg_ref[rows, p * LANES:(p + 1) * LANES] = (tile * ATTN_SCALE).astype(BF16)
                dk_prev = jnp.where(half == kh, dk[:WINDOW], dk_prev)
                dk_cur = jnp.where(half == kh, dk[WINDOW:], dk_cur)
                dv_prev = jnp.where(half == kh, dv[:WINDOW], dv_prev)
                dv_cur = jnp.where(half == kh, dv[WINDOW:], dv_cur)
                sink_terms = p_sink * delta
                for g in range(GQA_GROUP):
                    val = -jnp.sum(sink_terms[g * WINDOW:(g + 1) * WINDOW])
                    dsink = dsink + jnp.where((sub == 0) & (lane == kh * GQA_GROUP + g), val, 0.0)
            dkc_ref[rows, :] = dk_cur
            dkp_ref[rows, :] = dk_prev
            dvc_ref[rows, :] = dv_cur
            dvp_ref[rows, :] = dv_prev

        @pl.when(n == 0)
        def _():
            dsink_ref[...] = jnp.zeros_like(dsink_ref)

        dsink_ref[...] += dsink

    qg, kp, kc, vp, vc = _attn_specs(lay, d)
    a_blk = lay["a_off"] // (2 * ATTN_GROUP_LANES)
    grp = pl.BlockSpec((ATTN_STEP_ROWS, ATTN_GROUP_LANES), lambda m, n: (n, m))
    kv_blk = pl.BlockSpec((ATTN_STEP_ROWS, LANES), lambda m, n: (n, m))
    kv_shape = jax.ShapeDtypeStruct((t, kv_w), F32)
    outs = pl.pallas_call(
        body, name="attn_bwd", grid=(n_groups, t // ATTN_STEP_ROWS),
        in_specs=[pl.BlockSpec(memory_space=pltpu.SMEM), qg, kp, kc, vp, vc, grp, grp,
                  pl.BlockSpec(memory_space=pl.ANY)],
        out_specs=[pl.BlockSpec((ATTN_STEP_ROWS, 2 * ATTN_GROUP_LANES), lambda m, n: (n, a_blk + m)),
                   kv_blk, kv_blk, kv_blk, kv_blk, pl.BlockSpec((8, LANES), lambda m, n: (m, 0))],
        out_shape=[jax.ShapeDtypeStruct(dproj.shape, BF16), kv_shape, kv_shape, kv_shape, kv_shape,
                   jax.ShapeDtypeStruct((n_groups * 8, LANES), F32)],
        input_output_aliases={8: 0},
        compiler_params=_params(dimension_semantics=("parallel", "arbitrary")),
    )(sinks, proj, proj, proj, proj, proj, attn_o, dmixed, dproj)
    return outs


def _kv_combine(dkc, dkp, dvc, dvp, dproj, lay):
    t, kv_w = dkc.shape
    nb = t // WINDOW
    kv_blk_idx = lay["k_off"] // (2 * kv_w)

    def body(dkc_ref, dkp_ref, dvc_ref, dvp_ref, dproj_hbm, o_ref):
        del dproj_hbm
        keep = (pl.program_id(0) < nb - 1).astype(F32)
        o_ref[:, :kv_w] = (dkc_ref[...] + keep * dkp_ref[...]).astype(BF16)
        o_ref[:, kv_w:] = (dvc_ref[...] + keep * dvp_ref[...]).astype(BF16)

    cur = pl.BlockSpec((WINDOW, kv_w), lambda n: (n, 0))
    nxt = pl.BlockSpec((WINDOW, kv_w), lambda n: (jnp.minimum(n + 1, nb - 1), 0))
    return pl.pallas_call(
        body, name="kv_combine", grid=(nb,),
        in_specs=[cur, nxt, cur, nxt, pl.BlockSpec(memory_space=pl.ANY)],
        out_specs=pl.BlockSpec((WINDOW, 2 * kv_w), lambda n: (n, kv_blk_idx)),
        out_shape=jax.ShapeDtypeStruct(dproj.shape, BF16),
        input_output_aliases={4: 0},
        compiler_params=_params(dimension_semantics=("parallel",)),
    )(dkc, dkp, dvc, dvp, dproj)


def _lower_bound(lbl_ref):
    l0 = lbl_ref[0:1, :]
    l1 = lbl_ref[1:2, :]
    mx = jnp.maximum(l0, l1)
    e0 = jnp.exp(l0 - mx)
    e1 = jnp.exp(l1 - mx)
    return e0 / (e0 + e1)


def _chunk_masks():
    ti = lax.broadcasted_iota(jnp.int32, (CHUNK, CHUNK), 0)
    si = lax.broadcasted_iota(jnp.int32, (CHUNK, CHUNK), 1)
    diag = ((ti // HALF_CHUNK) == (si // HALF_CHUNK)) & (si <= ti)
    off = (ti >= HALF_CHUNK) & (si < HALF_CHUNK)
    lower = (si <= ti).astype(BF16)
    upper = (si >= ti).astype(BF16)
    return diag, off, lower, upper


def _rnn_gates(rq, rf, lb):
    sf = _sigmoid(rf)
    f = lb + (1.0 - lb) * sf
    sq = _sigmoid(rq)
    return sf, f, jnp.log(f), 1.0 - f, sq, rq * sq


def _rnn_decays(g_cum):
    row = lax.broadcasted_iota(jnp.int32, g_cum.shape, 0)
    ref_d = jnp.where(row < HALF_CHUNK, g_cum[HALF_CHUNK // 2 - 1:HALF_CHUNK // 2],
                      g_cum[HALF_CHUNK + HALF_CHUNK // 2 - 1:HALF_CHUNK + HALF_CHUNK // 2])
    ref_o = g_cum[HALF_CHUNK - 1:HALF_CHUNK]
    last = g_cum[CHUNK - 1:CHUNK]
    return dict(eq_d=jnp.exp(g_cum - ref_d), ek_d=jnp.exp(ref_d - g_cum),
                eq_o=jnp.exp(jnp.minimum(g_cum - ref_o, 0.0)), ek_o=jnp.exp(jnp.minimum(ref_o - g_cum, 0.0)),
                eg=jnp.exp(g_cum), ekl=jnp.exp(last - g_cum), e_last=jnp.exp(last))


def _head(a, j):
    return a[:, j * RNN_HEAD_DIM:(j + 1) * RNN_HEAD_DIM]


def _rnn_specs(t, tb, d):
    gw = RNN_GROUP_HEADS * RNN_HEAD_DIM
    return gw, t // tb, tb // CHUNK


def _rnn_fwd(proj, lb_logits, rnn_gain, mixed, d):
    t = proj.shape[0]
    tb = min(t, RNN_STEP_ROWS)
    gw, ntb, nch = _rnn_specs(t, tb, d)
    n_groups = d // gw
    n_heads = d // RNN_HEAD_DIM

    def body(blk_ref, lbl_ref, gain_ref, mixed_hbm, mix_ref, o_ref, st_out_ref, st_ref):
        del mixed_hbm

        @pl.when(pl.program_id(1) == 0)
        def _():
            st_ref[...] = jnp.zeros_like(st_ref)

        lb = _lower_bound(lbl_ref)
        gain = gain_ref[...]
        diag, off, lower, _ = _chunk_masks()

        def chunk(c, carry):
            rows = pl.ds(pl.multiple_of(c * CHUNK, CHUNK), CHUNK)
            rq = blk_ref[rows, 0:gw]
            rf = blk_ref[rows, gw:2 * gw]
            v = blk_ref[rows, 2 * gw:3 * gw]
            rg = blk_ref[rows, 3 * gw:4 * gw]
            _, _, g, k, _, q = _rnn_gates(rq, rf, lb)
            dec = _rnn_decays(_tri_dot(lower, g))
            qd = (q * dec["eq_d"]).astype(BF16)
            kd = (k * dec["ek_d"]).astype(BF16)
            qo = (q * dec["eq_o"]).astype(BF16)
            ko = (k * dec["ek_o"]).astype(BF16)
            qe = (q * dec["eg"]).astype(BF16)
            kl = (k * dec["ekl"]).astype(BF16)
            vb = v.astype(BF16)
            outs = []
            for j in range(RNN_GROUP_HEADS):
                st = st_ref[j]
                st_out_ref[j, c] = st
                attn = jnp.where(diag, _dot_nt(_head(qd, j), _head(kd, j)),
                                 jnp.where(off, _dot_nt(_head(qo, j), _head(ko, j)), 0.0))
                o = _dot(attn.astype(BF16), _head(vb, j)) + _dot_nt(_head(qe, j), st.astype(BF16))
                st_ref[j] = st * _head(dec["e_last"], j) + _dot_tn(_head(vb, j), _head(kl, j))
                rr = lax.rsqrt(jnp.mean(o * o, axis=-1, keepdims=True) + NORM_EPS)
                o_ref[rows, j * RNN_HEAD_DIM:(j + 1) * RNN_HEAD_DIM] = o
                outs.append(o * rr)
            on = jnp.concatenate(outs, axis=1) * gain
            mix_ref[rows, :] = (on * (rg * _sigmoid(rg))).astype(BF16)
            return carry

        lax.fori_loop(0, nch, chunk, 0, unroll=True)

    return pl.pallas_call(
        body, name="rnn_fwd", grid=(n_groups, ntb),
        in_specs=[pl.BlockSpec((tb, 4 * gw), lambda h, i: (i, h)),
                  pl.BlockSpec((2, gw), lambda h, i: (0, h)),
                  pl.BlockSpec((1, gw), lambda h, i: (0, h)),
                  pl.BlockSpec(memory_space=pl.ANY)],
        out_specs=[pl.BlockSpec((tb, gw), lambda h, i: (i, d // gw + h)),
                   pl.BlockSpec((tb, gw), lambda h, i: (i, h)),
                   pl.BlockSpec((RNN_GROUP_HEADS, nch, RNN_HEAD_DIM, RNN_HEAD_DIM), lambda h, i: (h, i, 0, 0))],
        out_shape=[jax.ShapeDtypeStruct(mixed.shape, BF16), jax.ShapeDtypeStruct((t, d), F32),
                   jax.ShapeDtypeStruct((n_heads, t // CHUNK, RNN_HEAD_DIM, RNN_HEAD_DIM), F32)],
        scratch_shapes=[pltpu.VMEM((RNN_GROUP_HEADS, RNN_HEAD_DIM, RNN_HEAD_DIM), F32)],
        input_output_aliases={3: 0},
        compiler_params=_params(dimension_semantics=("parallel", "arbitrary")),
    )(proj, lb_logits, rnn_gain, mixed)


def _rnn_bwd(proj, lb_logits, rnn_gain, o_pre, states, dmixed, d_total, d, rider=None):
    t = proj.shape[0]
    tb = min(t, RNN_STEP_ROWS)
    gw, ntb, nch = _rnn_specs(t, tb, d)
    n_groups = d // gw
    n_rin = 0 if rider is None else len(rider.operands)
    n_rout = 0 if rider is None else len(rider.out_shapes)

    def body(*refs):
        blk_ref, lbl_ref, gain_ref, o_ref, st_in_ref, dm_ref = refs[:6]
        rin = refs[6:6 + n_rin]
        dproj_ref, dgain_ref, dlb_ref = refs[6 + n_rin:9 + n_rin]
        rout = refs[9 + n_rin:9 + n_rin + n_rout]
        dst_ref = refs[9 + n_rin + n_rout]
        if rider is not None:
            rider.emit(pl.program_id(0) * ntb + pl.program_id(1), n_groups * ntb, rin, rout, refs[-2], refs[-1])

        @pl.when(pl.program_id(1) == 0)
        def _():
            dst_ref[...] = jnp.zeros_like(dst_ref)
            dgain_ref[...] = jnp.zeros_like(dgain_ref)
            dlb_ref[...] = jnp.zeros_like(dlb_ref)

        lb = _lower_bound(lbl_ref)
        gain = gain_ref[...]
        diag, off, lower, upper = _chunk_masks()
        last_row = lax.broadcasted_iota(jnp.int32, (CHUNK, RNN_HEAD_DIM), 0) == CHUNK - 1

        def chunk(step, carry):
            c = nch - 1 - step
            rows = pl.ds(pl.multiple_of(c * CHUNK, CHUNK), CHUNK)
            rq = blk_ref[rows, 0:gw]
            rf = blk_ref[rows, gw:2 * gw]
            v = blk_ref[rows, 2 * gw:3 * gw]
            rg = blk_ref[rows, 3 * gw:4 * gw]
            sf, f, g, k, sq, q = _rnn_gates(rq, rf, lb)
            dec = _rnn_decays(_tri_dot(lower, g))
            qd = (q * dec["eq_d"]).astype(BF16)
            kd = (k * dec["ek_d"]).astype(BF16)
            qo = (q * dec["eq_o"]).astype(BF16)
            ko = (k * dec["ek_o"]).astype(BF16)
            qe = (q * dec["eg"]).astype(BF16)
            kl = (k * dec["ekl"]).astype(BF16)
            vb = v.astype(BF16)

            o = o_ref[rows, :]
            dmix = dm_ref[rows, :]
            sg = _sigmoid(rg)
            n_parts = []
            for j in range(RNN_GROUP_HEADS):
                oj = _head(o, j)
                n_parts.append(oj * lax.rsqrt(jnp.mean(oj * oj, axis=-1, keepdims=True) + NORM_EPS))
            nrm = jnp.concatenate(n_parts, axis=1)
            d_on = dmix * (rg * sg)
            d_rg = dmix * (nrm * gain) * (sg * (1.0 + rg * (1.0 - sg)))
            dgain_ref[...] += jnp.sum(d_on * nrm, axis=0, keepdims=True)
            dn = d_on * gain

            dq_parts, dk_parts, dv_parts, dg_parts = [], [], [], []
            for j in range(RNN_GROUP_HEADS):
                oj, nj, dnj = _head(o, j), _head(nrm, j), _head(dn, j)
                rr = lax.rsqrt(jnp.mean(oj * oj, axis=-1, keepdims=True) + NORM_EPS)
                do = (rr * (dnj - nj * jnp.mean(dnj * nj, axis=-1, keepdims=True))).astype(BF16)
                st = st_in_ref[j, c]
                dst = dst_ref[j]
                stb, dstb = st.astype(BF16), dst.astype(BF16)
                qdj, kdj, qoj, koj = _head(qd, j), _head(kd, j), _head(qo, j), _head(ko, j)
                attn = jnp.where(diag, _dot_nt(qdj, kdj), jnp.where(off, _dot_nt(qoj, koj), 0.0))
                dattn = _dot_nt(do, _head(vb, j))
                da_d = jnp.where(diag, dattn, 0.0).astype(BF16)
                da_o = jnp.where(off, dattn, 0.0).astype(BF16)
                dv = _dot_tn(attn.astype(BF16), do) + _dot_nt(_head(kl, j), dstb)
                dq_inter = _dot(do, stb) * _head(dec["eg"], j)
                dq_d, dq_o = _dot(da_d, kdj), _dot(da_o, koj)
                dq = dq_inter + dq_d * _head(dec["eq_d"], j) + dq_o * _head(dec["eq_o"], j)
                dk_inter = _dot(_head(vb, j), dstb) * _head(dec["ekl"], j)
                dk_d, dk_o = _dot_tn(da_d, qdj), _dot_tn(da_o, qoj)
                dk = dk_inter + dk_d * _head(dec["ek_d"], j) + dk_o * _head(dec["ek_o"], j)
                kj, qj = _head(k, j), _head(q, j)
                e_last = _head(dec["e_last"], j)
                extra = (jnp.sum(kj * dk_inter, axis=0, keepdims=True)
                         + e_last * jnp.sum(st * dst, axis=0, keepdims=True))
                dg_cum = (qj * dq_inter - kj * dk_inter
                          + (qdj.astype(F32) * dq_d + qoj.astype(F32) * dq_o)
                          - (kdj.astype(F32) * dk_d + koj.astype(F32) * dk_o))
                dg_parts.append(jnp.where(last_row, dg_cum + extra, dg_cum))
                dst_ref[j] = dst * e_last + _dot_tn(do, _head(qe, j))
                dq_parts.append(dq)
                dk_parts.append(dk)
                dv_parts.append(dv)

            dq = jnp.concatenate(dq_parts, axis=1)
            dk = jnp.concatenate(dk_parts, axis=1)
            dg = _tri_dot(upper, jnp.concatenate(dg_parts, axis=1))
            df = dg / f - dk
            dlb_ref[...] += jnp.sum(df * (1.0 - sf), axis=0, keepdims=True)
            d_rf = df * (1.0 - lb) * (sf * (1.0 - sf))
            d_rq = dq * (sq * (1.0 + rq * (1.0 - sq)))
            dproj_ref[rows, 0:gw] = d_rq.astype(BF16)
            dproj_ref[rows, gw:2 * gw] = d_rf.astype(BF16)
            dproj_ref[rows, 2 * gw:3 * gw] = jnp.concatenate(dv_parts, axis=1).astype(BF16)
            dproj_ref[rows, 3 * gw:4 * gw] = d_rg.astype(BF16)
            return carry

        lax.fori_loop(0, nch, chunk, 0, unroll=True)

    rev = lambda i: ntb - 1 - i
    vec = pl.BlockSpec((1, gw), lambda h, i: (0, h))
    scratch = [pltpu.VMEM((RNN_GROUP_HEADS, RNN_HEAD_DIM, RNN_HEAD_DIM), F32)]
    out_shapes = [jax.ShapeDtypeStruct((t, d_total), BF16), jax.ShapeDtypeStruct((1, d), F32),
                  jax.ShapeDtypeStruct((1, d), F32)]
    operands = [proj, lb_logits, rnn_gain, o_pre, states, dmixed]
    sem = ("parallel", "arbitrary")
    if rider is not None:
        scratch += rider.scratch()
        out_shapes += list(rider.out_shapes)
        operands += list(rider.operands)
        sem = ("arbitrary", "arbitrary")
    return pl.pallas_call(
        body, name="rnn_bwd", grid=(n_groups, ntb),
        in_specs=[pl.BlockSpec((tb, 4 * gw), lambda h, i: (rev(i), h)),
                  pl.BlockSpec((2, gw), lambda h, i: (0, h)), vec,
                  pl.BlockSpec((tb, gw), lambda h, i: (rev(i), h)),
                  pl.BlockSpec((RNN_GROUP_HEADS, nch, RNN_HEAD_DIM, RNN_HEAD_DIM), lambda h, i: (h, rev(i), 0, 0)),
                  pl.BlockSpec((tb, gw), lambda h, i: (rev(i), d // gw + h))] + [HBM_SPEC] * n_rin,
        out_specs=[pl.BlockSpec((tb, 4 * gw), lambda h, i: (rev(i), h)), vec, vec] + [HBM_SPEC] * n_rout,
        out_shape=out_shapes, scratch_shapes=scratch,
        compiler_params=_params(dimension_semantics=sem),
    )(*operands)


def _local_grads(x, target, w_in_full, w_out, sinks, lb_logits, rnn_gain, pre_gain, post_gain, sc=None):
    t, d = x.shape
    comm = sc is not None
    lay = _layout(d)
    perm = lay["perm"]
    h, ht = _prenorm_fwd(x, pre_gain)
    if comm:
        proj, w_in_full = _proj_gather_mm(h, w_in_full, perm, sc)
        mixed, attn_o, w_out_full = _attn_fwd(proj, sinks, lay, d, _gather_rider(w_out, 0, 0.8))
    else:
        (proj,) = _proj_mm(h, w_in_full, perm)
        w_out_full = w_out
        mixed, attn_o = _attn_fwd(proj, sinks, lay, d)
    mixed, o_pre, states = _rnn_fwd(proj, lb_logits, rnn_gain, mixed, d)
    y = _out_mm(mixed, w_out_full)
    dy, dz, g_post, loss = _post_loss(x, y, target, post_gain)
    gw_out = _gw_out_mm(_transpose_bf16(mixed, "mixed_t"), dy)
    rider = None
    if comm:
        dmixed, recv_out = _dmixed_mm(dy, w_out_full, _pair_exchange_rider(gw_out, stacked=True))
        p_out = _pair_sum_out(gw_out, recv_out)
        rider = _chip_exchange_rider(p_out, lambda ref, chip: ref.at[chip])
    else:
        (dmixed,) = _dmixed_mm(dy, w_out_full)
    dproj, g_rnn, g_lb, *r_out = _rnn_bwd(proj, lb_logits, rnn_gain, o_pre, states, dmixed, lay["total"], d, rider)
    dproj, dkc, dkp, dvc, dvp, dsink = _attn_bwd(proj, sinks, attn_o, dmixed, dproj, lay, d)
    dproj = _kv_combine(dkc, dkp, dvc, dvp, dproj, lay)
    if comm:
        c = lax.axis_index("c")
        (g_other,) = _gw_in_mm("gw_in_mm_other", ht, dproj, perm, 1 - c)
        g_mine, recv_in = _gw_in_mm("gw_in_mm_mine", ht, dproj, perm, c, _pair_exchange_rider(g_other))
        p_in = _pair_sum_in(g_mine, recv_in, sc)
        rider = _chip_exchange_rider(
            p_in, lambda ref, chip: ref.at[:, pl.ds(pl.multiple_of(chip * sc, LANES), sc)])
    dh, *r_in = _dh_mm(dproj, w_in_full, perm, rider)
    grad_x, g_pre = _prenorm_bwd(x, dh, dz, pre_gain)
    heads_per_group = ATTN_GROUP_LANES // ATTN_HEAD_DIM
    g_sink = dsink.reshape(d // ATTN_GROUP_LANES, 8, LANES)[:, 0, :heads_per_group].reshape(1, -1)
    small = dict(sink=g_sink, lb=g_lb, rnn=g_rnn, pre=g_pre, post=g_post)
    if comm:
        return loss, grad_x, (p_in, r_in[0]), (p_out, r_out[0]), small
    gw_in = jnp.stack([_gw_in_mm("gw_in_mm_%d" % half, ht, dproj, perm, half)[0] for half in range(2)])
    return loss, grad_x, gw_in, gw_out, small


def _mesh_pos():
    x, y, c = lax.axis_index("x"), lax.axis_index("y"), lax.axis_index("c")
    chips = [(1 - x, y), (x, 1 - y), (1 - x, 1 - y)]
    return x, y, c, chips


def _remote(src, dst, send_sem, recv_sem, device):
    return pltpu.make_async_remote_copy(src_ref=src, dst_ref=dst, send_sem=send_sem, recv_sem=recv_sem,
                                        device_id=device, device_id_type=MESH)


HBM_SPEC = pl.BlockSpec(memory_space=pl.ANY)


def _gather_rider(part, axis, forward_at):
    rows = part.shape[0] if axis == 1 else part.shape[0] // N_CHIPS
    cols = part.shape[1] // N_CHIPS if axis == 1 else part.shape[1]
    half_rows = rows // 2

    def stages(ins, outs, send_sems, recv_sems):
        del ins
        full = outs[0]

        def piece(chip, half):
            if axis == 1:
                return full.at[pl.ds(half * half_rows, half_rows), pl.ds(pl.multiple_of(chip * cols, LANES), cols)]
            return full.at[pl.ds(pl.multiple_of(chip * rows + half * half_rows, 8), half_rows), :]

        def sends():
            x, y, c, chips = _mesh_pos()
            mine = piece(2 * x + y, c)
            return [_remote(mine, mine, send_sems.at[j], recv_sems.at[j], (px, py, c))
                    for j, (px, py) in enumerate(chips)]

        def forwards(half_of):
            x, y, c, chips = _mesh_pos()
            out = []
            for j, (px, py) in enumerate(chips):
                block = piece(2 * px + py, half_of(c))
                out.append(_remote(block, block, send_sems.at[3 + j], recv_sems.at[3 + j], (x, y, 1 - c)))
            return out

        def start():
            for cp in sends():
                cp.start()

        def forward():
            x, y, c, chips = _mesh_pos()
            for j, (px, py) in enumerate(chips):
                landed = piece(2 * px + py, c)
                _remote(landed, landed, send_sems.at[j], recv_sems.at[j], (x, y, 1 - c)).wait_recv()
            for cp in forwards(lambda c: c):
                cp.start()

        def finish():
            for cp in forwards(lambda c: 1 - c):
                cp.wait_recv()
            for cp in sends() + forwards(lambda c: c):
                cp.wait_send()

        return [(0.0, start), (forward_at, forward), (1.0, finish)]

    return _Rider((part,), (jax.ShapeDtypeStruct(part.shape, BF16),), {0: 0}, 6, stages)


def _chip_exchange_rider(partial, piece):
    if partial.ndim == 3:
        recv_shape = (N_CHIPS - 1,) + partial.shape[1:]
    else:
        recv_shape = (N_CHIPS - 1, partial.shape[0], partial.shape[1] // N_CHIPS)

    def stages(ins, outs, send_sems, recv_sems):
        def copies():
            x, y, c, chips = _mesh_pos()
            return [_remote(piece(ins[0], 2 * px + py), outs[0].at[j], send_sems.at[j], recv_sems.at[j], (px, py, c))
                    for j, (px, py) in enumerate(chips)]

        def start():
            for cp in copies():
                cp.start()

        def finish():
            for cp in copies():
                cp.wait()

        return [(0.0, start), (1.0, finish)]

    return _Rider((partial,), (jax.ShapeDtypeStruct(recv_shape, BF16),), {}, N_CHIPS - 1, stages)


def _pair_exchange_rider(g, stacked=False):
    shape = g.shape[1:] if stacked else g.shape

    def stages(ins, outs, send_sems, recv_sems):
        def copy():
            x, y, c, _ = _mesh_pos()
            src = ins[0].at[1 - c] if stacked else ins[0]
            return _remote(src, outs[0], send_sems.at[0], recv_sems.at[0], (x, y, 1 - c))

        return [(0.0, lambda: copy().start()), (1.0, lambda: copy().wait())]

    return _Rider((g,), (jax.ShapeDtypeStruct(shape, g.dtype),), {}, 1, stages)


def _pair_sum_in(mine, recv, sc):
    hd, d_in = mine.shape
    tr = min(hd, 256)

    def body(a_ref, b_ref, o_ref):
        o_ref[...] = (a_ref[...].astype(F32) + b_ref[...].astype(F32)).astype(BF16)

    blk = pl.BlockSpec((tr, sc), lambda i, j: (i, j))
    return pl.pallas_call(
        body, name="pair_sum_in", grid=(hd // tr, d_in // sc), in_specs=[blk, blk], out_specs=blk,
        out_shape=jax.ShapeDtypeStruct((hd, d_in), BF16),
        compiler_params=_params(dimension_semantics=("parallel", "parallel")),
    )(mine, recv)


def _pair_sum_out(gw_out, recv):
    _, n_chips, hr, d = gw_out.shape
    tr = min(hr, 256)
    c = lax.axis_index("c")

    def body(c_ref, a_ref, b_ref, o_ref):
        del c_ref
        o_ref[...] = (a_ref[...].astype(F32) + b_ref[...].astype(F32)).astype(BF16)

    blk = pl.BlockSpec((None, tr, d), lambda k, i, cc: (k, i, 0))
    gs = pltpu.PrefetchScalarGridSpec(
        num_scalar_prefetch=1, grid=(n_chips, hr // tr),
        in_specs=[pl.BlockSpec((None, None, tr, d), lambda k, i, cc: (cc[0], k, i, 0)), blk], out_specs=blk)
    return pl.pallas_call(
        body, name="pair_sum_out", grid_spec=gs, out_shape=jax.ShapeDtypeStruct((n_chips, hr, d), BF16),
        compiler_params=_params(dimension_semantics=("parallel", "parallel")),
    )(jnp.reshape(c, (1,)).astype(jnp.int32), gw_out, recv)


def _place():
    return jnp.stack([2 * lax.axis_index("x") + lax.axis_index("y"), lax.axis_index("c")]).astype(jnp.int32)


def _chip_sum_in(p_in, r_in, sc):
    hd = p_in.shape[0]
    tr = min(hd, 256)
    nblk = hd // tr

    def body(pos_ref, p_ref, r_ref, o_ref):
        del pos_ref
        acc = p_ref[...].astype(F32)
        for j in range(3):
            acc = acc + r_ref[j].astype(F32)
        o_ref[...] = acc

    gs = pltpu.PrefetchScalarGridSpec(
        num_scalar_prefetch=1, grid=(nblk,),
        in_specs=[pl.BlockSpec((tr, sc), lambda i, pos: (i, pos[0])), pl.BlockSpec((3, tr, sc), lambda i, pos: (0, i, 0))],
        out_specs=pl.BlockSpec((tr, sc), lambda i, pos: (pos[1] * nblk + i, 0)))
    return pl.pallas_call(
        body, name="chip_sum_in", grid_spec=gs, out_shape=jax.ShapeDtypeStruct((2 * hd, sc), F32),
        compiler_params=_params(dimension_semantics=("parallel",)),
    )(_place(), p_in, r_in)


def _chip_sum_out(p_out, r_out):
    _, hr, d = p_out.shape
    tr = min(hr, 256)
    nblk = hr // tr

    def body(pos_ref, p_ref, r_ref, o_ref):
        del pos_ref
        acc = p_ref[...].astype(F32)
        for j in range(3):
            acc = acc + r_ref[j].astype(F32)
        o_ref[...] = acc

    gs = pltpu.PrefetchScalarGridSpec(
        num_scalar_prefetch=1, grid=(nblk,),
        in_specs=[pl.BlockSpec((None, tr, d), lambda i, pos: (pos[0], i, 0)), pl.BlockSpec((3, tr, d), lambda i, pos: (0, i, 0))],
        out_specs=pl.BlockSpec((tr, d), lambda i, pos: (pos[1] * nblk + i, 0)))
    return pl.pallas_call(
        body, name="chip_sum_out", grid_spec=gs, out_shape=jax.ShapeDtypeStruct((2 * hr, d), F32),
        compiler_params=_params(dimension_semantics=("parallel",)),
    )(_place(), p_out, r_out)


def _adamw_math(w, g, m, v):
    m_new = ADAM_B1 * m + (1.0 - ADAM_B1) * g
    v_new = ADAM_B2 * v + (1.0 - ADAM_B2) * (g * g)
    m_hat = m_new / (1.0 - ADAM_B1 ** ADAM_STEP)
    v_hat = v_new / (1.0 - ADAM_B2 ** ADAM_STEP)
    delta = -ADAM_LR * (m_hat / (jnp.sqrt(v_hat) + ADAM_EPS) + ADAM_WD * w)
    return delta, m_new, v_new


def _share_halves(g_in, g_out):
    hd = g_in.shape[0] // 2
    hr = g_out.shape[0] // 2

    def body(gi_in, go_in, gi_ref, go_ref, send_sems, recv_sems):
        del gi_in, go_in
        x, y, c, _ = _mesh_pos()
        sibling = (x, y, 1 - c)
        mine_i = gi_ref.at[pl.ds(pl.multiple_of(c * hd, 8), hd), :]
        mine_o = go_ref.at[pl.ds(pl.multiple_of(c * hr, 8), hr), :]
        a = _remote(mine_i, mine_i, send_sems.at[0], recv_sems.at[0], sibling)
        b = _remote(mine_o, mine_o, send_sems.at[1], recv_sems.at[1], sibling)
        a.start()
        b.start()
        a.wait_send()
        b.wait_send()
        theirs_i = gi_ref.at[pl.ds(pl.multiple_of((1 - c) * hd, 8), hd), :]
        theirs_o = go_ref.at[pl.ds(pl.multiple_of((1 - c) * hr, 8), hr), :]
        _remote(theirs_i, theirs_i, send_sems.at[0], recv_sems.at[0], sibling).wait_recv()
        _remote(theirs_o, theirs_o, send_sems.at[1], recv_sems.at[1], sibling).wait_recv()

    return pl.pallas_call(
        body, name="share_halves",
        in_specs=[HBM_SPEC, HBM_SPEC], out_specs=[HBM_SPEC, HBM_SPEC],
        out_shape=[jax.ShapeDtypeStruct(g_in.shape, F32), jax.ShapeDtypeStruct(g_out.shape, F32)],
        input_output_aliases={0: 0, 1: 1},
        scratch_shapes=[pltpu.SemaphoreType.DMA((2,)), pltpu.SemaphoreType.DMA((2,))],
    )(g_in, g_out)


def _adamw(w, g, m, v, name):
    rows, cols = w.shape
    streams = 8
    fit = (VMEM_LIMIT_BYTES // 2) // (streams * 2 * cols * 4)
    tr = min(rows, 1 << (fit.bit_length() - 1))

    def body(w_ref, g_ref, m_ref, v_ref, d_ref, mo_ref, vo_ref, go_ref):
        gv = g_ref[...]
        delta, m_new, v_new = _adamw_math(w_ref[...], gv, m_ref[...], v_ref[...])
        d_ref[...] = delta
        mo_ref[...] = m_new
        vo_ref[...] = v_new
        go_ref[...] = gv

    spec = pl.BlockSpec((tr, cols), lambda i: (i, 0))
    shape = jax.ShapeDtypeStruct((rows, cols), F32)
    return pl.pallas_call(
        body, name=name, grid=(rows // tr,), in_specs=[spec] * 4, out_specs=[spec] * 4,
        out_shape=[shape] * 4, compiler_params=_params(dimension_semantics=("parallel",)),
    )(w, g, m, v)


SMALL_ROWS = 8


def _small_allreduce_adamw(part, w_pack, m_pack, v_pack):
    d = part.shape[1]

    def body(part_ref, w_ref, m_ref, v_ref, g_ref, d_ref, mo_ref, vo_ref, buf_ref, send_sems, recv_sems):
        x, y, c, _ = _mesh_pos()
        me = 4 * x + 2 * y + c
        buf_ref[0] = part_ref[...]
        copies = []
        for r in range(1, 8):
            rx, ry, rc = (r >> 2) & 1, (r >> 1) & 1, r & 1
            peer = (x ^ rx, y ^ ry, c ^ rc)
            copies.append(_remote(buf_ref.at[0], buf_ref.at[r], send_sems.at[r - 1], recv_sems.at[r - 1], peer))
        for cp in copies:
            cp.start()
        for cp in copies:
            cp.wait()
        total = buf_ref[me]
        for s in range(1, 8):
            total = total + buf_ref[s ^ me]
        w = w_ref[...]
        row = lax.broadcasted_iota(jnp.int32, (SMALL_ROWS, d), 0)
        l0, l1 = w[3:4], w[4:5]
        mx = jnp.maximum(l0, l1)
        e0, e1 = jnp.exp(l0 - mx), jnp.exp(l1 - mx)
        lb = e0 / (e0 + e1)
        g_l0 = total[3:4] * lb * (1.0 - lb)
        grads = jnp.where(row == 3, g_l0, jnp.where(row == 4, -g_l0, total))
        g_ref[...] = grads
        delta, m_new, v_new = _adamw_math(w, grads, m_ref[...], v_ref[...])
        d_ref[...] = delta
        mo_ref[...] = m_new
        vo_ref[...] = v_new

    vm = pl.BlockSpec(memory_space=pltpu.VMEM)
    shape = jax.ShapeDtypeStruct((SMALL_ROWS, d), F32)
    return pl.pallas_call(
        body, name="small_allreduce_adamw",
        in_specs=[vm] * 4, out_specs=[vm] * 4, out_shape=[shape] * 4,
        scratch_shapes=[pltpu.VMEM((8, SMALL_ROWS, d), F32), pltpu.SemaphoreType.DMA((7,)), pltpu.SemaphoreType.DMA((7,))],
    )(part, w_pack, m_pack, v_pack)


def _pack_small(d, pre, post, rnn, lb, sink, extra=None):
    rows = [pre, post, rnn, lb[0:1], lb[1:2],
            jnp.pad(sink, ((0, 0), (0, d - sink.shape[1]))),
            jnp.zeros((1, d), F32) if extra is None else extra,
            jnp.zeros((1, d), F32)]
    return jnp.concatenate(rows, axis=0)


def _unpack_small(p, n_sink):
    return dict(pre=p[0:1], post=p[1:2], rnn=p[2:3], lb=p[3:5], sink=p[5:6, :n_sink])


def kernel(x, w_in, attn_sinks, lb_logits, rnn_norm, w_out, pre_norm, post_norm, loss_target, m_w_in, m_attn_sinks, m_lb_logits, m_rnn_norm, m_w_out, m_pre_norm, m_post_norm, v_w_in, v_attn_sinks, v_lb_logits, v_rnn_norm, v_w_out, v_pre_norm, v_post_norm):
    t, d = x.shape[1], x.shape[2]
    sc = w_in.shape[2]
    n_sink = attn_sinks.shape[1]
    w_in2, w_out2 = w_in[0], w_out[0]

    w_in_part = _cast_into_gathered(w_in2, "cast_w_in", 1)
    w_out_part = _cast_into_gathered(w_out2, "cast_w_out", 0)
    loss_part, grad_x, (p_in, r_in), (p_out, r_out), small = _local_grads(
        x[0], loss_target[0], w_in_part, w_out_part, attn_sinks, lb_logits, rnn_norm, pre_norm, post_norm, sc)
    g_w_in, g_w_out = _share_halves(_chip_sum_in(p_in, r_in, sc), _chip_sum_out(p_out, r_out))
    d_w_in, nm_w_in, nv_w_in, g_w_in = _adamw(w_in2, g_w_in, m_w_in[0], v_w_in[0], "adamw_w_in")
    d_w_out, nm_w_out, nv_w_out, g_w_out = _adamw(w_out2, g_w_out, m_w_out[0], v_w_out[0], "adamw_w_out")

    lb_part = jnp.concatenate([small["lb"], jnp.zeros_like(small["lb"])], axis=0)
    loss_row = jnp.pad(loss_part[:, :1], ((0, 0), (0, d - 1)))
    part = _pack_small(d, small["pre"], small["post"], small["rnn"], lb_part, small["sink"], loss_row)
    w_pack = _pack_small(d, pre_norm, post_norm, rnn_norm, lb_logits, attn_sinks)
    m_pack = _pack_small(d, m_pre_norm, m_post_norm, m_rnn_norm, m_lb_logits, m_attn_sinks)
    v_pack = _pack_small(d, v_pre_norm, v_post_norm, v_rnn_norm, v_lb_logits, v_attn_sinks)
    g_pack, d_pack, nm_pack, nv_pack = _small_allreduce_adamw(part, w_pack, m_pack, v_pack)
    loss = g_pack[6, 0]
    g, dl, nm, nv = (_unpack_small(p, n_sink) for p in (g_pack, d_pack, nm_pack, nv_pack))

    def ordered(w_in_leaf, w_out_leaf, s):
        return (w_in_leaf[None], s["sink"], s["lb"], s["rnn"], w_out_leaf[None], s["pre"], s["post"])

    return (loss, grad_x[None],
            *ordered(g_w_in, g_w_out, g), *ordered(d_w_in, d_w_out, dl),
            *ordered(nm_w_in, nm_w_out, nm), *ordered(nv_w_in, nv_w_out, nv))
```

```python
import numpy as np
import jax
import jax.numpy as jnp
from jax import lax
from jax.experimental import pallas as pl
from jax.experimental.pallas import tpu as pltpu

F32 = jnp.float32
BF16 = jnp.bfloat16
MESH = pl.DeviceIdType.MESH

NORM_EPS = 1e-6
ATTN_HEAD_DIM = 64
GQA_GROUP = 8
WINDOW = 128
ATTN_STEP_BLOCKS = 4
ATTN_STEP_ROWS = ATTN_STEP_BLOCKS * WINDOW
RNN_HEAD_DIM = 128
CHUNK = 64
HALF_CHUNK = CHUNK // 2
ATTN_SCALE = ATTN_HEAD_DIM ** -0.5

ADAM_LR = 0.001
ADAM_B1 = 0.9
ADAM_B2 = 0.999
ADAM_EPS = 1e-08
ADAM_WD = 0.01
ADAM_STEP = 10

LANES = 128
COL_TILE = 512
RNN_GROUP_HEADS = 4
RNN_STEP_ROWS = 512
ATTN_GROUP_LANES = 1024
N_CHIPS = 4
VMEM_LIMIT_BYTES = 56 * 1024 * 1024
NEG_BIG = -1e30


def _params(**kw):
    return pltpu.CompilerParams(vmem_limit_bytes=VMEM_LIMIT_BYTES, **kw)


def _sigmoid(x):
    return 1.0 / (1.0 + jnp.exp(-x))


def _dot(a, b):
    return jnp.dot(a, b, preferred_element_type=F32)


def _dot_nt(a, b):
    return lax.dot_general(a, b, (((1,), (1,)), ((), ())), preferred_element_type=F32)


def _dot_tn(a, b):
    return lax.dot_general(a, b, (((0,), (0,)), ((), ())), preferred_element_type=F32)


def _tri_dot(tri_bf16, x):
    hi = x.astype(BF16)
    lo = (x - hi.astype(F32)).astype(BF16)
    return _dot(tri_bf16, hi) + _dot(tri_bf16, lo)


def _layout(d_model):
    d = d_model
    dkv = d // GQA_GROUP
    orig = dict(aq=0, ak=d, av=d + dkv, ag=d + 2 * dkv)
    base = d + 2 * dkv + d
    orig.update(rq=base, rf=base + d, ri=base + 2 * d, rg=base + 3 * d)
    group_w = RNN_GROUP_HEADS * RNN_HEAD_DIM
    cols = []
    for hg in range(d // group_w):
        for seg in ("rq", "rf", "ri", "rg"):
            cols.append((orig[seg] + hg * group_w, group_w))
    for m in range(d // ATTN_GROUP_LANES):
        for seg in ("aq", "ag"):
            cols.append((orig[seg] + m * ATTN_GROUP_LANES, ATTN_GROUP_LANES))
    cols.append((orig["ak"], dkv))
    cols.append((orig["av"], dkv))
    units = []
    for start, width in cols:
        assert start % LANES == 0 and width % LANES == 0
        units += [start + u for u in range(0, width, LANES)]
    per = COL_TILE // LANES
    assert len(units) % per == 0
    tiles = []
    for t in range(len(units) // per):
        run = units[t * per:(t + 1) * per]
        assert run[0] % COL_TILE == 0 and all(run[i] == run[0] + i * LANES for i in range(per))
        tiles.append(run[0] // COL_TILE)
    return dict(a_off=4 * d, k_off=6 * d, v_off=6 * d + dkv, total=6 * d + 2 * dkv,
                perm=np.asarray(tiles, np.int32))


def _chip_index():
    return jnp.reshape(2 * lax.axis_index("x") + lax.axis_index("y"), (1,)).astype(jnp.int32)


def _cast_into_gathered(a, name, axis):
    rows, cols = a.shape
    tr = min(rows, 512)
    nblk = rows // tr

    def body(me_ref, a_ref, o_ref):
        del me_ref
        o_ref[...] = a_ref[...].astype(BF16)

    if axis == 1:
        out_spec = pl.BlockSpec((tr, cols), lambda i, me: (i, me[0]))
        shape = (rows, N_CHIPS * cols)
    else:
        out_spec = pl.BlockSpec((tr, cols), lambda i, me: (me[0] * nblk + i, 0))
        shape = (N_CHIPS * rows, cols)
    gs = pltpu.PrefetchScalarGridSpec(num_scalar_prefetch=1, grid=(nblk,),
                                      in_specs=[pl.BlockSpec((tr, cols), lambda i, me: (i, 0))], out_specs=out_spec)
    return pl.pallas_call(
        body, name=name, grid_spec=gs, out_shape=jax.ShapeDtypeStruct(shape, BF16),
        compiler_params=_params(dimension_semantics=("parallel",)),
    )(_chip_index(), a)


def _prenorm_fwd(x, gain):
    t, d = x.shape
    tm = min(t, 256)

    def body(x_ref, g_ref, h_ref):
        xv = x_ref[...]
        r = lax.rsqrt(jnp.mean(xv * xv, axis=-1, keepdims=True) + NORM_EPS)
        h_ref[...] = ((xv * r) * g_ref[...]).astype(BF16)

    return pl.pallas_call(
        body, name="prenorm_fwd", grid=(t // tm,),
        in_specs=[pl.BlockSpec((tm, d), lambda i: (i, 0)), pl.BlockSpec((1, d), lambda i: (0, 0))],
        out_specs=pl.BlockSpec((tm, d), lambda i: (i, 0)),
        out_shape=jax.ShapeDtypeStruct((t, d), BF16),
        compiler_params=_params(dimension_semantics=("parallel",)),
    )(x, gain)


def _post_loss(x, y, target, gain):
    t, d = x.shape
    tm = min(t, 256)
    inv_d = 1.0 / d

    def body(x_ref, y_ref, t_ref, g_ref, dy_ref, dz_ref, gp_ref, loss_ref):
        i = pl.program_id(0)
        yv = y_ref[...]
        gain_v = g_ref[...]
        r = lax.rsqrt(jnp.mean(yv * yv, axis=-1, keepdims=True) + NORM_EPS)
        n = yv * r
        e = (x_ref[...] + n * gain_v) - t_ref[...]
        dz = e * inv_d
        dn = dz * gain_v
        dy = r * (dn - n * jnp.mean(dn * n, axis=-1, keepdims=True))
        dy_ref[...] = dy.astype(BF16)
        dz_ref[...] = dz

        @pl.when(i == 0)
        def _():
            gp_ref[...] = jnp.zeros_like(gp_ref)
            loss_ref[...] = jnp.zeros_like(loss_ref)

        gp_ref[...] += jnp.sum(dz * n, axis=0, keepdims=True)
        row = jnp.sum(e * e, axis=-1, keepdims=True)
        loss_ref[...] += jnp.full(loss_ref.shape, 0.5 * inv_d * jnp.sum(row), F32)

    row_spec = pl.BlockSpec((tm, d), lambda i: (i, 0))
    vec_spec = pl.BlockSpec((1, d), lambda i: (0, 0))
    return pl.pallas_call(
        body, name="post_loss", grid=(t // tm,),
        in_specs=[row_spec, row_spec, row_spec, vec_spec],
        out_specs=[row_spec, row_spec, vec_spec, pl.BlockSpec((1, LANES), lambda i: (0, 0))],
        out_shape=[jax.ShapeDtypeStruct((t, d), BF16), jax.ShapeDtypeStruct((t, d), F32),
                   jax.ShapeDtypeStruct((1, d), F32), jax.ShapeDtypeStruct((1, LANES), F32)],
        compiler_params=_params(dimension_semantics=("arbitrary",)),
    )(x, y, target, gain)


def _prenorm_bwd(x, dh, dz, gain):
    t, d = x.shape
    tm = min(t, 256)

    def body(x_ref, dh_ref, dz_ref, g_ref, gx_ref, gp_ref):
        i = pl.program_id(0)
        xv = x_ref[...]
        r = lax.rsqrt(jnp.mean(xv * xv, axis=-1, keepdims=True) + NORM_EPS)
        n = xv * r
        dhv = dh_ref[...]
        dn = dhv * g_ref[...]
        gx_ref[...] = dz_ref[...] + r * (dn - n * jnp.mean(dn * n, axis=-1, keepdims=True))

        @pl.when(i == 0)
        def _():
            gp_ref[...] = jnp.zeros_like(gp_ref)

        gp_ref[...] += jnp.sum(dhv * n, axis=0, keepdims=True)

    row_spec = pl.BlockSpec((tm, d), lambda i: (i, 0))
    vec_spec = pl.BlockSpec((1, d), lambda i: (0, 0))
    return pl.pallas_call(
        body, name="prenorm_bwd", grid=(t // tm,),
        in_specs=[row_spec, row_spec, row_spec, vec_spec],
        out_specs=[row_spec, vec_spec],
        out_shape=[jax.ShapeDtypeStruct((t, d), F32), jax.ShapeDtypeStruct((1, d), F32)],
        compiler_params=_params(dimension_semantics=("arbitrary",)),
    )(x, dh, dz, gain)


class _Rider:
    def __init__(self, operands, out_shapes, aliases, n_sems, stages):
        self.operands = tuple(operands)
        self.out_shapes = tuple(out_shapes)
        self.aliases = dict(aliases)
        self.n_sems = n_sems
        self.stages = stages

    def scratch(self):
        return [pltpu.SemaphoreType.DMA((self.n_sems,)), pltpu.SemaphoreType.DMA((self.n_sems,))]

    def emit(self, step, n_steps, in_refs, out_refs, send_sems, recv_sems):
        for frac, fn in self.stages(in_refs, out_refs, send_sems, recv_sems):
            at = min(n_steps - 1, int(frac * (n_steps - 1) + 0.5))
            pl.when(step == at)(fn)


def _matmul(name, a, b, *, out_shape, grid, a_spec, b_spec, o_spec, nt=False, ta=False, perm=None, rider=None):
    nk = grid[2]
    n_steps = grid[0] * grid[1] * grid[2]
    tm, tn = [s for s in o_spec.block_shape if s is not None][-2:]
    acc_in_out = out_shape.dtype == F32
    n_pre = 0 if perm is None else 1
    n_rin = 0 if rider is None else len(rider.operands)
    n_rout = 0 if rider is None else len(rider.out_shapes)
    use_acc = not (nk == 1 or acc_in_out)

    def body(*refs):
        refs = refs[n_pre:]
        a_ref, b_ref = refs[:2]
        rin = refs[2:2 + n_rin]
        o_ref = refs[2 + n_rin]
        rout = refs[3 + n_rin:3 + n_rin + n_rout]
        scratch_refs = refs[3 + n_rin + n_rout:]
        if rider is not None:
            step = (pl.program_id(0) * grid[1] + pl.program_id(1)) * grid[2] + pl.program_id(2)
            rider.emit(step, n_steps, rin, rout, scratch_refs[-2], scratch_refs[-1])
        def product():
            if ta:
                return _dot_tn(a_ref[...], b_ref[...])
            return _dot_nt(a_ref[...], b_ref[...]) if nt else _dot(a_ref[...], b_ref[...])

        if nk == 1:
            o_ref[...] = product().astype(o_ref.dtype)
            return
        acc_ref = o_ref if acc_in_out else scratch_refs[0]
        k = pl.program_id(2)

        @pl.when(k == 0)
        def _():
            acc_ref[...] = jnp.zeros_like(acc_ref)

        acc_ref[...] += product()

        if not acc_in_out:
            @pl.when(k == nk - 1)
            def _():
                o_ref[...] = acc_ref[...].astype(o_ref.dtype)

    scratch = [pltpu.VMEM((tm, tn), F32)] if use_acc else []
    in_specs = [a_spec, b_spec] + [HBM_SPEC] * n_rin
    out_specs = [o_spec] + [HBM_SPEC] * n_rout
    out_shapes = [out_shape]
    operands = [a, b]
    aliases = {}
    sem = ("parallel", "parallel", "arbitrary")
    if rider is not None:
        scratch += rider.scratch()
        out_shapes += list(rider.out_shapes)
        operands += list(rider.operands)
        aliases = {n_pre + 2 + i: 1 + o for i, o in rider.aliases.items()}
        sem = ("arbitrary", "arbitrary", "arbitrary")
    cp = _params(dimension_semantics=sem)
    if perm is None:
        return pl.pallas_call(body, name=name, grid=grid, in_specs=in_specs, out_specs=out_specs,
                              out_shape=out_shapes, scratch_shapes=scratch, input_output_aliases=aliases,
                              compiler_params=cp)(*operands)
    gs = pltpu.PrefetchScalarGridSpec(num_scalar_prefetch=1, grid=grid, in_specs=in_specs,
                                      out_specs=out_specs, scratch_shapes=scratch)
    return pl.pallas_call(body, name=name, grid_spec=gs, out_shape=out_shapes, input_output_aliases=aliases,
                          compiler_params=cp)(jnp.asarray(perm), *operands)


def _proj_mm(h, w_full, perm, rider=None):
    t, d = h.shape
    n_tiles = len(perm)
    tm = min(t, 1024)
    return _matmul(
        "proj_mm", h, w_full, perm=perm, rider=rider, grid=(t // tm, n_tiles, 1),
        out_shape=jax.ShapeDtypeStruct((t, n_tiles * COL_TILE), F32),
        a_spec=pl.BlockSpec((tm, d), lambda i, j, k, p: (i, 0)),
        b_spec=pl.BlockSpec((d, COL_TILE), lambda i, j, k, p: (0, p[j])),
        o_spec=pl.BlockSpec((tm, COL_TILE), lambda i, j, k, p: (i, j)))


def _proj_gather_mm(h, wi_part, perm, sc):
    t, d = h.shape
    n_tiles = len(perm)
    tm = min(t, 1024)
    n_i = t // tm
    hd = d // 2
    nf = sc // COL_TILE
    rem = sc - nf * COL_TILE
    assert 2 * rem == COL_TILE and n_tiles == N_CHIPS * nf + 2
    n_chunks = next(q for q in (4, 3, 2, 1) if nf % q == 0)
    tpc = nf // n_chunks
    n_kinds = n_chunks + 1
    n_sems = 3 * n_kinds
    rem_at = nf + 3 * n_chunks * tpc

    def first_full(chip):
        return (chip * sc + (rem if chip % 2 else 0)) // COL_TILE

    inverse = np.argsort(perm)
    table = np.zeros((N_CHIPS, 2, n_tiles), np.int32)
    for chip in range(N_CHIPS):
        seq = list(range(first_full(chip), first_full(chip) + nf))
        for q in range(n_chunks):
            for src in (chip ^ 2, chip ^ 1, chip ^ 3):
                seq += list(range(first_full(src) + q * tpc, first_full(src) + (q + 1) * tpc))
        seq += [first_full(chip - chip % 2) + nf, first_full((chip ^ 2) - chip % 2) + nf]
        assert sorted(seq) == list(range(n_tiles)), seq
        table[chip, 0] = inverse[seq]
        table[chip, 1] = seq
    me_chip = 2 * lax.axis_index("x") + lax.axis_index("y")
    tab = lax.dynamic_index_in_dim(jnp.asarray(table), me_chip, 0, keepdims=False)

    def body(tab_ref, h_hbm, wi_in, proj_ref, full, hbuf, bbuf, local_sems, send_sems, recv_sems):
        del wi_in
        jj = pl.program_id(0)
        i = pl.program_id(1)
        x, y, c, chips = _mesh_pos()
        sibling = (x, y, 1 - c)

        def piece(chip, half, kind):
            odd = chip % 2
            if kind == n_chunks:
                start, width = chip * sc + (1 - odd) * (nf * COL_TILE), rem
            else:
                start, width = chip * sc + odd * rem + kind * (tpc * COL_TILE), tpc * COL_TILE
            return full.at[pl.ds(half * hd, hd), pl.ds(pl.multiple_of(start, LANES), width)]

        def ici(j, kind):
            mine = piece(2 * x + y, c, kind)
            k = j * n_kinds + kind
            return _remote(mine, mine, send_sems.at[k], recv_sems.at[k], (chips[j][0], chips[j][1], c))

        def landed(j, kind):
            blk = piece(2 * chips[j][0] + chips[j][1], c, kind)
            k = j * n_kinds + kind
            return _remote(blk, blk, send_sems.at[k], recv_sems.at[k], sibling)

        def passed(j, kind, half):
            blk = piece(2 * chips[j][0] + chips[j][1], half, kind)
            k = n_sems + j * n_kinds + kind
            return _remote(blk, blk, send_sems.at[k], recv_sems.at[k], sibling)

        def fetch(pos, slot):
            col = pl.multiple_of(tab_ref[1, pos] * COL_TILE, LANES)
            return pltpu.make_async_copy(full.at[:, pl.ds(col, COL_TILE)], bbuf.at[slot], local_sems.at[slot])

        def load_h():
            return pltpu.make_async_copy(h_hbm, hbuf, local_sems.at[2])

        def relay(j, kind):
            landed(j, kind).wait_recv()
            passed(j, kind, c).start()

        @pl.when(i == 0)
        def _():
            @pl.when(jj == 0)
            def _():
                load_h().start()
                for kind in range(n_kinds):
                    for j in range(3):
                        ici(j, kind).start()
                fetch(0, 0).start()
                load_h().wait()

            for n in range(3 * n_chunks):
                at = nf + n * tpc
                pl.when(jj == at - 2)(lambda n=n: relay(n % 3, n // 3))
                pl.when(jj == at - 1)(lambda n=n: passed(n % 3, n // 3, 1 - c).wait_recv())

            @pl.when(jj == rem_at - 2)
            def _():
                for j in range(3):
                    relay(j, n_chunks)

            @pl.when(jj == rem_at - 1)
            def _():
                for j in range(3):
                    passed(j, n_chunks, 1 - c).wait_recv()

            @pl.when(jj + 1 < n_tiles)
            def _():
                fetch(jj + 1, (jj + 1) % 2).start()

            fetch(jj, jj % 2).wait()

            @pl.when(jj == n_tiles - 1)
            def _():
                for kind in range(n_kinds):
                    for j in range(3):
                        ici(j, kind).wait_send()
                        passed(j, kind, c).wait_send()

        rows = pl.ds(pl.multiple_of(i * tm, tm), tm)
        proj_ref[...] = _dot(hbuf[rows, :], bbuf[jj % 2])

    gs = pltpu.PrefetchScalarGridSpec(
        num_scalar_prefetch=1, grid=(n_tiles, n_i),
        in_specs=[HBM_SPEC, HBM_SPEC],
        out_specs=[pl.BlockSpec((tm, COL_TILE), lambda jj, i, tb: (i, tb[0, jj])), HBM_SPEC],
        scratch_shapes=[pltpu.VMEM((t, d), BF16), pltpu.VMEM((2, d, COL_TILE), BF16), pltpu.SemaphoreType.DMA((3,)),
                        pltpu.SemaphoreType.DMA((2 * n_sems,)), pltpu.SemaphoreType.DMA((2 * n_sems,))])
    return pl.pallas_call(
        body, name="proj_gather_mm", grid_spec=gs,
        out_shape=[jax.ShapeDtypeStruct((t, n_tiles * COL_TILE), F32), jax.ShapeDtypeStruct(wi_part.shape, BF16)],
        input_output_aliases={2: 1},
        compiler_params=_params(dimension_semantics=("arbitrary", "arbitrary")),
    )(tab, h, wi_part)


def _gw_in_mm(name, h, dproj, perm, half, rider=None):
    t, d = h.shape
    n_tiles = len(perm)
    hd = d // 2
    tm = min(hd, 1024)
    per_half = hd // tm
    table = jnp.concatenate([jnp.asarray(perm), jnp.reshape(half, (1,)).astype(jnp.int32)])
    return _matmul(
        name, h, dproj, perm=table, rider=rider, ta=True, grid=(per_half, n_tiles, 1),
        out_shape=jax.ShapeDtypeStruct((hd, n_tiles * COL_TILE), BF16),
        a_spec=pl.BlockSpec((t, tm), lambda i, j, k, p: (0, p[n_tiles] * per_half + i)),
        b_spec=pl.BlockSpec((t, COL_TILE), lambda i, j, k, p: (0, j)),
        o_spec=pl.BlockSpec((tm, COL_TILE), lambda i, j, k, p: (i, p[j])))


def _dh_mm(dproj, w_full, perm, rider=None):
    t = dproj.shape[0]
    d = w_full.shape[0]
    n_tiles = len(perm)
    tm = min(t, 2048)
    tn = min(d, 2048)
    return _matmul(
        "dh_mm", dproj, w_full, perm=perm, rider=rider, nt=True, grid=(t // tm, d // tn, n_tiles),
        out_shape=jax.ShapeDtypeStruct((t, d), F32),
        a_spec=pl.BlockSpec((tm, COL_TILE), lambda i, j, k, p: (i, k)),
        b_spec=pl.BlockSpec((tn, COL_TILE), lambda i, j, k, p: (j, p[k])),
        o_spec=pl.BlockSpec((tm, tn), lambda i, j, k, p: (i, j)))


def _out_mm(mixed, w_out_full):
    t, dm = mixed.shape
    d = w_out_full.shape[1]
    tm = min(t, 1024)
    tn = min(d, 512)
    tk = min(dm, 4096)
    return _matmul(
        "out_mm", mixed, w_out_full, grid=(t // tm, d // tn, dm // tk),
        out_shape=jax.ShapeDtypeStruct((t, d), F32),
        a_spec=pl.BlockSpec((tm, tk), lambda i, j, k: (i, k)),
        b_spec=pl.BlockSpec((tk, tn), lambda i, j, k: (k, j)),
        o_spec=pl.BlockSpec((tm, tn), lambda i, j, k: (i, j)))[0]


def _dmixed_mm(dy, w_out_full, rider=None):
    t, d = dy.shape
    dm = w_out_full.shape[0]
    tm = min(t, 1024)
    tn = min(dm, 1024)
    return _matmul(
        "dmixed_mm", dy, w_out_full, nt=True, rider=rider, grid=(t // tm, dm // tn, 1),
        out_shape=jax.ShapeDtypeStruct((t, dm), F32),
        a_spec=pl.BlockSpec((tm, d), lambda i, j, k: (i, 0)),
        b_spec=pl.BlockSpec((tn, d), lambda i, j, k: (j, 0)),
        o_spec=pl.BlockSpec((tm, tn), lambda i, j, k: (i, j)))


def _gw_out_mm(mixed, dy):
    t, dm = mixed.shape
    d = dy.shape[1]
    hr = dm // (2 * N_CHIPS)
    tn = min(d, 1024)
    return _matmul(
        "gw_out_mm", mixed, dy, ta=True, grid=(dm // hr, d // tn, 1),
        out_shape=jax.ShapeDtypeStruct((2, N_CHIPS, hr, d), BF16),
        a_spec=pl.BlockSpec((t, hr), lambda i, j, k: (0, i)),
        b_spec=pl.BlockSpec((t, tn), lambda i, j, k: (0, j)),
        o_spec=pl.BlockSpec((None, None, hr, tn), lambda i, j, k: (i % 2, i // 2, 0, j)))[0]


def _lane_half():
    return lax.broadcasted_iota(jnp.int32, (WINDOW, LANES), 1) // ATTN_HEAD_DIM


def _dup_kv(tile, kh):
    return jnp.where(_lane_half() == kh, tile, pltpu.roll(tile, ATTN_HEAD_DIM, 1))


def _stack_heads(tiles, kh):
    half = _lane_half()
    pieces = []
    for g in range(GQA_GROUP):
        pieces.append(jnp.where(half == g % 2, tiles[4 * kh + g // 2], 0.0))
    return jnp.concatenate(pieces, axis=0)


def _unstack_heads(stacked):
    half = _lane_half()
    out = []
    for j in range(GQA_GROUP // 2):
        a = stacked[(2 * j) * WINDOW:(2 * j + 1) * WINDOW]
        b = stacked[(2 * j + 1) * WINDOW:(2 * j + 2) * WINDOW]
        out.append(jnp.where(half == 0, a, b))
    return out


def _attn_probs(qs, kcat, sink_col, n):
    rows = GQA_GROUP * WINDOW
    s = _dot_nt(qs, kcat)
    qi = lax.broadcasted_iota(jnp.int32, (rows, 2 * WINDOW), 0) % WINDOW
    kj = lax.broadcasted_iota(jnp.int32, (rows, 2 * WINDOW), 1)
    first_key = WINDOW * (1 - jnp.minimum(n, 1))
    valid = (kj > qi) & (kj <= qi + WINDOW) & (kj >= first_key)
    s = jnp.where(valid, s, NEG_BIG)
    mx = jnp.maximum(jnp.max(s, axis=-1, keepdims=True), sink_col)
    p = jnp.exp(s - mx)
    p_sink = jnp.exp(sink_col - mx)
    inv = 1.0 / (jnp.sum(p, axis=-1, keepdims=True) + p_sink)
    return p * inv, p_sink * inv


def _attn_operands(sink_ref, q_tiles, k_prev, k_cur, v_prev, v_cur, m, kh):
    qs = _stack_heads([qt * ATTN_SCALE for qt in q_tiles], kh).astype(BF16)
    kcat = jnp.concatenate([_dup_kv(k_prev, kh), _dup_kv(k_cur, kh)], axis=0).astype(BF16)
    vcat = jnp.concatenate([_dup_kv(v_prev, kh), _dup_kv(v_cur, kh)], axis=0).astype(BF16)
    heads_per_group = ATTN_GROUP_LANES // ATTN_HEAD_DIM
    sink_col = jnp.concatenate(
        [jnp.full((WINDOW, 1), sink_ref[0, m * heads_per_group + kh * GQA_GROUP + g], F32)
         for g in range(GQA_GROUP)], axis=0)
    return qs, kcat, vcat, sink_col


def _attn_specs(lay, d):
    a_blk = lay["a_off"] // (2 * ATTN_GROUP_LANES)
    k_blk = lay["k_off"] // LANES
    v_blk = lay["v_off"] // LANES
    before = lambda n: jnp.maximum(ATTN_STEP_BLOCKS * n - 1, 0)
    qg = pl.BlockSpec((ATTN_STEP_ROWS, 2 * ATTN_GROUP_LANES), lambda m, n: (n, a_blk + m))
    kp = pl.BlockSpec((WINDOW, LANES), lambda m, n: (before(n), k_blk + m))
    kc = pl.BlockSpec((ATTN_STEP_ROWS, LANES), lambda m, n: (n, k_blk + m))
    vp = pl.BlockSpec((WINDOW, LANES), lambda m, n: (before(n), v_blk + m))
    vc = pl.BlockSpec((ATTN_STEP_ROWS, LANES), lambda m, n: (n, v_blk + m))
    return qg, kp, kc, vp, vc


def _block_rows(b):
    return slice(b * WINDOW, (b + 1) * WINDOW)


def _kv_tiles(prev_ref, cur_ref, b):
    prev = prev_ref[...] if b == 0 else cur_ref[_block_rows(b - 1), :]
    return prev, cur_ref[_block_rows(b), :]


def _attn_fwd(proj, sinks, lay, d, rider=None):
    t = proj.shape[0]
    n_groups = d // ATTN_GROUP_LANES
    n_blocks = t // ATTN_STEP_ROWS
    pairs = ATTN_GROUP_LANES // LANES
    n_rin = 0 if rider is None else len(rider.operands)
    n_rout = 0 if rider is None else len(rider.out_shapes)

    def body(*refs):
        sink_ref, qg_ref, kp_ref, kc_ref, vp_ref, vc_ref = refs[:6]
        rin = refs[6:6 + n_rin]
        mix_ref, o_ref = refs[6 + n_rin:8 + n_rin]
        rout = refs[8 + n_rin:8 + n_rin + n_rout]
        m = pl.program_id(0)
        n = pl.program_id(1)
        if rider is not None:
            rider.emit(m * n_blocks + n, n_groups * n_blocks, rin, rout, refs[-2], refs[-1])
        for b in range(ATTN_STEP_BLOCKS):
            rows = _block_rows(b)
            k_prev, k_cur = _kv_tiles(kp_ref, kc_ref, b)
            v_prev, v_cur = _kv_tiles(vp_ref, vc_ref, b)
            q_tiles = [qg_ref[rows, p * LANES:(p + 1) * LANES] for p in range(pairs)]
            for kh in range(2):
                qs, kcat, vcat, sink_col = _attn_operands(sink_ref, q_tiles, k_prev, k_cur, v_prev, v_cur, m, kh)
                probs, _ = _attn_probs(qs, kcat, sink_col, ATTN_STEP_BLOCKS * n + b)
                out = _dot(probs.astype(BF16), vcat)
                for j, tile in enumerate(_unstack_heads(out)):
                    p = 4 * kh + j
                    lanes = slice(p * LANES, (p + 1) * LANES)
                    gate = qg_ref[rows, ATTN_GROUP_LANES + p * LANES:ATTN_GROUP_LANES + (p + 1) * LANES]
                    o_ref[rows, lanes] = tile
                    mix_ref[rows, lanes] = (tile * (gate * _sigmoid(gate))).astype(BF16)

    qg, kp, kc, vp, vc = _attn_specs(lay, d)
    out_blk = pl.BlockSpec((ATTN_STEP_ROWS, ATTN_GROUP_LANES), lambda m, n: (n, m))
    out_shapes = [jax.ShapeDtypeStruct((t, 2 * d), BF16), jax.ShapeDtypeStruct((t, d), F32)]
    operands = [sinks, proj, proj, proj, proj, proj]
    scratch, aliases, sem = [], {}, ("parallel", "parallel")
    if rider is not None:
        scratch = rider.scratch()
        out_shapes += list(rider.out_shapes)
        operands += list(rider.operands)
        aliases = {6 + i: 2 + o for i, o in rider.aliases.items()}
        sem = ("arbitrary", "arbitrary")
    return pl.pallas_call(
        body, name="attn_fwd", grid=(n_groups, n_blocks),
        in_specs=[pl.BlockSpec(memory_space=pltpu.SMEM), qg, kp, kc, vp, vc] + [HBM_SPEC] * n_rin,
        out_specs=[out_blk, out_blk] + [HBM_SPEC] * n_rout,
        out_shape=out_shapes, scratch_shapes=scratch, input_output_aliases=aliases,
        compiler_params=_params(dimension_semantics=sem),
    )(*operands)


def _attn_bwd(proj, sinks, attn_o, dmixed, dproj, lay, d):
    t = proj.shape[0]
    n_groups = d // ATTN_GROUP_LANES
    pairs = ATTN_GROUP_LANES // LANES
    kv_w = n_groups * LANES

    def body(sink_ref, qg_ref, kp_ref, kc_ref, vp_ref, vc_ref, o_ref, dm_ref, dproj_hbm,
             dqg_ref, dkc_ref, dkp_ref, dvc_ref, dvp_ref, dsink_ref):
        del dproj_hbm
        m = pl.program_id(0)
        n = pl.program_id(1)
        half = _lane_half()
        sub = lax.broadcasted_iota(jnp.int32, (8, LANES), 0)
        lane = lax.broadcasted_iota(jnp.int32, (8, LANES), 1)
        dsink = jnp.zeros((8, LANES), F32)
        for b in range(ATTN_STEP_BLOCKS):
            rows = _block_rows(b)
            k_prev, k_cur = _kv_tiles(kp_ref, kc_ref, b)
            v_prev, v_cur = _kv_tiles(vp_ref, vc_ref, b)
            q_tiles = [qg_ref[rows, p * LANES:(p + 1) * LANES] for p in range(pairs)]
            do_tiles, o_tiles = [], []
            for p in range(pairs):
                lanes = slice(p * LANES, (p + 1) * LANES)
                gate_lanes = slice(ATTN_GROUP_LANES + p * LANES, ATTN_GROUP_LANES + (p + 1) * LANES)
                gate = qg_ref[rows, gate_lanes]
                sg = _sigmoid(gate)
                dmix = dm_ref[rows, lanes]
                ov = o_ref[rows, lanes]
                dqg_ref[rows, gate_lanes] = (dmix * ov * (sg * (1.0 + gate * (1.0 - sg)))).astype(BF16)
                do_tiles.append(dmix * (gate * sg))
                o_tiles.append(ov)

            dk_cur = dk_prev = dv_cur = dv_prev = jnp.zeros((WINDOW, LANES), F32)
            for kh in range(2):
                qs, kcat, vcat, sink_col = _attn_operands(sink_ref, q_tiles, k_prev, k_cur, v_prev, v_cur, m, kh)
                probs, p_sink = _attn_probs(qs, kcat, sink_col, ATTN_STEP_BLOCKS * n + b)
                dos = _stack_heads(do_tiles, kh)
                delta = jnp.sum(dos * _stack_heads(o_tiles, kh), axis=-1, keepdims=True)
                dos = dos.astype(BF16)
                dp = _dot_nt(dos, vcat)
                ds = (probs * (dp - delta)).astype(BF16)
                dv = _dot_tn(probs.astype(BF16), dos)
                dv = dv + pltpu.roll(dv, ATTN_HEAD_DIM, 1)
                dk = _dot_tn(ds, qs)
                dk = dk + pltpu.roll(dk, ATTN_HEAD_DIM, 1)
                dq = _dot(ds, kcat)
                for j, tile in enumerate(_unstack_heads(dq)):
                    p = 4 * kh + j
                    dqg_ref[rows, p * LANES:(p + 1) * LANES] = (tile * ATTN_SCALE).astype(BF16)
                dk_prev = jnp.where(half == kh, dk[:WINDOW], dk_prev)
                dk_cur = jnp.where(half == kh, dk[WINDOW:], dk_cur)
                dv_prev = jnp.where(half == kh, dv[:WINDOW], dv_prev)
                dv_cur = jnp.where(half == kh, dv[WINDOW:], dv_cur)
                sink_terms = p_sink * delta
                for g in range(GQA_GROUP):
                    val = -jnp.sum(sink_terms[g * WINDOW:(g + 1) * WINDOW])
                    dsink = dsink + jnp.where((sub == 0) & (lane == kh * GQA_GROUP + g), val, 0.0)
            dkc_ref[rows, :] = dk_cur
            dkp_ref[rows, :] = dk_prev
            dvc_ref[rows, :] = dv_cur
            dvp_ref[rows, :] = dv_prev

        @pl.when(n == 0)
        def _():
            dsink_ref[...] = jnp.zeros_like(dsink_ref)

        dsink_ref[...] += dsink

    qg, kp, kc, vp, vc = _attn_specs(lay, d)
    a_blk = lay["a_off"] // (2 * ATTN_GROUP_LANES)
    grp = pl.BlockSpec((ATTN_STEP_ROWS, ATTN_GROUP_LANES), lambda m, n: (n, m))
    kv_blk = pl.BlockSpec((ATTN_STEP_ROWS, LANES), lambda m, n: (n, m))
    kv_shape = jax.ShapeDtypeStruct((t, kv_w), F32)
    outs = pl.pallas_call(
        body, name="attn_bwd", grid=(n_groups, t // ATTN_STEP_ROWS),
        in_specs=[pl.BlockSpec(memory_space=pltpu.SMEM), qg, kp, kc, vp, vc, grp, grp,
                  pl.BlockSpec(memory_space=pl.ANY)],
        out_specs=[pl.BlockSpec((ATTN_STEP_ROWS, 2 * ATTN_GROUP_LANES), lambda m, n: (n, a_blk + m)),
                   kv_blk, kv_blk, kv_blk, kv_blk, pl.BlockSpec((8, LANES), lambda m, n: (m, 0))],
        out_shape=[jax.ShapeDtypeStruct(dproj.shape, BF16), kv_shape, kv_shape, kv_shape, kv_shape,
                   jax.ShapeDtypeStruct((n_groups * 8, LANES), F32)],
        input_output_aliases={8: 0},
        compiler_params=_params(dimension_semantics=("parallel", "arbitrary")),
    )(sinks, proj, proj, proj, proj, proj, attn_o, dmixed, dproj)
    return outs


def _kv_combine(dkc, dkp, dvc, dvp, dproj, lay):
    t, kv_w = dkc.shape
    nb = t // WINDOW
    kv_blk_idx = lay["k_off"] // (2 * kv_w)

    def body(dkc_ref, dkp_ref, dvc_ref, dvp_ref, dproj_hbm, o_ref):
        del dproj_hbm
        keep = (pl.program_id(0) < nb - 1).astype(F32)
        o_ref[:, :kv_w] = (dkc_ref[...] + keep * dkp_ref[...]).astype(BF16)
        o_ref[:, kv_w:] = (dvc_ref[...] + keep * dvp_ref[...]).astype(BF16)

    cur = pl.BlockSpec((WINDOW, kv_w), lambda n: (n, 0))
    nxt = pl.BlockSpec((WINDOW, kv_w), lambda n: (jnp.minimum(n + 1, nb - 1), 0))
    return pl.pallas_call(
        body, name="kv_combine", grid=(nb,),
        in_specs=[cur, nxt, cur, nxt, pl.BlockSpec(memory_space=pl.ANY)],
        out_specs=pl.BlockSpec((WINDOW, 2 * kv_w), lambda n: (n, kv_blk_idx)),
        out_shape=jax.ShapeDtypeStruct(dproj.shape, BF16),
        input_output_aliases={4: 0},
        compiler_params=_params(dimension_semantics=("parallel",)),
    )(dkc, dkp, dvc, dvp, dproj)


def _lower_bound(lbl_ref):
    l0 = lbl_ref[0:1, :]
    l1 = lbl_ref[1:2, :]
    mx = jnp.maximum(l0, l1)
    e0 = jnp.exp(l0 - mx)
    e1 = jnp.exp(l1 - mx)
    return e0 / (e0 + e1)


def _chunk_masks():
    ti = lax.broadcasted_iota(jnp.int32, (CHUNK, CHUNK), 0)
    si = lax.broadcasted_iota(jnp.int32, (CHUNK, CHUNK), 1)
    diag = ((ti // HALF_CHUNK) == (si // HALF_CHUNK)) & (si <= ti)
    off = (ti >= HALF_CHUNK) & (si < HALF_CHUNK)
    lower = (si <= ti).astype(BF16)
    upper = (si >= ti).astype(BF16)
    return diag, off, lower, upper


def _rnn_gates(rq, rf, lb):
    sf = _sigmoid(rf)
    f = lb + (1.0 - lb) * sf
    sq = _sigmoid(rq)
    return sf, f, jnp.log(f), 1.0 - f, sq, rq * sq


def _rnn_decays(g_cum):
    row = lax.broadcasted_iota(jnp.int32, g_cum.shape, 0)
    ref_d = jnp.where(row < HALF_CHUNK, g_cum[HALF_CHUNK // 2 - 1:HALF_CHUNK // 2],
                      g_cum[HALF_CHUNK + HALF_CHUNK // 2 - 1:HALF_CHUNK + HALF_CHUNK // 2])
    ref_o = g_cum[HALF_CHUNK - 1:HALF_CHUNK]
    last = g_cum[CHUNK - 1:CHUNK]
    return dict(eq_d=jnp.exp(g_cum - ref_d), ek_d=jnp.exp(ref_d - g_cum),
                eq_o=jnp.exp(jnp.minimum(g_cum - ref_o, 0.0)), ek_o=jnp.exp(jnp.minimum(ref_o - g_cum, 0.0)),
                eg=jnp.exp(g_cum), ekl=jnp.exp(last - g_cum), e_last=jnp.exp(last))


def _head(a, j):
    return a[:, j * RNN_HEAD_DIM:(j + 1) * RNN_HEAD_DIM]


def _rnn_specs(t, tb, d):
    gw = RNN_GROUP_HEADS * RNN_HEAD_DIM
    return gw, t // tb, tb // CHUNK


def _rnn_fwd(proj, lb_logits, rnn_gain, mixed, d, rider=None):
    t = proj.shape[0]
    tb = min(t, RNN_STEP_ROWS)
    gw, ntb, nch = _rnn_specs(t, tb, d)
    n_groups = d // gw
    n_heads = d // RNN_HEAD_DIM
    n_rin = 0 if rider is None else len(rider.operands)
    n_rout = 0 if rider is None else len(rider.out_shapes)

    def body(*refs):
        blk_ref, lbl_ref, gain_ref = refs[:3]
        rin = refs[4:4 + n_rin]
        mix_ref, o_ref, st_out_ref = refs[4 + n_rin:7 + n_rin]
        rout = refs[7 + n_rin:7 + n_rin + n_rout]
        st_ref = refs[7 + n_rin + n_rout]
        if rider is not None:
            rider.emit(pl.program_id(0) * ntb + pl.program_id(1), n_groups * ntb, rin, rout, refs[-2], refs[-1])

        @pl.when(pl.program_id(1) == 0)
        def _():
            st_ref[...] = jnp.zeros_like(st_ref)

        lb = _lower_bound(lbl_ref)
        gain = gain_ref[...]
        diag, off, lower, _ = _chunk_masks()

        def chunk(c, carry):
            rows = pl.ds(pl.multiple_of(c * CHUNK, CHUNK), CHUNK)
            rq = blk_ref[rows, 0:gw]
            rf = blk_ref[rows, gw:2 * gw]
            v = blk_ref[rows, 2 * gw:3 * gw]
            rg = blk_ref[rows, 3 * gw:4 * gw]
            _, _, g, k, _, q = _rnn_gates(rq, rf, lb)
            dec = _rnn_decays(_tri_dot(lower, g))
            qd = (q * dec["eq_d"]).astype(BF16)
            kd = (k * dec["ek_d"]).astype(BF16)
            qo = (q * dec["eq_o"]).astype(BF16)
            ko = (k * dec["ek_o"]).astype(BF16)
            qe = (q * dec["eg"]).astype(BF16)
            kl = (k * dec["ekl"]).astype(BF16)
            vb = v.astype(BF16)
            outs = []
            for j in range(RNN_GROUP_HEADS):
                st = st_ref[j]
                st_out_ref[j, c] = st
                attn = jnp.where(diag, _dot_nt(_head(qd, j), _head(kd, j)),
                                 jnp.where(off, _dot_nt(_head(qo, j), _head(ko, j)), 0.0))
                o = _dot(attn.astype(BF16), _head(vb, j)) + _dot_nt(_head(qe, j), st.astype(BF16))
                st_ref[j] = st * _head(dec["e_last"], j) + _dot_tn(_head(vb, j), _head(kl, j))
                rr = lax.rsqrt(jnp.mean(o * o, axis=-1, keepdims=True) + NORM_EPS)
                o_ref[rows, j * RNN_HEAD_DIM:(j + 1) * RNN_HEAD_DIM] = o
                outs.append(o * rr)
            on = jnp.concatenate(outs, axis=1) * gain
            mix_ref[rows, :] = (on * (rg * _sigmoid(rg))).astype(BF16)
            return carry

        lax.fori_loop(0, nch, chunk, 0, unroll=True)

    out_shapes = [jax.ShapeDtypeStruct(mixed.shape, BF16), jax.ShapeDtypeStruct((t, d), F32),
                  jax.ShapeDtypeStruct((n_heads, t // CHUNK, RNN_HEAD_DIM, RNN_HEAD_DIM), F32)]
    operands = [proj, lb_logits, rnn_gain, mixed]
    scratch = [pltpu.VMEM((RNN_GROUP_HEADS, RNN_HEAD_DIM, RNN_HEAD_DIM), F32)]
    aliases, sem = {3: 0}, ("parallel", "arbitrary")
    if rider is not None:
        scratch += rider.scratch()
        out_shapes += list(rider.out_shapes)
        operands += list(rider.operands)
        aliases.update({4 + i: 3 + o for i, o in rider.aliases.items()})
        sem = ("arbitrary", "arbitrary")
    return pl.pallas_call(
        body, name="rnn_fwd", grid=(n_groups, ntb),
        in_specs=[pl.BlockSpec((tb, 4 * gw), lambda h, i: (i, h)),
                  pl.BlockSpec((2, gw), lambda h, i: (0, h)),
                  pl.BlockSpec((1, gw), lambda h, i: (0, h)),
                  pl.BlockSpec(memory_space=pl.ANY)] + [HBM_SPEC] * n_rin,
        out_specs=[pl.BlockSpec((tb, gw), lambda h, i: (i, d // gw + h)),
                   pl.BlockSpec((tb, gw), lambda h, i: (i, h)),
                   pl.BlockSpec((RNN_GROUP_HEADS, nch, RNN_HEAD_DIM, RNN_HEAD_DIM), lambda h, i: (h, i, 0, 0))]
        + [HBM_SPEC] * n_rout,
        out_shape=out_shapes, scratch_shapes=scratch, input_output_aliases=aliases,
        compiler_params=_params(dimension_semantics=sem),
    )(*operands)


def _rnn_bwd(proj, lb_logits, rnn_gain, o_pre, states, dmixed, d_total, d, rider=None):
    t = proj.shape[0]
    tb = min(t, RNN_STEP_ROWS)
    gw, ntb, nch = _rnn_specs(t, tb, d)
    n_groups = d // gw
    n_rin = 0 if rider is None else len(rider.operands)
    n_rout = 0 if rider is None else len(rider.out_shapes)

    def body(*refs):
        blk_ref, lbl_ref, gain_ref, o_ref, st_in_ref, dm_ref = refs[:6]
        rin = refs[6:6 + n_rin]
        dproj_ref, dgain_ref, dlb_ref = refs[6 + n_rin:9 + n_rin]
        rout = refs[9 + n_rin:9 + n_rin + n_rout]
        dst_ref = refs[9 + n_rin + n_rout]
        if rider is not None:
            rider.emit(pl.program_id(0) * ntb + pl.program_id(1), n_groups * ntb, rin, rout, refs[-2], refs[-1])

        @pl.when(pl.program_id(1) == 0)
        def _():
            dst_ref[...] = jnp.zeros_like(dst_ref)
            dgain_ref[...] = jnp.zeros_like(dgain_ref)
            dlb_ref[...] = jnp.zeros_like(dlb_ref)

        lb = _lower_bound(lbl_ref)
        gain = gain_ref[...]
        diag, off, lower, upper = _chunk_masks()
        last_row = lax.broadcasted_iota(jnp.int32, (CHUNK, RNN_HEAD_DIM), 0) == CHUNK - 1

        def chunk(step, carry):
            c = nch - 1 - step
            rows = pl.ds(pl.multiple_of(c * CHUNK, CHUNK), CHUNK)
            rq = blk_ref[rows, 0:gw]
            rf = blk_ref[rows, gw:2 * gw]
            v = blk_ref[rows, 2 * gw:3 * gw]
            rg = blk_ref[rows, 3 * gw:4 * gw]
            sf, f, g, k, sq, q = _rnn_gates(rq, rf, lb)
            dec = _rnn_decays(_tri_dot(lower, g))
            qd = (q * dec["eq_d"]).astype(BF16)
            kd = (k * dec["ek_d"]).astype(BF16)
            qo = (q * dec["eq_o"]).astype(BF16)
            ko = (k * dec["ek_o"]).astype(BF16)
            qe = (q * dec["eg"]).astype(BF16)
            kl = (k * dec["ekl"]).astype(BF16)
            vb = v.astype(BF16)

            o = o_ref[rows, :]
            dmix = dm_ref[rows, :]
            sg = _sigmoid(rg)
            n_parts = []
            for j in range(RNN_GROUP_HEADS):
                oj = _head(o, j)
                n_parts.append(oj * lax.rsqrt(jnp.mean(oj * oj, axis=-1, keepdims=True) + NORM_EPS))
            nrm = jnp.concatenate(n_parts, axis=1)
            d_on = dmix * (rg * sg)
            d_rg = dmix * (nrm * gain) * (sg * (1.0 + rg * (1.0 - sg)))
            dgain_ref[...] += jnp.sum(d_on * nrm, axis=0, keepdims=True)
            dn = d_on * gain

            dq_parts, dk_parts, dv_parts, dg_parts = [], [], [], []
            for j in range(RNN_GROUP_HEADS):
                oj, nj, dnj = _head(o, j), _head(nrm, j), _head(dn, j)
                rr = lax.rsqrt(jnp.mean(oj * oj, axis=-1, keepdims=True) + NORM_EPS)
                do = (rr * (dnj - nj * jnp.mean(dnj * nj, axis=-1, keepdims=True))).astype(BF16)
                st = st_in_ref[j, c]
                dst = dst_ref[j]
                stb, dstb = st.astype(BF16), dst.astype(BF16)
                qdj, kdj, qoj, koj = _head(qd, j), _head(kd, j), _head(qo, j), _head(ko, j)
                attn = jnp.where(diag, _dot_nt(qdj, kdj), jnp.where(off, _dot_nt(qoj, koj), 0.0))
                dattn = _dot_nt(do, _head(vb, j))
                da_d = jnp.where(diag, dattn, 0.0).astype(BF16)
                da_o = jnp.where(off, dattn, 0.0).astype(BF16)
                dv = _dot_tn(attn.astype(BF16), do) + _dot_nt(_head(kl, j), dstb)
                dq_inter = _dot(do, stb) * _head(dec["eg"], j)
                dq_d, dq_o = _dot(da_d, kdj), _dot(da_o, koj)
                dq = dq_inter + dq_d * _head(dec["eq_d"], j) + dq_o * _head(dec["eq_o"], j)
                dk_inter = _dot(_head(vb, j), dstb) * _head(dec["ekl"], j)
                dk_d, dk_o = _dot_tn(da_d, qdj), _dot_tn(da_o, qoj)
                dk = dk_inter + dk_d * _head(dec["ek_d"], j) + dk_o * _head(dec["ek_o"], j)
                kj, qj = _head(k, j), _head(q, j)
                e_last = _head(dec["e_last"], j)
                extra = (jnp.sum(kj * dk_inter, axis=0, keepdims=True)
                         + e_last * jnp.sum(st * dst, axis=0, keepdims=True))
                dg_cum = (qj * dq_inter - kj * dk_inter
                          + (qdj.astype(F32) * dq_d + qoj.astype(F32) * dq_o)
                          - (kdj.astype(F32) * dk_d + koj.astype(F32) * dk_o))
                dg_parts.append(jnp.where(last_row, dg_cum + extra, dg_cum))
                dst_ref[j] = dst * e_last + _dot_tn(do, _head(qe, j))
                dq_parts.append(dq)
                dk_parts.append(dk)
                dv_parts.append(dv)

            dq = jnp.concatenate(dq_parts, axis=1)
            dk = jnp.concatenate(dk_parts, axis=1)
            dg = _tri_dot(upper, jnp.concatenate(dg_parts, axis=1))
            df = dg / f - dk
            dlb_ref[...] += jnp.sum(df * (1.0 - sf), axis=0, keepdims=True)
            d_rf = df * (1.0 - lb) * (sf * (1.0 - sf))
            d_rq = dq * (sq * (1.0 + rq * (1.0 - sq)))
            dproj_ref[rows, 0:gw] = d_rq.astype(BF16)
            dproj_ref[rows, gw:2 * gw] = d_rf.astype(BF16)
            dproj_ref[rows, 2 * gw:3 * gw] = jnp.concatenate(dv_parts, axis=1).astype(BF16)
            dproj_ref[rows, 3 * gw:4 * gw] = d_rg.astype(BF16)
            return carry

        lax.fori_loop(0, nch, chunk, 0, unroll=True)

    rev = lambda i: ntb - 1 - i
    vec = pl.BlockSpec((1, gw), lambda h, i: (0, h))
    scratch = [pltpu.VMEM((RNN_GROUP_HEADS, RNN_HEAD_DIM, RNN_HEAD_DIM), F32)]
    out_shapes = [jax.ShapeDtypeStruct((t, d_total), BF16), jax.ShapeDtypeStruct((1, d), F32),
                  jax.ShapeDtypeStruct((1, d), F32)]
    operands = [proj, lb_logits, rnn_gain, o_pre, states, dmixed]
    sem = ("parallel", "arbitrary")
    if rider is not None:
        scratch += rider.scratch()
        out_shapes += list(rider.out_shapes)
        operands += list(rider.operands)
        sem = ("arbitrary", "arbitrary")
    return pl.pallas_call(
        body, name="rnn_bwd", grid=(n_groups, ntb),
        in_specs=[pl.BlockSpec((tb, 4 * gw), lambda h, i: (rev(i), h)),
                  pl.BlockSpec((2, gw), lambda h, i: (0, h)), vec,
                  pl.BlockSpec((tb, gw), lambda h, i: (rev(i), h)),
                  pl.BlockSpec((RNN_GROUP_HEADS, nch, RNN_HEAD_DIM, RNN_HEAD_DIM), lambda h, i: (h, rev(i), 0, 0)),
                  pl.BlockSpec((tb, gw), lambda h, i: (rev(i), d // gw + h))] + [HBM_SPEC] * n_rin,
        out_specs=[pl.BlockSpec((tb, 4 * gw), lambda h, i: (rev(i), h)), vec, vec] + [HBM_SPEC] * n_rout,
        out_shape=out_shapes, scratch_shapes=scratch,
        compiler_params=_params(dimension_semantics=sem),
    )(*operands)


def _local_grads(x, target, w_in_full, w_out, sinks, lb_logits, rnn_gain, pre_gain, post_gain, sc=None):
    t, d = x.shape
    comm = sc is not None
    lay = _layout(d)
    perm = lay["perm"]
    h = _prenorm_fwd(x, pre_gain)
    if comm:
        proj, w_in_full = _proj_gather_mm(h, w_in_full, perm, sc)
        mixed, attn_o, w_out = _attn_fwd(proj, sinks, lay, d, _gather_rider(w_out, 0.7, 0, 2))
        mixed, o_pre, states, w_out_full = _rnn_fwd(proj, lb_logits, rnn_gain, mixed, d,
                                                    _gather_rider(w_out, 0.7, 1, 2))
    else:
        (proj,) = _proj_mm(h, w_in_full, perm)
        w_out_full = w_out
        mixed, attn_o = _attn_fwd(proj, sinks, lay, d)
        mixed, o_pre, states = _rnn_fwd(proj, lb_logits, rnn_gain, mixed, d)
    y = _out_mm(mixed, w_out_full)
    dy, dz, g_post, loss = _post_loss(x, y, target, post_gain)
    gw_out = _gw_out_mm(mixed, dy)
    rider = None
    if comm:
        dmixed, recv_out = _dmixed_mm(dy, w_out_full, _pair_exchange_rider(gw_out, stacked=True))
        p_out = _pair_sum_out(gw_out, recv_out)
        rider = _chip_exchange_rider(p_out, lambda ref, chip: ref.at[chip])
    else:
        (dmixed,) = _dmixed_mm(dy, w_out_full)
    dproj, g_rnn, g_lb, *r_out = _rnn_bwd(proj, lb_logits, rnn_gain, o_pre, states, dmixed, lay["total"], d, rider)
    dproj, dkc, dkp, dvc, dvp, dsink = _attn_bwd(proj, sinks, attn_o, dmixed, dproj, lay, d)
    dproj = _kv_combine(dkc, dkp, dvc, dvp, dproj, lay)
    if comm:
        c = lax.axis_index("c")
        (g_other,) = _gw_in_mm("gw_in_mm_other", h, dproj, perm, 1 - c)
        g_mine, recv_in = _gw_in_mm("gw_in_mm_mine", h, dproj, perm, c, _pair_exchange_rider(g_other))
        p_in = _pair_sum_in(g_mine, recv_in, sc)
        rider = _chip_exchange_rider(
            p_in, lambda ref, chip: ref.at[:, pl.ds(pl.multiple_of(chip * sc, LANES), sc)])
    dh, *r_in = _dh_mm(dproj, w_in_full, perm, rider)
    grad_x, g_pre = _prenorm_bwd(x, dh, dz, pre_gain)
    heads_per_group = ATTN_GROUP_LANES // ATTN_HEAD_DIM
    g_sink = dsink.reshape(d // ATTN_GROUP_LANES, 8, LANES)[:, 0, :heads_per_group].reshape(1, -1)
    small = dict(sink=g_sink, lb=g_lb, rnn=g_rnn, pre=g_pre, post=g_post)
    if comm:
        return loss, grad_x, (p_in, r_in[0]), (p_out, r_out[0]), small
    gw_in = jnp.stack([_gw_in_mm("gw_in_mm_%d" % half, h, dproj, perm, half)[0] for half in range(2)])
    return loss, grad_x, gw_in, gw_out, small


def _mesh_pos():
    x, y, c = lax.axis_index("x"), lax.axis_index("y"), lax.axis_index("c")
    chips = [(1 - x, y), (x, 1 - y), (1 - x, 1 - y)]
    return x, y, c, chips


def _remote(src, dst, send_sem, recv_sem, device):
    return pltpu.make_async_remote_copy(src_ref=src, dst_ref=dst, send_sem=send_sem, recv_sem=recv_sem,
                                        device_id=device, device_id_type=MESH)


HBM_SPEC = pl.BlockSpec(memory_space=pl.ANY)


def _gather_rider(part, forward_at, section, n_sections):
    rows = part.shape[0] // N_CHIPS
    half_rows = rows // 2
    sec_rows = half_rows // n_sections

    def stages(ins, outs, send_sems, recv_sems):
        del ins
        full = outs[0]

        def piece(chip, half):
            start = chip * rows + half * half_rows + section * sec_rows
            return full.at[pl.ds(pl.multiple_of(start, 8), sec_rows), :]

        def sends():
            x, y, c, chips = _mesh_pos()
            mine = piece(2 * x + y, c)
            return [_remote(mine, mine, send_sems.at[j], recv_sems.at[j], (px, py, c))
                    for j, (px, py) in enumerate(chips)]

        def forwards(half_of):
            x, y, c, chips = _mesh_pos()
            out = []
            for j, (px, py) in enumerate(chips):
                block = piece(2 * px + py, half_of(c))
                out.append(_remote(block, block, send_sems.at[3 + j], recv_sems.at[3 + j], (x, y, 1 - c)))
            return out

        def start():
            for cp in sends():
                cp.start()

        def forward():
            x, y, c, chips = _mesh_pos()
            for j, (px, py) in enumerate(chips):
                landed = piece(2 * px + py, c)
                _remote(landed, landed, send_sems.at[j], recv_sems.at[j], (x, y, 1 - c)).wait_recv()
            for cp in forwards(lambda c: c):
                cp.start()

        def finish():
            for cp in forwards(lambda c: 1 - c):
                cp.wait_recv()
            for cp in sends() + forwards(lambda c: c):
                cp.wait_send()

        return [(0.0, start), (forward_at, forward), (1.0, finish)]

    return _Rider((part,), (jax.ShapeDtypeStruct(part.shape, BF16),), {0: 0}, 6, stages)


def _chip_exchange_rider(partial, piece):
    if partial.ndim == 3:
        recv_shape = (N_CHIPS - 1,) + partial.shape[1:]
    else:
        recv_shape = (N_CHIPS - 1, partial.shape[0], partial.shape[1] // N_CHIPS)

    def stages(ins, outs, send_sems, recv_sems):
        def copies():
            x, y, c, chips = _mesh_pos()
            return [_remote(piece(ins[0], 2 * px + py), outs[0].at[j], send_sems.at[j], recv_sems.at[j], (px, py, c))
                    for j, (px, py) in enumerate(chips)]

        def start():
            for cp in copies():
                cp.start()

        def finish():
            for cp in copies():
                cp.wait()

        return [(0.0, start), (1.0, finish)]

    return _Rider((partial,), (jax.ShapeDtypeStruct(recv_shape, BF16),), {}, N_CHIPS - 1, stages)


def _pair_exchange_rider(g, stacked=False):
    shape = g.shape[1:] if stacked else g.shape

    def stages(ins, outs, send_sems, recv_sems):
        def copy():
            x, y, c, _ = _mesh_pos()
            src = ins[0].at[1 - c] if stacked else ins[0]
            return _remote(src, outs[0], send_sems.at[0], recv_sems.at[0], (x, y, 1 - c))

        return [(0.0, lambda: copy().start()), (1.0, lambda: copy().wait())]

    return _Rider((g,), (jax.ShapeDtypeStruct(shape, g.dtype),), {}, 1, stages)


def _pair_sum_in(mine, recv, sc):
    hd, d_in = mine.shape
    tr = min(hd, 256)

    def body(a_ref, b_ref, o_ref):
        o_ref[...] = (a_ref[...].astype(F32) + b_ref[...].astype(F32)).astype(BF16)

    blk = pl.BlockSpec((tr, sc), lambda i, j: (i, j))
    return pl.pallas_call(
        body, name="pair_sum_in", grid=(hd // tr, d_in // sc), in_specs=[blk, blk], out_specs=blk,
        out_shape=jax.ShapeDtypeStruct((hd, d_in), BF16),
        compiler_params=_params(dimension_semantics=("parallel", "parallel")),
    )(mine, recv)


def _pair_sum_out(gw_out, recv):
    _, n_chips, hr, d = gw_out.shape
    tr = min(hr, 256)
    c = lax.axis_index("c")

    def body(c_ref, a_ref, b_ref, o_ref):
        del c_ref
        o_ref[...] = (a_ref[...].astype(F32) + b_ref[...].astype(F32)).astype(BF16)

    blk = pl.BlockSpec((None, tr, d), lambda k, i, cc: (k, i, 0))
    gs = pltpu.PrefetchScalarGridSpec(
        num_scalar_prefetch=1, grid=(n_chips, hr // tr),
        in_specs=[pl.BlockSpec((None, None, tr, d), lambda k, i, cc: (cc[0], k, i, 0)), blk], out_specs=blk)
    return pl.pallas_call(
        body, name="pair_sum_out", grid_spec=gs, out_shape=jax.ShapeDtypeStruct((n_chips, hr, d), BF16),
        compiler_params=_params(dimension_semantics=("parallel", "parallel")),
    )(jnp.reshape(c, (1,)).astype(jnp.int32), gw_out, recv)


def _place():
    return jnp.stack([2 * lax.axis_index("x") + lax.axis_index("y"), lax.axis_index("c")]).astype(jnp.int32)


def _chip_sum_in(p_in, r_in, sc):
    hd = p_in.shape[0]
    tr = min(hd, 256)
    nblk = hd // tr

    def body(pos_ref, p_ref, r_ref, o_ref):
        del pos_ref
        acc = p_ref[...].astype(F32)
        for j in range(3):
            acc = acc + r_ref[j].astype(F32)
        o_ref[...] = acc

    gs = pltpu.PrefetchScalarGridSpec(
        num_scalar_prefetch=1, grid=(nblk,),
        in_specs=[pl.BlockSpec((tr, sc), lambda i, pos: (i, pos[0])), pl.BlockSpec((3, tr, sc), lambda i, pos: (0, i, 0))],
        out_specs=pl.BlockSpec((tr, sc), lambda i, pos: (pos[1] * nblk + i, 0)))
    return pl.pallas_call(
        body, name="chip_sum_in", grid_spec=gs, out_shape=jax.ShapeDtypeStruct((2 * hd, sc), F32),
        compiler_params=_params(dimension_semantics=("parallel",)),
    )(_place(), p_in, r_in)


def _chip_sum_out(p_out, r_out):
    _, hr, d = p_out.shape
    tr = min(hr, 256)
    nblk = hr // tr

    def body(pos_ref, p_ref, r_ref, o_ref):
        del pos_ref
        acc = p_ref[...].astype(F32)
        for j in range(3):
            acc = acc + r_ref[j].astype(F32)
        o_ref[...] = acc

    gs = pltpu.PrefetchScalarGridSpec(
        num_scalar_prefetch=1, grid=(nblk,),
        in_specs=[pl.BlockSpec((None, tr, d), lambda i, pos: (pos[0], i, 0)), pl.BlockSpec((3, tr, d), lambda i, pos: (0, i, 0))],
        out_specs=pl.BlockSpec((tr, d), lambda i, pos: (pos[1] * nblk + i, 0)))
    return pl.pallas_call(
        body, name="chip_sum_out", grid_spec=gs, out_shape=jax.ShapeDtypeStruct((2 * hr, d), F32),
        compiler_params=_params(dimension_semantics=("parallel",)),
    )(_place(), p_out, r_out)


def _adamw_math(w, g, m, v):
    m_new = ADAM_B1 * m + (1.0 - ADAM_B1) * g
    v_new = ADAM_B2 * v + (1.0 - ADAM_B2) * (g * g)
    m_hat = m_new / (1.0 - ADAM_B1 ** ADAM_STEP)
    v_hat = v_new / (1.0 - ADAM_B2 ** ADAM_STEP)
    delta = -ADAM_LR * (m_hat / (jnp.sqrt(v_hat) + ADAM_EPS) + ADAM_WD * w)
    return delta, m_new, v_new


def _share_halves(g_in, g_out):
    hd = g_in.shape[0] // 2
    hr = g_out.shape[0] // 2

    def body(gi_in, go_in, gi_ref, go_ref, send_sems, recv_sems):
        del gi_in, go_in
        x, y, c, _ = _mesh_pos()
        sibling = (x, y, 1 - c)
        mine_i = gi_ref.at[pl.ds(pl.multiple_of(c * hd, 8), hd), :]
        mine_o = go_ref.at[pl.ds(pl.multiple_of(c * hr, 8), hr), :]
        a = _remote(mine_i, mine_i, send_sems.at[0], recv_sems.at[0], sibling)
        b = _remote(mine_o, mine_o, send_sems.at[1], recv_sems.at[1], sibling)
        a.start()
        b.start()
        a.wait_send()
        b.wait_send()
        theirs_i = gi_ref.at[pl.ds(pl.multiple_of((1 - c) * hd, 8), hd), :]
        theirs_o = go_ref.at[pl.ds(pl.multiple_of((1 - c) * hr, 8), hr), :]
        _remote(theirs_i, theirs_i, send_sems.at[0], recv_sems.at[0], sibling).wait_recv()
        _remote(theirs_o, theirs_o, send_sems.at[1], recv_sems.at[1], sibling).wait_recv()

    return pl.pallas_call(
        body, name="share_halves",
        in_specs=[HBM_SPEC, HBM_SPEC], out_specs=[HBM_SPEC, HBM_SPEC],
        out_shape=[jax.ShapeDtypeStruct(g_in.shape, F32), jax.ShapeDtypeStruct(g_out.shape, F32)],
        input_output_aliases={0: 0, 1: 1},
        scratch_shapes=[pltpu.SemaphoreType.DMA((2,)), pltpu.SemaphoreType.DMA((2,))],
    )(g_in, g_out)


def _adamw(w, g, m, v, name):
    rows, cols = w.shape
    streams = 8
    fit = (VMEM_LIMIT_BYTES // 2) // (streams * 2 * cols * 4)
    tr = min(rows, 1 << (fit.bit_length() - 1))

    def body(w_ref, g_ref, m_ref, v_ref, d_ref, mo_ref, vo_ref, go_ref):
        gv = g_ref[...]
        delta, m_new, v_new = _adamw_math(w_ref[...], gv, m_ref[...], v_ref[...])
        d_ref[...] = delta
        mo_ref[...] = m_new
        vo_ref[...] = v_new
        go_ref[...] = gv

    spec = pl.BlockSpec((tr, cols), lambda i: (i, 0))
    shape = jax.ShapeDtypeStruct((rows, cols), F32)
    return pl.pallas_call(
        body, name=name, grid=(rows // tr,), in_specs=[spec] * 4, out_specs=[spec] * 4,
        out_shape=[shape] * 4, compiler_params=_params(dimension_semantics=("parallel",)),
    )(w, g, m, v)


SMALL_ROWS = 8


def _small_allreduce_adamw(part, w_pack, m_pack, v_pack):
    d = part.shape[1]

    def body(part_ref, w_ref, m_ref, v_ref, g_ref, d_ref, mo_ref, vo_ref, buf_ref, send_sems, recv_sems):
        x, y, c, _ = _mesh_pos()
        me = 4 * x + 2 * y + c
        buf_ref[0] = part_ref[...]
        copies = []
        for r in range(1, 8):
            rx, ry, rc = (r >> 2) & 1, (r >> 1) & 1, r & 1
            peer = (x ^ rx, y ^ ry, c ^ rc)
            copies.append(_remote(buf_ref.at[0], buf_ref.at[r], send_sems.at[r - 1], recv_sems.at[r - 1], peer))
        for cp in copies:
            cp.start()
        for cp in copies:
            cp.wait()
        total = buf_ref[me]
        for s in range(1, 8):
            total = total + buf_ref[s ^ me]
        w = w_ref[...]
        row = lax.broadcasted_iota(jnp.int32, (SMALL_ROWS, d), 0)
        l0, l1 = w[3:4], w[4:5]
        mx = jnp.maximum(l0, l1)
        e0, e1 = jnp.exp(l0 - mx), jnp.exp(l1 - mx)
        lb = e0 / (e0 + e1)
        g_l0 = total[3:4] * lb * (1.0 - lb)
        grads = jnp.where(row == 3, g_l0, jnp.where(row == 4, -g_l0, total))
        g_ref[...] = grads
        delta, m_new, v_new = _adamw_math(w, grads, m_ref[...], v_ref[...])
        d_ref[...] = delta
        mo_ref[...] = m_new
        vo_ref[...] = v_new

    vm = pl.BlockSpec(memory_space=pltpu.VMEM)
    shape = jax.ShapeDtypeStruct((SMALL_ROWS, d), F32)
    return pl.pallas_call(
        body, name="small_allreduce_adamw",
        in_specs=[vm] * 4, out_specs=[vm] * 4, out_shape=[shape] * 4,
        scratch_shapes=[pltpu.VMEM((8, SMALL_ROWS, d), F32), pltpu.SemaphoreType.DMA((7,)), pltpu.SemaphoreType.DMA((7,))],
    )(part, w_pack, m_pack, v_pack)


def _pack_small(d, pre, post, rnn, lb, sink, extra=None):
    rows = [pre, post, rnn, lb[0:1], lb[1:2],
            jnp.pad(sink, ((0, 0), (0, d - sink.shape[1]))),
            jnp.zeros((1, d), F32) if extra is None else extra,
            jnp.zeros((1, d), F32)]
    return jnp.concatenate(rows, axis=0)


def _unpack_small(p, n_sink):
    return dict(pre=p[0:1], post=p[1:2], rnn=p[2:3], lb=p[3:5], sink=p[5:6, :n_sink])


def kernel(x, w_in, attn_sinks, lb_logits, rnn_norm, w_out, pre_norm, post_norm, loss_target, m_w_in, m_attn_sinks, m_lb_logits, m_rnn_norm, m_w_out, m_pre_norm, m_post_norm, v_w_in, v_attn_sinks, v_lb_logits, v_rnn_norm, v_w_out, v_pre_norm, v_post_norm):
    t, d = x.shape[1], x.shape[2]
    sc = w_in.shape[2]
    n_sink = attn_sinks.shape[1]
    w_in2, w_out2 = w_in[0], w_out[0]

    w_in_part = _cast_into_gathered(w_in2, "cast_w_in", 1)
    w_out_part = _cast_into_gathered(w_out2, "cast_w_out", 0)
    loss_part, grad_x, (p_in, r_in), (p_out, r_out), small = _local_grads(
        x[0], loss_target[0], w_in_part, w_out_part, attn_sinks, lb_logits, rnn_norm, pre_norm, post_norm, sc)
    g_w_in, g_w_out = _share_halves(_chip_sum_in(p_in, r_in, sc), _chip_sum_out(p_out, r_out))
    d_w_in, nm_w_in, nv_w_in, g_w_in = _adamw(w_in2, g_w_in, m_w_in[0], v_w_in[0], "adamw_w_in")
    d_w_out, nm_w_out, nv_w_out, g_w_out = _adamw(w_out2, g_w_out, m_w_out[0], v_w_out[0], "adamw_w_out")

    lb_part = jnp.concatenate([small["lb"], jnp.zeros_like(small["lb"])], axis=0)
    loss_row = jnp.pad(loss_part[:, :1], ((0, 0), (0, d - 1)))
    part = _pack_small(d, small["pre"], small["post"], small["rnn"], lb_part, small["sink"], loss_row)
    w_pack = _pack_small(d, pre_norm, post_norm, rnn_norm, lb_logits, attn_sinks)
    m_pack = _pack_small(d, m_pre_norm, m_post_norm, m_rnn_norm, m_lb_logits, m_attn_sinks)
    v_pack = _pack_small(d, v_pre_norm, v_post_norm, v_rnn_norm, v_lb_logits, v_attn_sinks)
    g_pack, d_pack, nm_pack, nv_pack = _small_allreduce_adamw(part, w_pack, m_pack, v_pack)
    loss = g_pack[6, 0]
    g, dl, nm, nv = (_unpack_small(p, n_sink) for p in (g_pack, d_pack, nm_pack, nv_pack))

    def ordered(w_in_leaf, w_out_leaf, s):
        return (w_in_leaf[None], s["sink"], s["lb"], s["rnn"], w_out_leaf[None], s["pre"], s["post"])

    return (loss, grad_x[None],
            *ordered(g_w_in, g_w_out, g), *ordered(d_w_in, d_w_out, dl),
            *ordered(nm_w_in, nm_w_out, nm), *ordered(nv_w_in, nv_w_out, nv))
```

```python
import numpy as np
import jax
import jax.numpy as jnp
from jax import lax
from jax.experimental import pallas as pl
from jax.experimental.pallas import tpu as pltpu

F32 = jnp.float32
BF16 = jnp.bfloat16
MESH = pl.DeviceIdType.MESH

NORM_EPS = 1e-6
ATTN_HEAD_DIM = 64
GQA_GROUP = 8
WINDOW = 128
ATTN_STEP_BLOCKS = 4
ATTN_STEP_ROWS = ATTN_STEP_BLOCKS * WINDOW
RNN_HEAD_DIM = 128
CHUNK = 64
HALF_CHUNK = CHUNK // 2
ATTN_SCALE = ATTN_HEAD_DIM ** -0.5

ADAM_LR = 0.001
ADAM_B1 = 0.9
ADAM_B2 = 0.999
ADAM_EPS = 1e-08
ADAM_WD = 0.01
ADAM_STEP = 10

LANES = 128
COL_TILE = 512
RNN_GROUP_HEADS = 8
RNN_STEP_ROWS = 256
ATTN_GROUP_LANES = 1024
N_CHIPS = 4
VMEM_LIMIT_BYTES = 56 * 1024 * 1024
NEG_BIG = -1e30


def _params(**kw):
    return pltpu.CompilerParams(vmem_limit_bytes=VMEM_LIMIT_BYTES, **kw)


def _sigmoid(x):
    return 1.0 / (1.0 + jnp.exp(-x))


def _dot(a, b):
    return jnp.dot(a, b, preferred_element_type=F32)


def _dot_nt(a, b):
    return lax.dot_general(a, b, (((1,), (1,)), ((), ())), preferred_element_type=F32)


def _dot_tn(a, b):
    return lax.dot_general(a, b, (((0,), (0,)), ((), ())), preferred_element_type=F32)


def _tri_dot(tri_bf16, x):
    hi = x.astype(BF16)
    lo = (x - hi.astype(F32)).astype(BF16)
    return _dot(tri_bf16, hi) + _dot(tri_bf16, lo)


def _layout(d_model):
    d = d_model
    dkv = d // GQA_GROUP
    orig = dict(aq=0, ak=d, av=d + dkv, ag=d + 2 * dkv)
    base = d + 2 * dkv + d
    orig.update(rq=base, rf=base + d, ri=base + 2 * d, rg=base + 3 * d)
    group_w = RNN_GROUP_HEADS * RNN_HEAD_DIM
    cols = []
    for hg in range(d // group_w):
        for seg in ("rq", "rf", "ri", "rg"):
            cols.append((orig[seg] + hg * group_w, group_w))
    for m in range(d // ATTN_GROUP_LANES):
        for seg in ("aq", "ag"):
            cols.append((orig[seg] + m * ATTN_GROUP_LANES, ATTN_GROUP_LANES))
    cols.append((orig["ak"], dkv))
    cols.append((orig["av"], dkv))
    units = []
    for start, width in cols:
        assert start % LANES == 0 and width % LANES == 0
        units += [start + u for u in range(0, width, LANES)]
    per = COL_TILE // LANES
    assert len(units) % per == 0
    tiles = []
    for t in range(len(units) // per):
        run = units[t * per:(t + 1) * per]
        assert run[0] % COL_TILE == 0 and all(run[i] == run[0] + i * LANES for i in range(per))
        tiles.append(run[0] // COL_TILE)
    return dict(a_off=4 * d, k_off=6 * d, v_off=6 * d + dkv, total=6 * d + 2 * dkv,
                perm=np.asarray(tiles, np.int32))


def _chip_index():
    return jnp.reshape(2 * lax.axis_index("x") + lax.axis_index("y"), (1,)).astype(jnp.int32)


def _cast_into_gathered(a, name, axis):
    rows, cols = a.shape
    tr = min(rows, 512)
    nblk = rows // tr

    def body(me_ref, a_ref, o_ref):
        del me_ref
        o_ref[...] = a_ref[...].astype(BF16)

    if axis == 1:
        out_spec = pl.BlockSpec((tr, cols), lambda i, me: (i, me[0]))
        shape = (rows, N_CHIPS * cols)
    else:
        out_spec = pl.BlockSpec((tr, cols), lambda i, me: (me[0] * nblk + i, 0))
        shape = (N_CHIPS * rows, cols)
    gs = pltpu.PrefetchScalarGridSpec(num_scalar_prefetch=1, grid=(nblk,),
                                      in_specs=[pl.BlockSpec((tr, cols), lambda i, me: (i, 0))], out_specs=out_spec)
    return pl.pallas_call(
        body, name=name, grid_spec=gs, out_shape=jax.ShapeDtypeStruct(shape, BF16),
        compiler_params=_params(dimension_semantics=("parallel",)),
    )(_chip_index(), a)


def _prenorm_fwd(x, gain):
    t, d = x.shape
    tm = min(t, 256)

    def body(x_ref, g_ref, h_ref):
        xv = x_ref[...]
        r = lax.rsqrt(jnp.mean(xv * xv, axis=-1, keepdims=True) + NORM_EPS)
        h_ref[...] = ((xv * r) * g_ref[...]).astype(BF16)

    return pl.pallas_call(
        body, name="prenorm_fwd", grid=(t // tm,),
        in_specs=[pl.BlockSpec((tm, d), lambda i: (i, 0)), pl.BlockSpec((1, d), lambda i: (0, 0))],
        out_specs=pl.BlockSpec((tm, d), lambda i: (i, 0)),
        out_shape=jax.ShapeDtypeStruct((t, d), BF16),
        compiler_params=_params(dimension_semantics=("parallel",)),
    )(x, gain)


def _post_loss(x, y, target, gain):
    t, d = x.shape
    tm = min(t, 256)
    inv_d = 1.0 / d

    def body(x_ref, y_ref, t_ref, g_ref, dy_ref, dz_ref, gp_ref, loss_ref):
        i = pl.program_id(0)
        yv = y_ref[...]
        gain_v = g_ref[...]
        r = lax.rsqrt(jnp.mean(yv * yv, axis=-1, keepdims=True) + NORM_EPS)
        n = yv * r
        e = (x_ref[...] + n * gain_v) - t_ref[...]
        dz = e * inv_d
        dn = dz * gain_v
        dy = r * (dn - n * jnp.mean(dn * n, axis=-1, keepdims=True))
        dy_ref[...] = dy.astype(BF16)
        dz_ref[...] = dz

        @pl.when(i == 0)
        def _():
            gp_ref[...] = jnp.zeros_like(gp_ref)
            loss_ref[...] = jnp.zeros_like(loss_ref)

        gp_ref[...] += jnp.sum(dz * n, axis=0, keepdims=True)
        row = jnp.sum(e * e, axis=-1, keepdims=True)
        loss_ref[...] += jnp.full(loss_ref.shape, 0.5 * inv_d * jnp.sum(row), F32)

    row_spec = pl.BlockSpec((tm, d), lambda i: (i, 0))
    vec_spec = pl.BlockSpec((1, d), lambda i: (0, 0))
    return pl.pallas_call(
        body, name="post_loss", grid=(t // tm,),
        in_specs=[row_spec, row_spec, row_spec, vec_spec],
        out_specs=[row_spec, row_spec, vec_spec, pl.BlockSpec((1, LANES), lambda i: (0, 0))],
        out_shape=[jax.ShapeDtypeStruct((t, d), BF16), jax.ShapeDtypeStruct((t, d), F32),
                   jax.ShapeDtypeStruct((1, d), F32), jax.ShapeDtypeStruct((1, LANES), F32)],
        compiler_params=_params(dimension_semantics=("arbitrary",)),
    )(x, y, target, gain)


def _prenorm_bwd(x, dh, dz, gain):
    t, d = x.shape
    tm = min(t, 256)

    def body(x_ref, dh_ref, dz_ref, g_ref, gx_ref, gp_ref):
        i = pl.program_id(0)
        xv = x_ref[...]
        r = lax.rsqrt(jnp.mean(xv * xv, axis=-1, keepdims=True) + NORM_EPS)
        n = xv * r
        dhv = dh_ref[...]
        dn = dhv * g_ref[...]
        gx_ref[...] = dz_ref[...] + r * (dn - n * jnp.mean(dn * n, axis=-1, keepdims=True))

        @pl.when(i == 0)
        def _():
            gp_ref[...] = jnp.zeros_like(gp_ref)

        gp_ref[...] += jnp.sum(dhv * n, axis=0, keepdims=True)

    row_spec = pl.BlockSpec((tm, d), lambda i: (i, 0))
    vec_spec = pl.BlockSpec((1, d), lambda i: (0, 0))
    return pl.pallas_call(
        body, name="prenorm_bwd", grid=(t // tm,),
        in_specs=[row_spec, row_spec, row_spec, vec_spec],
        out_specs=[row_spec, vec_spec],
        out_shape=[jax.ShapeDtypeStruct((t, d), F32), jax.ShapeDtypeStruct((1, d), F32)],
        compiler_params=_params(dimension_semantics=("arbitrary",)),
    )(x, dh, dz, gain)


class _Rider:
    def __init__(self, operands, out_shapes, aliases, n_sems, stages):
        self.operands = tuple(operands)
        self.out_shapes = tuple(out_shapes)
        self.aliases = dict(aliases)
        self.n_sems = n_sems
        self.stages = stages

    def scratch(self):
        return [pltpu.SemaphoreType.DMA((self.n_sems,)), pltpu.SemaphoreType.DMA((self.n_sems,))]

    def emit(self, step, n_steps, in_refs, out_refs, send_sems, recv_sems):
        for frac, fn in self.stages(in_refs, out_refs, send_sems, recv_sems):
            at = min(n_steps - 1, int(frac * (n_steps - 1) + 0.5))
            pl.when(step == at)(fn)


def _matmul(name, a, b, *, out_shape, grid, a_spec, b_spec, o_spec, nt=False, ta=False, perm=None, rider=None):
    nk = grid[2]
    n_steps = grid[0] * grid[1] * grid[2]
    tm, tn = [s for s in o_spec.block_shape if s is not None][-2:]
    acc_in_out = out_shape.dtype == F32
    n_pre = 0 if perm is None else 1
    n_rin = 0 if rider is None else len(rider.operands)
    n_rout = 0 if rider is None else len(rider.out_shapes)
    use_acc = not (nk == 1 or acc_in_out)

    def body(*refs):
        refs = refs[n_pre:]
        a_ref, b_ref = refs[:2]
        rin = refs[2:2 + n_rin]
        o_ref = refs[2 + n_rin]
        rout = refs[3 + n_rin:3 + n_rin + n_rout]
        scratch_refs = refs[3 + n_rin + n_rout:]
        if rider is not None:
            step = (pl.program_id(0) * grid[1] + pl.program_id(1)) * grid[2] + pl.program_id(2)
            rider.emit(step, n_steps, rin, rout, scratch_refs[-2], scratch_refs[-1])
        def product():
            if ta:
                return _dot_tn(a_ref[...], b_ref[...])
            return _dot_nt(a_ref[...], b_ref[...]) if nt else _dot(a_ref[...], b_ref[...])

        if nk == 1:
            o_ref[...] = product().astype(o_ref.dtype)
            return
        acc_ref = o_ref if acc_in_out else scratch_refs[0]
        k = pl.program_id(2)

        @pl.when(k == 0)
        def _():
            acc_ref[...] = jnp.zeros_like(acc_ref)

        acc_ref[...] += product()

        if not acc_in_out:
            @pl.when(k == nk - 1)
            def _():
                o_ref[...] = acc_ref[...].astype(o_ref.dtype)

    scratch = [pltpu.VMEM((tm, tn), F32)] if use_acc else []
    in_specs = [a_spec, b_spec] + [HBM_SPEC] * n_rin
    out_specs = [o_spec] + [HBM_SPEC] * n_rout
    out_shapes = [out_shape]
    operands = [a, b]
    aliases = {}
    sem = ("parallel", "parallel", "arbitrary")
    if rider is not None:
        scratch += rider.scratch()
        out_shapes += list(rider.out_shapes)
        operands += list(rider.operands)
        aliases = {n_pre + 2 + i: 1 + o for i, o in rider.aliases.items()}
        sem = ("arbitrary", "arbitrary", "arbitrary")
    cp = _params(dimension_semantics=sem)
    if perm is None:
        return pl.pallas_call(body, name=name, grid=grid, in_specs=in_specs, out_specs=out_specs,
                              out_shape=out_shapes, scratch_shapes=scratch, input_output_aliases=aliases,
                              compiler_params=cp)(*operands)
    gs = pltpu.PrefetchScalarGridSpec(num_scalar_prefetch=1, grid=grid, in_specs=in_specs,
                                      out_specs=out_specs, scratch_shapes=scratch)
    return pl.pallas_call(body, name=name, grid_spec=gs, out_shape=out_shapes, input_output_aliases=aliases,
                          compiler_params=cp)(jnp.asarray(perm), *operands)


def _proj_mm(h, w_full, perm, rider=None):
    t, d = h.shape
    n_tiles = len(perm)
    tm = min(t, 1024)
    return _matmul(
        "proj_mm", h, w_full, perm=perm, rider=rider, grid=(t // tm, n_tiles, 1),
        out_shape=jax.ShapeDtypeStruct((t, n_tiles * COL_TILE), F32),
        a_spec=pl.BlockSpec((tm, d), lambda i, j, k, p: (i, 0)),
        b_spec=pl.BlockSpec((d, COL_TILE), lambda i, j, k, p: (0, p[j])),
        o_spec=pl.BlockSpec((tm, COL_TILE), lambda i, j, k, p: (i, j)))


def _proj_gather_mm(h, wi_part, perm, sc):
    t, d = h.shape
    n_tiles = len(perm)
    tm = min(t, 1024)
    n_i = t // tm
    hd = d // 2
    nf = sc // COL_TILE
    rem = sc - nf * COL_TILE
    assert 2 * rem == COL_TILE and n_tiles == N_CHIPS * nf + 2
    n_chunks = nf
    tpc = nf // n_chunks
    n_kinds = n_chunks + 1
    n_sems = 3 * n_kinds
    rem_at = nf + 3 * n_chunks * tpc

    def first_full(chip):
        return (chip * sc + (rem if chip % 2 else 0)) // COL_TILE

    inverse = np.argsort(perm)
    table = np.zeros((N_CHIPS, 2, n_tiles), np.int32)
    for chip in range(N_CHIPS):
        seq = list(range(first_full(chip), first_full(chip) + nf))
        for q in range(n_chunks):
            for src in (chip ^ 2, chip ^ 1, chip ^ 3):
                seq += list(range(first_full(src) + q * tpc, first_full(src) + (q + 1) * tpc))
        seq += [first_full(chip - chip % 2) + nf, first_full((chip ^ 2) - chip % 2) + nf]
        assert sorted(seq) == list(range(n_tiles)), seq
        table[chip, 0] = inverse[seq]
        table[chip, 1] = seq
    me_chip = 2 * lax.axis_index("x") + lax.axis_index("y")
    tab = lax.dynamic_index_in_dim(jnp.asarray(table), me_chip, 0, keepdims=False)

    def body(tab_ref, h_hbm, wi_in, proj_ref, full, hbuf, bbuf, local_sems, send_sems, recv_sems):
        del wi_in
        jj = pl.program_id(0)
        i = pl.program_id(1)
        x, y, c, chips = _mesh_pos()
        sibling = (x, y, 1 - c)

        def piece(chip, half, kind):
            odd = chip % 2
            if kind == n_chunks:
                start, width = chip * sc + (1 - odd) * (nf * COL_TILE), rem
            else:
                start, width = chip * sc + odd * rem + kind * (tpc * COL_TILE), tpc * COL_TILE
            return full.at[pl.ds(half * hd, hd), pl.ds(pl.multiple_of(start, LANES), width)]

        def ici(j, kind):
            mine = piece(2 * x + y, c, kind)
            k = j * n_kinds + kind
            return _remote(mine, mine, send_sems.at[k], recv_sems.at[k], (chips[j][0], chips[j][1], c))

        def landed(j, kind):
            blk = piece(2 * chips[j][0] + chips[j][1], c, kind)
            k = j * n_kinds + kind
            return _remote(blk, blk, send_sems.at[k], recv_sems.at[k], sibling)

        def passed(j, kind, half):
            blk = piece(2 * chips[j][0] + chips[j][1], half, kind)
            k = n_sems + j * n_kinds + kind
            return _remote(blk, blk, send_sems.at[k], recv_sems.at[k], sibling)

        def fetch(pos, slot):
            col = pl.multiple_of(tab_ref[1, pos] * COL_TILE, LANES)
            return pltpu.make_async_copy(full.at[:, pl.ds(col, COL_TILE)], bbuf.at[slot], local_sems.at[slot])

        def load_h():
            return pltpu.make_async_copy(h_hbm, hbuf, local_sems.at[2])

        def relay(j, kind):
            landed(j, kind).wait_recv()
            passed(j, kind, c).start()

        @pl.when(i == 0)
        def _():
            @pl.when(jj == 0)
            def _():
                load_h().start()
                for kind in range(n_kinds):
                    for j in range(3):
                        ici(j, kind).start()
                fetch(0, 0).start()
                load_h().wait()

            for n in range(3 * n_chunks):
                at = nf + n * tpc
                pl.when(jj == at - 2)(lambda n=n: relay(n % 3, n // 3))
                pl.when(jj == at - 1)(lambda n=n: passed(n % 3, n // 3, 1 - c).wait_recv())

            @pl.when(jj == rem_at - 2)
            def _():
                for j in range(3):
                    relay(j, n_chunks)

            @pl.when(jj == rem_at - 1)
            def _():
                for j in range(3):
                    passed(j, n_chunks, 1 - c).wait_recv()

            @pl.when(jj + 1 < n_tiles)
            def _():
                fetch(jj + 1, (jj + 1) % 2).start()

            fetch(jj, jj % 2).wait()

            @pl.when(jj == n_tiles - 1)
            def _():
                for kind in range(n_kinds):
                    for j in range(3):
                        ici(j, kind).wait_send()
                        passed(j, kind, c).wait_send()

        rows = pl.ds(pl.multiple_of(i * tm, tm), tm)
        proj_ref[...] = _dot(hbuf[rows, :], bbuf[jj % 2])

    gs = pltpu.PrefetchScalarGridSpec(
        num_scalar_prefetch=1, grid=(n_tiles, n_i),
        in_specs=[HBM_SPEC, HBM_SPEC],
        out_specs=[pl.BlockSpec((tm, COL_TILE), lambda jj, i, tb: (i, tb[0, jj])), HBM_SPEC],
        scratch_shapes=[pltpu.VMEM((t, d), BF16), pltpu.VMEM((2, d, COL_TILE), BF16), pltpu.SemaphoreType.DMA((3,)),
                        pltpu.SemaphoreType.DMA((2 * n_sems,)), pltpu.SemaphoreType.DMA((2 * n_sems,))])
    return pl.pallas_call(
        body, name="proj_gather_mm", grid_spec=gs,
        out_shape=[jax.ShapeDtypeStruct((t, n_tiles * COL_TILE), F32), jax.ShapeDtypeStruct(wi_part.shape, BF16)],
        input_output_aliases={2: 1},
        compiler_params=_params(dimension_semantics=("arbitrary", "arbitrary")),
    )(tab, h, wi_part)


def _gw_in_mm(name, h, dproj, perm, half, rider=None):
    t, d = h.shape
    n_tiles = len(perm)
    hd = d // 2
    tm = min(hd, 1024)
    per_half = hd // tm
    table = jnp.concatenate([jnp.asarray(perm), jnp.reshape(half, (1,)).astype(jnp.int32)])
    return _matmul(
        name, h, dproj, perm=table, rider=rider, ta=True, grid=(per_half, n_tiles, 1),
        out_shape=jax.ShapeDtypeStruct((hd, n_tiles * COL_TILE), BF16),
        a_spec=pl.BlockSpec((t, tm), lambda i, j, k, p: (0, p[n_tiles] * per_half + i)),
        b_spec=pl.BlockSpec((t, COL_TILE), lambda i, j, k, p: (0, j)),
        o_spec=pl.BlockSpec((tm, COL_TILE), lambda i, j, k, p: (i, p[j])))


def _dh_mm(dproj, w_full, perm, rider=None):
    t = dproj.shape[0]
    d = w_full.shape[0]
    n_tiles = len(perm)
    tm = min(t, 2048)
    tn = min(d, 2048)
    return _matmul(
        "dh_mm", dproj, w_full, perm=perm, rider=rider, nt=True, grid=(t // tm, d // tn, n_tiles),
        out_shape=jax.ShapeDtypeStruct((t, d), F32),
        a_spec=pl.BlockSpec((tm, COL_TILE), lambda i, j, k, p: (i, k)),
        b_spec=pl.BlockSpec((tn, COL_TILE), lambda i, j, k, p: (j, p[k])),
        o_spec=pl.BlockSpec((tm, tn), lambda i, j, k, p: (i, j)))


def _out_mm(mixed, w_out_full):
    t, dm = mixed.shape
    d = w_out_full.shape[1]
    tm = min(t, 1024)
    tn = min(d, 512)
    tk = min(dm, 4096)
    return _matmul(
        "out_mm", mixed, w_out_full, grid=(t // tm, d // tn, dm // tk),
        out_shape=jax.ShapeDtypeStruct((t, d), F32),
        a_spec=pl.BlockSpec((tm, tk), lambda i, j, k: (i, k)),
        b_spec=pl.BlockSpec((tk, tn), lambda i, j, k: (k, j)),
        o_spec=pl.BlockSpec((tm, tn), lambda i, j, k: (i, j)))[0]


def _dmixed_mm(dy, w_out_full, rider=None):
    t, d = dy.shape
    dm = w_out_full.shape[0]
    tm = min(t, 1024)
    tn = min(dm, 1024)
    return _matmul(
        "dmixed_mm", dy, w_out_full, nt=True, rider=rider, grid=(t // tm, dm // tn, 1),
        out_shape=jax.ShapeDtypeStruct((t, dm), F32),
        a_spec=pl.BlockSpec((tm, d), lambda i, j, k: (i, 0)),
        b_spec=pl.BlockSpec((tn, d), lambda i, j, k: (j, 0)),
        o_spec=pl.BlockSpec((tm, tn), lambda i, j, k: (i, j)))


def _gw_out_mm(mixed, dy):
    t, dm = mixed.shape
    d = dy.shape[1]
    hr = dm // (2 * N_CHIPS)
    tn = min(d, 1024)
    return _matmul(
        "gw_out_mm", mixed, dy, ta=True, grid=(dm // hr, d // tn, 1),
        out_shape=jax.ShapeDtypeStruct((2, N_CHIPS, hr, d), BF16),
        a_spec=pl.BlockSpec((t, hr), lambda i, j, k: (0, i)),
        b_spec=pl.BlockSpec((t, tn), lambda i, j, k: (0, j)),
        o_spec=pl.BlockSpec((None, None, hr, tn), lambda i, j, k: (i % 2, i // 2, 0, j)))[0]


def _lane_half():
    return lax.broadcasted_iota(jnp.int32, (WINDOW, LANES), 1) // ATTN_HEAD_DIM


def _dup_kv(tile, kh):
    return jnp.where(_lane_half() == kh, tile, pltpu.roll(tile, ATTN_HEAD_DIM, 1))


def _stack_heads(tiles, kh):
    half = _lane_half()
    pieces = []
    for g in range(GQA_GROUP):
        pieces.append(jnp.where(half == g % 2, tiles[4 * kh + g // 2], 0.0))
    return jnp.concatenate(pieces, axis=0)


def _unstack_heads(stacked):
    half = _lane_half()
    out = []
    for j in range(GQA_GROUP // 2):
        a = stacked[(2 * j) * WINDOW:(2 * j + 1) * WINDOW]
        b = stacked[(2 * j + 1) * WINDOW:(2 * j + 2) * WINDOW]
        out.append(jnp.where(half == 0, a, b))
    return out


def _attn_probs(qs, kcat, sink_col, n):
    rows = GQA_GROUP * WINDOW
    s = _dot_nt(qs, kcat)
    qi = lax.broadcasted_iota(jnp.int32, (rows, 2 * WINDOW), 0) % WINDOW
    kj = lax.broadcasted_iota(jnp.int32, (rows, 2 * WINDOW), 1)
    first_key = WINDOW * (1 - jnp.minimum(n, 1))
    valid = (kj > qi) & (kj <= qi + WINDOW) & (kj >= first_key)
    s = jnp.where(valid, s, NEG_BIG)
    mx = jnp.maximum(jnp.max(s, axis=-1, keepdims=True), sink_col)
    p = jnp.exp(s - mx)
    p_sink = jnp.exp(sink_col - mx)
    inv = 1.0 / (jnp.sum(p, axis=-1, keepdims=True) + p_sink)
    return p * inv, p_sink * inv


def _attn_operands(sink_ref, q_tiles, k_prev, k_cur, v_prev, v_cur, m, kh):
    qs = _stack_heads([qt * ATTN_SCALE for qt in q_tiles], kh).astype(BF16)
    kcat = jnp.concatenate([_dup_kv(k_prev, kh), _dup_kv(k_cur, kh)], axis=0).astype(BF16)
    vcat = jnp.concatenate([_dup_kv(v_prev, kh), _dup_kv(v_cur, kh)], axis=0).astype(BF16)
    heads_per_group = ATTN_GROUP_LANES // ATTN_HEAD_DIM
    sink_col = jnp.concatenate(
        [jnp.full((WINDOW, 1), sink_ref[0, m * heads_per_group + kh * GQA_GROUP + g], F32)
         for g in range(GQA_GROUP)], axis=0)
    return qs, kcat, vcat, sink_col


def _attn_specs(lay, d):
    a_blk = lay["a_off"] // (2 * ATTN_GROUP_LANES)
    k_blk = lay["k_off"] // LANES
    v_blk = lay["v_off"] // LANES
    before = lambda n: jnp.maximum(ATTN_STEP_BLOCKS * n - 1, 0)
    qg = pl.BlockSpec((ATTN_STEP_ROWS, 2 * ATTN_GROUP_LANES), lambda m, n: (n, a_blk + m))
    kp = pl.BlockSpec((WINDOW, LANES), lambda m, n: (before(n), k_blk + m))
    kc = pl.BlockSpec((ATTN_STEP_ROWS, LANES), lambda m, n: (n, k_blk + m))
    vp = pl.BlockSpec((WINDOW, LANES), lambda m, n: (before(n), v_blk + m))
    vc = pl.BlockSpec((ATTN_STEP_ROWS, LANES), lambda m, n: (n, v_blk + m))
    return qg, kp, kc, vp, vc


def _block_rows(b):
    return slice(b * WINDOW, (b + 1) * WINDOW)


def _kv_tiles(prev_ref, cur_ref, b):
    prev = prev_ref[...] if b == 0 else cur_ref[_block_rows(b - 1), :]
    return prev, cur_ref[_block_rows(b), :]


def _attn_fwd(proj, sinks, lay, d, rider=None):
    t = proj.shape[0]
    n_groups = d // ATTN_GROUP_LANES
    n_blocks = t // ATTN_STEP_ROWS
    pairs = ATTN_GROUP_LANES // LANES
    n_rin = 0 if rider is None else len(rider.operands)
    n_rout = 0 if rider is None else len(rider.out_shapes)

    def body(*refs):
        sink_ref, qg_ref, kp_ref, kc_ref, vp_ref, vc_ref = refs[:6]
        rin = refs[6:6 + n_rin]
        mix_ref, o_ref = refs[6 + n_rin:8 + n_rin]
        rout = refs[8 + n_rin:8 + n_rin + n_rout]
        m = pl.program_id(0)
        n = pl.program_id(1)
        if rider is not None:
            rider.emit(m * n_blocks + n, n_groups * n_blocks, rin, rout, refs[-2], refs[-1])
        for b in range(ATTN_STEP_BLOCKS):
            rows = _block_rows(b)
            k_prev, k_cur = _kv_tiles(kp_ref, kc_ref, b)
            v_prev, v_cur = _kv_tiles(vp_ref, vc_ref, b)
            q_tiles = [qg_ref[rows, p * LANES:(p + 1) * LANES] for p in range(pairs)]
            for kh in range(2):
                qs, kcat, vcat, sink_col = _attn_operands(sink_ref, q_tiles, k_prev, k_cur, v_prev, v_cur, m, kh)
                probs, _ = _attn_probs(qs, kcat, sink_col, ATTN_STEP_BLOCKS * n + b)
                out = _dot(probs.astype(BF16), vcat)
                for j, tile in enumerate(_unstack_heads(out)):
                    p = 4 * kh + j
                    lanes = slice(p * LANES, (p + 1) * LANES)
                    gate = qg_ref[rows, ATTN_GROUP_LANES + p * LANES:ATTN_GROUP_LANES + (p + 1) * LANES]
                    o_ref[rows, lanes] = tile
                    mix_ref[rows, lanes] = (tile * (gate * _sigmoid(gate))).astype(BF16)

    qg, kp, kc, vp, vc = _attn_specs(lay, d)
    out_blk = pl.BlockSpec((ATTN_STEP_ROWS, ATTN_GROUP_LANES), lambda m, n: (n, m))
    out_shapes = [jax.ShapeDtypeStruct((t, 2 * d), BF16), jax.ShapeDtypeStruct((t, d), F32)]
    operands = [sinks, proj, proj, proj, proj, proj]
    scratch, aliases, sem = [], {}, ("parallel", "parallel")
    if rider is not None:
        scratch = rider.scratch()
        out_shapes += list(rider.out_shapes)
        operands += list(rider.operands)
        aliases = {6 + i: 2 + o for i, o in rider.aliases.items()}
        sem = ("arbitrary", "arbitrary")
    return pl.pallas_call(
        body, name="attn_fwd", grid=(n_groups, n_blocks),
        in_specs=[pl.BlockSpec(memory_space=pltpu.SMEM), qg, kp, kc, vp, vc] + [HBM_SPEC] * n_rin,
        out_specs=[out_blk, out_blk] + [HBM_SPEC] * n_rout,
        out_shape=out_shapes, scratch_shapes=scratch, input_output_aliases=aliases,
        compiler_params=_params(dimension_semantics=sem),
    )(*operands)


def _attn_bwd(proj, sinks, attn_o, dmixed, dproj, lay, d):
    t = proj.shape[0]
    n_groups = d // ATTN_GROUP_LANES
    pairs = ATTN_GROUP_LANES // LANES
    kv_w = n_groups * LANES

    def body(sink_ref, qg_ref, kp_ref, kc_ref, vp_ref, vc_ref, o_ref, dm_ref, dproj_hbm,
             dqg_ref, dkc_ref, dkp_ref, dvc_ref, dvp_ref, dsink_ref):
        del dproj_hbm
        m = pl.program_id(0)
        n = pl.program_id(1)
        half = _lane_half()
        sub = lax.broadcasted_iota(jnp.int32, (8, LANES), 0)
        lane = lax.broadcasted_iota(jnp.int32, (8, LANES), 1)
        dsink = jnp.zeros((8, LANES), F32)
        for b in range(ATTN_STEP_BLOCKS):
            rows = _block_rows(b)
            k_prev, k_cur = _kv_tiles(kp_ref, kc_ref, b)
            v_prev, v_cur = _kv_tiles(vp_ref, vc_ref, b)
            q_tiles = [qg_ref[rows, p * LANES:(p + 1) * LANES] for p in range(pairs)]
            do_tiles, o_tiles = [], []
            for p in range(pairs):
                lanes = slice(p * LANES, (p + 1) * LANES)
                gate_lanes = slice(ATTN_GROUP_LANES + p * LANES, ATTN_GROUP_LANES + (p + 1) * LANES)
                gate = qg_ref[rows, gate_lanes]
                sg = _sigmoid(gate)
                dmix = dm_ref[rows, lanes]
                ov = o_ref[rows, lanes]
                dqg_ref[rows, gate_lanes] = (dmix * ov * (sg * (1.0 + gate * (1.0 - sg)))).astype(BF16)
                do_tiles.append(dmix * (gate * sg))
                o_tiles.append(ov)

            dk_cur = dk_prev = dv_cur = dv_prev = jnp.zeros((WINDOW, LANES), F32)
            for kh in range(2):
                qs, kcat, vcat, sink_col = _attn_operands(sink_ref, q_tiles, k_prev, k_cur, v_prev, v_cur, m, kh)
                probs, p_sink = _attn_probs(qs, kcat, sink_col, ATTN_STEP_BLOCKS * n + b)
                dos = _stack_heads(do_tiles, kh)
                delta = jnp.sum(dos * _stack_heads(o_tiles, kh), axis=-1, keepdims=True)
                dos = dos.astype(BF16)
                dp = _dot_nt(dos, vcat)
                ds = (probs * (dp - delta)).astype(BF16)
                dv = _dot_tn(probs.astype(BF16), dos)
                dv = dv + pltpu.roll(dv, ATTN_HEAD_DIM, 1)
                dk = _dot_tn(ds, qs)
                dk = dk + pltpu.roll(dk, ATTN_HEAD_DIM, 1)
                dq = _dot(ds, kcat)
                for j, tile in enumerate(_unstack_heads(dq)):
                    p = 4 * kh + j
                    dqg_ref[rows, p * LANES:(p + 1) * LANES] = (tile * ATTN_SCALE).astype(BF16)
                dk_prev = jnp.where(half == kh, dk[:WINDOW], dk_prev)
                dk_cur = jnp.where(half == kh, dk[WINDOW:], dk_cur)
                dv_prev = jnp.where(half == kh, dv[:WINDOW], dv_prev)
                dv_cur = jnp.where(half == kh, dv[WINDOW:], dv_cur)
                sink_terms = p_sink * delta
                for g in range(GQA_GROUP):
                    val = -jnp.sum(sink_terms[g * WINDOW:(g + 1) * WINDOW])
                    dsink = dsink + jnp.where((sub == 0) & (lane == kh * GQA_GROUP + g), val, 0.0)
            dkc_ref[rows, :] = dk_cur
            dkp_ref[rows, :] = dk_prev
            dvc_ref[rows, :] = dv_cur
            dvp_ref[rows, :] = dv_prev

        @pl.when(n == 0)
        def _():
            dsink_ref[...] = jnp.zeros_like(dsink_ref)

        dsink_ref[...] += dsink

    qg, kp, kc, vp, vc = _attn_specs(lay, d)
    a_blk = lay["a_off"] // (2 * ATTN_GROUP_LANES)
    grp = pl.BlockSpec((ATTN_STEP_ROWS, ATTN_GROUP_LANES), lambda m, n: (n, m))
    kv_blk = pl.BlockSpec((ATTN_STEP_ROWS, LANES), lambda m, n: (n, m))
    kv_shape = jax.ShapeDtypeStruct((t, kv_w), F32)
    outs = pl.pallas_call(
        body, name="attn_bwd", grid=(n_groups, t // ATTN_STEP_ROWS),
        in_specs=[pl.BlockSpec(memory_space=pltpu.SMEM), qg, kp, kc, vp, vc, grp, grp,
                  pl.BlockSpec(memory_space=pl.ANY)],
        out_specs=[pl.BlockSpec((ATTN_STEP_ROWS, 2 * ATTN_GROUP_LANES), lambda m, n: (n, a_blk + m)),
                   kv_blk, kv_blk, kv_blk, kv_blk, pl.BlockSpec((8, LANES), lambda m, n: (m, 0))],
        out_shape=[jax.ShapeDtypeStruct(dproj.shape, BF16), kv_shape, kv_shape, kv_shape, kv_shape,
                   jax.ShapeDtypeStruct((n_groups * 8, LANES), F32)],
        input_output_aliases={8: 0},
        compiler_params=_params(dimension_semantics=("parallel", "arbitrary")),
    )(sinks, proj, proj, proj, proj, proj, attn_o, dmixed, dproj)
    return outs


def _kv_combine(dkc, dkp, dvc, dvp, dproj, lay):
    t, kv_w = dkc.shape
    nb = t // WINDOW
    kv_blk_idx = lay["k_off"] // (2 * kv_w)

    def body(dkc_ref, dkp_ref, dvc_ref, dvp_ref, dproj_hbm, o_ref):
        del dproj_hbm
        keep = (pl.program_id(0) < nb - 1).astype(F32)
        o_ref[:, :kv_w] = (dkc_ref[...] + keep * dkp_ref[...]).astype(BF16)
        o_ref[:, kv_w:] = (dvc_ref[...] + keep * dvp_ref[...]).astype(BF16)

    cur = pl.BlockSpec((WINDOW, kv_w), lambda n: (n, 0))
    nxt = pl.BlockSpec((WINDOW, kv_w), lambda n: (jnp.minimum(n + 1, nb - 1), 0))
    return pl.pallas_call(
        body, name="kv_combine", grid=(nb,),
        in_specs=[cur, nxt, cur, nxt, pl.BlockSpec(memory_space=pl.ANY)],
        out_specs=pl.BlockSpec((WINDOW, 2 * kv_w), lambda n: (n, kv_blk_idx)),
        out_shape=jax.ShapeDtypeStruct(dproj.shape, BF16),
        input_output_aliases={4: 0},
        compiler_params=_params(dimension_semantics=("parallel",)),
    )(dkc, dkp, dvc, dvp, dproj)


def _lower_bound(lbl_ref):
    l0 = lbl_ref[0:1, :]
    l1 = lbl_ref[1:2, :]
    mx = jnp.maximum(l0, l1)
    e0 = jnp.exp(l0 - mx)
    e1 = jnp.exp(l1 - mx)
    return e0 / (e0 + e1)


def _chunk_masks():
    ti = lax.broadcasted_iota(jnp.int32, (CHUNK, CHUNK), 0)
    si = lax.broadcasted_iota(jnp.int32, (CHUNK, CHUNK), 1)
    diag = ((ti // HALF_CHUNK) == (si // HALF_CHUNK)) & (si <= ti)
    off = (ti >= HALF_CHUNK) & (si < HALF_CHUNK)
    lower = (si <= ti).astype(BF16)
    upper = (si >= ti).astype(BF16)
    return diag, off, lower, upper


def _rnn_gates(rq, rf, lb):
    sf = _sigmoid(rf)
    f = lb + (1.0 - lb) * sf
    sq = _sigmoid(rq)
    return sf, f, jnp.log(f), 1.0 - f, sq, rq * sq


def _rnn_decays(g_cum):
    row = lax.broadcasted_iota(jnp.int32, g_cum.shape, 0)
    ref_d = jnp.where(row < HALF_CHUNK, g_cum[HALF_CHUNK // 2 - 1:HALF_CHUNK // 2],
                      g_cum[HALF_CHUNK + HALF_CHUNK // 2 - 1:HALF_CHUNK + HALF_CHUNK // 2])
    ref_o = g_cum[HALF_CHUNK - 1:HALF_CHUNK]
    last = g_cum[CHUNK - 1:CHUNK]
    return dict(eq_d=jnp.exp(g_cum - ref_d), ek_d=jnp.exp(ref_d - g_cum),
                eq_o=jnp.exp(jnp.minimum(g_cum - ref_o, 0.0)), ek_o=jnp.exp(jnp.minimum(ref_o - g_cum, 0.0)),
                eg=jnp.exp(g_cum), ekl=jnp.exp(last - g_cum), e_last=jnp.exp(last))


def _head(a, j):
    return a[:, j * RNN_HEAD_DIM:(j + 1) * RNN_HEAD_DIM]


def _rnn_specs(t, tb, d):
    gw = RNN_GROUP_HEADS * RNN_HEAD_DIM
    return gw, t // tb, tb // CHUNK


def _rnn_fwd(proj, lb_logits, rnn_gain, mixed, d, rider=None):
    t = proj.shape[0]
    tb = min(t, RNN_STEP_ROWS)
    gw, ntb, nch = _rnn_specs(t, tb, d)
    n_groups = d // gw
    n_heads = d // RNN_HEAD_DIM
    n_rin = 0 if rider is None else len(rider.operands)
    n_rout = 0 if rider is None else len(rider.out_shapes)

    def body(*refs):
        blk_ref, lbl_ref, gain_ref = refs[:3]
        rin = refs[4:4 + n_rin]
        mix_ref, o_ref, st_out_ref = refs[4 + n_rin:7 + n_rin]
        rout = refs[7 + n_rin:7 + n_rin + n_rout]
        st_ref = refs[7 + n_rin + n_rout]
        if rider is not None:
            rider.emit(pl.program_id(0) * ntb + pl.program_id(1), n_groups * ntb, rin, rout, refs[-2], refs[-1])

        @pl.when(pl.program_id(1) == 0)
        def _():
            st_ref[...] = jnp.zeros_like(st_ref)

        lb = _lower_bound(lbl_ref)
        gain = gain_ref[...]
        diag, off, lower, _ = _chunk_masks()

        def chunk(c, carry):
            rows = pl.ds(pl.multiple_of(c * CHUNK, CHUNK), CHUNK)
            rq = blk_ref[rows, 0:gw]
            rf = blk_ref[rows, gw:2 * gw]
            v = blk_ref[rows, 2 * gw:3 * gw]
            rg = blk_ref[rows, 3 * gw:4 * gw]
            _, _, g, k, _, q = _rnn_gates(rq, rf, lb)
            dec = _rnn_decays(_tri_dot(lower, g))
            qd = (q * dec["eq_d"]).astype(BF16)
            kd = (k * dec["ek_d"]).astype(BF16)
            qo = (q * dec["eq_o"]).astype(BF16)
            ko = (k * dec["ek_o"]).astype(BF16)
            qe = (q * dec["eg"]).astype(BF16)
            kl = (k * dec["ekl"]).astype(BF16)
            vb = v.astype(BF16)
            outs = []
            for j in range(RNN_GROUP_HEADS):
                st = st_ref[j]
                st_out_ref[j, c] = st
                attn = jnp.where(diag, _dot_nt(_head(qd, j), _head(kd, j)),
                                 jnp.where(off, _dot_nt(_head(qo, j), _head(ko, j)), 0.0))
                o = _dot(attn.astype(BF16), _head(vb, j)) + _dot_nt(_head(qe, j), st.astype(BF16))
                st_ref[j] = st * _head(dec["e_last"], j) + _dot_tn(_head(vb, j), _head(kl, j))
                rr = lax.rsqrt(jnp.mean(o * o, axis=-1, keepdims=True) + NORM_EPS)
                o_ref[rows, j * RNN_HEAD_DIM:(j + 1) * RNN_HEAD_DIM] = o
                outs.append(o * rr)
            on = jnp.concatenate(outs, axis=1) * gain
            mix_ref[rows, :] = (on * (rg * _sigmoid(rg))).astype(BF16)
            return carry

        lax.fori_loop(0, nch, chunk, 0, unroll=True)

    out_shapes = [jax.ShapeDtypeStruct(mixed.shape, BF16), jax.ShapeDtypeStruct((t, d), F32),
                  jax.ShapeDtypeStruct((n_heads, t // CHUNK, RNN_HEAD_DIM, RNN_HEAD_DIM), F32)]
    operands = [proj, lb_logits, rnn_gain, mixed]
    scratch = [pltpu.VMEM((RNN_GROUP_HEADS, RNN_HEAD_DIM, RNN_HEAD_DIM), F32)]
    aliases, sem = {3: 0}, ("parallel", "arbitrary")
    if rider is not None:
        scratch += rider.scratch()
        out_shapes += list(rider.out_shapes)
        operands += list(rider.operands)
        aliases.update({4 + i: 3 + o for i, o in rider.aliases.items()})
        sem = ("arbitrary", "arbitrary")
    return pl.pallas_call(
        body, name="rnn_fwd", grid=(n_groups, ntb),
        in_specs=[pl.BlockSpec((tb, 4 * gw), lambda h, i: (i, h)),
                  pl.BlockSpec((2, gw), lambda h, i: (0, h)),
                  pl.BlockSpec((1, gw), lambda h, i: (0, h)),
                  pl.BlockSpec(memory_space=pl.ANY)] + [HBM_SPEC] * n_rin,
        out_specs=[pl.BlockSpec((tb, gw), lambda h, i: (i, d // gw + h)),
                   pl.BlockSpec((tb, gw), lambda h, i: (i, h)),
                   pl.BlockSpec((RNN_GROUP_HEADS, nch, RNN_HEAD_DIM, RNN_HEAD_DIM), lambda h, i: (h, i, 0, 0))]
        + [HBM_SPEC] * n_rout,
        out_shape=out_shapes, scratch_shapes=scratch, input_output_aliases=aliases,
        compiler_params=_params(dimension_semantics=sem),
    )(*operands)


def _rnn_bwd(proj, lb_logits, rnn_gain, o_pre, states, dmixed, d_total, d, rider=None):
    t = proj.shape[0]
    tb = min(t, RNN_STEP_ROWS)
    gw, ntb, nch = _rnn_specs(t, tb, d)
    n_groups = d // gw
    n_rin = 0 if rider is None else len(rider.operands)
    n_rout = 0 if rider is None else len(rider.out_shapes)

    def body(*refs):
        blk_ref, lbl_ref, gain_ref, o_ref, st_in_ref, dm_ref = refs[:6]
        rin = refs[6:6 + n_rin]
        dproj_ref, dgain_ref, dlb_ref = refs[6 + n_rin:9 + n_rin]
        rout = refs[9 + n_rin:9 + n_rin + n_rout]
        dst_ref = refs[9 + n_rin + n_rout]
        if rider is not None:
            rider.emit(pl.program_id(0) * ntb + pl.program_id(1), n_groups * ntb, rin, rout, refs[-2], refs[-1])

        @pl.when(pl.program_id(1) == 0)
        def _():
            dst_ref[...] = jnp.zeros_like(dst_ref)
            dgain_ref[...] = jnp.zeros_like(dgain_ref)
            dlb_ref[...] = jnp.zeros_like(dlb_ref)

        lb = _lower_bound(lbl_ref)
        gain = gain_ref[...]
        diag, off, lower, upper = _chunk_masks()
        last_row = lax.broadcasted_iota(jnp.int32, (CHUNK, RNN_HEAD_DIM), 0) == CHUNK - 1

        def chunk(step, carry):
            c = nch - 1 - step
            rows = pl.ds(pl.multiple_of(c * CHUNK, CHUNK), CHUNK)
            rq = blk_ref[rows, 0:gw]
            rf = blk_ref[rows, gw:2 * gw]
            v = blk_ref[rows, 2 * gw:3 * gw]
            rg = blk_ref[rows, 3 * gw:4 * gw]
            sf, f, g, k, sq, q = _rnn_gates(rq, rf, lb)
            dec = _rnn_decays(_tri_dot(lower, g))
            qd = (q * dec["eq_d"]).astype(BF16)
            kd = (k * dec["ek_d"]).astype(BF16)
            qo = (q * dec["eq_o"]).astype(BF16)
            ko = (k * dec["ek_o"]).astype(BF16)
            qe = (q * dec["eg"]).astype(BF16)
            kl = (k * dec["ekl"]).astype(BF16)
            vb = v.astype(BF16)

            o = o_ref[rows, :]
            dmix = dm_ref[rows, :]
            sg = _sigmoid(rg)
            n_parts = []
            for j in range(RNN_GROUP_HEADS):
                oj = _head(o, j)
                n_parts.append(oj * lax.rsqrt(jnp.mean(oj * oj, axis=-1, keepdims=True) + NORM_EPS))
            nrm = jnp.concatenate(n_parts, axis=1)
            d_on = dmix * (rg * sg)
            d_rg = dmix * (nrm * gain) * (sg * (1.0 + rg * (1.0 - sg)))
            dgain_ref[...] += jnp.sum(d_on * nrm, axis=0, keepdims=True)
            dn = d_on * gain

            dq_parts, dk_parts, dv_parts, dg_parts = [], [], [], []
            for j in range(RNN_GROUP_HEADS):
                oj, nj, dnj = _head(o, j), _head(nrm, j), _head(dn, j)
                rr = lax.rsqrt(jnp.mean(oj * oj, axis=-1, keepdims=True) + NORM_EPS)
                do = (rr * (dnj - nj * jnp.mean(dnj * nj, axis=-1, keepdims=True))).astype(BF16)
                st = st_in_ref[j, c]
                dst = dst_ref[j]
                stb, dstb = st.astype(BF16), dst.astype(BF16)
                qdj, kdj, qoj, koj = _head(qd, j), _head(kd, j), _head(qo, j), _head(ko, j)
                attn = jnp.where(diag, _dot_nt(qdj, kdj), jnp.where(off, _dot_nt(qoj, koj), 0.0))
                dattn = _dot_nt(do, _head(vb, j))
                da_d = jnp.where(diag, dattn, 0.0).astype(BF16)
                da_o = jnp.where(off, dattn, 0.0).astype(BF16)
                dv = _dot_tn(attn.astype(BF16), do) + _dot_nt(_head(kl, j), dstb)
                dq_inter = _dot(do, stb) * _head(dec["eg"], j)
                dq_d, dq_o = _dot(da_d, kdj), _dot(da_o, koj)
                dq = dq_inter + dq_d * _head(dec["eq_d"], j) + dq_o * _head(dec["eq_o"], j)
                dk_inter = _dot(_head(vb, j), dstb) * _head(dec["ekl"], j)
                dk_d, dk_o = _dot_tn(da_d, qdj), _dot_tn(da_o, qoj)
                dk = dk_inter + dk_d * _head(dec["ek_d"], j) + dk_o * _head(dec["ek_o"], j)
                kj, qj = _head(k, j), _head(q, j)
                e_last = _head(dec["e_last"], j)
                extra = (jnp.sum(kj * dk_inter, axis=0, keepdims=True)
                         + e_last * jnp.sum(st * dst, axis=0, keepdims=True))
                dg_cum = (qj * dq_inter - kj * dk_inter
                          + (qdj.astype(F32) * dq_d + qoj.astype(F32) * dq_o)
                          - (kdj.astype(F32) * dk_d + koj.astype(F32) * dk_o))
                dg_parts.append(jnp.where(last_row, dg_cum + extra, dg_cum))
                dst_ref[j] = dst * e_last + _dot_tn(do, _head(qe, j))
                dq_parts.append(dq)
                dk_parts.append(dk)
                dv_parts.append(dv)

            dq = jnp.concatenate(dq_parts, axis=1)
            dk = jnp.concatenate(dk_parts, axis=1)
            dg = _tri_dot(upper, jnp.concatenate(dg_parts, axis=1))
            df = dg / f - dk
            dlb_ref[...] += jnp.sum(df * (1.0 - sf), axis=0, keepdims=True)
            d_rf = df * (1.0 - lb) * (sf * (1.0 - sf))
            d_rq = dq * (sq * (1.0 + rq * (1.0 - sq)))
            dproj_ref[rows, 0:gw] = d_rq.astype(BF16)
            dproj_ref[rows, gw:2 * gw] = d_rf.astype(BF16)
            dproj_ref[rows, 2 * gw:3 * gw] = jnp.concatenate(dv_parts, axis=1).astype(BF16)
            dproj_ref[rows, 3 * gw:4 * gw] = d_rg.astype(BF16)
            return carry

        lax.fori_loop(0, nch, chunk, 0, unroll=True)

    rev = lambda i: ntb - 1 - i
    vec = pl.BlockSpec((1, gw), lambda h, i: (0, h))
    scratch = [pltpu.VMEM((RNN_GROUP_HEADS, RNN_HEAD_DIM, RNN_HEAD_DIM), F32)]
    out_shapes = [jax.ShapeDtypeStruct((t, d_total), BF16), jax.ShapeDtypeStruct((1, d), F32),
                  jax.ShapeDtypeStruct((1, d), F32)]
    operands = [proj, lb_logits, rnn_gain, o_pre, states, dmixed]
    sem = ("parallel", "arbitrary")
    if rider is not None:
        scratch += rider.scratch()
        out_shapes += list(rider.out_shapes)
        operands += list(rider.operands)
        sem = ("arbitrary", "arbitrary")
    return pl.pallas_call(
        body, name="rnn_bwd", grid=(n_groups, ntb),
        in_specs=[pl.BlockSpec((tb, 4 * gw), lambda h, i: (rev(i), h)),
                  pl.BlockSpec((2, gw), lambda h, i: (0, h)), vec,
                  pl.BlockSpec((tb, gw), lambda h, i: (rev(i), h)),
                  pl.BlockSpec((RNN_GROUP_HEADS, nch, RNN_HEAD_DIM, RNN_HEAD_DIM), lambda h, i: (h, rev(i), 0, 0)),
                  pl.BlockSpec((tb, gw), lambda h, i: (rev(i), d // gw + h))] + [HBM_SPEC] * n_rin,
        out_specs=[pl.BlockSpec((tb, 4 * gw), lambda h, i: (rev(i), h)), vec, vec] + [HBM_SPEC] * n_rout,
        out_shape=out_shapes, scratch_shapes=scratch,
        compiler_params=_params(dimension_semantics=sem),
    )(*operands)


def _local_grads(x, target, w_in_full, w_out, sinks, lb_logits, rnn_gain, pre_gain, post_gain, sc=None):
    t, d = x.shape
    comm = sc is not None
    lay = _layout(d)
    perm = lay["perm"]
    h = _prenorm_fwd(x, pre_gain)
    if comm:
        proj, w_in_full = _proj_gather_mm(h, w_in_full, perm, sc)
        mixed, attn_o, w_out = _attn_fwd(proj, sinks, lay, d, _gather_rider(w_out, 0.7, 0, 2))
        mixed, o_pre, states, w_out_full = _rnn_fwd(proj, lb_logits, rnn_gain, mixed, d,
                                                    _gather_rider(w_out, 0.7, 1, 2))
    else:
        (proj,) = _proj_mm(h, w_in_full, perm)
        w_out_full = w_out
        mixed, attn_o = _attn_fwd(proj, sinks, lay, d)
        mixed, o_pre, states = _rnn_fwd(proj, lb_logits, rnn_gain, mixed, d)
    y = _out_mm(mixed, w_out_full)
    dy, dz, g_post, loss = _post_loss(x, y, target, post_gain)
    gw_out = _gw_out_mm(mixed, dy)
    rider = None
    if comm:
        dmixed, recv_out = _dmixed_mm(dy, w_out_full, _pair_exchange_rider(gw_out, stacked=True))
        p_out = _pair_sum_out(gw_out, recv_out)
        rider = _chip_exchange_rider(p_out, lambda ref, chip: ref.at[chip])
    else:
        (dmixed,) = _dmixed_mm(dy, w_out_full)
    dproj, g_rnn, g_lb, *r_out = _rnn_bwd(proj, lb_logits, rnn_gain, o_pre, states, dmixed, lay["total"], d, rider)
    dproj, dkc, dkp, dvc, dvp, dsink = _attn_bwd(proj, sinks, attn_o, dmixed, dproj, lay, d)
    dproj = _kv_combine(dkc, dkp, dvc, dvp, dproj, lay)
    if comm:
        c = lax.axis_index("c")
        (g_other,) = _gw_in_mm("gw_in_mm_other", h, dproj, perm, 1 - c)
        g_mine, recv_in = _gw_in_mm("gw_in_mm_mine", h, dproj, perm, c, _pair_exchange_rider(g_other))
        p_in = _pair_sum_in(g_mine, recv_in, sc)
        rider = _chip_exchange_rider(
            p_in, lambda ref, chip: ref.at[:, pl.ds(pl.multiple_of(chip * sc, LANES), sc)])
    dh, *r_in = _dh_mm(dproj, w_in_full, perm, rider)
    grad_x, g_pre = _prenorm_bwd(x, dh, dz, pre_gain)
    heads_per_group = ATTN_GROUP_LANES // ATTN_HEAD_DIM
    g_sink = dsink.reshape(d // ATTN_GROUP_LANES, 8, LANES)[:, 0, :heads_per_group].reshape(1, -1)
    small = dict(sink=g_sink, lb=g_lb, rnn=g_rnn, pre=g_pre, post=g_post)
    if comm:
        return loss, grad_x, (p_in, r_in[0]), (p_out, r_out[0]), small
    gw_in = jnp.stack([_gw_in_mm("gw_in_mm_%d" % half, h, dproj, perm, half)[0] for half in range(2)])
    return loss, grad_x, gw_in, gw_out, small


def _mesh_pos():
    x, y, c = lax.axis_index("x"), lax.axis_index("y"), lax.axis_index("c")
    chips = [(1 - x, y), (x, 1 - y), (1 - x, 1 - y)]
    return x, y, c, chips


def _remote(src, dst, send_sem, recv_sem, device):
    return pltpu.make_async_remote_copy(src_ref=src, dst_ref=dst, send_sem=send_sem, recv_sem=recv_sem,
                                        device_id=device, device_id_type=MESH)


HBM_SPEC = pl.BlockSpec(memory_space=pl.ANY)


def _gather_rider(part, forward_at, section, n_sections):
    rows = part.shape[0] // N_CHIPS
    half_rows = rows // 2
    sec_rows = half_rows // n_sections

    def stages(ins, outs, send_sems, recv_sems):
        del ins
        full = outs[0]

        def piece(chip, half):
            start = chip * rows + half * half_rows + section * sec_rows
            return full.at[pl.ds(pl.multiple_of(start, 8), sec_rows), :]

        def sends():
            x, y, c, chips = _mesh_pos()
            mine = piece(2 * x + y, c)
            return [_remote(mine, mine, send_sems.at[j], recv_sems.at[j], (px, py, c))
                    for j, (px, py) in enumerate(chips)]

        def forwards(half_of):
            x, y, c, chips = _mesh_pos()
            out = []
            for j, (px, py) in enumerate(chips):
                block = piece(2 * px + py, half_of(c))
                out.append(_remote(block, block, send_sems.at[3 + j], recv_sems.at[3 + j], (x, y, 1 - c)))
            return out

        def start():
            for cp in sends():
                cp.start()

        def forward():
            x, y, c, chips = _mesh_pos()
            for j, (px, py) in enumerate(chips):
                landed = piece(2 * px + py, c)
                _remote(landed, landed, send_sems.at[j], recv_sems.at[j], (x, y, 1 - c)).wait_recv()
            for cp in forwards(lambda c: c):
                cp.start()

        def finish():
            for cp in forwards(lambda c: 1 - c):
                cp.wait_recv()
            for cp in sends() + forwards(lambda c: c):
                cp.wait_send()

        return [(0.0, start), (forward_at, forward), (1.0, finish)]

    return _Rider((part,), (jax.ShapeDtypeStruct(part.shape, BF16),), {0: 0}, 6, stages)


def _chip_exchange_rider(partial, piece):
    if partial.ndim == 3:
        recv_shape = (N_CHIPS - 1,) + partial.shape[1:]
    else:
        recv_shape = (N_CHIPS - 1, partial.shape[0], partial.shape[1] // N_CHIPS)

    def stages(ins, outs, send_sems, recv_sems):
        def copies():
            x, y, c, chips = _mesh_pos()
            return [_remote(piece(ins[0], 2 * px + py), outs[0].at[j], send_sems.at[j], recv_sems.at[j], (px, py, c))
                    for j, (px, py) in enumerate(chips)]

        def start():
            for cp in copies():
                cp.start()

        def finish():
            for cp in copies():
                cp.wait()

        return [(0.0, start), (1.0, finish)]

    return _Rider((partial,), (jax.ShapeDtypeStruct(recv_shape, BF16),), {}, N_CHIPS - 1, stages)


def _pair_exchange_rider(g, stacked=False):
    shape = g.shape[1:] if stacked else g.shape

    def stages(ins, outs, send_sems, recv_sems):
        def copy():
            x, y, c, _ = _mesh_pos()
            src = ins[0].at[1 - c] if stacked else ins[0]
            return _remote(src, outs[0], send_sems.at[0], recv_sems.at[0], (x, y, 1 - c))

        return [(0.0, lambda: copy().start()), (1.0, lambda: copy().wait())]

    return _Rider((g,), (jax.ShapeDtypeStruct(shape, g.dtype),), {}, 1, stages)


def _pair_sum_in(mine, recv, sc):
    hd, d_in = mine.shape
    tr = min(hd, 256)

    def body(a_ref, b_ref, o_ref):
        o_ref[...] = (a_ref[...].astype(F32) + b_ref[...].astype(F32)).astype(BF16)

    blk = pl.BlockSpec((tr, sc), lambda i, j: (i, j))
    return pl.pallas_call(
        body, name="pair_sum_in", grid=(hd // tr, d_in // sc), in_specs=[blk, blk], out_specs=blk,
        out_shape=jax.ShapeDtypeStruct((hd, d_in), BF16),
        compiler_params=_params(dimension_semantics=("parallel", "parallel")),
    )(mine, recv)


def _pair_sum_out(gw_out, recv):
    _, n_chips, hr, d = gw_out.shape
    tr = min(hr, 256)
    c = lax.axis_index("c")

    def body(c_ref, a_ref, b_ref, o_ref):
        del c_ref
        o_ref[...] = (a_ref[...].astype(F32) + b_ref[...].astype(F32)).astype(BF16)

    blk = pl.BlockSpec((None, tr, d), lambda k, i, cc: (k, i, 0))
    gs = pltpu.PrefetchScalarGridSpec(
        num_scalar_prefetch=1, grid=(n_chips, hr // tr),
        in_specs=[pl.BlockSpec((None, None, tr, d), lambda k, i, cc: (cc[0], k, i, 0)), blk], out_specs=blk)
    return pl.pallas_call(
        body, name="pair_sum_out", grid_spec=gs, out_shape=jax.ShapeDtypeStruct((n_chips, hr, d), BF16),
        compiler_params=_params(dimension_semantics=("parallel", "parallel")),
    )(jnp.reshape(c, (1,)).astype(jnp.int32), gw_out, recv)


def _place():
    return jnp.stack([2 * lax.axis_index("x") + lax.axis_index("y"), lax.axis_index("c")]).astype(jnp.int32)


def _chip_sum_in(p_in, r_in, sc):
    hd = p_in.shape[0]
    tr = min(hd, 256)
    nblk = hd // tr

    def body(pos_ref, p_ref, r_ref, o_ref):
        del pos_ref
        acc = p_ref[...].astype(F32)
        for j in range(3):
            acc = acc + r_ref[j].astype(F32)
        o_ref[...] = acc

    gs = pltpu.PrefetchScalarGridSpec(
        num_scalar_prefetch=1, grid=(nblk,),
        in_specs=[pl.BlockSpec((tr, sc), lambda i, pos: (i, pos[0])), pl.BlockSpec((3, tr, sc), lambda i, pos: (0, i, 0))],
        out_specs=pl.BlockSpec((tr, sc), lambda i, pos: (pos[1] * nblk + i, 0)))
    return pl.pallas_call(
        body, name="chip_sum_in", grid_spec=gs, out_shape=jax.ShapeDtypeStruct((2 * hd, sc), F32),
        compiler_params=_params(dimension_semantics=("parallel",)),
    )(_place(), p_in, r_in)


def _chip_sum_out(p_out, r_out):
    _, hr, d = p_out.shape
    tr = min(hr, 256)
    nblk = hr // tr

    def body(pos_ref, p_ref, r_ref, o_ref):
        del pos_ref
        acc = p_ref[...].astype(F32)
        for j in range(3):
            acc = acc + r_ref[j].astype(F32)
        o_ref[...] = acc

    gs = pltpu.PrefetchScalarGridSpec(
        num_scalar_prefetch=1, grid=(nblk,),
        in_specs=[pl.BlockSpec((None, tr, d), lambda i, pos: (pos[0], i, 0)), pl.BlockSpec((3, tr, d), lambda i, pos: (0, i, 0))],
        out_specs=pl.BlockSpec((tr, d), lambda i, pos: (pos[1] * nblk + i, 0)))
    return pl.pallas_call(
        body, name="chip_sum_out", grid_spec=gs, out_shape=jax.ShapeDtypeStruct((2 * hr, d), F32),
        compiler_params=_params(dimension_semantics=("parallel",)),
    )(_place(), p_out, r_out)


def _adamw_math(w, g, m, v):
    m_new = ADAM_B1 * m + (1.0 - ADAM_B1) * g
    v_new = ADAM_B2 * v + (1.0 - ADAM_B2) * (g * g)
    m_hat = m_new / (1.0 - ADAM_B1 ** ADAM_STEP)
    v_hat = v_new / (1.0 - ADAM_B2 ** ADAM_STEP)
    delta = -ADAM_LR * (m_hat / (jnp.sqrt(v_hat) + ADAM_EPS) + ADAM_WD * w)
    return delta, m_new, v_new


def _share_halves(g_in, g_out):
    hd = g_in.shape[0] // 2
    hr = g_out.shape[0] // 2

    def body(gi_in, go_in, gi_ref, go_ref, send_sems, recv_sems):
        del gi_in, go_in
        x, y, c, _ = _mesh_pos()
        sibling = (x, y, 1 - c)
        mine_i = gi_ref.at[pl.ds(pl.multiple_of(c * hd, 8), hd), :]
        mine_o = go_ref.at[pl.ds(pl.multiple_of(c * hr, 8), hr), :]
        a = _remote(mine_i, mine_i, send_sems.at[0], recv_sems.at[0], sibling)
        b = _remote(mine_o, mine_o, send_sems.at[1], recv_sems.at[1], sibling)
        a.start()
        b.start()
        a.wait_send()
        b.wait_send()
        theirs_i = gi_ref.at[pl.ds(pl.multiple_of((1 - c) * hd, 8), hd), :]
        theirs_o = go_ref.at[pl.ds(pl.multiple_of((1 - c) * hr, 8), hr), :]
        _remote(theirs_i, theirs_i, send_sems.at[0], recv_sems.at[0], sibling).wait_recv()
        _remote(theirs_o, theirs_o, send_sems.at[1], recv_sems.at[1], sibling).wait_recv()

    return pl.pallas_call(
        body, name="share_halves",
        in_specs=[HBM_SPEC, HBM_SPEC], out_specs=[HBM_SPEC, HBM_SPEC],
        out_shape=[jax.ShapeDtypeStruct(g_in.shape, F32), jax.ShapeDtypeStruct(g_out.shape, F32)],
        input_output_aliases={0: 0, 1: 1},
        scratch_shapes=[pltpu.SemaphoreType.DMA((2,)), pltpu.SemaphoreType.DMA((2,))],
    )(g_in, g_out)


def _adamw(w, g, m, v, name):
    rows, cols = w.shape
    streams = 8
    fit = (VMEM_LIMIT_BYTES // 2) // (streams * 2 * cols * 4)
    tr = min(rows, 1 << (fit.bit_length() - 1))

    def body(w_ref, g_ref, m_ref, v_ref, d_ref, mo_ref, vo_ref, go_ref):
        gv = g_ref[...]
        delta, m_new, v_new = _adamw_math(w_ref[...], gv, m_ref[...], v_ref[...])
        d_ref[...] = delta
        mo_ref[...] = m_new
        vo_ref[...] = v_new
        go_ref[...] = gv

    spec = pl.BlockSpec((tr, cols), lambda i: (i, 0))
    shape = jax.ShapeDtypeStruct((rows, cols), F32)
    return pl.pallas_call(
        body, name=name, grid=(rows // tr,), in_specs=[spec] * 4, out_specs=[spec] * 4,
        out_shape=[shape] * 4, compiler_params=_params(dimension_semantics=("parallel",)),
    )(w, g, m, v)


SMALL_ROWS = 8


def _small_allreduce_adamw(part, w_pack, m_pack, v_pack):
    d = part.shape[1]

    def body(part_ref, w_ref, m_ref, v_ref, g_ref, d_ref, mo_ref, vo_ref, buf_ref, send_sems, recv_sems):
        x, y, c, _ = _mesh_pos()
        me = 4 * x + 2 * y + c
        buf_ref[0] = part_ref[...]
        copies = []
        for r in range(1, 8):
            rx, ry, rc = (r >> 2) & 1, (r >> 1) & 1, r & 1
            peer = (x ^ rx, y ^ ry, c ^ rc)
            copies.append(_remote(buf_ref.at[0], buf_ref.at[r], send_sems.at[r - 1], recv_sems.at[r - 1], peer))
        for cp in copies:
            cp.start()
        for cp in copies:
            cp.wait()
        total = buf_ref[me]
        for s in range(1, 8):
            total = total + buf_ref[s ^ me]
        w = w_ref[...]
        row = lax.broadcasted_iota(jnp.int32, (SMALL_ROWS, d), 0)
        l0, l1 = w[3:4], w[4:5]
        mx = jnp.maximum(l0, l1)
        e0, e1 = jnp.exp(l0 - mx), jnp.exp(l1 - mx)
        lb = e0 / (e0 + e1)
        g_l0 = total[3:4] * lb * (1.0 - lb)
        grads = jnp.where(row == 3, g_l0, jnp.where(row == 4, -g_l0, total))
        g_ref[...] = grads
        delta, m_new, v_new = _adamw_math(w, grads, m_ref[...], v_ref[...])
        d_ref[...] = delta
        mo_ref[...] = m_new
        vo_ref[...] = v_new

    vm = pl.BlockSpec(memory_space=pltpu.VMEM)
    shape = jax.ShapeDtypeStruct((SMALL_ROWS, d), F32)
    return pl.pallas_call(
        body, name="small_allreduce_adamw",
        in_specs=[vm] * 4, out_specs=[vm] * 4, out_shape=[shape] * 4,
        scratch_shapes=[pltpu.VMEM((8, SMALL_ROWS, d), F32), pltpu.SemaphoreType.DMA((7,)), pltpu.SemaphoreType.DMA((7,))],
    )(part, w_pack, m_pack, v_pack)


def _pack_small(d, pre, post, rnn, lb, sink, extra=None):
    rows = [pre, post, rnn, lb[0:1], lb[1:2],
            jnp.pad(sink, ((0, 0), (0, d - sink.shape[1]))),
            jnp.zeros((1, d), F32) if extra is None else extra,
            jnp.zeros((1, d), F32)]
    return jnp.concatenate(rows, axis=0)


def _unpack_small(p, n_sink):
    return dict(pre=p[0:1], post=p[1:2], rnn=p[2:3], lb=p[3:5], sink=p[5:6, :n_sink])


def kernel(x, w_in, attn_sinks, lb_logits, rnn_norm, w_out, pre_norm, post_norm, loss_target, m_w_in, m_attn_sinks, m_lb_logits, m_rnn_norm, m_w_out, m_pre_norm, m_post_norm, v_w_in, v_attn_sinks, v_lb_logits, v_rnn_norm, v_w_out, v_pre_norm, v_post_norm):
    t, d = x.shape[1], x.shape[2]
    sc = w_in.shape[2]
    n_sink = attn_sinks.shape[1]
    w_in2, w_out2 = w_in[0], w_out[0]

    w_in_part = _cast_into_gathered(w_in2, "cast_w_in", 1)
    w_out_part = _cast_into_gathered(w_out2, "cast_w_out", 0)
    loss_part, grad_x, (p_in, r_in), (p_out, r_out), small = _local_grads(
        x[0], loss_target[0], w_in_part, w_out_part, attn_sinks, lb_logits, rnn_norm, pre_norm, post_norm, sc)
    g_w_in, g_w_out = _share_halves(_chip_sum_in(p_in, r_in, sc), _chip_sum_out(p_out, r_out))
    d_w_in, nm_w_in, nv_w_in, g_w_in = _adamw(w_in2, g_w_in, m_w_in[0], v_w_in[0], "adamw_w_in")
    d_w_out, nm_w_out, nv_w_out, g_w_out = _adamw(w_out2, g_w_out, m_w_out[0], v_w_out[0], "adamw_w_out")

    lb_part = jnp.concatenate([small["lb"], jnp.zeros_like(small["lb"])], axis=0)
    loss_row = jnp.pad(loss_part[:, :1], ((0, 0), (0, d - 1)))
    part = _pack_small(d, small["pre"], small["post"], small["rnn"], lb_part, small["sink"], loss_row)
    w_pack = _pack_small(d, pre_norm, post_norm, rnn_norm, lb_logits, attn_sinks)
    m_pack = _pack_small(d, m_pre_norm, m_post_norm, m_rnn_norm, m_lb_logits, m_attn_sinks)
    v_pack = _pack_small(d, v_pre_norm, v_post_norm, v_rnn_norm, v_lb_logits, v_attn_sinks)
    g_pack, d_pack, nm_pack, nv_pack = _small_allreduce_adamw(part, w_pack, m_pack, v_pack)
    loss = g_pack[6, 0]
    g, dl, nm, nv = (_unpack_small(p, n_sink) for p in (g_pack, d_pack, nm_pack, nv_pack))

    def ordered(w_in_leaf, w_out_leaf, s):
        return (w_in_leaf[None], s["sink"], s["lb"], s["rnn"], w_out_leaf[None], s["pre"], s["post"])

    return (loss, grad_x[None],
            *ordered(g_w_in, g_w_out, g), *ordered(d_w_in, d_w_out, dl),
            *ordered(nm_w_in, nm_w_out, nm), *ordered(nv_w_in, nv_w_out, nv))
```

```python
import numpy as np
import jax
import jax.numpy as jnp
from jax import lax
from jax.experimental import pallas as pl
from jax.experimental.pallas import tpu as pltpu

F32 = jnp.float32
BF16 = jnp.bfloat16
MESH = pl.DeviceIdType.MESH

NORM_EPS = 1e-6
ATTN_HEAD_DIM = 64
GQA_GROUP = 8
WINDOW = 128
ATTN_STEP_BLOCKS = 4
ATTN_STEP_ROWS = ATTN_STEP_BLOCKS * WINDOW
RNN_HEAD_DIM = 128
CHUNK = 64
HALF_CHUNK = CHUNK // 2
ATTN_SCALE = ATTN_HEAD_DIM ** -0.5

ADAM_LR = 0.001
ADAM_B1 = 0.9
ADAM_B2 = 0.999
ADAM_EPS = 1e-08
ADAM_WD = 0.01
ADAM_STEP = 10

LANES = 128
COL_TILE = 512
RNN_GROUP_HEADS = 8
RNN_STEP_ROWS = 256
ATTN_GROUP_LANES = 1024
N_CHIPS = 4
VMEM_LIMIT_BYTES = 56 * 1024 * 1024
NEG_BIG = -1e30


def _params(**kw):
    return pltpu.CompilerParams(vmem_limit_bytes=VMEM_LIMIT_BYTES, **kw)


def _sigmoid(x):
    return 1.0 / (1.0 + jnp.exp(-x))


def _dot(a, b):
    return jnp.dot(a, b, preferred_element_type=F32)


def _dot_nt(a, b):
    return lax.dot_general(a, b, (((1,), (1,)), ((), ())), preferred_element_type=F32)


def _dot_tn(a, b):
    return lax.dot_general(a, b, (((0,), (0,)), ((), ())), preferred_element_type=F32)


def _tri_dot(tri_bf16, x):
    hi = x.astype(BF16)
    lo = (x - hi.astype(F32)).astype(BF16)
    return _dot(tri_bf16, hi) + _dot(tri_bf16, lo)


def _layout(d_model):
    d = d_model
    dkv = d // GQA_GROUP
    orig = dict(aq=0, ak=d, av=d + dkv, ag=d + 2 * dkv)
    base = d + 2 * dkv + d
    orig.update(rq=base, rf=base + d, ri=base + 2 * d, rg=base + 3 * d)
    group_w = RNN_GROUP_HEADS * RNN_HEAD_DIM
    cols = []
    for hg in range(d // group_w):
        for seg in ("rq", "rf", "ri", "rg"):
            cols.append((orig[seg] + hg * group_w, group_w))
    for m in range(d // ATTN_GROUP_LANES):
        for seg in ("aq", "ag"):
            cols.append((orig[seg] + m * ATTN_GROUP_LANES, ATTN_GROUP_LANES))
    cols.append((orig["ak"], dkv))
    cols.append((orig["av"], dkv))
    units = []
    for start, width in cols:
        assert start % LANES == 0 and width % LANES == 0
        units += [start + u for u in range(0, width, LANES)]
    per = COL_TILE // LANES
    assert len(units) % per == 0
    tiles = []
    for t in range(len(units) // per):
        run = units[t * per:(t + 1) * per]
        assert run[0] % COL_TILE == 0 and all(run[i] == run[0] + i * LANES for i in range(per))
        tiles.append(run[0] // COL_TILE)
    return dict(a_off=4 * d, k_off=6 * d, v_off=6 * d + dkv, total=6 * d + 2 * dkv,
                perm=np.asarray(tiles, np.int32))


def _chip_index():
    return jnp.reshape(2 * lax.axis_index("x") + lax.axis_index("y"), (1,)).astype(jnp.int32)


def _cast_into_gathered(a, name, axis):
    rows, cols = a.shape
    tr = min(rows, 512)
    nblk = rows // tr

    def body(me_ref, a_ref, o_ref):
        del me_ref
        o_ref[...] = a_ref[...].astype(BF16)

    if axis == 1:
        out_spec = pl.BlockSpec((tr, cols), lambda i, me: (i, me[0]))
        shape = (rows, N_CHIPS * cols)
    else:
        out_spec = pl.BlockSpec((tr, cols), lambda i, me: (me[0] * nblk + i, 0))
        shape = (N_CHIPS * rows, cols)
    gs = pltpu.PrefetchScalarGridSpec(num_scalar_prefetch=1, grid=(nblk,),
                                      in_specs=[pl.BlockSpec((tr, cols), lambda i, me: (i, 0))], out_specs=out_spec)
    return pl.pallas_call(
        body, name=name, grid_spec=gs, out_shape=jax.ShapeDtypeStruct(shape, BF16),
        compiler_params=_params(dimension_semantics=("parallel",)),
    )(_chip_index(), a)


def _prenorm_fwd(x, gain):
    t, d = x.shape
    tm = min(t, 256)

    def body(x_ref, g_ref, h_ref):
        xv = x_ref[...]
        r = lax.rsqrt(jnp.mean(xv * xv, axis=-1, keepdims=True) + NORM_EPS)
        h_ref[...] = ((xv * r) * g_ref[...]).astype(BF16)

    return pl.pallas_call(
        body, name="prenorm_fwd", grid=(t // tm,),
        in_specs=[pl.BlockSpec((tm, d), lambda i: (i, 0)), pl.BlockSpec((1, d), lambda i: (0, 0))],
        out_specs=pl.BlockSpec((tm, d), lambda i: (i, 0)),
        out_shape=jax.ShapeDtypeStruct((t, d), BF16),
        compiler_params=_params(dimension_semantics=("parallel",)),
    )(x, gain)


def _post_loss(x, y, target, gain):
    t, d = x.shape
    tm = min(t, 256)
    inv_d = 1.0 / d

    def body(x_ref, y_ref, t_ref, g_ref, dy_ref, dz_ref, gp_ref, loss_ref):
        i = pl.program_id(0)
        yv = y_ref[...]
        gain_v = g_ref[...]
        r = lax.rsqrt(jnp.mean(yv * yv, axis=-1, keepdims=True) + NORM_EPS)
        n = yv * r
        e = (x_ref[...] + n * gain_v) - t_ref[...]
        dz = e * inv_d
        dn = dz * gain_v
        dy = r * (dn - n * jnp.mean(dn * n, axis=-1, keepdims=True))
        dy_ref[...] = dy.astype(BF16)
        dz_ref[...] = dz

        @pl.when(i == 0)
        def _():
            gp_ref[...] = jnp.zeros_like(gp_ref)
            loss_ref[...] = jnp.zeros_like(loss_ref)

        gp_ref[...] += jnp.sum(dz * n, axis=0, keepdims=True)
        row = jnp.sum(e * e, axis=-1, keepdims=True)
        loss_ref[...] += jnp.full(loss_ref.shape, 0.5 * inv_d * jnp.sum(row), F32)

    row_spec = pl.BlockSpec((tm, d), lambda i: (i, 0))
    vec_spec = pl.BlockSpec((1, d), lambda i: (0, 0))
    return pl.pallas_call(
        body, name="post_loss", grid=(t // tm,),
        in_specs=[row_spec, row_spec, row_spec, vec_spec],
        out_specs=[row_spec, row_spec, vec_spec, pl.BlockSpec((1, LANES), lambda i: (0, 0))],
        out_shape=[jax.ShapeDtypeStruct((t, d), BF16), jax.ShapeDtypeStruct((t, d), F32),
                   jax.ShapeDtypeStruct((1, d), F32), jax.ShapeDtypeStruct((1, LANES), F32)],
        compiler_params=_params(dimension_semantics=("arbitrary",)),
    )(x, y, target, gain)


def _prenorm_bwd(x, dh, dz, gain):
    t, d = x.shape
    tm = min(t, 256)

    def body(x_ref, dh_ref, dz_ref, g_ref, gx_ref, gp_ref):
        i = pl.program_id(0)
        xv = x_ref[...]
        r = lax.rsqrt(jnp.mean(xv * xv, axis=-1, keepdims=True) + NORM_EPS)
        n = xv * r
        dhv = dh_ref[...]
        dn = dhv * g_ref[...]
        gx_ref[...] = dz_ref[...] + r * (dn - n * jnp.mean(dn * n, axis=-1, keepdims=True))

        @pl.when(i == 0)
        def _():
            gp_ref[...] = jnp.zeros_like(gp_ref)

        gp_ref[...] += jnp.sum(dhv * n, axis=0, keepdims=True)

    row_spec = pl.BlockSpec((tm, d), lambda i: (i, 0))
    vec_spec = pl.BlockSpec((1, d), lambda i: (0, 0))
    return pl.pallas_call(
        body, name="prenorm_bwd", grid=(t // tm,),
        in_specs=[row_spec, row_spec, row_spec, vec_spec],
        out_specs=[row_spec, vec_spec],
        out_shape=[jax.ShapeDtypeStruct((t, d), F32), jax.ShapeDtypeStruct((1, d), F32)],
        compiler_params=_params(dimension_semantics=("arbitrary",)),
    )(x, dh, dz, gain)


class _Rider:
    def __init__(self, operands, out_shapes, aliases, n_sems, stages):
        self.operands = tuple(operands)
        self.out_shapes = tuple(out_shapes)
        self.aliases = dict(aliases)
        self.n_sems = n_sems
        self.stages = stages

    def scratch(self):
        return [pltpu.SemaphoreType.DMA((self.n_sems,)), pltpu.SemaphoreType.DMA((self.n_sems,))]

    def emit(self, step, n_steps, in_refs, out_refs, send_sems, recv_sems):
        for frac, fn in self.stages(in_refs, out_refs, send_sems, recv_sems):
            at = min(n_steps - 1, int(frac * (n_steps - 1) + 0.5))
            pl.when(step == at)(fn)


def _matmul(name, a, b, *, out_shape, grid, a_spec, b_spec, o_spec, nt=False, ta=False, perm=None, rider=None):
    nk = grid[2]
    n_steps = grid[0] * grid[1] * grid[2]
    tm, tn = [s for s in o_spec.block_shape if s is not None][-2:]
    acc_in_out = out_shape.dtype == F32
    n_pre = 0 if perm is None else 1
    n_rin = 0 if rider is None else len(rider.operands)
    n_rout = 0 if rider is None else len(rider.out_shapes)
    use_acc = not (nk == 1 or acc_in_out)

    def body(*refs):
        refs = refs[n_pre:]
        a_ref, b_ref = refs[:2]
        rin = refs[2:2 + n_rin]
        o_ref = refs[2 + n_rin]
        rout = refs[3 + n_rin:3 + n_rin + n_rout]
        scratch_refs = refs[3 + n_rin + n_rout:]
        if rider is not None:
            step = (pl.program_id(0) * grid[1] + pl.program_id(1)) * grid[2] + pl.program_id(2)
            rider.emit(step, n_steps, rin, rout, scratch_refs[-2], scratch_refs[-1])
        def product():
            if ta:
                return _dot_tn(a_ref[...], b_ref[...])
            return _dot_nt(a_ref[...], b_ref[...]) if nt else _dot(a_ref[...], b_ref[...])

        if nk == 1:
            o_ref[...] = product().astype(o_ref.dtype)
            return
        acc_ref = o_ref if acc_in_out else scratch_refs[0]
        k = pl.program_id(2)

        @pl.when(k == 0)
        def _():
            acc_ref[...] = jnp.zeros_like(acc_ref)

        acc_ref[...] += product()

        if not acc_in_out:
            @pl.when(k == nk - 1)
            def _():
                o_ref[...] = acc_ref[...].astype(o_ref.dtype)

    scratch = [pltpu.VMEM((tm, tn), F32)] if use_acc else []
    in_specs = [a_spec, b_spec] + [HBM_SPEC] * n_rin
    out_specs = [o_spec] + [HBM_SPEC] * n_rout
    out_shapes = [out_shape]
    operands = [a, b]
    aliases = {}
    sem = ("parallel", "parallel", "arbitrary")
    if rider is not None:
        scratch += rider.scratch()
        out_shapes += list(rider.out_shapes)
        operands += list(rider.operands)
        aliases = {n_pre + 2 + i: 1 + o for i, o in rider.aliases.items()}
        sem = ("arbitrary", "arbitrary", "arbitrary")
    cp = _params(dimension_semantics=sem)
    if perm is None:
        return pl.pallas_call(body, name=name, grid=grid, in_specs=in_specs, out_specs=out_specs,
                              out_shape=out_shapes, scratch_shapes=scratch, input_output_aliases=aliases,
                              compiler_params=cp)(*operands)
    gs = pltpu.PrefetchScalarGridSpec(num_scalar_prefetch=1, grid=grid, in_specs=in_specs,
                                      out_specs=out_specs, scratch_shapes=scratch)
    return pl.pallas_call(body, name=name, grid_spec=gs, out_shape=out_shapes, input_output_aliases=aliases,
                          compiler_params=cp)(jnp.asarray(perm), *operands)


def _proj_mm(h, w_full, perm, rider=None):
    t, d = h.shape
    n_tiles = len(perm)
    tm = min(t, 1024)
    return _matmul(
        "proj_mm", h, w_full, perm=perm, rider=rider, grid=(t // tm, n_tiles, 1),
        out_shape=jax.ShapeDtypeStruct((t, n_tiles * COL_TILE), F32),
        a_spec=pl.BlockSpec((tm, d), lambda i, j, k, p: (i, 0)),
        b_spec=pl.BlockSpec((d, COL_TILE), lambda i, j, k, p: (0, p[j])),
        o_spec=pl.BlockSpec((tm, COL_TILE), lambda i, j, k, p: (i, j)))


def _proj_gather_mm(h, wi_part, perm, sc):
    t, d = h.shape
    n_tiles = len(perm)
    tm = min(t, 1024)
    n_i = t // tm
    hd = d // 2
    nf = sc // COL_TILE
    rem = sc - nf * COL_TILE
    assert 2 * rem == COL_TILE and n_tiles == N_CHIPS * nf + 2
    n_kinds = nf + 1
    last = nf
    ahead = min(6, nf)
    slots = []
    for q in range(nf):
        slots += [(0, q), (1, q)] + ([(2, q - 1)] if q else [])
    slots.append((2, nf - 1))
    rem_at = nf + len(slots)

    def first_full(chip):
        return (chip * sc + (rem if chip % 2 else 0)) // COL_TILE

    inverse = np.argsort(perm)
    table = np.zeros((N_CHIPS, 2, n_tiles), np.int32)
    for chip in range(N_CHIPS):
        seq = list(range(first_full(chip), first_full(chip) + nf))
        seq += [first_full((chip ^ 2, chip ^ 1, chip ^ 3)[j]) + q for j, q in slots]
        seq += [first_full(chip - chip % 2) + nf, first_full((chip ^ 2) - chip % 2) + nf]
        assert sorted(seq) == list(range(n_tiles)), seq
        table[chip, 0] = inverse[seq]
        table[chip, 1] = seq
    me_chip = 2 * lax.axis_index("x") + lax.axis_index("y")
    tab = lax.dynamic_index_in_dim(jnp.asarray(table), me_chip, 0, keepdims=False)

    def body(tab_ref, h_hbm, wi_in, proj_ref, full, hbuf, bbuf, local_sems, send_sems, recv_sems):
        del wi_in
        jj = pl.program_id(0)
        i = pl.program_id(1)
        x, y, c, chips = _mesh_pos()
        sibling = (x, y, 1 - c)

        def chip_of(j):
            return 2 * chips[j][0] + chips[j][1]

        def piece(chip, half, kind, part=None):
            odd = chip % 2
            if kind == last:
                start, width = chip * sc + (1 - odd) * (nf * COL_TILE), rem
            else:
                start, width = chip * sc + odd * rem + kind * COL_TILE, COL_TILE
            if part is not None:
                width //= 2
                start = start + part * width
            return full.at[pl.ds(half * hd, hd), pl.ds(pl.multiple_of(start, LANES), width)]

        def ici(j, kind):
            mine = piece(2 * x + y, c, kind)
            k = j * n_kinds + kind
            return _remote(mine, mine, send_sems.at[k], recv_sems.at[k], (chips[j][0], chips[j][1], c))

        def landed(j, kind):
            blk = piece(chip_of(j), c, kind)
            k = j * n_kinds + kind
            return _remote(blk, blk, send_sems.at[k], recv_sems.at[k], sibling)

        def hop(p, kind):
            blk = piece(chip_of(p), c, kind, part=p)
            k = (2 + p) * n_kinds + kind
            return _remote(blk, blk, send_sems.at[k], recv_sems.at[k], (chips[1 - p][0], chips[1 - p][1], c))

        def hopped(p, kind):
            blk = piece(chip_of(2), c, kind, part=p)
            k = (2 + p) * n_kinds + kind
            return _remote(blk, blk, send_sems.at[k], recv_sems.at[k], sibling)

        def passed(j, kind, half):
            blk = piece(chip_of(j), half, kind)
            k = (4 + j) * n_kinds + kind
            return _remote(blk, blk, send_sems.at[k], recv_sems.at[k], sibling)

        def fetch(pos, slot):
            col = pl.multiple_of(tab_ref[1, pos] * COL_TILE, LANES)
            return pltpu.make_async_copy(full.at[:, pl.ds(col, COL_TILE)], bbuf.at[slot], local_sems.at[slot])

        def load_h():
            return pltpu.make_async_copy(h_hbm, hbuf, local_sems.at[2])

        def send(kind):
            for j in range(2):
                ici(j, kind).start()

        def relay(j, kind):
            if j == 2:
                hopped(0, kind).wait_recv()
                hopped(1, kind).wait_recv()
            else:
                landed(j, kind).wait_recv()
                hop(j, kind).start()
            passed(j, kind, c).start()
            if j == 0 and kind + ahead < n_kinds:
                send(kind + ahead)

        events = {}
        for s, (j, kind) in enumerate(slots):
            events.setdefault(nf + s - 2, []).append(lambda j=j, kind=kind: relay(j, kind))
            events.setdefault(nf + s - 1, []).append(lambda j=j, kind=kind: passed(j, kind, 1 - c).wait_recv())
        for j in range(3):
            events.setdefault(rem_at - 4 + j, []).append(lambda j=j: relay(j, last))
            events.setdefault(rem_at - 1, []).append(lambda j=j: passed(j, last, 1 - c).wait_recv())

        @pl.when(i == 0)
        def _():
            @pl.when(jj == 0)
            def _():
                load_h().start()
                for kind in range(ahead):
                    send(kind)
                fetch(0, 0).start()
                load_h().wait()

            for pos in sorted(events):
                def run(pos=pos):
                    for fn in events[pos]:
                        fn()
                pl.when(jj == pos)(run)

            @pl.when(jj + 1 < n_tiles)
            def _():
                fetch(jj + 1, (jj + 1) % 2).start()

            fetch(jj, jj % 2).wait()

            @pl.when(jj == n_tiles - 1)
            def _():
                for kind in range(n_kinds):
                    for j in range(2):
                        ici(j, kind).wait_send()
                        hop(j, kind).wait_send()
                    for j in range(3):
                        passed(j, kind, c).wait_send()

        rows = pl.ds(pl.multiple_of(i * tm, tm), tm)
        proj_ref[...] = _dot(hbuf[rows, :], bbuf[jj % 2])

    gs = pltpu.PrefetchScalarGridSpec(
        num_scalar_prefetch=1, grid=(n_tiles, n_i),
        in_specs=[HBM_SPEC, HBM_SPEC],
        out_specs=[pl.BlockSpec((tm, COL_TILE), lambda jj, i, tb: (i, tb[0, jj])), HBM_SPEC],
        scratch_shapes=[pltpu.VMEM((t, d), BF16), pltpu.VMEM((2, d, COL_TILE), BF16), pltpu.SemaphoreType.DMA((3,)),
                        pltpu.SemaphoreType.DMA((7 * n_kinds,)), pltpu.SemaphoreType.DMA((7 * n_kinds,))])
    return pl.pallas_call(
        body, name="proj_gather_mm", grid_spec=gs,
        out_shape=[jax.ShapeDtypeStruct((t, n_tiles * COL_TILE), F32), jax.ShapeDtypeStruct(wi_part.shape, BF16)],
        input_output_aliases={2: 1},
        compiler_params=_params(dimension_semantics=("arbitrary", "arbitrary")),
    )(tab, h, wi_part)


def _gw_in_mm(name, h, dproj, perm, half, rider=None):
    t, d = h.shape
    n_tiles = len(perm)
    hd = d // 2
    tm = min(hd, 1024)
    per_half = hd // tm
    table = jnp.concatenate([jnp.asarray(perm), jnp.reshape(half, (1,)).astype(jnp.int32)])
    return _matmul(
        name, h, dproj, perm=table, rider=rider, ta=True, grid=(per_half, n_tiles, 1),
        out_shape=jax.ShapeDtypeStruct((hd, n_tiles * COL_TILE), BF16),
        a_spec=pl.BlockSpec((t, tm), lambda i, j, k, p: (0, p[n_tiles] * per_half + i)),
        b_spec=pl.BlockSpec((t, COL_TILE), lambda i, j, k, p: (0, j)),
        o_spec=pl.BlockSpec((tm, COL_TILE), lambda i, j, k, p: (i, p[j])))


def _dh_mm(dproj, w_full, perm, rider=None):
    t = dproj.shape[0]
    d = w_full.shape[0]
    n_tiles = len(perm)
    tm = min(t, 2048)
    tn = min(d, 2048)
    return _matmul(
        "dh_mm", dproj, w_full, perm=perm, rider=rider, nt=True, grid=(t // tm, d // tn, n_tiles),
        out_shape=jax.ShapeDtypeStruct((t, d), F32),
        a_spec=pl.BlockSpec((tm, COL_TILE), lambda i, j, k, p: (i, k)),
        b_spec=pl.BlockSpec((tn, COL_TILE), lambda i, j, k, p: (j, p[k])),
        o_spec=pl.BlockSpec((tm, tn), lambda i, j, k, p: (i, j)))


def _out_mm(mixed, w_out_full):
    t, dm = mixed.shape
    d = w_out_full.shape[1]
    tm = min(t, 1024)
    tn = min(d, 512)
    tk = min(dm, 4096)
    return _matmul(
        "out_mm", mixed, w_out_full, grid=(t // tm, d // tn, dm // tk),
        out_shape=jax.ShapeDtypeStruct((t, d), F32),
        a_spec=pl.BlockSpec((tm, tk), lambda i, j, k: (i, k)),
        b_spec=pl.BlockSpec((tk, tn), lambda i, j, k: (k, j)),
        o_spec=pl.BlockSpec((tm, tn), lambda i, j, k: (i, j)))[0]


def _dmixed_mm(dy, w_out_full, rider=None):
    t, d = dy.shape
    dm = w_out_full.shape[0]
    tm = min(t, 1024)
    tn = min(dm, 1024)
    return _matmul(
        "dmixed_mm", dy, w_out_full, nt=True, rider=rider, grid=(t // tm, dm // tn, 1),
        out_shape=jax.ShapeDtypeStruct((t, dm), F32),
        a_spec=pl.BlockSpec((tm, d), lambda i, j, k: (i, 0)),
        b_spec=pl.BlockSpec((tn, d), lambda i, j, k: (j, 0)),
        o_spec=pl.BlockSpec((tm, tn), lambda i, j, k: (i, j)))


def _gw_out_mm(mixed, dy):
    t, dm = mixed.shape
    d = dy.shape[1]
    hr = dm // (2 * N_CHIPS)
    tn = min(d, 1024)
    return _matmul(
        "gw_out_mm", mixed, dy, ta=True, grid=(dm // hr, d // tn, 1),
        out_shape=jax.ShapeDtypeStruct((2, N_CHIPS, hr, d), BF16),
        a_spec=pl.BlockSpec((t, hr), lambda i, j, k: (0, i)),
        b_spec=pl.BlockSpec((t, tn), lambda i, j, k: (0, j)),
        o_spec=pl.BlockSpec((None, None, hr, tn), lambda i, j, k: (i % 2, i // 2, 0, j)))[0]


def _lane_half():
    return lax.broadcasted_iota(jnp.int32, (WINDOW, LANES), 1) // ATTN_HEAD_DIM


def _dup_kv(tile, kh):
    return jnp.where(_lane_half() == kh, tile, pltpu.roll(tile, ATTN_HEAD_DIM, 1))


def _stack_heads(tiles, kh):
    half = _lane_half()
    pieces = []
    for g in range(GQA_GROUP):
        pieces.append(jnp.where(half == g % 2, tiles[4 * kh + g // 2], 0.0))
    return jnp.concatenate(pieces, axis=0)


def _unstack_heads(stacked):
    half = _lane_half()
    out = []
    for j in range(GQA_GROUP // 2):
        a = stacked[(2 * j) * WINDOW:(2 * j + 1) * WINDOW]
        b = stacked[(2 * j + 1) * WINDOW:(2 * j + 2) * WINDOW]
        out.append(jnp.where(half == 0, a, b))
    return out


def _attn_probs(qs, kcat, sink_col, n):
    rows = GQA_GROUP * WINDOW
    s = _dot_nt(qs, kcat)
    qi = lax.broadcasted_iota(jnp.int32, (rows, 2 * WINDOW), 0) % WINDOW
    kj = lax.broadcasted_iota(jnp.int32, (rows, 2 * WINDOW), 1)
    first_key = WINDOW * (1 - jnp.minimum(n, 1))
    valid = (kj > qi) & (kj <= qi + WINDOW) & (kj >= first_key)
    s = jnp.where(valid, s, NEG_BIG)
    mx = jnp.maximum(jnp.max(s, axis=-1, keepdims=True), sink_col)
    p = jnp.exp(s - mx)
    p_sink = jnp.exp(sink_col - mx)
    inv = 1.0 / (jnp.sum(p, axis=-1, keepdims=True) + p_sink)
    return p * inv, p_sink * inv


def _attn_operands(sink_ref, q_tiles, k_prev, k_cur, v_prev, v_cur, m, kh):
    qs = _stack_heads([qt * ATTN_SCALE for qt in q_tiles], kh).astype(BF16)
    kcat = jnp.concatenate([_dup_kv(k_prev, kh), _dup_kv(k_cur, kh)], axis=0).astype(BF16)
    vcat = jnp.concatenate([_dup_kv(v_prev, kh), _dup_kv(v_cur, kh)], axis=0).astype(BF16)
    heads_per_group = ATTN_GROUP_LANES // ATTN_HEAD_DIM
    sink_col = jnp.concatenate(
        [jnp.full((WINDOW, 1), sink_ref[0, m * heads_per_group + kh * GQA_GROUP + g], F32)
         for g in range(GQA_GROUP)], axis=0)
    return qs, kcat, vcat, sink_col


def _attn_specs(lay, d):
    a_blk = lay["a_off"] // (2 * ATTN_GROUP_LANES)
    k_blk = lay["k_off"] // LANES
    v_blk = lay["v_off"] // LANES
    before = lambda n: jnp.maximum(ATTN_STEP_BLOCKS * n - 1, 0)
    qg = pl.BlockSpec((ATTN_STEP_ROWS, 2 * ATTN_GROUP_LANES), lambda m, n: (n, a_blk + m))
    kp = pl.BlockSpec((WINDOW, LANES), lambda m, n: (before(n), k_blk + m))
    kc = pl.BlockSpec((ATTN_STEP_ROWS, LANES), lambda m, n: (n, k_blk + m))
    vp = pl.BlockSpec((WINDOW, LANES), lambda m, n: (before(n), v_blk + m))
    vc = pl.BlockSpec((ATTN_STEP_ROWS, LANES), lambda m, n: (n, v_blk + m))
    return qg, kp, kc, vp, vc


def _block_rows(b):
    return slice(b * WINDOW, (b + 1) * WINDOW)


def _kv_tiles(prev_ref, cur_ref, b):
    prev = prev_ref[...] if b == 0 else cur_ref[_block_rows(b - 1), :]
    return prev, cur_ref[_block_rows(b), :]


def _attn_fwd(proj, sinks, lay, d, rider=None):
    t = proj.shape[0]
    n_groups = d // ATTN_GROUP_LANES
    n_blocks = t // ATTN_STEP_ROWS
    pairs = ATTN_GROUP_LANES // LANES
    n_rin = 0 if rider is None else len(rider.operands)
    n_rout = 0 if rider is None else len(rider.out_shapes)

    def body(*refs):
        sink_ref, qg_ref, kp_ref, kc_ref, vp_ref, vc_ref = refs[:6]
        rin = refs[6:6 + n_rin]
        mix_ref, o_ref = refs[6 + n_rin:8 + n_rin]
        rout = refs[8 + n_rin:8 + n_rin + n_rout]
        m = pl.program_id(0)
        n = pl.program_id(1)
        if rider is not None:
            rider.emit(m * n_blocks + n, n_groups * n_blocks, rin, rout, refs[-2], refs[-1])
        for b in range(ATTN_STEP_BLOCKS):
            rows = _block_rows(b)
            k_prev, k_cur = _kv_tiles(kp_ref, kc_ref, b)
            v_prev, v_cur = _kv_tiles(vp_ref, vc_ref, b)
            q_tiles = [qg_ref[rows, p * LANES:(p + 1) * LANES] for p in range(pairs)]
            for kh in range(2):
                qs, kcat, vcat, sink_col = _attn_operands(sink_ref, q_tiles, k_prev, k_cur, v_prev, v_cur, m, kh)
                probs, _ = _attn_probs(qs, kcat, sink_col, ATTN_STEP_BLOCKS * n + b)
                out = _dot(probs.astype(BF16), vcat)
                for j, tile in enumerate(_unstack_heads(out)):
                    p = 4 * kh + j
                    lanes = slice(p * LANES, (p + 1) * LANES)
                    gate = qg_ref[rows, ATTN_GROUP_LANES + p * LANES:ATTN_GROUP_LANES + (p + 1) * LANES]
                    o_ref[rows, lanes] = tile
                    mix_ref[rows, lanes] = (tile * (gate * _sigmoid(gate))).astype(BF16)

    qg, kp, kc, vp, vc = _attn_specs(lay, d)
    out_blk = pl.BlockSpec((ATTN_STEP_ROWS, ATTN_GROUP_LANES), lambda m, n: (n, m))
    out_shapes = [jax.ShapeDtypeStruct((t, 2 * d), BF16), jax.ShapeDtypeStruct((t, d), F32)]
    operands = [sinks, proj, proj, proj, proj, proj]
    scratch, aliases, sem = [], {}, ("parallel", "parallel")
    if rider is not None:
        scratch = rider.scratch()
        out_shapes += list(rider.out_shapes)
        operands += list(rider.operands)
        aliases = {6 + i: 2 + o for i, o in rider.aliases.items()}
        sem = ("arbitrary", "arbitrary")
    return pl.pallas_call(
        body, name="attn_fwd", grid=(n_groups, n_blocks),
        in_specs=[pl.BlockSpec(memory_space=pltpu.SMEM), qg, kp, kc, vp, vc] + [HBM_SPEC] * n_rin,
        out_specs=[out_blk, out_blk] + [HBM_SPEC] * n_rout,
        out_shape=out_shapes, scratch_shapes=scratch, input_output_aliases=aliases,
        compiler_params=_params(dimension_semantics=sem),
    )(*operands)


def _attn_bwd(proj, sinks, attn_o, dmixed, dproj, lay, d):
    t = proj.shape[0]
    n_groups = d // ATTN_GROUP_LANES
    pairs = ATTN_GROUP_LANES // LANES
    kv_w = n_groups * LANES

    def body(sink_ref, qg_ref, kp_ref, kc_ref, vp_ref, vc_ref, o_ref, dm_ref, dproj_hbm,
             dqg_ref, dkc_ref, dkp_ref, dvc_ref, dvp_ref, dsink_ref):
        del dproj_hbm
        m = pl.program_id(0)
        n = pl.program_id(1)
        half = _lane_half()
        sub = lax.broadcasted_iota(jnp.int32, (8, LANES), 0)
        lane = lax.broadcasted_iota(jnp.int32, (8, LANES), 1)
        dsink = jnp.zeros((8, LANES), F32)
        for b in range(ATTN_STEP_BLOCKS):
            rows = _block_rows(b)
            k_prev, k_cur = _kv_tiles(kp_ref, kc_ref, b)
            v_prev, v_cur = _kv_tiles(vp_ref, vc_ref, b)
            q_tiles = [qg_ref[rows, p * LANES:(p + 1) * LANES] for p in range(pairs)]
            do_tiles, o_tiles = [], []
            for p in range(pairs):
                lanes = slice(p * LANES, (p + 1) * LANES)
                gate_lanes = slice(ATTN_GROUP_LANES + p * LANES, ATTN_GROUP_LANES + (p + 1) * LANES)
                gate = qg_ref[rows, gate_lanes]
                sg = _sigmoid(gate)
                dmix = dm_ref[rows, lanes]
                ov = o_ref[rows, lanes]
                dqg_ref[rows, gate_lanes] = (dmix * ov * (sg * (1.0 + gate * (1.0 - sg)))).astype(BF16)
                do_tiles.append(dmix * (gate * sg))
                o_tiles.append(ov)

            dk_cur = dk_prev = dv_cur = dv_prev = jnp.zeros((WINDOW, LANES), F32)
            for kh in range(2):
                qs, kcat, vcat, sink_col = _attn_operands(sink_ref, q_tiles, k_prev, k_cur, v_prev, v_cur, m, kh)
                probs, p_sink = _attn_probs(qs, kcat, sink_col, ATTN_STEP_BLOCKS * n + b)
                dos = _stack_heads(do_tiles, kh)
                delta = jnp.sum(dos * _stack_heads(o_tiles, kh), axis=-1, keepdims=True)
                dos = dos.astype(BF16)
                dp = _dot_nt(dos, vcat)
                ds = (probs * (dp - delta)).astype(BF16)
                dv = _dot_tn(probs.astype(BF16), dos)
                dv = dv + pltpu.roll(dv, ATTN_HEAD_DIM, 1)
                dk = _dot_tn(ds, qs)
                dk = dk + pltpu.roll(dk, ATTN_HEAD_DIM, 1)
                dq = _dot(ds, kcat)
                for j, tile in enumerate(_unstack_heads(dq)):
                    p = 4 * kh + j
                    dqg_ref[rows, p * LANES:(p + 1) * LANES] = (tile * ATTN_SCALE).astype(BF16)
                dk_prev = jnp.where(half == kh, dk[:WINDOW], dk_prev)
                dk_cur = jnp.where(half == kh, dk[WINDOW:], dk_cur)
                dv_prev = jnp.where(half == kh, dv[:WINDOW], dv_prev)
                dv_cur = jnp.where(half == kh, dv[WINDOW:], dv_cur)
                sink_terms = p_sink * delta
                for g in range(GQA_GROUP):
                    val = -jnp.sum(sink_terms[g * WINDOW:(g + 1) * WINDOW])
                    dsink = dsink + jnp.where((sub == 0) & (lane == kh * GQA_GROUP + g), val, 0.0)
            dkc_ref[rows, :] = dk_cur
            dkp_ref[rows, :] = dk_prev
            dvc_ref[rows, :] = dv_cur
            dvp_ref[rows, :] = dv_prev

        @pl.when(n == 0)
        def _():
            dsink_ref[...] = jnp.zeros_like(dsink_ref)

        dsink_ref[...] += dsink

    qg, kp, kc, vp, vc = _attn_specs(lay, d)
    a_blk = lay["a_off"] // (2 * ATTN_GROUP_LANES)
    grp = pl.BlockSpec((ATTN_STEP_ROWS, ATTN_GROUP_LANES), lambda m, n: (n, m))
    kv_blk = pl.BlockSpec((ATTN_STEP_ROWS, LANES), lambda m, n: (n, m))
    kv_shape = jax.ShapeDtypeStruct((t, kv_w), F32)
    outs = pl.pallas_call(
        body, name="attn_bwd", grid=(n_groups, t // ATTN_STEP_ROWS),
        in_specs=[pl.BlockSpec(memory_space=pltpu.SMEM), qg, kp, kc, vp, vc, grp, grp,
                  pl.BlockSpec(memory_space=pl.ANY)],
        out_specs=[pl.BlockSpec((ATTN_STEP_ROWS, 2 * ATTN_GROUP_LANES), lambda m, n: (n, a_blk + m)),
                   kv_blk, kv_blk, kv_blk, kv_blk, pl.BlockSpec((8, LANES), lambda m, n: (m, 0))],
        out_shape=[jax.ShapeDtypeStruct(dproj.shape, BF16), kv_shape, kv_shape, kv_shape, kv_shape,
                   jax.ShapeDtypeStruct((n_groups * 8, LANES), F32)],
        input_output_aliases={8: 0},
        compiler_params=_params(dimension_semantics=("parallel", "arbitrary")),
    )(sinks, proj, proj, proj, proj, proj, attn_o, dmixed, dproj)
    return outs


def _kv_combine(dkc, dkp, dvc, dvp, dproj, lay):
    t, kv_w = dkc.shape
    nb = t // WINDOW
    kv_blk_idx = lay["k_off"] // (2 * kv_w)

    def body(dkc_ref, dkp_ref, dvc_ref, dvp_ref, dproj_hbm, o_ref):
        del dproj_hbm
        keep = (pl.program_id(0) < nb - 1).astype(F32)
        o_ref[:, :kv_w] = (dkc_ref[...] + keep * dkp_ref[...]).astype(BF16)
        o_ref[:, kv_w:] = (dvc_ref[...] + keep * dvp_ref[...]).astype(BF16)

    cur = pl.BlockSpec((WINDOW, kv_w), lambda n: (n, 0))
    nxt = pl.BlockSpec((WINDOW, kv_w), lambda n: (jnp.minimum(n + 1, nb - 1), 0))
    return pl.pallas_call(
        body, name="kv_combine", grid=(nb,),
        in_specs=[cur, nxt, cur, nxt, pl.BlockSpec(memory_space=pl.ANY)],
        out_specs=pl.BlockSpec((WINDOW, 2 * kv_w), lambda n: (n, kv_blk_idx)),
        out_shape=jax.ShapeDtypeStruct(dproj.shape, BF16),
        input_output_aliases={4: 0},
        compiler_params=_params(dimension_semantics=("parallel",)),
    )(dkc, dkp, dvc, dvp, dproj)


def _lower_bound(lbl_ref):
    l0 = lbl_ref[0:1, :]
    l1 = lbl_ref[1:2, :]
    mx = jnp.maximum(l0, l1)
    e0 = jnp.exp(l0 - mx)
    e1 = jnp.exp(l1 - mx)
    return e0 / (e0 + e1)


def _chunk_masks():
    ti = lax.broadcasted_iota(jnp.int32, (CHUNK, CHUNK), 0)
    si = lax.broadcasted_iota(jnp.int32, (CHUNK, CHUNK), 1)
    diag = ((ti // HALF_CHUNK) == (si // HALF_CHUNK)) & (si <= ti)
    off = (ti >= HALF_CHUNK) & (si < HALF_CHUNK)
    lower = (si <= ti).astype(BF16)
    upper = (si >= ti).astype(BF16)
    return diag, off, lower, upper


def _rnn_gates(rq, rf, lb):
    sf = _sigmoid(rf)
    f = lb + (1.0 - lb) * sf
    sq = _sigmoid(rq)
    return sf, f, jnp.log(f), 1.0 - f, sq, rq * sq


def _rnn_decays(g_cum):
    row = lax.broadcasted_iota(jnp.int32, g_cum.shape, 0)
    ref_d = jnp.where(row < HALF_CHUNK, g_cum[HALF_CHUNK // 2 - 1:HALF_CHUNK // 2],
                      g_cum[HALF_CHUNK + HALF_CHUNK // 2 - 1:HALF_CHUNK + HALF_CHUNK // 2])
    ref_o = g_cum[HALF_CHUNK - 1:HALF_CHUNK]
    last = g_cum[CHUNK - 1:CHUNK]
    return dict(eq_d=jnp.exp(g_cum - ref_d), ek_d=jnp.exp(ref_d - g_cum),
                eq_o=jnp.exp(jnp.minimum(g_cum - ref_o, 0.0)), ek_o=jnp.exp(jnp.minimum(ref_o - g_cum, 0.0)),
                eg=jnp.exp(g_cum), ekl=jnp.exp(last - g_cum), e_last=jnp.exp(last))


def _head(a, j):
    return a[:, j * RNN_HEAD_DIM:(j + 1) * RNN_HEAD_DIM]


def _rnn_specs(t, tb, d):
    gw = RNN_GROUP_HEADS * RNN_HEAD_DIM
    return gw, t // tb, tb // CHUNK


def _rnn_fwd(proj, lb_logits, rnn_gain, mixed, d, rider=None):
    t = proj.shape[0]
    tb = min(t, RNN_STEP_ROWS)
    gw, ntb, nch = _rnn_specs(t, tb, d)
    n_groups = d // gw
    n_heads = d // RNN_HEAD_DIM
    n_rin = 0 if rider is None else len(rider.operands)
    n_rout = 0 if rider is None else len(rider.out_shapes)

    def body(*refs):
        blk_ref, lbl_ref, gain_ref = refs[:3]
        rin = refs[4:4 + n_rin]
        mix_ref, o_ref, st_out_ref = refs[4 + n_rin:7 + n_rin]
        rout = refs[7 + n_rin:7 + n_rin + n_rout]
        st_ref = refs[7 + n_rin + n_rout]
        if rider is not None:
            rider.emit(pl.program_id(0) * ntb + pl.program_id(1), n_groups * ntb, rin, rout, refs[-2], refs[-1])

        @pl.when(pl.program_id(1) == 0)
        def _():
            st_ref[...] = jnp.zeros_like(st_ref)

        lb = _lower_bound(lbl_ref)
        gain = gain_ref[...]
        diag, off, lower, _ = _chunk_masks()

        def chunk(c, carry):
            rows = pl.ds(pl.multiple_of(c * CHUNK, CHUNK), CHUNK)
            rq = blk_ref[rows, 0:gw]
            rf = blk_ref[rows, gw:2 * gw]
            v = blk_ref[rows, 2 * gw:3 * gw]
            rg = blk_ref[rows, 3 * gw:4 * gw]
            _, _, g, k, _, q = _rnn_gates(rq, rf, lb)
            dec = _rnn_decays(_tri_dot(lower, g))
            qd = (q * dec["eq_d"]).astype(BF16)
            kd = (k * dec["ek_d"]).astype(BF16)
            qo = (q * dec["eq_o"]).astype(BF16)
            ko = (k * dec["ek_o"]).astype(BF16)
            qe = (q * dec["eg"]).astype(BF16)
            kl = (k * dec["ekl"]).astype(BF16)
            vb = v.astype(BF16)
            outs = []
            for j in range(RNN_GROUP_HEADS):
                st = st_ref[j]
                st_out_ref[j, c] = st
                attn = jnp.where(diag, _dot_nt(_head(qd, j), _head(kd, j)),
                                 jnp.where(off, _dot_nt(_head(qo, j), _head(ko, j)), 0.0))
                o = _dot(attn.astype(BF16), _head(vb, j)) + _dot_nt(_head(qe, j), st.astype(BF16))
                st_ref[j] = st * _head(dec["e_last"], j) + _dot_tn(_head(vb, j), _head(kl, j))
                rr = lax.rsqrt(jnp.mean(o * o, axis=-1, keepdims=True) + NORM_EPS)
                o_ref[rows, j * RNN_HEAD_DIM:(j + 1) * RNN_HEAD_DIM] = o
                outs.append(o * rr)
            on = jnp.concatenate(outs, axis=1) * gain
            mix_ref[rows, :] = (on * (rg * _sigmoid(rg))).astype(BF16)
            return carry

        lax.fori_loop(0, nch, chunk, 0, unroll=True)

    out_shapes = [jax.ShapeDtypeStruct(mixed.shape, BF16), jax.ShapeDtypeStruct((t, d), F32),
                  jax.ShapeDtypeStruct((n_heads, t // CHUNK, RNN_HEAD_DIM, RNN_HEAD_DIM), F32)]
    operands = [proj, lb_logits, rnn_gain, mixed]
    scratch = [pltpu.VMEM((RNN_GROUP_HEADS, RNN_HEAD_DIM, RNN_HEAD_DIM), F32)]
    aliases, sem = {3: 0}, ("parallel", "arbitrary")
    if rider is not None:
        scratch += rider.scratch()
        out_shapes += list(rider.out_shapes)
        operands += list(rider.operands)
        aliases.update({4 + i: 3 + o for i, o in rider.aliases.items()})
        sem = ("arbitrary", "arbitrary")
    return pl.pallas_call(
        body, name="rnn_fwd", grid=(n_groups, ntb),
        in_specs=[pl.BlockSpec((tb, 4 * gw), lambda h, i: (i, h)),
                  pl.BlockSpec((2, gw), lambda h, i: (0, h)),
                  pl.BlockSpec((1, gw), lambda h, i: (0, h)),
                  pl.BlockSpec(memory_space=pl.ANY)] + [HBM_SPEC] * n_rin,
        out_specs=[pl.BlockSpec((tb, gw), lambda h, i: (i, d // gw + h)),
                   pl.BlockSpec((tb, gw), lambda h, i: (i, h)),
                   pl.BlockSpec((RNN_GROUP_HEADS, nch, RNN_HEAD_DIM, RNN_HEAD_DIM), lambda h, i: (h, i, 0, 0))]
        + [HBM_SPEC] * n_rout,
        out_shape=out_shapes, scratch_shapes=scratch, input_output_aliases=aliases,
        compiler_params=_params(dimension_semantics=sem),
    )(*operands)


def _rnn_bwd(proj, lb_logits, rnn_gain, o_pre, states, dmixed, d_total, d, rider=None):
    t = proj.shape[0]
    tb = min(t, RNN_STEP_ROWS)
    gw, ntb, nch = _rnn_specs(t, tb, d)
    n_groups = d // gw
    n_rin = 0 if rider is None else len(rider.operands)
    n_rout = 0 if rider is None else len(rider.out_shapes)

    def body(*refs):
        blk_ref, lbl_ref, gain_ref, o_ref, st_in_ref, dm_ref = refs[:6]
        rin = refs[6:6 + n_rin]
        dproj_ref, dgain_ref, dlb_ref = refs[6 + n_rin:9 + n_rin]
        rout = refs[9 + n_rin:9 + n_rin + n_rout]
        dst_ref = refs[9 + n_rin + n_rout]
        if rider is not None:
            rider.emit(pl.program_id(0) * ntb + pl.program_id(1), n_groups * ntb, rin, rout, refs[-2], refs[-1])

        @pl.when(pl.program_id(1) == 0)
        def _():
            dst_ref[...] = jnp.zeros_like(dst_ref)
            dgain_ref[...] = jnp.zeros_like(dgain_ref)
            dlb_ref[...] = jnp.zeros_like(dlb_ref)

        lb = _lower_bound(lbl_ref)
        gain = gain_ref[...]
        diag, off, lower, upper = _chunk_masks()
        last_row = lax.broadcasted_iota(jnp.int32, (CHUNK, RNN_HEAD_DIM), 0) == CHUNK - 1

        def chunk(step, carry):
            c = nch - 1 - step
            rows = pl.ds(pl.multiple_of(c * CHUNK, CHUNK), CHUNK)
            rq = blk_ref[rows, 0:gw]
            rf = blk_ref[rows, gw:2 * gw]
            v = blk_ref[rows, 2 * gw:3 * gw]
            rg = blk_ref[rows, 3 * gw:4 * gw]
            sf, f, g, k, sq, q = _rnn_gates(rq, rf, lb)
            dec = _rnn_decays(_tri_dot(lower, g))
            qd = (q * dec["eq_d"]).astype(BF16)
            kd = (k * dec["ek_d"]).astype(BF16)
            qo = (q * dec["eq_o"]).astype(BF16)
            ko = (k * dec["ek_o"]).astype(BF16)
            qe = (q * dec["eg"]).astype(BF16)
            kl = (k * dec["ekl"]).astype(BF16)
            vb = v.astype(BF16)

            o = o_ref[rows, :]
            dmix = dm_ref[rows, :]
            sg = _sigmoid(rg)
            n_parts = []
            for j in range(RNN_GROUP_HEADS):
                oj = _head(o, j)
                n_parts.append(oj * lax.rsqrt(jnp.mean(oj * oj, axis=-1, keepdims=True) + NORM_EPS))
            nrm = jnp.concatenate(n_parts, axis=1)
            d_on = dmix * (rg * sg)
            d_rg = dmix * (nrm * gain) * (sg * (1.0 + rg * (1.0 - sg)))
            dgain_ref[...] += jnp.sum(d_on * nrm, axis=0, keepdims=True)
            dn = d_on * gain

            dq_parts, dk_parts, dv_parts, dg_parts = [], [], [], []
            for j in range(RNN_GROUP_HEADS):
                oj, nj, dnj = _head(o, j), _head(nrm, j), _head(dn, j)
                rr = lax.rsqrt(jnp.mean(oj * oj, axis=-1, keepdims=True) + NORM_EPS)
                do = (rr * (dnj - nj * jnp.mean(dnj * nj, axis=-1, keepdims=True))).astype(BF16)
                st = st_in_ref[j, c]
                dst = dst_ref[j]
                stb, dstb = st.astype(BF16), dst.astype(BF16)
                qdj, kdj, qoj, koj = _head(qd, j), _head(kd, j), _head(qo, j), _head(ko, j)
                attn = jnp.where(diag, _dot_nt(qdj, kdj), jnp.where(off, _dot_nt(qoj, koj), 0.0))
                dattn = _dot_nt(do, _head(vb, j))
                da_d = jnp.where(diag, dattn, 0.0).astype(BF16)
                da_o = jnp.where(off, dattn, 0.0).astype(BF16)
                dv = _dot_tn(attn.astype(BF16), do) + _dot_nt(_head(kl, j), dstb)
                dq_inter = _dot(do, stb) * _head(dec["eg"], j)
                dq_d, dq_o = _dot(da_d, kdj), _dot(da_o, koj)
                dq = dq_inter + dq_d * _head(dec["eq_d"], j) + dq_o * _head(dec["eq_o"], j)
                dk_inter = _dot(_head(vb, j), dstb) * _head(dec["ekl"], j)
                dk_d, dk_o = _dot_tn(da_d, qdj), _dot_tn(da_o, qoj)
                dk = dk_inter + dk_d * _head(dec["ek_d"], j) + dk_o * _head(dec["ek_o"], j)
                kj, qj = _head(k, j), _head(q, j)
                e_last = _head(dec["e_last"], j)
                extra = (jnp.sum(kj * dk_inter, axis=0, keepdims=True)
                         + e_last * jnp.sum(st * dst, axis=0, keepdims=True))
                dg_cum = (qj * dq_inter - kj * dk_inter
                          + (qdj.astype(F32) * dq_d + qoj.astype(F32) * dq_o)
                          - (kdj.astype(F32) * dk_d + koj.astype(F32) * dk_o))
                dg_parts.append(jnp.where(last_row, dg_cum + extra, dg_cum))
                dst_ref[j] = dst * e_last + _dot_tn(do, _head(qe, j))
                dq_parts.append(dq)
                dk_parts.append(dk)
                dv_parts.append(dv)

            dq = jnp.concatenate(dq_parts, axis=1)
            dk = jnp.concatenate(dk_parts, axis=1)
            dg = _tri_dot(upper, jnp.concatenate(dg_parts, axis=1))
            df = dg / f - dk
            dlb_ref[...] += jnp.sum(df * (1.0 - sf), axis=0, keepdims=True)
            d_rf = df * (1.0 - lb) * (sf * (1.0 - sf))
            d_rq = dq * (sq * (1.0 + rq * (1.0 - sq)))
            dproj_ref[rows, 0:gw] = d_rq.astype(BF16)
            dproj_ref[rows, gw:2 * gw] = d_rf.astype(BF16)
            dproj_ref[rows, 2 * gw:3 * gw] = jnp.concatenate(dv_parts, axis=1).astype(BF16)
            dproj_ref[rows, 3 * gw:4 * gw] = d_rg.astype(BF16)
            return carry

        lax.fori_loop(0, nch, chunk, 0, unroll=True)

    rev = lambda i: ntb - 1 - i
    vec = pl.BlockSpec((1, gw), lambda h, i: (0, h))
    scratch = [pltpu.VMEM((RNN_GROUP_HEADS, RNN_HEAD_DIM, RNN_HEAD_DIM), F32)]
    out_shapes = [jax.ShapeDtypeStruct((t, d_total), BF16), jax.ShapeDtypeStruct((1, d), F32),
                  jax.ShapeDtypeStruct((1, d), F32)]
    operands = [proj, lb_logits, rnn_gain, o_pre, states, dmixed]
    sem = ("parallel", "arbitrary")
    if rider is not None:
        scratch += rider.scratch()
        out_shapes += list(rider.out_shapes)
        operands += list(rider.operands)
        sem = ("arbitrary", "arbitrary")
    return pl.pallas_call(
        body, name="rnn_bwd", grid=(n_groups, ntb),
        in_specs=[pl.BlockSpec((tb, 4 * gw), lambda h, i: (rev(i), h)),
                  pl.BlockSpec((2, gw), lambda h, i: (0, h)), vec,
                  pl.BlockSpec((tb, gw), lambda h, i: (rev(i), h)),
                  pl.BlockSpec((RNN_GROUP_HEADS, nch, RNN_HEAD_DIM, RNN_HEAD_DIM), lambda h, i: (h, rev(i), 0, 0)),
                  pl.BlockSpec((tb, gw), lambda h, i: (rev(i), d // gw + h))] + [HBM_SPEC] * n_rin,
        out_specs=[pl.BlockSpec((tb, 4 * gw), lambda h, i: (rev(i), h)), vec, vec] + [HBM_SPEC] * n_rout,
        out_shape=out_shapes, scratch_shapes=scratch,
        compiler_params=_params(dimension_semantics=sem),
    )(*operands)


def _local_grads(x, target, w_in_full, w_out, sinks, lb_logits, rnn_gain, pre_gain, post_gain, sc=None):
    t, d = x.shape
    comm = sc is not None
    lay = _layout(d)
    perm = lay["perm"]
    h = _prenorm_fwd(x, pre_gain)
    if comm:
        proj, w_in_full = _proj_gather_mm(h, w_in_full, perm, sc)
        mixed, attn_o, w_out = _attn_fwd(proj, sinks, lay, d, _gather_rider(w_out, 0.7, 0, 2))
        mixed, o_pre, states, w_out_full = _rnn_fwd(proj, lb_logits, rnn_gain, mixed, d,
                                                    _gather_rider(w_out, 0.7, 1, 2))
    else:
        (proj,) = _proj_mm(h, w_in_full, perm)
        w_out_full = w_out
        mixed, attn_o = _attn_fwd(proj, sinks, lay, d)
        mixed, o_pre, states = _rnn_fwd(proj, lb_logits, rnn_gain, mixed, d)
    y = _out_mm(mixed, w_out_full)
    dy, dz, g_post, loss = _post_loss(x, y, target, post_gain)
    gw_out = _gw_out_mm(mixed, dy)
    rider = None
    if comm:
        dmixed, recv_out = _dmixed_mm(dy, w_out_full, _pair_exchange_rider(gw_out, stacked=True))
        p_out = _pair_sum_out(gw_out, recv_out)
        rider = _chip_exchange_rider(p_out, lambda ref, chip: ref.at[chip])
    else:
        (dmixed,) = _dmixed_mm(dy, w_out_full)
    dproj, g_rnn, g_lb, *r_out = _rnn_bwd(proj, lb_logits, rnn_gain, o_pre, states, dmixed, lay["total"], d, rider)
    dproj, dkc, dkp, dvc, dvp, dsink = _attn_bwd(proj, sinks, attn_o, dmixed, dproj, lay, d)
    dproj = _kv_combine(dkc, dkp, dvc, dvp, dproj, lay)
    if comm:
        c = lax.axis_index("c")
        (g_other,) = _gw_in_mm("gw_in_mm_other", h, dproj, perm, 1 - c)
        g_mine, recv_in = _gw_in_mm("gw_in_mm_mine", h, dproj, perm, c, _pair_exchange_rider(g_other))
        p_in = _pair_sum_in(g_mine, recv_in, sc)
        rider = _chip_exchange_rider(
            p_in, lambda ref, chip: ref.at[:, pl.ds(pl.multiple_of(chip * sc, LANES), sc)])
    dh, *r_in = _dh_mm(dproj, w_in_full, perm, rider)
    grad_x, g_pre = _prenorm_bwd(x, dh, dz, pre_gain)
    heads_per_group = ATTN_GROUP_LANES // ATTN_HEAD_DIM
    g_sink = dsink.reshape(d // ATTN_GROUP_LANES, 8, LANES)[:, 0, :heads_per_group].reshape(1, -1)
    small = dict(sink=g_sink, lb=g_lb, rnn=g_rnn, pre=g_pre, post=g_post)
    if comm:
        return loss, grad_x, (p_in, r_in[0]), (p_out, r_out[0]), small
    gw_in = jnp.stack([_gw_in_mm("gw_in_mm_%d" % half, h, dproj, perm, half)[0] for half in range(2)])
    return loss, grad_x, gw_in, gw_out, small


def _mesh_pos():
    x, y, c = lax.axis_index("x"), lax.axis_index("y"), lax.axis_index("c")
    chips = [(1 - x, y), (x, 1 - y), (1 - x, 1 - y)]
    return x, y, c, chips


def _remote(src, dst, send_sem, recv_sem, device):
    return pltpu.make_async_remote_copy(src_ref=src, dst_ref=dst, send_sem=send_sem, recv_sem=recv_sem,
                                        device_id=device, device_id_type=MESH)


HBM_SPEC = pl.BlockSpec(memory_space=pl.ANY)


def _gather_rider(part, forward_at, section, n_sections):
    rows = part.shape[0] // N_CHIPS
    half_rows = rows // 2
    sec_rows = half_rows // n_sections

    def stages(ins, outs, send_sems, recv_sems):
        del ins
        full = outs[0]

        def piece(chip, half):
            start = chip * rows + half * half_rows + section * sec_rows
            return full.at[pl.ds(pl.multiple_of(start, 8), sec_rows), :]

        def sends():
            x, y, c, chips = _mesh_pos()
            mine = piece(2 * x + y, c)
            return [_remote(mine, mine, send_sems.at[j], recv_sems.at[j], (px, py, c))
                    for j, (px, py) in enumerate(chips)]

        def forwards(half_of):
            x, y, c, chips = _mesh_pos()
            out = []
            for j, (px, py) in enumerate(chips):
                block = piece(2 * px + py, half_of(c))
                out.append(_remote(block, block, send_sems.at[3 + j], recv_sems.at[3 + j], (x, y, 1 - c)))
            return out

        def start():
            for cp in sends():
                cp.start()

        def forward():
            x, y, c, chips = _mesh_pos()
            for j, (px, py) in enumerate(chips):
                landed = piece(2 * px + py, c)
                _remote(landed, landed, send_sems.at[j], recv_sems.at[j], (x, y, 1 - c)).wait_recv()
            for cp in forwards(lambda c: c):
                cp.start()

        def finish():
            for cp in forwards(lambda c: 1 - c):
                cp.wait_recv()
            for cp in sends() + forwards(lambda c: c):
                cp.wait_send()

        return [(0.0, start), (forward_at, forward), (1.0, finish)]

    return _Rider((part,), (jax.ShapeDtypeStruct(part.shape, BF16),), {0: 0}, 6, stages)


def _chip_exchange_rider(partial, piece):
    if partial.ndim == 3:
        recv_shape = (N_CHIPS - 1,) + partial.shape[1:]
    else:
        recv_shape = (N_CHIPS - 1, partial.shape[0], partial.shape[1] // N_CHIPS)

    def stages(ins, outs, send_sems, recv_sems):
        def copies():
            x, y, c, chips = _mesh_pos()
            return [_remote(piece(ins[0], 2 * px + py), outs[0].at[j], send_sems.at[j], recv_sems.at[j], (px, py, c))
                    for j, (px, py) in enumerate(chips)]

        def start():
            for cp in copies():
                cp.start()

        def finish():
            for cp in copies():
                cp.wait()

        return [(0.0, start), (1.0, finish)]

    return _Rider((partial,), (jax.ShapeDtypeStruct(recv_shape, BF16),), {}, N_CHIPS - 1, stages)


def _pair_exchange_rider(g, stacked=False):
    shape = g.shape[1:] if stacked else g.shape

    def stages(ins, outs, send_sems, recv_sems):
        def copy():
            x, y, c, _ = _mesh_pos()
            src = ins[0].at[1 - c] if stacked else ins[0]
            return _remote(src, outs[0], send_sems.at[0], recv_sems.at[0], (x, y, 1 - c))

        return [(0.0, lambda: copy().start()), (1.0, lambda: copy().wait())]

    return _Rider((g,), (jax.ShapeDtypeStruct(shape, g.dtype),), {}, 1, stages)


def _pair_sum_in(mine, recv, sc):
    hd, d_in = mine.shape
    tr = min(hd, 256)

    def body(a_ref, b_ref, o_ref):
        o_ref[...] = (a_ref[...].astype(F32) + b_ref[...].astype(F32)).astype(BF16)

    blk = pl.BlockSpec((tr, sc), lambda i, j: (i, j))
    return pl.pallas_call(
        body, name="pair_sum_in", grid=(hd // tr, d_in // sc), in_specs=[blk, blk], out_specs=blk,
        out_shape=jax.ShapeDtypeStruct((hd, d_in), BF16),
        compiler_params=_params(dimension_semantics=("parallel", "parallel")),
    )(mine, recv)


def _pair_sum_out(gw_out, recv):
    _, n_chips, hr, d = gw_out.shape
    tr = min(hr, 256)
    c = lax.axis_index("c")

    def body(c_ref, a_ref, b_ref, o_ref):
        del c_ref
        o_ref[...] = (a_ref[...].astype(F32) + b_ref[...].astype(F32)).astype(BF16)

    blk = pl.BlockSpec((None, tr, d), lambda k, i, cc: (k, i, 0))
    gs = pltpu.PrefetchScalarGridSpec(
        num_scalar_prefetch=1, grid=(n_chips, hr // tr),
        in_specs=[pl.BlockSpec((None, None, tr, d), lambda k, i, cc: (cc[0], k, i, 0)), blk], out_specs=blk)
    return pl.pallas_call(
        body, name="pair_sum_out", grid_spec=gs, out_shape=jax.ShapeDtypeStruct((n_chips, hr, d), BF16),
        compiler_params=_params(dimension_semantics=("parallel", "parallel")),
    )(jnp.reshape(c, (1,)).astype(jnp.int32), gw_out, recv)


def _place():
    return jnp.stack([2 * lax.axis_index("x") + lax.axis_index("y"), lax.axis_index("c")]).astype(jnp.int32)


def _chip_sum_in(p_in, r_in, sc):
    hd = p_in.shape[0]
    tr = min(hd, 256)
    nblk = hd // tr

    def body(pos_ref, p_ref, r_ref, o_ref):
        del pos_ref
        acc = p_ref[...].astype(F32)
        for j in range(3):
            acc = acc + r_ref[j].astype(F32)
        o_ref[...] = acc

    gs = pltpu.PrefetchScalarGridSpec(
        num_scalar_prefetch=1, grid=(nblk,),
        in_specs=[pl.BlockSpec((tr, sc), lambda i, pos: (i, pos[0])), pl.BlockSpec((3, tr, sc), lambda i, pos: (0, i, 0))],
        out_specs=pl.BlockSpec((tr, sc), lambda i, pos: (pos[1] * nblk + i, 0)))
    return pl.pallas_call(
        body, name="chip_sum_in", grid_spec=gs, out_shape=jax.ShapeDtypeStruct((2 * hd, sc), F32),
        compiler_params=_params(dimension_semantics=("parallel",)),
    )(_place(), p_in, r_in)


def _chip_sum_out(p_out, r_out):
    _, hr, d = p_out.shape
    tr = min(hr, 256)
    nblk = hr // tr

    def body(pos_ref, p_ref, r_ref, o_ref):
        del pos_ref
        acc = p_ref[...].astype(F32)
        for j in range(3):
            acc = acc + r_ref[j].astype(F32)
        o_ref[...] = acc

    gs = pltpu.PrefetchScalarGridSpec(
        num_scalar_prefetch=1, grid=(nblk,),
        in_specs=[pl.BlockSpec((None, tr, d), lambda i, pos: (pos[0], i, 0)), pl.BlockSpec((3, tr, d), lambda i, pos: (0, i, 0))],
        out_specs=pl.BlockSpec((tr, d), lambda i, pos: (pos[1] * nblk + i, 0)))
    return pl.pallas_call(
        body, name="chip_sum_out", grid_spec=gs, out_shape=jax.ShapeDtypeStruct((2 * hr, d), F32),
        compiler_params=_params(dimension_semantics=("parallel",)),
    )(_place(), p_out, r_out)


def _adamw_math(w, g, m, v):
    m_new = ADAM_B1 * m + (1.0 - ADAM_B1) * g
    v_new = ADAM_B2 * v + (1.0 - ADAM_B2) * (g * g)
    m_hat = m_new / (1.0 - ADAM_B1 ** ADAM_STEP)
    v_hat = v_new / (1.0 - ADAM_B2 ** ADAM_STEP)
    delta = -ADAM_LR * (m_hat / (jnp.sqrt(v_hat) + ADAM_EPS) + ADAM_WD * w)
    return delta, m_new, v_new


def _share_halves(g_in, g_out):
    hd = g_in.shape[0] // 2
    hr = g_out.shape[0] // 2

    def body(gi_in, go_in, gi_ref, go_ref, send_sems, recv_sems):
        del gi_in, go_in
        x, y, c, _ = _mesh_pos()
        sibling = (x, y, 1 - c)
        mine_i = gi_ref.at[pl.ds(pl.multiple_of(c * hd, 8), hd), :]
        mine_o = go_ref.at[pl.ds(pl.multiple_of(c * hr, 8), hr), :]
        a = _remote(mine_i, mine_i, send_sems.at[0], recv_sems.at[0], sibling)
        b = _remote(mine_o, mine_o, send_sems.at[1], recv_sems.at[1], sibling)
        a.start()
        b.start()
        a.wait_send()
        b.wait_send()
        theirs_i = gi_ref.at[pl.ds(pl.multiple_of((1 - c) * hd, 8), hd), :]
        theirs_o = go_ref.at[pl.ds(pl.multiple_of((1 - c) * hr, 8), hr), :]
        _remote(theirs_i, theirs_i, send_sems.at[0], recv_sems.at[0], sibling).wait_recv()
        _remote(theirs_o, theirs_o, send_sems.at[1], recv_sems.at[1], sibling).wait_recv()

    return pl.pallas_call(
        body, name="share_halves",
        in_specs=[HBM_SPEC, HBM_SPEC], out_specs=[HBM_SPEC, HBM_SPEC],
        out_shape=[jax.ShapeDtypeStruct(g_in.shape, F32), jax.ShapeDtypeStruct(g_out.shape, F32)],
        input_output_aliases={0: 0, 1: 1},
        scratch_shapes=[pltpu.SemaphoreType.DMA((2,)), pltpu.SemaphoreType.DMA((2,))],
    )(g_in, g_out)


def _adamw(w, g, m, v, name):
    rows, cols = w.shape
    streams = 8
    fit = (VMEM_LIMIT_BYTES // 2) // (streams * 2 * cols * 4)
    tr = min(rows, 1 << (fit.bit_length() - 1))

    def body(w_ref, g_ref, m_ref, v_ref, d_ref, mo_ref, vo_ref, go_ref):
        gv = g_ref[...]
        delta, m_new, v_new = _adamw_math(w_ref[...], gv, m_ref[...], v_ref[...])
        d_ref[...] = delta
        mo_ref[...] = m_new
        vo_ref[...] = v_new
        go_ref[...] = gv

    spec = pl.BlockSpec((tr, cols), lambda i: (i, 0))
    shape = jax.ShapeDtypeStruct((rows, cols), F32)
    return pl.pallas_call(
        body, name=name, grid=(rows // tr,), in_specs=[spec] * 4, out_specs=[spec] * 4,
        out_shape=[shape] * 4, compiler_params=_params(dimension_semantics=("parallel",)),
    )(w, g, m, v)


SMALL_ROWS = 8


def _small_allreduce_adamw(part, w_pack, m_pack, v_pack):
    d = part.shape[1]

    def body(part_ref, w_ref, m_ref, v_ref, g_ref, d_ref, mo_ref, vo_ref, buf_ref, send_sems, recv_sems):
        x, y, c, _ = _mesh_pos()
        me = 4 * x + 2 * y + c
        buf_ref[0] = part_ref[...]
        copies = []
        for r in range(1, 8):
            rx, ry, rc = (r >> 2) & 1, (r >> 1) & 1, r & 1
            peer = (x ^ rx, y ^ ry, c ^ rc)
            copies.append(_remote(buf_ref.at[0], buf_ref.at[r], send_sems.at[r - 1], recv_sems.at[r - 1], peer))
        for cp in copies:
            cp.start()
        for cp in copies:
            cp.wait()
        total = buf_ref[me]
        for s in range(1, 8):
            total = total + buf_ref[s ^ me]
        w = w_ref[...]
        row = lax.broadcasted_iota(jnp.int32, (SMALL_ROWS, d), 0)
        l0, l1 = w[3:4], w[4:5]
        mx = jnp.maximum(l0, l1)
        e0, e1 = jnp.exp(l0 - mx), jnp.exp(l1 - mx)
        lb = e0 / (e0 + e1)
        g_l0 = total[3:4] * lb * (1.0 - lb)
        grads = jnp.where(row == 3, g_l0, jnp.where(row == 4, -g_l0, total))
        g_ref[...] = grads
        delta, m_new, v_new = _adamw_math(w, grads, m_ref[...], v_ref[...])
        d_ref[...] = delta
        mo_ref[...] = m_new
        vo_ref[...] = v_new

    vm = pl.BlockSpec(memory_space=pltpu.VMEM)
    shape = jax.ShapeDtypeStruct((SMALL_ROWS, d), F32)
    return pl.pallas_call(
        body, name="small_allreduce_adamw",
        in_specs=[vm] * 4, out_specs=[vm] * 4, out_shape=[shape] * 4,
        scratch_shapes=[pltpu.VMEM((8, SMALL_ROWS, d), F32), pltpu.SemaphoreType.DMA((7,)), pltpu.SemaphoreType.DMA((7,))],
    )(part, w_pack, m_pack, v_pack)


def _pack_small(d, pre, post, rnn, lb, sink, extra=None):
    rows = [pre, post, rnn, lb[0:1], lb[1:2],
            jnp.pad(sink, ((0, 0), (0, d - sink.shape[1]))),
            jnp.zeros((1, d), F32) if extra is None else extra,
            jnp.zeros((1, d), F32)]
    return jnp.concatenate(rows, axis=0)


def _unpack_small(p, n_sink):
    return dict(pre=p[0:1], post=p[1:2], rnn=p[2:3], lb=p[3:5], sink=p[5:6, :n_sink])


def kernel(x, w_in, attn_sinks, lb_logits, rnn_norm, w_out, pre_norm, post_norm, loss_target, m_w_in, m_attn_sinks, m_lb_logits, m_rnn_norm, m_w_out, m_pre_norm, m_post_norm, v_w_in, v_attn_sinks, v_lb_logits, v_rnn_norm, v_w_out, v_pre_norm, v_post_norm):
    t, d = x.shape[1], x.shape[2]
    sc = w_in.shape[2]
    n_sink = attn_sinks.shape[1]
    w_in2, w_out2 = w_in[0], w_out[0]

    w_in_part = _cast_into_gathered(w_in2, "cast_w_in", 1)
    w_out_part = _cast_into_gathered(w_out2, "cast_w_out", 0)
    loss_part, grad_x, (p_in, r_in), (p_out, r_out), small = _local_grads(
        x[0], loss_target[0], w_in_part, w_out_part, attn_sinks, lb_logits, rnn_norm, pre_norm, post_norm, sc)
    g_w_in, g_w_out = _share_halves(_chip_sum_in(p_in, r_in, sc), _chip_sum_out(p_out, r_out))
    d_w_in, nm_w_in, nv_w_in, g_w_in = _adamw(w_in2, g_w_in, m_w_in[0], v_w_in[0], "adamw_w_in")
    d_w_out, nm_w_out, nv_w_out, g_w_out = _adamw(w_out2, g_w_out, m_w_out[0], v_w_out[0], "adamw_w_out")

    lb_part = jnp.concatenate([small["lb"], jnp.zeros_like(small["lb"])], axis=0)
    loss_row = jnp.pad(loss_part[:, :1], ((0, 0), (0, d - 1)))
    part = _pack_small(d, small["pre"], small["post"], small["rnn"], lb_part, small["sink"], loss_row)
    w_pack = _pack_small(d, pre_norm, post_norm, rnn_norm, lb_logits, attn_sinks)
    m_pack = _pack_small(d, m_pre_norm, m_post_norm, m_rnn_norm, m_lb_logits, m_attn_sinks)
    v_pack = _pack_small(d, v_pre_norm, v_post_norm, v_rnn_norm, v_lb_logits, v_attn_sinks)
    g_pack, d_pack, nm_pack, nv_pack = _small_allreduce_adamw(part, w_pack, m_pack, v_pack)
    loss = g_pack[6, 0]
    g, dl, nm, nv = (_unpack_small(p, n_sink) for p in (g_pack, d_pack, nm_pack, nv_pack))

    def ordered(w_in_leaf, w_out_leaf, s):
        return (w_in_leaf[None], s["sink"], s["lb"], s["rnn"], w_out_leaf[None], s["pre"], s["post"])

    return (loss, grad_x[None],
            *ordered(g_w_in, g_w_out, g), *ordered(d_w_in, d_w_out, dl),
            *ordered(nm_w_in, nm_w_out, nm), *ordered(nv_w_in, nv_w_out, nv))
```

```python
import numpy as np
import jax
import jax.numpy as jnp
from jax import lax
from jax.experimental import pallas as pl
from jax.experimental.pallas import tpu as pltpu

F32 = jnp.float32
BF16 = jnp.bfloat16
MESH = pl.DeviceIdType.MESH

NORM_EPS = 1e-6
ATTN_HEAD_DIM = 64
GQA_GROUP = 8
WINDOW = 128
ATTN_STEP_BLOCKS = 8
ATTN_STEP_ROWS = ATTN_STEP_BLOCKS * WINDOW
RNN_HEAD_DIM = 128
CHUNK = 64
HALF_CHUNK = CHUNK // 2
ATTN_SCALE = ATTN_HEAD_DIM ** -0.5

ADAM_LR = 0.001
ADAM_B1 = 0.9
ADAM_B2 = 0.999
ADAM_EPS = 1e-08
ADAM_WD = 0.01
ADAM_STEP = 10

LANES = 128
COL_TILE = 512
RNN_GROUP_HEADS = 8
RNN_STEP_ROWS = 512
ATTN_GROUP_LANES = 1024
N_CHIPS = 4
VMEM_LIMIT_BYTES = 56 * 1024 * 1024
NEG_BIG = -1e30


def _params(**kw):
    return pltpu.CompilerParams(vmem_limit_bytes=VMEM_LIMIT_BYTES, **kw)


def _sigmoid(x):
    return 1.0 / (1.0 + jnp.exp(-x))


def _dot(a, b):
    return jnp.dot(a, b, preferred_element_type=F32)


def _dot_nt(a, b):
    return lax.dot_general(a, b, (((1,), (1,)), ((), ())), preferred_element_type=F32)


def _dot_tn(a, b):
    return lax.dot_general(a, b, (((0,), (0,)), ((), ())), preferred_element_type=F32)


def _tri_dot(tri_bf16, x):
    hi = x.astype(BF16)
    lo = (x - hi.astype(F32)).astype(BF16)
    return _dot(tri_bf16, hi) + _dot(tri_bf16, lo)


def _layout(d_model):
    d = d_model
    dkv = d // GQA_GROUP
    orig = dict(aq=0, ak=d, av=d + dkv, ag=d + 2 * dkv)
    base = d + 2 * dkv + d
    orig.update(rq=base, rf=base + d, ri=base + 2 * d, rg=base + 3 * d)
    group_w = RNN_GROUP_HEADS * RNN_HEAD_DIM
    cols = []
    for hg in range(d // group_w):
        for seg in ("rq", "rf", "ri", "rg"):
            cols.append((orig[seg] + hg * group_w, group_w))
    for m in range(d // ATTN_GROUP_LANES):
        for seg in ("aq", "ag"):
            cols.append((orig[seg] + m * ATTN_GROUP_LANES, ATTN_GROUP_LANES))
    cols.append((orig["ak"], dkv))
    cols.append((orig["av"], dkv))
    units = []
    for start, width in cols:
        assert start % LANES == 0 and width % LANES == 0
        units += [start + u for u in range(0, width, LANES)]
    per = COL_TILE // LANES
    assert len(units) % per == 0
    tiles = []
    for t in range(len(units) // per):
        run = units[t * per:(t + 1) * per]
        assert run[0] % COL_TILE == 0 and all(run[i] == run[0] + i * LANES for i in range(per))
        tiles.append(run[0] // COL_TILE)
    return dict(a_off=4 * d, k_off=6 * d, v_off=6 * d + dkv, total=6 * d + 2 * dkv,
                perm=np.asarray(tiles, np.int32))


def _chip_index():
    return jnp.reshape(2 * lax.axis_index("x") + lax.axis_index("y"), (1,)).astype(jnp.int32)


def _cast_into_gathered(a, name, axis):
    rows, cols = a.shape
    tr = min(rows, 512)
    nblk = rows // tr

    def body(me_ref, a_ref, o_ref):
        del me_ref
        o_ref[...] = a_ref[...].astype(BF16)

    if axis == 1:
        out_spec = pl.BlockSpec((tr, cols), lambda i, me: (i, me[0]))
        shape = (rows, N_CHIPS * cols)
    else:
        out_spec = pl.BlockSpec((tr, cols), lambda i, me: (me[0] * nblk + i, 0))
        shape = (N_CHIPS * rows, cols)
    gs = pltpu.PrefetchScalarGridSpec(num_scalar_prefetch=1, grid=(nblk,),
                                      in_specs=[pl.BlockSpec((tr, cols), lambda i, me: (i, 0))], out_specs=out_spec)
    return pl.pallas_call(
        body, name=name, grid_spec=gs, out_shape=jax.ShapeDtypeStruct(shape, BF16),
        compiler_params=_params(dimension_semantics=("parallel",)),
    )(_chip_index(), a)


def _prenorm_fwd(x, gain):
    t, d = x.shape
    tm = min(t, 256)

    def body(x_ref, g_ref, h_ref):
        xv = x_ref[...]
        r = lax.rsqrt(jnp.mean(xv * xv, axis=-1, keepdims=True) + NORM_EPS)
        h_ref[...] = ((xv * r) * g_ref[...]).astype(BF16)

    return pl.pallas_call(
        body, name="prenorm_fwd", grid=(t // tm,),
        in_specs=[pl.BlockSpec((tm, d), lambda i: (i, 0)), pl.BlockSpec((1, d), lambda i: (0, 0))],
        out_specs=pl.BlockSpec((tm, d), lambda i: (i, 0)),
        out_shape=jax.ShapeDtypeStruct((t, d), BF16),
        compiler_params=_params(dimension_semantics=("parallel",)),
    )(x, gain)


def _post_loss(x, y, target, gain):
    t, d = x.shape
    tm = min(t, 256)
    inv_d = 1.0 / d

    def body(x_ref, y_ref, t_ref, g_ref, dy_ref, dz_ref, gp_ref, loss_ref):
        i = pl.program_id(0)
        yv = y_ref[...]
        gain_v = g_ref[...]
        r = lax.rsqrt(jnp.mean(yv * yv, axis=-1, keepdims=True) + NORM_EPS)
        n = yv * r
        e = (x_ref[...] + n * gain_v) - t_ref[...]
        dz = e * inv_d
        dn = dz * gain_v
        dy = r * (dn - n * jnp.mean(dn * n, axis=-1, keepdims=True))
        dy_ref[...] = dy.astype(BF16)
        dz_ref[...] = dz

        @pl.when(i == 0)
        def _():
            gp_ref[...] = jnp.zeros_like(gp_ref)
            loss_ref[...] = jnp.zeros_like(loss_ref)

        gp_ref[...] += jnp.sum(dz * n, axis=0, keepdims=True)
        row = jnp.sum(e * e, axis=-1, keepdims=True)
        loss_ref[...] += jnp.full(loss_ref.shape, 0.5 * inv_d * jnp.sum(row), F32)

    row_spec = pl.BlockSpec((tm, d), lambda i: (i, 0))
    vec_spec = pl.BlockSpec((1, d), lambda i: (0, 0))
    return pl.pallas_call(
        body, name="post_loss", grid=(t // tm,),
        in_specs=[row_spec, row_spec, row_spec, vec_spec],
        out_specs=[row_spec, row_spec, vec_spec, pl.BlockSpec((1, LANES), lambda i: (0, 0))],
        out_shape=[jax.ShapeDtypeStruct((t, d), BF16), jax.ShapeDtypeStruct((t, d), F32),
                   jax.ShapeDtypeStruct((1, d), F32), jax.ShapeDtypeStruct((1, LANES), F32)],
        compiler_params=_params(dimension_semantics=("arbitrary",)),
    )(x, y, target, gain)


def _prenorm_bwd(x, dh, dz, gain):
    t, d = x.shape
    tm = min(t, 256)

    def body(x_ref, dh_ref, dz_ref, g_ref, gx_ref, gp_ref):
        i = pl.program_id(0)
        xv = x_ref[...]
        r = lax.rsqrt(jnp.mean(xv * xv, axis=-1, keepdims=True) + NORM_EPS)
        n = xv * r
        dhv = dh_ref[...]
        dn = dhv * g_ref[...]
        gx_ref[...] = dz_ref[...] + r * (dn - n * jnp.mean(dn * n, axis=-1, keepdims=True))

        @pl.when(i == 0)
        def _():
            gp_ref[...] = jnp.zeros_like(gp_ref)

        gp_ref[...] += jnp.sum(dhv * n, axis=0, keepdims=True)

    row_spec = pl.BlockSpec((tm, d), lambda i: (i, 0))
    vec_spec = pl.BlockSpec((1, d), lambda i: (0, 0))
    return pl.pallas_call(
        body, name="prenorm_bwd", grid=(t // tm,),
        in_specs=[row_spec, row_spec, row_spec, vec_spec],
        out_specs=[row_spec, vec_spec],
        out_shape=[jax.ShapeDtypeStruct((t, d), F32), jax.ShapeDtypeStruct((1, d), F32)],
        compiler_params=_params(dimension_semantics=("arbitrary",)),
    )(x, dh, dz, gain)


class _Rider:
    def __init__(self, operands, out_shapes, aliases, n_sems, stages):
        self.operands = tuple(operands)
        self.out_shapes = tuple(out_shapes)
        self.aliases = dict(aliases)
        self.n_sems = n_sems
        self.stages = stages

    def scratch(self):
        return [pltpu.SemaphoreType.DMA((self.n_sems,)), pltpu.SemaphoreType.DMA((self.n_sems,))]

    def emit(self, step, n_steps, in_refs, out_refs, send_sems, recv_sems):
        for frac, fn in self.stages(in_refs, out_refs, send_sems, recv_sems):
            at = min(n_steps - 1, int(frac * (n_steps - 1) + 0.5))
            pl.when(step == at)(fn)


def _matmul(name, a, b, *, out_shape, grid, a_spec, b_spec, o_spec, nt=False, ta=False, perm=None, rider=None):
    nk = grid[2]
    n_steps = grid[0] * grid[1] * grid[2]
    tm, tn = [s for s in o_spec.block_shape if s is not None][-2:]
    acc_in_out = out_shape.dtype == F32
    n_pre = 0 if perm is None else 1
    n_rin = 0 if rider is None else len(rider.operands)
    n_rout = 0 if rider is None else len(rider.out_shapes)
    use_acc = not (nk == 1 or acc_in_out)

    def body(*refs):
        refs = refs[n_pre:]
        a_ref, b_ref = refs[:2]
        rin = refs[2:2 + n_rin]
        o_ref = refs[2 + n_rin]
        rout = refs[3 + n_rin:3 + n_rin + n_rout]
        scratch_refs = refs[3 + n_rin + n_rout:]
        if rider is not None:
            step = (pl.program_id(0) * grid[1] + pl.program_id(1)) * grid[2] + pl.program_id(2)
            rider.emit(step, n_steps, rin, rout, scratch_refs[-2], scratch_refs[-1])
        def product():
            if ta:
                return _dot_tn(a_ref[...], b_ref[...])
            return _dot_nt(a_ref[...], b_ref[...]) if nt else _dot(a_ref[...], b_ref[...])

        if nk == 1:
            o_ref[...] = product().astype(o_ref.dtype)
            return
        acc_ref = o_ref if acc_in_out else scratch_refs[0]
        k = pl.program_id(2)

        @pl.when(k == 0)
        def _():
            acc_ref[...] = jnp.zeros_like(acc_ref)

        acc_ref[...] += product()

        if not acc_in_out:
            @pl.when(k == nk - 1)
            def _():
                o_ref[...] = acc_ref[...].astype(o_ref.dtype)

    scratch = [pltpu.VMEM((tm, tn), F32)] if use_acc else []
    in_specs = [a_spec, b_spec] + [HBM_SPEC] * n_rin
    out_specs = [o_spec] + [HBM_SPEC] * n_rout
    out_shapes = [out_shape]
    operands = [a, b]
    aliases = {}
    sem = ("parallel", "parallel", "arbitrary")
    if rider is not None:
        scratch += rider.scratch()
        out_shapes += list(rider.out_shapes)
        operands += list(rider.operands)
        aliases = {n_pre + 2 + i: 1 + o for i, o in rider.aliases.items()}
        sem = ("arbitrary", "arbitrary", "arbitrary")
    cp = _params(dimension_semantics=sem)
    if perm is None:
        return pl.pallas_call(body, name=name, grid=grid, in_specs=in_specs, out_specs=out_specs,
                              out_shape=out_shapes, scratch_shapes=scratch, input_output_aliases=aliases,
                              compiler_params=cp)(*operands)
    gs = pltpu.PrefetchScalarGridSpec(num_scalar_prefetch=1, grid=grid, in_specs=in_specs,
                                      out_specs=out_specs, scratch_shapes=scratch)
    return pl.pallas_call(body, name=name, grid_spec=gs, out_shape=out_shapes, input_output_aliases=aliases,
                          compiler_params=cp)(jnp.asarray(perm), *operands)


def _proj_mm(h, w_full, perm, rider=None):
    t, d = h.shape
    n_tiles = len(perm)
    tm = min(t, 1024)
    return _matmul(
        "proj_mm", h, w_full, perm=perm, rider=rider, grid=(t // tm, n_tiles, 1),
        out_shape=jax.ShapeDtypeStruct((t, n_tiles * COL_TILE), F32),
        a_spec=pl.BlockSpec((tm, d), lambda i, j, k, p: (i, 0)),
        b_spec=pl.BlockSpec((d, COL_TILE), lambda i, j, k, p: (0, p[j])),
        o_spec=pl.BlockSpec((tm, COL_TILE), lambda i, j, k, p: (i, j)))


def _proj_gather_mm(h, wi_part, perm, sc):
    t, d = h.shape
    n_tiles = len(perm)
    tm = min(t, 1024)
    n_i = t // tm
    hd = d // 2
    nf = sc // COL_TILE
    rem = sc - nf * COL_TILE
    assert 2 * rem == COL_TILE and n_tiles == N_CHIPS * nf + 2
    n_kinds = nf + 1
    last = nf
    ahead = min(6, nf)
    slots = []
    for q in range(nf):
        slots += [(0, q), (1, q)] + ([(2, q - 1)] if q else [])
    slots.append((2, nf - 1))
    rem_at = nf + len(slots)

    def first_full(chip):
        return (chip * sc + (rem if chip % 2 else 0)) // COL_TILE

    inverse = np.argsort(perm)
    table = np.zeros((N_CHIPS, 2, n_tiles), np.int32)
    for chip in range(N_CHIPS):
        seq = list(range(first_full(chip), first_full(chip) + nf))
        seq += [first_full((chip ^ 2, chip ^ 1, chip ^ 3)[j]) + q for j, q in slots]
        seq += [first_full(chip - chip % 2) + nf, first_full((chip ^ 2) - chip % 2) + nf]
        assert sorted(seq) == list(range(n_tiles)), seq
        table[chip, 0] = inverse[seq]
        table[chip, 1] = seq
    me_chip = 2 * lax.axis_index("x") + lax.axis_index("y")
    tab = lax.dynamic_index_in_dim(jnp.asarray(table), me_chip, 0, keepdims=False)

    def body(tab_ref, h_hbm, wi_in, proj_ref, full, hbuf, bbuf, local_sems, send_sems, recv_sems):
        del wi_in
        jj = pl.program_id(0)
        i = pl.program_id(1)
        x, y, c, chips = _mesh_pos()
        sibling = (x, y, 1 - c)

        def chip_of(j):
            return 2 * chips[j][0] + chips[j][1]

        def piece(chip, half, kind, part=None):
            odd = chip % 2
            if kind == last:
                start, width = chip * sc + (1 - odd) * (nf * COL_TILE), rem
            else:
                start, width = chip * sc + odd * rem + kind * COL_TILE, COL_TILE
            if part is not None:
                width //= 2
                start = start + part * width
            return full.at[pl.ds(half * hd, hd), pl.ds(pl.multiple_of(start, LANES), width)]

        def ici(j, kind):
            mine = piece(2 * x + y, c, kind)
            k = j * n_kinds + kind
            return _remote(mine, mine, send_sems.at[k], recv_sems.at[k], (chips[j][0], chips[j][1], c))

        def landed(j, kind):
            blk = piece(chip_of(j), c, kind)
            k = j * n_kinds + kind
            return _remote(blk, blk, send_sems.at[k], recv_sems.at[k], sibling)

        def hop(p, kind):
            blk = piece(chip_of(p), c, kind, part=p)
            k = (2 + p) * n_kinds + kind
            return _remote(blk, blk, send_sems.at[k], recv_sems.at[k], (chips[1 - p][0], chips[1 - p][1], c))

        def hopped(p, kind):
            blk = piece(chip_of(2), c, kind, part=p)
            k = (2 + p) * n_kinds + kind
            return _remote(blk, blk, send_sems.at[k], recv_sems.at[k], sibling)

        def passed(j, kind, half):
            blk = piece(chip_of(j), half, kind)
            k = (4 + j) * n_kinds + kind
            return _remote(blk, blk, send_sems.at[k], recv_sems.at[k], sibling)

        def fetch(pos, slot):
            col = pl.multiple_of(tab_ref[1, pos] * COL_TILE, LANES)
            return pltpu.make_async_copy(full.at[:, pl.ds(col, COL_TILE)], bbuf.at[slot], local_sems.at[slot])

        def load_h():
            return pltpu.make_async_copy(h_hbm, hbuf, local_sems.at[2])

        def send(kind):
            for j in range(2):
                ici(j, kind).start()

        def relay(j, kind):
            if j == 2:
                hopped(0, kind).wait_recv()
                hopped(1, kind).wait_recv()
            else:
                landed(j, kind).wait_recv()
                hop(j, kind).start()
            passed(j, kind, c).start()
            if j == 0 and kind + ahead < n_kinds:
                send(kind + ahead)

        events = {}
        for s, (j, kind) in enumerate(slots):
            events.setdefault(nf + s - 2, []).append(lambda j=j, kind=kind: relay(j, kind))
            events.setdefault(nf + s - 1, []).append(lambda j=j, kind=kind: passed(j, kind, 1 - c).wait_recv())
        for j in range(3):
            events.setdefault(rem_at - 4 + j, []).append(lambda j=j: relay(j, last))
            events.setdefault(rem_at - 1, []).append(lambda j=j: passed(j, last, 1 - c).wait_recv())

        @pl.when(i == 0)
        def _():
            @pl.when(jj == 0)
            def _():
                load_h().start()
                for kind in range(ahead):
                    send(kind)
                fetch(0, 0).start()
                load_h().wait()

            for pos in sorted(events):
                def run(pos=pos):
                    for fn in events[pos]:
                        fn()
                pl.when(jj == pos)(run)

            @pl.when(jj + 1 < n_tiles)
            def _():
                fetch(jj + 1, (jj + 1) % 2).start()

            fetch(jj, jj % 2).wait()

            @pl.when(jj == n_tiles - 1)
            def _():
                for kind in range(n_kinds):
                    for j in range(2):
                        ici(j, kind).wait_send()
                        hop(j, kind).wait_send()
                    for j in range(3):
                        passed(j, kind, c).wait_send()

        rows = pl.ds(pl.multiple_of(i * tm, tm), tm)
        proj_ref[...] = _dot(hbuf[rows, :], bbuf[jj % 2])

    gs = pltpu.PrefetchScalarGridSpec(
        num_scalar_prefetch=1, grid=(n_tiles, n_i),
        in_specs=[HBM_SPEC, HBM_SPEC],
        out_specs=[pl.BlockSpec((tm, COL_TILE), lambda jj, i, tb: (i, tb[0, jj])), HBM_SPEC],
        scratch_shapes=[pltpu.VMEM((t, d), BF16), pltpu.VMEM((2, d, COL_TILE), BF16), pltpu.SemaphoreType.DMA((3,)),
                        pltpu.SemaphoreType.DMA((7 * n_kinds,)), pltpu.SemaphoreType.DMA((7 * n_kinds,))])
    return pl.pallas_call(
        body, name="proj_gather_mm", grid_spec=gs,
        out_shape=[jax.ShapeDtypeStruct((t, n_tiles * COL_TILE), F32), jax.ShapeDtypeStruct(wi_part.shape, BF16)],
        input_output_aliases={2: 1},
        compiler_params=_params(dimension_semantics=("arbitrary", "arbitrary")),
    )(tab, h, wi_part)


def _gw_in_mm(name, h, dproj, perm, half, rider=None):
    t, d = h.shape
    n_tiles = len(perm)
    hd = d // 2
    tm = min(hd, 1024)
    per_half = hd // tm
    table = jnp.concatenate([jnp.asarray(perm), jnp.reshape(half, (1,)).astype(jnp.int32)])
    return _matmul(
        name, h, dproj, perm=table, rider=rider, ta=True, grid=(per_half, n_tiles, 1),
        out_shape=jax.ShapeDtypeStruct((hd, n_tiles * COL_TILE), BF16),
        a_spec=pl.BlockSpec((t, tm), lambda i, j, k, p: (0, p[n_tiles] * per_half + i)),
        b_spec=pl.BlockSpec((t, COL_TILE), lambda i, j, k, p: (0, j)),
        o_spec=pl.BlockSpec((tm, COL_TILE), lambda i, j, k, p: (i, p[j])))


def _dh_mm(dproj, w_full, perm, rider=None):
    t = dproj.shape[0]
    d = w_full.shape[0]
    n_tiles = len(perm)
    tm = min(t, 2048)
    tn = min(d, 2048)
    return _matmul(
        "dh_mm", dproj, w_full, perm=perm, rider=rider, nt=True, grid=(t // tm, d // tn, n_tiles),
        out_shape=jax.ShapeDtypeStruct((t, d), F32),
        a_spec=pl.BlockSpec((tm, COL_TILE), lambda i, j, k, p: (i, k)),
        b_spec=pl.BlockSpec((tn, COL_TILE), lambda i, j, k, p: (j, p[k])),
        o_spec=pl.BlockSpec((tm, tn), lambda i, j, k, p: (i, j)))


def _out_mm(mixed, w_out_full):
    t, dm = mixed.shape
    d = w_out_full.shape[1]
    tm = min(t, 1024)
    tn = min(d, 512)
    tk = min(dm, 4096)
    return _matmul(
        "out_mm", mixed, w_out_full, grid=(t // tm, d // tn, dm // tk),
        out_shape=jax.ShapeDtypeStruct((t, d), F32),
        a_spec=pl.BlockSpec((tm, tk), lambda i, j, k: (i, k)),
        b_spec=pl.BlockSpec((tk, tn), lambda i, j, k: (k, j)),
        o_spec=pl.BlockSpec((tm, tn), lambda i, j, k: (i, j)))[0]


def _dmixed_mm(dy, w_out_full, rider=None):
    t, d = dy.shape
    dm = w_out_full.shape[0]
    tm = min(t, 1024)
    tn = min(dm, 1024)
    return _matmul(
        "dmixed_mm", dy, w_out_full, nt=True, rider=rider, grid=(t // tm, dm // tn, 1),
        out_shape=jax.ShapeDtypeStruct((t, dm), F32),
        a_spec=pl.BlockSpec((tm, d), lambda i, j, k: (i, 0)),
        b_spec=pl.BlockSpec((tn, d), lambda i, j, k: (j, 0)),
        o_spec=pl.BlockSpec((tm, tn), lambda i, j, k: (i, j)))


def _gw_out_mm(mixed, dy):
    t, dm = mixed.shape
    d = dy.shape[1]
    hr = dm // (2 * N_CHIPS)
    tn = min(d, 1024)
    return _matmul(
        "gw_out_mm", mixed, dy, ta=True, grid=(dm // hr, d // tn, 1),
        out_shape=jax.ShapeDtypeStruct((2, N_CHIPS, hr, d), BF16),
        a_spec=pl.BlockSpec((t, hr), lambda i, j, k: (0, i)),
        b_spec=pl.BlockSpec((t, tn), lambda i, j, k: (0, j)),
        o_spec=pl.BlockSpec((None, None, hr, tn), lambda i, j, k: (i % 2, i // 2, 0, j)))[0]


def _lane_half():
    return lax.broadcasted_iota(jnp.int32, (WINDOW, LANES), 1) // ATTN_HEAD_DIM


def _dup_kv(tile, kh):
    return jnp.where(_lane_half() == kh, tile, pltpu.roll(tile, ATTN_HEAD_DIM, 1))


def _stack_heads(tiles, kh):
    half = _lane_half()
    pieces = []
    for g in range(GQA_GROUP):
        pieces.append(jnp.where(half == g % 2, tiles[4 * kh + g // 2], 0.0))
    return jnp.concatenate(pieces, axis=0)


def _unstack_heads(stacked):
    half = _lane_half()
    out = []
    for j in range(GQA_GROUP // 2):
        a = stacked[(2 * j) * WINDOW:(2 * j + 1) * WINDOW]
        b = stacked[(2 * j + 1) * WINDOW:(2 * j + 2) * WINDOW]
        out.append(jnp.where(half == 0, a, b))
    return out


def _attn_probs(qs, kcat, sink_col, n):
    rows = GQA_GROUP * WINDOW
    s = _dot_nt(qs, kcat)
    qi = lax.broadcasted_iota(jnp.int32, (rows, 2 * WINDOW), 0) % WINDOW
    kj = lax.broadcasted_iota(jnp.int32, (rows, 2 * WINDOW), 1)
    first_key = WINDOW * (1 - jnp.minimum(n, 1))
    valid = (kj > qi) & (kj <= qi + WINDOW) & (kj >= first_key)
    s = jnp.where(valid, s, NEG_BIG)
    mx = jnp.maximum(jnp.max(s, axis=-1, keepdims=True), sink_col)
    p = jnp.exp(s - mx)
    p_sink = jnp.exp(sink_col - mx)
    inv = 1.0 / (jnp.sum(p, axis=-1, keepdims=True) + p_sink)
    return p * inv, p_sink * inv


def _attn_operands(sink_ref, q_tiles, k_prev, k_cur, v_prev, v_cur, m, kh):
    qs = _stack_heads([qt * ATTN_SCALE for qt in q_tiles], kh).astype(BF16)
    kcat = jnp.concatenate([_dup_kv(k_prev, kh), _dup_kv(k_cur, kh)], axis=0).astype(BF16)
    vcat = jnp.concatenate([_dup_kv(v_prev, kh), _dup_kv(v_cur, kh)], axis=0).astype(BF16)
    heads_per_group = ATTN_GROUP_LANES // ATTN_HEAD_DIM
    sink_col = jnp.concatenate(
        [jnp.full((WINDOW, 1), sink_ref[0, m * heads_per_group + kh * GQA_GROUP + g], F32)
         for g in range(GQA_GROUP)], axis=0)
    return qs, kcat, vcat, sink_col


def _attn_specs(lay, d):
    a_blk = lay["a_off"] // (2 * ATTN_GROUP_LANES)
    k_blk = lay["k_off"] // LANES
    v_blk = lay["v_off"] // LANES
    before = lambda n: jnp.maximum(ATTN_STEP_BLOCKS * n - 1, 0)
    qg = pl.BlockSpec((ATTN_STEP_ROWS, 2 * ATTN_GROUP_LANES), lambda m, n: (n, a_blk + m))
    kp = pl.BlockSpec((WINDOW, LANES), lambda m, n: (before(n), k_blk + m))
    kc = pl.BlockSpec((ATTN_STEP_ROWS, LANES), lambda m, n: (n, k_blk + m))
    vp = pl.BlockSpec((WINDOW, LANES), lambda m, n: (before(n), v_blk + m))
    vc = pl.BlockSpec((ATTN_STEP_ROWS, LANES), lambda m, n: (n, v_blk + m))
    return qg, kp, kc, vp, vc


def _block_rows(b):
    return slice(b * WINDOW, (b + 1) * WINDOW)


def _kv_tiles(prev_ref, cur_ref, b):
    prev = prev_ref[...] if b == 0 else cur_ref[_block_rows(b - 1), :]
    return prev, cur_ref[_block_rows(b), :]


def _attn_fwd(proj, sinks, lay, d, rider=None):
    t = proj.shape[0]
    n_groups = d // ATTN_GROUP_LANES
    n_blocks = t // ATTN_STEP_ROWS
    pairs = ATTN_GROUP_LANES // LANES
    n_rin = 0 if rider is None else len(rider.operands)
    n_rout = 0 if rider is None else len(rider.out_shapes)

    def body(*refs):
        sink_ref, qg_ref, kp_ref, kc_ref, vp_ref, vc_ref = refs[:6]
        rin = refs[6:6 + n_rin]
        mix_ref, o_ref = refs[6 + n_rin:8 + n_rin]
        rout = refs[8 + n_rin:8 + n_rin + n_rout]
        m = pl.program_id(0)
        n = pl.program_id(1)
        if rider is not None:
            rider.emit(m * n_blocks + n, n_groups * n_blocks, rin, rout, refs[-2], refs[-1])
        for b in range(ATTN_STEP_BLOCKS):
            rows = _block_rows(b)
            k_prev, k_cur = _kv_tiles(kp_ref, kc_ref, b)
            v_prev, v_cur = _kv_tiles(vp_ref, vc_ref, b)
            q_tiles = [qg_ref[rows, p * LANES:(p + 1) * LANES] for p in range(pairs)]
            for kh in range(2):
                qs, kcat, vcat, sink_col = _attn_operands(sink_ref, q_tiles, k_prev, k_cur, v_prev, v_cur, m, kh)
                probs, _ = _attn_probs(qs, kcat, sink_col, ATTN_STEP_BLOCKS * n + b)
                out = _dot(probs.astype(BF16), vcat)
                for j, tile in enumerate(_unstack_heads(out)):
                    p = 4 * kh + j
                    lanes = slice(p * LANES, (p + 1) * LANES)
                    gate = qg_ref[rows, ATTN_GROUP_LANES + p * LANES:ATTN_GROUP_LANES + (p + 1) * LANES]
                    o_ref[rows, lanes] = tile
                    mix_ref[rows, lanes] = (tile * (gate * _sigmoid(gate))).astype(BF16)

    qg, kp, kc, vp, vc = _attn_specs(lay, d)
    out_blk = pl.BlockSpec((ATTN_STEP_ROWS, ATTN_GROUP_LANES), lambda m, n: (n, m))
    out_shapes = [jax.ShapeDtypeStruct((t, 2 * d), BF16), jax.ShapeDtypeStruct((t, d), F32)]
    operands = [sinks, proj, proj, proj, proj, proj]
    scratch, aliases, sem = [], {}, ("parallel", "parallel")
    if rider is not None:
        scratch = rider.scratch()
        out_shapes += list(rider.out_shapes)
        operands += list(rider.operands)
        aliases = {6 + i: 2 + o for i, o in rider.aliases.items()}
        sem = ("arbitrary", "arbitrary")
    return pl.pallas_call(
        body, name="attn_fwd", grid=(n_groups, n_blocks),
        in_specs=[pl.BlockSpec(memory_space=pltpu.SMEM), qg, kp, kc, vp, vc] + [HBM_SPEC] * n_rin,
        out_specs=[out_blk, out_blk] + [HBM_SPEC] * n_rout,
        out_shape=out_shapes, scratch_shapes=scratch, input_output_aliases=aliases,
        compiler_params=_params(dimension_semantics=sem),
    )(*operands)


def _attn_bwd(proj, sinks, attn_o, dmixed, dproj, lay, d):
    t = proj.shape[0]
    n_groups = d // ATTN_GROUP_LANES
    pairs = ATTN_GROUP_LANES // LANES
    kv_w = n_groups * LANES

    def body(sink_ref, qg_ref, kp_ref, kc_ref, vp_ref, vc_ref, o_ref, dm_ref, dproj_hbm,
             dqg_ref, dkc_ref, dkp_ref, dvc_ref, dvp_ref, dsink_ref):
        del dproj_hbm
        m = pl.program_id(0)
        n = pl.program_id(1)
        half = _lane_half()
        sub = lax.broadcasted_iota(jnp.int32, (8, LANES), 0)
        lane = lax.broadcasted_iota(jnp.int32, (8, LANES), 1)
        dsink = jnp.zeros((8, LANES), F32)
        for b in range(ATTN_STEP_BLOCKS):
            rows = _block_rows(b)
            k_prev, k_cur = _kv_tiles(kp_ref, kc_ref, b)
            v_prev, v_cur = _kv_tiles(vp_ref, vc_ref, b)
            q_tiles = [qg_ref[rows, p * LANES:(p + 1) * LANES] for p in range(pairs)]
            do_tiles, o_tiles = [], []
            for p in range(pairs):
                lanes = slice(p * LANES, (p + 1) * LANES)
                gate_lanes = slice(ATTN_GROUP_LANES + p * LANES, ATTN_GROUP_LANES + (p + 1) * LANES)
                gate = qg_ref[rows, gate_lanes]
                sg = _sigmoid(gate)
                dmix = dm_ref[rows, lanes]
                ov = o_ref[rows, lanes]
                dqg_ref[rows, gate_lanes] = (dmix * ov * (sg * (1.0 + gate * (1.0 - sg)))).astype(BF16)
                do_tiles.append(dmix * (gate * sg))
                o_tiles.append(ov)

            dk_cur = dk_prev = dv_cur = dv_prev = jnp.zeros((WINDOW, LANES), F32)
            for kh in range(2):
                qs, kcat, vcat, sink_col = _attn_operands(sink_ref, q_tiles, k_prev, k_cur, v_prev, v_cur, m, kh)
                probs, p_sink = _attn_probs(qs, kcat, sink_col, ATTN_STEP_BLOCKS * n + b)
                dos = _stack_heads(do_tiles, kh)
                delta = jnp.sum(dos * _stack_heads(o_tiles, kh), axis=-1, keepdims=True)
                dos = dos.astype(BF16)
                dp = _dot_nt(dos, vcat)
                ds = (probs * (dp - delta)).astype(BF16)
                dv = _dot_tn(probs.astype(BF16), dos)
                dv = dv + pltpu.roll(dv, ATTN_HEAD_DIM, 1)
                dk = _dot_tn(ds, qs)
                dk = dk + pltpu.roll(dk, ATTN_HEAD_DIM, 1)
                dq = _dot(ds, kcat)
                for j, tile in enumerate(_unstack_heads(dq)):
                    p = 4 * kh + j
                    dqg_ref[rows, p * LANES:(p + 1) * LANES] = (tile * ATTN_SCALE).astype(BF16)
                dk_prev = jnp.where(half == kh, dk[:WINDOW], dk_prev)
                dk_cur = jnp.where(half == kh, dk[WINDOW:], dk_cur)
                dv_prev = jnp.where(half == kh, dv[:WINDOW], dv_prev)
                dv_cur = jnp.where(half == kh, dv[WINDOW:], dv_cur)
                sink_terms = p_sink * delta
                for g in range(GQA_GROUP):
                    val = -jnp.sum(sink_terms[g * WINDOW:(g + 1) * WINDOW])
                    dsink = dsink + jnp.where((sub == 0) & (lane == kh * GQA_GROUP + g), val, 0.0)
            dkc_ref[rows, :] = dk_cur
            dkp_ref[rows, :] = dk_prev
            dvc_ref[rows, :] = dv_cur
            dvp_ref[rows, :] = dv_prev

        @pl.when(n == 0)
        def _():
            dsink_ref[...] = jnp.zeros_like(dsink_ref)

        dsink_ref[...] += dsink

    qg, kp, kc, vp, vc = _attn_specs(lay, d)
    a_blk = lay["a_off"] // (2 * ATTN_GROUP_LANES)
    grp = pl.BlockSpec((ATTN_STEP_ROWS, ATTN_GROUP_LANES), lambda m, n: (n, m))
    kv_blk = pl.BlockSpec((ATTN_STEP_ROWS, LANES), lambda m, n: (n, m))
    kv_shape = jax.ShapeDtypeStruct((t, kv_w), F32)
    outs = pl.pallas_call(
        body, name="attn_bwd", grid=(n_groups, t // ATTN_STEP_ROWS),
        in_specs=[pl.BlockSpec(memory_space=pltpu.SMEM), qg, kp, kc, vp, vc, grp, grp,
                  pl.BlockSpec(memory_space=pl.ANY)],
        out_specs=[pl.BlockSpec((ATTN_STEP_ROWS, 2 * ATTN_GROUP_LANES), lambda m, n: (n, a_blk + m)),
                   kv_blk, kv_blk, kv_blk, kv_blk, pl.BlockSpec((8, LANES), lambda m, n: (m, 0))],
        out_shape=[jax.ShapeDtypeStruct(dproj.shape, BF16), kv_shape, kv_shape, kv_shape, kv_shape,
                   jax.ShapeDtypeStruct((n_groups * 8, LANES), F32)],
        input_output_aliases={8: 0},
        compiler_params=_params(dimension_semantics=("parallel", "arbitrary")),
    )(sinks, proj, proj, proj, proj, proj, attn_o, dmixed, dproj)
    return outs


def _kv_combine(dkc, dkp, dvc, dvp, dproj, lay):
    t, kv_w = dkc.shape
    nb = t // WINDOW
    kv_blk_idx = lay["k_off"] // (2 * kv_w)

    def body(dkc_ref, dkp_ref, dvc_ref, dvp_ref, dproj_hbm, o_ref):
        del dproj_hbm
        keep = (pl.program_id(0) < nb - 1).astype(F32)
        o_ref[:, :kv_w] = (dkc_ref[...] + keep * dkp_ref[...]).astype(BF16)
        o_ref[:, kv_w:] = (dvc_ref[...] + keep * dvp_ref[...]).astype(BF16)

    cur = pl.BlockSpec((WINDOW, kv_w), lambda n: (n, 0))
    nxt = pl.BlockSpec((WINDOW, kv_w), lambda n: (jnp.minimum(n + 1, nb - 1), 0))
    return pl.pallas_call(
        body, name="kv_combine", grid=(nb,),
        in_specs=[cur, nxt, cur, nxt, pl.BlockSpec(memory_space=pl.ANY)],
        out_specs=pl.BlockSpec((WINDOW, 2 * kv_w), lambda n: (n, kv_blk_idx)),
        out_shape=jax.ShapeDtypeStruct(dproj.shape, BF16),
        input_output_aliases={4: 0},
        compiler_params=_params(dimension_semantics=("parallel",)),
    )(dkc, dkp, dvc, dvp, dproj)


def _lower_bound(lbl_ref):
    l0 = lbl_ref[0:1, :]
    l1 = lbl_ref[1:2, :]
    mx = jnp.maximum(l0, l1)
    e0 = jnp.exp(l0 - mx)
    e1 = jnp.exp(l1 - mx)
    return e0 / (e0 + e1)


def _chunk_masks():
    ti = lax.broadcasted_iota(jnp.int32, (CHUNK, CHUNK), 0)
    si = lax.broadcasted_iota(jnp.int32, (CHUNK, CHUNK), 1)
    diag = ((ti // HALF_CHUNK) == (si // HALF_CHUNK)) & (si <= ti)
    off = (ti >= HALF_CHUNK) & (si < HALF_CHUNK)
    lower = (si <= ti).astype(BF16)
    upper = (si >= ti).astype(BF16)
    return diag, off, lower, upper


def _rnn_gates(rq, rf, lb):
    sf = _sigmoid(rf)
    f = lb + (1.0 - lb) * sf
    sq = _sigmoid(rq)
    return sf, f, jnp.log(f), 1.0 - f, sq, rq * sq


def _rnn_decays(g_cum):
    row = lax.broadcasted_iota(jnp.int32, g_cum.shape, 0)
    ref_d = jnp.where(row < HALF_CHUNK, g_cum[HALF_CHUNK // 2 - 1:HALF_CHUNK // 2],
                      g_cum[HALF_CHUNK + HALF_CHUNK // 2 - 1:HALF_CHUNK + HALF_CHUNK // 2])
    ref_o = g_cum[HALF_CHUNK - 1:HALF_CHUNK]
    last = g_cum[CHUNK - 1:CHUNK]
    return dict(eq_d=jnp.exp(g_cum - ref_d), ek_d=jnp.exp(ref_d - g_cum),
                eq_o=jnp.exp(jnp.minimum(g_cum - ref_o, 0.0)), ek_o=jnp.exp(jnp.minimum(ref_o - g_cum, 0.0)),
                eg=jnp.exp(g_cum), ekl=jnp.exp(last - g_cum), e_last=jnp.exp(last))


def _head(a, j):
    return a[:, j * RNN_HEAD_DIM:(j + 1) * RNN_HEAD_DIM]


def _rnn_specs(t, tb, d):
    gw = RNN_GROUP_HEADS * RNN_HEAD_DIM
    return gw, t // tb, tb // CHUNK


def _rnn_fwd(proj, lb_logits, rnn_gain, mixed, d, rider=None):
    t = proj.shape[0]
    tb = min(t, RNN_STEP_ROWS)
    gw, ntb, nch = _rnn_specs(t, tb, d)
    n_groups = d // gw
    n_heads = d // RNN_HEAD_DIM
    n_rin = 0 if rider is None else len(rider.operands)
    n_rout = 0 if rider is None else len(rider.out_shapes)

    def body(*refs):
        blk_ref, lbl_ref, gain_ref = refs[:3]
        rin = refs[4:4 + n_rin]
        mix_ref, o_ref, st_out_ref = refs[4 + n_rin:7 + n_rin]
        rout = refs[7 + n_rin:7 + n_rin + n_rout]
        st_ref = refs[7 + n_rin + n_rout]
        if rider is not None:
            rider.emit(pl.program_id(0) * ntb + pl.program_id(1), n_groups * ntb, rin, rout, refs[-2], refs[-1])

        @pl.when(pl.program_id(1) == 0)
        def _():
            st_ref[...] = jnp.zeros_like(st_ref)

        lb = _lower_bound(lbl_ref)
        gain = gain_ref[...]
        diag, off, lower, _ = _chunk_masks()

        def chunk(c, carry):
            rows = pl.ds(pl.multiple_of(c * CHUNK, CHUNK), CHUNK)
            rq = blk_ref[rows, 0:gw]
            rf = blk_ref[rows, gw:2 * gw]
            v = blk_ref[rows, 2 * gw:3 * gw]
            rg = blk_ref[rows, 3 * gw:4 * gw]
            _, _, g, k, _, q = _rnn_gates(rq, rf, lb)
            dec = _rnn_decays(_tri_dot(lower, g))
            qd = (q * dec["eq_d"]).astype(BF16)
            kd = (k * dec["ek_d"]).astype(BF16)
            qo = (q * dec["eq_o"]).astype(BF16)
            ko = (k * dec["ek_o"]).astype(BF16)
            qe = (q * dec["eg"]).astype(BF16)
            kl = (k * dec["ekl"]).astype(BF16)
            vb = v.astype(BF16)
            outs = []
            for j in range(RNN_GROUP_HEADS):
                st = st_ref[j]
                st_out_ref[j, c] = st
                attn = jnp.where(diag, _dot_nt(_head(qd, j), _head(kd, j)),
                                 jnp.where(off, _dot_nt(_head(qo, j), _head(ko, j)), 0.0))
                o = _dot(attn.astype(BF16), _head(vb, j)) + _dot_nt(_head(qe, j), st.astype(BF16))
                st_ref[j] = st * _head(dec["e_last"], j) + _dot_tn(_head(vb, j), _head(kl, j))
                rr = lax.rsqrt(jnp.mean(o * o, axis=-1, keepdims=True) + NORM_EPS)
                o_ref[rows, j * RNN_HEAD_DIM:(j + 1) * RNN_HEAD_DIM] = o
                outs.append(o * rr)
            on = jnp.concatenate(outs, axis=1) * gain
            mix_ref[rows, :] = (on * (rg * _sigmoid(rg))).astype(BF16)
            return carry

        lax.fori_loop(0, nch, chunk, 0, unroll=True)

    out_shapes = [jax.ShapeDtypeStruct(mixed.shape, BF16), jax.ShapeDtypeStruct((t, d), F32),
                  jax.ShapeDtypeStruct((n_heads, t // CHUNK, RNN_HEAD_DIM, RNN_HEAD_DIM), F32)]
    operands = [proj, lb_logits, rnn_gain, mixed]
    scratch = [pltpu.VMEM((RNN_GROUP_HEADS, RNN_HEAD_DIM, RNN_HEAD_DIM), F32)]
    aliases, sem = {3: 0}, ("parallel", "arbitrary")
    if rider is not None:
        scratch += rider.scratch()
        out_shapes += list(rider.out_shapes)
        operands += list(rider.operands)
        aliases.update({4 + i: 3 + o for i, o in rider.aliases.items()})
        sem = ("arbitrary", "arbitrary")
    return pl.pallas_call(
        body, name="rnn_fwd", grid=(n_groups, ntb),
        in_specs=[pl.BlockSpec((tb, 4 * gw), lambda h, i: (i, h)),
                  pl.BlockSpec((2, gw), lambda h, i: (0, h)),
                  pl.BlockSpec((1, gw), lambda h, i: (0, h)),
                  pl.BlockSpec(memory_space=pl.ANY)] + [HBM_SPEC] * n_rin,
        out_specs=[pl.BlockSpec((tb, gw), lambda h, i: (i, d // gw + h)),
                   pl.BlockSpec((tb, gw), lambda h, i: (i, h)),
                   pl.BlockSpec((RNN_GROUP_HEADS, nch, RNN_HEAD_DIM, RNN_HEAD_DIM), lambda h, i: (h, i, 0, 0))]
        + [HBM_SPEC] * n_rout,
        out_shape=out_shapes, scratch_shapes=scratch, input_output_aliases=aliases,
        compiler_params=_params(dimension_semantics=sem),
    )(*operands)


def _rnn_bwd(proj, lb_logits, rnn_gain, o_pre, states, dmixed, d_total, d, rider=None):
    t = proj.shape[0]
    tb = min(t, RNN_STEP_ROWS)
    gw, ntb, nch = _rnn_specs(t, tb, d)
    n_groups = d // gw
    n_rin = 0 if rider is None else len(rider.operands)
    n_rout = 0 if rider is None else len(rider.out_shapes)

    def body(*refs):
        blk_ref, lbl_ref, gain_ref, o_ref, st_in_ref, dm_ref = refs[:6]
        rin = refs[6:6 + n_rin]
        dproj_ref, dgain_ref, dlb_ref = refs[6 + n_rin:9 + n_rin]
        rout = refs[9 + n_rin:9 + n_rin + n_rout]
        dst_ref = refs[9 + n_rin + n_rout]
        if rider is not None:
            rider.emit(pl.program_id(0) * ntb + pl.program_id(1), n_groups * ntb, rin, rout, refs[-2], refs[-1])

        @pl.when(pl.program_id(1) == 0)
        def _():
            dst_ref[...] = jnp.zeros_like(dst_ref)
            dgain_ref[...] = jnp.zeros_like(dgain_ref)
            dlb_ref[...] = jnp.zeros_like(dlb_ref)

        lb = _lower_bound(lbl_ref)
        gain = gain_ref[...]
        diag, off, lower, upper = _chunk_masks()
        last_row = lax.broadcasted_iota(jnp.int32, (CHUNK, RNN_HEAD_DIM), 0) == CHUNK - 1

        def chunk(step, carry):
            c = nch - 1 - step
            rows = pl.ds(pl.multiple_of(c * CHUNK, CHUNK), CHUNK)
            rq = blk_ref[rows, 0:gw]
            rf = blk_ref[rows, gw:2 * gw]
            v = blk_ref[rows, 2 * gw:3 * gw]
            rg = blk_ref[rows, 3 * gw:4 * gw]
            sf, f, g, k, sq, q = _rnn_gates(rq, rf, lb)
            dec = _rnn_decays(_tri_dot(lower, g))
            qd = (q * dec["eq_d"]).astype(BF16)
            kd = (k * dec["ek_d"]).astype(BF16)
            qo = (q * dec["eq_o"]).astype(BF16)
            ko = (k * dec["ek_o"]).astype(BF16)
            qe = (q * dec["eg"]).astype(BF16)
            kl = (k * dec["ekl"]).astype(BF16)
            vb = v.astype(BF16)

            o = o_ref[rows, :]
            dmix = dm_ref[rows, :]
            sg = _sigmoid(rg)
            n_parts = []
            for j in range(RNN_GROUP_HEADS):
                oj = _head(o, j)
                n_parts.append(oj * lax.rsqrt(jnp.mean(oj * oj, axis=-1, keepdims=True) + NORM_EPS))
            nrm = jnp.concatenate(n_parts, axis=1)
            d_on = dmix * (rg * sg)
            d_rg = dmix * (nrm * gain) * (sg * (1.0 + rg * (1.0 - sg)))
            dgain_ref[...] += jnp.sum(d_on * nrm, axis=0, keepdims=True)
            dn = d_on * gain

            dq_parts, dk_parts, dv_parts, dg_parts = [], [], [], []
            for j in range(RNN_GROUP_HEADS):
                oj, nj, dnj = _head(o, j), _head(nrm, j), _head(dn, j)
                rr = lax.rsqrt(jnp.mean(oj * oj, axis=-1, keepdims=True) + NORM_EPS)
                do = (rr * (dnj - nj * jnp.mean(dnj * nj, axis=-1, keepdims=True))).astype(BF16)
                st = st_in_ref[j, c]
                dst = dst_ref[j]
                stb, dstb = st.astype(BF16), dst.astype(BF16)
                qdj, kdj, qoj, koj = _head(qd, j), _head(kd, j), _head(qo, j), _head(ko, j)
                attn = jnp.where(diag, _dot_nt(qdj, kdj), jnp.where(off, _dot_nt(qoj, koj), 0.0))
                dattn = _dot_nt(do, _head(vb, j))
                da_d = jnp.where(diag, dattn, 0.0).astype(BF16)
                da_o = jnp.where(off, dattn, 0.0).astype(BF16)
                dv = _dot_tn(attn.astype(BF16), do) + _dot_nt(_head(kl, j), dstb)
                dq_inter = _dot(do, stb) * _head(dec["eg"], j)
                dq_d, dq_o = _dot(da_d, kdj), _dot(da_o, koj)
                dq = dq_inter + dq_d * _head(dec["eq_d"], j) + dq_o * _head(dec["eq_o"], j)
                dk_inter = _dot(_head(vb, j), dstb) * _head(dec["ekl"], j)
                dk_d, dk_o = _dot_tn(da_d, qdj), _dot_tn(da_o, qoj)
                dk = dk_inter + dk_d * _head(dec["ek_d"], j) + dk_o * _head(dec["ek_o"], j)
                kj, qj = _head(k, j), _head(q, j)
                e_last = _head(dec["e_last"], j)
                extra = (jnp.sum(kj * dk_inter, axis=0, keepdims=True)
                         + e_last * jnp.sum(st * dst, axis=0, keepdims=True))
                dg_cum = (qj * dq_inter - kj * dk_inter
                          + (qdj.astype(F32) * dq_d + qoj.astype(F32) * dq_o)
                          - (kdj.astype(F32) * dk_d + koj.astype(F32) * dk_o))
                dg_parts.append(jnp.where(last_row, dg_cum + extra, dg_cum))
                dst_ref[j] = dst * e_last + _dot_tn(do, _head(qe, j))
                dq_parts.append(dq)
                dk_parts.append(dk)
                dv_parts.append(dv)

            dq = jnp.concatenate(dq_parts, axis=1)
            dk = jnp.concatenate(dk_parts, axis=1)
            dg = _tri_dot(upper, jnp.concatenate(dg_parts, axis=1))
            df = dg / f - dk
            dlb_ref[...] += jnp.sum(df * (1.0 - sf), axis=0, keepdims=True)
            d_rf = df * (1.0 - lb) * (sf * (1.0 - sf))
            d_rq = dq * (sq * (1.0 + rq * (1.0 - sq)))
            dproj_ref[rows, 0:gw] = d_rq.astype(BF16)
            dproj_ref[rows, gw:2 * gw] = d_rf.astype(BF16)
            dproj_ref[rows, 2 * gw:3 * gw] = jnp.concatenate(dv_parts, axis=1).astype(BF16)
            dproj_ref[rows, 3 * gw:4 * gw] = d_rg.astype(BF16)
            return carry

        lax.fori_loop(0, nch, chunk, 0, unroll=True)

    rev = lambda i: ntb - 1 - i
    vec = pl.BlockSpec((1, gw), lambda h, i: (0, h))
    scratch = [pltpu.VMEM((RNN_GROUP_HEADS, RNN_HEAD_DIM, RNN_HEAD_DIM), F32)]
    out_shapes = [jax.ShapeDtypeStruct((t, d_total), BF16), jax.ShapeDtypeStruct((1, d), F32),
                  jax.ShapeDtypeStruct((1, d), F32)]
    operands = [proj, lb_logits, rnn_gain, o_pre, states, dmixed]
    sem = ("parallel", "arbitrary")
    if rider is not None:
        scratch += rider.scratch()
        out_shapes += list(rider.out_shapes)
        operands += list(rider.operands)
        sem = ("arbitrary", "arbitrary")
    return pl.pallas_call(
        body, name="rnn_bwd", grid=(n_groups, ntb),
        in_specs=[pl.BlockSpec((tb, 4 * gw), lambda h, i: (rev(i), h)),
                  pl.BlockSpec((2, gw), lambda h, i: (0, h)), vec,
                  pl.BlockSpec((tb, gw), lambda h, i: (rev(i), h)),
                  pl.BlockSpec((RNN_GROUP_HEADS, nch, RNN_HEAD_DIM, RNN_HEAD_DIM), lambda h, i: (h, rev(i), 0, 0)),
                  pl.BlockSpec((tb, gw), lambda h, i: (rev(i), d // gw + h))] + [HBM_SPEC] * n_rin,
        out_specs=[pl.BlockSpec((tb, 4 * gw), lambda h, i: (rev(i), h)), vec, vec] + [HBM_SPEC] * n_rout,
        out_shape=out_shapes, scratch_shapes=scratch,
        compiler_params=_params(dimension_semantics=sem),
    )(*operands)


def _local_grads(x, target, w_in_full, w_out, sinks, lb_logits, rnn_gain, pre_gain, post_gain, sc=None):
    t, d = x.shape
    comm = sc is not None
    lay = _layout(d)
    perm = lay["perm"]
    h = _prenorm_fwd(x, pre_gain)
    if comm:
        proj, w_in_full = _proj_gather_mm(h, w_in_full, perm, sc)
        mixed, attn_o, w_out = _attn_fwd(proj, sinks, lay, d, _gather_rider(w_out, 0.7, 0, 2))
        mixed, o_pre, states, w_out_full = _rnn_fwd(proj, lb_logits, rnn_gain, mixed, d,
                                                    _gather_rider(w_out, 0.7, 1, 2))
    else:
        (proj,) = _proj_mm(h, w_in_full, perm)
        w_out_full = w_out
        mixed, attn_o = _attn_fwd(proj, sinks, lay, d)
        mixed, o_pre, states = _rnn_fwd(proj, lb_logits, rnn_gain, mixed, d)
    y = _out_mm(mixed, w_out_full)
    dy, dz, g_post, loss = _post_loss(x, y, target, post_gain)
    gw_out = _gw_out_mm(mixed, dy)
    rider = None
    if comm:
        dmixed, recv_out = _dmixed_mm(dy, w_out_full, _pair_exchange_rider(gw_out, stacked=True))
        p_out = _pair_sum_out(gw_out, recv_out)
        rider = _chip_exchange_rider(p_out, lambda ref, chip: ref.at[chip])
    else:
        (dmixed,) = _dmixed_mm(dy, w_out_full)
    dproj, g_rnn, g_lb, *r_out = _rnn_bwd(proj, lb_logits, rnn_gain, o_pre, states, dmixed, lay["total"], d, rider)
    dproj, dkc, dkp, dvc, dvp, dsink = _attn_bwd(proj, sinks, attn_o, dmixed, dproj, lay, d)
    dproj = _kv_combine(dkc, dkp, dvc, dvp, dproj, lay)
    if comm:
        c = lax.axis_index("c")
        (g_other,) = _gw_in_mm("gw_in_mm_other", h, dproj, perm, 1 - c)
        g_mine, recv_in = _gw_in_mm("gw_in_mm_mine", h, dproj, perm, c, _pair_exchange_rider(g_other))
        p_in = _pair_sum_in(g_mine, recv_in, sc)
        rider = _chip_exchange_rider(
            p_in, lambda ref, chip: ref.at[:, pl.ds(pl.multiple_of(chip * sc, LANES), sc)])
    dh, *r_in = _dh_mm(dproj, w_in_full, perm, rider)
    grad_x, g_pre = _prenorm_bwd(x, dh, dz, pre_gain)
    heads_per_group = ATTN_GROUP_LANES // ATTN_HEAD_DIM
    g_sink = dsink.reshape(d // ATTN_GROUP_LANES, 8, LANES)[:, 0, :heads_per_group].reshape(1, -1)
    small = dict(sink=g_sink, lb=g_lb, rnn=g_rnn, pre=g_pre, post=g_post)
    if comm:
        return loss, grad_x, (p_in, r_in[0]), (p_out, r_out[0]), small
    gw_in = jnp.stack([_gw_in_mm("gw_in_mm_%d" % half, h, dproj, perm, half)[0] for half in range(2)])
    return loss, grad_x, gw_in, gw_out, small


def _mesh_pos():
    x, y, c = lax.axis_index("x"), lax.axis_index("y"), lax.axis_index("c")
    chips = [(1 - x, y), (x, 1 - y), (1 - x, 1 - y)]
    return x, y, c, chips


def _remote(src, dst, send_sem, recv_sem, device):
    return pltpu.make_async_remote_copy(src_ref=src, dst_ref=dst, send_sem=send_sem, recv_sem=recv_sem,
                                        device_id=device, device_id_type=MESH)


HBM_SPEC = pl.BlockSpec(memory_space=pl.ANY)


def _gather_rider(part, forward_at, section, n_sections):
    rows = part.shape[0] // N_CHIPS
    half_rows = rows // 2
    sec_rows = half_rows // n_sections

    def stages(ins, outs, send_sems, recv_sems):
        del ins
        full = outs[0]

        def piece(chip, half):
            start = chip * rows + half * half_rows + section * sec_rows
            return full.at[pl.ds(pl.multiple_of(start, 8), sec_rows), :]

        def sends():
            x, y, c, chips = _mesh_pos()
            mine = piece(2 * x + y, c)
            return [_remote(mine, mine, send_sems.at[j], recv_sems.at[j], (px, py, c))
                    for j, (px, py) in enumerate(chips)]

        def forwards(half_of):
            x, y, c, chips = _mesh_pos()
            out = []
            for j, (px, py) in enumerate(chips):
                block = piece(2 * px + py, half_of(c))
                out.append(_remote(block, block, send_sems.at[3 + j], recv_sems.at[3 + j], (x, y, 1 - c)))
            return out

        def start():
            for cp in sends():
                cp.start()

        def forward():
            x, y, c, chips = _mesh_pos()
            for j, (px, py) in enumerate(chips):
                landed = piece(2 * px + py, c)
                _remote(landed, landed, send_sems.at[j], recv_sems.at[j], (x, y, 1 - c)).wait_recv()
            for cp in forwards(lambda c: c):
                cp.start()

        def finish():
            for cp in forwards(lambda c: 1 - c):
                cp.wait_recv()
            for cp in sends() + forwards(lambda c: c):
                cp.wait_send()

        return [(0.0, start), (forward_at, forward), (1.0, finish)]

    return _Rider((part,), (jax.ShapeDtypeStruct(part.shape, BF16),), {0: 0}, 6, stages)


def _chip_exchange_rider(partial, piece):
    if partial.ndim == 3:
        recv_shape = (N_CHIPS - 1,) + partial.shape[1:]
    else:
        recv_shape = (N_CHIPS - 1, partial.shape[0], partial.shape[1] // N_CHIPS)

    def stages(ins, outs, send_sems, recv_sems):
        def copies():
            x, y, c, chips = _mesh_pos()
            return [_remote(piece(ins[0], 2 * px + py), outs[0].at[j], send_sems.at[j], recv_sems.at[j], (px, py, c))
                    for j, (px, py) in enumerate(chips)]

        def start():
            for cp in copies():
                cp.start()

        def finish():
            for cp in copies():
                cp.wait()

        return [(0.0, start), (1.0, finish)]

    return _Rider((partial,), (jax.ShapeDtypeStruct(recv_shape, BF16),), {}, N_CHIPS - 1, stages)


def _pair_exchange_rider(g, stacked=False):
    shape = g.shape[1:] if stacked else g.shape

    def stages(ins, outs, send_sems, recv_sems):
        def copy():
            x, y, c, _ = _mesh_pos()
            src = ins[0].at[1 - c] if stacked else ins[0]
            return _remote(src, outs[0], send_sems.at[0], recv_sems.at[0], (x, y, 1 - c))

        return [(0.0, lambda: copy().start()), (1.0, lambda: copy().wait())]

    return _Rider((g,), (jax.ShapeDtypeStruct(shape, g.dtype),), {}, 1, stages)


def _pair_sum_in(mine, recv, sc):
    hd, d_in = mine.shape
    tr = min(hd, 256)

    def body(a_ref, b_ref, o_ref):
        o_ref[...] = (a_ref[...].astype(F32) + b_ref[...].astype(F32)).astype(BF16)

    blk = pl.BlockSpec((tr, sc), lambda i, j: (i, j))
    return pl.pallas_call(
        body, name="pair_sum_in", grid=(hd // tr, d_in // sc), in_specs=[blk, blk], out_specs=blk,
        out_shape=jax.ShapeDtypeStruct((hd, d_in), BF16),
        compiler_params=_params(dimension_semantics=("parallel", "parallel")),
    )(mine, recv)


def _pair_sum_out(gw_out, recv):
    _, n_chips, hr, d = gw_out.shape
    tr = min(hr, 256)
    c = lax.axis_index("c")

    def body(c_ref, a_ref, b_ref, o_ref):
        del c_ref
        o_ref[...] = (a_ref[...].astype(F32) + b_ref[...].astype(F32)).astype(BF16)

    blk = pl.BlockSpec((None, tr, d), lambda k, i, cc: (k, i, 0))
    gs = pltpu.PrefetchScalarGridSpec(
        num_scalar_prefetch=1, grid=(n_chips, hr // tr),
        in_specs=[pl.BlockSpec((None, None, tr, d), lambda k, i, cc: (cc[0], k, i, 0)), blk], out_specs=blk)
    return pl.pallas_call(
        body, name="pair_sum_out", grid_spec=gs, out_shape=jax.ShapeDtypeStruct((n_chips, hr, d), BF16),
        compiler_params=_params(dimension_semantics=("parallel", "parallel")),
    )(jnp.reshape(c, (1,)).astype(jnp.int32), gw_out, recv)


def _place():
    return jnp.stack([2 * lax.axis_index("x") + lax.axis_index("y"), lax.axis_index("c")]).astype(jnp.int32)


def _chip_sum_in(p_in, r_in, sc):
    hd = p_in.shape[0]
    tr = min(hd, 256)
    nblk = hd // tr

    def body(pos_ref, p_ref, r_ref, o_ref):
        del pos_ref
        acc = p_ref[...].astype(F32)
        for j in range(3):
            acc = acc + r_ref[j].astype(F32)
        o_ref[...] = acc

    gs = pltpu.PrefetchScalarGridSpec(
        num_scalar_prefetch=1, grid=(nblk,),
        in_specs=[pl.BlockSpec((tr, sc), lambda i, pos: (i, pos[0])), pl.BlockSpec((3, tr, sc), lambda i, pos: (0, i, 0))],
        out_specs=pl.BlockSpec((tr, sc), lambda i, pos: (pos[1] * nblk + i, 0)))
    return pl.pallas_call(
        body, name="chip_sum_in", grid_spec=gs, out_shape=jax.ShapeDtypeStruct((2 * hd, sc), F32),
        compiler_params=_params(dimension_semantics=("parallel",)),
    )(_place(), p_in, r_in)


def _chip_sum_out(p_out, r_out):
    _, hr, d = p_out.shape
    tr = min(hr, 256)
    nblk = hr // tr

    def body(pos_ref, p_ref, r_ref, o_ref):
        del pos_ref
        acc = p_ref[...].astype(F32)
        for j in range(3):
            acc = acc + r_ref[j].astype(F32)
        o_ref[...] = acc

    gs = pltpu.PrefetchScalarGridSpec(
        num_scalar_prefetch=1, grid=(nblk,),
        in_specs=[pl.BlockSpec((None, tr, d), lambda i, pos: (pos[0], i, 0)), pl.BlockSpec((3, tr, d), lambda i, pos: (0, i, 0))],
        out_specs=pl.BlockSpec((tr, d), lambda i, pos: (pos[1] * nblk + i, 0)))
    return pl.pallas_call(
        body, name="chip_sum_out", grid_spec=gs, out_shape=jax.ShapeDtypeStruct((2 * hr, d), F32),
        compiler_params=_params(dimension_semantics=("parallel",)),
    )(_place(), p_out, r_out)


def _adamw_math(w, g, m, v):
    m_new = ADAM_B1 * m + (1.0 - ADAM_B1) * g
    v_new = ADAM_B2 * v + (1.0 - ADAM_B2) * (g * g)
    m_hat = m_new / (1.0 - ADAM_B1 ** ADAM_STEP)
    v_hat = v_new / (1.0 - ADAM_B2 ** ADAM_STEP)
    delta = -ADAM_LR * (m_hat / (jnp.sqrt(v_hat) + ADAM_EPS) + ADAM_WD * w)
    return delta, m_new, v_new


def _share_halves(g_in, g_out):
    hd = g_in.shape[0] // 2
    hr = g_out.shape[0] // 2

    def body(gi_in, go_in, gi_ref, go_ref, send_sems, recv_sems):
        del gi_in, go_in
        x, y, c, _ = _mesh_pos()
        sibling = (x, y, 1 - c)
        mine_i = gi_ref.at[pl.ds(pl.multiple_of(c * hd, 8), hd), :]
        mine_o = go_ref.at[pl.ds(pl.multiple_of(c * hr, 8), hr), :]
        a = _remote(mine_i, mine_i, send_sems.at[0], recv_sems.at[0], sibling)
        b = _remote(mine_o, mine_o, send_sems.at[1], recv_sems.at[1], sibling)
        a.start()
        b.start()
        a.wait_send()
        b.wait_send()
        theirs_i = gi_ref.at[pl.ds(pl.multiple_of((1 - c) * hd, 8), hd), :]
        theirs_o = go_ref.at[pl.ds(pl.multiple_of((1 - c) * hr, 8), hr), :]
        _remote(theirs_i, theirs_i, send_sems.at[0], recv_sems.at[0], sibling).wait_recv()
        _remote(theirs_o, theirs_o, send_sems.at[1], recv_sems.at[1], sibling).wait_recv()

    return pl.pallas_call(
        body, name="share_halves",
        in_specs=[HBM_SPEC, HBM_SPEC], out_specs=[HBM_SPEC, HBM_SPEC],
        out_shape=[jax.ShapeDtypeStruct(g_in.shape, F32), jax.ShapeDtypeStruct(g_out.shape, F32)],
        input_output_aliases={0: 0, 1: 1},
        scratch_shapes=[pltpu.SemaphoreType.DMA((2,)), pltpu.SemaphoreType.DMA((2,))],
    )(g_in, g_out)


def _adamw(w, g, m, v, name):
    rows, cols = w.shape
    streams = 8
    fit = (VMEM_LIMIT_BYTES // 2) // (streams * 2 * cols * 4)
    tr = min(rows, 1 << (fit.bit_length() - 1))

    def body(w_ref, g_ref, m_ref, v_ref, d_ref, mo_ref, vo_ref, go_ref):
        gv = g_ref[...]
        delta, m_new, v_new = _adamw_math(w_ref[...], gv, m_ref[...], v_ref[...])
        d_ref[...] = delta
        mo_ref[...] = m_new
        vo_ref[...] = v_new
        go_ref[...] = gv

    spec = pl.BlockSpec((tr, cols), lambda i: (i, 0))
    shape = jax.ShapeDtypeStruct((rows, cols), F32)
    return pl.pallas_call(
        body, name=name, grid=(rows // tr,), in_specs=[spec] * 4, out_specs=[spec] * 4,
        out_shape=[shape] * 4, compiler_params=_params(dimension_semantics=("parallel",)),
    )(w, g, m, v)


SMALL_ROWS = 8


def _small_allreduce_adamw(part, w_pack, m_pack, v_pack):
    d = part.shape[1]

    def body(part_ref, w_ref, m_ref, v_ref, g_ref, d_ref, mo_ref, vo_ref, buf_ref, send_sems, recv_sems):
        x, y, c, _ = _mesh_pos()
        me = 4 * x + 2 * y + c
        buf_ref[0] = part_ref[...]
        copies = []
        for r in range(1, 8):
            rx, ry, rc = (r >> 2) & 1, (r >> 1) & 1, r & 1
            peer = (x ^ rx, y ^ ry, c ^ rc)
            copies.append(_remote(buf_ref.at[0], buf_ref.at[r], send_sems.at[r - 1], recv_sems.at[r - 1], peer))
        for cp in copies:
            cp.start()
        for cp in copies:
            cp.wait()
        total = buf_ref[me]
        for s in range(1, 8):
            total = total + buf_ref[s ^ me]
        w = w_ref[...]
        row = lax.broadcasted_iota(jnp.int32, (SMALL_ROWS, d), 0)
        l0, l1 = w[3:4], w[4:5]
        mx = jnp.maximum(l0, l1)
        e0, e1 = jnp.exp(l0 - mx), jnp.exp(l1 - mx)
        lb = e0 / (e0 + e1)
        g_l0 = total[3:4] * lb * (1.0 - lb)
        grads = jnp.where(row == 3, g_l0, jnp.where(row == 4, -g_l0, total))
        g_ref[...] = grads
        delta, m_new, v_new = _adamw_math(w, grads, m_ref[...], v_ref[...])
        d_ref[...] = delta
        mo_ref[...] = m_new
        vo_ref[...] = v_new

    vm = pl.BlockSpec(memory_space=pltpu.VMEM)
    shape = jax.ShapeDtypeStruct((SMALL_ROWS, d), F32)
    return pl.pallas_call(
        body, name="small_allreduce_adamw",
        in_specs=[vm] * 4, out_specs=[vm] * 4, out_shape=[shape] * 4,
        scratch_shapes=[pltpu.VMEM((8, SMALL_ROWS, d), F32), pltpu.SemaphoreType.DMA((7,)), pltpu.SemaphoreType.DMA((7,))],
    )(part, w_pack, m_pack, v_pack)


def _pack_small(d, pre, post, rnn, lb, sink, extra=None):
    rows = [pre, post, rnn, lb[0:1], lb[1:2],
            jnp.pad(sink, ((0, 0), (0, d - sink.shape[1]))),
            jnp.zeros((1, d), F32) if extra is None else extra,
            jnp.zeros((1, d), F32)]
    return jnp.concatenate(rows, axis=0)


def _unpack_small(p, n_sink):
    return dict(pre=p[0:1], post=p[1:2], rnn=p[2:3], lb=p[3:5], sink=p[5:6, :n_sink])


def kernel(x, w_in, attn_sinks, lb_logits, rnn_norm, w_out, pre_norm, post_norm, loss_target, m_w_in, m_attn_sinks, m_lb_logits, m_rnn_norm, m_w_out, m_pre_norm, m_post_norm, v_w_in, v_attn_sinks, v_lb_logits, v_rnn_norm, v_w_out, v_pre_norm, v_post_norm):
    t, d = x.shape[1], x.shape[2]
    sc = w_in.shape[2]
    n_sink = attn_sinks.shape[1]
    w_in2, w_out2 = w_in[0], w_out[0]

    w_in_part = _cast_into_gathered(w_in2, "cast_w_in", 1)
    w_out_part = _cast_into_gathered(w_out2, "cast_w_out", 0)
    loss_part, grad_x, (p_in, r_in), (p_out, r_out), small = _local_grads(
        x[0], loss_target[0], w_in_part, w_out_part, attn_sinks, lb_logits, rnn_norm, pre_norm, post_norm, sc)
    g_w_in, g_w_out = _share_halves(_chip_sum_in(p_in, r_in, sc), _chip_sum_out(p_out, r_out))
    d_w_in, nm_w_in, nv_w_in, g_w_in = _adamw(w_in2, g_w_in, m_w_in[0], v_w_in[0], "adamw_w_in")
    d_w_out, nm_w_out, nv_w_out, g_w_out = _adamw(w_out2, g_w_out, m_w_out[0], v_w_out[0], "adamw_w_out")

    lb_part = jnp.concatenate([small["lb"], jnp.zeros_like(small["lb"])], axis=0)
    loss_row = jnp.pad(loss_part[:, :1], ((0, 0), (0, d - 1)))
    part = _pack_small(d, small["pre"], small["post"], small["rnn"], lb_part, small["sink"], loss_row)
    w_pack = _pack_small(d, pre_norm, post_norm, rnn_norm, lb_logits, attn_sinks)
    m_pack = _pack_small(d, m_pre_norm, m_post_norm, m_rnn_norm, m_lb_logits, m_attn_sinks)
    v_pack = _pack_small(d, v_pre_norm, v_post_norm, v_rnn_norm, v_lb_logits, v_attn_sinks)
    g_pack, d_pack, nm_pack, nv_pack = _small_allreduce_adamw(part, w_pack, m_pack, v_pack)
    loss = g_pack[6, 0]
    g, dl, nm, nv = (_unpack_small(p, n_sink) for p in (g_pack, d_pack, nm_pack, nv_pack))

    def ordered(w_in_leaf, w_out_leaf, s):
        return (w_in_leaf[None], s["sink"], s["lb"], s["rnn"], w_out_leaf[None], s["pre"], s["post"])

    return (loss, grad_x[None],
            *ordered(g_w_in, g_w_out, g), *ordered(d_w_in, d_w_out, dl),
            *ordered(nm_w_in, nm_w_out, nm), *ordered(nv_w_in, nv_w_out, nv))
```

```python
import numpy as np
import jax
import jax.numpy as jnp
from jax import lax
from jax.experimental import pallas as pl
from jax.experimental.pallas import tpu as pltpu

F32 = jnp.float32
BF16 = jnp.bfloat16
MESH = pl.DeviceIdType.MESH

NORM_EPS = 1e-6
ATTN_HEAD_DIM = 64
GQA_GROUP = 8
WINDOW = 128
ATTN_STEP_BLOCKS = 8
ATTN_STEP_ROWS = ATTN_STEP_BLOCKS * WINDOW
RNN_HEAD_DIM = 128
CHUNK = 64
HALF_CHUNK = CHUNK // 2
ATTN_SCALE = ATTN_HEAD_DIM ** -0.5

ADAM_LR = 0.001
ADAM_B1 = 0.9
ADAM_B2 = 0.999
ADAM_EPS = 1e-08
ADAM_WD = 0.01
ADAM_STEP = 10

LANES = 128
COL_TILE = 512
RNN_GROUP_HEADS = 8
RNN_STEP_ROWS = 512
ATTN_GROUP_LANES = 1024
N_CHIPS = 4
VMEM_LIMIT_BYTES = 56 * 1024 * 1024
NEG_BIG = -1e30


def _params(**kw):
    return pltpu.CompilerParams(vmem_limit_bytes=VMEM_LIMIT_BYTES, **kw)


def _sigmoid(x):
    return 1.0 / (1.0 + jnp.exp(-x))


def _dot(a, b):
    return jnp.dot(a, b, preferred_element_type=F32)


def _dot_nt(a, b):
    return lax.dot_general(a, b, (((1,), (1,)), ((), ())), preferred_element_type=F32)


def _dot_tn(a, b):
    return lax.dot_general(a, b, (((0,), (0,)), ((), ())), preferred_element_type=F32)


def _tri_dot(tri_bf16, x):
    hi = x.astype(BF16)
    lo = (x - hi.astype(F32)).astype(BF16)
    return _dot(tri_bf16, hi) + _dot(tri_bf16, lo)


def _layout(d_model):
    d = d_model
    dkv = d // GQA_GROUP
    orig = dict(aq=0, ak=d, av=d + dkv, ag=d + 2 * dkv)
    base = d + 2 * dkv + d
    orig.update(rq=base, rf=base + d, ri=base + 2 * d, rg=base + 3 * d)
    group_w = RNN_GROUP_HEADS * RNN_HEAD_DIM
    cols = []
    for hg in range(d // group_w):
        for seg in ("rq", "rf", "ri", "rg"):
            cols.append((orig[seg] + hg * group_w, group_w))
    for m in range(d // ATTN_GROUP_LANES):
        for seg in ("aq", "ag"):
            cols.append((orig[seg] + m * ATTN_GROUP_LANES, ATTN_GROUP_LANES))
    cols.append((orig["ak"], dkv))
    cols.append((orig["av"], dkv))
    units = []
    for start, width in cols:
        assert start % LANES == 0 and width % LANES == 0
        units += [start + u for u in range(0, width, LANES)]
    per = COL_TILE // LANES
    assert len(units) % per == 0
    tiles = []
    for t in range(len(units) // per):
        run = units[t * per:(t + 1) * per]
        assert run[0] % COL_TILE == 0 and all(run[i] == run[0] + i * LANES for i in range(per))
        tiles.append(run[0] // COL_TILE)
    return dict(a_off=4 * d, k_off=6 * d, v_off=6 * d + dkv, total=6 * d + 2 * dkv,
                perm=np.asarray(tiles, np.int32))


def _chip_index():
    return jnp.reshape(2 * lax.axis_index("x") + lax.axis_index("y"), (1,)).astype(jnp.int32)


def _cast_into_gathered(a, name, axis):
    rows, cols = a.shape
    tr = min(rows, 512)
    nblk = rows // tr

    def body(me_ref, a_ref, o_ref):
        del me_ref
        o_ref[...] = a_ref[...].astype(BF16)

    if axis == 1:
        out_spec = pl.BlockSpec((tr, cols), lambda i, me: (i, me[0]))
        shape = (rows, N_CHIPS * cols)
    else:
        out_spec = pl.BlockSpec((tr, cols), lambda i, me: (me[0] * nblk + i, 0))
        shape = (N_CHIPS * rows, cols)
    gs = pltpu.PrefetchScalarGridSpec(num_scalar_prefetch=1, grid=(nblk,),
                                      in_specs=[pl.BlockSpec((tr, cols), lambda i, me: (i, 0))], out_specs=out_spec)
    return pl.pallas_call(
        body, name=name, grid_spec=gs, out_shape=jax.ShapeDtypeStruct(shape, BF16),
        compiler_params=_params(dimension_semantics=("parallel",)),
    )(_chip_index(), a)


def _prenorm_fwd(x, gain):
    t, d = x.shape
    tm = min(t, 256)

    def body(x_ref, g_ref, h_ref):
        xv = x_ref[...]
        r = lax.rsqrt(jnp.mean(xv * xv, axis=-1, keepdims=True) + NORM_EPS)
        h_ref[...] = ((xv * r) * g_ref[...]).astype(BF16)

    return pl.pallas_call(
        body, name="prenorm_fwd", grid=(t // tm,),
        in_specs=[pl.BlockSpec((tm, d), lambda i: (i, 0)), pl.BlockSpec((1, d), lambda i: (0, 0))],
        out_specs=pl.BlockSpec((tm, d), lambda i: (i, 0)),
        out_shape=jax.ShapeDtypeStruct((t, d), BF16),
        compiler_params=_params(dimension_semantics=("parallel",)),
    )(x, gain)


def _post_loss(x, y, target, gain):
    t, d = x.shape
    tm = min(t, 256)
    inv_d = 1.0 / d

    def body(x_ref, y_ref, t_ref, g_ref, dy_ref, dz_ref, gp_ref, loss_ref):
        i = pl.program_id(0)
        yv = y_ref[...]
        gain_v = g_ref[...]
        r = lax.rsqrt(jnp.mean(yv * yv, axis=-1, keepdims=True) + NORM_EPS)
        n = yv * r
        e = (x_ref[...] + n * gain_v) - t_ref[...]
        dz = e * inv_d
        dn = dz * gain_v
        dy = r * (dn - n * jnp.mean(dn * n, axis=-1, keepdims=True))
        dy_ref[...] = dy.astype(BF16)
        dz_ref[...] = dz

        @pl.when(i == 0)
        def _():
            gp_ref[...] = jnp.zeros_like(gp_ref)
            loss_ref[...] = jnp.zeros_like(loss_ref)

        gp_ref[...] += jnp.sum(dz * n, axis=0, keepdims=True)
        row = jnp.sum(e * e, axis=-1, keepdims=True)
        loss_ref[...] += jnp.full(loss_ref.shape, 0.5 * inv_d * jnp.sum(row), F32)

    row_spec = pl.BlockSpec((tm, d), lambda i: (i, 0))
    vec_spec = pl.BlockSpec((1, d), lambda i: (0, 0))
    return pl.pallas_call(
        body, name="post_loss", grid=(t // tm,),
        in_specs=[row_spec, row_spec, row_spec, vec_spec],
        out_specs=[row_spec, row_spec, vec_spec, pl.BlockSpec((1, LANES), lambda i: (0, 0))],
        out_shape=[jax.ShapeDtypeStruct((t, d), BF16), jax.ShapeDtypeStruct((t, d), F32),
                   jax.ShapeDtypeStruct((1, d), F32), jax.ShapeDtypeStruct((1, LANES), F32)],
        compiler_params=_params(dimension_semantics=("arbitrary",)),
    )(x, y, target, gain)


def _prenorm_bwd(x, dh, dz, gain, rider=None):
    t, d = x.shape
    tm = min(t, 256)
    n_rin = 0 if rider is None else len(rider.operands)
    n_rout = 0 if rider is None else len(rider.out_shapes)

    def body(*refs):
        x_ref, dh_ref, dz_ref, g_ref = refs[:4]
        gx_ref, gp_ref = refs[4 + n_rin:6 + n_rin]
        i = pl.program_id(0)
        if rider is not None:
            rider.emit(i, t // tm, refs[4:4 + n_rin], refs[6 + n_rin:6 + n_rin + n_rout], refs[-2], refs[-1])
        xv = x_ref[...]
        r = lax.rsqrt(jnp.mean(xv * xv, axis=-1, keepdims=True) + NORM_EPS)
        n = xv * r
        dhv = dh_ref[...]
        dn = dhv * g_ref[...]
        gx_ref[...] = dz_ref[...] + r * (dn - n * jnp.mean(dn * n, axis=-1, keepdims=True))

        @pl.when(i == 0)
        def _():
            gp_ref[...] = jnp.zeros_like(gp_ref)

        gp_ref[...] += jnp.sum(dhv * n, axis=0, keepdims=True)

    row_spec = pl.BlockSpec((tm, d), lambda i: (i, 0))
    vec_spec = pl.BlockSpec((1, d), lambda i: (0, 0))
    out_shapes = [jax.ShapeDtypeStruct((t, d), F32), jax.ShapeDtypeStruct((1, d), F32)]
    operands = [x, dh, dz, gain]
    scratch, aliases = [], {}
    if rider is not None:
        scratch = rider.scratch()
        out_shapes += list(rider.out_shapes)
        operands += list(rider.operands)
        aliases = {4 + i: 2 + o for i, o in rider.aliases.items()}
    return pl.pallas_call(
        body, name="prenorm_bwd", grid=(t // tm,),
        in_specs=[row_spec, row_spec, row_spec, vec_spec] + [HBM_SPEC] * n_rin,
        out_specs=[row_spec, vec_spec] + [HBM_SPEC] * n_rout,
        out_shape=out_shapes, scratch_shapes=scratch, input_output_aliases=aliases,
        compiler_params=_params(dimension_semantics=("arbitrary",)),
    )(*operands)


class _Rider:
    def __init__(self, operands, out_shapes, aliases, n_sems, stages):
        self.operands = tuple(operands)
        self.out_shapes = tuple(out_shapes)
        self.aliases = dict(aliases)
        self.n_sems = n_sems
        self.stages = stages

    def scratch(self):
        return [pltpu.SemaphoreType.DMA((self.n_sems,)), pltpu.SemaphoreType.DMA((self.n_sems,))]

    def emit(self, step, n_steps, in_refs, out_refs, send_sems, recv_sems):
        for frac, fn in self.stages(in_refs, out_refs, send_sems, recv_sems):
            at = min(n_steps - 1, int(frac * (n_steps - 1) + 0.5))
            pl.when(step == at)(fn)


def _matmul(name, a, b, *, out_shape, grid, a_spec, b_spec, o_spec, nt=False, ta=False, perm=None, rider=None):
    nk = grid[2]
    n_steps = grid[0] * grid[1] * grid[2]
    tm, tn = [s for s in o_spec.block_shape if s is not None][-2:]
    acc_in_out = out_shape.dtype == F32
    n_pre = 0 if perm is None else 1
    n_rin = 0 if rider is None else len(rider.operands)
    n_rout = 0 if rider is None else len(rider.out_shapes)
    use_acc = not (nk == 1 or acc_in_out)

    def body(*refs):
        refs = refs[n_pre:]
        a_ref, b_ref = refs[:2]
        rin = refs[2:2 + n_rin]
        o_ref = refs[2 + n_rin]
        rout = refs[3 + n_rin:3 + n_rin + n_rout]
        scratch_refs = refs[3 + n_rin + n_rout:]
        if rider is not None:
            step = (pl.program_id(0) * grid[1] + pl.program_id(1)) * grid[2] + pl.program_id(2)
            rider.emit(step, n_steps, rin, rout, scratch_refs[-2], scratch_refs[-1])
        def product():
            if ta:
                return _dot_tn(a_ref[...], b_ref[...])
            return _dot_nt(a_ref[...], b_ref[...]) if nt else _dot(a_ref[...], b_ref[...])

        if nk == 1:
            o_ref[...] = product().astype(o_ref.dtype)
            return
        acc_ref = o_ref if acc_in_out else scratch_refs[0]
        k = pl.program_id(2)

        @pl.when(k == 0)
        def _():
            acc_ref[...] = jnp.zeros_like(acc_ref)

        acc_ref[...] += product()

        if not acc_in_out:
            @pl.when(k == nk - 1)
            def _():
                o_ref[...] = acc_ref[...].astype(o_ref.dtype)

    scratch = [pltpu.VMEM((tm, tn), F32)] if use_acc else []
    in_specs = [a_spec, b_spec] + [HBM_SPEC] * n_rin
    out_specs = [o_spec] + [HBM_SPEC] * n_rout
    out_shapes = [out_shape]
    operands = [a, b]
    aliases = {}
    sem = ("parallel", "parallel", "arbitrary")
    if rider is not None:
        scratch += rider.scratch()
        out_shapes += list(rider.out_shapes)
        operands += list(rider.operands)
        aliases = {n_pre + 2 + i: 1 + o for i, o in rider.aliases.items()}
        sem = ("arbitrary", "arbitrary", "arbitrary")
    cp = _params(dimension_semantics=sem)
    if perm is None:
        return pl.pallas_call(body, name=name, grid=grid, in_specs=in_specs, out_specs=out_specs,
                              out_shape=out_shapes, scratch_shapes=scratch, input_output_aliases=aliases,
                              compiler_params=cp)(*operands)
    gs = pltpu.PrefetchScalarGridSpec(num_scalar_prefetch=1, grid=grid, in_specs=in_specs,
                                      out_specs=out_specs, scratch_shapes=scratch)
    return pl.pallas_call(body, name=name, grid_spec=gs, out_shape=out_shapes, input_output_aliases=aliases,
                          compiler_params=cp)(jnp.asarray(perm), *operands)


def _proj_mm(h, w_full, perm, rider=None):
    t, d = h.shape
    n_tiles = len(perm)
    tm = min(t, 1024)
    return _matmul(
        "proj_mm", h, w_full, perm=perm, rider=rider, grid=(t // tm, n_tiles, 1),
        out_shape=jax.ShapeDtypeStruct((t, n_tiles * COL_TILE), F32),
        a_spec=pl.BlockSpec((tm, d), lambda i, j, k, p: (i, 0)),
        b_spec=pl.BlockSpec((d, COL_TILE), lambda i, j, k, p: (0, p[j])),
        o_spec=pl.BlockSpec((tm, COL_TILE), lambda i, j, k, p: (i, j)))


def _proj_gather_mm(h, wi_part, perm, sc):
    t, d = h.shape
    n_tiles = len(perm)
    tm = min(t, 1024)
    n_i = t // tm
    hd = d // 2
    nf = sc // COL_TILE
    rem = sc - nf * COL_TILE
    assert 2 * rem == COL_TILE and n_tiles == N_CHIPS * nf + 2
    n_kinds = nf + 1
    last = nf
    ahead = min(6, nf)
    slots = []
    for q in range(nf):
        slots += [(0, q), (1, q)] + ([(2, q - 1)] if q else [])
    slots.append((2, nf - 1))
    rem_at = nf + len(slots)

    def first_full(chip):
        return (chip * sc + (rem if chip % 2 else 0)) // COL_TILE

    inverse = np.argsort(perm)
    table = np.zeros((N_CHIPS, 2, n_tiles), np.int32)
    for chip in range(N_CHIPS):
        seq = list(range(first_full(chip), first_full(chip) + nf))
        seq += [first_full((chip ^ 2, chip ^ 1, chip ^ 3)[j]) + q for j, q in slots]
        seq += [first_full(chip - chip % 2) + nf, first_full((chip ^ 2) - chip % 2) + nf]
        assert sorted(seq) == list(range(n_tiles)), seq
        table[chip, 0] = inverse[seq]
        table[chip, 1] = seq
    me_chip = 2 * lax.axis_index("x") + lax.axis_index("y")
    tab = lax.dynamic_index_in_dim(jnp.asarray(table), me_chip, 0, keepdims=False)

    def body(tab_ref, h_hbm, wi_in, proj_ref, full, hbuf, bbuf, local_sems, send_sems, recv_sems):
        del wi_in
        jj = pl.program_id(0)
        i = pl.program_id(1)
        x, y, c, chips = _mesh_pos()
        sibling = (x, y, 1 - c)

        def chip_of(j):
            return 2 * chips[j][0] + chips[j][1]

        def piece(chip, half, kind, part=None):
            odd = chip % 2
            if kind == last:
                start, width = chip * sc + (1 - odd) * (nf * COL_TILE), rem
            else:
                start, width = chip * sc + odd * rem + kind * COL_TILE, COL_TILE
            if part is not None:
                width //= 2
                start = start + part * width
            return full.at[pl.ds(half * hd, hd), pl.ds(pl.multiple_of(start, LANES), width)]

        def ici(j, kind):
            mine = piece(2 * x + y, c, kind)
            k = j * n_kinds + kind
            return _remote(mine, mine, send_sems.at[k], recv_sems.at[k], (chips[j][0], chips[j][1], c))

        def landed(j, kind):
            blk = piece(chip_of(j), c, kind)
            k = j * n_kinds + kind
            return _remote(blk, blk, send_sems.at[k], recv_sems.at[k], sibling)

        def hop(p, kind):
            blk = piece(chip_of(p), c, kind, part=p)
            k = (2 + p) * n_kinds + kind
            return _remote(blk, blk, send_sems.at[k], recv_sems.at[k], (chips[1 - p][0], chips[1 - p][1], c))

        def hopped(p, kind):
            blk = piece(chip_of(2), c, kind, part=p)
            k = (2 + p) * n_kinds + kind
            return _remote(blk, blk, send_sems.at[k], recv_sems.at[k], sibling)

        def passed(j, kind, half):
            blk = piece(chip_of(j), half, kind)
            k = (4 + j) * n_kinds + kind
            return _remote(blk, blk, send_sems.at[k], recv_sems.at[k], sibling)

        def fetch(pos, slot):
            col = pl.multiple_of(tab_ref[1, pos] * COL_TILE, LANES)
            return pltpu.make_async_copy(full.at[:, pl.ds(col, COL_TILE)], bbuf.at[slot], local_sems.at[slot])

        def load_h():
            return pltpu.make_async_copy(h_hbm, hbuf, local_sems.at[2])

        def send(kind):
            for j in range(2):
                ici(j, kind).start()

        def relay(j, kind):
            if j == 2:
                hopped(0, kind).wait_recv()
                hopped(1, kind).wait_recv()
            else:
                landed(j, kind).wait_recv()
                hop(j, kind).start()
            passed(j, kind, c).start()
            if j == 0 and kind + ahead < n_kinds:
                send(kind + ahead)

        events = {}
        for s, (j, kind) in enumerate(slots):
            events.setdefault(nf + s - 2, []).append(lambda j=j, kind=kind: relay(j, kind))
            events.setdefault(nf + s - 1, []).append(lambda j=j, kind=kind: passed(j, kind, 1 - c).wait_recv())
        for j in range(3):
            events.setdefault(rem_at - 4 + j, []).append(lambda j=j: relay(j, last))
            events.setdefault(rem_at - 1, []).append(lambda j=j: passed(j, last, 1 - c).wait_recv())

        @pl.when(i == 0)
        def _():
            @pl.when(jj == 0)
            def _():
                load_h().start()
                for kind in range(ahead):
                    send(kind)
                fetch(0, 0).start()
                load_h().wait()

            for pos in sorted(events):
                def run(pos=pos):
                    for fn in events[pos]:
                        fn()
                pl.when(jj == pos)(run)

            @pl.when(jj + 1 < n_tiles)
            def _():
                fetch(jj + 1, (jj + 1) % 2).start()

            fetch(jj, jj % 2).wait()

            @pl.when(jj == n_tiles - 1)
            def _():
                for kind in range(n_kinds):
                    for j in range(2):
                        ici(j, kind).wait_send()
                        hop(j, kind).wait_send()
                    for j in range(3):
                        passed(j, kind, c).wait_send()

        rows = pl.ds(pl.multiple_of(i * tm, tm), tm)
        proj_ref[...] = _dot(hbuf[rows, :], bbuf[jj % 2])

    gs = pltpu.PrefetchScalarGridSpec(
        num_scalar_prefetch=1, grid=(n_tiles, n_i),
        in_specs=[HBM_SPEC, HBM_SPEC],
        out_specs=[pl.BlockSpec((tm, COL_TILE), lambda jj, i, tb: (i, tb[0, jj])), HBM_SPEC],
        scratch_shapes=[pltpu.VMEM((t, d), BF16), pltpu.VMEM((2, d, COL_TILE), BF16), pltpu.SemaphoreType.DMA((3,)),
                        pltpu.SemaphoreType.DMA((7 * n_kinds,)), pltpu.SemaphoreType.DMA((7 * n_kinds,))])
    return pl.pallas_call(
        body, name="proj_gather_mm", grid_spec=gs,
        out_shape=[jax.ShapeDtypeStruct((t, n_tiles * COL_TILE), F32), jax.ShapeDtypeStruct(wi_part.shape, BF16)],
        input_output_aliases={2: 1},
        compiler_params=_params(dimension_semantics=("arbitrary", "arbitrary")),
    )(tab, h, wi_part)


def _gw_in_mm(name, h, dproj, perm, half, rider=None):
    t, d = h.shape
    n_tiles = len(perm)
    hd = d // 2
    tm = min(hd, 1024)
    per_half = hd // tm
    table = jnp.concatenate([jnp.asarray(perm), jnp.reshape(half, (1,)).astype(jnp.int32)])
    return _matmul(
        name, h, dproj, perm=table, rider=rider, ta=True, grid=(per_half, n_tiles, 1),
        out_shape=jax.ShapeDtypeStruct((hd, n_tiles * COL_TILE), BF16),
        a_spec=pl.BlockSpec((t, tm), lambda i, j, k, p: (0, p[n_tiles] * per_half + i)),
        b_spec=pl.BlockSpec((t, COL_TILE), lambda i, j, k, p: (0, j)),
        o_spec=pl.BlockSpec((tm, COL_TILE), lambda i, j, k, p: (i, p[j])))


def _dh_mm(dproj, w_full, perm, rider=None):
    t = dproj.shape[0]
    d = w_full.shape[0]
    n_tiles = len(perm)
    tm = min(t, 2048)
    tn = min(d, 2048)
    return _matmul(
        "dh_mm", dproj, w_full, perm=perm, rider=rider, nt=True, grid=(t // tm, d // tn, n_tiles),
        out_shape=jax.ShapeDtypeStruct((t, d), F32),
        a_spec=pl.BlockSpec((tm, COL_TILE), lambda i, j, k, p: (i, k)),
        b_spec=pl.BlockSpec((tn, COL_TILE), lambda i, j, k, p: (j, p[k])),
        o_spec=pl.BlockSpec((tm, tn), lambda i, j, k, p: (i, j)))


def _out_mm(mixed, w_out_full):
    t, dm = mixed.shape
    d = w_out_full.shape[1]
    tm = min(t, 1024)
    tn = min(d, 512)
    tk = min(dm, 4096)
    return _matmul(
        "out_mm", mixed, w_out_full, grid=(t // tm, d // tn, dm // tk),
        out_shape=jax.ShapeDtypeStruct((t, d), F32),
        a_spec=pl.BlockSpec((tm, tk), lambda i, j, k: (i, k)),
        b_spec=pl.BlockSpec((tk, tn), lambda i, j, k: (k, j)),
        o_spec=pl.BlockSpec((tm, tn), lambda i, j, k: (i, j)))[0]


def _dmixed_mm(dy, w_out_full, rider=None):
    t, d = dy.shape
    dm = w_out_full.shape[0]
    tm = min(t, 1024)
    tn = min(dm, 1024)
    return _matmul(
        "dmixed_mm", dy, w_out_full, nt=True, rider=rider, grid=(t // tm, dm // tn, 1),
        out_shape=jax.ShapeDtypeStruct((t, dm), F32),
        a_spec=pl.BlockSpec((tm, d), lambda i, j, k: (i, 0)),
        b_spec=pl.BlockSpec((tn, d), lambda i, j, k: (j, 0)),
        o_spec=pl.BlockSpec((tm, tn), lambda i, j, k: (i, j)))


def _gw_out_mm(mixed, dy):
    t, dm = mixed.shape
    d = dy.shape[1]
    hr = dm // (2 * N_CHIPS)
    tn = min(d, 1024)
    return _matmul(
        "gw_out_mm", mixed, dy, ta=True, grid=(dm // hr, d // tn, 1),
        out_shape=jax.ShapeDtypeStruct((2, N_CHIPS, hr, d), BF16),
        a_spec=pl.BlockSpec((t, hr), lambda i, j, k: (0, i)),
        b_spec=pl.BlockSpec((t, tn), lambda i, j, k: (0, j)),
        o_spec=pl.BlockSpec((None, None, hr, tn), lambda i, j, k: (i % 2, i // 2, 0, j)))[0]


def _lane_half():
    return lax.broadcasted_iota(jnp.int32, (WINDOW, LANES), 1) // ATTN_HEAD_DIM


def _dup_kv(tile, kh):
    return jnp.where(_lane_half() == kh, tile, pltpu.roll(tile, ATTN_HEAD_DIM, 1))


def _stack_heads(tiles, kh):
    half = _lane_half()
    pieces = []
    for g in range(GQA_GROUP):
        pieces.append(jnp.where(half == g % 2, tiles[4 * kh + g // 2], 0.0))
    return jnp.concatenate(pieces, axis=0)


def _unstack_heads(stacked):
    half = _lane_half()
    out = []
    for j in range(GQA_GROUP // 2):
        a = stacked[(2 * j) * WINDOW:(2 * j + 1) * WINDOW]
        b = stacked[(2 * j + 1) * WINDOW:(2 * j + 2) * WINDOW]
        out.append(jnp.where(half == 0, a, b))
    return out


def _attn_probs(qs, kcat, sink_col, n):
    rows = GQA_GROUP * WINDOW
    s = _dot_nt(qs, kcat)
    qi = lax.broadcasted_iota(jnp.int32, (rows, 2 * WINDOW), 0) % WINDOW
    kj = lax.broadcasted_iota(jnp.int32, (rows, 2 * WINDOW), 1)
    first_key = WINDOW * (1 - jnp.minimum(n, 1))
    valid = (kj > qi) & (kj <= qi + WINDOW) & (kj >= first_key)
    s = jnp.where(valid, s, NEG_BIG)
    mx = jnp.maximum(jnp.max(s, axis=-1, keepdims=True), sink_col)
    p = jnp.exp(s - mx)
    p_sink = jnp.exp(sink_col - mx)
    inv = 1.0 / (jnp.sum(p, axis=-1, keepdims=True) + p_sink)
    return p * inv, p_sink * inv


def _attn_operands(sink_ref, q_tiles, k_prev, k_cur, v_prev, v_cur, m, kh):
    qs = _stack_heads([qt * ATTN_SCALE for qt in q_tiles], kh).astype(BF16)
    kcat = jnp.concatenate([_dup_kv(k_prev, kh), _dup_kv(k_cur, kh)], axis=0).astype(BF16)
    vcat = jnp.concatenate([_dup_kv(v_prev, kh), _dup_kv(v_cur, kh)], axis=0).astype(BF16)
    heads_per_group = ATTN_GROUP_LANES // ATTN_HEAD_DIM
    sink_col = jnp.concatenate(
        [jnp.full((WINDOW, 1), sink_ref[0, m * heads_per_group + kh * GQA_GROUP + g], F32)
         for g in range(GQA_GROUP)], axis=0)
    return qs, kcat, vcat, sink_col


def _attn_specs(lay, d):
    a_blk = lay["a_off"] // (2 * ATTN_GROUP_LANES)
    k_blk = lay["k_off"] // LANES
    v_blk = lay["v_off"] // LANES
    before = lambda n: jnp.maximum(ATTN_STEP_BLOCKS * n - 1, 0)
    qg = pl.BlockSpec((ATTN_STEP_ROWS, 2 * ATTN_GROUP_LANES), lambda m, n: (n, a_blk + m))
    kp = pl.BlockSpec((WINDOW, LANES), lambda m, n: (before(n), k_blk + m))
    kc = pl.BlockSpec((ATTN_STEP_ROWS, LANES), lambda m, n: (n, k_blk + m))
    vp = pl.BlockSpec((WINDOW, LANES), lambda m, n: (before(n), v_blk + m))
    vc = pl.BlockSpec((ATTN_STEP_ROWS, LANES), lambda m, n: (n, v_blk + m))
    return qg, kp, kc, vp, vc


def _block_rows(b):
    return slice(b * WINDOW, (b + 1) * WINDOW)


def _kv_tiles(prev_ref, cur_ref, b):
    prev = prev_ref[...] if b == 0 else cur_ref[_block_rows(b - 1), :]
    return prev, cur_ref[_block_rows(b), :]


def _attn_fwd(proj, sinks, lay, d, rider=None):
    t = proj.shape[0]
    n_groups = d // ATTN_GROUP_LANES
    n_blocks = t // ATTN_STEP_ROWS
    pairs = ATTN_GROUP_LANES // LANES
    n_rin = 0 if rider is None else len(rider.operands)
    n_rout = 0 if rider is None else len(rider.out_shapes)

    def body(*refs):
        sink_ref, qg_ref, kp_ref, kc_ref, vp_ref, vc_ref = refs[:6]
        rin = refs[6:6 + n_rin]
        mix_ref, o_ref = refs[6 + n_rin:8 + n_rin]
        rout = refs[8 + n_rin:8 + n_rin + n_rout]
        m = pl.program_id(0)
        n = pl.program_id(1)
        if rider is not None:
            rider.emit(m * n_blocks + n, n_groups * n_blocks, rin, rout, refs[-2], refs[-1])
        for b in range(ATTN_STEP_BLOCKS):
            rows = _block_rows(b)
            k_prev, k_cur = _kv_tiles(kp_ref, kc_ref, b)
            v_prev, v_cur = _kv_tiles(vp_ref, vc_ref, b)
            q_tiles = [qg_ref[rows, p * LANES:(p + 1) * LANES] for p in range(pairs)]
            for kh in range(2):
                qs, kcat, vcat, sink_col = _attn_operands(sink_ref, q_tiles, k_prev, k_cur, v_prev, v_cur, m, kh)
                probs, _ = _attn_probs(qs, kcat, sink_col, ATTN_STEP_BLOCKS * n + b)
                out = _dot(probs.astype(BF16), vcat)
                for j, tile in enumerate(_unstack_heads(out)):
                    p = 4 * kh + j
                    lanes = slice(p * LANES, (p + 1) * LANES)
                    gate = qg_ref[rows, ATTN_GROUP_LANES + p * LANES:ATTN_GROUP_LANES + (p + 1) * LANES]
                    o_ref[rows, lanes] = tile
                    mix_ref[rows, lanes] = (tile * (gate * _sigmoid(gate))).astype(BF16)

    qg, kp, kc, vp, vc = _attn_specs(lay, d)
    out_blk = pl.BlockSpec((ATTN_STEP_ROWS, ATTN_GROUP_LANES), lambda m, n: (n, m))
    out_shapes = [jax.ShapeDtypeStruct((t, 2 * d), BF16), jax.ShapeDtypeStruct((t, d), F32)]
    operands = [sinks, proj, proj, proj, proj, proj]
    scratch, aliases, sem = [], {}, ("parallel", "parallel")
    if rider is not None:
        scratch = rider.scratch()
        out_shapes += list(rider.out_shapes)
        operands += list(rider.operands)
        aliases = {6 + i: 2 + o for i, o in rider.aliases.items()}
        sem = ("arbitrary", "arbitrary")
    return pl.pallas_call(
        body, name="attn_fwd", grid=(n_groups, n_blocks),
        in_specs=[pl.BlockSpec(memory_space=pltpu.SMEM), qg, kp, kc, vp, vc] + [HBM_SPEC] * n_rin,
        out_specs=[out_blk, out_blk] + [HBM_SPEC] * n_rout,
        out_shape=out_shapes, scratch_shapes=scratch, input_output_aliases=aliases,
        compiler_params=_params(dimension_semantics=sem),
    )(*operands)


def _attn_bwd(proj, sinks, attn_o, dmixed, dproj, lay, d):
    t = proj.shape[0]
    n_groups = d // ATTN_GROUP_LANES
    pairs = ATTN_GROUP_LANES // LANES
    kv_w = n_groups * LANES

    def body(sink_ref, qg_ref, kp_ref, kc_ref, vp_ref, vc_ref, o_ref, dm_ref, dproj_hbm,
             dqg_ref, dkc_ref, dkp_ref, dvc_ref, dvp_ref, dsink_ref):
        del dproj_hbm
        m = pl.program_id(0)
        n = pl.program_id(1)
        half = _lane_half()
        sub = lax.broadcasted_iota(jnp.int32, (8, LANES), 0)
        lane = lax.broadcasted_iota(jnp.int32, (8, LANES), 1)
        dsink = jnp.zeros((8, LANES), F32)
        for b in range(ATTN_STEP_BLOCKS):
            rows = _block_rows(b)
            k_prev, k_cur = _kv_tiles(kp_ref, kc_ref, b)
            v_prev, v_cur = _kv_tiles(vp_ref, vc_ref, b)
            q_tiles = [qg_ref[rows, p * LANES:(p + 1) * LANES] for p in range(pairs)]
            do_tiles, o_tiles = [], []
            for p in range(pairs):
                lanes = slice(p * LANES, (p + 1) * LANES)
                gate_lanes = slice(ATTN_GROUP_LANES + p * LANES, ATTN_GROUP_LANES + (p + 1) * LANES)
                gate = qg_ref[rows, gate_lanes]
                sg = _sigmoid(gate)
                dmix = dm_ref[rows, lanes]
                ov = o_ref[rows, lanes]
                dqg_ref[rows, gate_lanes] = (dmix * ov * (sg * (1.0 + gate * (1.0 - sg)))).astype(BF16)
                do_tiles.append(dmix * (gate * sg))
                o_tiles.append(ov)

            dk_cur = dk_prev = dv_cur = dv_prev = jnp.zeros((WINDOW, LANES), F32)
            for kh in range(2):
                qs, kcat, vcat, sink_col = _attn_operands(sink_ref, q_tiles, k_prev, k_cur, v_prev, v_cur, m, kh)
                probs, p_sink = _attn_probs(qs, kcat, sink_col, ATTN_STEP_BLOCKS * n + b)
                dos = _stack_heads(do_tiles, kh)
                delta = jnp.sum(dos * _stack_heads(o_tiles, kh), axis=-1, keepdims=True)
                dos = dos.astype(BF16)
                dp = _dot_nt(dos, vcat)
                ds = (probs * (dp - delta)).astype(BF16)
                dv = _dot_tn(probs.astype(BF16), dos)
                dv = dv + pltpu.roll(dv, ATTN_HEAD_DIM, 1)
                dk = _dot_tn(ds, qs)
                dk = dk + pltpu.roll(dk, ATTN_HEAD_DIM, 1)
                dq = _dot(ds, kcat)
                for j, tile in enumerate(_unstack_heads(dq)):
                    p = 4 * kh + j
                    dqg_ref[rows, p * LANES:(p + 1) * LANES] = (tile * ATTN_SCALE).astype(BF16)
                dk_prev = jnp.where(half == kh, dk[:WINDOW], dk_prev)
                dk_cur = jnp.where(half == kh, dk[WINDOW:], dk_cur)
                dv_prev = jnp.where(half == kh, dv[:WINDOW], dv_prev)
                dv_cur = jnp.where(half == kh, dv[WINDOW:], dv_cur)
                sink_terms = p_sink * delta
                for g in range(GQA_GROUP):
                    val = -jnp.sum(sink_terms[g * WINDOW:(g + 1) * WINDOW])
                    dsink = dsink + jnp.where((sub == 0) & (lane == kh * GQA_GROUP + g), val, 0.0)
            dkc_ref[rows, :] = dk_cur
            dkp_ref[rows, :] = dk_prev
            dvc_ref[rows, :] = dv_cur
            dvp_ref[rows, :] = dv_prev

        @pl.when(n == 0)
        def _():
            dsink_ref[...] = jnp.zeros_like(dsink_ref)

        dsink_ref[...] += dsink

    qg, kp, kc, vp, vc = _attn_specs(lay, d)
    a_blk = lay["a_off"] // (2 * ATTN_GROUP_LANES)
    grp = pl.BlockSpec((ATTN_STEP_ROWS, ATTN_GROUP_LANES), lambda m, n: (n, m))
    kv_blk = pl.BlockSpec((ATTN_STEP_ROWS, LANES), lambda m, n: (n, m))
    kv_shape = jax.ShapeDtypeStruct((t, kv_w), F32)
    outs = pl.pallas_call(
        body, name="attn_bwd", grid=(n_groups, t // ATTN_STEP_ROWS),
        in_specs=[pl.BlockSpec(memory_space=pltpu.SMEM), qg, kp, kc, vp, vc, grp, grp,
                  pl.BlockSpec(memory_space=pl.ANY)],
        out_specs=[pl.BlockSpec((ATTN_STEP_ROWS, 2 * ATTN_GROUP_LANES), lambda m, n: (n, a_blk + m)),
                   kv_blk, kv_blk, kv_blk, kv_blk, pl.BlockSpec((8, LANES), lambda m, n: (m, 0))],
        out_shape=[jax.ShapeDtypeStruct(dproj.shape, BF16), kv_shape, kv_shape, kv_shape, kv_shape,
                   jax.ShapeDtypeStruct((n_groups * 8, LANES), F32)],
        input_output_aliases={8: 0},
        compiler_params=_params(dimension_semantics=("parallel", "arbitrary")),
    )(sinks, proj, proj, proj, proj, proj, attn_o, dmixed, dproj)
    return outs


def _kv_combine(dkc, dkp, dvc, dvp, dproj, lay):
    t, kv_w = dkc.shape
    nb = t // WINDOW
    kv_blk_idx = lay["k_off"] // (2 * kv_w)

    def body(dkc_ref, dkp_ref, dvc_ref, dvp_ref, dproj_hbm, o_ref):
        del dproj_hbm
        keep = (pl.program_id(0) < nb - 1).astype(F32)
        o_ref[:, :kv_w] = (dkc_ref[...] + keep * dkp_ref[...]).astype(BF16)
        o_ref[:, kv_w:] = (dvc_ref[...] + keep * dvp_ref[...]).astype(BF16)

    cur = pl.BlockSpec((WINDOW, kv_w), lambda n: (n, 0))
    nxt = pl.BlockSpec((WINDOW, kv_w), lambda n: (jnp.minimum(n + 1, nb - 1), 0))
    return pl.pallas_call(
        body, name="kv_combine", grid=(nb,),
        in_specs=[cur, nxt, cur, nxt, pl.BlockSpec(memory_space=pl.ANY)],
        out_specs=pl.BlockSpec((WINDOW, 2 * kv_w), lambda n: (n, kv_blk_idx)),
        out_shape=jax.ShapeDtypeStruct(dproj.shape, BF16),
        input_output_aliases={4: 0},
        compiler_params=_params(dimension_semantics=("parallel",)),
    )(dkc, dkp, dvc, dvp, dproj)


def _lower_bound(lbl_ref):
    l0 = lbl_ref[0:1, :]
    l1 = lbl_ref[1:2, :]
    mx = jnp.maximum(l0, l1)
    e0 = jnp.exp(l0 - mx)
    e1 = jnp.exp(l1 - mx)
    return e0 / (e0 + e1)


def _chunk_masks():
    ti = lax.broadcasted_iota(jnp.int32, (CHUNK, CHUNK), 0)
    si = lax.broadcasted_iota(jnp.int32, (CHUNK, CHUNK), 1)
    diag = ((ti // HALF_CHUNK) == (si // HALF_CHUNK)) & (si <= ti)
    off = (ti >= HALF_CHUNK) & (si < HALF_CHUNK)
    lower = (si <= ti).astype(BF16)
    upper = (si >= ti).astype(BF16)
    return diag, off, lower, upper


def _rnn_gates(rq, rf, lb):
    sf = _sigmoid(rf)
    f = lb + (1.0 - lb) * sf
    sq = _sigmoid(rq)
    return sf, f, jnp.log(f), 1.0 - f, sq, rq * sq


def _rnn_decays(g_cum):
    row = lax.broadcasted_iota(jnp.int32, g_cum.shape, 0)
    ref_d = jnp.where(row < HALF_CHUNK, g_cum[HALF_CHUNK // 2 - 1:HALF_CHUNK // 2],
                      g_cum[HALF_CHUNK + HALF_CHUNK // 2 - 1:HALF_CHUNK + HALF_CHUNK // 2])
    ref_o = g_cum[HALF_CHUNK - 1:HALF_CHUNK]
    last = g_cum[CHUNK - 1:CHUNK]
    return dict(eq_d=jnp.exp(g_cum - ref_d), ek_d=jnp.exp(ref_d - g_cum),
                eq_o=jnp.exp(jnp.minimum(g_cum - ref_o, 0.0)), ek_o=jnp.exp(jnp.minimum(ref_o - g_cum, 0.0)),
                eg=jnp.exp(g_cum), ekl=jnp.exp(last - g_cum), e_last=jnp.exp(last))


def _head(a, j):
    return a[:, j * RNN_HEAD_DIM:(j + 1) * RNN_HEAD_DIM]


def _rnn_specs(t, tb, d):
    gw = RNN_GROUP_HEADS * RNN_HEAD_DIM
    return gw, t // tb, tb // CHUNK


def _rnn_fwd(proj, lb_logits, rnn_gain, mixed, d, rider=None):
    t = proj.shape[0]
    tb = min(t, RNN_STEP_ROWS)
    gw, ntb, nch = _rnn_specs(t, tb, d)
    n_groups = d // gw
    n_heads = d // RNN_HEAD_DIM
    n_rin = 0 if rider is None else len(rider.operands)
    n_rout = 0 if rider is None else len(rider.out_shapes)

    def body(*refs):
        blk_ref, lbl_ref, gain_ref = refs[:3]
        rin = refs[4:4 + n_rin]
        mix_ref, o_ref, st_out_ref = refs[4 + n_rin:7 + n_rin]
        rout = refs[7 + n_rin:7 + n_rin + n_rout]
        st_ref = refs[7 + n_rin + n_rout]
        if rider is not None:
            rider.emit(pl.program_id(0) * ntb + pl.program_id(1), n_groups * ntb, rin, rout, refs[-2], refs[-1])

        @pl.when(pl.program_id(1) == 0)
        def _():
            st_ref[...] = jnp.zeros_like(st_ref)

        lb = _lower_bound(lbl_ref)
        gain = gain_ref[...]
        diag, off, lower, _ = _chunk_masks()

        def chunk(c, carry):
            rows = pl.ds(pl.multiple_of(c * CHUNK, CHUNK), CHUNK)
            rq = blk_ref[rows, 0:gw]
            rf = blk_ref[rows, gw:2 * gw]
            v = blk_ref[rows, 2 * gw:3 * gw]
            rg = blk_ref[rows, 3 * gw:4 * gw]
            _, _, g, k, _, q = _rnn_gates(rq, rf, lb)
            dec = _rnn_decays(_tri_dot(lower, g))
            qd = (q * dec["eq_d"]).astype(BF16)
            kd = (k * dec["ek_d"]).astype(BF16)
            qo = (q * dec["eq_o"]).astype(BF16)
            ko = (k * dec["ek_o"]).astype(BF16)
            qe = (q * dec["eg"]).astype(BF16)
            kl = (k * dec["ekl"]).astype(BF16)
            vb = v.astype(BF16)
            outs = []
            for j in range(RNN_GROUP_HEADS):
                st = st_ref[j]
                st_out_ref[j, c] = st
                attn = jnp.where(diag, _dot_nt(_head(qd, j), _head(kd, j)),
                                 jnp.where(off, _dot_nt(_head(qo, j), _head(ko, j)), 0.0))
                o = _dot(attn.astype(BF16), _head(vb, j)) + _dot_nt(_head(qe, j), st.astype(BF16))
                st_ref[j] = st * _head(dec["e_last"], j) + _dot_tn(_head(vb, j), _head(kl, j))
                rr = lax.rsqrt(jnp.mean(o * o, axis=-1, keepdims=True) + NORM_EPS)
                o_ref[rows, j * RNN_HEAD_DIM:(j + 1) * RNN_HEAD_DIM] = o
                outs.append(o * rr)
            on = jnp.concatenate(outs, axis=1) * gain
            mix_ref[rows, :] = (on * (rg * _sigmoid(rg))).astype(BF16)
            return carry

        lax.fori_loop(0, nch, chunk, 0, unroll=True)

    out_shapes = [jax.ShapeDtypeStruct(mixed.shape, BF16), jax.ShapeDtypeStruct((t, d), F32),
                  jax.ShapeDtypeStruct((n_heads, t // CHUNK, RNN_HEAD_DIM, RNN_HEAD_DIM), F32)]
    operands = [proj, lb_logits, rnn_gain, mixed]
    scratch = [pltpu.VMEM((RNN_GROUP_HEADS, RNN_HEAD_DIM, RNN_HEAD_DIM), F32)]
    aliases, sem = {3: 0}, ("parallel", "arbitrary")
    if rider is not None:
        scratch += rider.scratch()
        out_shapes += list(rider.out_shapes)
        operands += list(rider.operands)
        aliases.update({4 + i: 3 + o for i, o in rider.aliases.items()})
        sem = ("arbitrary", "arbitrary")
    return pl.pallas_call(
        body, name="rnn_fwd", grid=(n_groups, ntb),
        in_specs=[pl.BlockSpec((tb, 4 * gw), lambda h, i: (i, h)),
                  pl.BlockSpec((2, gw), lambda h, i: (0, h)),
                  pl.BlockSpec((1, gw), lambda h, i: (0, h)),
                  pl.BlockSpec(memory_space=pl.ANY)] + [HBM_SPEC] * n_rin,
        out_specs=[pl.BlockSpec((tb, gw), lambda h, i: (i, d // gw + h)),
                   pl.BlockSpec((tb, gw), lambda h, i: (i, h)),
                   pl.BlockSpec((RNN_GROUP_HEADS, nch, RNN_HEAD_DIM, RNN_HEAD_DIM), lambda h, i: (h, i, 0, 0))]
        + [HBM_SPEC] * n_rout,
        out_shape=out_shapes, scratch_shapes=scratch, input_output_aliases=aliases,
        compiler_params=_params(dimension_semantics=sem),
    )(*operands)


def _rnn_bwd(proj, lb_logits, rnn_gain, o_pre, states, dmixed, d_total, d, rider=None):
    t = proj.shape[0]
    tb = min(t, RNN_STEP_ROWS)
    gw, ntb, nch = _rnn_specs(t, tb, d)
    n_groups = d // gw
    n_rin = 0 if rider is None else len(rider.operands)
    n_rout = 0 if rider is None else len(rider.out_shapes)

    def body(*refs):
        blk_ref, lbl_ref, gain_ref, o_ref, st_in_ref, dm_ref = refs[:6]
        rin = refs[6:6 + n_rin]
        dproj_ref, dgain_ref, dlb_ref = refs[6 + n_rin:9 + n_rin]
        rout = refs[9 + n_rin:9 + n_rin + n_rout]
        dst_ref = refs[9 + n_rin + n_rout]
        if rider is not None:
            rider.emit(pl.program_id(0) * ntb + pl.program_id(1), n_groups * ntb, rin, rout, refs[-2], refs[-1])

        @pl.when(pl.program_id(1) == 0)
        def _():
            dst_ref[...] = jnp.zeros_like(dst_ref)
            dgain_ref[...] = jnp.zeros_like(dgain_ref)
            dlb_ref[...] = jnp.zeros_like(dlb_ref)

        lb = _lower_bound(lbl_ref)
        gain = gain_ref[...]
        diag, off, lower, upper = _chunk_masks()
        last_row = lax.broadcasted_iota(jnp.int32, (CHUNK, RNN_HEAD_DIM), 0) == CHUNK - 1

        def chunk(step, carry):
            c = nch - 1 - step
            rows = pl.ds(pl.multiple_of(c * CHUNK, CHUNK), CHUNK)
            rq = blk_ref[rows, 0:gw]
            rf = blk_ref[rows, gw:2 * gw]
            v = blk_ref[rows, 2 * gw:3 * gw]
            rg = blk_ref[rows, 3 * gw:4 * gw]
            sf, f, g, k, sq, q = _rnn_gates(rq, rf, lb)
            dec = _rnn_decays(_tri_dot(lower, g))
            qd = (q * dec["eq_d"]).astype(BF16)
            kd = (k * dec["ek_d"]).astype(BF16)
            qo = (q * dec["eq_o"]).astype(BF16)
            ko = (k * dec["ek_o"]).astype(BF16)
            qe = (q * dec["eg"]).astype(BF16)
            kl = (k * dec["ekl"]).astype(BF16)
            vb = v.astype(BF16)

            o = o_ref[rows, :]
            dmix = dm_ref[rows, :]
            sg = _sigmoid(rg)
            n_parts = []
            for j in range(RNN_GROUP_HEADS):
                oj = _head(o, j)
                n_parts.append(oj * lax.rsqrt(jnp.mean(oj * oj, axis=-1, keepdims=True) + NORM_EPS))
            nrm = jnp.concatenate(n_parts, axis=1)
            d_on = dmix * (rg * sg)
            d_rg = dmix * (nrm * gain) * (sg * (1.0 + rg * (1.0 - sg)))
            dgain_ref[...] += jnp.sum(d_on * nrm, axis=0, keepdims=True)
            dn = d_on * gain

            dq_parts, dk_parts, dv_parts, dg_parts = [], [], [], []
            for j in range(RNN_GROUP_HEADS):
                oj, nj, dnj = _head(o, j), _head(nrm, j), _head(dn, j)
                rr = lax.rsqrt(jnp.mean(oj * oj, axis=-1, keepdims=True) + NORM_EPS)
                do = (rr * (dnj - nj * jnp.mean(dnj * nj, axis=-1, keepdims=True))).astype(BF16)
                st = st_in_ref[j, c]
                dst = dst_ref[j]
                stb, dstb = st.astype(BF16), dst.astype(BF16)
                qdj, kdj, qoj, koj = _head(qd, j), _head(kd, j), _head(qo, j), _head(ko, j)
                attn = jnp.where(diag, _dot_nt(qdj, kdj), jnp.where(off, _dot_nt(qoj, koj), 0.0))
                dattn = _dot_nt(do, _head(vb, j))
                da_d = jnp.where(diag, dattn, 0.0).astype(BF16)
                da_o = jnp.where(off, dattn, 0.0).astype(BF16)
                dv = _dot_tn(attn.astype(BF16), do) + _dot_nt(_head(kl, j), dstb)
                dq_inter = _dot(do, stb) * _head(dec["eg"], j)
                dq_d, dq_o = _dot(da_d, kdj), _dot(da_o, koj)
                dq = dq_inter + dq_d * _head(dec["eq_d"], j) + dq_o * _head(dec["eq_o"], j)
                dk_inter = _dot(_head(vb, j), dstb) * _head(dec["ekl"], j)
                dk_d, dk_o = _dot_tn(da_d, qdj), _dot_tn(da_o, qoj)
                dk = dk_inter + dk_d * _head(dec["ek_d"], j) + dk_o * _head(dec["ek_o"], j)
                kj, qj = _head(k, j), _head(q, j)
                e_last = _head(dec["e_last"], j)
                extra = (jnp.sum(kj * dk_inter, axis=0, keepdims=True)
                         + e_last * jnp.sum(st * dst, axis=0, keepdims=True))
                dg_cum = (qj * dq_inter - kj * dk_inter
                          + (qdj.astype(F32) * dq_d + qoj.astype(F32) * dq_o)
                          - (kdj.astype(F32) * dk_d + koj.astype(F32) * dk_o))
                dg_parts.append(jnp.where(last_row, dg_cum + extra, dg_cum))
                dst_ref[j] = dst * e_last + _dot_tn(do, _head(qe, j))
                dq_parts.append(dq)
                dk_parts.append(dk)
                dv_parts.append(dv)

            dq = jnp.concatenate(dq_parts, axis=1)
            dk = jnp.concatenate(dk_parts, axis=1)
            dg = _tri_dot(upper, jnp.concatenate(dg_parts, axis=1))
            df = dg / f - dk
            dlb_ref[...] += jnp.sum(df * (1.0 - sf), axis=0, keepdims=True)
            d_rf = df * (1.0 - lb) * (sf * (1.0 - sf))
            d_rq = dq * (sq * (1.0 + rq * (1.0 - sq)))
            dproj_ref[rows, 0:gw] = d_rq.astype(BF16)
            dproj_ref[rows, gw:2 * gw] = d_rf.astype(BF16)
            dproj_ref[rows, 2 * gw:3 * gw] = jnp.concatenate(dv_parts, axis=1).astype(BF16)
            dproj_ref[rows, 3 * gw:4 * gw] = d_rg.astype(BF16)
            return carry

        lax.fori_loop(0, nch, chunk, 0, unroll=True)

    rev = lambda i: ntb - 1 - i
    vec = pl.BlockSpec((1, gw), lambda h, i: (0, h))
    scratch = [pltpu.VMEM((RNN_GROUP_HEADS, RNN_HEAD_DIM, RNN_HEAD_DIM), F32)]
    out_shapes = [jax.ShapeDtypeStruct((t, d_total), BF16), jax.ShapeDtypeStruct((1, d), F32),
                  jax.ShapeDtypeStruct((1, d), F32)]
    operands = [proj, lb_logits, rnn_gain, o_pre, states, dmixed]
    sem = ("parallel", "arbitrary")
    if rider is not None:
        scratch += rider.scratch()
        out_shapes += list(rider.out_shapes)
        operands += list(rider.operands)
        sem = ("arbitrary", "arbitrary")
    return pl.pallas_call(
        body, name="rnn_bwd", grid=(n_groups, ntb),
        in_specs=[pl.BlockSpec((tb, 4 * gw), lambda h, i: (rev(i), h)),
                  pl.BlockSpec((2, gw), lambda h, i: (0, h)), vec,
                  pl.BlockSpec((tb, gw), lambda h, i: (rev(i), h)),
                  pl.BlockSpec((RNN_GROUP_HEADS, nch, RNN_HEAD_DIM, RNN_HEAD_DIM), lambda h, i: (h, rev(i), 0, 0)),
                  pl.BlockSpec((tb, gw), lambda h, i: (rev(i), d // gw + h))] + [HBM_SPEC] * n_rin,
        out_specs=[pl.BlockSpec((tb, 4 * gw), lambda h, i: (rev(i), h)), vec, vec] + [HBM_SPEC] * n_rout,
        out_shape=out_shapes, scratch_shapes=scratch,
        compiler_params=_params(dimension_semantics=sem),
    )(*operands)


def _local_grads(x, target, w_in_full, w_out, sinks, lb_logits, rnn_gain, pre_gain, post_gain, sc=None):
    t, d = x.shape
    comm = sc is not None
    lay = _layout(d)
    perm = lay["perm"]
    h = _prenorm_fwd(x, pre_gain)
    if comm:
        proj, w_in_full = _proj_gather_mm(h, w_in_full, perm, sc)
        mixed, attn_o, w_out = _attn_fwd(proj, sinks, lay, d, _gather_rider(w_out, 0.7, 0, 2))
        mixed, o_pre, states, w_out_full = _rnn_fwd(proj, lb_logits, rnn_gain, mixed, d,
                                                    _gather_rider(w_out, 0.7, 1, 2))
    else:
        (proj,) = _proj_mm(h, w_in_full, perm)
        w_out_full = w_out
        mixed, attn_o = _attn_fwd(proj, sinks, lay, d)
        mixed, o_pre, states = _rnn_fwd(proj, lb_logits, rnn_gain, mixed, d)
    y = _out_mm(mixed, w_out_full)
    dy, dz, g_post, loss = _post_loss(x, y, target, post_gain)
    gw_out = _gw_out_mm(mixed, dy)
    rider = None
    if comm:
        dmixed, recv_out = _dmixed_mm(dy, w_out_full, _pair_exchange_rider(gw_out, stacked=True))
        p_out = _pair_sum_out(gw_out, recv_out)
        rider = _chip_exchange_rider(p_out, lambda ref, chip: ref.at[chip])
    else:
        (dmixed,) = _dmixed_mm(dy, w_out_full)
    dproj, g_rnn, g_lb, *r_out = _rnn_bwd(proj, lb_logits, rnn_gain, o_pre, states, dmixed, lay["total"], d, rider)
    dproj, dkc, dkp, dvc, dvp, dsink = _attn_bwd(proj, sinks, attn_o, dmixed, dproj, lay, d)
    dproj = _kv_combine(dkc, dkp, dvc, dvp, dproj, lay)
    if comm:
        c = lax.axis_index("c")
        (g_other,) = _gw_in_mm("gw_in_mm_other", h, dproj, perm, 1 - c)
        g_mine, recv_in = _gw_in_mm("gw_in_mm_mine", h, dproj, perm, c, _pair_exchange_rider(g_other))
        p_in = _pair_sum_in(g_mine, recv_in, sc)
        rider = _chip_exchange_rider(
            p_in, lambda ref, chip: ref.at[:, pl.ds(pl.multiple_of(chip * sc, LANES), sc)])
    dh, *r_in = _dh_mm(dproj, w_in_full, perm, rider)
    if comm:
        rider = _share_rider(_chip_sum_in(p_in, r_in[0], sc), _chip_sum_out(p_out, r_out[0]))
        grad_x, g_pre, gw_in, gw_out = _prenorm_bwd(x, dh, dz, pre_gain, rider)
    else:
        grad_x, g_pre = _prenorm_bwd(x, dh, dz, pre_gain)
    heads_per_group = ATTN_GROUP_LANES // ATTN_HEAD_DIM
    g_sink = dsink.reshape(d // ATTN_GROUP_LANES, 8, LANES)[:, 0, :heads_per_group].reshape(1, -1)
    small = dict(sink=g_sink, lb=g_lb, rnn=g_rnn, pre=g_pre, post=g_post)
    if comm:
        return loss, grad_x, gw_in, gw_out, small
    gw_in = jnp.stack([_gw_in_mm("gw_in_mm_%d" % half, h, dproj, perm, half)[0] for half in range(2)])
    return loss, grad_x, gw_in, gw_out, small


def _mesh_pos():
    x, y, c = lax.axis_index("x"), lax.axis_index("y"), lax.axis_index("c")
    chips = [(1 - x, y), (x, 1 - y), (1 - x, 1 - y)]
    return x, y, c, chips


def _remote(src, dst, send_sem, recv_sem, device):
    return pltpu.make_async_remote_copy(src_ref=src, dst_ref=dst, send_sem=send_sem, recv_sem=recv_sem,
                                        device_id=device, device_id_type=MESH)


HBM_SPEC = pl.BlockSpec(memory_space=pl.ANY)


def _gather_rider(part, forward_at, section, n_sections):
    rows = part.shape[0] // N_CHIPS
    half_rows = rows // 2
    sec_rows = half_rows // n_sections

    def stages(ins, outs, send_sems, recv_sems):
        del ins
        full = outs[0]

        def piece(chip, half):
            start = chip * rows + half * half_rows + section * sec_rows
            return full.at[pl.ds(pl.multiple_of(start, 8), sec_rows), :]

        def sends():
            x, y, c, chips = _mesh_pos()
            mine = piece(2 * x + y, c)
            return [_remote(mine, mine, send_sems.at[j], recv_sems.at[j], (px, py, c))
                    for j, (px, py) in enumerate(chips)]

        def forwards(half_of):
            x, y, c, chips = _mesh_pos()
            out = []
            for j, (px, py) in enumerate(chips):
                block = piece(2 * px + py, half_of(c))
                out.append(_remote(block, block, send_sems.at[3 + j], recv_sems.at[3 + j], (x, y, 1 - c)))
            return out

        def start():
            for cp in sends():
                cp.start()

        def forward():
            x, y, c, chips = _mesh_pos()
            for j, (px, py) in enumerate(chips):
                landed = piece(2 * px + py, c)
                _remote(landed, landed, send_sems.at[j], recv_sems.at[j], (x, y, 1 - c)).wait_recv()
            for cp in forwards(lambda c: c):
                cp.start()

        def finish():
            for cp in forwards(lambda c: 1 - c):
                cp.wait_recv()
            for cp in sends() + forwards(lambda c: c):
                cp.wait_send()

        return [(0.0, start), (forward_at, forward), (1.0, finish)]

    return _Rider((part,), (jax.ShapeDtypeStruct(part.shape, BF16),), {0: 0}, 6, stages)


def _chip_exchange_rider(partial, piece):
    if partial.ndim == 3:
        recv_shape = (N_CHIPS - 1,) + partial.shape[1:]
    else:
        recv_shape = (N_CHIPS - 1, partial.shape[0], partial.shape[1] // N_CHIPS)

    def stages(ins, outs, send_sems, recv_sems):
        def copies():
            x, y, c, chips = _mesh_pos()
            return [_remote(piece(ins[0], 2 * px + py), outs[0].at[j], send_sems.at[j], recv_sems.at[j], (px, py, c))
                    for j, (px, py) in enumerate(chips)]

        def start():
            for cp in copies():
                cp.start()

        def finish():
            for cp in copies():
                cp.wait()

        return [(0.0, start), (1.0, finish)]

    return _Rider((partial,), (jax.ShapeDtypeStruct(recv_shape, BF16),), {}, N_CHIPS - 1, stages)


def _pair_exchange_rider(g, stacked=False):
    shape = g.shape[1:] if stacked else g.shape

    def stages(ins, outs, send_sems, recv_sems):
        def copy():
            x, y, c, _ = _mesh_pos()
            src = ins[0].at[1 - c] if stacked else ins[0]
            return _remote(src, outs[0], send_sems.at[0], recv_sems.at[0], (x, y, 1 - c))

        return [(0.0, lambda: copy().start()), (1.0, lambda: copy().wait())]

    return _Rider((g,), (jax.ShapeDtypeStruct(shape, g.dtype),), {}, 1, stages)


def _pair_sum_in(mine, recv, sc):
    hd, d_in = mine.shape
    tr = min(hd, 256)

    def body(a_ref, b_ref, o_ref):
        o_ref[...] = (a_ref[...].astype(F32) + b_ref[...].astype(F32)).astype(BF16)

    blk = pl.BlockSpec((tr, sc), lambda i, j: (i, j))
    return pl.pallas_call(
        body, name="pair_sum_in", grid=(hd // tr, d_in // sc), in_specs=[blk, blk], out_specs=blk,
        out_shape=jax.ShapeDtypeStruct((hd, d_in), BF16),
        compiler_params=_params(dimension_semantics=("parallel", "parallel")),
    )(mine, recv)


def _pair_sum_out(gw_out, recv):
    _, n_chips, hr, d = gw_out.shape
    tr = min(hr, 256)
    c = lax.axis_index("c")

    def body(c_ref, a_ref, b_ref, o_ref):
        del c_ref
        o_ref[...] = (a_ref[...].astype(F32) + b_ref[...].astype(F32)).astype(BF16)

    blk = pl.BlockSpec((None, tr, d), lambda k, i, cc: (k, i, 0))
    gs = pltpu.PrefetchScalarGridSpec(
        num_scalar_prefetch=1, grid=(n_chips, hr // tr),
        in_specs=[pl.BlockSpec((None, None, tr, d), lambda k, i, cc: (cc[0], k, i, 0)), blk], out_specs=blk)
    return pl.pallas_call(
        body, name="pair_sum_out", grid_spec=gs, out_shape=jax.ShapeDtypeStruct((n_chips, hr, d), BF16),
        compiler_params=_params(dimension_semantics=("parallel", "parallel")),
    )(jnp.reshape(c, (1,)).astype(jnp.int32), gw_out, recv)


def _place():
    return jnp.stack([2 * lax.axis_index("x") + lax.axis_index("y"), lax.axis_index("c")]).astype(jnp.int32)


def _chip_sum_in(p_in, r_in, sc):
    hd = p_in.shape[0]
    tr = min(hd, 256)
    nblk = hd // tr

    def body(pos_ref, p_ref, r_ref, o_ref):
        del pos_ref
        acc = p_ref[...].astype(F32)
        for j in range(3):
            acc = acc + r_ref[j].astype(F32)
        o_ref[...] = acc

    gs = pltpu.PrefetchScalarGridSpec(
        num_scalar_prefetch=1, grid=(nblk,),
        in_specs=[pl.BlockSpec((tr, sc), lambda i, pos: (i, pos[0])), pl.BlockSpec((3, tr, sc), lambda i, pos: (0, i, 0))],
        out_specs=pl.BlockSpec((tr, sc), lambda i, pos: (pos[1] * nblk + i, 0)))
    return pl.pallas_call(
        body, name="chip_sum_in", grid_spec=gs, out_shape=jax.ShapeDtypeStruct((2 * hd, sc), F32),
        compiler_params=_params(dimension_semantics=("parallel",)),
    )(_place(), p_in, r_in)


def _chip_sum_out(p_out, r_out):
    _, hr, d = p_out.shape
    tr = min(hr, 256)
    nblk = hr // tr

    def body(pos_ref, p_ref, r_ref, o_ref):
        del pos_ref
        acc = p_ref[...].astype(F32)
        for j in range(3):
            acc = acc + r_ref[j].astype(F32)
        o_ref[...] = acc

    gs = pltpu.PrefetchScalarGridSpec(
        num_scalar_prefetch=1, grid=(nblk,),
        in_specs=[pl.BlockSpec((None, tr, d), lambda i, pos: (pos[0], i, 0)), pl.BlockSpec((3, tr, d), lambda i, pos: (0, i, 0))],
        out_specs=pl.BlockSpec((tr, d), lambda i, pos: (pos[1] * nblk + i, 0)))
    return pl.pallas_call(
        body, name="chip_sum_out", grid_spec=gs, out_shape=jax.ShapeDtypeStruct((2 * hr, d), F32),
        compiler_params=_params(dimension_semantics=("parallel",)),
    )(_place(), p_out, r_out)


def _adamw_math(w, g, m, v):
    m_new = ADAM_B1 * m + (1.0 - ADAM_B1) * g
    v_new = ADAM_B2 * v + (1.0 - ADAM_B2) * (g * g)
    m_hat = m_new / (1.0 - ADAM_B1 ** ADAM_STEP)
    v_hat = v_new / (1.0 - ADAM_B2 ** ADAM_STEP)
    delta = -ADAM_LR * (m_hat / (jnp.sqrt(v_hat) + ADAM_EPS) + ADAM_WD * w)
    return delta, m_new, v_new


def _share_rider(g_in, g_out):
    def stages(ins, outs, send_sems, recv_sems):
        del ins

        def copies(half_of):
            x, y, c, _ = _mesh_pos()
            out = []
            for k, ref in enumerate(outs):
                rows = ref.shape[0] // 2
                blk = ref.at[pl.ds(pl.multiple_of(half_of(c) * rows, 8), rows), :]
                out.append(_remote(blk, blk, send_sems.at[k], recv_sems.at[k], (x, y, 1 - c)))
            return out

        def start():
            for cp in copies(lambda c: c):
                cp.start()

        def finish():
            for cp in copies(lambda c: c):
                cp.wait_send()
            for cp in copies(lambda c: 1 - c):
                cp.wait_recv()

        return [(0.0, start), (1.0, finish)]

    shapes = (jax.ShapeDtypeStruct(g_in.shape, F32), jax.ShapeDtypeStruct(g_out.shape, F32))
    return _Rider((g_in, g_out), shapes, {0: 0, 1: 1}, 2, stages)


def _adamw(w, g, m, v, name):
    rows, cols = w.shape
    streams = 8
    fit = (VMEM_LIMIT_BYTES // 2) // (streams * 2 * cols * 4)
    tr = min(rows, 1 << (fit.bit_length() - 1))

    def body(w_ref, g_ref, m_ref, v_ref, d_ref, mo_ref, vo_ref, go_ref):
        gv = g_ref[...]
        delta, m_new, v_new = _adamw_math(w_ref[...], gv, m_ref[...], v_ref[...])
        d_ref[...] = delta
        mo_ref[...] = m_new
        vo_ref[...] = v_new
        go_ref[...] = gv

    spec = pl.BlockSpec((tr, cols), lambda i: (i, 0))
    shape = jax.ShapeDtypeStruct((rows, cols), F32)
    return pl.pallas_call(
        body, name=name, grid=(rows // tr,), in_specs=[spec] * 4, out_specs=[spec] * 4,
        out_shape=[shape] * 4, compiler_params=_params(dimension_semantics=("parallel",)),
    )(w, g, m, v)


SMALL_ROWS = 8


def _small_allreduce_adamw(part, w_pack, m_pack, v_pack):
    d = part.shape[1]

    def body(part_ref, w_ref, m_ref, v_ref, g_ref, d_ref, mo_ref, vo_ref, buf_ref, send_sems, recv_sems):
        x, y, c, _ = _mesh_pos()
        me = 4 * x + 2 * y + c
        buf_ref[0] = part_ref[...]
        copies = []
        for r in range(1, 8):
            rx, ry, rc = (r >> 2) & 1, (r >> 1) & 1, r & 1
            peer = (x ^ rx, y ^ ry, c ^ rc)
            copies.append(_remote(buf_ref.at[0], buf_ref.at[r], send_sems.at[r - 1], recv_sems.at[r - 1], peer))
        for cp in copies:
            cp.start()
        for cp in copies:
            cp.wait()
        total = buf_ref[me]
        for s in range(1, 8):
            total = total + buf_ref[s ^ me]
        w = w_ref[...]
        row = lax.broadcasted_iota(jnp.int32, (SMALL_ROWS, d), 0)
        l0, l1 = w[3:4], w[4:5]
        mx = jnp.maximum(l0, l1)
        e0, e1 = jnp.exp(l0 - mx), jnp.exp(l1 - mx)
        lb = e0 / (e0 + e1)
        g_l0 = total[3:4] * lb * (1.0 - lb)
        grads = jnp.where(row == 3, g_l0, jnp.where(row == 4, -g_l0, total))
        g_ref[...] = grads
        delta, m_new, v_new = _adamw_math(w, grads, m_ref[...], v_ref[...])
        d_ref[...] = delta
        mo_ref[...] = m_new
        vo_ref[...] = v_new

    vm = pl.BlockSpec(memory_space=pltpu.VMEM)
    shape = jax.ShapeDtypeStruct((SMALL_ROWS, d), F32)
    return pl.pallas_call(
        body, name="small_allreduce_adamw",
        in_specs=[vm] * 4, out_specs=[vm] * 4, out_shape=[shape] * 4,
        scratch_shapes=[pltpu.VMEM((8, SMALL_ROWS, d), F32), pltpu.SemaphoreType.DMA((7,)), pltpu.SemaphoreType.DMA((7,))],
    )(part, w_pack, m_pack, v_pack)


def _pack_small(d, pre, post, rnn, lb, sink, extra=None):
    rows = [pre, post, rnn, lb[0:1], lb[1:2],
            jnp.pad(sink, ((0, 0), (0, d - sink.shape[1]))),
            jnp.zeros((1, d), F32) if extra is None else extra,
            jnp.zeros((1, d), F32)]
    return jnp.concatenate(rows, axis=0)


def _unpack_small(p, n_sink):
    return dict(pre=p[0:1], post=p[1:2], rnn=p[2:3], lb=p[3:5], sink=p[5:6, :n_sink])


def kernel(x, w_in, attn_sinks, lb_logits, rnn_norm, w_out, pre_norm, post_norm, loss_target, m_w_in, m_attn_sinks, m_lb_logits, m_rnn_norm, m_w_out, m_pre_norm, m_post_norm, v_w_in, v_attn_sinks, v_lb_logits, v_rnn_norm, v_w_out, v_pre_norm, v_post_norm):
    t, d = x.shape[1], x.shape[2]
    sc = w_in.shape[2]
    n_sink = attn_sinks.shape[1]
    w_in2, w_out2 = w_in[0], w_out[0]

    w_in_part = _cast_into_gathered(w_in2, "cast_w_in", 1)
    w_out_part = _cast_into_gathered(w_out2, "cast_w_out", 0)
    loss_part, grad_x, g_w_in, g_w_out, small = _local_grads(
        x[0], loss_target[0], w_in_part, w_out_part, attn_sinks, lb_logits, rnn_norm, pre_norm, post_norm, sc)
    d_w_in, nm_w_in, nv_w_in, g_w_in = _adamw(w_in2, g_w_in, m_w_in[0], v_w_in[0], "adamw_w_in")
    d_w_out, nm_w_out, nv_w_out, g_w_out = _adamw(w_out2, g_w_out, m_w_out[0], v_w_out[0], "adamw_w_out")

    lb_part = jnp.concatenate([small["lb"], jnp.zeros_like(small["lb"])], axis=0)
    loss_row = jnp.pad(loss_part[:, :1], ((0, 0), (0, d - 1)))
    part = _pack_small(d, small["pre"], small["post"], small["rnn"], lb_part, small["sink"], loss_row)
    w_pack = _pack_small(d, pre_norm, post_norm, rnn_norm, lb_logits, attn_sinks)
    m_pack = _pack_small(d, m_pre_norm, m_post_norm, m_rnn_norm, m_lb_logits, m_attn_sinks)
    v_pack = _pack_small(d, v_pre_norm, v_post_norm, v_rnn_norm, v_lb_logits, v_attn_sinks)
    g_pack, d_pack, nm_pack, nv_pack = _small_allreduce_adamw(part, w_pack, m_pack, v_pack)
    loss = g_pack[6, 0]
    g, dl, nm, nv = (_unpack_small(p, n_sink) for p in (g_pack, d_pack, nm_pack, nv_pack))

    def ordered(w_in_leaf, w_out_leaf, s):
        return (w_in_leaf[None], s["sink"], s["lb"], s["rnn"], w_out_leaf[None], s["pre"], s["post"])

    return (loss, grad_x[None],
            *ordered(g_w_in, g_w_out, g), *ordered(d_w_in, d_w_out, dl),
            *ordered(nm_w_in, nm_w_out, nm), *ordered(nv_w_in, nv_w_out, nv))
```

```python
import numpy as np
import jax
import jax.numpy as jnp
from jax import lax
from jax.experimental import pallas as pl
from jax.experimental.pallas import tpu as pltpu

F32 = jnp.float32
BF16 = jnp.bfloat16
MESH = pl.DeviceIdType.MESH

NORM_EPS = 1e-6
ATTN_HEAD_DIM = 64
GQA_GROUP = 8
WINDOW = 128
ATTN_STEP_BLOCKS = 8
ATTN_STEP_ROWS = ATTN_STEP_BLOCKS * WINDOW
RNN_HEAD_DIM = 128
CHUNK = 64
HALF_CHUNK = CHUNK // 2
ATTN_SCALE = ATTN_HEAD_DIM ** -0.5

ADAM_LR = 0.001
ADAM_B1 = 0.9
ADAM_B2 = 0.999
ADAM_EPS = 1e-08
ADAM_WD = 0.01
ADAM_STEP = 10

LANES = 128
COL_TILE = 512
RNN_GROUP_HEADS = 8
RNN_STEP_ROWS = 512
ATTN_GROUP_LANES = 1024
N_CHIPS = 4
VMEM_LIMIT_BYTES = 56 * 1024 * 1024
NEG_BIG = -1e30


def _params(**kw):
    return pltpu.CompilerParams(vmem_limit_bytes=VMEM_LIMIT_BYTES, **kw)


def _sigmoid(x):
    return 1.0 / (1.0 + jnp.exp(-x))


def _dot(a, b):
    return jnp.dot(a, b, preferred_element_type=F32)


def _dot_nt(a, b):
    return lax.dot_general(a, b, (((1,), (1,)), ((), ())), preferred_element_type=F32)


def _dot_tn(a, b):
    return lax.dot_general(a, b, (((0,), (0,)), ((), ())), preferred_element_type=F32)


def _tri_dot(tri_bf16, x):
    hi = x.astype(BF16)
    lo = (x - hi.astype(F32)).astype(BF16)
    return _dot(tri_bf16, hi) + _dot(tri_bf16, lo)


def _layout(d_model):
    d = d_model
    dkv = d // GQA_GROUP
    orig = dict(aq=0, ak=d, av=d + dkv, ag=d + 2 * dkv)
    base = d + 2 * dkv + d
    orig.update(rq=base, rf=base + d, ri=base + 2 * d, rg=base + 3 * d)
    group_w = RNN_GROUP_HEADS * RNN_HEAD_DIM
    cols = []
    for hg in range(d // group_w):
        for seg in ("rq", "rf", "ri", "rg"):
            cols.append((orig[seg] + hg * group_w, group_w))
    for m in range(d // ATTN_GROUP_LANES):
        for seg in ("aq", "ag"):
            cols.append((orig[seg] + m * ATTN_GROUP_LANES, ATTN_GROUP_LANES))
    cols.append((orig["ak"], dkv))
    cols.append((orig["av"], dkv))
    units = []
    for start, width in cols:
        assert start % LANES == 0 and width % LANES == 0
        units += [start + u for u in range(0, width, LANES)]
    per = COL_TILE // LANES
    assert len(units) % per == 0
    tiles = []
    for t in range(len(units) // per):
        run = units[t * per:(t + 1) * per]
        assert run[0] % COL_TILE == 0 and all(run[i] == run[0] + i * LANES for i in range(per))
        tiles.append(run[0] // COL_TILE)
    return dict(a_off=4 * d, k_off=6 * d, v_off=6 * d + dkv, total=6 * d + 2 * dkv,
                perm=np.asarray(tiles, np.int32))


def _chip_index():
    return jnp.reshape(2 * lax.axis_index("x") + lax.axis_index("y"), (1,)).astype(jnp.int32)


def _cast_into_gathered(a, name, axis):
    rows, cols = a.shape
    tr = min(rows, 512)
    nblk = rows // tr

    def body(me_ref, a_ref, o_ref):
        del me_ref
        o_ref[...] = a_ref[...].astype(BF16)

    if axis == 1:
        out_spec = pl.BlockSpec((tr, cols), lambda i, me: (i, me[0]))
        shape = (rows, N_CHIPS * cols)
    else:
        out_spec = pl.BlockSpec((tr, cols), lambda i, me: (me[0] * nblk + i, 0))
        shape = (N_CHIPS * rows, cols)
    gs = pltpu.PrefetchScalarGridSpec(num_scalar_prefetch=1, grid=(nblk,),
                                      in_specs=[pl.BlockSpec((tr, cols), lambda i, me: (i, 0))], out_specs=out_spec)
    return pl.pallas_call(
        body, name=name, grid_spec=gs, out_shape=jax.ShapeDtypeStruct(shape, BF16),
        compiler_params=_params(dimension_semantics=("parallel",)),
    )(_chip_index(), a)


def _prenorm_fwd(x, gain):
    t, d = x.shape
    tm = min(t, 256)

    def body(x_ref, g_ref, h_ref):
        xv = x_ref[...]
        r = lax.rsqrt(jnp.mean(xv * xv, axis=-1, keepdims=True) + NORM_EPS)
        h_ref[...] = ((xv * r) * g_ref[...]).astype(BF16)

    return pl.pallas_call(
        body, name="prenorm_fwd", grid=(t // tm,),
        in_specs=[pl.BlockSpec((tm, d), lambda i: (i, 0)), pl.BlockSpec((1, d), lambda i: (0, 0))],
        out_specs=pl.BlockSpec((tm, d), lambda i: (i, 0)),
        out_shape=jax.ShapeDtypeStruct((t, d), BF16),
        compiler_params=_params(dimension_semantics=("parallel",)),
    )(x, gain)


def _post_loss(x, y, target, gain):
    t, d = x.shape
    tm = min(t, 256)
    inv_d = 1.0 / d

    def body(x_ref, y_ref, t_ref, g_ref, dy_ref, dz_ref, gp_ref, loss_ref):
        i = pl.program_id(0)
        yv = y_ref[...]
        gain_v = g_ref[...]
        r = lax.rsqrt(jnp.mean(yv * yv, axis=-1, keepdims=True) + NORM_EPS)
        n = yv * r
        e = (x_ref[...] + n * gain_v) - t_ref[...]
        dz = e * inv_d
        dn = dz * gain_v
        dy = r * (dn - n * jnp.mean(dn * n, axis=-1, keepdims=True))
        dy_ref[...] = dy.astype(BF16)
        dz_ref[...] = dz

        @pl.when(i == 0)
        def _():
            gp_ref[...] = jnp.zeros_like(gp_ref)
            loss_ref[...] = jnp.zeros_like(loss_ref)

        gp_ref[...] += jnp.sum(dz * n, axis=0, keepdims=True)
        row = jnp.sum(e * e, axis=-1, keepdims=True)
        loss_ref[...] += jnp.full(loss_ref.shape, 0.5 * inv_d * jnp.sum(row), F32)

    row_spec = pl.BlockSpec((tm, d), lambda i: (i, 0))
    vec_spec = pl.BlockSpec((1, d), lambda i: (0, 0))
    return pl.pallas_call(
        body, name="post_loss", grid=(t // tm,),
        in_specs=[row_spec, row_spec, row_spec, vec_spec],
        out_specs=[row_spec, row_spec, vec_spec, pl.BlockSpec((1, LANES), lambda i: (0, 0))],
        out_shape=[jax.ShapeDtypeStruct((t, d), BF16), jax.ShapeDtypeStruct((t, d), F32),
                   jax.ShapeDtypeStruct((1, d), F32), jax.ShapeDtypeStruct((1, LANES), F32)],
        compiler_params=_params(dimension_semantics=("arbitrary",)),
    )(x, y, target, gain)


def _prenorm_bwd(x, dh, dz, gain, rider=None):
    t, d = x.shape
    tm = min(t, 256)
    n_rin = 0 if rider is None else len(rider.operands)
    n_rout = 0 if rider is None else len(rider.out_shapes)

    def body(*refs):
        x_ref, dh_ref, dz_ref, g_ref = refs[:4]
        gx_ref, gp_ref = refs[4 + n_rin:6 + n_rin]
        i = pl.program_id(0)
        if rider is not None:
            rider.emit(i, t // tm, refs[4:4 + n_rin], refs[6 + n_rin:6 + n_rin + n_rout], refs[-2], refs[-1])
        xv = x_ref[...]
        r = lax.rsqrt(jnp.mean(xv * xv, axis=-1, keepdims=True) + NORM_EPS)
        n = xv * r
        dhv = dh_ref[...]
        dn = dhv * g_ref[...]
        gx_ref[...] = dz_ref[...] + r * (dn - n * jnp.mean(dn * n, axis=-1, keepdims=True))

        @pl.when(i == 0)
        def _():
            gp_ref[...] = jnp.zeros_like(gp_ref)

        gp_ref[...] += jnp.sum(dhv * n, axis=0, keepdims=True)

    row_spec = pl.BlockSpec((tm, d), lambda i: (i, 0))
    vec_spec = pl.BlockSpec((1, d), lambda i: (0, 0))
    out_shapes = [jax.ShapeDtypeStruct((t, d), F32), jax.ShapeDtypeStruct((1, d), F32)]
    operands = [x, dh, dz, gain]
    scratch, aliases = [], {}
    if rider is not None:
        scratch = rider.scratch()
        out_shapes += list(rider.out_shapes)
        operands += list(rider.operands)
        aliases = {4 + i: 2 + o for i, o in rider.aliases.items()}
    return pl.pallas_call(
        body, name="prenorm_bwd", grid=(t // tm,),
        in_specs=[row_spec, row_spec, row_spec, vec_spec] + [HBM_SPEC] * n_rin,
        out_specs=[row_spec, vec_spec] + [HBM_SPEC] * n_rout,
        out_shape=out_shapes, scratch_shapes=scratch, input_output_aliases=aliases,
        compiler_params=_params(dimension_semantics=("arbitrary",)),
    )(*operands)


class _Rider:
    def __init__(self, operands, out_shapes, aliases, n_sems, stages):
        self.operands = tuple(operands)
        self.out_shapes = tuple(out_shapes)
        self.aliases = dict(aliases)
        self.n_sems = n_sems
        self.stages = stages

    def scratch(self):
        return [pltpu.SemaphoreType.DMA((self.n_sems,)), pltpu.SemaphoreType.DMA((self.n_sems,))]

    def emit(self, step, n_steps, in_refs, out_refs, send_sems, recv_sems):
        for frac, fn in self.stages(in_refs, out_refs, send_sems, recv_sems):
            at = min(n_steps - 1, int(frac * (n_steps - 1) + 0.5))
            pl.when(step == at)(fn)


def _matmul(name, a, b, *, out_shape, grid, a_spec, b_spec, o_spec, nt=False, ta=False, perm=None, rider=None):
    nk = grid[2]
    n_steps = grid[0] * grid[1] * grid[2]
    tm, tn = [s for s in o_spec.block_shape if s is not None][-2:]
    acc_in_out = out_shape.dtype == F32
    n_pre = 0 if perm is None else 1
    n_rin = 0 if rider is None else len(rider.operands)
    n_rout = 0 if rider is None else len(rider.out_shapes)
    use_acc = not (nk == 1 or acc_in_out)

    def body(*refs):
        refs = refs[n_pre:]
        a_ref, b_ref = refs[:2]
        rin = refs[2:2 + n_rin]
        o_ref = refs[2 + n_rin]
        rout = refs[3 + n_rin:3 + n_rin + n_rout]
        scratch_refs = refs[3 + n_rin + n_rout:]
        if rider is not None:
            step = (pl.program_id(0) * grid[1] + pl.program_id(1)) * grid[2] + pl.program_id(2)
            rider.emit(step, n_steps, rin, rout, scratch_refs[-2], scratch_refs[-1])
        def product():
            if ta:
                return _dot_tn(a_ref[...], b_ref[...])
            return _dot_nt(a_ref[...], b_ref[...]) if nt else _dot(a_ref[...], b_ref[...])

        if nk == 1:
            o_ref[...] = product().astype(o_ref.dtype)
            return
        acc_ref = o_ref if acc_in_out else scratch_refs[0]
        k = pl.program_id(2)

        @pl.when(k == 0)
        def _():
            acc_ref[...] = jnp.zeros_like(acc_ref)

        acc_ref[...] += product()

        if not acc_in_out:
            @pl.when(k == nk - 1)
            def _():
                o_ref[...] = acc_ref[...].astype(o_ref.dtype)

    scratch = [pltpu.VMEM((tm, tn), F32)] if use_acc else []
    in_specs = [a_spec, b_spec] + [HBM_SPEC] * n_rin
    out_specs = [o_spec] + [HBM_SPEC] * n_rout
    out_shapes = [out_shape]
    operands = [a, b]
    aliases = {}
    sem = ("parallel", "parallel", "arbitrary")
    if rider is not None:
        scratch += rider.scratch()
        out_shapes += list(rider.out_shapes)
        operands += list(rider.operands)
        aliases = {n_pre + 2 + i: 1 + o for i, o in rider.aliases.items()}
        sem = ("arbitrary", "arbitrary", "arbitrary")
    cp = _params(dimension_semantics=sem)
    if perm is None:
        return pl.pallas_call(body, name=name, grid=grid, in_specs=in_specs, out_specs=out_specs,
                              out_shape=out_shapes, scratch_shapes=scratch, input_output_aliases=aliases,
                              compiler_params=cp)(*operands)
    gs = pltpu.PrefetchScalarGridSpec(num_scalar_prefetch=1, grid=grid, in_specs=in_specs,
                                      out_specs=out_specs, scratch_shapes=scratch)
    return pl.pallas_call(body, name=name, grid_spec=gs, out_shape=out_shapes, input_output_aliases=aliases,
                          compiler_params=cp)(jnp.asarray(perm), *operands)


def _proj_mm(h, w_full, perm, rider=None):
    t, d = h.shape
    n_tiles = len(perm)
    tm = min(t, 1024)
    return _matmul(
        "proj_mm", h, w_full, perm=perm, rider=rider, grid=(t // tm, n_tiles, 1),
        out_shape=jax.ShapeDtypeStruct((t, n_tiles * COL_TILE), F32),
        a_spec=pl.BlockSpec((tm, d), lambda i, j, k, p: (i, 0)),
        b_spec=pl.BlockSpec((d, COL_TILE), lambda i, j, k, p: (0, p[j])),
        o_spec=pl.BlockSpec((tm, COL_TILE), lambda i, j, k, p: (i, j)))


def _proj_gather_mm(h, wi_part, perm, sc):
    t, d = h.shape
    n_tiles = len(perm)
    tm = min(t, 1024)
    n_i = t // tm
    hd = d // 2
    nf = sc // COL_TILE
    rem = sc - nf * COL_TILE
    assert 2 * rem == COL_TILE and n_tiles == N_CHIPS * nf + 2
    n_kinds = nf + 1
    last = nf
    ahead = min(6, nf)
    slots = []
    for q in range(nf):
        slots += [(0, q), (1, q)] + ([(2, q - 1)] if q else [])
    slots.append((2, nf - 1))
    rem_at = nf + len(slots)

    def first_full(chip):
        return (chip * sc + (rem if chip % 2 else 0)) // COL_TILE

    inverse = np.argsort(perm)
    table = np.zeros((N_CHIPS, 2, n_tiles), np.int32)
    for chip in range(N_CHIPS):
        seq = list(range(first_full(chip), first_full(chip) + nf))
        seq += [first_full((chip ^ 2, chip ^ 1, chip ^ 3)[j]) + q for j, q in slots]
        seq += [first_full(chip - chip % 2) + nf, first_full((chip ^ 2) - chip % 2) + nf]
        assert sorted(seq) == list(range(n_tiles)), seq
        table[chip, 0] = inverse[seq]
        table[chip, 1] = seq
    me_chip = 2 * lax.axis_index("x") + lax.axis_index("y")
    tab = lax.dynamic_index_in_dim(jnp.asarray(table), me_chip, 0, keepdims=False)

    def body(tab_ref, h_hbm, wi_in, proj_ref, full, hbuf, bbuf, local_sems, send_sems, recv_sems):
        del wi_in
        jj = pl.program_id(0)
        i = pl.program_id(1)
        x, y, c, chips = _mesh_pos()
        sibling = (x, y, 1 - c)

        def chip_of(j):
            return 2 * chips[j][0] + chips[j][1]

        def piece(chip, half, kind, part=None):
            odd = chip % 2
            if kind == last:
                start, width = chip * sc + (1 - odd) * (nf * COL_TILE), rem
            else:
                start, width = chip * sc + odd * rem + kind * COL_TILE, COL_TILE
            if part is not None:
                width //= 2
                start = start + part * width
            return full.at[pl.ds(half * hd, hd), pl.ds(pl.multiple_of(start, LANES), width)]

        def ici(j, kind):
            mine = piece(2 * x + y, c, kind)
            k = j * n_kinds + kind
            return _remote(mine, mine, send_sems.at[k], recv_sems.at[k], (chips[j][0], chips[j][1], c))

        def landed(j, kind):
            blk = piece(chip_of(j), c, kind)
            k = j * n_kinds + kind
            return _remote(blk, blk, send_sems.at[k], recv_sems.at[k], sibling)

        def hop(p, kind):
            blk = piece(chip_of(p), c, kind, part=p)
            k = (2 + p) * n_kinds + kind
            return _remote(blk, blk, send_sems.at[k], recv_sems.at[k], (chips[1 - p][0], chips[1 - p][1], c))

        def hopped(p, kind):
            blk = piece(chip_of(2), c, kind, part=p)
            k = (2 + p) * n_kinds + kind
            return _remote(blk, blk, send_sems.at[k], recv_sems.at[k], sibling)

        def passed(j, kind, half):
            blk = piece(chip_of(j), half, kind)
            k = (4 + j) * n_kinds + kind
            return _remote(blk, blk, send_sems.at[k], recv_sems.at[k], sibling)

        def fetch(pos, slot):
            col = pl.multiple_of(tab_ref[1, pos] * COL_TILE, LANES)
            return pltpu.make_async_copy(full.at[:, pl.ds(col, COL_TILE)], bbuf.at[slot], local_sems.at[slot])

        def load_h():
            return pltpu.make_async_copy(h_hbm, hbuf, local_sems.at[2])

        def send(kind):
            for j in range(2):
                ici(j, kind).start()

        def relay(j, kind):
            if j == 2:
                hopped(0, kind).wait_recv()
                hopped(1, kind).wait_recv()
            else:
                landed(j, kind).wait_recv()
                hop(j, kind).start()
            passed(j, kind, c).start()
            if j == 0 and kind + ahead < n_kinds:
                send(kind + ahead)

        events = {}
        for s, (j, kind) in enumerate(slots):
            events.setdefault(nf + s - 2, []).append(lambda j=j, kind=kind: relay(j, kind))
            events.setdefault(nf + s - 1, []).append(lambda j=j, kind=kind: passed(j, kind, 1 - c).wait_recv())
        for j in range(3):
            events.setdefault(rem_at - 4 + j, []).append(lambda j=j: relay(j, last))
            events.setdefault(rem_at - 1, []).append(lambda j=j: passed(j, last, 1 - c).wait_recv())

        @pl.when(i == 0)
        def _():
            @pl.when(jj == 0)
            def _():
                load_h().start()
                for kind in range(ahead):
                    send(kind)
                fetch(0, 0).start()
                load_h().wait()

            for pos in sorted(events):
                def run(pos=pos):
                    for fn in events[pos]:
                        fn()
                pl.when(jj == pos)(run)

            @pl.when(jj + 1 < n_tiles)
            def _():
                fetch(jj + 1, (jj + 1) % 2).start()

            fetch(jj, jj % 2).wait()

            @pl.when(jj == n_tiles - 1)
            def _():
                for kind in range(n_kinds):
                    for j in range(2):
                        ici(j, kind).wait_send()
                        hop(j, kind).wait_send()
                    for j in range(3):
                        passed(j, kind, c).wait_send()

        rows = pl.ds(pl.multiple_of(i * tm, tm), tm)
        proj_ref[...] = _dot(hbuf[rows, :], bbuf[jj % 2])

    gs = pltpu.PrefetchScalarGridSpec(
        num_scalar_prefetch=1, grid=(n_tiles, n_i),
        in_specs=[HBM_SPEC, HBM_SPEC],
        out_specs=[pl.BlockSpec((tm, COL_TILE), lambda jj, i, tb: (i, tb[0, jj])), HBM_SPEC],
        scratch_shapes=[pltpu.VMEM((t, d), BF16), pltpu.VMEM((2, d, COL_TILE), BF16), pltpu.SemaphoreType.DMA((3,)),
                        pltpu.SemaphoreType.DMA((7 * n_kinds,)), pltpu.SemaphoreType.DMA((7 * n_kinds,))])
    return pl.pallas_call(
        body, name="proj_gather_mm", grid_spec=gs,
        out_shape=[jax.ShapeDtypeStruct((t, n_tiles * COL_TILE), F32), jax.ShapeDtypeStruct(wi_part.shape, BF16)],
        input_output_aliases={2: 1},
        compiler_params=_params(dimension_semantics=("arbitrary", "arbitrary")),
    )(tab, h, wi_part)


def _gw_in_mm(name, h, dproj, perm, half, rider=None):
    t, d = h.shape
    n_tiles = len(perm)
    hd = d // 2
    tm = min(hd, 1024)
    per_half = hd // tm
    table = jnp.concatenate([jnp.asarray(perm), jnp.reshape(half, (1,)).astype(jnp.int32)])
    return _matmul(
        name, h, dproj, perm=table, rider=rider, ta=True, grid=(per_half, n_tiles, 1),
        out_shape=jax.ShapeDtypeStruct((hd, n_tiles * COL_TILE), BF16),
        a_spec=pl.BlockSpec((t, tm), lambda i, j, k, p: (0, p[n_tiles] * per_half + i)),
        b_spec=pl.BlockSpec((t, COL_TILE), lambda i, j, k, p: (0, j)),
        o_spec=pl.BlockSpec((tm, COL_TILE), lambda i, j, k, p: (i, p[j])))


def _dh_mm(dproj, w_full, perm, rider=None):
    t = dproj.shape[0]
    d = w_full.shape[0]
    n_tiles = len(perm)
    tm = min(t, 2048)
    tn = min(d, 2048)
    return _matmul(
        "dh_mm", dproj, w_full, perm=perm, rider=rider, nt=True, grid=(t // tm, d // tn, n_tiles),
        out_shape=jax.ShapeDtypeStruct((t, d), F32),
        a_spec=pl.BlockSpec((tm, COL_TILE), lambda i, j, k, p: (i, k)),
        b_spec=pl.BlockSpec((tn, COL_TILE), lambda i, j, k, p: (j, p[k])),
        o_spec=pl.BlockSpec((tm, tn), lambda i, j, k, p: (i, j)))


def _out_mm(mixed, w_out_full):
    t, dm = mixed.shape
    d = w_out_full.shape[1]
    tm = min(t, 1024)
    tn = min(d, 256)
    tk = dm
    return _matmul(
        "out_mm", mixed, w_out_full, grid=(t // tm, d // tn, dm // tk),
        out_shape=jax.ShapeDtypeStruct((t, d), F32),
        a_spec=pl.BlockSpec((tm, tk), lambda i, j, k: (i, k)),
        b_spec=pl.BlockSpec((tk, tn), lambda i, j, k: (k, j)),
        o_spec=pl.BlockSpec((tm, tn), lambda i, j, k: (i, j)))[0]


def _dmixed_mm(dy, w_out_full, rider=None):
    t, d = dy.shape
    dm = w_out_full.shape[0]
    tm = min(t, 1024)
    tn = min(dm, 1024)
    return _matmul(
        "dmixed_mm", dy, w_out_full, nt=True, rider=rider, grid=(t // tm, dm // tn, 1),
        out_shape=jax.ShapeDtypeStruct((t, dm), F32),
        a_spec=pl.BlockSpec((tm, d), lambda i, j, k: (i, 0)),
        b_spec=pl.BlockSpec((tn, d), lambda i, j, k: (j, 0)),
        o_spec=pl.BlockSpec((tm, tn), lambda i, j, k: (i, j)))


def _gw_out_mm(mixed, dy):
    t, dm = mixed.shape
    d = dy.shape[1]
    hr = dm // (2 * N_CHIPS)
    tn = min(d, 1024)
    return _matmul(
        "gw_out_mm", mixed, dy, ta=True, grid=(dm // hr, d // tn, 1),
        out_shape=jax.ShapeDtypeStruct((2, N_CHIPS, hr, d), BF16),
        a_spec=pl.BlockSpec((t, hr), lambda i, j, k: (0, i)),
        b_spec=pl.BlockSpec((t, tn), lambda i, j, k: (0, j)),
        o_spec=pl.BlockSpec((None, None, hr, tn), lambda i, j, k: (i % 2, i // 2, 0, j)))[0]


def _lane_half():
    return lax.broadcasted_iota(jnp.int32, (WINDOW, LANES), 1) // ATTN_HEAD_DIM


def _dup_kv(tile, kh):
    return jnp.where(_lane_half() == kh, tile, pltpu.roll(tile, ATTN_HEAD_DIM, 1))


def _stack_heads(tiles, kh):
    half = _lane_half()
    pieces = []
    for g in range(GQA_GROUP):
        pieces.append(jnp.where(half == g % 2, tiles[4 * kh + g // 2], 0.0))
    return jnp.concatenate(pieces, axis=0)


def _unstack_heads(stacked):
    half = _lane_half()
    out = []
    for j in range(GQA_GROUP // 2):
        a = stacked[(2 * j) * WINDOW:(2 * j + 1) * WINDOW]
        b = stacked[(2 * j + 1) * WINDOW:(2 * j + 2) * WINDOW]
        out.append(jnp.where(half == 0, a, b))
    return out


def _attn_probs(qs, kcat, sink_col, n):
    rows = GQA_GROUP * WINDOW
    s = _dot_nt(qs, kcat)
    qi = lax.broadcasted_iota(jnp.int32, (rows, 2 * WINDOW), 0) % WINDOW
    kj = lax.broadcasted_iota(jnp.int32, (rows, 2 * WINDOW), 1)
    first_key = WINDOW * (1 - jnp.minimum(n, 1))
    valid = (kj > qi) & (kj <= qi + WINDOW) & (kj >= first_key)
    s = jnp.where(valid, s, NEG_BIG)
    mx = jnp.maximum(jnp.max(s, axis=-1, keepdims=True), sink_col)
    p = jnp.exp(s - mx)
    p_sink = jnp.exp(sink_col - mx)
    inv = 1.0 / (jnp.sum(p, axis=-1, keepdims=True) + p_sink)
    return p * inv, p_sink * inv


def _attn_operands(sink_ref, q_tiles, k_prev, k_cur, v_prev, v_cur, m, kh):
    qs = _stack_heads([qt * ATTN_SCALE for qt in q_tiles], kh).astype(BF16)
    kcat = jnp.concatenate([_dup_kv(k_prev, kh), _dup_kv(k_cur, kh)], axis=0).astype(BF16)
    vcat = jnp.concatenate([_dup_kv(v_prev, kh), _dup_kv(v_cur, kh)], axis=0).astype(BF16)
    heads_per_group = ATTN_GROUP_LANES // ATTN_HEAD_DIM
    sink_col = jnp.concatenate(
        [jnp.full((WINDOW, 1), sink_ref[0, m * heads_per_group + kh * GQA_GROUP + g], F32)
         for g in range(GQA_GROUP)], axis=0)
    return qs, kcat, vcat, sink_col


def _attn_specs(lay, d):
    a_blk = lay["a_off"] // (2 * ATTN_GROUP_LANES)
    k_blk = lay["k_off"] // LANES
    v_blk = lay["v_off"] // LANES
    before = lambda n: jnp.maximum(ATTN_STEP_BLOCKS * n - 1, 0)
    qg = pl.BlockSpec((ATTN_STEP_ROWS, 2 * ATTN_GROUP_LANES), lambda m, n: (n, a_blk + m))
    kp = pl.BlockSpec((WINDOW, LANES), lambda m, n: (before(n), k_blk + m))
    kc = pl.BlockSpec((ATTN_STEP_ROWS, LANES), lambda m, n: (n, k_blk + m))
    vp = pl.BlockSpec((WINDOW, LANES), lambda m, n: (before(n), v_blk + m))
    vc = pl.BlockSpec((ATTN_STEP_ROWS, LANES), lambda m, n: (n, v_blk + m))
    return qg, kp, kc, vp, vc


def _block_rows(b):
    return slice(b * WINDOW, (b + 1) * WINDOW)


def _kv_tiles(prev_ref, cur_ref, b):
    prev = prev_ref[...] if b == 0 else cur_ref[_block_rows(b - 1), :]
    return prev, cur_ref[_block_rows(b), :]


def _attn_fwd(proj, sinks, lay, d, rider=None):
    t = proj.shape[0]
    n_groups = d // ATTN_GROUP_LANES
    n_blocks = t // ATTN_STEP_ROWS
    pairs = ATTN_GROUP_LANES // LANES
    n_rin = 0 if rider is None else len(rider.operands)
    n_rout = 0 if rider is None else len(rider.out_shapes)

    def body(*refs):
        sink_ref, qg_ref, kp_ref, kc_ref, vp_ref, vc_ref = refs[:6]
        rin = refs[6:6 + n_rin]
        mix_ref, o_ref = refs[6 + n_rin:8 + n_rin]
        rout = refs[8 + n_rin:8 + n_rin + n_rout]
        m = pl.program_id(0)
        n = pl.program_id(1)
        if rider is not None:
            rider.emit(m * n_blocks + n, n_groups * n_blocks, rin, rout, refs[-2], refs[-1])
        for b in range(ATTN_STEP_BLOCKS):
            rows = _block_rows(b)
            k_prev, k_cur = _kv_tiles(kp_ref, kc_ref, b)
            v_prev, v_cur = _kv_tiles(vp_ref, vc_ref, b)
            q_tiles = [qg_ref[rows, p * LANES:(p + 1) * LANES] for p in range(pairs)]
            for kh in range(2):
                qs, kcat, vcat, sink_col = _attn_operands(sink_ref, q_tiles, k_prev, k_cur, v_prev, v_cur, m, kh)
                probs, _ = _attn_probs(qs, kcat, sink_col, ATTN_STEP_BLOCKS * n + b)
                out = _dot(probs.astype(BF16), vcat)
                for j, tile in enumerate(_unstack_heads(out)):
                    p = 4 * kh + j
                    lanes = slice(p * LANES, (p + 1) * LANES)
                    gate = qg_ref[rows, ATTN_GROUP_LANES + p * LANES:ATTN_GROUP_LANES + (p + 1) * LANES]
                    o_ref[rows, lanes] = tile
                    mix_ref[rows, lanes] = (tile * (gate * _sigmoid(gate))).astype(BF16)

    qg, kp, kc, vp, vc = _attn_specs(lay, d)
    out_blk = pl.BlockSpec((ATTN_STEP_ROWS, ATTN_GROUP_LANES), lambda m, n: (n, m))
    out_shapes = [jax.ShapeDtypeStruct((t, 2 * d), BF16), jax.ShapeDtypeStruct((t, d), F32)]
    operands = [sinks, proj, proj, proj, proj, proj]
    scratch, aliases, sem = [], {}, ("parallel", "parallel")
    if rider is not None:
        scratch = rider.scratch()
        out_shapes += list(rider.out_shapes)
        operands += list(rider.operands)
        aliases = {6 + i: 2 + o for i, o in rider.aliases.items()}
        sem = ("arbitrary", "arbitrary")
    return pl.pallas_call(
        body, name="attn_fwd", grid=(n_groups, n_blocks),
        in_specs=[pl.BlockSpec(memory_space=pltpu.SMEM), qg, kp, kc, vp, vc] + [HBM_SPEC] * n_rin,
        out_specs=[out_blk, out_blk] + [HBM_SPEC] * n_rout,
        out_shape=out_shapes, scratch_shapes=scratch, input_output_aliases=aliases,
        compiler_params=_params(dimension_semantics=sem),
    )(*operands)


def _attn_bwd(proj, sinks, attn_o, dmixed, dproj, lay, d):
    t = proj.shape[0]
    n_groups = d // ATTN_GROUP_LANES
    pairs = ATTN_GROUP_LANES // LANES
    kv_w = n_groups * LANES

    def body(sink_ref, qg_ref, kp_ref, kc_ref, vp_ref, vc_ref, o_ref, dm_ref, dproj_hbm,
             dqg_ref, dkc_ref, dkp_ref, dvc_ref, dvp_ref, dsink_ref):
        del dproj_hbm
        m = pl.program_id(0)
        n = pl.program_id(1)
        half = _lane_half()
        sub = lax.broadcasted_iota(jnp.int32, (8, LANES), 0)
        lane = lax.broadcasted_iota(jnp.int32, (8, LANES), 1)
        dsink = jnp.zeros((8, LANES), F32)
        for b in range(ATTN_STEP_BLOCKS):
            rows = _block_rows(b)
            k_prev, k_cur = _kv_tiles(kp_ref, kc_ref, b)
            v_prev, v_cur = _kv_tiles(vp_ref, vc_ref, b)
            q_tiles = [qg_ref[rows, p * LANES:(p + 1) * LANES] for p in range(pairs)]
            do_tiles, o_tiles = [], []
            for p in range(pairs):
                lanes = slice(p * LANES, (p + 1) * LANES)
                gate_lanes = slice(ATTN_GROUP_LANES + p * LANES, ATTN_GROUP_LANES + (p + 1) * LANES)
                gate = qg_ref[rows, gate_lanes]
                sg = _sigmoid(gate)
                dmix = dm_ref[rows, lanes]
                ov = o_ref[rows, lanes]
                dqg_ref[rows, gate_lanes] = (dmix * ov * (sg * (1.0 + gate * (1.0 - sg)))).astype(BF16)
                do_tiles.append(dmix * (gate * sg))
                o_tiles.append(ov)

            dk_cur = dk_prev = dv_cur = dv_prev = jnp.zeros((WINDOW, LANES), F32)
            for kh in range(2):
                qs, kcat, vcat, sink_col = _attn_operands(sink_ref, q_tiles, k_prev, k_cur, v_prev, v_cur, m, kh)
                probs, p_sink = _attn_probs(qs, kcat, sink_col, ATTN_STEP_BLOCKS * n + b)
                dos = _stack_heads(do_tiles, kh)
                delta = jnp.sum(dos * _stack_heads(o_tiles, kh), axis=-1, keepdims=True)
                dos = dos.astype(BF16)
                dp = _dot_nt(dos, vcat)
                ds = (probs * (dp - delta)).astype(BF16)
                dv = _dot_tn(probs.astype(BF16), dos)
                dv = dv + pltpu.roll(dv, ATTN_HEAD_DIM, 1)
                dk = _dot_tn(ds, qs)
                dk = dk + pltpu.roll(dk, ATTN_HEAD_DIM, 1)
                dq = _dot(ds, kcat)
                for j, tile in enumerate(_unstack_heads(dq)):
                    p = 4 * kh + j
                    dqg_ref[rows, p * LANES:(p + 1) * LANES] = (tile * ATTN_SCALE).astype(BF16)
                dk_prev = jnp.where(half == kh, dk[:WINDOW], dk_prev)
                dk_cur = jnp.where(half == kh, dk[WINDOW:], dk_cur)
                dv_prev = jnp.where(half == kh, dv[:WINDOW], dv_prev)
                dv_cur = jnp.where(half == kh, dv[WINDOW:], dv_cur)
                sink_terms = p_sink * delta
                for g in range(GQA_GROUP):
                    val = -jnp.sum(sink_terms[g * WINDOW:(g + 1) * WINDOW])
                    dsink = dsink + jnp.where((sub == 0) & (lane == kh * GQA_GROUP + g), val, 0.0)
            dkc_ref[rows, :] = dk_cur
            dkp_ref[rows, :] = dk_prev
            dvc_ref[rows, :] = dv_cur
            dvp_ref[rows, :] = dv_prev

        @pl.when(n == 0)
        def _():
            dsink_ref[...] = jnp.zeros_like(dsink_ref)

        dsink_ref[...] += dsink

    qg, kp, kc, vp, vc = _attn_specs(lay, d)
    a_blk = lay["a_off"] // (2 * ATTN_GROUP_LANES)
    grp = pl.BlockSpec((ATTN_STEP_ROWS, ATTN_GROUP_LANES), lambda m, n: (n, m))
    kv_blk = pl.BlockSpec((ATTN_STEP_ROWS, LANES), lambda m, n: (n, m))
    kv_shape = jax.ShapeDtypeStruct((t, kv_w), F32)
    outs = pl.pallas_call(
        body, name="attn_bwd", grid=(n_groups, t // ATTN_STEP_ROWS),
        in_specs=[pl.BlockSpec(memory_space=pltpu.SMEM), qg, kp, kc, vp, vc, grp, grp,
                  pl.BlockSpec(memory_space=pl.ANY)],
        out_specs=[pl.BlockSpec((ATTN_STEP_ROWS, 2 * ATTN_GROUP_LANES), lambda m, n: (n, a_blk + m)),
                   kv_blk, kv_blk, kv_blk, kv_blk, pl.BlockSpec((8, LANES), lambda m, n: (m, 0))],
        out_shape=[jax.ShapeDtypeStruct(dproj.shape, BF16), kv_shape, kv_shape, kv_shape, kv_shape,
                   jax.ShapeDtypeStruct((n_groups * 8, LANES), F32)],
        input_output_aliases={8: 0},
        compiler_params=_params(dimension_semantics=("parallel", "arbitrary")),
    )(sinks, proj, proj, proj, proj, proj, attn_o, dmixed, dproj)
    return outs


def _kv_combine(dkc, dkp, dvc, dvp, dproj, lay):
    t, kv_w = dkc.shape
    nb = t // WINDOW
    kv_blk_idx = lay["k_off"] // (2 * kv_w)

    def body(dkc_ref, dkp_ref, dvc_ref, dvp_ref, dproj_hbm, o_ref):
        del dproj_hbm
        keep = (pl.program_id(0) < nb - 1).astype(F32)
        o_ref[:, :kv_w] = (dkc_ref[...] + keep * dkp_ref[...]).astype(BF16)
        o_ref[:, kv_w:] = (dvc_ref[...] + keep * dvp_ref[...]).astype(BF16)

    cur = pl.BlockSpec((WINDOW, kv_w), lambda n: (n, 0))
    nxt = pl.BlockSpec((WINDOW, kv_w), lambda n: (jnp.minimum(n + 1, nb - 1), 0))
    return pl.pallas_call(
        body, name="kv_combine", grid=(nb,),
        in_specs=[cur, nxt, cur, nxt, pl.BlockSpec(memory_space=pl.ANY)],
        out_specs=pl.BlockSpec((WINDOW, 2 * kv_w), lambda n: (n, kv_blk_idx)),
        out_shape=jax.ShapeDtypeStruct(dproj.shape, BF16),
        input_output_aliases={4: 0},
        compiler_params=_params(dimension_semantics=("parallel",)),
    )(dkc, dkp, dvc, dvp, dproj)


def _lower_bound(lbl_ref):
    l0 = lbl_ref[0:1, :]
    l1 = lbl_ref[1:2, :]
    mx = jnp.maximum(l0, l1)
    e0 = jnp.exp(l0 - mx)
    e1 = jnp.exp(l1 - mx)
    return e0 / (e0 + e1)


def _chunk_masks():
    ti = lax.broadcasted_iota(jnp.int32, (CHUNK, CHUNK), 0)
    si = lax.broadcasted_iota(jnp.int32, (CHUNK, CHUNK), 1)
    diag = ((ti // HALF_CHUNK) == (si // HALF_CHUNK)) & (si <= ti)
    off = (ti >= HALF_CHUNK) & (si < HALF_CHUNK)
    lower = (si <= ti).astype(BF16)
    upper = (si >= ti).astype(BF16)
    return diag, off, lower, upper


def _rnn_gates(rq, rf, lb):
    sf = _sigmoid(rf)
    f = lb + (1.0 - lb) * sf
    sq = _sigmoid(rq)
    return sf, f, jnp.log(f), 1.0 - f, sq, rq * sq


def _rnn_decays(g_cum):
    row = lax.broadcasted_iota(jnp.int32, g_cum.shape, 0)
    ref_d = jnp.where(row < HALF_CHUNK, g_cum[HALF_CHUNK // 2 - 1:HALF_CHUNK // 2],
                      g_cum[HALF_CHUNK + HALF_CHUNK // 2 - 1:HALF_CHUNK + HALF_CHUNK // 2])
    ref_o = g_cum[HALF_CHUNK - 1:HALF_CHUNK]
    last = g_cum[CHUNK - 1:CHUNK]
    return dict(eq_d=jnp.exp(g_cum - ref_d), ek_d=jnp.exp(ref_d - g_cum),
                eq_o=jnp.exp(jnp.minimum(g_cum - ref_o, 0.0)), ek_o=jnp.exp(jnp.minimum(ref_o - g_cum, 0.0)),
                eg=jnp.exp(g_cum), ekl=jnp.exp(last - g_cum), e_last=jnp.exp(last))


def _head(a, j):
    return a[:, j * RNN_HEAD_DIM:(j + 1) * RNN_HEAD_DIM]


def _rnn_specs(t, tb, d):
    gw = RNN_GROUP_HEADS * RNN_HEAD_DIM
    return gw, t // tb, tb // CHUNK


def _rnn_fwd(proj, lb_logits, rnn_gain, mixed, d, rider=None):
    t = proj.shape[0]
    tb = min(t, RNN_STEP_ROWS)
    gw, ntb, nch = _rnn_specs(t, tb, d)
    n_groups = d // gw
    n_heads = d // RNN_HEAD_DIM
    n_rin = 0 if rider is None else len(rider.operands)
    n_rout = 0 if rider is None else len(rider.out_shapes)

    def body(*refs):
        blk_ref, lbl_ref, gain_ref = refs[:3]
        rin = refs[4:4 + n_rin]
        mix_ref, o_ref, st_out_ref = refs[4 + n_rin:7 + n_rin]
        rout = refs[7 + n_rin:7 + n_rin + n_rout]
        st_ref = refs[7 + n_rin + n_rout]
        if rider is not None:
            rider.emit(pl.program_id(0) * ntb + pl.program_id(1), n_groups * ntb, rin, rout, refs[-2], refs[-1])

        @pl.when(pl.program_id(1) == 0)
        def _():
            st_ref[...] = jnp.zeros_like(st_ref)

        lb = _lower_bound(lbl_ref)
        gain = gain_ref[...]
        diag, off, lower, _ = _chunk_masks()

        def chunk(c, carry):
            rows = pl.ds(pl.multiple_of(c * CHUNK, CHUNK), CHUNK)
            rq = blk_ref[rows, 0:gw]
            rf = blk_ref[rows, gw:2 * gw]
            v = blk_ref[rows, 2 * gw:3 * gw]
            rg = blk_ref[rows, 3 * gw:4 * gw]
            _, _, g, k, _, q = _rnn_gates(rq, rf, lb)
            dec = _rnn_decays(_tri_dot(lower, g))
            qd = (q * dec["eq_d"]).astype(BF16)
            kd = (k * dec["ek_d"]).astype(BF16)
            qo = (q * dec["eq_o"]).astype(BF16)
            ko = (k * dec["ek_o"]).astype(BF16)
            qe = (q * dec["eg"]).astype(BF16)
            kl = (k * dec["ekl"]).astype(BF16)
            vb = v.astype(BF16)
            outs = []
            for j in range(RNN_GROUP_HEADS):
                st = st_ref[j]
                st_out_ref[j, c] = st
                attn = jnp.where(diag, _dot_nt(_head(qd, j), _head(kd, j)),
                                 jnp.where(off, _dot_nt(_head(qo, j), _head(ko, j)), 0.0))
                o = _dot(attn.astype(BF16), _head(vb, j)) + _dot_nt(_head(qe, j), st.astype(BF16))
                st_ref[j] = st * _head(dec["e_last"], j) + _dot_tn(_head(vb, j), _head(kl, j))
                rr = lax.rsqrt(jnp.mean(o * o, axis=-1, keepdims=True) + NORM_EPS)
                o_ref[rows, j * RNN_HEAD_DIM:(j + 1) * RNN_HEAD_DIM] = o
                outs.append(o * rr)
            on = jnp.concatenate(outs, axis=1) * gain
            mix_ref[rows, :] = (on * (rg * _sigmoid(rg))).astype(BF16)
            return carry

        lax.fori_loop(0, nch, chunk, 0, unroll=True)

    out_shapes = [jax.ShapeDtypeStruct(mixed.shape, BF16), jax.ShapeDtypeStruct((t, d), F32),
                  jax.ShapeDtypeStruct((n_heads, t // CHUNK, RNN_HEAD_DIM, RNN_HEAD_DIM), F32)]
    operands = [proj, lb_logits, rnn_gain, mixed]
    scratch = [pltpu.VMEM((RNN_GROUP_HEADS, RNN_HEAD_DIM, RNN_HEAD_DIM), F32)]
    aliases, sem = {3: 0}, ("parallel", "arbitrary")
    if rider is not None:
        scratch += rider.scratch()
        out_shapes += list(rider.out_shapes)
        operands += list(rider.operands)
        aliases.update({4 + i: 3 + o for i, o in rider.aliases.items()})
        sem = ("arbitrary", "arbitrary")
    return pl.pallas_call(
        body, name="rnn_fwd", grid=(n_groups, ntb),
        in_specs=[pl.BlockSpec((tb, 4 * gw), lambda h, i: (i, h)),
                  pl.BlockSpec((2, gw), lambda h, i: (0, h)),
                  pl.BlockSpec((1, gw), lambda h, i: (0, h)),
                  pl.BlockSpec(memory_space=pl.ANY)] + [HBM_SPEC] * n_rin,
        out_specs=[pl.BlockSpec((tb, gw), lambda h, i: (i, d // gw + h)),
                   pl.BlockSpec((tb, gw), lambda h, i: (i, h)),
                   pl.BlockSpec((RNN_GROUP_HEADS, nch, RNN_HEAD_DIM, RNN_HEAD_DIM), lambda h, i: (h, i, 0, 0))]
        + [HBM_SPEC] * n_rout,
        out_shape=out_shapes, scratch_shapes=scratch, input_output_aliases=aliases,
        compiler_params=_params(dimension_semantics=sem),
    )(*operands)


def _rnn_bwd(proj, lb_logits, rnn_gain, o_pre, states, dmixed, d_total, d, rider=None):
    t = proj.shape[0]
    tb = min(t, RNN_STEP_ROWS)
    gw, ntb, nch = _rnn_specs(t, tb, d)
    n_groups = d // gw
    n_rin = 0 if rider is None else len(rider.operands)
    n_rout = 0 if rider is None else len(rider.out_shapes)

    def body(*refs):
        blk_ref, lbl_ref, gain_ref, o_ref, st_in_ref, dm_ref = refs[:6]
        rin = refs[6:6 + n_rin]
        dproj_ref, dgain_ref, dlb_ref = refs[6 + n_rin:9 + n_rin]
        rout = refs[9 + n_rin:9 + n_rin + n_rout]
        dst_ref = refs[9 + n_rin + n_rout]
        if rider is not None:
            rider.emit(pl.program_id(0) * ntb + pl.program_id(1), n_groups * ntb, rin, rout, refs[-2], refs[-1])

        @pl.when(pl.program_id(1) == 0)
        def _():
            dst_ref[...] = jnp.zeros_like(dst_ref)
            dgain_ref[...] = jnp.zeros_like(dgain_ref)
            dlb_ref[...] = jnp.zeros_like(dlb_ref)

        lb = _lower_bound(lbl_ref)
        gain = gain_ref[...]
        diag, off, lower, upper = _chunk_masks()
        last_row = lax.broadcasted_iota(jnp.int32, (CHUNK, RNN_HEAD_DIM), 0) == CHUNK - 1

        def chunk(step, carry):
            c = nch - 1 - step
            rows = pl.ds(pl.multiple_of(c * CHUNK, CHUNK), CHUNK)
            rq = blk_ref[rows, 0:gw]
            rf = blk_ref[rows, gw:2 * gw]
            v = blk_ref[rows, 2 * gw:3 * gw]
            rg = blk_ref[rows, 3 * gw:4 * gw]
            sf, f, g, k, sq, q = _rnn_gates(rq, rf, lb)
            dec = _rnn_decays(_tri_dot(lower, g))
            qd = (q * dec["eq_d"]).astype(BF16)
            kd = (k * dec["ek_d"]).astype(BF16)
            qo = (q * dec["eq_o"]).astype(BF16)
            ko = (k * dec["ek_o"]).astype(BF16)
            qe = (q * dec["eg"]).astype(BF16)
            kl = (k * dec["ekl"]).astype(BF16)
            vb = v.astype(BF16)

            o = o_ref[rows, :]
            dmix = dm_ref[rows, :]
            sg = _sigmoid(rg)
            n_parts = []
            for j in range(RNN_GROUP_HEADS):
                oj = _head(o, j)
                n_parts.append(oj * lax.rsqrt(jnp.mean(oj * oj, axis=-1, keepdims=True) + NORM_EPS))
            nrm = jnp.concatenate(n_parts, axis=1)
            d_on = dmix * (rg * sg)
            d_rg = dmix * (nrm * gain) * (sg * (1.0 + rg * (1.0 - sg)))
            dgain_ref[...] += jnp.sum(d_on * nrm, axis=0, keepdims=True)
            dn = d_on * gain

            dq_parts, dk_parts, dv_parts, dg_parts = [], [], [], []
            for j in range(RNN_GROUP_HEADS):
                oj, nj, dnj = _head(o, j), _head(nrm, j), _head(dn, j)
                rr = lax.rsqrt(jnp.mean(oj * oj, axis=-1, keepdims=True) + NORM_EPS)
                do = (rr * (dnj - nj * jnp.mean(dnj * nj, axis=-1, keepdims=True))).astype(BF16)
                st = st_in_ref[j, c]
                dst = dst_ref[j]
                stb, dstb = st.astype(BF16), dst.astype(BF16)
                qdj, kdj, qoj, koj = _head(qd, j), _head(kd, j), _head(qo, j), _head(ko, j)
                attn = jnp.where(diag, _dot_nt(qdj, kdj), jnp.where(off, _dot_nt(qoj, koj), 0.0))
                dattn = _dot_nt(do, _head(vb, j))
                da_d = jnp.where(diag, dattn, 0.0).astype(BF16)
                da_o = jnp.where(off, dattn, 0.0).astype(BF16)
                dv = _dot_tn(attn.astype(BF16), do) + _dot_nt(_head(kl, j), dstb)
                dq_inter = _dot(do, stb) * _head(dec["eg"], j)
                dq_d, dq_o = _dot(da_d, kdj), _dot(da_o, koj)
                dq = dq_inter + dq_d * _head(dec["eq_d"], j) + dq_o * _head(dec["eq_o"], j)
                dk_inter = _dot(_head(vb, j), dstb) * _head(dec["ekl"], j)
                dk_d, dk_o = _dot_tn(da_d, qdj), _dot_tn(da_o, qoj)
                dk = dk_inter + dk_d * _head(dec["ek_d"], j) + dk_o * _head(dec["ek_o"], j)
                kj, qj = _head(k, j), _head(q, j)
                e_last = _head(dec["e_last"], j)
                extra = (jnp.sum(kj * dk_inter, axis=0, keepdims=True)
                         + e_last * jnp.sum(st * dst, axis=0, keepdims=True))
                dg_cum = (qj * dq_inter - kj * dk_inter
                          + (qdj.astype(F32) * dq_d + qoj.astype(F32) * dq_o)
                          - (kdj.astype(F32) * dk_d + koj.astype(F32) * dk_o))
                dg_parts.append(jnp.where(last_row, dg_cum + extra, dg_cum))
                dst_ref[j] = dst * e_last + _dot_tn(do, _head(qe, j))
                dq_parts.append(dq)
                dk_parts.append(dk)
                dv_parts.append(dv)

            dq = jnp.concatenate(dq_parts, axis=1)
            dk = jnp.concatenate(dk_parts, axis=1)
            dg = _tri_dot(upper, jnp.concatenate(dg_parts, axis=1))
            df = dg / f - dk
            dlb_ref[...] += jnp.sum(df * (1.0 - sf), axis=0, keepdims=True)
            d_rf = df * (1.0 - lb) * (sf * (1.0 - sf))
            d_rq = dq * (sq * (1.0 + rq * (1.0 - sq)))
            dproj_ref[rows, 0:gw] = d_rq.astype(BF16)
            dproj_ref[rows, gw:2 * gw] = d_rf.astype(BF16)
            dproj_ref[rows, 2 * gw:3 * gw] = jnp.concatenate(dv_parts, axis=1).astype(BF16)
            dproj_ref[rows, 3 * gw:4 * gw] = d_rg.astype(BF16)
            return carry

        lax.fori_loop(0, nch, chunk, 0, unroll=True)

    rev = lambda i: ntb - 1 - i
    vec = pl.BlockSpec((1, gw), lambda h, i: (0, h))
    scratch = [pltpu.VMEM((RNN_GROUP_HEADS, RNN_HEAD_DIM, RNN_HEAD_DIM), F32)]
    out_shapes = [jax.ShapeDtypeStruct((t, d_total), BF16), jax.ShapeDtypeStruct((1, d), F32),
                  jax.ShapeDtypeStruct((1, d), F32)]
    operands = [proj, lb_logits, rnn_gain, o_pre, states, dmixed]
    sem = ("parallel", "arbitrary")
    if rider is not None:
        scratch += rider.scratch()
        out_shapes += list(rider.out_shapes)
        operands += list(rider.operands)
        sem = ("arbitrary", "arbitrary")
    return pl.pallas_call(
        body, name="rnn_bwd", grid=(n_groups, ntb),
        in_specs=[pl.BlockSpec((tb, 4 * gw), lambda h, i: (rev(i), h)),
                  pl.BlockSpec((2, gw), lambda h, i: (0, h)), vec,
                  pl.BlockSpec((tb, gw), lambda h, i: (rev(i), h)),
                  pl.BlockSpec((RNN_GROUP_HEADS, nch, RNN_HEAD_DIM, RNN_HEAD_DIM), lambda h, i: (h, rev(i), 0, 0)),
                  pl.BlockSpec((tb, gw), lambda h, i: (rev(i), d // gw + h))] + [HBM_SPEC] * n_rin,
        out_specs=[pl.BlockSpec((tb, 4 * gw), lambda h, i: (rev(i), h)), vec, vec] + [HBM_SPEC] * n_rout,
        out_shape=out_shapes, scratch_shapes=scratch,
        compiler_params=_params(dimension_semantics=sem),
    )(*operands)


def _local_grads(x, target, w_in_full, w_out, sinks, lb_logits, rnn_gain, pre_gain, post_gain, sc=None):
    t, d = x.shape
    comm = sc is not None
    lay = _layout(d)
    perm = lay["perm"]
    h = _prenorm_fwd(x, pre_gain)
    if comm:
        proj, w_in_full = _proj_gather_mm(h, w_in_full, perm, sc)
        mixed, attn_o, w_out = _attn_fwd(proj, sinks, lay, d, _gather_rider(w_out, 0.7, 0, 2))
        mixed, o_pre, states, w_out_full = _rnn_fwd(proj, lb_logits, rnn_gain, mixed, d,
                                                    _gather_rider(w_out, 0.7, 1, 2))
    else:
        (proj,) = _proj_mm(h, w_in_full, perm)
        w_out_full = w_out
        mixed, attn_o = _attn_fwd(proj, sinks, lay, d)
        mixed, o_pre, states = _rnn_fwd(proj, lb_logits, rnn_gain, mixed, d)
    y = _out_mm(mixed, w_out_full)
    dy, dz, g_post, loss = _post_loss(x, y, target, post_gain)
    gw_out = _gw_out_mm(mixed, dy)
    rider = None
    if comm:
        dmixed, recv_out = _dmixed_mm(dy, w_out_full, _pair_exchange_rider(gw_out, stacked=True))
        p_out = _pair_sum_out(gw_out, recv_out)
        rider = _chip_exchange_rider(p_out, lambda ref, chip: ref.at[chip])
    else:
        (dmixed,) = _dmixed_mm(dy, w_out_full)
    dproj, g_rnn, g_lb, *r_out = _rnn_bwd(proj, lb_logits, rnn_gain, o_pre, states, dmixed, lay["total"], d, rider)
    dproj, dkc, dkp, dvc, dvp, dsink = _attn_bwd(proj, sinks, attn_o, dmixed, dproj, lay, d)
    dproj = _kv_combine(dkc, dkp, dvc, dvp, dproj, lay)
    if comm:
        c = lax.axis_index("c")
        (g_other,) = _gw_in_mm("gw_in_mm_other", h, dproj, perm, 1 - c)
        g_mine, recv_in = _gw_in_mm("gw_in_mm_mine", h, dproj, perm, c, _pair_exchange_rider(g_other))
        p_in = _pair_sum_in(g_mine, recv_in, sc)
        rider = _chip_exchange_rider(
            p_in, lambda ref, chip: ref.at[:, pl.ds(pl.multiple_of(chip * sc, LANES), sc)])
    dh, *r_in = _dh_mm(dproj, w_in_full, perm, rider)
    if comm:
        rider = _share_rider(_chip_sum_in(p_in, r_in[0], sc), _chip_sum_out(p_out, r_out[0]))
        grad_x, g_pre, gw_in, gw_out = _prenorm_bwd(x, dh, dz, pre_gain, rider)
    else:
        grad_x, g_pre = _prenorm_bwd(x, dh, dz, pre_gain)
    heads_per_group = ATTN_GROUP_LANES // ATTN_HEAD_DIM
    g_sink = dsink.reshape(d // ATTN_GROUP_LANES, 8, LANES)[:, 0, :heads_per_group].reshape(1, -1)
    small = dict(sink=g_sink, lb=g_lb, rnn=g_rnn, pre=g_pre, post=g_post)
    if comm:
        return loss, grad_x, gw_in, gw_out, small
    gw_in = jnp.stack([_gw_in_mm("gw_in_mm_%d" % half, h, dproj, perm, half)[0] for half in range(2)])
    return loss, grad_x, gw_in, gw_out, small


def _mesh_pos():
    x, y, c = lax.axis_index("x"), lax.axis_index("y"), lax.axis_index("c")
    chips = [(1 - x, y), (x, 1 - y), (1 - x, 1 - y)]
    return x, y, c, chips


def _remote(src, dst, send_sem, recv_sem, device):
    return pltpu.make_async_remote_copy(src_ref=src, dst_ref=dst, send_sem=send_sem, recv_sem=recv_sem,
                                        device_id=device, device_id_type=MESH)


HBM_SPEC = pl.BlockSpec(memory_space=pl.ANY)


def _gather_rider(part, forward_at, section, n_sections):
    rows = part.shape[0] // N_CHIPS
    half_rows = rows // 2
    sec_rows = half_rows // n_sections

    def stages(ins, outs, send_sems, recv_sems):
        del ins
        full = outs[0]

        def piece(chip, half):
            start = chip * rows + half * half_rows + section * sec_rows
            return full.at[pl.ds(pl.multiple_of(start, 8), sec_rows), :]

        def sends():
            x, y, c, chips = _mesh_pos()
            mine = piece(2 * x + y, c)
            return [_remote(mine, mine, send_sems.at[j], recv_sems.at[j], (px, py, c))
                    for j, (px, py) in enumerate(chips)]

        def forwards(half_of):
            x, y, c, chips = _mesh_pos()
            out = []
            for j, (px, py) in enumerate(chips):
                block = piece(2 * px + py, half_of(c))
                out.append(_remote(block, block, send_sems.at[3 + j], recv_sems.at[3 + j], (x, y, 1 - c)))
            return out

        def start():
            for cp in sends():
                cp.start()

        def forward():
            x, y, c, chips = _mesh_pos()
            for j, (px, py) in enumerate(chips):
                landed = piece(2 * px + py, c)
                _remote(landed, landed, send_sems.at[j], recv_sems.at[j], (x, y, 1 - c)).wait_recv()
            for cp in forwards(lambda c: c):
                cp.start()

        def finish():
            for cp in forwards(lambda c: 1 - c):
                cp.wait_recv()
            for cp in sends() + forwards(lambda c: c):
                cp.wait_send()

        return [(0.0, start), (forward_at, forward), (1.0, finish)]

    return _Rider((part,), (jax.ShapeDtypeStruct(part.shape, BF16),), {0: 0}, 6, stages)


def _chip_exchange_rider(partial, piece):
    if partial.ndim == 3:
        recv_shape = (N_CHIPS - 1,) + partial.shape[1:]
    else:
        recv_shape = (N_CHIPS - 1, partial.shape[0], partial.shape[1] // N_CHIPS)

    def stages(ins, outs, send_sems, recv_sems):
        def copies():
            x, y, c, chips = _mesh_pos()
            return [_remote(piece(ins[0], 2 * px + py), outs[0].at[j], send_sems.at[j], recv_sems.at[j], (px, py, c))
                    for j, (px, py) in enumerate(chips)]

        def start():
            for cp in copies():
                cp.start()

        def finish():
            for cp in copies():
                cp.wait()

        return [(0.0, start), (1.0, finish)]

    return _Rider((partial,), (jax.ShapeDtypeStruct(recv_shape, BF16),), {}, N_CHIPS - 1, stages)


def _pair_exchange_rider(g, stacked=False):
    shape = g.shape[1:] if stacked else g.shape

    def stages(ins, outs, send_sems, recv_sems):
        def copy():
            x, y, c, _ = _mesh_pos()
            src = ins[0].at[1 - c] if stacked else ins[0]
            return _remote(src, outs[0], send_sems.at[0], recv_sems.at[0], (x, y, 1 - c))

        return [(0.0, lambda: copy().start()), (1.0, lambda: copy().wait())]

    return _Rider((g,), (jax.ShapeDtypeStruct(shape, g.dtype),), {}, 1, stages)


def _pair_sum_in(mine, recv, sc):
    hd, d_in = mine.shape
    tr = min(hd, 256)

    def body(a_ref, b_ref, o_ref):
        o_ref[...] = (a_ref[...].astype(F32) + b_ref[...].astype(F32)).astype(BF16)

    blk = pl.BlockSpec((tr, sc), lambda i, j: (i, j))
    return pl.pallas_call(
        body, name="pair_sum_in", grid=(hd // tr, d_in // sc), in_specs=[blk, blk], out_specs=blk,
        out_shape=jax.ShapeDtypeStruct((hd, d_in), BF16),
        compiler_params=_params(dimension_semantics=("parallel", "parallel")),
    )(mine, recv)


def _pair_sum_out(gw_out, recv):
    _, n_chips, hr, d = gw_out.shape
    tr = min(hr, 256)
    c = lax.axis_index("c")

    def body(c_ref, a_ref, b_ref, o_ref):
        del c_ref
        o_ref[...] = (a_ref[...].astype(F32) + b_ref[...].astype(F32)).astype(BF16)

    blk = pl.BlockSpec((None, tr, d), lambda k, i, cc: (k, i, 0))
    gs = pltpu.PrefetchScalarGridSpec(
        num_scalar_prefetch=1, grid=(n_chips, hr // tr),
        in_specs=[pl.BlockSpec((None, None, tr, d), lambda k, i, cc: (cc[0], k, i, 0)), blk], out_specs=blk)
    return pl.pallas_call(
        body, name="pair_sum_out", grid_spec=gs, out_shape=jax.ShapeDtypeStruct((n_chips, hr, d), BF16),
        compiler_params=_params(dimension_semantics=("parallel", "parallel")),
    )(jnp.reshape(c, (1,)).astype(jnp.int32), gw_out, recv)


def _place():
    return jnp.stack([2 * lax.axis_index("x") + lax.axis_index("y"), lax.axis_index("c")]).astype(jnp.int32)


def _chip_sum_in(p_in, r_in, sc):
    hd = p_in.shape[0]
    tr = min(hd, 256)
    nblk = hd // tr

    def body(pos_ref, p_ref, r_ref, o_ref):
        del pos_ref
        acc = p_ref[...].astype(F32)
        for j in range(3):
            acc = acc + r_ref[j].astype(F32)
        o_ref[...] = acc

    gs = pltpu.PrefetchScalarGridSpec(
        num_scalar_prefetch=1, grid=(nblk,),
        in_specs=[pl.BlockSpec((tr, sc), lambda i, pos: (i, pos[0])), pl.BlockSpec((3, tr, sc), lambda i, pos: (0, i, 0))],
        out_specs=pl.BlockSpec((tr, sc), lambda i, pos: (pos[1] * nblk + i, 0)))
    return pl.pallas_call(
        body, name="chip_sum_in", grid_spec=gs, out_shape=jax.ShapeDtypeStruct((2 * hd, sc), F32),
        compiler_params=_params(dimension_semantics=("parallel",)),
    )(_place(), p_in, r_in)


def _chip_sum_out(p_out, r_out):
    _, hr, d = p_out.shape
    tr = min(hr, 256)
    nblk = hr // tr

    def body(pos_ref, p_ref, r_ref, o_ref):
        del pos_ref
        acc = p_ref[...].astype(F32)
        for j in range(3):
            acc = acc + r_ref[j].astype(F32)
        o_ref[...] = acc

    gs = pltpu.PrefetchScalarGridSpec(
        num_scalar_prefetch=1, grid=(nblk,),
        in_specs=[pl.BlockSpec((None, tr, d), lambda i, pos: (pos[0], i, 0)), pl.BlockSpec((3, tr, d), lambda i, pos: (0, i, 0))],
        out_specs=pl.BlockSpec((tr, d), lambda i, pos: (pos[1] * nblk + i, 0)))
    return pl.pallas_call(
        body, name="chip_sum_out", grid_spec=gs, out_shape=jax.ShapeDtypeStruct((2 * hr, d), F32),
        compiler_params=_params(dimension_semantics=("parallel",)),
    )(_place(), p_out, r_out)


def _adamw_math(w, g, m, v):
    m_new = ADAM_B1 * m + (1.0 - ADAM_B1) * g
    v_new = ADAM_B2 * v + (1.0 - ADAM_B2) * (g * g)
    m_hat = m_new / (1.0 - ADAM_B1 ** ADAM_STEP)
    v_hat = v_new / (1.0 - ADAM_B2 ** ADAM_STEP)
    delta = -ADAM_LR * (m_hat / (jnp.sqrt(v_hat) + ADAM_EPS) + ADAM_WD * w)
    return delta, m_new, v_new


def _share_rider(g_in, g_out):
    def stages(ins, outs, send_sems, recv_sems):
        del ins

        def copies(half_of):
            x, y, c, _ = _mesh_pos()
            out = []
            for k, ref in enumerate(outs):
                rows = ref.shape[0] // 2
                blk = ref.at[pl.ds(pl.multiple_of(half_of(c) * rows, 8), rows), :]
                out.append(_remote(blk, blk, send_sems.at[k], recv_sems.at[k], (x, y, 1 - c)))
            return out

        def start():
            for cp in copies(lambda c: c):
                cp.start()

        def finish():
            for cp in copies(lambda c: c):
                cp.wait_send()
            for cp in copies(lambda c: 1 - c):
                cp.wait_recv()

        return [(0.0, start), (1.0, finish)]

    shapes = (jax.ShapeDtypeStruct(g_in.shape, F32), jax.ShapeDtypeStruct(g_out.shape, F32))
    return _Rider((g_in, g_out), shapes, {0: 0, 1: 1}, 2, stages)


def _adamw(w, g, m, v, name):
    rows, cols = w.shape
    streams = 8
    fit = (VMEM_LIMIT_BYTES // 2) // (streams * 2 * cols * 4)
    tr = min(rows, 1 << (fit.bit_length() - 1))

    def body(w_ref, g_ref, m_ref, v_ref, d_ref, mo_ref, vo_ref, go_ref):
        gv = g_ref[...]
        delta, m_new, v_new = _adamw_math(w_ref[...], gv, m_ref[...], v_ref[...])
        d_ref[...] = delta
        mo_ref[...] = m_new
        vo_ref[...] = v_new
        go_ref[...] = gv

    spec = pl.BlockSpec((tr, cols), lambda i: (i, 0))
    shape = jax.ShapeDtypeStruct((rows, cols), F32)
    return pl.pallas_call(
        body, name=name, grid=(rows // tr,), in_specs=[spec] * 4, out_specs=[spec] * 4,
        out_shape=[shape] * 4, compiler_params=_params(dimension_semantics=("parallel",)),
    )(w, g, m, v)


SMALL_ROWS = 8


def _small_allreduce_adamw(part, w_pack, m_pack, v_pack):
    d = part.shape[1]

    def body(part_ref, w_ref, m_ref, v_ref, g_ref, d_ref, mo_ref, vo_ref, buf_ref, send_sems, recv_sems):
        x, y, c, _ = _mesh_pos()
        me = 4 * x + 2 * y + c
        buf_ref[0] = part_ref[...]
        copies = []
        for r in range(1, 8):
            rx, ry, rc = (r >> 2) & 1, (r >> 1) & 1, r & 1
            peer = (x ^ rx, y ^ ry, c ^ rc)
            copies.append(_remote(buf_ref.at[0], buf_ref.at[r], send_sems.at[r - 1], recv_sems.at[r - 1], peer))
        for cp in copies:
            cp.start()
        for cp in copies:
            cp.wait()
        total = buf_ref[me]
        for s in range(1, 8):
            total = total + buf_ref[s ^ me]
        w = w_ref[...]
        row = lax.broadcasted_iota(jnp.int32, (SMALL_ROWS, d), 0)
        l0, l1 = w[3:4], w[4:5]
        mx = jnp.maximum(l0, l1)
        e0, e1 = jnp.exp(l0 - mx), jnp.exp(l1 - mx)
        lb = e0 / (e0 + e1)
        g_l0 = total[3:4] * lb * (1.0 - lb)
        grads = jnp.where(row == 3, g_l0, jnp.where(row == 4, -g_l0, total))
        g_ref[...] = grads
        delta, m_new, v_new = _adamw_math(w, grads, m_ref[...], v_ref[...])
        d_ref[...] = delta
        mo_ref[...] = m_new
        vo_ref[...] = v_new

    vm = pl.BlockSpec(memory_space=pltpu.VMEM)
    shape = jax.ShapeDtypeStruct((SMALL_ROWS, d), F32)
    return pl.pallas_call(
        body, name="small_allreduce_adamw",
        in_specs=[vm] * 4, out_specs=[vm] * 4, out_shape=[shape] * 4,
        scratch_shapes=[pltpu.VMEM((8, SMALL_ROWS, d), F32), pltpu.SemaphoreType.DMA((7,)), pltpu.SemaphoreType.DMA((7,))],
    )(part, w_pack, m_pack, v_pack)


def _pack_small(d, pre, post, rnn, lb, sink, extra=None):
    rows = [pre, post, rnn, lb[0:1], lb[1:2],
            jnp.pad(sink, ((0, 0), (0, d - sink.shape[1]))),
            jnp.zeros((1, d), F32) if extra is None else extra,
            jnp.zeros((1, d), F32)]
    return jnp.concatenate(rows, axis=0)


def _unpack_small(p, n_sink):
    return dict(pre=p[0:1], post=p[1:2], rnn=p[2:3], lb=p[3:5], sink=p[5:6, :n_sink])


def kernel(x, w_in, attn_sinks, lb_logits, rnn_norm, w_out, pre_norm, post_norm, loss_target, m_w_in, m_attn_sinks, m_lb_logits, m_rnn_norm, m_w_out, m_pre_norm, m_post_norm, v_w_in, v_attn_sinks, v_lb_logits, v_rnn_norm, v_w_out, v_pre_norm, v_post_norm):
    t, d = x.shape[1], x.shape[2]
    sc = w_in.shape[2]
    n_sink = attn_sinks.shape[1]
    w_in2, w_out2 = w_in[0], w_out[0]

    w_in_part = _cast_into_gathered(w_in2, "cast_w_in", 1)
    w_out_part = _cast_into_gathered(w_out2, "cast_w_out", 0)
    loss_part, grad_x, g_w_in, g_w_out, small = _local_grads(
        x[0], loss_target[0], w_in_part, w_out_part, attn_sinks, lb_logits, rnn_norm, pre_norm, post_norm, sc)
    d_w_in, nm_w_in, nv_w_in, g_w_in = _adamw(w_in2, g_w_in, m_w_in[0], v_w_in[0], "adamw_w_in")
    d_w_out, nm_w_out, nv_w_out, g_w_out = _adamw(w_out2, g_w_out, m_w_out[0], v_w_out[0], "adamw_w_out")

    lb_part = jnp.concatenate([small["lb"], jnp.zeros_like(small["lb"])], axis=0)
    loss_row = jnp.pad(loss_part[:, :1], ((0, 0), (0, d - 1)))
    part = _pack_small(d, small["pre"], small["post"], small["rnn"], lb_part, small["sink"], loss_row)
    w_pack = _pack_small(d, pre_norm, post_norm, rnn_norm, lb_logits, attn_sinks)
    m_pack = _pack_small(d, m_pre_norm, m_post_norm, m_rnn_norm, m_lb_logits, m_attn_sinks)
    v_pack = _pack_small(d, v_pre_norm, v_post_norm, v_rnn_norm, v_lb_logits, v_attn_sinks)
    g_pack, d_pack, nm_pack, nv_pack = _small_allreduce_adamw(part, w_pack, m_pack, v_pack)
    loss = g_pack[6, 0]
    g, dl, nm, nv = (_unpack_small(p, n_sink) for p in (g_pack, d_pack, nm_pack, nv_pack))

    def ordered(w_in_leaf, w_out_leaf, s):
        return (w_in_leaf[None], s["sink"], s["lb"], s["rnn"], w_out_leaf[None], s["pre"], s["post"])

    return (loss, grad_x[None],
            *ordered(g_w_in, g_w_out, g), *ordered(d_w_in, d_w_out, dl),
            *ordered(nm_w_in, nm_w_out, nm), *ordered(nv_w_in, nv_w_out, nv))
```
